```python
import jax, jax.numpy as jnp
from jax import lax
import numpy as np

D_MODEL = 1024
BATCH = 8
SEQ = 16384
DEPTH = 1

GRID_W = 64
CTX_LEN = 256
MLSTM_HEADS = 4
MLSTM_DIM = 1024
MLSTM_HEAD_DIM = MLSTM_DIM // MLSTM_HEADS
MLSTM_CHUNK = 64
QK_CONV = 3
SGU_GROUPS = 4
SGU_DIM = 1024
SGU_GROUP_DIM = SGU_DIM // SGU_GROUPS
SGU_CHUNK = 128
D_FF = 2816
FFN_CONV = 3
N_BRANCH = 2
N_MOD = 6
N_GATES = 4 * MLSTM_HEADS
N_SCAN_IN = 3 * MLSTM_DIM + N_GATES
N_IN = N_SCAN_IN + MLSTM_DIM + 2 * SGU_DIM + N_BRANCH * D_MODEL
SCAN_SPLITS = (MLSTM_DIM, 2 * MLSTM_DIM, 3 * MLSTM_DIM)
REST_SPLITS = (MLSTM_DIM, MLSTM_DIM + SGU_DIM, MLSTM_DIM + 2 * SGU_DIM, MLSTM_DIM + 2 * SGU_DIM + D_MODEL)
EPS = 1e-6
M_INIT = -1e30

kernel_name = "hybrid_mlstm_sgu_convffn_dit_block"


def rmsnorm(x, g):
    xf = x.astype(jnp.float32)
    y = xf * lax.rsqrt(jnp.mean(xf * xf, axis=-1, keepdims=True) + EPS)
    return (y * g.astype(jnp.float32)).astype(x.dtype)


def modulate(h, shift, scale):
    return h * (1.0 + scale) + shift


def dwconv1d(x, w):
    return lax.conv_general_dilated(x, w[:, None, :].astype(x.dtype), window_strides=(1,), padding='SAME',
                                    dimension_numbers=('NWC', 'WIO', 'NWC'), feature_group_count=x.shape[-1])


def dwconv2d(x, w):
    return lax.conv_general_dilated(x, w[:, :, None, :].astype(x.dtype), window_strides=(1, 1), padding='SAME',
                                    dimension_numbers=('NHWC', 'HWIO', 'NHWC'), feature_group_count=x.shape[-1])


def mlstm_init_state(b):
    return (jnp.zeros((b, MLSTM_HEADS, MLSTM_HEAD_DIM, MLSTM_HEAD_DIM), jnp.float32),
            jnp.zeros((b, MLSTM_HEADS, MLSTM_HEAD_DIM), jnp.float32),
            jnp.full((b, MLSTM_HEADS), M_INIT, jnp.float32))


def mlstm_chunk_stats(k, v, log_i, log_f):
    b = jnp.cumsum(log_f, axis=-1)
    g = b[..., -1]
    a = g[..., None] - b + log_i
    m_loc = jnp.max(a, axis=-1)
    kw = k * jnp.exp(a - m_loc[..., None])[..., None]
    kv = jnp.einsum('bnhld,bnhle->bnhde', kw, v)
    kn = jnp.sum(kw, axis=3)
    return b, g, m_loc, kv, kn


def mlstm_state_scan(g, m_loc, kv, kn, state0):
    def step(carry, inp):
        c_st, n_st, m_st = carry
        g_c, m_c, kv_c, kn_c = inp
        m_new = jnp.maximum(g_c + m_st, m_c)
        a_old = jnp.exp(g_c + m_st - m_new)
        a_new = jnp.exp(m_c - m_new)
        c_new = a_old[..., None, None] * c_st + a_new[..., None, None] * kv_c
        n_new = a_old[..., None] * n_st + a_new[..., None] * kn_c
        return (c_new, n_new, m_new), (c_st, n_st, m_st)
    xs = (jnp.moveaxis(g, 1, 0), jnp.moveaxis(m_loc, 1, 0), jnp.moveaxis(kv, 1, 0), jnp.moveaxis(kn, 1, 0))
    final, prev = lax.scan(step, state0, xs)
    prev = (jnp.moveaxis(prev[0], 0, 1), jnp.moveaxis(prev[1], 0, 1), jnp.moveaxis(prev[2], 0, 1))
    return prev, final


def mlstm_chunk_out(q, k, v, log_i, b, prev):
    c_prev, n_prev, m_prev = prev
    L = q.shape[3]
    inter = b + m_prev[..., None]
    d = b[..., :, None] - b[..., None, :] + log_i[..., None, :]
    scan_order = jnp.tril(jnp.ones((L, L), dtype=bool))
    d = jnp.where(scan_order, d, -jnp.inf)
    m_row = jnp.maximum(inter, jnp.max(d, axis=-1))
    w_inter = jnp.exp(inter - m_row)
    s = jnp.einsum('bnhld,bnhsd->bnhls', q, k) * jnp.exp(d - m_row[..., None])
    num = jnp.einsum('bnhls,bnhse->bnhle', s, v) + w_inter[..., None] * jnp.einsum('bnhld,bnhde->bnhle', q, c_prev)
    den = jnp.sum(s, axis=-1) + w_inter * jnp.einsum('bnhld,bnhd->bnhl', q, n_prev)
    return num / jnp.maximum(jnp.abs(den), jnp.exp(-m_row))[..., None]


def mlstm_direction(q, k, v, i_pre, f_pre, state0, with_out):
    bsz, t_len = q.shape[0], q.shape[1]
    n_chunks = t_len // MLSTM_CHUNK

    def chunks(t):
        return jnp.swapaxes(t.reshape((bsz, n_chunks, MLSTM_CHUNK) + t.shape[2:]), 2, 3)

    qc, kc, vc = chunks(q), chunks(k), chunks(v)
    log_i = chunks(i_pre)
    log_f = chunks(jax.nn.log_sigmoid(f_pre))
    b, g, m_loc, kv, kn = mlstm_chunk_stats(kc, vc, log_i, log_f)
    prev, final = mlstm_state_scan(g, m_loc, kv, kn, state0)
    if not with_out:
        return None, final
    h = mlstm_chunk_out(qc, kc, vc, log_i, b, prev)
    return jnp.swapaxes(h, 2, 3).reshape(bsz, t_len, MLSTM_HEADS, MLSTM_HEAD_DIM), final


def mlstm_bidir(q, k, v, gates, state_f, state_b, with_out):
    flip = lambda t: jnp.flip(t, axis=1)
    h_f, fin_f = mlstm_direction(q, k, v, gates[:, :, 0], gates[:, :, 1], state_f, with_out)
    h_b, fin_b = mlstm_direction(flip(q), flip(k), flip(v), flip(gates[:, :, 2]), flip(gates[:, :, 3]), state_b, with_out)
    h = h_f + flip(h_b) if with_out else None
    return h, fin_f, fin_b


def spatial_gating(u, v, ln_g, ln_b, w_s, b_s):
    bsz, t_len, _ = u.shape
    vf = v.astype(jnp.float32)
    vf = vf - jnp.mean(vf, axis=-1, keepdims=True)
    vn = vf * lax.rsqrt(jnp.mean(vf * vf, axis=-1, keepdims=True) + EPS) * ln_g + ln_b
    vn = vn.reshape(bsz, t_len // SGU_CHUNK, SGU_CHUNK, SGU_GROUPS, SGU_GROUP_DIM)
    mixed = jnp.einsum('gpq,bnqgd->bnpgd', w_s.astype(jnp.float32), vn) + b_s.T[:, :, None]
    return u * mixed.reshape(bsz, t_len, SGU_DIM).astype(u.dtype)


def token_mixer(hn, w_in_l, b_gate_l, conv_qk_l, head_norm_g_l, sgu_ln_g_l, sgu_ln_b_l, w_s_l, b_s_l,
                w_branch_mlstm_l, w_branch_sgu_l, w_out_l, state_f, state_b, with_out):
    bsz, t_len, _ = hn.shape
    n_cols = N_IN if with_out else N_SCAN_IN
    z = hn @ w_in_l[:, :n_cols]
    z_q, z_k, z_v, z_g = jnp.split(z[..., :N_SCAN_IN], SCAN_SPLITS, axis=-1)
    qk = jax.nn.silu(dwconv1d(jnp.concatenate([z_q, z_k], axis=-1), conv_qk_l)).astype(jnp.float32)
    q, k = jnp.split(qk, 2, axis=-1)
    heads = lambda t: t.reshape(bsz, t_len, MLSTM_HEADS, MLSTM_HEAD_DIM)
    q = heads(q) * (MLSTM_HEAD_DIM ** -0.5)
    gates = (z_g + b_gate_l).astype(jnp.float32).reshape(bsz, t_len, 4, MLSTM_HEADS)
    h_m, fin_f, fin_b = mlstm_bidir(q, heads(k), heads(z_v.astype(jnp.float32)), gates, state_f, state_b, with_out)
    if not with_out:
        return None, fin_f, fin_b
    z_o, z_u, z_vg, z_gm, z_gg = jnp.split(z[..., N_SCAN_IN:], REST_SPLITS, axis=-1)
    h_m = h_m * lax.rsqrt(jnp.mean(h_m * h_m, axis=-1, keepdims=True) + EPS)
    h_m = (h_m * head_norm_g_l.astype(jnp.float32).reshape(MLSTM_HEADS, MLSTM_HEAD_DIM)).reshape(bsz, t_len, MLSTM_DIM)
    y_m = (jax.nn.sigmoid(z_o.astype(jnp.float32)) * h_m).astype(hn.dtype)
    y_s = spatial_gating(jax.nn.gelu(z_u), jax.nn.gelu(z_vg), sgu_ln_g_l, sgu_ln_b_l, w_s_l, b_s_l)
    y = jax.nn.sigmoid(z_gm) * (y_m @ w_branch_mlstm_l) + jax.nn.sigmoid(z_gg) * (y_s @ w_branch_sgu_l)
    return y @ w_out_l, fin_f, fin_b


def conv_ffn(hn, w_up_l, w_conv_l, w_down_l, grid_rows):
    a, b = jnp.split(hn @ w_up_l, 2, axis=-1)
    bsz, t_len, _ = a.shape
    if grid_rows is None:
        a = dwconv1d(a, w_conv_l[FFN_CONV // 2])
    else:
        a = dwconv2d(a.reshape(bsz, grid_rows, GRID_W, D_FF), w_conv_l).reshape(bsz, t_len, D_FF)
    return (jax.nn.silu(a) * b) @ w_down_l


def _fwd_setup_inputs(seed: int = 0) -> dict:
    key = jax.random.key(seed)
    ks = iter(jax.random.split(key, 40))
    nrm = lambda shape, std: std * jax.random.normal(next(ks), shape, jnp.float32)
    L, D = DEPTH, D_MODEL
    b_i = nrm((L, 2, MLSTM_HEADS), 0.1)
    b_f = jax.random.uniform(next(ks), (L, 2, MLSTM_HEADS), jnp.float32, 3.0, 6.0)
    b_gate = jnp.stack([b_i, b_f], axis=2).reshape(L, N_GATES)
    return {
        "x": nrm((BATCH, SEQ, D), 1.0),
        "c": nrm((BATCH, D), 1.0),
        "ctx": nrm((BATCH, CTX_LEN, D), 1.0),
        "c_ctx": nrm((D,), 1.0),
        "w_mod": nrm((L, D, N_MOD * D), 0.02),
        "b_mod": nrm((L, N_MOD * D), 0.02),
        "norm1_g": 1.0 + nrm((L, D), 0.02),
        "w_in": nrm((L, D, N_IN), D ** -0.5),
        "b_gate": b_gate,
        "conv_qk": nrm((L, QK_CONV, 2 * MLSTM_DIM), QK_CONV ** -0.5),
        "head_norm_g": 1.0 + nrm((L, MLSTM_DIM), 0.02),
        "sgu_ln_g": 1.0 + nrm((L, SGU_DIM), 0.02),
        "sgu_ln_b": nrm((L, SGU_DIM), 0.02),
        "w_s": nrm((L, SGU_GROUPS, SGU_CHUNK, SGU_CHUNK), SGU_CHUNK ** -0.5),
        "b_s": 1.0 + nrm((L, SGU_GROUPS, SGU_CHUNK), 0.1),
        "w_branch_mlstm": nrm((L, MLSTM_DIM, D), MLSTM_DIM ** -0.5),
        "w_branch_sgu": nrm((L, SGU_DIM, D), SGU_DIM ** -0.5),
        "w_out": nrm((L, D, D), D ** -0.5),
        "norm2_g": 1.0 + nrm((L, D), 0.02),
        "w_up": nrm((L, D, 2 * D_FF), D ** -0.5),
        "w_ffn_conv": nrm((L, FFN_CONV, FFN_CONV, D_FF), 1.0 / FFN_CONV),
        "w_down": nrm((L, D_FF, D), D_FF ** -0.5),
        "final_g": 1.0 + nrm((D,), 0.02),
    }


def _fwd_reference(x, c, ctx, c_ctx, w_mod, b_mod, norm1_g, w_in, b_gate, conv_qk, head_norm_g, sgu_ln_g, sgu_ln_b,
              w_s, b_s, w_branch_mlstm, w_branch_sgu, w_out, norm2_g, w_up, w_ffn_conv, w_down, final_g):
    bsz, t_len, _ = x.shape
    rows = t_len // GRID_W
    h_x, h_c = x, ctx
    for l in range(DEPTH):
        last = l == DEPTH - 1
        mx = jnp.split((jax.nn.silu(c) @ w_mod[l] + b_mod[l])[:, None, :], N_MOD, axis=-1)
        mc = jnp.split(jax.nn.silu(c_ctx) @ w_mod[l] + b_mod[l], N_MOD, axis=-1)
        mixer_w = (w_in[l], b_gate[l], conv_qk[l], head_norm_g[l], sgu_ln_g[l], sgu_ln_b[l], w_s[l], b_s[l],
                   w_branch_mlstm[l], w_branch_sgu[l], w_out[l])
        hn_c = modulate(rmsnorm(h_c, norm1_g[l]), mc[0], mc[1])
        init = mlstm_init_state(bsz)
        out_c, st_f, st_b = token_mixer(hn_c, *mixer_w, init, init, not last)
        hn_x = modulate(rmsnorm(h_x, norm1_g[l]), mx[0], mx[1])
        out_x, _, _ = token_mixer(hn_x, *mixer_w, st_f, st_b, True)
        h_x = h_x + mx[2] * out_x
        h_x = h_x + mx[5] * conv_ffn(modulate(rmsnorm(h_x, norm2_g[l]), mx[3], mx[4]), w_up[l], w_ffn_conv[l], w_down[l], rows)
        if not last:
            h_c = h_c + mc[2] * out_c
            h_c = h_c + mc[5] * conv_ffn(modulate(rmsnorm(h_c, norm2_g[l]), mc[3], mc[4]), w_up[l], w_ffn_conv[l], w_down[l], None)
    return rmsnorm(h_x, final_g)


import jax as _jax
import jax.numpy as _jnp

TWIN_FORMAT = 'train_step'
FWD_PARAMS = ['x', 'c', 'ctx', 'c_ctx', 'w_mod', 'b_mod', 'norm1_g', 'w_in', 'b_gate', 'conv_qk', 'head_norm_g', 'sgu_ln_g', 'sgu_ln_b', 'w_s', 'b_s', 'w_branch_mlstm', 'w_branch_sgu', 'w_out', 'norm2_g', 'w_up', 'w_ffn_conv', 'w_down', 'final_g']
TWIN_WEIGHTS = ['c_ctx', 'w_mod', 'b_mod', 'norm1_g', 'w_in', 'b_gate', 'conv_qk', 'head_norm_g', 'sgu_ln_g', 'sgu_ln_b', 'w_s', 'b_s', 'w_branch_mlstm', 'w_branch_sgu', 'w_out', 'norm2_g', 'w_up', 'w_ffn_conv', 'w_down', 'final_g']
TWIN_DIFF_INPUT = 'x'
TWIN_INPUTS = ['x', 'c', 'ctx', 'c_ctx', 'w_mod', 'b_mod', 'norm1_g', 'w_in', 'b_gate', 'conv_qk', 'head_norm_g', 'sgu_ln_g', 'sgu_ln_b', 'w_s', 'b_s', 'w_branch_mlstm', 'w_branch_sgu', 'w_out', 'norm2_g', 'w_up', 'w_ffn_conv', 'w_down', 'final_g', 'loss_target', 'm_c_ctx', 'm_w_mod', 'm_b_mod', 'm_norm1_g', 'm_w_in', 'm_b_gate', 'm_conv_qk', 'm_head_norm_g', 'm_sgu_ln_g', 'm_sgu_ln_b', 'm_w_s', 'm_b_s', 'm_w_branch_mlstm', 'm_w_branch_sgu', 'm_w_out', 'm_norm2_g', 'm_w_up', 'm_w_ffn_conv', 'm_w_down', 'm_final_g', 'v_c_ctx', 'v_w_mod', 'v_b_mod', 'v_norm1_g', 'v_w_in', 'v_b_gate', 'v_conv_qk', 'v_head_norm_g', 'v_sgu_ln_g', 'v_sgu_ln_b', 'v_w_s', 'v_b_s', 'v_w_branch_mlstm', 'v_w_branch_sgu', 'v_w_out', 'v_norm2_g', 'v_w_up', 'v_w_ffn_conv', 'v_w_down', 'v_final_g']
TWIN_OUTPUTS = ['loss', 'grad_x', 'grad_c_ctx', 'grad_w_mod', 'grad_b_mod', 'grad_norm1_g', 'grad_w_in', 'grad_b_gate', 'grad_conv_qk', 'grad_head_norm_g', 'grad_sgu_ln_g', 'grad_sgu_ln_b', 'grad_w_s', 'grad_b_s', 'grad_w_branch_mlstm', 'grad_w_branch_sgu', 'grad_w_out', 'grad_norm2_g', 'grad_w_up', 'grad_w_ffn_conv', 'grad_w_down', 'grad_final_g', 'delta_c_ctx', 'delta_w_mod', 'delta_b_mod', 'delta_norm1_g', 'delta_w_in', 'delta_b_gate', 'delta_conv_qk', 'delta_head_norm_g', 'delta_sgu_ln_g', 'delta_sgu_ln_b', 'delta_w_s', 'delta_b_s', 'delta_w_branch_mlstm', 'delta_w_branch_sgu', 'delta_w_out', 'delta_norm2_g', 'delta_w_up', 'delta_w_ffn_conv', 'delta_w_down', 'delta_final_g', 'new_m_c_ctx', 'new_m_w_mod', 'new_m_b_mod', 'new_m_norm1_g', 'new_m_w_in', 'new_m_b_gate', 'new_m_conv_qk', 'new_m_head_norm_g', 'new_m_sgu_ln_g', 'new_m_sgu_ln_b', 'new_m_w_s', 'new_m_b_s', 'new_m_w_branch_mlstm', 'new_m_w_branch_sgu', 'new_m_w_out', 'new_m_norm2_g', 'new_m_w_up', 'new_m_w_ffn_conv', 'new_m_w_down', 'new_m_final_g', 'new_v_c_ctx', 'new_v_w_mod', 'new_v_b_mod', 'new_v_norm1_g', 'new_v_w_in', 'new_v_b_gate', 'new_v_conv_qk', 'new_v_head_norm_g', 'new_v_sgu_ln_g', 'new_v_sgu_ln_b', 'new_v_w_s', 'new_v_b_s', 'new_v_w_branch_mlstm', 'new_v_w_branch_sgu', 'new_v_w_out', 'new_v_norm2_g', 'new_v_w_up', 'new_v_w_ffn_conv', 'new_v_w_down', 'new_v_final_g']
TWIN_LEAF_KINDS = {'loss': 'loss', 'grad_x': 'grad_x', 'grad_c_ctx': 'grad_w', 'grad_w_mod': 'grad_w', 'grad_b_mod': 'grad_w', 'grad_norm1_g': 'grad_w', 'grad_w_in': 'grad_w', 'grad_b_gate': 'grad_w', 'grad_conv_qk': 'grad_w', 'grad_head_norm_g': 'grad_w', 'grad_sgu_ln_g': 'grad_w', 'grad_sgu_ln_b': 'grad_w', 'grad_w_s': 'grad_w', 'grad_b_s': 'grad_w', 'grad_w_branch_mlstm': 'grad_w', 'grad_w_branch_sgu': 'grad_w', 'grad_w_out': 'grad_w', 'grad_norm2_g': 'grad_w', 'grad_w_up': 'grad_w', 'grad_w_ffn_conv': 'grad_w', 'grad_w_down': 'grad_w', 'grad_final_g': 'grad_w', 'delta_c_ctx': 'delta_w', 'delta_w_mod': 'delta_w', 'delta_b_mod': 'delta_w', 'delta_norm1_g': 'delta_w', 'delta_w_in': 'delta_w', 'delta_b_gate': 'delta_w', 'delta_conv_qk': 'delta_w', 'delta_head_norm_g': 'delta_w', 'delta_sgu_ln_g': 'delta_w', 'delta_sgu_ln_b': 'delta_w', 'delta_w_s': 'delta_w', 'delta_b_s': 'delta_w', 'delta_w_branch_mlstm': 'delta_w', 'delta_w_branch_sgu': 'delta_w', 'delta_w_out': 'delta_w', 'delta_norm2_g': 'delta_w', 'delta_w_up': 'delta_w', 'delta_w_ffn_conv': 'delta_w', 'delta_w_down': 'delta_w', 'delta_final_g': 'delta_w', 'new_m_c_ctx': 'new_m', 'new_m_w_mod': 'new_m', 'new_m_b_mod': 'new_m', 'new_m_norm1_g': 'new_m', 'new_m_w_in': 'new_m', 'new_m_b_gate': 'new_m', 'new_m_conv_qk': 'new_m', 'new_m_head_norm_g': 'new_m', 'new_m_sgu_ln_g': 'new_m', 'new_m_sgu_ln_b': 'new_m', 'new_m_w_s': 'new_m', 'new_m_b_s': 'new_m', 'new_m_w_branch_mlstm': 'new_m', 'new_m_w_branch_sgu': 'new_m', 'new_m_w_out': 'new_m', 'new_m_norm2_g': 'new_m', 'new_m_w_up': 'new_m', 'new_m_w_ffn_conv': 'new_m', 'new_m_w_down': 'new_m', 'new_m_final_g': 'new_m', 'new_v_c_ctx': 'new_v', 'new_v_w_mod': 'new_v', 'new_v_b_mod': 'new_v', 'new_v_norm1_g': 'new_v', 'new_v_w_in': 'new_v', 'new_v_b_gate': 'new_v', 'new_v_conv_qk': 'new_v', 'new_v_head_norm_g': 'new_v', 'new_v_sgu_ln_g': 'new_v', 'new_v_sgu_ln_b': 'new_v', 'new_v_w_s': 'new_v', 'new_v_b_s': 'new_v', 'new_v_w_branch_mlstm': 'new_v', 'new_v_w_branch_sgu': 'new_v', 'new_v_w_out': 'new_v', 'new_v_norm2_g': 'new_v', 'new_v_w_up': 'new_v', 'new_v_w_ffn_conv': 'new_v', 'new_v_w_down': 'new_v', 'new_v_final_g': 'new_v'}


def _forward(args):
    return _fwd_reference(*[args[k] for k in FWD_PARAMS])


def _output_shape():
    def fwd():
        inp = _fwd_setup_inputs(0)
        return _fwd_reference(*[inp[k] for k in FWD_PARAMS])
    out = _jax.eval_shape(fwd)
    return out.shape, out.dtype

N_MICROBATCH = 1
ADAM_LR = 0.001
ADAM_B1 = 0.9
ADAM_B2 = 0.999
ADAM_EPS = 1e-08
ADAM_WD = 0.01
ADAM_STEP = 10
PER_EXAMPLE_BATCH_AXIS = {'x': 0, 'c': 0, 'ctx': 0, 'loss_target': 0}
SHARED_INPUTS = []
_WEIGHT_DTYPES = {'c_ctx': _jnp.float32, 'w_mod': _jnp.float32, 'b_mod': _jnp.float32, 'norm1_g': _jnp.float32, 'w_in': _jnp.float32, 'b_gate': _jnp.float32, 'conv_qk': _jnp.float32, 'head_norm_g': _jnp.float32, 'sgu_ln_g': _jnp.float32, 'sgu_ln_b': _jnp.float32, 'w_s': _jnp.float32, 'b_s': _jnp.float32, 'w_branch_mlstm': _jnp.float32, 'w_branch_sgu': _jnp.float32, 'w_out': _jnp.float32, 'norm2_g': _jnp.float32, 'w_up': _jnp.float32, 'w_ffn_conv': _jnp.float32, 'w_down': _jnp.float32, 'final_g': _jnp.float32}
MOMENT_SCALE = {'c_ctx': 9.592509e-03, 'w_mod': 1.247780e-01, 'b_mod': 2.286454e-01, 'norm1_g': 1.213898e-01, 'w_in': 4.355049e-02, 'b_gate': 1.109411e-01, 'conv_qk': 9.573997e-03, 'head_norm_g': 4.212437e-02, 'sgu_ln_g': 5.532425e-02, 'sgu_ln_b': 5.981037e-02, 'w_s': 8.215749e-02, 'b_s': 7.789112e-02, 'w_branch_mlstm': 4.209566e-02, 'w_branch_sgu': 8.082580e-02, 'w_out': 9.092142e-02, 'norm2_g': 1.690031e-01, 'w_up': 7.068666e-02, 'w_ffn_conv': 7.206308e-02, 'w_down': 1.162209e-01, 'final_g': 1.279288e+02}


def _to_microbatches(a, axis):
    t = _jnp.moveaxis(a, axis, 0)
    t = t.reshape((N_MICROBATCH, t.shape[0] // N_MICROBATCH) + t.shape[1:])
    return _jnp.moveaxis(t, 1, axis + 1)


def setup_inputs(seed: int = 0) -> dict:
    inp = _fwd_setup_inputs(seed)
    key = _jax.random.fold_in(_jax.random.key(seed), 7919)
    shape, _ = _output_shape()
    out = dict(inp)
    out["loss_target"] = _jax.random.normal(_jax.random.fold_in(key, 0), shape, _jnp.float32)
    for i, name in enumerate(TWIN_WEIGHTS):
        w = inp[name].astype(_jnp.float32)
        if MOMENT_SCALE is None:
            s = _jnp.sqrt(_jnp.mean(_jnp.square(w)) + 1e-30)
        else:
            s = MOMENT_SCALE[name]
        km, kv = _jax.random.split(_jax.random.fold_in(key, i + 1))
        out[name] = w
        out["m_" + name] = s * _jax.random.normal(km, w.shape, _jnp.float32)
        out["v_" + name] = (s * s) * _jax.random.uniform(kv, w.shape, _jnp.float32, 0.5, 1.5)
    if N_MICROBATCH > 1:
        for name, axis in PER_EXAMPLE_BATCH_AXIS.items():
            out[name] = _to_microbatches(out[name], axis)
    return {'x': out['x'], 'c': out['c'], 'ctx': out['ctx'], 'c_ctx': out['c_ctx'], 'w_mod': out['w_mod'], 'b_mod': out['b_mod'], 'norm1_g': out['norm1_g'], 'w_in': out['w_in'], 'b_gate': out['b_gate'], 'conv_qk': out['conv_qk'], 'head_norm_g': out['head_norm_g'], 'sgu_ln_g': out['sgu_ln_g'], 'sgu_ln_b': out['sgu_ln_b'], 'w_s': out['w_s'], 'b_s': out['b_s'], 'w_branch_mlstm': out['w_branch_mlstm'], 'w_branch_sgu': out['w_branch_sgu'], 'w_out': out['w_out'], 'norm2_g': out['norm2_g'], 'w_up': out['w_up'], 'w_ffn_conv': out['w_ffn_conv'], 'w_down': out['w_down'], 'final_g': out['final_g'], 'loss_target': out['loss_target'], 'm_c_ctx': out['m_c_ctx'], 'm_w_mod': out['m_w_mod'], 'm_b_mod': out['m_b_mod'], 'm_norm1_g': out['m_norm1_g'], 'm_w_in': out['m_w_in'], 'm_b_gate': out['m_b_gate'], 'm_conv_qk': out['m_conv_qk'], 'm_head_norm_g': out['m_head_norm_g'], 'm_sgu_ln_g': out['m_sgu_ln_g'], 'm_sgu_ln_b': out['m_sgu_ln_b'], 'm_w_s': out['m_w_s'], 'm_b_s': out['m_b_s'], 'm_w_branch_mlstm': out['m_w_branch_mlstm'], 'm_w_branch_sgu': out['m_w_branch_sgu'], 'm_w_out': out['m_w_out'], 'm_norm2_g': out['m_norm2_g'], 'm_w_up': out['m_w_up'], 'm_w_ffn_conv': out['m_w_ffn_conv'], 'm_w_down': out['m_w_down'], 'm_final_g': out['m_final_g'], 'v_c_ctx': out['v_c_ctx'], 'v_w_mod': out['v_w_mod'], 'v_b_mod': out['v_b_mod'], 'v_norm1_g': out['v_norm1_g'], 'v_w_in': out['v_w_in'], 'v_b_gate': out['v_b_gate'], 'v_conv_qk': out['v_conv_qk'], 'v_head_norm_g': out['v_head_norm_g'], 'v_sgu_ln_g': out['v_sgu_ln_g'], 'v_sgu_ln_b': out['v_sgu_ln_b'], 'v_w_s': out['v_w_s'], 'v_b_s': out['v_b_s'], 'v_w_branch_mlstm': out['v_w_branch_mlstm'], 'v_w_branch_sgu': out['v_w_branch_sgu'], 'v_w_out': out['v_w_out'], 'v_norm2_g': out['v_norm2_g'], 'v_w_up': out['v_w_up'], 'v_w_ffn_conv': out['v_w_ffn_conv'], 'v_w_down': out['v_w_down'], 'v_final_g': out['v_final_g']}


def _loss(weights, diff, rest, loss_target):
    with _jax.named_scope("forward"):
        args = {**rest, TWIN_DIFF_INPUT: diff, **{k: w.astype(_WEIGHT_DTYPES[k]) for k, w in weights.items()}}
        y = _forward(args)
    with _jax.named_scope("loss_head"):
        err = _jnp.square(y.astype(_jnp.float32) - loss_target)
        return 0.5 * _jnp.sum(_jnp.mean(err, axis=-1)) if err.ndim else 0.5 * err


def _adamw(w, g, m, v):
    m = ADAM_B1 * m + (1.0 - ADAM_B1) * g
    v = ADAM_B2 * v + (1.0 - ADAM_B2) * _jnp.square(g)
    m_hat = m / (1.0 - ADAM_B1 ** ADAM_STEP)
    v_hat = v / (1.0 - ADAM_B2 ** ADAM_STEP)
    delta = -ADAM_LR * (m_hat / (_jnp.sqrt(v_hat) + ADAM_EPS) + ADAM_WD * w)
    return delta, m, v


def reference(x, c, ctx, c_ctx, w_mod, b_mod, norm1_g, w_in, b_gate, conv_qk, head_norm_g, sgu_ln_g, sgu_ln_b, w_s, b_s, w_branch_mlstm, w_branch_sgu, w_out, norm2_g, w_up, w_ffn_conv, w_down, final_g, loss_target, m_c_ctx, m_w_mod, m_b_mod, m_norm1_g, m_w_in, m_b_gate, m_conv_qk, m_head_norm_g, m_sgu_ln_g, m_sgu_ln_b, m_w_s, m_b_s, m_w_branch_mlstm, m_w_branch_sgu, m_w_out, m_norm2_g, m_w_up, m_w_ffn_conv, m_w_down, m_final_g, v_c_ctx, v_w_mod, v_b_mod, v_norm1_g, v_w_in, v_b_gate, v_conv_qk, v_head_norm_g, v_sgu_ln_g, v_sgu_ln_b, v_w_s, v_b_s, v_w_branch_mlstm, v_w_branch_sgu, v_w_out, v_norm2_g, v_w_up, v_w_ffn_conv, v_w_down, v_final_g):
    given = dict(x=x, c=c, ctx=ctx, c_ctx=c_ctx, w_mod=w_mod, b_mod=b_mod, norm1_g=norm1_g, w_in=w_in, b_gate=b_gate, conv_qk=conv_qk, head_norm_g=head_norm_g, sgu_ln_g=sgu_ln_g, sgu_ln_b=sgu_ln_b, w_s=w_s, b_s=b_s, w_branch_mlstm=w_branch_mlstm, w_branch_sgu=w_branch_sgu, w_out=w_out, norm2_g=norm2_g, w_up=w_up, w_ffn_conv=w_ffn_conv, w_down=w_down, final_g=final_g, loss_target=loss_target, m_c_ctx=m_c_ctx, m_w_mod=m_w_mod, m_b_mod=m_b_mod, m_norm1_g=m_norm1_g, m_w_in=m_w_in, m_b_gate=m_b_gate, m_conv_qk=m_conv_qk, m_head_norm_g=m_head_norm_g, m_sgu_ln_g=m_sgu_ln_g, m_sgu_ln_b=m_sgu_ln_b, m_w_s=m_w_s, m_b_s=m_b_s, m_w_branch_mlstm=m_w_branch_mlstm, m_w_branch_sgu=m_w_branch_sgu, m_w_out=m_w_out, m_norm2_g=m_norm2_g, m_w_up=m_w_up, m_w_ffn_conv=m_w_ffn_conv, m_w_down=m_w_down, m_final_g=m_final_g, v_c_ctx=v_c_ctx, v_w_mod=v_w_mod, v_b_mod=v_b_mod, v_norm1_g=v_norm1_g, v_w_in=v_w_in, v_b_gate=v_b_gate, v_conv_qk=v_conv_qk, v_head_norm_g=v_head_norm_g, v_sgu_ln_g=v_sgu_ln_g, v_sgu_ln_b=v_sgu_ln_b, v_w_s=v_w_s, v_b_s=v_b_s, v_w_branch_mlstm=v_w_branch_mlstm, v_w_branch_sgu=v_w_branch_sgu, v_w_out=v_w_out, v_norm2_g=v_norm2_g, v_w_up=v_w_up, v_w_ffn_conv=v_w_ffn_conv, v_w_down=v_w_down, v_final_g=v_final_g)
    weights = {n: given[n] for n in TWIN_WEIGHTS}
    shared = {n: given[n] for n in SHARED_INPUTS}
    per_example = {n: given[n] for n in ['x', 'c', 'ctx']}
    grad_fn = _jax.value_and_grad(_loss, argnums=(0, 1))

    def one_microbatch(ex, loss_target):
        ex = dict(ex)
        diff = ex.pop(TWIN_DIFF_INPUT)
        return grad_fn(weights, diff, {**shared, **ex}, loss_target)

    if N_MICROBATCH == 1:
        loss, (grad_w, grad_x) = one_microbatch(per_example, given["loss_target"])
    else:
        def body(carry, xs):
            loss_sum, grad_sum = carry
            l_k, (gw_k, gx_k) = one_microbatch(xs[0], xs[1])
            with _jax.named_scope("update"):
                return (loss_sum + l_k, _jax.tree.map(_jnp.add, grad_sum, gw_k)), gx_k

        init = (_jnp.zeros((), _jnp.float32), _jax.tree.map(_jnp.zeros_like, weights))
        (loss, grad_w), grad_x = _jax.lax.scan(body, init, (per_example, given["loss_target"]))
    with _jax.named_scope("update"):
        delta_w, new_m, new_v = {}, {}, {}
        for n in TWIN_WEIGHTS:
            delta_w[n], new_m[n], new_v[n] = _adamw(weights[n], grad_w[n], given["m_" + n], given["v_" + n])
    return (loss, grad_x, *[grad_w[n] for n in TWIN_WEIGHTS], *[delta_w[n] for n in TWIN_WEIGHTS],
            *[new_m[n] for n in TWIN_WEIGHTS], *[new_v[n] for n in TWIN_WEIGHTS])
```

```python
import functools
import math

import jax
import jax.numpy as jnp
from jax import lax
from jax.experimental import pallas as pl
from jax.experimental.pallas import tpu as pltpu

F32 = jnp.float32
BF16 = jnp.bfloat16

D = 1024
NH = 4
DH = 256
LC = 64
GW = 64
SG = 4
SGD = 256
SCH = 128
DFF = 2816
NMOD = 6
NGATE = 16
NIN = 8208
EPS = 1e-6
M_INIT = -1e30
TR = 256
LANE = 128
VMEM_LIMIT = 56 * 1024 * 1024
MESH = pl.DeviceIdType.MESH

ADAM_LR = 0.001
ADAM_B1 = 0.9
ADAM_B2 = 0.999
ADAM_EPS = 1e-08
ADAM_WD = 0.01
ADAM_STEP = 10

CB_O, CB_U, CB_VG, CB_GM, CB_GG, CB_V, CB_Q, CB_K = range(8)


def _pick(n, cands):
    for c in cands:
        if n % c == 0:
            return c
    return n


def _cparams(sem):
    return pltpu.CompilerParams(dimension_semantics=sem, vmem_limit_bytes=VMEM_LIMIT)


def _sigmoid(x):
    return 1.0 / (1.0 + jnp.exp(-x))


def _silu(x):
    return x * _sigmoid(x)


def _dsilu(x):
    s = _sigmoid(x)
    return s * (1.0 + x * (1.0 - s))


_GC = math.sqrt(2.0 / math.pi)


def _gelu(x):
    return 0.5 * x * (1.0 + jnp.tanh(_GC * (x + 0.044715 * x * x * x)))


def _dgelu(x):
    t = jnp.tanh(_GC * (x + 0.044715 * x * x * x))
    return 0.5 * (1.0 + t) + 0.5 * x * (1.0 - t * t) * _GC * (1.0 + 3.0 * 0.044715 * x * x)


def _dot(a, b):
    return jnp.dot(a.astype(BF16), b.astype(BF16), preferred_element_type=F32)


def _dot_nt(a, b):
    return lax.dot_general(a.astype(BF16), b.astype(BF16), (((1,), (1,)), ((), ())), preferred_element_type=F32)


def _dot_tn(a, b):
    return lax.dot_general(a.astype(BF16), b.astype(BF16), (((0,), (0,)), ((), ())), preferred_element_type=F32)


def _exact_dot(tri, x):
    x1 = x.astype(BF16)
    r1 = x - x1.astype(F32)
    x2 = r1.astype(BF16)
    x3 = (r1 - x2.astype(F32)).astype(BF16)
    return (jnp.dot(tri, x1, preferred_element_type=F32) + jnp.dot(tri, x2, preferred_element_type=F32)
            + jnp.dot(tri, x3, preferred_element_type=F32))


def _rb(tm, w, col=0, off=0):
    return pl.BlockSpec((tm, w), lambda i: (i + off, col))


def _bc(r, w):
    return pl.BlockSpec((r, w), lambda i: (0, 0))


def mm_nn(a, b, out_dtype, name):
    squeeze = a.ndim == 2
    if squeeze:
        a, b = a[None], b[None]
    g, m, k = a.shape
    n = b.shape[2]
    tm = _pick(m, (1280, 1024, 512, 256, 128))
    tn = _pick(n, (1024, 1408, 512, 128))
    tk = _pick(k, (1024, 1408, 512, 128))
    nk = k // tk

    def body(a_ref, b_ref, o_ref, *scr):
        if nk == 1:
            o_ref[0] = _dot(a_ref[0], b_ref[0]).astype(o_ref.dtype)
        else:
            acc_ref, = scr
            kk = pl.program_id(3)

            @pl.when(kk == 0)
            def _():
                acc_ref[...] = jnp.zeros_like(acc_ref)

            acc_ref[...] += _dot(a_ref[0], b_ref[0])

            @pl.when(kk == nk - 1)
            def _():
                o_ref[0] = acc_ref[...].astype(o_ref.dtype)

    out = pl.pallas_call(
        body, name=name, grid=(g, n // tn, m // tm, nk),
        in_specs=[pl.BlockSpec((1, tm, tk), lambda gi, j, i, kk: (gi, i, kk)),
                  pl.BlockSpec((1, tk, tn), lambda gi, j, i, kk: (gi, kk, j))],
        out_specs=pl.BlockSpec((1, tm, tn), lambda gi, j, i, kk: (gi, i, j)),
        out_shape=jax.ShapeDtypeStruct((g, m, n), out_dtype),
        scratch_shapes=[] if nk == 1 else [pltpu.VMEM((tm, tn), F32)],
        compiler_params=_cparams(("parallel", "parallel", "parallel", "arbitrary")),
    )(a, b)
    return out[0] if squeeze else out


def mm_tn(a, b, name):
    squeeze = a.ndim == 2
    if squeeze:
        a, b = a[None], b[None]
    g, t, ka = a.shape
    n = b.shape[2]
    tka = _pick(ka, (1024, 1408, 512, 128))
    tn = _pick(n, (1024, 1408, 512, 128))
    tt = _pick(t, (1280, 1024, 512, 256, 128))
    nt = t // tt

    def body(a_ref, b_ref, o_ref):
        tt_i = pl.program_id(3)

        @pl.when(tt_i == 0)
        def _():
            o_ref[...] = jnp.zeros_like(o_ref)

        o_ref[0] += _dot_tn(a_ref[0], b_ref[0])

    out = pl.pallas_call(
        body, name=name, grid=(g, ka // tka, n // tn, nt),
        in_specs=[pl.BlockSpec((1, tt, tka), lambda gi, i, j, ti: (gi, ti, i)),
                  pl.BlockSpec((1, tt, tn), lambda gi, i, j, ti: (gi, ti, j))],
        out_specs=pl.BlockSpec((1, tka, tn), lambda gi, i, j, ti: (gi, i, j)),
        out_shape=jax.ShapeDtypeStruct((g, ka, n), F32),
        compiler_params=_cparams(("parallel", "parallel", "parallel", "arbitrary")),
    )(a, b)
    return out[0] if squeeze else out


def mod_fwd(cc, w_mod, b_mod):
    n = w_mod.shape[1]

    def body(c_ref, w_ref, b_ref, o_ref):
        o_ref[...] = _dot(_silu(c_ref[...]), w_ref[...]) + b_ref[...]

    return pl.pallas_call(
        body, name="mod_fwd", grid=(n // D,),
        in_specs=[_bc(16, D), pl.BlockSpec((D, D), lambda j: (0, j)), pl.BlockSpec((1, D), lambda j: (0, j))],
        out_specs=pl.BlockSpec((16, D), lambda j: (0, j)),
        out_shape=jax.ShapeDtypeStruct((16, n), F32),
        compiler_params=_cparams(("parallel",)),
    )(cc, w_mod, b_mod)


def norm1_fwd(xcat, g, tab, n_ctx_tiles):
    tp = xcat.shape[0]

    def body(x_ref, g_ref, tab_ref, o_ref):
        x = x_ref[...]
        r = lax.rsqrt(jnp.mean(x * x, axis=-1, keepdims=True) + EPS)
        nrm = x * r * g_ref[...]
        is_ctx = pl.program_id(0) < n_ctx_tiles
        sh = jnp.where(is_ctx, tab_ref[0:1, :], tab_ref[2:3, :])
        sc = jnp.where(is_ctx, tab_ref[1:2, :], tab_ref[3:4, :])
        o_ref[...] = (nrm * (1.0 + sc) + sh).astype(BF16)

    return pl.pallas_call(
        body, name="norm1_fwd", grid=(tp // TR,),
        in_specs=[_rb(TR, D), _bc(1, D), _bc(8, D)],
        out_specs=_rb(TR, D),
        out_shape=jax.ShapeDtypeStruct((tp, D), BF16),
        compiler_params=_cparams(("parallel",)),
    )(xcat, g, tab)


def norm2_fwd(x, out, g, tab):
    t = x.shape[0]

    def body(x_ref, o_in_ref, g_ref, tab_ref, h1_ref, hn_ref):
        h1 = x_ref[...] + tab_ref[0:1, :] * o_in_ref[...]
        h1_ref[...] = h1
        r = lax.rsqrt(jnp.mean(h1 * h1, axis=-1, keepdims=True) + EPS)
        nrm = h1 * r * g_ref[...]
        hn_ref[...] = (nrm * (1.0 + tab_ref[2:3, :]) + tab_ref[1:2, :]).astype(BF16)

    return pl.pallas_call(
        body, name="norm2_fwd", grid=(t // TR,),
        in_specs=[_rb(TR, D), _rb(TR, D), _bc(1, D), _bc(8, D)],
        out_specs=[_rb(TR, D), _rb(TR, D)],
        out_shape=[jax.ShapeDtypeStruct((t, D), F32), jax.ShapeDtypeStruct((t, D), BF16)],
        compiler_params=_cparams(("parallel",)),
    )(x, out, g, tab)


def _halo_specs(tm, w, col, n_rows, hb):
    per = tm // hb
    last = n_rows // hb - 1
    prev = pl.BlockSpec((hb, w), lambda i: (jnp.maximum(i * per - 1, 0), col))
    nxt = pl.BlockSpec((hb, w), lambda i: (jnp.minimum((i + 1) * per, last), col))
    return prev, nxt


def _shift_rows(x, prev_row, next_row):
    tm = x.shape[0]
    rid = lax.broadcasted_iota(jnp.int32, x.shape, 0)
    xm1 = jnp.where(rid == 0, prev_row, pltpu.roll(x, 1, 0))
    xp1 = jnp.where(rid == tm - 1, next_row, pltpu.roll(x, tm - 1, 0))
    return xm1, xp1


def _seq_edges(i, n_ctx_tiles, n_tiles):
    first = jnp.logical_or(i == 0, i == n_ctx_tiles)
    last = jnp.logical_or(i == n_ctx_tiles - 1, i == n_tiles - 1)
    return first, last


def qkconv_fwd(zmain, zg, conv_w, b_gate, n_ctx_tiles):
    tp = zmain.shape[0]
    nt = tp // TR
    w2 = 2 * D
    prev_s, next_s = _halo_specs(TR, w2, CB_Q // 2, tp, 16)

    def body(z_ref, zp_ref, zn_ref, w_ref, zg_ref, bg_ref, q_ref, k_ref, g_ref, zgb_ref):
        i = pl.program_id(0)
        first, last = _seq_edges(i, n_ctx_tiles, nt)
        z = z_ref[...].astype(F32)
        pr = jnp.where(first, 0.0, zp_ref[15:16, :].astype(F32))
        nx = jnp.where(last, 0.0, zn_ref[0:1, :].astype(F32))
        zm1, zp1 = _shift_rows(z, pr, nx)
        cv = w_ref[0:1, :] * zm1 + w_ref[1:2, :] * z + w_ref[2:3, :] * zp1
        a = _silu(cv)
        q_ref[...] = (a[:, :D] * (DH ** -0.5)).astype(BF16)
        k_ref[...] = a[:, D:].astype(BF16)
        zgb = zg_ref[...] + bg_ref[...]
        zgb_ref[...] = zgb
        logf = jnp.minimum(zgb, 0.0) - jnp.log(1.0 + jnp.exp(-jnp.abs(zgb)))
        rr = lax.broadcasted_iota(jnp.int32, (TR, TR), 0)
        cc = lax.broadcasted_iota(jnp.int32, (TR, TR), 1)
        same = (rr // LC) == (cc // LC)
        low = jnp.where(jnp.logical_and(same, cc <= rr), 1.0, 0.0).astype(BF16)
        upp = jnp.where(jnp.logical_and(same, cc >= rr), 1.0, 0.0).astype(BF16)
        bf = _exact_dot(low, logf)
        bb = _exact_dot(upp, logf)
        lane = lax.broadcasted_iota(jnp.int32, (TR, LANE), 1)
        g = jnp.where(jnp.logical_and(lane >= 4, lane < 8), bf,
                      jnp.where(jnp.logical_and(lane >= 12, lane < 16), bb, zgb))
        g_ref[...] = g

    return pl.pallas_call(
        body, name="qkconv_fwd", grid=(nt,),
        in_specs=[_rb(TR, w2, CB_Q // 2), prev_s, next_s, _bc(8, w2), _rb(TR, LANE), _bc(1, LANE)],
        out_specs=[_rb(TR, D), _rb(TR, D), _rb(TR, LANE), _rb(TR, LANE)],
        out_shape=[jax.ShapeDtypeStruct((tp, D), BF16), jax.ShapeDtypeStruct((tp, D), BF16),
                   jax.ShapeDtypeStruct((tp, LANE), F32), jax.ShapeDtypeStruct((tp, LANE), F32)],
        compiler_params=_cparams(("parallel",)),
    )(zmain, zmain, zmain, conv_w, zg, b_gate)


def qkconv_bwd_a(zmain, dqf, dqb, dkf, dkb, conv_w, n_ctx_tiles):
    tp = zmain.shape[0]
    nt = tp // TR
    w2 = 2 * D
    prev_s, next_s = _halo_specs(TR, w2, CB_Q // 2, tp, 16)

    def body(z_ref, zp_ref, zn_ref, w_ref, dqf_ref, dqb_ref, dkf_ref, dkb_ref, dc_ref, dw_ref):
        i = pl.program_id(0)
        first, last = _seq_edges(i, n_ctx_tiles, nt)
        z = z_ref[...].astype(F32)
        pr = jnp.where(first, 0.0, zp_ref[15:16, :].astype(F32))
        nx = jnp.where(last, 0.0, zn_ref[0:1, :].astype(F32))
        zm1, zp1 = _shift_rows(z, pr, nx)
        cv = w_ref[0:1, :] * zm1 + w_ref[1:2, :] * z + w_ref[2:3, :] * zp1
        da = jnp.concatenate(
            [(dqf_ref[...].astype(F32) + dqb_ref[...].astype(F32)) * (DH ** -0.5),
             dkf_ref[...].astype(F32) + dkb_ref[...].astype(F32)], axis=1)
        dc = da * _dsilu(cv)
        dc_ref[...] = dc.astype(BF16)

        @pl.when(i == 0)
        def _():
            dw_ref[...] = jnp.zeros_like(dw_ref)

        dw_ref[0:1, :] += jnp.sum(zm1 * dc, axis=0, keepdims=True)
        dw_ref[1:2, :] += jnp.sum(z * dc, axis=0, keepdims=True)
        dw_ref[2:3, :] += jnp.sum(zp1 * dc, axis=0, keepdims=True)

    return pl.pallas_call(
        body, name="qkconv_bwd_a", grid=(nt,),
        in_specs=[_rb(TR, w2, CB_Q // 2), prev_s, next_s, _bc(8, w2), _rb(TR, D), _rb(TR, D), _rb(TR, D), _rb(TR, D)],
        out_specs=[_rb(TR, w2), _bc(8, w2)],
        out_shape=[jax.ShapeDtypeStruct((tp, w2), BF16), jax.ShapeDtypeStruct((8, w2), F32)],
        compiler_params=_cparams(("arbitrary",)),
    )(zmain, zmain, zmain, conv_w, dqf, dqb, dkf, dkb)


def qkconv_bwd_b(dz, dc, conv_w, n_ctx_tiles):
    tp = dc.shape[0]
    nt = tp // TR
    w2 = 2 * D
    prev_s, next_s = _halo_specs(TR, w2, 0, tp, 16)

    def body(dz_in_ref, d_ref, dp_ref, dn_ref, w_ref, o_ref):
        del dz_in_ref
        i = pl.program_id(0)
        first, last = _seq_edges(i, n_ctx_tiles, nt)
        d = d_ref[...].astype(F32)
        pr = jnp.where(first, 0.0, dp_ref[15:16, :].astype(F32))
        nx = jnp.where(last, 0.0, dn_ref[0:1, :].astype(F32))
        dm1, dp1 = _shift_rows(d, pr, nx)
        o_ref[...] = (w_ref[0:1, :] * dp1 + w_ref[1:2, :] * d + w_ref[2:3, :] * dm1).astype(BF16)

    return pl.pallas_call(
        body, name="qkconv_bwd_b", grid=(nt,),
        in_specs=[pl.BlockSpec(memory_space=pl.ANY), _rb(TR, w2), prev_s, next_s, _bc(8, w2)],
        out_specs=_rb(TR, w2, CB_Q // 2),
        out_shape=jax.ShapeDtypeStruct(dz.shape, BF16),
        input_output_aliases={0: 0},
        compiler_params=_cparams(("parallel",)),
    )(dz, dc, dc, dc, conv_w)


def add_into_dz(dz, a, b, col):
    tp = a.shape[0]

    def body(dz_in_ref, a_ref, b_ref, o_ref):
        del dz_in_ref
        o_ref[...] = (a_ref[...].astype(F32) + b_ref[...].astype(F32)).astype(BF16)

    return pl.pallas_call(
        body, name="add_into_dz", grid=(tp // TR,),
        in_specs=[pl.BlockSpec(memory_space=pl.ANY), _rb(TR, D), _rb(TR, D)],
        out_specs=_rb(TR, D, col),
        out_shape=jax.ShapeDtypeStruct(dz.shape, BF16),
        input_output_aliases={0: 0},
        compiler_params=_cparams(("parallel",)),
    )(dz, a, b)


def _chunk_maps(nc, ncc):
    def fwd(t):
        return t

    def bwd(t):
        return jnp.where(t < ncc, ncc - 1 - t, nc - 1 + ncc - t)

    return fwd, bwd


def _mlstm_chunk(d, h, gc, gr, q_ref, k_ref, v_ref, cp, npv, m_prev, mask):
    ic, bcol = 8 * d + h, 8 * d + 4 + h
    i_col, b_col = gc[:, ic:ic + 1], gc[:, bcol:bcol + 1]
    i_row, b_row = gr[ic:ic + 1, :], gr[bcol:bcol + 1, :]
    g = b_row[:, LC - 1:LC] if d == 0 else b_row[:, 0:1]
    a_row = g - b_row + i_row
    m_loc = jnp.max(a_row, axis=1, keepdims=True)
    dmat = jnp.where(mask, b_col - b_row + i_row, -jnp.inf)
    inter = b_col + m_prev
    m_row = jnp.maximum(inter, jnp.max(dmat, axis=1, keepdims=True))
    e = jnp.exp(dmat - m_row)
    w = jnp.exp(inter - m_row)
    hs = slice(h * DH, (h + 1) * DH)
    qh, kh, vh = q_ref[:, hs], k_ref[:, hs], v_ref[:, hs]
    p = _dot_nt(qh, kh)
    s = p * e
    cpb = cp.astype(BF16)
    qc = _dot(qh, cpb)
    num = _dot(s, vh) + w * qc
    qn = jnp.sum(qh.astype(F32) * npv, axis=1, keepdims=True)
    den = jnp.sum(s, axis=1, keepdims=True) + w * qn
    thr = jnp.exp(-m_row)
    m_new = jnp.maximum(g + m_prev, m_loc)
    a_old = jnp.exp(g + m_prev - m_new)
    a_col = g - b_col + i_col
    return dict(qh=qh, kh=kh, vh=vh, e=e, w=w, s=s, cpb=cpb, qc=qc, num=num, qn=qn, den=den, thr=thr,
                m_loc=m_loc, m_new=m_new, a_old=a_old, a_col=a_col, hs=hs)


def mlstm_fwd(qa, ka, zmain, gcol, grow, ncc):
    tp = qa.shape[0]
    nc = tp // LC
    cf, cb = _chunk_maps(nc, ncc)

    def body(qf, kf, vf, gcf, grf, qb, kb, vb, gcb, grb,
             hf_o, hb_o, cf_o, cb_o, nf_o, nb_o, mf_o, mb_o, c_sc, n_sc, m_sc):
        t = pl.program_id(0)

        @pl.when(t == 0)
        def _():
            c_sc[...] = jnp.zeros_like(c_sc)
            n_sc[...] = jnp.zeros_like(n_sc)
            m_sc[...] = jnp.full(m_sc.shape, M_INIT, F32)

        row = lax.broadcasted_iota(jnp.int32, (LC, LC), 0)
        col = lax.broadcasted_iota(jnp.int32, (LC, LC), 1)
        dirs = ((qf, kf, vf, gcf, grf, hf_o, cf_o, nf_o, mf_o), (qb, kb, vb, gcb, grb, hb_o, cb_o, nb_o, mb_o))
        for d, (q_ref, k_ref, v_ref, gc_ref, gr_ref, h_o, c_o, n_o, m_o) in enumerate(dirs):
            mask = (col <= row) if d == 0 else (col >= row)
            gc = gc_ref[...]
            gr = gr_ref[0]
            for h in range(NH):
                idx = d * NH + h
                cp = c_sc[idx]
                npv = n_sc[idx]
                m_full = m_sc[idx]
                m_prev = m_full[:, 0:1]
                r = _mlstm_chunk(d, h, gc, gr, q_ref, k_ref, v_ref, cp, npv, m_prev, mask)
                hs = r["hs"]
                h_o[:, hs] = r["num"] / jnp.maximum(jnp.abs(r["den"]), r["thr"])
                c_o[0, hs, :] = r["cpb"]
                n_o[0, h:h + 1, :] = npv
                m_o[0, h:h + 1, :] = m_full
                a_new = jnp.exp(r["m_loc"] - r["m_new"])
                kw = r["kh"].astype(F32) * jnp.exp(r["a_col"] - r["m_loc"])
                kv = _dot_tn(kw, r["vh"])
                kn = jnp.sum(kw, axis=0, keepdims=True)
                c_sc[idx] = r["a_old"] * cp + a_new * kv
                n_sc[idx] = r["a_old"] * npv + a_new * kn
                m_sc[idx] = jnp.broadcast_to(r["m_new"], (1, LANE))

    def dspecs(cm):
        return [pl.BlockSpec((LC, D), lambda t: (cm(t), 0)),
                pl.BlockSpec((LC, D), lambda t: (cm(t), 0)),
                pl.BlockSpec((LC, D), lambda t: (cm(t), CB_V)),
                pl.BlockSpec((LC, LANE), lambda t: (cm(t), 0)),
                pl.BlockSpec((1, 16, LC), lambda t: (cm(t), 0, 0))]

    def ospec(cm, shp):
        return pl.BlockSpec((1,) + shp, lambda t: (cm(t), 0, 0))

    return pl.pallas_call(
        body, name="mlstm_fwd", grid=(nc,),
        in_specs=dspecs(cf) + dspecs(cb),
        out_specs=[pl.BlockSpec((LC, D), lambda t: (cf(t), 0)), pl.BlockSpec((LC, D), lambda t: (cb(t), 0)),
                   ospec(cf, (D, DH)), ospec(cb, (D, DH)), ospec(cf, (NH, DH)), ospec(cb, (NH, DH)),
                   ospec(cf, (NH, LANE)), ospec(cb, (NH, LANE))],
        out_shape=[jax.ShapeDtypeStruct((tp, D), F32), jax.ShapeDtypeStruct((tp, D), F32),
                   jax.ShapeDtypeStruct((nc, D, DH), BF16), jax.ShapeDtypeStruct((nc, D, DH), BF16),
                   jax.ShapeDtypeStruct((nc, NH, DH), F32), jax.ShapeDtypeStruct((nc, NH, DH), F32),
                   jax.ShapeDtypeStruct((nc, NH, LANE), F32), jax.ShapeDtypeStruct((nc, NH, LANE), F32)],
        scratch_shapes=[pltpu.VMEM((2 * NH, DH, DH), F32), pltpu.VMEM((2 * NH, 1, DH), F32),
                        pltpu.VMEM((2 * NH, 1, LANE), F32)],
        compiler_params=_cparams(("arbitrary",)),
    )(qa, ka, zmain, gcol, grow, qa, ka, zmain, gcol, grow)


def mlstm_bwd(qa, ka, zmain, gcol, grow, states, dhm, ncc):
    tp = qa.shape[0]
    nc = tp // LC
    cf0, cb0 = _chunk_maps(nc, ncc)
    cf = lambda t: cf0(nc - 1 - t)
    cb = lambda t: cb0(nc - 1 - t)
    csf, csb, nsf, nsb, msf, msb = states

    def body(qf, kf, vf, gcf, grf, cpf, npf, mpf, dhf, qb, kb, vb, gcb, grb, cpb_, npb, mpb, dhb,
             dqf_o, dkf_o, dvf_o, colf_o, rowf_o, dqb_o, dkb_o, dvb_o, colb_o, rowb_o, dc_sc, dn_sc):
        t = pl.program_id(0)

        @pl.when(t == 0)
        def _():
            dc_sc[...] = jnp.zeros_like(dc_sc)
            dn_sc[...] = jnp.zeros_like(dn_sc)

        row = lax.broadcasted_iota(jnp.int32, (LC, LC), 0)
        col = lax.broadcasted_iota(jnp.int32, (LC, LC), 1)
        dirs = ((qf, kf, vf, gcf, grf, cpf, npf, mpf, dhf, dqf_o, dkf_o, dvf_o, colf_o, rowf_o, cf),
                (qb, kb, vb, gcb, grb, cpb_, npb, mpb, dhb, dqb_o, dkb_o, dvb_o, colb_o, rowb_o, cb))
        for d, (q_ref, k_ref, v_ref, gc_ref, gr_ref, cp_ref, np_ref, mp_ref, dh_ref,
                dq_o, dk_o, dv_o, col_o, row_o, cm) in enumerate(dirs):
            mask = (col <= row) if d == 0 else (col >= row)
            live = jnp.where(cm(t) >= ncc, 1.0, 0.0).astype(F32)
            gc = gc_ref[...]
            gr = gr_ref[0]
            col_o[...] = jnp.zeros_like(col_o)
            row_o[...] = jnp.zeros_like(row_o)
            for h in range(NH):
                idx = d * NH + h
                hs = slice(h * DH, (h + 1) * DH)
                cp = cp_ref[0, hs, :]
                npv = np_ref[0, h:h + 1, :]
                m_prev = mp_ref[0, h:h + 1, 0:1]
                r = _mlstm_chunk(d, h, gc, gr, q_ref, k_ref, v_ref, cp, npv, m_prev, mask)
                qh, kh, vh, e, w, s = r["qh"], r["kh"], r["vh"], r["e"], r["w"], r["s"]
                qf32, kf32 = qh.astype(F32), kh.astype(F32)
                den, thr = r["den"], r["thr"]
                rden = 1.0 / jnp.maximum(jnp.abs(den), thr)
                hh = r["num"] * rden
                dh = dh_ref[:, hs] * live
                dnum = dh * rden
                sgn = jnp.where(jnp.abs(den) > thr, jnp.sign(den), 0.0)
                dden = -jnp.sum(dh * hh, axis=1, keepdims=True) * rden * sgn
                ds = _dot_nt(dnum, vh) + dden
                dp = ds * e
                gm = ds * s
                rowsum = jnp.sum(gm, axis=1, keepdims=True)
                colsum = jnp.sum(gm, axis=0, keepdims=True)
                dq = _dot(dp, kh) + w * (_dot_nt(dnum, r["cpb"]) + dden * npv)
                dcs = dc_sc[idx]
                dns = dn_sc[idx]
                kfac = jnp.exp(r["a_col"] - r["m_new"])
                vdc = _dot_nt(vh, dcs)
                dk = _dot_tn(dp, qh) + kfac * (vdc + dns)
                dv = _dot_tn(s, dnum) + kfac * _dot(kh, dcs)
                beta = w * (jnp.sum(dnum * r["qc"], axis=1, keepdims=True) + dden * r["qn"])
                alpha = kfac * (jnp.sum(kf32 * vdc, axis=1, keepdims=True) + jnp.sum(kf32 * dns, axis=1, keepdims=True))
                dq_o[:, hs] = dq.astype(BF16)
                dk_o[:, hs] = dk.astype(BF16)
                dv_o[:, hs] = dv.astype(BF16)
                cpf = r["cpb"].astype(F32)
                inner = (jnp.sum(jnp.sum(dcs * cpf, axis=1, keepdims=True), axis=0, keepdims=True)
                         + jnp.sum(dns * npv, axis=1, keepdims=True))
                gam = jnp.sum(alpha, axis=0, keepdims=True) + r["a_old"] * inner
                col_o[:, h:h + 1] = rowsum + beta - alpha
                col_o[:, 4 + h:5 + h] = alpha
                col_o[:, 8 + h:9 + h] = jnp.broadcast_to(gam, (LC, 1))
                row_o[0, h:h + 1, :] = colsum
                wq = qf32 * w
                dc_sc[idx] = r["a_old"] * dcs + _dot_tn(wq, dnum)
                dn_sc[idx] = r["a_old"] * dns + jnp.sum(wq * dden, axis=0, keepdims=True)

    def dspecs(cm):
        return [pl.BlockSpec((LC, D), lambda t: (cm(t), 0)),
                pl.BlockSpec((LC, D), lambda t: (cm(t), 0)),
                pl.BlockSpec((LC, D), lambda t: (cm(t), CB_V)),
                pl.BlockSpec((LC, LANE), lambda t: (cm(t), 0)),
                pl.BlockSpec((1, 16, LC), lambda t: (cm(t), 0, 0)),
                pl.BlockSpec((1, D, DH), lambda t: (cm(t), 0, 0)),
                pl.BlockSpec((1, NH, DH), lambda t: (cm(t), 0, 0)),
                pl.BlockSpec((1, NH, LANE), lambda t: (cm(t), 0, 0)),
                pl.BlockSpec((LC, D), lambda t: (jnp.maximum(cm(t) - ncc, 0), 0))]

    def ospecs(cm):
        return [pl.BlockSpec((LC, D), lambda t: (cm(t), 0)),
                pl.BlockSpec((LC, D), lambda t: (cm(t), 0)),
                pl.BlockSpec((LC, D), lambda t: (cm(t), 0)),
                pl.BlockSpec((LC, LANE), lambda t: (cm(t), 0)),
                pl.BlockSpec((1, 8, LC), lambda t: (cm(t), 0, 0))]

    oshape = [jax.ShapeDtypeStruct((tp, D), BF16)] * 3 + [jax.ShapeDtypeStruct((tp, LANE), F32),
                                                        jax.ShapeDtypeStruct((nc, 8, LC), F32)]
    return pl.pallas_call(
        body, name="mlstm_bwd", grid=(nc,),
        in_specs=dspecs(cf) + dspecs(cb),
        out_specs=ospecs(cf) + ospecs(cb),
        out_shape=oshape + oshape,
        scratch_shapes=[pltpu.VMEM((2 * NH, DH, DH), F32), pltpu.VMEM((2 * NH, 1, DH), F32)],
        compiler_params=_cparams(("arbitrary",)),
    )(qa, ka, zmain, gcol, grow, csf, nsf, msf, dhm, qa, ka, zmain, gcol, grow, csb, nsb, msb, dhm)


def gates_bwd(rg, gam, csi, csf, ali, zgb):
    tp = rg.shape[0]

    def body(rg_ref, gam_ref, csi_ref, csf_ref, ali_ref, zgb_ref, o_ref, db_ref):
        i = pl.program_id(0)

        @pl.when(i == 0)
        def _():
            db_ref[...] = jnp.zeros_like(db_ref)

        lane = lax.broadcasted_iota(jnp.int32, (TR, LANE), 1)
        i_l = jnp.logical_or(lane < 4, jnp.logical_and(lane >= 8, lane < 12))
        f_l = jnp.logical_or(jnp.logical_and(lane >= 4, lane < 8), jnp.logical_and(lane >= 12, lane < 16))
        dbh = jnp.where(f_l, rg_ref[...] - csf_ref[...], 0.0)
        rr = lax.broadcasted_iota(jnp.int32, (TR, TR), 0)
        cc = lax.broadcasted_iota(jnp.int32, (TR, TR), 1)
        same = (rr // LC) == (cc // LC)
        low = jnp.where(jnp.logical_and(same, cc <= rr), 1.0, 0.0).astype(BF16)
        upp = jnp.where(jnp.logical_and(same, cc >= rr), 1.0, 0.0).astype(BF16)
        dlogf = jnp.where(lane < 8, _exact_dot(upp, dbh), _exact_dot(low, dbh)) + gam_ref[...]
        out = (jnp.where(i_l, csi_ref[...] + ali_ref[...], 0.0)
               + jnp.where(f_l, dlogf * _sigmoid(-zgb_ref[...]), 0.0))
        o_ref[...] = out
        db_ref[...] += jnp.sum(out, axis=0, keepdims=True)

    spec = _rb(TR, LANE)
    return pl.pallas_call(
        body, name="gates_bwd", grid=(tp // TR,),
        in_specs=[spec] * 6,
        out_specs=[spec, _bc(1, LANE)],
        out_shape=[jax.ShapeDtypeStruct((tp, LANE), F32), jax.ShapeDtypeStruct((1, LANE), F32)],
        compiler_params=_cparams(("arbitrary",)),
    )(rg, gam, csi, csf, ali, zgb)


def _head_norm(hm, gh):
    xs, rs = [], []
    for h in range(NH):
        seg = hm[:, h * DH:(h + 1) * DH]
        r = lax.rsqrt(jnp.mean(seg * seg, axis=-1, keepdims=True) + EPS)
        xs.append(seg * r)
        rs.append(r)
    xh = jnp.concatenate(xs, axis=1)
    return xh, rs, xh * gh


def _sgu_norm(zvg, ln_g, ln_b):
    vg = _gelu(zvg)
    mu = jnp.mean(vg, axis=-1, keepdims=True)
    vc = vg - mu
    rstd = lax.rsqrt(jnp.mean(vc * vc, axis=-1, keepdims=True) + EPS)
    vhat = vc * rstd
    return vhat, rstd, vhat * ln_g + ln_b


def _sgu_mix(vn, ws_ref, bs_ref):
    rows = []
    for c in range(TR // SCH):
        cols = []
        for g in range(SG):
            blk = vn[c * SCH:(c + 1) * SCH, g * SGD:(g + 1) * SGD]
            cols.append(_dot(ws_ref[g * SCH:(g + 1) * SCH, :], blk) + bs_ref[:, g:g + 1])
        rows.append(jnp.concatenate(cols, axis=1))
    return jnp.concatenate(rows, axis=0)


def mixer_fwd(hf, hb, zmain, gh, ln_g, ln_b, ws, bs_t, n_ctx_tiles):
    t = hf.shape[0] - n_ctx_tiles * TR
    off = n_ctx_tiles

    def body(hf_ref, hb_ref, zo_ref, zu_ref, zv_ref, gh_ref, lg_ref, lb_ref, ws_ref, bs_ref, o_ref):
        hm = hf_ref[...] + hb_ref[...]
        _, _, hn = _head_norm(hm, gh_ref[...])
        o_ref[0] = (_sigmoid(zo_ref[...].astype(F32)) * hn).astype(BF16)
        _, _, vn = _sgu_norm(zv_ref[...].astype(F32), lg_ref[...], lb_ref[...])
        mixed = _sgu_mix(vn, ws_ref, bs_ref)
        o_ref[1] = (_gelu(zu_ref[...].astype(F32)) * mixed).astype(BF16)

    return pl.pallas_call(
        body, name="mixer_fwd", grid=(t // TR,),
        in_specs=[_rb(TR, D, 0, off), _rb(TR, D, 0, off), _rb(TR, D, CB_O, off), _rb(TR, D, CB_U, off),
                  _rb(TR, D, CB_VG, off), _bc(1, D), _bc(1, D), _bc(1, D), _bc(SG * SCH, SCH), _bc(SCH, LANE)],
        out_specs=pl.BlockSpec((2, TR, D), lambda i: (0, i, 0)),
        out_shape=jax.ShapeDtypeStruct((2, t, D), BF16),
        compiler_params=_cparams(("parallel",)),
    )(hf, hb, zmain, zmain, zmain, gh, ln_g, ln_b, ws, bs_t)


def merge_fwd(zmain, pp, n_ctx_tiles):
    t = pp.shape[1]
    off = n_ctx_tiles

    def body(zgm_ref, zgg_ref, pp_ref, o_ref):
        y = (_sigmoid(zgm_ref[...].astype(F32)) * pp_ref[0].astype(F32)
             + _sigmoid(zgg_ref[...].astype(F32)) * pp_ref[1].astype(F32))
        o_ref[...] = y.astype(BF16)

    return pl.pallas_call(
        body, name="merge_fwd", grid=(t // TR,),
        in_specs=[_rb(TR, D, CB_GM, off), _rb(TR, D, CB_GG, off), pl.BlockSpec((2, TR, D), lambda i: (0, i, 0))],
        out_specs=_rb(TR, D),
        out_shape=jax.ShapeDtypeStruct((t, D), BF16),
        compiler_params=_cparams(("parallel",)),
    )(zmain, zmain, pp)


def merge_bwd(zmain, pp, dy, tp, n_ctx_tiles):
    t = dy.shape[0]
    nt = tp // TR
    xrow = lambda i: jnp.maximum(i - n_ctx_tiles, 0)

    def body(zg_ref, pp_ref, dy_ref, dpp_ref, dz_ref):
        i = pl.program_id(1)
        zg = zg_ref[...].astype(F32)
        sg = _sigmoid(zg)
        dyv = dy_ref[...].astype(F32)
        dpp_ref[0] = (dyv * sg).astype(BF16)
        dzv = dyv * pp_ref[0].astype(F32) * sg * (1.0 - sg)
        dz_ref[...] = jnp.where(i >= n_ctx_tiles, dzv, 0.0).astype(BF16)

    return pl.pallas_call(
        body, name="merge_bwd", grid=(2, nt),
        in_specs=[pl.BlockSpec((TR, D), lambda j, i: (i, CB_GM + j)),
                  pl.BlockSpec((1, TR, D), lambda j, i: (j, xrow(i), 0)),
                  pl.BlockSpec((TR, D), lambda j, i: (xrow(i), 0))],
        out_specs=[pl.BlockSpec((1, TR, D), lambda j, i: (j, xrow(i), 0)),
                   pl.BlockSpec((TR, D), lambda j, i: (i, CB_GM + j))],
        out_shape=[jax.ShapeDtypeStruct((2, t, D), BF16), jax.ShapeDtypeStruct((tp, 8 * D), BF16)],
        compiler_params=_cparams(("arbitrary", "arbitrary")),
    )(zmain, pp, dy)


def mixer_bwd(dz, hf, hb, zmain, dyms, gh, ln_g, ln_b, ws, bs_t, n_ctx_tiles):
    tp = hf.shape[0]
    t = tp - n_ctx_tiles * TR
    nt = tp // TR
    xrow = lambda i: jnp.maximum(i - n_ctx_tiles, 0)

    def body(dz_in_ref, hf_ref, hb_ref, zo_ref, zu_ref, zv_ref, dy_ref, gh_ref, lg_ref, lb_ref, ws_ref, bs_ref,
             dz_ref, dhm_ref, dgh_ref, dlg_ref, dlb_ref, dws_ref, dbs_ref):
        del dz_in_ref
        i = pl.program_id(0)

        @pl.when(i == 0)
        def _():
            for ref in (dgh_ref, dlg_ref, dlb_ref, dws_ref, dbs_ref):
                ref[...] = jnp.zeros_like(ref)

        @pl.when(i < n_ctx_tiles)
        def _():
            dz_ref[...] = jnp.zeros_like(dz_ref)

        @pl.when(i >= n_ctx_tiles)
        def _():
            gh_v = gh_ref[...]
            hm = hf_ref[...] + hb_ref[...]
            xh, rs, hn = _head_norm(hm, gh_v)
            zo = zo_ref[...].astype(F32)
            so = _sigmoid(zo)
            dym = dy_ref[0].astype(F32)
            d_zo = dym * hn * so * (1.0 - so)
            d_hn = dym * so
            dgh_ref[...] += jnp.sum(d_hn * xh, axis=0, keepdims=True)
            d_xh = d_hn * gh_v
            segs = []
            for h in range(NH):
                hs = slice(h * DH, (h + 1) * DH)
                dx, xs = d_xh[:, hs], xh[:, hs]
                segs.append(rs[h] * (dx - xs * jnp.mean(dx * xs, axis=-1, keepdims=True)))
            dhm_ref[...] = jnp.concatenate(segs, axis=1)
            zu = zu_ref[...].astype(F32)
            zv = zv_ref[...].astype(F32)
            lg = lg_ref[...]
            vhat, rstd, vn = _sgu_norm(zv, lg, lb_ref[...])
            mixed = _sgu_mix(vn, ws_ref, bs_ref)
            dys = dy_ref[1].astype(F32)
            d_zu = dys * mixed * _dgelu(zu)
            d_mixed = dys * _gelu(zu)
            rows = []
            for c in range(TR // SCH):
                cols = []
                for g in range(SG):
                    rsl, csl = slice(c * SCH, (c + 1) * SCH), slice(g * SGD, (g + 1) * SGD)
                    dm = d_mixed[rsl, csl]
                    cols.append(_dot_tn(ws_ref[g * SCH:(g + 1) * SCH, :], dm))
                    dws_ref[g * SCH:(g + 1) * SCH, :] += _dot_nt(dm, vn[rsl, csl])
                    dbs_ref[:, g:g + 1] += jnp.sum(dm, axis=1, keepdims=True)
                rows.append(jnp.concatenate(cols, axis=1))
            d_vn = jnp.concatenate(rows, axis=0)
            dlg_ref[...] += jnp.sum(d_vn * vhat, axis=0, keepdims=True)
            dlb_ref[...] += jnp.sum(d_vn, axis=0, keepdims=True)
            d_vhat = d_vn * lg
            d_vg = rstd * (d_vhat - jnp.mean(d_vhat, axis=-1, keepdims=True)
                           - vhat * jnp.mean(d_vhat * vhat, axis=-1, keepdims=True))
            d_zv = d_vg * _dgelu(zv)
            dz_ref[...] = jnp.concatenate([d_zo, d_zu, d_zv], axis=1).astype(BF16)

    return pl.pallas_call(
        body, name="mixer_bwd", grid=(nt,),
        in_specs=[pl.BlockSpec(memory_space=pl.ANY), _rb(TR, D), _rb(TR, D), _rb(TR, D, CB_O), _rb(TR, D, CB_U),
                  _rb(TR, D, CB_VG), pl.BlockSpec((2, TR, D), lambda i: (0, xrow(i), 0)),
                  _bc(1, D), _bc(1, D), _bc(1, D), _bc(SG * SCH, SCH), _bc(SCH, LANE)],
        out_specs=[_rb(TR, 3 * D), pl.BlockSpec((TR, D), lambda i: (xrow(i), 0)),
                   _bc(1, D), _bc(1, D), _bc(1, D), _bc(SG * SCH, SCH), _bc(SCH, LANE)],
        out_shape=[jax.ShapeDtypeStruct(dz.shape, BF16), jax.ShapeDtypeStruct((t, D), F32),
                   jax.ShapeDtypeStruct((1, D), F32), jax.ShapeDtypeStruct((1, D), F32),
                   jax.ShapeDtypeStruct((1, D), F32), jax.ShapeDtypeStruct((SG * SCH, SCH), F32),
                   jax.ShapeDtypeStruct((SCH, LANE), F32)],
        input_output_aliases={0: 0},
        compiler_params=_cparams(("arbitrary",)),
    )(dz, hf, hb, zmain, zmain, zmain, dyms, gh, ln_g, ln_b, ws, bs_t)


FCB = DFF // 2
TF = 512


def _ffn_halo(col, t):
    per = TF // GW
    last = t // GW - 1
    prev = pl.BlockSpec((GW, FCB), lambda i, j: (jnp.maximum(i * per - 1, 0), col(j)))
    nxt = pl.BlockSpec((GW, FCB), lambda i, j: (jnp.minimum((i + 1) * per, last), col(j)))
    return prev, nxt


def _conv_taps(ext):
    n = ext.shape[0]
    colid = lax.broadcasted_iota(jnp.int32, (TF, 1), 0) % GW
    views = (pltpu.roll(ext, 1, 0), ext, pltpu.roll(ext, n - 1, 0))
    taps = {}
    for ky in range(3):
        base = GW * ky
        for kx in range(3):
            v = views[kx][base:base + TF]
            if kx == 0:
                v = jnp.where(colid >= 1, v, 0.0)
            elif kx == 2:
                v = jnp.where(colid <= GW - 2, v, 0.0)
            taps[(ky, kx)] = v
    return taps


def _ext(c_ref, p_ref, n_ref, i, nt):
    pr = jnp.where(i == 0, 0.0, p_ref[...].astype(F32))
    nx = jnp.where(i == nt - 1, 0.0, n_ref[...].astype(F32))
    return jnp.concatenate([pr, c_ref[...].astype(F32), nx], axis=0)


def ffn_act_fwd(up, wc):
    t = up.shape[0]
    nt = t // TF
    prev_s, next_s = _ffn_halo(lambda j: j, t)

    def body(a_ref, ap_ref, an_ref, b_ref, w_ref, o_ref):
        i = pl.program_id(0)
        taps = _conv_taps(_ext(a_ref, ap_ref, an_ref, i, nt))
        ac = sum(w_ref[3 * ky + kx:3 * ky + kx + 1, :] * taps[(ky, kx)] for ky in range(3) for kx in range(3))
        o_ref[...] = (_silu(ac) * b_ref[...].astype(F32)).astype(BF16)

    return pl.pallas_call(
        body, name="ffn_act_fwd", grid=(nt, 2),
        in_specs=[pl.BlockSpec((TF, FCB), lambda i, j: (i, j)), prev_s, next_s,
                  pl.BlockSpec((TF, FCB), lambda i, j: (i, 2 + j)), pl.BlockSpec((16, FCB), lambda i, j: (0, j))],
        out_specs=pl.BlockSpec((TF, FCB), lambda i, j: (i, j)),
        out_shape=jax.ShapeDtypeStruct((t, DFF), BF16),
        compiler_params=_cparams(("parallel", "parallel")),
    )(up, up, up, up, wc)


def ffn_act_bwd(up, dact, wc):
    t = up.shape[0]
    nt = t // TF
    prev_s, next_s = _ffn_halo(lambda j: j, t)

    def body(a_ref, ap_ref, an_ref, b_ref, da_ref, w_ref, dup_ref, dac_ref):
        i = pl.program_id(0)
        taps = _conv_taps(_ext(a_ref, ap_ref, an_ref, i, nt))
        ac = sum(w_ref[3 * ky + kx:3 * ky + kx + 1, :] * taps[(ky, kx)] for ky in range(3) for kx in range(3))
        da = da_ref[...].astype(F32)
        dup_ref[...] = (da * _silu(ac)).astype(BF16)
        dac_ref[...] = (da * b_ref[...].astype(F32) * _dsilu(ac)).astype(BF16)

    return pl.pallas_call(
        body, name="ffn_act_bwd", grid=(nt, 2),
        in_specs=[pl.BlockSpec((TF, FCB), lambda i, j: (i, j)), prev_s, next_s,
                  pl.BlockSpec((TF, FCB), lambda i, j: (i, 2 + j)), pl.BlockSpec((TF, FCB), lambda i, j: (i, j)),
                  pl.BlockSpec((16, FCB), lambda i, j: (0, j))],
        out_specs=[pl.BlockSpec((TF, FCB), lambda i, j: (i, 2 + j)), pl.BlockSpec((TF, FCB), lambda i, j: (i, j))],
        out_shape=[jax.ShapeDtypeStruct((t, 2 * DFF), BF16), jax.ShapeDtypeStruct((t, DFF), BF16)],
        compiler_params=_cparams(("parallel", "parallel")),
    )(up, up, up, up, dact, wc)


def ffn_conv_bwd(dup, up, dac, wc):
    t = up.shape[0]
    nt = t // TF
    prev_a, next_a = _ffn_halo(lambda j: j, t)

    def body(dup_in_ref, a_ref, ap_ref, an_ref, g_ref, gp_ref, gn_ref, w_ref, o_ref, dw_ref):
        del dup_in_ref
        i = pl.program_id(1)

        @pl.when(i == 0)
        def _():
            dw_ref[...] = jnp.zeros_like(dw_ref)

        gtaps = _conv_taps(_ext(g_ref, gp_ref, gn_ref, i, nt))
        o_ref[...] = sum(w_ref[3 * (2 - ky) + (2 - kx):3 * (2 - ky) + (2 - kx) + 1, :] * gtaps[(ky, kx)]
                         for ky in range(3) for kx in range(3)).astype(BF16)
        ataps = _conv_taps(_ext(a_ref, ap_ref, an_ref, i, nt))
        g = g_ref[...].astype(F32)
        for ky in range(3):
            for kx in range(3):
                k = 3 * ky + kx
                dw_ref[k:k + 1, :] += jnp.sum(ataps[(ky, kx)] * g, axis=0, keepdims=True)

    sw = lambda s: pl.BlockSpec(s.block_shape, lambda j, i, f=s.index_map: f(i, j))
    return pl.pallas_call(
        body, name="ffn_conv_bwd", grid=(2, nt),
        in_specs=[pl.BlockSpec(memory_space=pl.ANY),
                  pl.BlockSpec((TF, FCB), lambda j, i: (i, j)), sw(prev_a), sw(next_a),
                  pl.BlockSpec((TF, FCB), lambda j, i: (i, j)), sw(prev_a), sw(next_a),
                  pl.BlockSpec((16, FCB), lambda j, i: (0, j))],
        out_specs=[pl.BlockSpec((TF, FCB), lambda j, i: (i, j)), pl.BlockSpec((16, FCB), lambda j, i: (0, j))],
        out_shape=[jax.ShapeDtypeStruct(dup.shape, BF16), jax.ShapeDtypeStruct((16, DFF), F32)],
        input_output_aliases={0: 0},
        compiler_params=_cparams(("arbitrary", "arbitrary")),
    )(dup, up, up, up, dac, dac, dac, wc)


def head_fwd_bwd(h1, f, target, gfin, tab):
    t = h1.shape[0]

    def body(h1_ref, f_ref, t_ref, g_ref, tab_ref, dh2_ref, df_ref, acc_ref):
        i = pl.program_id(0)

        @pl.when(i == 0)
        def _():
            acc_ref[...] = jnp.zeros_like(acc_ref)

        gate = tab_ref[0:1, :]
        fv = f_ref[...]
        h2 = h1_ref[...] + gate * fv
        r = lax.rsqrt(jnp.mean(h2 * h2, axis=-1, keepdims=True) + EPS)
        xh = h2 * r
        gv = g_ref[...]
        err = xh * gv - t_ref[...]
        acc_ref[0:1, :] += jnp.sum(0.5 * jnp.mean(err * err, axis=-1, keepdims=True), axis=0, keepdims=True)
        dy = err * (1.0 / D)
        acc_ref[1:2, :] += jnp.sum(dy * xh, axis=0, keepdims=True)
        dxh = dy * gv
        dh2 = r * (dxh - xh * jnp.mean(dxh * xh, axis=-1, keepdims=True))
        dh2_ref[...] = dh2
        acc_ref[2:3, :] += jnp.sum(dh2 * fv, axis=0, keepdims=True)
        df_ref[...] = (dh2 * gate).astype(BF16)

    return pl.pallas_call(
        body, name="head_fwd_bwd", grid=(t // TR,),
        in_specs=[_rb(TR, D), _rb(TR, D), _rb(TR, D), _bc(1, D), _bc(8, D)],
        out_specs=[_rb(TR, D), _rb(TR, D), _bc(8, D)],
        out_shape=[jax.ShapeDtypeStruct((t, D), F32), jax.ShapeDtypeStruct((t, D), BF16),
                   jax.ShapeDtypeStruct((8, D), F32)],
        compiler_params=_cparams(("arbitrary",)),
    )(h1, f, target, gfin, tab)


def norm2_bwd(h1, dhn2, dh2, out, g, tab):
    t = h1.shape[0]

    def body(h1_ref, dhn_ref, dh2_ref, out_ref, g_ref, tab_ref, dh1_ref, dout_ref, acc_ref):
        i = pl.program_id(0)

        @pl.when(i == 0)
        def _():
            acc_ref[...] = jnp.zeros_like(acc_ref)

        h1v = h1_ref[...]
        r = lax.rsqrt(jnp.mean(h1v * h1v, axis=-1, keepdims=True) + EPS)
        xh = h1v * r
        gv = g_ref[...]
        dhn = dhn_ref[...]
        acc_ref[0:1, :] += jnp.sum(dhn, axis=0, keepdims=True)
        acc_ref[1:2, :] += jnp.sum(dhn * xh * gv, axis=0, keepdims=True)
        dn = dhn * (1.0 + tab_ref[2:3, :])
        acc_ref[2:3, :] += jnp.sum(dn * xh, axis=0, keepdims=True)
        dxh = dn * gv
        dh1 = dh2_ref[...] + r * (dxh - xh * jnp.mean(dxh * xh, axis=-1, keepdims=True))
        dh1_ref[...] = dh1
        acc_ref[3:4, :] += jnp.sum(dh1 * out_ref[...], axis=0, keepdims=True)
        dout_ref[...] = (dh1 * tab_ref[0:1, :]).astype(BF16)

    return pl.pallas_call(
        body, name="norm2_bwd", grid=(t // TR,),
        in_specs=[_rb(TR, D), _rb(TR, D), _rb(TR, D), _rb(TR, D), _bc(1, D), _bc(8, D)],
        out_specs=[_rb(TR, D), _rb(TR, D), _bc(8, D)],
        out_shape=[jax.ShapeDtypeStruct((t, D), F32), jax.ShapeDtypeStruct((t, D), BF16),
                   jax.ShapeDtypeStruct((8, D), F32)],
        compiler_params=_cparams(("arbitrary",)),
    )(h1, dhn2, dh2, out, g, tab)


def norm1_bwd(xcat, da, db, dh1, g, tab, n_ctx_tiles):
    tp = xcat.shape[0]
    t = tp - n_ctx_tiles * TR
    xrow = lambda i: jnp.maximum(i - n_ctx_tiles, 0)

    def body(x_ref, da_ref, db_ref, dh1_ref, g_ref, tab_ref, dx_ref, acc_ref):
        i = pl.program_id(0)

        @pl.when(i == 0)
        def _():
            acc_ref[...] = jnp.zeros_like(acc_ref)

        x = x_ref[...]
        r = lax.rsqrt(jnp.mean(x * x, axis=-1, keepdims=True) + EPS)
        xh = x * r
        gv = g_ref[...]
        dhn = da_ref[...] + db_ref[...]
        s_shift = jnp.sum(dhn, axis=0, keepdims=True)
        s_scale = jnp.sum(dhn * xh * gv, axis=0, keepdims=True)
        is_ctx = i < n_ctx_tiles

        @pl.when(is_ctx)
        def _():
            acc_ref[0:1, :] += s_shift
            acc_ref[1:2, :] += s_scale

        @pl.when(jnp.logical_not(is_ctx))
        def _():
            acc_ref[2:3, :] += s_shift
            acc_ref[3:4, :] += s_scale

        acc_ref[5:6, :] += s_shift
        acc_ref[6:7, :] += s_scale
        sc = jnp.where(is_ctx, tab_ref[1:2, :], tab_ref[3:4, :])
        dn = dhn * (1.0 + sc)
        acc_ref[4:5, :] += jnp.sum(dn * xh, axis=0, keepdims=True)
        dxh = dn * gv
        dx_ref[...] = dh1_ref[...] + r * (dxh - xh * jnp.mean(dxh * xh, axis=-1, keepdims=True))

    return pl.pallas_call(
        body, name="norm1_bwd", grid=(tp // TR,),
        in_specs=[_rb(TR, D), _rb(TR, D), _rb(TR, D), pl.BlockSpec((TR, D), lambda i: (xrow(i), 0)),
                  _bc(1, D), _bc(8, D)],
        out_specs=[pl.BlockSpec((TR, D), lambda i: (xrow(i), 0)), _bc(8, D)],
        out_shape=[jax.ShapeDtypeStruct((t, D), F32), jax.ShapeDtypeStruct((8, D), F32)],
        compiler_params=_cparams(("arbitrary",)),
    )(xcat, da, db, dh1, g, tab)


def adamw(w, g, m, v, name):
    rows, cols = w.shape
    tm = _pick(rows, (256, 176, 128, 64, 8))
    c1 = 1.0 / (1.0 - ADAM_B1 ** ADAM_STEP)
    c2 = 1.0 / (1.0 - ADAM_B2 ** ADAM_STEP)

    def body(w_ref, g_ref, m_ref, v_ref, d_ref, mo_ref, vo_ref):
        gv = g_ref[...]
        mn = ADAM_B1 * m_ref[...] + (1.0 - ADAM_B1) * gv
        vn = ADAM_B2 * v_ref[...] + (1.0 - ADAM_B2) * (gv * gv)
        mo_ref[...] = mn
        vo_ref[...] = vn
        d_ref[...] = -ADAM_LR * ((mn * c1) / (jnp.sqrt(vn * c2) + ADAM_EPS) + ADAM_WD * w_ref[...])

    spec = pl.BlockSpec((tm, cols), lambda i: (i, 0))
    sds = jax.ShapeDtypeStruct((rows, cols), F32)
    return pl.pallas_call(
        body, name=name, grid=(rows // tm,),
        in_specs=[spec] * 4, out_specs=[spec] * 3, out_shape=[sds] * 3,
        compiler_params=_cparams(("parallel",)),
    )(w, g, m, v)


def add_halves(a, b, name):
    rows, cols = a.shape
    tm = _pick(rows, (512, 256, 128, 64, 16))

    def body(a_ref, b_ref, o_ref, ob_ref):
        s = a_ref[...].astype(F32) + b_ref[...].astype(F32)
        o_ref[...] = s
        ob_ref[...] = s.astype(BF16)

    spec = pl.BlockSpec((tm, cols), lambda i: (i, 0))
    return pl.pallas_call(
        body, name=name, grid=(rows // tm,),
        in_specs=[spec] * 2, out_specs=[spec] * 2,
        out_shape=[jax.ShapeDtypeStruct((rows, cols), F32), jax.ShapeDtypeStruct((rows, cols), BF16)],
        compiler_params=_cparams(("parallel",)),
    )(a, b)


def add4(a, r0, r1, r2, name):
    rows, cols = a.shape
    tm = _pick(rows, (512, 256, 128, 64, 16))

    def body(a_ref, r0_ref, r1_ref, r2_ref, o_ref):
        o_ref[...] = ((a_ref[...] + r0_ref[...].astype(F32)) + r1_ref[...].astype(F32)) + r2_ref[...].astype(F32)

    spec = pl.BlockSpec((tm, cols), lambda i: (i, 0))
    return pl.pallas_call(
        body, name=name, grid=(rows // tm,),
        in_specs=[spec] * 4, out_specs=spec, out_shape=jax.ShapeDtypeStruct((rows, cols), F32),
        compiler_params=_cparams(("parallel",)),
    )(a, r0, r1, r2)


def sum8(stack, name):
    _, rows, cols = stack.shape
    tm = _pick(rows, (512, 256, 128, 64, 8))

    def body(s_ref, o_ref):
        acc = s_ref[0]
        for k in range(1, 8):
            acc = acc + s_ref[k]
        o_ref[...] = acc

    return pl.pallas_call(
        body, name=name, grid=(rows // tm,),
        in_specs=[pl.BlockSpec((8, tm, cols), lambda i: (0, i, 0))],
        out_specs=pl.BlockSpec((tm, cols), lambda i: (i, 0)),
        out_shape=jax.ShapeDtypeStruct((rows, cols), F32),
        compiler_params=_cparams(("parallel",)),
    )(stack)


def _coords():
    return lax.axis_index("x"), lax.axis_index("y"), lax.axis_index("c")


def _other_chips(x, y):
    return [(1 - x, y), (x, 1 - y), (1 - x, 1 - y)]


_ANY = pl.BlockSpec(memory_space=pl.ANY)


def gather_chips(slab):
    r, wd = slab.shape
    rh = r // 2

    def body(x_ref, out_ref, send_sems, recv_sems, local_sem):
        x, y, c = _coords()
        me = 2 * x + y
        sibling = (x, y, 1 - c)
        chips = _other_chips(x, y)

        def half(chip, hc):
            return out_ref.at[chip, pl.ds(hc * rh, rh), :]

        mine = pltpu.make_async_copy(x_ref, out_ref.at[me], local_sem)
        mine.start()
        first = []
        for j, (px, py) in enumerate(chips):
            cp = pltpu.make_async_remote_copy(
                src_ref=x_ref.at[pl.ds(c * rh, rh), :], dst_ref=half(me, c),
                send_sem=send_sems.at[j], recv_sem=recv_sems.at[j], device_id=(px, py, c), device_id_type=MESH)
            cp.start()
            first.append(cp)
        passed = []
        for j, (px, py) in enumerate(chips):
            src = 2 * px + py
            landed = pltpu.make_async_remote_copy(
                src_ref=half(src, c), dst_ref=half(src, c),
                send_sem=send_sems.at[j], recv_sem=recv_sems.at[j], device_id=(px, py, c), device_id_type=MESH)
            landed.wait_recv()
            fw = pltpu.make_async_remote_copy(
                src_ref=half(src, c), dst_ref=half(src, c),
                send_sem=send_sems.at[3 + j], recv_sem=recv_sems.at[3 + j], device_id=sibling, device_id_type=MESH)
            fw.start()
            passed.append(fw)
        for j, (px, py) in enumerate(chips):
            src = 2 * px + py
            got = pltpu.make_async_remote_copy(
                src_ref=half(src, 1 - c), dst_ref=half(src, 1 - c),
                send_sem=send_sems.at[3 + j], recv_sem=recv_sems.at[3 + j], device_id=sibling, device_id_type=MESH)
            got.wait_recv()
        for cp in first + passed:
            cp.wait_send()
        mine.wait()

    return pl.pallas_call(
        body, name="gather_chips",
        in_specs=[_ANY], out_specs=_ANY,
        out_shape=jax.ShapeDtypeStruct((4, r, wd), slab.dtype),
        scratch_shapes=[pltpu.SemaphoreType.DMA((6,)), pltpu.SemaphoreType.DMA((6,)), pltpu.SemaphoreType.DMA],
    )(slab)


def swap_halves(gs):
    _, r, wd = gs.shape
    rh = r // 2

    def body(g_ref, out_ref, send_sem, recv_sem):
        x, y, c = _coords()
        cp = pltpu.make_async_remote_copy(
            src_ref=g_ref.at[:, pl.ds((1 - c) * rh, rh), :], dst_ref=out_ref,
            send_sem=send_sem, recv_sem=recv_sem, device_id=(x, y, 1 - c), device_id_type=MESH)
        cp.start()
        cp.wait()

    return pl.pallas_call(
        body, name="swap_halves",
        in_specs=[_ANY], out_specs=_ANY,
        out_shape=jax.ShapeDtypeStruct((4, rh, wd), gs.dtype),
        scratch_shapes=[pltpu.SemaphoreType.DMA, pltpu.SemaphoreType.DMA],
    )(gs)


def scatter_chips(pb):
    _, rh, wd = pb.shape

    def body(p_ref, out_ref, send_sems, recv_sems):
        x, y, c = _coords()
        cps = []
        for j, (px, py) in enumerate(_other_chips(x, y)):
            cp = pltpu.make_async_remote_copy(
                src_ref=p_ref.at[2 * px + py], dst_ref=out_ref.at[j],
                send_sem=send_sems.at[j], recv_sem=recv_sems.at[j], device_id=(px, py, c), device_id_type=MESH)
            cp.start()
            cps.append(cp)
        for cp in cps:
            cp.wait()

    return pl.pallas_call(
        body, name="scatter_chips",
        in_specs=[_ANY], out_specs=_ANY,
        out_shape=jax.ShapeDtypeStruct((3, rh, wd), pb.dtype),
        scratch_shapes=[pltpu.SemaphoreType.DMA((3,)), pltpu.SemaphoreType.DMA((3,))],
    )(pb)


def join_halves(red):
    rh, wd = red.shape

    def body(r_ref, out_ref, send_sem, recv_sem, local_sem):
        x, y, c = _coords()
        mine = pltpu.make_async_copy(r_ref, out_ref.at[c], local_sem)
        mine.start()
        cp = pltpu.make_async_remote_copy(
            src_ref=r_ref, dst_ref=out_ref.at[c],
            send_sem=send_sem, recv_sem=recv_sem, device_id=(x, y, 1 - c), device_id_type=MESH)
        cp.start()
        cp.wait_send()
        got = pltpu.make_async_remote_copy(
            src_ref=r_ref, dst_ref=out_ref.at[1 - c],
            send_sem=send_sem, recv_sem=recv_sem, device_id=(x, y, 1 - c), device_id_type=MESH)
        got.wait_recv()
        mine.wait()

    return pl.pallas_call(
        body, name="join_halves",
        in_specs=[_ANY], out_specs=_ANY,
        out_shape=jax.ShapeDtypeStruct((2, rh, wd), red.dtype),
        scratch_shapes=[pltpu.SemaphoreType.DMA, pltpu.SemaphoreType.DMA, pltpu.SemaphoreType.DMA],
    )(red)


def gather_all(vec):
    r, wd = vec.shape

    def body(v_ref, out_ref, send_sems, recv_sems, local_sem):
        x, y, c = _coords()
        me = 4 * x + 2 * y + c
        mine = pltpu.make_async_copy(v_ref, out_ref.at[me], local_sem)
        mine.start()
        cps = []
        for k in range(1, 8):
            mx, my, mc = (k >> 2) & 1, (k >> 1) & 1, k & 1
            peer = (x ^ mx, y ^ my, c ^ mc)
            cp = pltpu.make_async_remote_copy(
                src_ref=v_ref, dst_ref=out_ref.at[me],
                send_sem=send_sems.at[k - 1], recv_sem=recv_sems.at[k - 1], device_id=peer, device_id_type=MESH)
            cp.start()
            cps.append(cp)
        for k in range(1, 8):
            mx, my, mc = (k >> 2) & 1, (k >> 1) & 1, k & 1
            peer = (x ^ mx, y ^ my, c ^ mc)
            src = 4 * peer[0] + 2 * peer[1] + peer[2]
            got = pltpu.make_async_remote_copy(
                src_ref=v_ref, dst_ref=out_ref.at[src],
                send_sem=send_sems.at[k - 1], recv_sem=recv_sems.at[k - 1], device_id=peer, device_id_type=MESH)
            got.wait_recv()
        for cp in cps:
            cp.wait_send()
        mine.wait()

    return pl.pallas_call(
        body, name="gather_all",
        in_specs=[_ANY], out_specs=_ANY,
        out_shape=jax.ShapeDtypeStruct((8, r, wd), vec.dtype),
        scratch_shapes=[pltpu.SemaphoreType.DMA((7,)), pltpu.SemaphoreType.DMA((7,)), pltpu.SemaphoreType.DMA],
    )(vec)


def _pad_rows(a, rows):
    return jnp.pad(a, ((0, rows - a.shape[0]), (0, 0)))


def _pad_cols(a, cols):
    return jnp.pad(a, ((0, 0), (0, cols - a.shape[1])))


def local_step(x, c, ctx, c_ctx, target, wt, sm):
    t, tc = x.shape[0], ctx.shape[0]
    tp = t + tc
    nct = tc // TR
    ncc = tc // LC
    nc = tp // LC

    w_in = wt["w_in"]
    segs = {"q": (0, D), "k": (D, 2 * D), "v": (2 * D, 3 * D), "g": (3 * D, 3 * D + NGATE)}
    base = 3 * D + NGATE
    for n_i, nm in enumerate(("o", "u", "vg", "gm", "gg")):
        segs[nm] = (base + n_i * D, base + (n_i + 1) * D)
    order = ("o", "u", "vg", "gm", "gg", "v", "q", "k")
    w_main = jnp.concatenate([w_in[:, segs[nm][0]:segs[nm][1]] for nm in order], axis=1)
    w_g = _pad_cols(w_in[:, segs["g"][0]:segs["g"][1]], LANE)
    w_main_t = w_main.T
    w_g_t = w_g.T

    cc = _pad_rows(jnp.concatenate([c.reshape(1, D), c_ctx.reshape(1, D)], axis=0), 16)
    modv = mod_fwd(cc, wt["w_mod"], sm["b_mod"].reshape(1, NMOD * D))
    mx = modv[0].reshape(NMOD, D)
    mc = modv[1].reshape(NMOD, D)
    tab1 = _pad_rows(jnp.stack([mc[0], mc[1], mx[0], mx[1]]), 8)
    tab2 = _pad_rows(jnp.stack([mx[2], mx[3], mx[4]]), 8)
    tab3 = _pad_rows(mx[5:6], 8)

    g1 = sm["norm1_g"].reshape(1, D)
    g2 = sm["norm2_g"].reshape(1, D)
    gfin = sm["final_g"].reshape(1, D)
    gh = sm["head_norm_g"].reshape(1, D)
    ln_g = sm["sgu_ln_g"].reshape(1, D)
    ln_b = sm["sgu_ln_b"].reshape(1, D)
    ws = sm["w_s"].reshape(SG * SCH, SCH).astype(BF16)
    bs_t = _pad_cols(sm["b_s"].reshape(SG, SCH).T, LANE)
    conv_w = _pad_rows(sm["conv_qk"].reshape(3, 2 * D), 8)
    b_gate = _pad_cols(sm["b_gate"].reshape(1, NGATE), LANE)
    wc = _pad_rows(sm["w_ffn_conv"].reshape(9, DFF), 16)

    xcat = jnp.concatenate([ctx, x], axis=0)
    hn1 = norm1_fwd(xcat, g1, tab1, nct)
    zmain = mm_nn(hn1, w_main, BF16, "mm_zmain")
    zg = mm_nn(hn1, w_g, F32, "mm_zg")
    qa, ka, gcol, zgb = qkconv_fwd(zmain, zg, conv_w, b_gate, nct)
    grow = gcol[:, :16].reshape(nc, LC, 16).transpose(0, 2, 1)
    hf, hb, csf, csb, nsf, nsb, msf, msb = mlstm_fwd(qa, ka, zmain, gcol, grow, ncc)
    yms = mixer_fwd(hf, hb, zmain, gh, ln_g, ln_b, ws, bs_t, nct)
    w_br = jnp.stack([wt["w_branch_mlstm"], wt["w_branch_sgu"]])
    pp = mm_nn(yms, w_br, BF16, "mm_branch")
    y = merge_fwd(zmain, pp, nct)
    out = mm_nn(y, wt["w_out"], F32, "mm_out")
    h1, hn2 = norm2_fwd(x, out, g2, tab2)
    up = mm_nn(hn2, wt["w_up"], BF16, "mm_up")
    act = ffn_act_fwd(up, wc)
    f = mm_nn(act, wt["w_down"], F32, "mm_down")
    dh2, df, acc_h = head_fwd_bwd(h1, f, target, gfin, tab3)
    loss = acc_h[0, 0]

    g_w_down = mm_tn(act, df, "mmt_down")
    dact = mm_nn(df, wt["w_down"].T, BF16, "mm_ddown")
    dup, dac = ffn_act_bwd(up, dact, wc)
    dup, g_wc = ffn_conv_bwd(dup, up, dac, wc)
    g_w_up = mm_tn(hn2, dup, "mmt_up")
    dhn2 = mm_nn(dup, wt["w_up"].T, F32, "mm_dup")
    dh1, dout, acc_2 = norm2_bwd(h1, dhn2, dh2, out, g2, tab2)
    g_w_out = mm_tn(y, dout, "mmt_out")
    dy = mm_nn(dout, wt["w_out"].T, BF16, "mm_dout")
    dpp, dz = merge_bwd(zmain, pp, dy, tp, nct)
    g_w_br = mm_tn(yms, dpp, "mmt_branch")
    dyms = mm_nn(dpp, jnp.stack([wt["w_branch_mlstm"].T, wt["w_branch_sgu"].T]), BF16, "mm_dbranch")
    dz, dhm, g_gh, g_lng, g_lnb, g_ws, g_bs = mixer_bwd(dz, hf, hb, zmain, dyms, gh, ln_g, ln_b, ws, bs_t, nct)
    (dqf, dkf, dvf, colf, rowf, dqb, dkb, dvb, colb, rowb) = mlstm_bwd(
        qa, ka, zmain, gcol, grow, (csf, csb, nsf, nsb, msf, msb), dhm, ncc)

    csum_f = rowf[:, :4, :].transpose(0, 2, 1).reshape(tp, 4)
    csum_b = rowb[:, :4, :].transpose(0, 2, 1).reshape(tp, 4)
    z4 = jnp.zeros((tp, 4), F32)

    def lanes(parts):
        return _pad_cols(jnp.concatenate(parts, axis=1), LANE)

    rg = lanes([z4, colf[:, 0:4], z4, colb[:, 0:4]])
    csi = lanes([csum_f, z4, csum_b, z4])
    csf_l = lanes([z4, csum_f, z4, csum_b])
    ali = lanes([colf[:, 4:8], z4, colb[:, 4:8], z4])
    gam = lanes([z4, colf[:, 8:12], z4, colb[:, 8:12]])
    dzg, g_bgate = gates_bwd(rg, gam, csi, csf_l, ali, zgb)

    dc, g_convw = qkconv_bwd_a(zmain, dqf, dqb, dkf, dkb, conv_w, nct)
    dz = qkconv_bwd_b(dz, dc, conv_w, nct)
    dz = add_into_dz(dz, dvf, dvb, CB_V)

    g_w_main = mm_tn(hn1, dz, "mmt_main")
    g_w_g = mm_tn(hn1, dzg, "mmt_g")
    da = mm_nn(dz, w_main_t, F32, "mm_dmain")
    db = mm_nn(dzg, w_g_t, F32, "mm_dg")
    grad_x, acc_1 = norm1_bwd(xcat, da, db, dh1, g1, tab1, nct)

    blk = lambda cb: g_w_main[:, cb * D:(cb + 1) * D]
    g_w_in = jnp.concatenate([blk(CB_Q), blk(CB_K), blk(CB_V), g_w_g[:, :NGATE], blk(CB_O), blk(CB_U), blk(CB_VG),
                              blk(CB_GM), blk(CB_GG)], axis=1)

    d_modx = jnp.concatenate([acc_1[2], acc_1[3], acc_2[3], acc_2[0], acc_2[1], acc_h[2]])
    d_modc = jnp.concatenate([acc_1[0], acc_1[1], jnp.zeros((4 * D,), F32)])
    d_modb = jnp.concatenate([acc_1[5], acc_1[6], acc_2[3], acc_2[0], acc_2[1], acc_h[2]])

    big = {"w_in": g_w_in, "w_branch_mlstm": g_w_br[0], "w_branch_sgu": g_w_br[1], "w_out": g_w_out,
           "w_up": g_w_up, "w_down": g_w_down}
    small = {"b_mod": d_modb, "norm1_g": acc_1[4], "b_gate": g_bgate[0, :NGATE], "conv_qk": g_convw[:3].reshape(-1),
             "head_norm_g": g_gh[0], "sgu_ln_g": g_lng[0], "sgu_ln_b": g_lnb[0], "w_s": g_ws.reshape(-1),
             "b_s": g_bs[:, :SG].T.reshape(-1), "norm2_g": acc_2[2], "w_ffn_conv": g_wc[:9].reshape(-1),
             "final_g": acc_h[1]}
    return loss, grad_x, big, small, d_modx, d_modc


def mod_bwd_w(a_all, dm_all, name):
    n = dm_all.shape[1]
    tn = _pick(n, (512, 128))

    def body(a_ref, d_ref, o_ref):
        o_ref[...] = _dot_tn(_silu(a_ref[...]), d_ref[...])

    return pl.pallas_call(
        body, name=name, grid=(n // tn,),
        in_specs=[_bc(16, D), pl.BlockSpec((16, tn), lambda j: (0, j))],
        out_specs=pl.BlockSpec((D, tn), lambda j: (0, j)),
        out_shape=jax.ShapeDtypeStruct((D, n), F32),
        compiler_params=_cparams(("parallel",)),
    )(a_all, dm_all)


def mod_bwd_cctx(dmc, w_mod_t, c_ctx):
    def body(d_ref, w_ref, c_ref, o_ref):
        o_ref[...] = _dot(d_ref[...], w_ref[...]) * _dsilu(c_ref[...])

    return pl.pallas_call(
        body, name="mod_bwd_cctx", grid=(1,),
        in_specs=[_bc(16, 2 * D), _bc(2 * D, D), _bc(1, D)],
        out_specs=_bc(16, D),
        out_shape=jax.ShapeDtypeStruct((16, D), F32),
        compiler_params=_cparams(("arbitrary",)),
    )(dmc, w_mod_t, c_ctx)


BIG = ("w_mod", "w_in", "w_branch_mlstm", "w_branch_sgu", "w_out", "w_up", "w_down")
BIG_AXIS = {"w_mod": 1, "w_in": 1, "w_branch_mlstm": 0, "w_branch_sgu": 0, "w_out": 0, "w_up": 1, "w_down": 0}
SMALL = ("c_ctx", "b_mod", "norm1_g", "b_gate", "conv_qk", "head_norm_g", "sgu_ln_g", "sgu_ln_b", "w_s", "b_s",
         "norm2_g", "w_ffn_conv", "final_g")
SMALL_SHARDED = {"conv_qk": (3, 2 * D), "w_ffn_conv": (9, DFF)}
PACK_ALIGN = 32 * D


def _pack(arrs, dtype, align=PACK_ALIGN, width=D):
    flat = jnp.concatenate([a.reshape(-1).astype(dtype) for a in arrs])
    n = flat.shape[0]
    padded = -(-n // align) * align
    return jnp.pad(flat, (0, padded - n)).reshape(padded // width, width)


def _unpack(slab, shapes):
    flat = slab.reshape(-1)
    outs, off = [], 0
    for shp in shapes:
        n = math.prod(shp)
        outs.append(flat[off:off + n].reshape(shp))
        off += n
    return outs


def _split_hi_lo(a):
    hi = a.astype(BF16)
    lo = (a - hi.astype(F32)).astype(BF16)
    return hi, lo


def kernel(x, c, ctx, c_ctx, w_mod, b_mod, norm1_g, w_in, b_gate, conv_qk, head_norm_g, sgu_ln_g, sgu_ln_b, w_s, b_s, w_branch_mlstm, w_branch_sgu, w_out, norm2_g, w_up, w_ffn_conv, w_down, final_g, loss_target, m_c_ctx, m_w_mod, m_b_mod, m_norm1_g, m_w_in, m_b_gate, m_conv_qk, m_head_norm_g, m_sgu_ln_g, m_sgu_ln_b, m_w_s, m_b_s, m_w_branch_mlstm, m_w_branch_sgu, m_w_out, m_norm2_g, m_w_up, m_w_ffn_conv, m_w_down, m_final_g, v_c_ctx, v_w_mod, v_b_mod, v_norm1_g, v_w_in, v_b_gate, v_conv_qk, v_head_norm_g, v_sgu_ln_g, v_sgu_ln_b, v_w_s, v_b_s, v_w_branch_mlstm, v_w_branch_sgu, v_w_out, v_norm2_g, v_w_up, v_w_ffn_conv, v_w_down, v_final_g):
    params = dict(c_ctx=c_ctx, w_mod=w_mod, b_mod=b_mod, norm1_g=norm1_g, w_in=w_in, b_gate=b_gate, conv_qk=conv_qk,
                  head_norm_g=head_norm_g, sgu_ln_g=sgu_ln_g, sgu_ln_b=sgu_ln_b, w_s=w_s, b_s=b_s,
                  w_branch_mlstm=w_branch_mlstm, w_branch_sgu=w_branch_sgu, w_out=w_out, norm2_g=norm2_g, w_up=w_up,
                  w_ffn_conv=w_ffn_conv, w_down=w_down, final_g=final_g)
    mom_m = dict(c_ctx=m_c_ctx, w_mod=m_w_mod, b_mod=m_b_mod, norm1_g=m_norm1_g, w_in=m_w_in, b_gate=m_b_gate,
                 conv_qk=m_conv_qk, head_norm_g=m_head_norm_g, sgu_ln_g=m_sgu_ln_g, sgu_ln_b=m_sgu_ln_b, w_s=m_w_s,
                 b_s=m_b_s, w_branch_mlstm=m_w_branch_mlstm, w_branch_sgu=m_w_branch_sgu, w_out=m_w_out,
                 norm2_g=m_norm2_g, w_up=m_w_up, w_ffn_conv=m_w_ffn_conv, w_down=m_w_down, final_g=m_final_g)
    mom_v = dict(c_ctx=v_c_ctx, w_mod=v_w_mod, b_mod=v_b_mod, norm1_g=v_norm1_g, w_in=v_w_in, b_gate=v_b_gate,
                 conv_qk=v_conv_qk, head_norm_g=v_head_norm_g, sgu_ln_g=v_sgu_ln_g, sgu_ln_b=v_sgu_ln_b, w_s=v_w_s,
                 b_s=v_b_s, w_branch_mlstm=v_w_branch_mlstm, w_branch_sgu=v_w_branch_sgu, w_out=v_w_out,
                 norm2_g=v_norm2_g, w_up=v_w_up, w_ffn_conv=v_w_ffn_conv, w_down=v_w_down, final_g=v_final_g)
    chip = 2 * lax.axis_index("x") + lax.axis_index("y")

    shard2d = {n: params[n].reshape(params[n].shape[-2:]) for n in BIG}
    conv_sh = conv_qk.reshape(3, -1)
    fconv_sh = w_ffn_conv.reshape(9, -1)

    c_hi, c_lo = _split_hi_lo(conv_sh)
    f_hi, f_lo = _split_hi_lo(fconv_sh)
    send = [shard2d[n] for n in BIG] + [c_hi, c_lo, f_hi, f_lo]
    shapes = [a.shape for a in send]
    slab = _pack(send, BF16)
    full = gather_chips(slab)
    parts = [_unpack(full[j], shapes) for j in range(4)]
    wt = {}
    for k, n in enumerate(BIG):
        wt[n] = jnp.concatenate([parts[j][k] for j in range(4)], axis=BIG_AXIS[n])
    nb = len(BIG)
    conv_full = jnp.concatenate([parts[j][nb].astype(F32) + parts[j][nb + 1].astype(F32) for j in range(4)], axis=1)
    fconv_full = jnp.concatenate([parts[j][nb + 2].astype(F32) + parts[j][nb + 3].astype(F32) for j in range(4)],
                                 axis=1)

    sm = dict(b_mod=b_mod, norm1_g=norm1_g, b_gate=b_gate, conv_qk=conv_full, head_norm_g=head_norm_g,
              sgu_ln_g=sgu_ln_g, sgu_ln_b=sgu_ln_b, w_s=w_s, b_s=b_s, norm2_g=norm2_g, w_ffn_conv=fconv_full,
              final_g=final_g)

    loss_l, grad_x, gbig, gsmall, d_modx, d_modc = local_step(
        x[0], c, ctx[0], c_ctx, loss_target[0], wt, sm)

    rs_names = BIG[1:]
    chunks = []
    for j in range(4):
        pieces = []
        for n in rs_names:
            g = gbig[n]
            size = g.shape[BIG_AXIS[n]] // 4
            pieces.append(lax.slice_in_dim(g, j * size, (j + 1) * size, axis=BIG_AXIS[n]))
        chunks.append(_pack(pieces, BF16))
    gs = jnp.stack(chunks)
    r_rows = gs.shape[1]
    rh = r_rows // 2
    cidx = lax.axis_index("c")
    from_sib = swap_halves(gs)
    my_half = lax.dynamic_slice_in_dim(gs, cidx * rh, rh, axis=1)
    pair_f32, pair_bf = add_halves(my_half.reshape(4 * rh, D), from_sib.reshape(4 * rh, D), "pair_sum")
    pair_f32 = pair_f32.reshape(4, rh, D)
    recv = scatter_chips(pair_bf.reshape(4, rh, D))
    own = lax.dynamic_index_in_dim(pair_f32, chip, axis=0, keepdims=False)
    red_half = add4(own, recv[0], recv[1], recv[2], "chip_sum")
    red = join_halves(red_half).reshape(r_rows, D)
    rs_shapes = [shard2d[n].shape for n in rs_names]
    g_shard = dict(zip(rs_names, _unpack(red, rs_shapes)))

    small_order = ("b_mod", "norm1_g", "b_gate", "conv_qk", "head_norm_g", "sgu_ln_g", "sgu_ln_b", "w_s", "b_s", "norm2_g",
                   "w_ffn_conv", "final_g")
    vec_parts = [gsmall[n] for n in small_order] + [d_modx, d_modc, c.reshape(-1), loss_l.reshape(1)]
    vec_shapes = [a.shape for a in vec_parts]
    vec = _pack(vec_parts, F32, align=8 * LANE, width=LANE)
    allv = gather_all(vec)
    summed = sum8(allv, "small_sum")
    s_parts = _unpack(summed, vec_shapes)
    g_small = dict(zip(small_order, s_parts[:len(small_order)]))
    dmc_sum = s_parts[len(small_order) + 1]
    loss = s_parts[-1][0]
    per_dev = [_unpack(allv[k], vec_shapes) for k in range(8)]
    dmx_all = jnp.stack([p[len(small_order)] for p in per_dev])
    c_all = jnp.stack([p[len(small_order) + 2] for p in per_dev])

    a_all = _pad_rows(jnp.concatenate([c_all, c_ctx.reshape(1, D)], axis=0), 16)
    dm_all = _pad_rows(jnp.concatenate([dmx_all, dmc_sum.reshape(1, NMOD * D)], axis=0), 16)
    ncol = NMOD * D // 4
    dm_shard = lax.dynamic_slice_in_dim(dm_all, chip * ncol, ncol, axis=1)
    g_shard["w_mod"] = mod_bwd_w(a_all, dm_shard, "mod_bwd_w")
    w_mod_t = wt["w_mod"][:, :2 * D].T
    g_cctx = mod_bwd_cctx(_pad_rows(dmc_sum[:2 * D].reshape(1, 2 * D), 16), w_mod_t, c_ctx.reshape(1, D))[0]
    g_small["c_ctx"] = g_cctx

    results = {}
    for n in BIG:
        shp = params[n].shape
        d_, m_, v_ = adamw(shard2d[n], g_shard[n], mom_m[n].reshape(shard2d[n].shape),
                           mom_v[n].reshape(shard2d[n].shape), "adamw_" + n)
        results[n] = (g_shard[n].reshape(shp), d_.reshape(shp), m_.reshape(shp), v_.reshape(shp))

    conv_g = lax.dynamic_slice_in_dim(g_small["conv_qk"].reshape(3, 2 * D), chip * (2 * D // 4), 2 * D // 4, axis=1)
    fconv_g = lax.dynamic_slice_in_dim(g_small["w_ffn_conv"].reshape(9, DFF), chip * (DFF // 4), DFF // 4, axis=1)
    g_small["conv_qk"] = conv_g
    g_small["w_ffn_conv"] = fconv_g
    w_list = [params[n].reshape(-1) for n in SMALL]
    g_list = [g_small[n].reshape(-1) for n in SMALL]
    m_list = [mom_m[n].reshape(-1) for n in SMALL]
    v_list = [mom_v[n].reshape(-1) for n in SMALL]
    sm_shapes = [params[n].shape for n in SMALL]
    pk = lambda lst: _pack(lst, F32, align=8 * LANE, width=LANE)
    gp = pk(g_list)
    d_s, m_s, v_s = adamw(pk(w_list), gp, pk(m_list), pk(v_list), "adamw_small")
    for n, gg, dd, mm, vv in zip(SMALL, _unpack(gp, sm_shapes), _unpack(d_s, sm_shapes), _unpack(m_s, sm_shapes),
                                 _unpack(v_s, sm_shapes)):
        results[n] = (gg, dd, mm, vv)

    order = ("c_ctx", "w_mod", "b_mod", "norm1_g", "w_in", "b_gate", "conv_qk", "head_norm_g", "sgu_ln_g", "sgu_ln_b",
             "w_s", "b_s", "w_branch_mlstm", "w_branch_sgu", "w_out", "norm2_g", "w_up", "w_ffn_conv", "w_down",
             "final_g")
    outs = [loss, grad_x[None]]
    for k in range(4):
        outs += [results[n][k] for n in order]
    return tuple(outs)
```

```python
import functools
import math

import jax
import jax.numpy as jnp
from jax import lax
from jax.experimental import pallas as pl
from jax.experimental.pallas import tpu as pltpu

F32 = jnp.float32
BF16 = jnp.bfloat16

D = 1024
NH = 4
DH = 256
LC = 256
GW = 64
SG = 4
SGD = 256
SCH = 128
DFF = 2816
NMOD = 6
NGATE = 16
NIN = 8208
EPS = 1e-6
M_INIT = -1e30
TR = 256
LANE = 128
VMEM_LIMIT = 56 * 1024 * 1024
MESH = pl.DeviceIdType.MESH

ADAM_LR = 0.001
ADAM_B1 = 0.9
ADAM_B2 = 0.999
ADAM_EPS = 1e-08
ADAM_WD = 0.01
ADAM_STEP = 10

CB_O, CB_U, CB_VG, CB_GM, CB_GG, CB_V, CB_Q, CB_K = range(8)


def _pick(n, cands):
    for c in cands:
        if n % c == 0:
            return c
    return n


def _cparams(sem):
    return pltpu.CompilerParams(dimension_semantics=sem, vmem_limit_bytes=VMEM_LIMIT)


def _sigmoid(x):
    return 1.0 / (1.0 + jnp.exp(-x))


def _silu(x):
    return x * _sigmoid(x)


def _dsilu(x):
    s = _sigmoid(x)
    return s * (1.0 + x * (1.0 - s))


_GC = math.sqrt(2.0 / math.pi)


def _gelu(x):
    return 0.5 * x * (1.0 + jnp.tanh(_GC * (x + 0.044715 * x * x * x)))


def _dgelu(x):
    t = jnp.tanh(_GC * (x + 0.044715 * x * x * x))
    return 0.5 * (1.0 + t) + 0.5 * x * (1.0 - t * t) * _GC * (1.0 + 3.0 * 0.044715 * x * x)


def _dot(a, b):
    return jnp.dot(a.astype(BF16), b.astype(BF16), preferred_element_type=F32)


def _dot_nt(a, b):
    return lax.dot_general(a.astype(BF16), b.astype(BF16), (((1,), (1,)), ((), ())), preferred_element_type=F32)


def _dot_tn(a, b):
    return lax.dot_general(a.astype(BF16), b.astype(BF16), (((0,), (0,)), ((), ())), preferred_element_type=F32)


def _exact_dot(tri, x):
    x1 = x.astype(BF16)
    r1 = x - x1.astype(F32)
    x2 = r1.astype(BF16)
    x3 = (r1 - x2.astype(F32)).astype(BF16)
    return (jnp.dot(tri, x1, preferred_element_type=F32) + jnp.dot(tri, x2, preferred_element_type=F32)
            + jnp.dot(tri, x3, preferred_element_type=F32))


def _rb(tm, w, col=0, off=0):
    return pl.BlockSpec((tm, w), lambda i: (i + off, col))


def _bc(r, w):
    return pl.BlockSpec((r, w), lambda i: (0, 0))


def mm_nn(a, b, out_dtype, name):
    squeeze = a.ndim == 2
    if squeeze:
        a, b = a[None], b[None]
    g, m, k = a.shape
    n = b.shape[2]
    tm = _pick(m, (1280, 1024, 512, 256, 128))
    tn = _pick(n, (1024, 1408, 512, 128))
    tk = _pick(k, (1024, 1408, 512, 128))
    nk = k // tk

    def body(a_ref, b_ref, o_ref, *scr):
        if nk == 1:
            o_ref[0] = _dot(a_ref[0], b_ref[0]).astype(o_ref.dtype)
        else:
            acc_ref, = scr
            kk = pl.program_id(3)

            @pl.when(kk == 0)
            def _():
                acc_ref[...] = jnp.zeros_like(acc_ref)

            acc_ref[...] += _dot(a_ref[0], b_ref[0])

            @pl.when(kk == nk - 1)
            def _():
                o_ref[0] = acc_ref[...].astype(o_ref.dtype)

    out = pl.pallas_call(
        body, name=name, grid=(g, n // tn, m // tm, nk),
        in_specs=[pl.BlockSpec((1, tm, tk), lambda gi, j, i, kk: (gi, i, kk)),
                  pl.BlockSpec((1, tk, tn), lambda gi, j, i, kk: (gi, kk, j))],
        out_specs=pl.BlockSpec((1, tm, tn), lambda gi, j, i, kk: (gi, i, j)),
        out_shape=jax.ShapeDtypeStruct((g, m, n), out_dtype),
        scratch_shapes=[] if nk == 1 else [pltpu.VMEM((tm, tn), F32)],
        compiler_params=_cparams(("parallel", "parallel", "parallel", "arbitrary")),
    )(a, b)
    return out[0] if squeeze else out


def mm_tn(a, b, name, out_dtype=F32):
    squeeze = a.ndim == 2
    if squeeze:
        a, b = a[None], b[None]
    g, t, ka = a.shape
    n = b.shape[2]
    tka = _pick(ka, (1024, 1408, 512, 128))
    tn = _pick(n, (1024, 1408, 512, 128))
    tt = _pick(t, (1280, 1024, 512, 256, 128))
    nt = t // tt

    def body(a_ref, b_ref, o_ref, acc_ref):
        tt_i = pl.program_id(3)

        @pl.when(tt_i == 0)
        def _():
            acc_ref[...] = jnp.zeros_like(acc_ref)

        acc_ref[...] += _dot_tn(a_ref[0], b_ref[0])

        @pl.when(tt_i == nt - 1)
        def _():
            o_ref[0] = acc_ref[...].astype(o_ref.dtype)

    out = pl.pallas_call(
        body, name=name, grid=(g, ka // tka, n // tn, nt),
        in_specs=[pl.BlockSpec((1, tt, tka), lambda gi, i, j, ti: (gi, ti, i)),
                  pl.BlockSpec((1, tt, tn), lambda gi, i, j, ti: (gi, ti, j))],
        out_specs=pl.BlockSpec((1, tka, tn), lambda gi, i, j, ti: (gi, i, j)),
        out_shape=jax.ShapeDtypeStruct((g, ka, n), out_dtype),
        scratch_shapes=[pltpu.VMEM((tka, tn), F32)],
        compiler_params=_cparams(("parallel", "parallel", "parallel", "arbitrary")),
    )(a, b)
    return out[0] if squeeze else out


def mod_fwd(cc, w_mod, b_mod):
    n = w_mod.shape[1]

    def body(c_ref, w_ref, b_ref, o_ref):
        o_ref[...] = _dot(_silu(c_ref[...]), w_ref[...]) + b_ref[...]

    return pl.pallas_call(
        body, name="mod_fwd", grid=(n // D,),
        in_specs=[_bc(16, D), pl.BlockSpec((D, D), lambda j: (0, j)), pl.BlockSpec((1, D), lambda j: (0, j))],
        out_specs=pl.BlockSpec((16, D), lambda j: (0, j)),
        out_shape=jax.ShapeDtypeStruct((16, n), F32),
        compiler_params=_cparams(("parallel",)),
    )(cc, w_mod, b_mod)


def norm1_fwd(xcat, g, tab, n_ctx_tiles):
    tp = xcat.shape[0]

    def body(x_ref, g_ref, tab_ref, o_ref):
        x = x_ref[...]
        r = lax.rsqrt(jnp.mean(x * x, axis=-1, keepdims=True) + EPS)
        nrm = x * r * g_ref[...]
        is_ctx = pl.program_id(0) < n_ctx_tiles
        sh = jnp.where(is_ctx, tab_ref[0:1, :], tab_ref[2:3, :])
        sc = jnp.where(is_ctx, tab_ref[1:2, :], tab_ref[3:4, :])
        o_ref[...] = (nrm * (1.0 + sc) + sh).astype(BF16)

    return pl.pallas_call(
        body, name="norm1_fwd", grid=(tp // TR,),
        in_specs=[_rb(TR, D), _bc(1, D), _bc(8, D)],
        out_specs=_rb(TR, D),
        out_shape=jax.ShapeDtypeStruct((tp, D), BF16),
        compiler_params=_cparams(("parallel",)),
    )(xcat, g, tab)


def norm2_fwd(x, out, g, tab):
    t = x.shape[0]

    def body(x_ref, o_in_ref, g_ref, tab_ref, h1_ref, hn_ref):
        h1 = x_ref[...] + tab_ref[0:1, :] * o_in_ref[...]
        h1_ref[...] = h1
        r = lax.rsqrt(jnp.mean(h1 * h1, axis=-1, keepdims=True) + EPS)
        nrm = h1 * r * g_ref[...]
        hn_ref[...] = (nrm * (1.0 + tab_ref[2:3, :]) + tab_ref[1:2, :]).astype(BF16)

    return pl.pallas_call(
        body, name="norm2_fwd", grid=(t // TR,),
        in_specs=[_rb(TR, D), _rb(TR, D), _bc(1, D), _bc(8, D)],
        out_specs=[_rb(TR, D), _rb(TR, D)],
        out_shape=[jax.ShapeDtypeStruct((t, D), F32), jax.ShapeDtypeStruct((t, D), BF16)],
        compiler_params=_cparams(("parallel",)),
    )(x, out, g, tab)


def _halo_specs(tm, w, col, n_rows, hb):
    per = tm // hb
    last = n_rows // hb - 1
    prev = pl.BlockSpec((hb, w), lambda i: (jnp.maximum(i * per - 1, 0), col))
    nxt = pl.BlockSpec((hb, w), lambda i: (jnp.minimum((i + 1) * per, last), col))
    return prev, nxt


def _shift_rows(x, prev_row, next_row):
    tm = x.shape[0]
    rid = lax.broadcasted_iota(jnp.int32, x.shape, 0)
    xm1 = jnp.where(rid == 0, prev_row, pltpu.roll(x, 1, 0))
    xp1 = jnp.where(rid == tm - 1, next_row, pltpu.roll(x, tm - 1, 0))
    return xm1, xp1


def _seq_edges(i, n_ctx_tiles, n_tiles):
    first = jnp.logical_or(i == 0, i == n_ctx_tiles)
    last = jnp.logical_or(i == n_ctx_tiles - 1, i == n_tiles - 1)
    return first, last


def qkconv_fwd(zmain, zg, conv_w, b_gate, n_ctx_tiles):
    tp = zmain.shape[0]
    nt = tp // TR
    w2 = 2 * D
    prev_s, next_s = _halo_specs(TR, w2, CB_Q // 2, tp, 16)

    def body(z_ref, zp_ref, zn_ref, w_ref, zg_ref, bg_ref, q_ref, k_ref, g_ref, zgb_ref):
        i = pl.program_id(0)
        first, last = _seq_edges(i, n_ctx_tiles, nt)
        z = z_ref[...].astype(F32)
        pr = jnp.where(first, 0.0, zp_ref[15:16, :].astype(F32))
        nx = jnp.where(last, 0.0, zn_ref[0:1, :].astype(F32))
        zm1, zp1 = _shift_rows(z, pr, nx)
        cv = w_ref[0:1, :] * zm1 + w_ref[1:2, :] * z + w_ref[2:3, :] * zp1
        a = _silu(cv)
        q_ref[...] = (a[:, :D] * (DH ** -0.5)).astype(BF16)
        k_ref[...] = a[:, D:].astype(BF16)
        zgb = zg_ref[...] + bg_ref[...]
        zgb_ref[...] = zgb
        logf = jnp.minimum(zgb, 0.0) - jnp.log(1.0 + jnp.exp(-jnp.abs(zgb)))
        rr = lax.broadcasted_iota(jnp.int32, (TR, TR), 0)
        cc = lax.broadcasted_iota(jnp.int32, (TR, TR), 1)
        same = (rr // LC) == (cc // LC)
        low = jnp.where(jnp.logical_and(same, cc <= rr), 1.0, 0.0).astype(BF16)
        upp = jnp.where(jnp.logical_and(same, cc >= rr), 1.0, 0.0).astype(BF16)
        bf = _exact_dot(low, logf)
        bb = _exact_dot(upp, logf)
        lane = lax.broadcasted_iota(jnp.int32, (TR, LANE), 1)
        g = jnp.where(jnp.logical_and(lane >= 4, lane < 8), bf,
                      jnp.where(jnp.logical_and(lane >= 12, lane < 16), bb, zgb))
        g_ref[...] = g

    return pl.pallas_call(
        body, name="qkconv_fwd", grid=(nt,),
        in_specs=[_rb(TR, w2, CB_Q // 2), prev_s, next_s, _bc(8, w2), _rb(TR, LANE), _bc(1, LANE)],
        out_specs=[_rb(TR, D), _rb(TR, D), _rb(TR, LANE), _rb(TR, LANE)],
        out_shape=[jax.ShapeDtypeStruct((tp, D), BF16), jax.ShapeDtypeStruct((tp, D), BF16),
                   jax.ShapeDtypeStruct((tp, LANE), F32), jax.ShapeDtypeStruct((tp, LANE), F32)],
        compiler_params=_cparams(("parallel",)),
    )(zmain, zmain, zmain, conv_w, zg, b_gate)


def qkconv_bwd_a(zmain, dqf, dqb, dkf, dkb, conv_w, n_ctx_tiles):
    tp = zmain.shape[0]
    nt = tp // TR
    w2 = 2 * D
    prev_s, next_s = _halo_specs(TR, w2, CB_Q // 2, tp, 16)

    def body(z_ref, zp_ref, zn_ref, w_ref, dqf_ref, dqb_ref, dkf_ref, dkb_ref, dc_ref, dw_ref):
        i = pl.program_id(0)
        first, last = _seq_edges(i, n_ctx_tiles, nt)
        z = z_ref[...].astype(F32)
        pr = jnp.where(first, 0.0, zp_ref[15:16, :].astype(F32))
        nx = jnp.where(last, 0.0, zn_ref[0:1, :].astype(F32))
        zm1, zp1 = _shift_rows(z, pr, nx)
        cv = w_ref[0:1, :] * zm1 + w_ref[1:2, :] * z + w_ref[2:3, :] * zp1
        da = jnp.concatenate(
            [(dqf_ref[...].astype(F32) + dqb_ref[...].astype(F32)) * (DH ** -0.5),
             dkf_ref[...].astype(F32) + dkb_ref[...].astype(F32)], axis=1)
        dc = da * _dsilu(cv)
        dc_ref[...] = dc.astype(BF16)

        @pl.when(i == 0)
        def _():
            dw_ref[...] = jnp.zeros_like(dw_ref)

        dw_ref[0:1, :] += jnp.sum(zm1 * dc, axis=0, keepdims=True)
        dw_ref[1:2, :] += jnp.sum(z * dc, axis=0, keepdims=True)
        dw_ref[2:3, :] += jnp.sum(zp1 * dc, axis=0, keepdims=True)

    return pl.pallas_call(
        body, name="qkconv_bwd_a", grid=(nt,),
        in_specs=[_rb(TR, w2, CB_Q // 2), prev_s, next_s, _bc(8, w2), _rb(TR, D), _rb(TR, D), _rb(TR, D), _rb(TR, D)],
        out_specs=[_rb(TR, w2), _bc(8, w2)],
        out_shape=[jax.ShapeDtypeStruct((tp, w2), BF16), jax.ShapeDtypeStruct((8, w2), F32)],
        compiler_params=_cparams(("arbitrary",)),
    )(zmain, zmain, zmain, conv_w, dqf, dqb, dkf, dkb)


def qkconv_bwd_b(dz, dc, conv_w, n_ctx_tiles):
    tp = dc.shape[0]
    nt = tp // TR
    w2 = 2 * D
    prev_s, next_s = _halo_specs(TR, w2, 0, tp, 16)

    def body(dz_in_ref, d_ref, dp_ref, dn_ref, w_ref, o_ref):
        del dz_in_ref
        i = pl.program_id(0)
        first, last = _seq_edges(i, n_ctx_tiles, nt)
        d = d_ref[...].astype(F32)
        pr = jnp.where(first, 0.0, dp_ref[15:16, :].astype(F32))
        nx = jnp.where(last, 0.0, dn_ref[0:1, :].astype(F32))
        dm1, dp1 = _shift_rows(d, pr, nx)
        o_ref[...] = (w_ref[0:1, :] * dp1 + w_ref[1:2, :] * d + w_ref[2:3, :] * dm1).astype(BF16)

    return pl.pallas_call(
        body, name="qkconv_bwd_b", grid=(nt,),
        in_specs=[pl.BlockSpec(memory_space=pl.ANY), _rb(TR, w2), prev_s, next_s, _bc(8, w2)],
        out_specs=_rb(TR, w2, CB_Q // 2),
        out_shape=jax.ShapeDtypeStruct(dz.shape, BF16),
        input_output_aliases={0: 0},
        compiler_params=_cparams(("parallel",)),
    )(dz, dc, dc, dc, conv_w)


def add_into_dz(dz, a, b, col):
    tp = a.shape[0]

    def body(dz_in_ref, a_ref, b_ref, o_ref):
        del dz_in_ref
        o_ref[...] = (a_ref[...].astype(F32) + b_ref[...].astype(F32)).astype(BF16)

    return pl.pallas_call(
        body, name="add_into_dz", grid=(tp // TR,),
        in_specs=[pl.BlockSpec(memory_space=pl.ANY), _rb(TR, D), _rb(TR, D)],
        out_specs=_rb(TR, D, col),
        out_shape=jax.ShapeDtypeStruct(dz.shape, BF16),
        input_output_aliases={0: 0},
        compiler_params=_cparams(("parallel",)),
    )(dz, a, b)


def _chunk_maps(nc, ncc):
    def fwd(t):
        return t

    def bwd(t):
        return jnp.where(t < ncc, ncc - 1 - t, nc - 1 + ncc - t)

    return fwd, bwd


def _mlstm_chunk(d, h, gc, gr, q_ref, k_ref, v_ref, cp, npv, m_prev, mask):
    ic, bcol = 8 * d + h, 8 * d + 4 + h
    i_col, b_col = gc[:, ic:ic + 1], gc[:, bcol:bcol + 1]
    i_row, b_row = gr[ic:ic + 1, :], gr[bcol:bcol + 1, :]
    g = b_row[:, LC - 1:LC] if d == 0 else b_row[:, 0:1]
    a_row = g - b_row + i_row
    m_loc = jnp.max(a_row, axis=1, keepdims=True)
    dmat = jnp.where(mask, b_col - b_row + i_row, -jnp.inf)
    inter = b_col + m_prev
    m_row = jnp.maximum(inter, jnp.max(dmat, axis=1, keepdims=True))
    e = jnp.exp(dmat - m_row)
    w = jnp.exp(inter - m_row)
    hs = slice(h * DH, (h + 1) * DH)
    qh, kh, vh = q_ref[:, hs], k_ref[:, hs], v_ref[:, hs]
    p = _dot_nt(qh, kh)
    s = p * e
    cpb = cp.astype(BF16)
    qc = _dot(qh, cpb)
    num = _dot(s, vh) + w * qc
    qn = jnp.sum(qh.astype(F32) * npv, axis=1, keepdims=True)
    den = jnp.sum(s, axis=1, keepdims=True) + w * qn
    thr = jnp.exp(-m_row)
    m_new = jnp.maximum(g + m_prev, m_loc)
    a_old = jnp.exp(g + m_prev - m_new)
    a_col = g - b_col + i_col
    return dict(qh=qh, kh=kh, vh=vh, e=e, w=w, s=s, cpb=cpb, qc=qc, num=num, qn=qn, den=den, thr=thr,
                m_loc=m_loc, m_new=m_new, a_old=a_old, a_col=a_col, hs=hs)


def mlstm_fwd(qa, ka, zmain, gcol, grow, ncc):
    tp = qa.shape[0]
    nc = tp // LC
    cf, cb = _chunk_maps(nc, ncc)

    def body(qf, kf, vf, gcf, grf, qb, kb, vb, gcb, grb,
             hf_o, hb_o, cf_o, cb_o, nf_o, nb_o, mf_o, mb_o, c_sc, n_sc, m_sc):
        t = pl.program_id(0)

        @pl.when(t == 0)
        def _():
            c_sc[...] = jnp.zeros_like(c_sc)
            n_sc[...] = jnp.zeros_like(n_sc)
            m_sc[...] = jnp.full(m_sc.shape, M_INIT, F32)

        row = lax.broadcasted_iota(jnp.int32, (LC, LC), 0)
        col = lax.broadcasted_iota(jnp.int32, (LC, LC), 1)
        dirs = ((qf, kf, vf, gcf, grf, hf_o, cf_o, nf_o, mf_o), (qb, kb, vb, gcb, grb, hb_o, cb_o, nb_o, mb_o))
        for d, (q_ref, k_ref, v_ref, gc_ref, gr_ref, h_o, c_o, n_o, m_o) in enumerate(dirs):
            mask = (col <= row) if d == 0 else (col >= row)
            gc = gc_ref[...]
            gr = gr_ref[0]
            for h in range(NH):
                idx = d * NH + h
                cp = c_sc[idx]
                npv = n_sc[idx]
                m_full = m_sc[idx]
                m_prev = m_full[:, 0:1]
                r = _mlstm_chunk(d, h, gc, gr, q_ref, k_ref, v_ref, cp, npv, m_prev, mask)
                hs = r["hs"]
                h_o[:, hs] = r["num"] / jnp.maximum(jnp.abs(r["den"]), r["thr"])
                c_o[0, hs, :] = r["cpb"]
                n_o[0, h:h + 1, :] = npv
                m_o[0, h:h + 1, :] = m_full
                a_new = jnp.exp(r["m_loc"] - r["m_new"])
                kw = r["kh"].astype(F32) * jnp.exp(r["a_col"] - r["m_loc"])
                kv = _dot_tn(kw, r["vh"])
                kn = jnp.sum(kw, axis=0, keepdims=True)
                c_sc[idx] = r["a_old"] * cp + a_new * kv
                n_sc[idx] = r["a_old"] * npv + a_new * kn
                m_sc[idx] = jnp.broadcast_to(r["m_new"], (1, LANE))

    def dspecs(cm):
        return [pl.BlockSpec((LC, D), lambda t: (cm(t), 0)),
                pl.BlockSpec((LC, D), lambda t: (cm(t), 0)),
                pl.BlockSpec((LC, D), lambda t: (cm(t), CB_V)),
                pl.BlockSpec((LC, LANE), lambda t: (cm(t), 0)),
                pl.BlockSpec((1, 16, LC), lambda t: (cm(t), 0, 0))]

    def ospec(cm, shp):
        return pl.BlockSpec((1,) + shp, lambda t: (cm(t), 0, 0))

    return pl.pallas_call(
        body, name="mlstm_fwd", grid=(nc,),
        in_specs=dspecs(cf) + dspecs(cb),
        out_specs=[pl.BlockSpec((LC, D), lambda t: (cf(t), 0)), pl.BlockSpec((LC, D), lambda t: (cb(t), 0)),
                   ospec(cf, (D, DH)), ospec(cb, (D, DH)), ospec(cf, (NH, DH)), ospec(cb, (NH, DH)),
                   ospec(cf, (NH, LANE)), ospec(cb, (NH, LANE))],
        out_shape=[jax.ShapeDtypeStruct((tp, D), F32), jax.ShapeDtypeStruct((tp, D), F32),
                   jax.ShapeDtypeStruct((nc, D, DH), BF16), jax.ShapeDtypeStruct((nc, D, DH), BF16),
                   jax.ShapeDtypeStruct((nc, NH, DH), F32), jax.ShapeDtypeStruct((nc, NH, DH), F32),
                   jax.ShapeDtypeStruct((nc, NH, LANE), F32), jax.ShapeDtypeStruct((nc, NH, LANE), F32)],
        scratch_shapes=[pltpu.VMEM((2 * NH, DH, DH), F32), pltpu.VMEM((2 * NH, 1, DH), F32),
                        pltpu.VMEM((2 * NH, 1, LANE), F32)],
        compiler_params=_cparams(("arbitrary",)),
    )(qa, ka, zmain, gcol, grow, qa, ka, zmain, gcol, grow)


def mlstm_bwd(qa, ka, zmain, gcol, grow, states, dhm, ncc):
    tp = qa.shape[0]
    nc = tp // LC
    cf0, cb0 = _chunk_maps(nc, ncc)
    cf = lambda t: cf0(nc - 1 - t)
    cb = lambda t: cb0(nc - 1 - t)
    csf, csb, nsf, nsb, msf, msb = states

    def body(qf, kf, vf, gcf, grf, cpf, npf, mpf, dhf, qb, kb, vb, gcb, grb, cpb_, npb, mpb, dhb,
             dqf_o, dkf_o, dvf_o, colf_o, rowf_o, dqb_o, dkb_o, dvb_o, colb_o, rowb_o, dc_sc, dn_sc):
        t = pl.program_id(0)

        @pl.when(t == 0)
        def _():
            dc_sc[...] = jnp.zeros_like(dc_sc)
            dn_sc[...] = jnp.zeros_like(dn_sc)

        row = lax.broadcasted_iota(jnp.int32, (LC, LC), 0)
        col = lax.broadcasted_iota(jnp.int32, (LC, LC), 1)
        dirs = ((qf, kf, vf, gcf, grf, cpf, npf, mpf, dhf, dqf_o, dkf_o, dvf_o, colf_o, rowf_o, cf),
                (qb, kb, vb, gcb, grb, cpb_, npb, mpb, dhb, dqb_o, dkb_o, dvb_o, colb_o, rowb_o, cb))
        for d, (q_ref, k_ref, v_ref, gc_ref, gr_ref, cp_ref, np_ref, mp_ref, dh_ref,
                dq_o, dk_o, dv_o, col_o, row_o, cm) in enumerate(dirs):
            mask = (col <= row) if d == 0 else (col >= row)
            live = jnp.where(cm(t) >= ncc, 1.0, 0.0).astype(F32)
            gc = gc_ref[...]
            gr = gr_ref[0]
            col_o[...] = jnp.zeros_like(col_o)
            row_o[...] = jnp.zeros_like(row_o)
            for h in range(NH):
                idx = d * NH + h
                hs = slice(h * DH, (h + 1) * DH)
                cp = cp_ref[0, hs, :]
                npv = np_ref[0, h:h + 1, :]
                m_prev = mp_ref[0, h:h + 1, 0:1]
                r = _mlstm_chunk(d, h, gc, gr, q_ref, k_ref, v_ref, cp, npv, m_prev, mask)
                qh, kh, vh, e, w, s = r["qh"], r["kh"], r["vh"], r["e"], r["w"], r["s"]
                qf32, kf32 = qh.astype(F32), kh.astype(F32)
                den, thr = r["den"], r["thr"]
                rden = 1.0 / jnp.maximum(jnp.abs(den), thr)
                hh = r["num"] * rden
                dh = dh_ref[:, hs] * live
                dnum = dh * rden
                sgn = jnp.where(jnp.abs(den) > thr, jnp.sign(den), 0.0)
                dden = -jnp.sum(dh * hh, axis=1, keepdims=True) * rden * sgn
                ds = _dot_nt(dnum, vh) + dden
                dp = ds * e
                gm = ds * s
                rowsum = jnp.sum(gm, axis=1, keepdims=True)
                colsum = jnp.sum(gm, axis=0, keepdims=True)
                dq = _dot(dp, kh) + w * (_dot_nt(dnum, r["cpb"]) + dden * npv)
                dcs = dc_sc[idx]
                dns = dn_sc[idx]
                kfac = jnp.exp(r["a_col"] - r["m_new"])
                vdc = _dot_nt(vh, dcs)
                dk = _dot_tn(dp, qh) + kfac * (vdc + dns)
                dv = _dot_tn(s, dnum) + kfac * _dot(kh, dcs)
                beta = w * (jnp.sum(dnum * r["qc"], axis=1, keepdims=True) + dden * r["qn"])
                alpha = kfac * (jnp.sum(kf32 * vdc, axis=1, keepdims=True) + jnp.sum(kf32 * dns, axis=1, keepdims=True))
                dq_o[:, hs] = dq.astype(BF16)
                dk_o[:, hs] = dk.astype(BF16)
                dv_o[:, hs] = dv.astype(BF16)
                cpf = r["cpb"].astype(F32)
                inner = (jnp.sum(jnp.sum(dcs * cpf, axis=1, keepdims=True), axis=0, keepdims=True)
                         + jnp.sum(dns * npv, axis=1, keepdims=True))
                gam = jnp.sum(alpha, axis=0, keepdims=True) + r["a_old"] * inner
                col_o[:, h:h + 1] = rowsum + beta - alpha
                col_o[:, 4 + h:5 + h] = alpha
                col_o[:, 8 + h:9 + h] = jnp.broadcast_to(gam, (LC, 1))
                row_o[0, h:h + 1, :] = colsum
                wq = qf32 * w
                dc_sc[idx] = r["a_old"] * dcs + _dot_tn(wq, dnum)
                dn_sc[idx] = r["a_old"] * dns + jnp.sum(wq * dden, axis=0, keepdims=True)

    def dspecs(cm):
        return [pl.BlockSpec((LC, D), lambda t: (cm(t), 0)),
                pl.BlockSpec((LC, D), lambda t: (cm(t), 0)),
                pl.BlockSpec((LC, D), lambda t: (cm(t), CB_V)),
                pl.BlockSpec((LC, LANE), lambda t: (cm(t), 0)),
                pl.BlockSpec((1, 16, LC), lambda t: (cm(t), 0, 0)),
                pl.BlockSpec((1, D, DH), lambda t: (cm(t), 0, 0)),
                pl.BlockSpec((1, NH, DH), lambda t: (cm(t), 0, 0)),
                pl.BlockSpec((1, NH, LANE), lambda t: (cm(t), 0, 0)),
                pl.BlockSpec((LC, D), lambda t: (jnp.maximum(cm(t) - ncc, 0), 0))]

    def ospecs(cm):
        return [pl.BlockSpec((LC, D), lambda t: (cm(t), 0)),
                pl.BlockSpec((LC, D), lambda t: (cm(t), 0)),
                pl.BlockSpec((LC, D), lambda t: (cm(t), 0)),
                pl.BlockSpec((LC, LANE), lambda t: (cm(t), 0)),
                pl.BlockSpec((1, 8, LC), lambda t: (cm(t), 0, 0))]

    oshape = [jax.ShapeDtypeStruct((tp, D), BF16)] * 3 + [jax.ShapeDtypeStruct((tp, LANE), F32),
                                                        jax.ShapeDtypeStruct((nc, 8, LC), F32)]
    return pl.pallas_call(
        body, name="mlstm_bwd", grid=(nc,),
        in_specs=dspecs(cf) + dspecs(cb),
        out_specs=ospecs(cf) + ospecs(cb),
        out_shape=oshape + oshape,
        scratch_shapes=[pltpu.VMEM((2 * NH, DH, DH), F32), pltpu.VMEM((2 * NH, 1, DH), F32)],
        compiler_params=_cparams(("arbitrary",)),
    )(qa, ka, zmain, gcol, grow, csf, nsf, msf, dhm, qa, ka, zmain, gcol, grow, csb, nsb, msb, dhm)


def gates_bwd(rg, gam, csi, csf, ali, zgb):
    tp = rg.shape[0]

    def body(rg_ref, gam_ref, csi_ref, csf_ref, ali_ref, zgb_ref, o_ref, db_ref):
        i = pl.program_id(0)

        @pl.when(i == 0)
        def _():
            db_ref[...] = jnp.zeros_like(db_ref)

        lane = lax.broadcasted_iota(jnp.int32, (TR, LANE), 1)
        i_l = jnp.logical_or(lane < 4, jnp.logical_and(lane >= 8, lane < 12))
        f_l = jnp.logical_or(jnp.logical_and(lane >= 4, lane < 8), jnp.logical_and(lane >= 12, lane < 16))
        dbh = jnp.where(f_l, rg_ref[...] - csf_ref[...], 0.0)
        rr = lax.broadcasted_iota(jnp.int32, (TR, TR), 0)
        cc = lax.broadcasted_iota(jnp.int32, (TR, TR), 1)
        same = (rr // LC) == (cc // LC)
        low = jnp.where(jnp.logical_and(same, cc <= rr), 1.0, 0.0).astype(BF16)
        upp = jnp.where(jnp.logical_and(same, cc >= rr), 1.0, 0.0).astype(BF16)
        dlogf = jnp.where(lane < 8, _exact_dot(upp, dbh), _exact_dot(low, dbh)) + gam_ref[...]
        out = (jnp.where(i_l, csi_ref[...] + ali_ref[...], 0.0)
               + jnp.where(f_l, dlogf * _sigmoid(-zgb_ref[...]), 0.0))
        o_ref[...] = out
        db_ref[...] += jnp.sum(out, axis=0, keepdims=True)

    spec = _rb(TR, LANE)
    return pl.pallas_call(
        body, name="gates_bwd", grid=(tp // TR,),
        in_specs=[spec] * 6,
        out_specs=[spec, _bc(1, LANE)],
        out_shape=[jax.ShapeDtypeStruct((tp, LANE), F32), jax.ShapeDtypeStruct((1, LANE), F32)],
        compiler_params=_cparams(("arbitrary",)),
    )(rg, gam, csi, csf, ali, zgb)


def _head_norm(hm, gh):
    xs, rs = [], []
    for h in range(NH):
        seg = hm[:, h * DH:(h + 1) * DH]
        r = lax.rsqrt(jnp.mean(seg * seg, axis=-1, keepdims=True) + EPS)
        xs.append(seg * r)
        rs.append(r)
    xh = jnp.concatenate(xs, axis=1)
    return xh, rs, xh * gh


def _sgu_norm(zvg, ln_g, ln_b):
    vg = _gelu(zvg)
    mu = jnp.mean(vg, axis=-1, keepdims=True)
    vc = vg - mu
    rstd = lax.rsqrt(jnp.mean(vc * vc, axis=-1, keepdims=True) + EPS)
    vhat = vc * rstd
    return vhat, rstd, vhat * ln_g + ln_b


def _sgu_mix(vn, ws_ref, bs_ref):
    rows = []
    for c in range(TR // SCH):
        cols = []
        for g in range(SG):
            blk = vn[c * SCH:(c + 1) * SCH, g * SGD:(g + 1) * SGD]
            cols.append(_dot(ws_ref[g * SCH:(g + 1) * SCH, :], blk) + bs_ref[:, g:g + 1])
        rows.append(jnp.concatenate(cols, axis=1))
    return jnp.concatenate(rows, axis=0)


def mixer_fwd(hf, hb, zmain, gh, ln_g, ln_b, ws, bs_t, n_ctx_tiles):
    t = hf.shape[0] - n_ctx_tiles * TR
    off = n_ctx_tiles

    def body(hf_ref, hb_ref, zo_ref, zu_ref, zv_ref, gh_ref, lg_ref, lb_ref, ws_ref, bs_ref, o_ref):
        hm = hf_ref[...] + hb_ref[...]
        _, _, hn = _head_norm(hm, gh_ref[...])
        o_ref[0] = (_sigmoid(zo_ref[...].astype(F32)) * hn).astype(BF16)
        _, _, vn = _sgu_norm(zv_ref[...].astype(F32), lg_ref[...], lb_ref[...])
        mixed = _sgu_mix(vn, ws_ref, bs_ref)
        o_ref[1] = (_gelu(zu_ref[...].astype(F32)) * mixed).astype(BF16)

    return pl.pallas_call(
        body, name="mixer_fwd", grid=(t // TR,),
        in_specs=[_rb(TR, D, 0, off), _rb(TR, D, 0, off), _rb(TR, D, CB_O, off), _rb(TR, D, CB_U, off),
                  _rb(TR, D, CB_VG, off), _bc(1, D), _bc(1, D), _bc(1, D), _bc(SG * SCH, SCH), _bc(SCH, LANE)],
        out_specs=pl.BlockSpec((2, TR, D), lambda i: (0, i, 0)),
        out_shape=jax.ShapeDtypeStruct((2, t, D), BF16),
        compiler_params=_cparams(("parallel",)),
    )(hf, hb, zmain, zmain, zmain, gh, ln_g, ln_b, ws, bs_t)


def merge_fwd(zmain, pp, n_ctx_tiles):
    t = pp.shape[1]
    off = n_ctx_tiles

    def body(zgm_ref, zgg_ref, pp_ref, o_ref):
        y = (_sigmoid(zgm_ref[...].astype(F32)) * pp_ref[0].astype(F32)
             + _sigmoid(zgg_ref[...].astype(F32)) * pp_ref[1].astype(F32))
        o_ref[...] = y.astype(BF16)

    return pl.pallas_call(
        body, name="merge_fwd", grid=(t // TR,),
        in_specs=[_rb(TR, D, CB_GM, off), _rb(TR, D, CB_GG, off), pl.BlockSpec((2, TR, D), lambda i: (0, i, 0))],
        out_specs=_rb(TR, D),
        out_shape=jax.ShapeDtypeStruct((t, D), BF16),
        compiler_params=_cparams(("parallel",)),
    )(zmain, zmain, pp)


def merge_bwd(zmain, pp, dy, tp, n_ctx_tiles):
    t = dy.shape[0]
    nt = tp // TR
    xrow = lambda i: jnp.maximum(i - n_ctx_tiles, 0)

    def body(zg_ref, pp_ref, dy_ref, dpp_ref, dz_ref):
        i = pl.program_id(1)
        zg = zg_ref[...].astype(F32)
        sg = _sigmoid(zg)
        dyv = dy_ref[...].astype(F32)
        dpp_ref[0] = (dyv * sg).astype(BF16)
        dzv = dyv * pp_ref[0].astype(F32) * sg * (1.0 - sg)
        dz_ref[...] = jnp.where(i >= n_ctx_tiles, dzv, 0.0).astype(BF16)

    return pl.pallas_call(
        body, name="merge_bwd", grid=(2, nt),
        in_specs=[pl.BlockSpec((TR, D), lambda j, i: (i, CB_GM + j)),
                  pl.BlockSpec((1, TR, D), lambda j, i: (j, xrow(i), 0)),
                  pl.BlockSpec((TR, D), lambda j, i: (xrow(i), 0))],
        out_specs=[pl.BlockSpec((1, TR, D), lambda j, i: (j, xrow(i), 0)),
                   pl.BlockSpec((TR, D), lambda j, i: (i, CB_GM + j))],
        out_shape=[jax.ShapeDtypeStruct((2, t, D), BF16), jax.ShapeDtypeStruct((tp, 8 * D), BF16)],
        compiler_params=_cparams(("arbitrary", "arbitrary")),
    )(zmain, pp, dy)


def mixer_bwd(dz, hf, hb, zmain, dyms, gh, ln_g, ln_b, ws, bs_t, n_ctx_tiles):
    tp = hf.shape[0]
    t = tp - n_ctx_tiles * TR
    nt = tp // TR
    xrow = lambda i: jnp.maximum(i - n_ctx_tiles, 0)

    def body(dz_in_ref, hf_ref, hb_ref, zo_ref, zu_ref, zv_ref, dy_ref, gh_ref, lg_ref, lb_ref, ws_ref, bs_ref,
             dz_ref, dhm_ref, dgh_ref, dlg_ref, dlb_ref, dws_ref, dbs_ref):
        del dz_in_ref
        i = pl.program_id(0)

        @pl.when(i == 0)
        def _():
            for ref in (dgh_ref, dlg_ref, dlb_ref, dws_ref, dbs_ref):
                ref[...] = jnp.zeros_like(ref)

        @pl.when(i < n_ctx_tiles)
        def _():
            dz_ref[...] = jnp.zeros_like(dz_ref)

        @pl.when(i >= n_ctx_tiles)
        def _():
            gh_v = gh_ref[...]
            hm = hf_ref[...] + hb_ref[...]
            xh, rs, hn = _head_norm(hm, gh_v)
            zo = zo_ref[...].astype(F32)
            so = _sigmoid(zo)
            dym = dy_ref[0].astype(F32)
            d_zo = dym * hn * so * (1.0 - so)
            d_hn = dym * so
            dgh_ref[...] += jnp.sum(d_hn * xh, axis=0, keepdims=True)
            d_xh = d_hn * gh_v
            segs = []
            for h in range(NH):
                hs = slice(h * DH, (h + 1) * DH)
                dx, xs = d_xh[:, hs], xh[:, hs]
                segs.append(rs[h] * (dx - xs * jnp.mean(dx * xs, axis=-1, keepdims=True)))
            dhm_ref[...] = jnp.concatenate(segs, axis=1)
            zu = zu_ref[...].astype(F32)
            zv = zv_ref[...].astype(F32)
            lg = lg_ref[...]
            vhat, rstd, vn = _sgu_norm(zv, lg, lb_ref[...])
            mixed = _sgu_mix(vn, ws_ref, bs_ref)
            dys = dy_ref[1].astype(F32)
            d_zu = dys * mixed * _dgelu(zu)
            d_mixed = dys * _gelu(zu)
            rows = []
            for c in range(TR // SCH):
                cols = []
                for g in range(SG):
                    rsl, csl = slice(c * SCH, (c + 1) * SCH), slice(g * SGD, (g + 1) * SGD)
                    dm = d_mixed[rsl, csl]
                    cols.append(_dot_tn(ws_ref[g * SCH:(g + 1) * SCH, :], dm))
                    dws_ref[g * SCH:(g + 1) * SCH, :] += _dot_nt(dm, vn[rsl, csl])
                    dbs_ref[:, g:g + 1] += jnp.sum(dm, axis=1, keepdims=True)
                rows.append(jnp.concatenate(cols, axis=1))
            d_vn = jnp.concatenate(rows, axis=0)
            dlg_ref[...] += jnp.sum(d_vn * vhat, axis=0, keepdims=True)
            dlb_ref[...] += jnp.sum(d_vn, axis=0, keepdims=True)
            d_vhat = d_vn * lg
            d_vg = rstd * (d_vhat - jnp.mean(d_vhat, axis=-1, keepdims=True)
                           - vhat * jnp.mean(d_vhat * vhat, axis=-1, keepdims=True))
            d_zv = d_vg * _dgelu(zv)
            dz_ref[...] = jnp.concatenate([d_zo, d_zu, d_zv], axis=1).astype(BF16)

    return pl.pallas_call(
        body, name="mixer_bwd", grid=(nt,),
        in_specs=[pl.BlockSpec(memory_space=pl.ANY), _rb(TR, D), _rb(TR, D), _rb(TR, D, CB_O), _rb(TR, D, CB_U),
                  _rb(TR, D, CB_VG), pl.BlockSpec((2, TR, D), lambda i: (0, xrow(i), 0)),
                  _bc(1, D), _bc(1, D), _bc(1, D), _bc(SG * SCH, SCH), _bc(SCH, LANE)],
        out_specs=[_rb(TR, 3 * D), pl.BlockSpec((TR, D), lambda i: (xrow(i), 0)),
                   _bc(1, D), _bc(1, D), _bc(1, D), _bc(SG * SCH, SCH), _bc(SCH, LANE)],
        out_shape=[jax.ShapeDtypeStruct(dz.shape, BF16), jax.ShapeDtypeStruct((t, D), F32),
                   jax.ShapeDtypeStruct((1, D), F32), jax.ShapeDtypeStruct((1, D), F32),
                   jax.ShapeDtypeStruct((1, D), F32), jax.ShapeDtypeStruct((SG * SCH, SCH), F32),
                   jax.ShapeDtypeStruct((SCH, LANE), F32)],
        input_output_aliases={0: 0},
        compiler_params=_cparams(("arbitrary",)),
    )(dz, hf, hb, zmain, zmain, zmain, dyms, gh, ln_g, ln_b, ws, bs_t)


FCB = DFF // 2
TF = 512


def _ffn_halo(col, t):
    per = TF // GW
    last = t // GW - 1
    prev = pl.BlockSpec((GW, FCB), lambda i, j: (jnp.maximum(i * per - 1, 0), col(j)))
    nxt = pl.BlockSpec((GW, FCB), lambda i, j: (jnp.minimum((i + 1) * per, last), col(j)))
    return prev, nxt


def _conv_taps(ext):
    n = ext.shape[0]
    colid = lax.broadcasted_iota(jnp.int32, (TF, 1), 0) % GW
    views = (pltpu.roll(ext, 1, 0), ext, pltpu.roll(ext, n - 1, 0))
    taps = {}
    for ky in range(3):
        base = GW * ky
        for kx in range(3):
            v = views[kx][base:base + TF]
            if kx == 0:
                v = jnp.where(colid >= 1, v, 0.0)
            elif kx == 2:
                v = jnp.where(colid <= GW - 2, v, 0.0)
            taps[(ky, kx)] = v
    return taps


def _ext(c_ref, p_ref, n_ref, i, nt):
    pr = jnp.where(i == 0, 0.0, p_ref[...].astype(F32))
    nx = jnp.where(i == nt - 1, 0.0, n_ref[...].astype(F32))
    return jnp.concatenate([pr, c_ref[...].astype(F32), nx], axis=0)


def ffn_act_fwd(up, wc):
    t = up.shape[0]
    nt = t // TF
    prev_s, next_s = _ffn_halo(lambda j: j, t)

    def body(a_ref, ap_ref, an_ref, b_ref, w_ref, o_ref):
        i = pl.program_id(0)
        taps = _conv_taps(_ext(a_ref, ap_ref, an_ref, i, nt))
        ac = sum(w_ref[3 * ky + kx:3 * ky + kx + 1, :] * taps[(ky, kx)] for ky in range(3) for kx in range(3))
        o_ref[...] = (_silu(ac) * b_ref[...].astype(F32)).astype(BF16)

    return pl.pallas_call(
        body, name="ffn_act_fwd", grid=(nt, 2),
        in_specs=[pl.BlockSpec((TF, FCB), lambda i, j: (i, j)), prev_s, next_s,
                  pl.BlockSpec((TF, FCB), lambda i, j: (i, 2 + j)), pl.BlockSpec((16, FCB), lambda i, j: (0, j))],
        out_specs=pl.BlockSpec((TF, FCB), lambda i, j: (i, j)),
        out_shape=jax.ShapeDtypeStruct((t, DFF), BF16),
        compiler_params=_cparams(("parallel", "parallel")),
    )(up, up, up, up, wc)


def ffn_act_bwd(up, dact, wc):
    t = up.shape[0]
    nt = t // TF
    prev_s, next_s = _ffn_halo(lambda j: j, t)

    def body(a_ref, ap_ref, an_ref, b_ref, da_ref, w_ref, dup_ref, dac_ref):
        i = pl.program_id(0)
        taps = _conv_taps(_ext(a_ref, ap_ref, an_ref, i, nt))
        ac = sum(w_ref[3 * ky + kx:3 * ky + kx + 1, :] * taps[(ky, kx)] for ky in range(3) for kx in range(3))
        da = da_ref[...].astype(F32)
        dup_ref[...] = (da * _silu(ac)).astype(BF16)
        dac_ref[...] = (da * b_ref[...].astype(F32) * _dsilu(ac)).astype(BF16)

    return pl.pallas_call(
        body, name="ffn_act_bwd", grid=(nt, 2),
        in_specs=[pl.BlockSpec((TF, FCB), lambda i, j: (i, j)), prev_s, next_s,
                  pl.BlockSpec((TF, FCB), lambda i, j: (i, 2 + j)), pl.BlockSpec((TF, FCB), lambda i, j: (i, j)),
                  pl.BlockSpec((16, FCB), lambda i, j: (0, j))],
        out_specs=[pl.BlockSpec((TF, FCB), lambda i, j: (i, 2 + j)), pl.BlockSpec((TF, FCB), lambda i, j: (i, j))],
        out_shape=[jax.ShapeDtypeStruct((t, 2 * DFF), BF16), jax.ShapeDtypeStruct((t, DFF), BF16)],
        compiler_params=_cparams(("parallel", "parallel")),
    )(up, up, up, up, dact, wc)


def ffn_conv_bwd(dup, up, dac, wc):
    t = up.shape[0]
    nt = t // TF
    prev_a, next_a = _ffn_halo(lambda j: j, t)

    def body(dup_in_ref, a_ref, ap_ref, an_ref, g_ref, gp_ref, gn_ref, w_ref, o_ref, dw_ref):
        del dup_in_ref
        i = pl.program_id(1)

        @pl.when(i == 0)
        def _():
            dw_ref[...] = jnp.zeros_like(dw_ref)

        gtaps = _conv_taps(_ext(g_ref, gp_ref, gn_ref, i, nt))
        o_ref[...] = sum(w_ref[3 * (2 - ky) + (2 - kx):3 * (2 - ky) + (2 - kx) + 1, :] * gtaps[(ky, kx)]
                         for ky in range(3) for kx in range(3)).astype(BF16)
        ataps = _conv_taps(_ext(a_ref, ap_ref, an_ref, i, nt))
        g = g_ref[...].astype(F32)
        for ky in range(3):
            for kx in range(3):
                k = 3 * ky + kx
                dw_ref[k:k + 1, :] += jnp.sum(ataps[(ky, kx)] * g, axis=0, keepdims=True)

    sw = lambda s: pl.BlockSpec(s.block_shape, lambda j, i, f=s.index_map: f(i, j))
    return pl.pallas_call(
        body, name="ffn_conv_bwd", grid=(2, nt),
        in_specs=[pl.BlockSpec(memory_space=pl.ANY),
                  pl.BlockSpec((TF, FCB), lambda j, i: (i, j)), sw(prev_a), sw(next_a),
                  pl.BlockSpec((TF, FCB), lambda j, i: (i, j)), sw(prev_a), sw(next_a),
                  pl.BlockSpec((16, FCB), lambda j, i: (0, j))],
        out_specs=[pl.BlockSpec((TF, FCB), lambda j, i: (i, j)), pl.BlockSpec((16, FCB), lambda j, i: (0, j))],
        out_shape=[jax.ShapeDtypeStruct(dup.shape, BF16), jax.ShapeDtypeStruct((16, DFF), F32)],
        input_output_aliases={0: 0},
        compiler_params=_cparams(("arbitrary", "arbitrary")),
    )(dup, up, up, up, dac, dac, dac, wc)


def head_fwd_bwd(h1, f, target, gfin, tab):
    t = h1.shape[0]

    def body(h1_ref, f_ref, t_ref, g_ref, tab_ref, dh2_ref, df_ref, acc_ref):
        i = pl.program_id(0)

        @pl.when(i == 0)
        def _():
            acc_ref[...] = jnp.zeros_like(acc_ref)

        gate = tab_ref[0:1, :]
        fv = f_ref[...]
        h2 = h1_ref[...] + gate * fv
        r = lax.rsqrt(jnp.mean(h2 * h2, axis=-1, keepdims=True) + EPS)
        xh = h2 * r
        gv = g_ref[...]
        err = xh * gv - t_ref[...]
        acc_ref[0:1, :] += jnp.sum(0.5 * jnp.mean(err * err, axis=-1, keepdims=True), axis=0, keepdims=True)
        dy = err * (1.0 / D)
        acc_ref[1:2, :] += jnp.sum(dy * xh, axis=0, keepdims=True)
        dxh = dy * gv
        dh2 = r * (dxh - xh * jnp.mean(dxh * xh, axis=-1, keepdims=True))
        dh2_ref[...] = dh2
        acc_ref[2:3, :] += jnp.sum(dh2 * fv, axis=0, keepdims=True)
        df_ref[...] = (dh2 * gate).astype(BF16)

    return pl.pallas_call(
        body, name="head_fwd_bwd", grid=(t // TR,),
        in_specs=[_rb(TR, D), _rb(TR, D), _rb(TR, D), _bc(1, D), _bc(8, D)],
        out_specs=[_rb(TR, D), _rb(TR, D), _bc(8, D)],
        out_shape=[jax.ShapeDtypeStruct((t, D), F32), jax.ShapeDtypeStruct((t, D), BF16),
                   jax.ShapeDtypeStruct((8, D), F32)],
        compiler_params=_cparams(("arbitrary",)),
    )(h1, f, target, gfin, tab)


def norm2_bwd(h1, dhn2, dh2, out, g, tab):
    t = h1.shape[0]

    def body(h1_ref, dhn_ref, dh2_ref, out_ref, g_ref, tab_ref, dh1_ref, dout_ref, acc_ref):
        i = pl.program_id(0)

        @pl.when(i == 0)
        def _():
            acc_ref[...] = jnp.zeros_like(acc_ref)

        h1v = h1_ref[...]
        r = lax.rsqrt(jnp.mean(h1v * h1v, axis=-1, keepdims=True) + EPS)
        xh = h1v * r
        gv = g_ref[...]
        dhn = dhn_ref[...]
        acc_ref[0:1, :] += jnp.sum(dhn, axis=0, keepdims=True)
        acc_ref[1:2, :] += jnp.sum(dhn * xh * gv, axis=0, keepdims=True)
        dn = dhn * (1.0 + tab_ref[2:3, :])
        acc_ref[2:3, :] += jnp.sum(dn * xh, axis=0, keepdims=True)
        dxh = dn * gv
        dh1 = dh2_ref[...] + r * (dxh - xh * jnp.mean(dxh * xh, axis=-1, keepdims=True))
        dh1_ref[...] = dh1
        acc_ref[3:4, :] += jnp.sum(dh1 * out_ref[...], axis=0, keepdims=True)
        dout_ref[...] = (dh1 * tab_ref[0:1, :]).astype(BF16)

    return pl.pallas_call(
        body, name="norm2_bwd", grid=(t // TR,),
        in_specs=[_rb(TR, D), _rb(TR, D), _rb(TR, D), _rb(TR, D), _bc(1, D), _bc(8, D)],
        out_specs=[_rb(TR, D), _rb(TR, D), _bc(8, D)],
        out_shape=[jax.ShapeDtypeStruct((t, D), F32), jax.ShapeDtypeStruct((t, D), BF16),
                   jax.ShapeDtypeStruct((8, D), F32)],
        compiler_params=_cparams(("arbitrary",)),
    )(h1, dhn2, dh2, out, g, tab)


def norm1_bwd(xcat, da, db, dh1, g, tab, n_ctx_tiles):
    tp = xcat.shape[0]
    t = tp - n_ctx_tiles * TR
    xrow = lambda i: jnp.maximum(i - n_ctx_tiles, 0)

    def body(x_ref, da_ref, db_ref, dh1_ref, g_ref, tab_ref, dx_ref, acc_ref):
        i = pl.program_id(0)

        @pl.when(i == 0)
        def _():
            acc_ref[...] = jnp.zeros_like(acc_ref)

        x = x_ref[...]
        r = lax.rsqrt(jnp.mean(x * x, axis=-1, keepdims=True) + EPS)
        xh = x * r
        gv = g_ref[...]
        dhn = da_ref[...] + db_ref[...]
        s_shift = jnp.sum(dhn, axis=0, keepdims=True)
        s_scale = jnp.sum(dhn * xh * gv, axis=0, keepdims=True)
        is_ctx = i < n_ctx_tiles

        @pl.when(is_ctx)
        def _():
            acc_ref[0:1, :] += s_shift
            acc_ref[1:2, :] += s_scale

        @pl.when(jnp.logical_not(is_ctx))
        def _():
            acc_ref[2:3, :] += s_shift
            acc_ref[3:4, :] += s_scale

        acc_ref[5:6, :] += s_shift
        acc_ref[6:7, :] += s_scale
        sc = jnp.where(is_ctx, tab_ref[1:2, :], tab_ref[3:4, :])
        dn = dhn * (1.0 + sc)
        acc_ref[4:5, :] += jnp.sum(dn * xh, axis=0, keepdims=True)
        dxh = dn * gv
        dx_ref[...] = dh1_ref[...] + r * (dxh - xh * jnp.mean(dxh * xh, axis=-1, keepdims=True))

    return pl.pallas_call(
        body, name="norm1_bwd", grid=(tp // TR,),
        in_specs=[_rb(TR, D), _rb(TR, D), _rb(TR, D), pl.BlockSpec((TR, D), lambda i: (xrow(i), 0)),
                  _bc(1, D), _bc(8, D)],
        out_specs=[pl.BlockSpec((TR, D), lambda i: (xrow(i), 0)), _bc(8, D)],
        out_shape=[jax.ShapeDtypeStruct((t, D), F32), jax.ShapeDtypeStruct((8, D), F32)],
        compiler_params=_cparams(("arbitrary",)),
    )(xcat, da, db, dh1, g, tab)


def adamw(w, g, m, v, name):
    rows, cols = w.shape
    tm = rows if rows * cols <= 256 * 1024 else _pick(rows, (256, 176, 128, 64, 8))
    c1 = 1.0 / (1.0 - ADAM_B1 ** ADAM_STEP)
    c2 = 1.0 / (1.0 - ADAM_B2 ** ADAM_STEP)

    def body(w_ref, g_ref, m_ref, v_ref, d_ref, mo_ref, vo_ref):
        gv = g_ref[...]
        mn = ADAM_B1 * m_ref[...] + (1.0 - ADAM_B1) * gv
        vn = ADAM_B2 * v_ref[...] + (1.0 - ADAM_B2) * (gv * gv)
        mo_ref[...] = mn
        vo_ref[...] = vn
        d_ref[...] = -ADAM_LR * ((mn * c1) / (jnp.sqrt(vn * c2) + ADAM_EPS) + ADAM_WD * w_ref[...])

    spec = pl.BlockSpec((tm, cols), lambda i: (i, 0))
    sds = jax.ShapeDtypeStruct((rows, cols), F32)
    return pl.pallas_call(
        body, name=name, grid=(rows // tm,),
        in_specs=[spec] * 4, out_specs=[spec] * 3, out_shape=[sds] * 3,
        compiler_params=_cparams(("parallel",)),
    )(w, g, m, v)


def add_n(arrs, out_dtype, name):
    shp = arrs[0].shape
    cols = shp[-1]
    flat = [a.reshape(-1, cols) for a in arrs]
    rows = flat[0].shape[0]
    tm = max(t for t in range(16, rows + 1, 16) if rows % t == 0 and t * cols <= 512 * 1024)

    def body(*refs):
        acc = refs[0][...].astype(F32)
        for r in refs[1:-1]:
            acc = acc + r[...].astype(F32)
        refs[-1][...] = acc.astype(refs[-1].dtype)

    spec = pl.BlockSpec((tm, cols), lambda i: (i, 0))
    out = pl.pallas_call(
        body, name=name, grid=(rows // tm,),
        in_specs=[spec] * len(flat), out_specs=spec, out_shape=jax.ShapeDtypeStruct((rows, cols), out_dtype),
        compiler_params=_cparams(("parallel",)),
    )(*flat)
    return out.reshape(shp)


def sum8(stack, name):
    _, rows, cols = stack.shape
    tm = rows if rows <= 2048 else _pick(rows, (512, 256, 128, 64, 8))

    def body(s_ref, o_ref):
        acc = s_ref[0]
        for k in range(1, 8):
            acc = acc + s_ref[k]
        o_ref[...] = acc

    return pl.pallas_call(
        body, name=name, grid=(rows // tm,),
        in_specs=[pl.BlockSpec((8, tm, cols), lambda i: (0, i, 0))],
        out_specs=pl.BlockSpec((tm, cols), lambda i: (i, 0)),
        out_shape=jax.ShapeDtypeStruct((rows, cols), F32),
        compiler_params=_cparams(("parallel",)),
    )(stack)


def _coords():
    return lax.axis_index("x"), lax.axis_index("y"), lax.axis_index("c")


def _other_chips(x, y):
    return [(1 - x, y), (x, 1 - y), (1 - x, 1 - y)]


_ANY = pl.BlockSpec(memory_space=pl.ANY)


def gather_chips(slabs):
    ns = len(slabs)

    def body(*refs):
        x_refs, out_refs = refs[:ns], refs[ns:2 * ns]
        send_sems, recv_sems = refs[2 * ns:]
        x, y, c = _coords()
        me = 2 * x + y
        sibling = (x, y, 1 - c)
        chips = _other_chips(x, y)

        def half(s, chip, hc):
            rh = slabs[s].shape[0] // 2
            return out_refs[s].at[chip, pl.ds(hc * rh, rh), :]

        def own_half(s):
            rh = slabs[s].shape[0] // 2
            return x_refs[s].at[pl.ds(c * rh, rh), :]

        sends = []
        for s in range(ns):
            for j, (px, py) in enumerate(chips):
                cp = pltpu.make_async_remote_copy(
                    src_ref=own_half(s), dst_ref=half(s, me, c), send_sem=send_sems.at[6 * s + j],
                    recv_sem=recv_sems.at[6 * s + j], device_id=(px, py, c), device_id_type=MESH)
                cp.start()
                sends.append(cp)
        for s in range(ns):
            for j, (px, py) in enumerate(chips):
                src = 2 * px + py
                landed = pltpu.make_async_remote_copy(
                    src_ref=half(s, src, c), dst_ref=half(s, src, c), send_sem=send_sems.at[6 * s + j],
                    recv_sem=recv_sems.at[6 * s + j], device_id=(px, py, c), device_id_type=MESH)
                landed.wait_recv()
                fw = pltpu.make_async_remote_copy(
                    src_ref=half(s, src, c), dst_ref=half(s, src, c), send_sem=send_sems.at[6 * s + 3 + j],
                    recv_sem=recv_sems.at[6 * s + 3 + j], device_id=sibling, device_id_type=MESH)
                fw.start()
                sends.append(fw)
        for s in range(ns):
            for j, (px, py) in enumerate(chips):
                src = 2 * px + py
                got = pltpu.make_async_remote_copy(
                    src_ref=half(s, src, 1 - c), dst_ref=half(s, src, 1 - c), send_sem=send_sems.at[6 * s + 3 + j],
                    recv_sem=recv_sems.at[6 * s + 3 + j], device_id=sibling, device_id_type=MESH)
                got.wait_recv()
        for cp in sends:
            cp.wait_send()

    return pl.pallas_call(
        body, name="gather_chips",
        in_specs=[_ANY] * ns, out_specs=[_ANY] * ns,
        out_shape=[jax.ShapeDtypeStruct((4,) + s.shape, s.dtype) for s in slabs],
        scratch_shapes=[pltpu.SemaphoreType.DMA((6 * ns,)), pltpu.SemaphoreType.DMA((6 * ns,))],
    )(*slabs)


def swap_halves(gss):
    ns = len(gss)

    def body(*refs):
        g_refs, out_refs = refs[:ns], refs[ns:2 * ns]
        send_sems, recv_sems = refs[2 * ns:]
        x, y, c = _coords()
        cps = []
        for s in range(ns):
            rh = gss[s].shape[1] // 2
            cp = pltpu.make_async_remote_copy(
                src_ref=g_refs[s].at[:, pl.ds((1 - c) * rh, rh), :], dst_ref=out_refs[s],
                send_sem=send_sems.at[s], recv_sem=recv_sems.at[s], device_id=(x, y, 1 - c), device_id_type=MESH)
            cp.start()
            cps.append(cp)
        for cp in cps:
            cp.wait()

    return pl.pallas_call(
        body, name="swap_halves",
        in_specs=[_ANY] * ns, out_specs=[_ANY] * ns,
        out_shape=[jax.ShapeDtypeStruct((4, g.shape[1] // 2, g.shape[2]), g.dtype) for g in gss],
        scratch_shapes=[pltpu.SemaphoreType.DMA((ns,)), pltpu.SemaphoreType.DMA((ns,))],
    )(*gss)


def scatter_chips(pbs):
    ns = len(pbs)

    def body(*refs):
        p_refs, out_refs = refs[:ns], refs[ns:2 * ns]
        send_sems, recv_sems = refs[2 * ns:]
        x, y, c = _coords()
        cps = []
        for s in range(ns):
            for j, (px, py) in enumerate(_other_chips(x, y)):
                cp = pltpu.make_async_remote_copy(
                    src_ref=p_refs[s].at[2 * px + py], dst_ref=out_refs[s].at[j], send_sem=send_sems.at[3 * s + j],
                    recv_sem=recv_sems.at[3 * s + j], device_id=(px, py, c), device_id_type=MESH)
                cp.start()
                cps.append(cp)
        for cp in cps:
            cp.wait()

    return pl.pallas_call(
        body, name="scatter_chips",
        in_specs=[_ANY] * ns, out_specs=[_ANY] * ns,
        out_shape=[jax.ShapeDtypeStruct((3,) + p.shape[1:], p.dtype) for p in pbs],
        scratch_shapes=[pltpu.SemaphoreType.DMA((3 * ns,)), pltpu.SemaphoreType.DMA((3 * ns,))],
    )(*pbs)


def join_halves(reds):
    ns = len(reds)

    def body(*refs):
        r_refs, out_refs = refs[:ns], refs[ns:2 * ns]
        send_sems, recv_sems = refs[2 * ns:]
        x, y, c = _coords()
        cps = []
        for s in range(ns):
            cp = pltpu.make_async_remote_copy(
                src_ref=r_refs[s], dst_ref=out_refs[s], send_sem=send_sems.at[s], recv_sem=recv_sems.at[s],
                device_id=(x, y, 1 - c), device_id_type=MESH)
            cp.start()
            cps.append(cp)
        for cp in cps:
            cp.wait()

    return pl.pallas_call(
        body, name="join_halves",
        in_specs=[_ANY] * ns, out_specs=[_ANY] * ns,
        out_shape=[jax.ShapeDtypeStruct(r.shape, r.dtype) for r in reds],
        scratch_shapes=[pltpu.SemaphoreType.DMA((ns,)), pltpu.SemaphoreType.DMA((ns,))],
    )(*reds)


def gather_all(vec, name):
    r, wd = vec.shape

    def body(v_ref, out_ref, send_sems, recv_sems):
        x, y, c = _coords()
        me = 4 * x + 2 * y + c
        cps = []
        for k in range(1, 8):
            mx, my, mc = (k >> 2) & 1, (k >> 1) & 1, k & 1
            peer = (x ^ mx, y ^ my, c ^ mc)
            cp = pltpu.make_async_remote_copy(
                src_ref=v_ref, dst_ref=out_ref.at[me],
                send_sem=send_sems.at[k - 1], recv_sem=recv_sems.at[k - 1], device_id=peer, device_id_type=MESH)
            cp.start()
            cps.append(cp)
        for k in range(1, 8):
            mx, my, mc = (k >> 2) & 1, (k >> 1) & 1, k & 1
            peer = (x ^ mx, y ^ my, c ^ mc)
            src = 4 * peer[0] + 2 * peer[1] + peer[2]
            got = pltpu.make_async_remote_copy(
                src_ref=v_ref, dst_ref=out_ref.at[src],
                send_sem=send_sems.at[k - 1], recv_sem=recv_sems.at[k - 1], device_id=peer, device_id_type=MESH)
            got.wait_recv()
        for cp in cps:
            cp.wait_send()

    return pl.pallas_call(
        body, name=name,
        in_specs=[_ANY], out_specs=_ANY,
        out_shape=jax.ShapeDtypeStruct((8, r, wd), vec.dtype),
        scratch_shapes=[pltpu.SemaphoreType.DMA((7,)), pltpu.SemaphoreType.DMA((7,))],
    )(vec)


def _pad_rows(a, rows):
    return jnp.pad(a, ((0, rows - a.shape[0]), (0, 0)))


def _pad_cols(a, cols):
    return jnp.pad(a, ((0, 0), (0, cols - a.shape[1])))


def local_step(x, c, ctx, c_ctx, target, wt, sm):
    t, tc = x.shape[0], ctx.shape[0]
    tp = t + tc
    nct = tc // TR
    ncc = tc // LC
    nc = tp // LC

    w_in = wt["w_in"]
    segs = {"q": (0, D), "k": (D, 2 * D), "v": (2 * D, 3 * D), "g": (3 * D, 3 * D + NGATE)}
    base = 3 * D + NGATE
    for n_i, nm in enumerate(("o", "u", "vg", "gm", "gg")):
        segs[nm] = (base + n_i * D, base + (n_i + 1) * D)
    order = ("o", "u", "vg", "gm", "gg", "v", "q", "k")
    w_main = jnp.concatenate([w_in[:, segs[nm][0]:segs[nm][1]] for nm in order], axis=1)
    w_g = _pad_cols(w_in[:, segs["g"][0]:segs["g"][1]], LANE)
    w_main_t = w_main.T
    w_g_t = w_g.T

    cc = _pad_rows(jnp.concatenate([c.reshape(1, D), c_ctx.reshape(1, D)], axis=0), 16)
    modv = mod_fwd(cc, wt["w_mod"], sm["b_mod"].reshape(1, NMOD * D))
    mx = modv[0].reshape(NMOD, D)
    mc = modv[1].reshape(NMOD, D)
    tab1 = _pad_rows(jnp.stack([mc[0], mc[1], mx[0], mx[1]]), 8)
    tab2 = _pad_rows(jnp.stack([mx[2], mx[3], mx[4]]), 8)
    tab3 = _pad_rows(mx[5:6], 8)

    g1 = sm["norm1_g"].reshape(1, D)
    g2 = sm["norm2_g"].reshape(1, D)
    gfin = sm["final_g"].reshape(1, D)
    gh = sm["head_norm_g"].reshape(1, D)
    ln_g = sm["sgu_ln_g"].reshape(1, D)
    ln_b = sm["sgu_ln_b"].reshape(1, D)
    ws = sm["w_s"].reshape(SG * SCH, SCH).astype(BF16)
    bs_t = _pad_cols(sm["b_s"].reshape(SG, SCH).T, LANE)
    conv_w = _pad_rows(sm["conv_qk"].reshape(3, 2 * D), 8)
    b_gate = _pad_cols(sm["b_gate"].reshape(1, NGATE), LANE)
    wc = _pad_rows(sm["w_ffn_conv"].reshape(9, DFF), 16)

    xcat = jnp.concatenate([ctx, x], axis=0)
    hn1 = norm1_fwd(xcat, g1, tab1, nct)
    zmain = mm_nn(hn1, w_main, BF16, "mm_zmain")
    zg = mm_nn(hn1, w_g, F32, "mm_zg")
    qa, ka, gcol, zgb = qkconv_fwd(zmain, zg, conv_w, b_gate, nct)
    grow = gcol[:, :16].reshape(nc, LC, 16).transpose(0, 2, 1)
    hf, hb, csf, csb, nsf, nsb, msf, msb = mlstm_fwd(qa, ka, zmain, gcol, grow, ncc)
    yms = mixer_fwd(hf, hb, zmain, gh, ln_g, ln_b, ws, bs_t, nct)
    w_br = jnp.stack([wt["w_branch_mlstm"], wt["w_branch_sgu"]])
    pp = mm_nn(yms, w_br, BF16, "mm_branch")
    y = merge_fwd(zmain, pp, nct)
    out = mm_nn(y, wt["w_out"], F32, "mm_out")
    h1, hn2 = norm2_fwd(x, out, g2, tab2)
    up = mm_nn(hn2, wt["w_up"], BF16, "mm_up")
    act = ffn_act_fwd(up, wc)
    f = mm_nn(act, wt["w_down"], F32, "mm_down")
    dh2, df, acc_h = head_fwd_bwd(h1, f, target, gfin, tab3)
    loss = acc_h[0, 0]

    g_w_down = mm_tn(act, df, "mmt_down", BF16)
    dact = mm_nn(df, wt["w_down"].T, BF16, "mm_ddown")
    dup, dac = ffn_act_bwd(up, dact, wc)
    dup, g_wc = ffn_conv_bwd(dup, up, dac, wc)
    g_w_up = mm_tn(hn2, dup, "mmt_up", BF16)
    dhn2 = mm_nn(dup, wt["w_up"].T, F32, "mm_dup")
    dh1, dout, acc_2 = norm2_bwd(h1, dhn2, dh2, out, g2, tab2)
    g_w_out = mm_tn(y, dout, "mmt_out", BF16)
    dy = mm_nn(dout, wt["w_out"].T, BF16, "mm_dout")
    dpp, dz = merge_bwd(zmain, pp, dy, tp, nct)
    g_w_br = mm_tn(yms, dpp, "mmt_branch", BF16)
    dyms = mm_nn(dpp, jnp.stack([wt["w_branch_mlstm"].T, wt["w_branch_sgu"].T]), BF16, "mm_dbranch")
    dz, dhm, g_gh, g_lng, g_lnb, g_ws, g_bs = mixer_bwd(dz, hf, hb, zmain, dyms, gh, ln_g, ln_b, ws, bs_t, nct)
    (dqf, dkf, dvf, colf, rowf, dqb, dkb, dvb, colb, rowb) = mlstm_bwd(
        qa, ka, zmain, gcol, grow, (csf, csb, nsf, nsb, msf, msb), dhm, ncc)

    csum_f = rowf[:, :4, :].transpose(0, 2, 1).reshape(tp, 4)
    csum_b = rowb[:, :4, :].transpose(0, 2, 1).reshape(tp, 4)
    z4 = jnp.zeros((tp, 4), F32)

    def lanes(parts):
        return _pad_cols(jnp.concatenate(parts, axis=1), LANE)

    rg = lanes([z4, colf[:, 0:4], z4, colb[:, 0:4]])
    csi = lanes([csum_f, z4, csum_b, z4])
    csf_l = lanes([z4, csum_f, z4, csum_b])
    ali = lanes([colf[:, 4:8], z4, colb[:, 4:8], z4])
    gam = lanes([z4, colf[:, 8:12], z4, colb[:, 8:12]])
    dzg, g_bgate = gates_bwd(rg, gam, csi, csf_l, ali, zgb)

    dc, g_convw = qkconv_bwd_a(zmain, dqf, dqb, dkf, dkb, conv_w, nct)
    dz = qkconv_bwd_b(dz, dc, conv_w, nct)
    dz = add_into_dz(dz, dvf, dvb, CB_V)

    g_w_main = mm_tn(hn1, dz, "mmt_main", BF16)
    g_w_g = mm_tn(hn1, dzg, "mmt_g", BF16)
    da = mm_nn(dz, w_main_t, F32, "mm_dmain")
    db = mm_nn(dzg, w_g_t, F32, "mm_dg")
    grad_x, acc_1 = norm1_bwd(xcat, da, db, dh1, g1, tab1, nct)

    blk = lambda cb: g_w_main[:, cb * D:(cb + 1) * D]
    g_w_in = jnp.concatenate([blk(CB_Q), blk(CB_K), blk(CB_V), g_w_g[:, :NGATE], blk(CB_O), blk(CB_U), blk(CB_VG),
                              blk(CB_GM), blk(CB_GG)], axis=1)

    d_modx = jnp.concatenate([acc_1[2], acc_1[3], acc_2[3], acc_2[0], acc_2[1], acc_h[2]])
    d_modc = jnp.concatenate([acc_1[0], acc_1[1], jnp.zeros((4 * D,), F32)])
    d_modb = jnp.concatenate([acc_1[5], acc_1[6], acc_2[3], acc_2[0], acc_2[1], acc_h[2]])

    big = {"w_in": g_w_in, "w_branch_mlstm": g_w_br[0], "w_branch_sgu": g_w_br[1], "w_out": g_w_out,
           "w_up": g_w_up, "w_down": g_w_down}
    small = {"b_mod": d_modb, "norm1_g": acc_1[4], "b_gate": g_bgate[0, :NGATE], "conv_qk": g_convw[:3].reshape(-1),
             "head_norm_g": g_gh[0], "sgu_ln_g": g_lng[0], "sgu_ln_b": g_lnb[0], "w_s": g_ws.reshape(-1),
             "b_s": g_bs[:, :SG].T.reshape(-1), "norm2_g": acc_2[2], "w_ffn_conv": g_wc[:9].reshape(-1),
             "final_g": acc_h[1]}
    return loss, grad_x, big, small, d_modx, d_modc


def mod_bwd_w(a_all, dm_all, name):
    n = dm_all.shape[1]
    tn = _pick(n, (512, 128))

    def body(a_ref, d_ref, o_ref):
        o_ref[...] = _dot_tn(_silu(a_ref[...]), d_ref[...])

    return pl.pallas_call(
        body, name=name, grid=(n // tn,),
        in_specs=[_bc(16, D), pl.BlockSpec((16, tn), lambda j: (0, j))],
        out_specs=pl.BlockSpec((D, tn), lambda j: (0, j)),
        out_shape=jax.ShapeDtypeStruct((D, n), F32),
        compiler_params=_cparams(("parallel",)),
    )(a_all, dm_all)


def mod_bwd_cctx(dmc, w_mod_t, c_ctx):
    def body(d_ref, w_ref, c_ref, o_ref):
        o_ref[...] = _dot(d_ref[...], w_ref[...]) * _dsilu(c_ref[...])

    return pl.pallas_call(
        body, name="mod_bwd_cctx", grid=(1,),
        in_specs=[_bc(16, 2 * D), _bc(2 * D, D), _bc(1, D)],
        out_specs=_bc(16, D),
        out_shape=jax.ShapeDtypeStruct((16, D), F32),
        compiler_params=_cparams(("arbitrary",)),
    )(dmc, w_mod_t, c_ctx)


BIG = ("w_mod", "w_in", "w_branch_mlstm", "w_branch_sgu", "w_out", "w_up", "w_down")
BIG_AXIS = {"w_mod": 1, "w_in": 1, "w_branch_mlstm": 0, "w_branch_sgu": 0, "w_out": 0, "w_up": 1, "w_down": 0}
SMALL = ("c_ctx", "b_mod", "norm1_g", "b_gate", "conv_qk", "head_norm_g", "sgu_ln_g", "sgu_ln_b", "w_s", "b_s",
         "norm2_g", "w_ffn_conv", "final_g")
SMALL_SHARDED = {"conv_qk": (3, 2 * D), "w_ffn_conv": (9, DFF)}
PACK_ALIGN = 32 * D


def _pack(arrs, dtype, align=PACK_ALIGN, width=D):
    flat = jnp.concatenate([a.reshape(-1).astype(dtype) for a in arrs])
    n = flat.shape[0]
    padded = -(-n // align) * align
    return jnp.pad(flat, (0, padded - n)).reshape(padded // width, width)


def _unpack(slab, shapes):
    flat = slab.reshape(-1)
    outs, off = [], 0
    for shp in shapes:
        n = math.prod(shp)
        outs.append(flat[off:off + n].reshape(shp))
        off += n
    return outs


def _round_up(n, m):
    return -(-n // m) * m


def kernel(x, c, ctx, c_ctx, w_mod, b_mod, norm1_g, w_in, b_gate, conv_qk, head_norm_g, sgu_ln_g, sgu_ln_b, w_s, b_s, w_branch_mlstm, w_branch_sgu, w_out, norm2_g, w_up, w_ffn_conv, w_down, final_g, loss_target, m_c_ctx, m_w_mod, m_b_mod, m_norm1_g, m_w_in, m_b_gate, m_conv_qk, m_head_norm_g, m_sgu_ln_g, m_sgu_ln_b, m_w_s, m_b_s, m_w_branch_mlstm, m_w_branch_sgu, m_w_out, m_norm2_g, m_w_up, m_w_ffn_conv, m_w_down, m_final_g, v_c_ctx, v_w_mod, v_b_mod, v_norm1_g, v_w_in, v_b_gate, v_conv_qk, v_head_norm_g, v_sgu_ln_g, v_sgu_ln_b, v_w_s, v_b_s, v_w_branch_mlstm, v_w_branch_sgu, v_w_out, v_norm2_g, v_w_up, v_w_ffn_conv, v_w_down, v_final_g):
    params = dict(c_ctx=c_ctx, w_mod=w_mod, b_mod=b_mod, norm1_g=norm1_g, w_in=w_in, b_gate=b_gate, conv_qk=conv_qk,
                  head_norm_g=head_norm_g, sgu_ln_g=sgu_ln_g, sgu_ln_b=sgu_ln_b, w_s=w_s, b_s=b_s,
                  w_branch_mlstm=w_branch_mlstm, w_branch_sgu=w_branch_sgu, w_out=w_out, norm2_g=norm2_g, w_up=w_up,
                  w_ffn_conv=w_ffn_conv, w_down=w_down, final_g=final_g)
    mom_m = dict(c_ctx=m_c_ctx, w_mod=m_w_mod, b_mod=m_b_mod, norm1_g=m_norm1_g, w_in=m_w_in, b_gate=m_b_gate,
                 conv_qk=m_conv_qk, head_norm_g=m_head_norm_g, sgu_ln_g=m_sgu_ln_g, sgu_ln_b=m_sgu_ln_b, w_s=m_w_s,
                 b_s=m_b_s, w_branch_mlstm=m_w_branch_mlstm, w_branch_sgu=m_w_branch_sgu, w_out=m_w_out,
                 norm2_g=m_norm2_g, w_up=m_w_up, w_ffn_conv=m_w_ffn_conv, w_down=m_w_down, final_g=m_final_g)
    mom_v = dict(c_ctx=v_c_ctx, w_mod=v_w_mod, b_mod=v_b_mod, norm1_g=v_norm1_g, w_in=v_w_in, b_gate=v_b_gate,
                 conv_qk=v_conv_qk, head_norm_g=v_head_norm_g, sgu_ln_g=v_sgu_ln_g, sgu_ln_b=v_sgu_ln_b, w_s=v_w_s,
                 b_s=v_b_s, w_branch_mlstm=v_w_branch_mlstm, w_branch_sgu=v_w_branch_sgu, w_out=v_w_out,
                 norm2_g=v_norm2_g, w_up=v_w_up, w_ffn_conv=v_w_ffn_conv, w_down=v_w_down, final_g=v_final_g)
    chip = 2 * lax.axis_index("x") + lax.axis_index("y")

    shard2d = {n: params[n].reshape(params[n].shape[-2:]) for n in BIG}
    conv_sh = conv_qk.reshape(3, -1)
    fconv_sh = w_ffn_conv.reshape(9, -1)

    dev = 2 * chip + lax.axis_index("c")

    col_names, row_names = ("w_mod", "w_up", "w_in"), ("w_branch_mlstm", "w_branch_sgu", "w_out", "w_down")
    col_w = [shard2d[n].shape[1] for n in col_names]
    row_h = [shard2d[n].shape[0] for n in row_names]
    col_slab = _pad_cols(jnp.concatenate([shard2d[n].astype(BF16) for n in col_names], axis=1),
                         _round_up(sum(col_w), LANE))
    row_slab = jnp.concatenate([shard2d[n].astype(BF16) for n in row_names], axis=0)
    col_all, row_all = gather_chips([col_slab, row_slab])
    col_all = jnp.stack([jnp.where(chip == j, col_slab, col_all[j]) for j in range(4)])
    row_all = jnp.stack([jnp.where(chip == j, row_slab, row_all[j]) for j in range(4)])
    wt = {}
    off = 0
    for n, wd in zip(col_names, col_w):
        wt[n] = jnp.concatenate([col_all[j, :, off:off + wd] for j in range(4)], axis=1)
        off += wd
    off = 0
    for n, ht in zip(row_names, row_h):
        wt[n] = jnp.concatenate([row_all[j, off:off + ht, :] for j in range(4)], axis=0)
        off += ht

    cvec = _pack([conv_sh, fconv_sh], F32, align=8 * LANE, width=LANE)
    call = gather_all(cvec, "gather_conv")
    cparts = [_unpack(jnp.where(dev == 2 * j, cvec, call[2 * j]), [conv_sh.shape, fconv_sh.shape]) for j in range(4)]
    conv_full = jnp.concatenate([p[0] for p in cparts], axis=1)
    fconv_full = jnp.concatenate([p[1] for p in cparts], axis=1)

    sm = dict(b_mod=b_mod, norm1_g=norm1_g, b_gate=b_gate, conv_qk=conv_full, head_norm_g=head_norm_g,
              sgu_ln_g=sgu_ln_g, sgu_ln_b=sgu_ln_b, w_s=w_s, b_s=b_s, norm2_g=norm2_g, w_ffn_conv=fconv_full,
              final_g=final_g)

    loss_l, grad_x, gbig, gsmall, d_modx, d_modc = local_step(
        x[0], c, ctx[0], c_ctx, loss_target[0], wt, sm)

    gcol_names, gcol_w = col_names[1:], col_w[1:]
    gcol_pad = _round_up(sum(gcol_w), LANE)

    def chip_cols(j):
        return _pad_cols(jnp.concatenate([gbig[n][:, j * wd:(j + 1) * wd] for n, wd in zip(gcol_names, gcol_w)],
                                         axis=1), gcol_pad)

    def chip_rows(j):
        return jnp.concatenate([gbig[n][j * ht:(j + 1) * ht] for n, ht in zip(row_names, row_h)], axis=0)

    gss = [jnp.stack([chip_cols(j) for j in range(4)]), jnp.stack([chip_rows(j) for j in range(4)])]
    cidx = lax.axis_index("c")
    from_sib = swap_halves(gss)
    reds = []
    pair_bf, own_terms = [], []
    for s, (gs, fs) in enumerate(zip(gss, from_sib)):
        rh = gs.shape[1] // 2
        my_half = lax.dynamic_slice_in_dim(gs, cidx * rh, rh, axis=1)
        pair_bf.append(add_n([my_half, fs], BF16, "pair_sum_%d" % s))
        own_terms.append([lax.dynamic_index_in_dim(my_half, chip, axis=0, keepdims=False),
                          lax.dynamic_index_in_dim(fs, chip, axis=0, keepdims=False)])
    recv = scatter_chips(pair_bf)
    for s in range(2):
        reds.append(add_n(own_terms[s] + [recv[s][0], recv[s][1], recv[s][2]], F32, "chip_sum_%d" % s))
    others = join_halves(reds)
    full_red = [jnp.where(cidx == 0, jnp.concatenate([m, o], axis=0), jnp.concatenate([o, m], axis=0))
                for m, o in zip(reds, others)]
    g_shard = {}
    off = 0
    for n, wd in zip(gcol_names, gcol_w):
        g_shard[n] = full_red[0][:, off:off + wd]
        off += wd
    off = 0
    for n, ht in zip(row_names, row_h):
        g_shard[n] = full_red[1][off:off + ht]
        off += ht

    small_order = ("b_mod", "norm1_g", "b_gate", "conv_qk", "head_norm_g", "sgu_ln_g", "sgu_ln_b", "w_s", "b_s", "norm2_g",
                   "w_ffn_conv", "final_g")
    vec_parts = [gsmall[n] for n in small_order] + [d_modx, d_modc, c.reshape(-1), loss_l.reshape(1)]
    vec_shapes = [a.shape for a in vec_parts]
    vec = _pack(vec_parts, F32, align=8 * LANE, width=LANE)
    allv = gather_all(vec, "gather_small")
    allv = jnp.stack([jnp.where(dev == k, vec, allv[k]) for k in range(8)])
    summed = sum8(allv, "small_sum")
    s_parts = _unpack(summed, vec_shapes)
    g_small = dict(zip(small_order, s_parts[:len(small_order)]))
    dmc_sum = s_parts[len(small_order) + 1]
    loss = s_parts[-1][0]
    per_dev = [_unpack(allv[k], vec_shapes) for k in range(8)]
    dmx_all = jnp.stack([p[len(small_order)] for p in per_dev])
    c_all = jnp.stack([p[len(small_order) + 2] for p in per_dev])

    a_all = _pad_rows(jnp.concatenate([c_all, c_ctx.reshape(1, D)], axis=0), 16)
    dm_all = _pad_rows(jnp.concatenate([dmx_all, dmc_sum.reshape(1, NMOD * D)], axis=0), 16)
    ncol = NMOD * D // 4
    dm_shard = lax.dynamic_slice_in_dim(dm_all, chip * ncol, ncol, axis=1)
    g_shard["w_mod"] = mod_bwd_w(a_all, dm_shard, "mod_bwd_w")
    w_mod_t = wt["w_mod"][:, :2 * D].T
    g_cctx = mod_bwd_cctx(_pad_rows(dmc_sum[:2 * D].reshape(1, 2 * D), 16), w_mod_t, c_ctx.reshape(1, D))[0]
    g_small["c_ctx"] = g_cctx

    results = {}
    for n in BIG:
        shp = params[n].shape
        d_, m_, v_ = adamw(shard2d[n], g_shard[n], mom_m[n].reshape(shard2d[n].shape),
                           mom_v[n].reshape(shard2d[n].shape), "adamw_" + n)
        results[n] = (g_shard[n].reshape(shp), d_.reshape(shp), m_.reshape(shp), v_.reshape(shp))

    conv_g = lax.dynamic_slice_in_dim(g_small["conv_qk"].reshape(3, 2 * D), chip * (2 * D // 4), 2 * D // 4, axis=1)
    fconv_g = lax.dynamic_slice_in_dim(g_small["w_ffn_conv"].reshape(9, DFF), chip * (DFF // 4), DFF // 4, axis=1)
    g_small["conv_qk"] = conv_g
    g_small["w_ffn_conv"] = fconv_g
    w_list = [params[n].reshape(-1) for n in SMALL]
    g_list = [g_small[n].reshape(-1) for n in SMALL]
    m_list = [mom_m[n].reshape(-1) for n in SMALL]
    v_list = [mom_v[n].reshape(-1) for n in SMALL]
    sm_shapes = [params[n].shape for n in SMALL]
    pk = lambda lst: _pack(lst, F32, align=8 * LANE, width=LANE)
    gp = pk(g_list)
    d_s, m_s, v_s = adamw(pk(w_list), gp, pk(m_list), pk(v_list), "adamw_small")
    for n, gg, dd, mm, vv in zip(SMALL, _unpack(gp, sm_shapes), _unpack(d_s, sm_shapes), _unpack(m_s, sm_shapes),
                                 _unpack(v_s, sm_shapes)):
        results[n] = (gg, dd, mm, vv)

    order = ("c_ctx", "w_mod", "b_mod", "norm1_g", "w_in", "b_gate", "conv_qk", "head_norm_g", "sgu_ln_g", "sgu_ln_b",
             "w_s", "b_s", "w_branch_mlstm", "w_branch_sgu", "w_out", "norm2_g", "w_up", "w_ffn_conv", "w_down",
             "final_g")
    outs = [loss, grad_x[None]]
    for k in range(4):
        outs += [results[n][k] for n in order]
    return tuple(outs)
```

```python
import functools
import math

import jax
import jax.numpy as jnp
from jax import lax
from jax.experimental import pallas as pl
from jax.experimental.pallas import tpu as pltpu

F32 = jnp.float32
BF16 = jnp.bfloat16

D = 1024
NH = 4
DH = 256
LC = 256
GW = 64
SG = 4
SGD = 256
SCH = 128
DFF = 2816
NMOD = 6
NGATE = 16
NIN = 8208
EPS = 1e-6
M_INIT = -1e30
TR = 256
LANE = 128
VMEM_LIMIT = 56 * 1024 * 1024
MESH = pl.DeviceIdType.MESH

ADAM_LR = 0.001
ADAM_B1 = 0.9
ADAM_B2 = 0.999
ADAM_EPS = 1e-08
ADAM_WD = 0.01
ADAM_STEP = 10

CB_O, CB_U, CB_VG, CB_GM, CB_GG, CB_V, CB_Q, CB_K = range(8)


def _pick(n, cands):
    for c in cands:
        if n % c == 0:
            return c
    return n


def _cparams(sem):
    return pltpu.CompilerParams(dimension_semantics=sem, vmem_limit_bytes=VMEM_LIMIT)


def _sigmoid(x):
    return 1.0 / (1.0 + jnp.exp(-x))


def _silu(x):
    return x * _sigmoid(x)


def _dsilu(x):
    s = _sigmoid(x)
    return s * (1.0 + x * (1.0 - s))


_GC = math.sqrt(2.0 / math.pi)


def _gelu(x):
    return 0.5 * x * (1.0 + jnp.tanh(_GC * (x + 0.044715 * x * x * x)))


def _dgelu(x):
    t = jnp.tanh(_GC * (x + 0.044715 * x * x * x))
    return 0.5 * (1.0 + t) + 0.5 * x * (1.0 - t * t) * _GC * (1.0 + 3.0 * 0.044715 * x * x)


def _dot(a, b):
    return jnp.dot(a.astype(BF16), b.astype(BF16), preferred_element_type=F32)


def _dot_nt(a, b):
    return lax.dot_general(a.astype(BF16), b.astype(BF16), (((1,), (1,)), ((), ())), preferred_element_type=F32)


def _dot_tn(a, b):
    return lax.dot_general(a.astype(BF16), b.astype(BF16), (((0,), (0,)), ((), ())), preferred_element_type=F32)


def _exact_dot(tri, x):
    x1 = x.astype(BF16)
    r1 = x - x1.astype(F32)
    x2 = r1.astype(BF16)
    x3 = (r1 - x2.astype(F32)).astype(BF16)
    return (jnp.dot(tri, x1, preferred_element_type=F32) + jnp.dot(tri, x2, preferred_element_type=F32)
            + jnp.dot(tri, x3, preferred_element_type=F32))


def _rb(tm, w, col=0, off=0):
    return pl.BlockSpec((tm, w), lambda i: (i + off, col))


def _bc(r, w):
    return pl.BlockSpec((r, w), lambda i: (0, 0))


def mm_nn(a, b, out_dtype, name):
    squeeze = a.ndim == 2
    if squeeze:
        a, b = a[None], b[None]
    g, m, k = a.shape
    n = b.shape[2]
    tm = _pick(m, (1280, 1024, 512, 256, 128))
    tn = _pick(n, (2048, 1408, 1024, 512, 128))
    tk = _pick(k, (2048, 1408, 1024, 512, 128))
    nk = k // tk

    def body(a_ref, b_ref, o_ref, *scr):
        if nk == 1:
            o_ref[0] = _dot(a_ref[0], b_ref[0]).astype(o_ref.dtype)
        else:
            acc_ref, = scr
            kk = pl.program_id(3)

            @pl.when(kk == 0)
            def _():
                acc_ref[...] = jnp.zeros_like(acc_ref)

            acc_ref[...] += _dot(a_ref[0], b_ref[0])

            @pl.when(kk == nk - 1)
            def _():
                o_ref[0] = acc_ref[...].astype(o_ref.dtype)

    out = pl.pallas_call(
        body, name=name, grid=(g, n // tn, m // tm, nk),
        in_specs=[pl.BlockSpec((1, tm, tk), lambda gi, j, i, kk: (gi, i, kk)),
                  pl.BlockSpec((1, tk, tn), lambda gi, j, i, kk: (gi, kk, j))],
        out_specs=pl.BlockSpec((1, tm, tn), lambda gi, j, i, kk: (gi, i, j)),
        out_shape=jax.ShapeDtypeStruct((g, m, n), out_dtype),
        scratch_shapes=[] if nk == 1 else [pltpu.VMEM((tm, tn), F32)],
        compiler_params=_cparams(("parallel", "parallel", "parallel", "arbitrary")),
    )(a, b)
    return out[0] if squeeze else out


def mm_tn(a, b, name, out_dtype=F32):
    squeeze = a.ndim == 2
    if squeeze:
        a, b = a[None], b[None]
    g, t, ka = a.shape
    n = b.shape[2]
    tka = _pick(ka, (1024, 1408, 512, 128))
    tn = _pick(n, (2048, 1408, 1024, 512, 128))
    tt = _pick(t, (1280, 1024, 512, 256, 128))
    nt = t // tt

    def body(a_ref, b_ref, o_ref, acc_ref):
        tt_i = pl.program_id(3)

        @pl.when(tt_i == 0)
        def _():
            acc_ref[...] = jnp.zeros_like(acc_ref)

        acc_ref[...] += _dot_tn(a_ref[0], b_ref[0])

        @pl.when(tt_i == nt - 1)
        def _():
            o_ref[0] = acc_ref[...].astype(o_ref.dtype)

    out = pl.pallas_call(
        body, name=name, grid=(g, ka // tka, n // tn, nt),
        in_specs=[pl.BlockSpec((1, tt, tka), lambda gi, i, j, ti: (gi, ti, i)),
                  pl.BlockSpec((1, tt, tn), lambda gi, i, j, ti: (gi, ti, j))],
        out_specs=pl.BlockSpec((1, tka, tn), lambda gi, i, j, ti: (gi, i, j)),
        out_shape=jax.ShapeDtypeStruct((g, ka, n), out_dtype),
        scratch_shapes=[pltpu.VMEM((tka, tn), F32)],
        compiler_params=_cparams(("parallel", "parallel", "parallel", "arbitrary")),
    )(a, b)
    return out[0] if squeeze else out


def mod_fwd(cc, w_mod, b_mod):
    n = w_mod.shape[1]

    def body(c_ref, w_ref, b_ref, o_ref):
        o_ref[...] = _dot(_silu(c_ref[...]), w_ref[...]) + b_ref[...]

    return pl.pallas_call(
        body, name="mod_fwd", grid=(n // D,),
        in_specs=[_bc(16, D), pl.BlockSpec((D, D), lambda j: (0, j)), pl.BlockSpec((1, D), lambda j: (0, j))],
        out_specs=pl.BlockSpec((16, D), lambda j: (0, j)),
        out_shape=jax.ShapeDtypeStruct((16, n), F32),
        compiler_params=_cparams(("parallel",)),
    )(cc, w_mod, b_mod)


def _ctx_x_specs(n_ctx_tiles):
    return [pl.BlockSpec((TR, D), lambda i: (jnp.minimum(i, n_ctx_tiles - 1), 0)),
            pl.BlockSpec((TR, D), lambda i: (jnp.maximum(i - n_ctx_tiles, 0), 0))]


def norm1_fwd(ctx, x, g, tab, n_ctx_tiles):
    tp = ctx.shape[0] + x.shape[0]

    def body(c_ref, x_ref, g_ref, tab_ref, o_ref):
        is_ctx = pl.program_id(0) < n_ctx_tiles
        x = jnp.where(is_ctx, c_ref[...], x_ref[...])
        r = lax.rsqrt(jnp.mean(x * x, axis=-1, keepdims=True) + EPS)
        nrm = x * r * g_ref[...]
        sh = jnp.where(is_ctx, tab_ref[0:1, :], tab_ref[2:3, :])
        sc = jnp.where(is_ctx, tab_ref[1:2, :], tab_ref[3:4, :])
        o_ref[...] = (nrm * (1.0 + sc) + sh).astype(BF16)

    return pl.pallas_call(
        body, name="norm1_fwd", grid=(tp // TR,),
        in_specs=_ctx_x_specs(n_ctx_tiles) + [_bc(1, D), _bc(8, D)],
        out_specs=_rb(TR, D),
        out_shape=jax.ShapeDtypeStruct((tp, D), BF16),
        compiler_params=_cparams(("parallel",)),
    )(ctx, x, g, tab)


def norm2_fwd(x, out, g, tab):
    t = x.shape[0]

    def body(x_ref, o_in_ref, g_ref, tab_ref, h1_ref, hn_ref):
        h1 = x_ref[...] + tab_ref[0:1, :] * o_in_ref[...]
        h1_ref[...] = h1
        r = lax.rsqrt(jnp.mean(h1 * h1, axis=-1, keepdims=True) + EPS)
        nrm = h1 * r * g_ref[...]
        hn_ref[...] = (nrm * (1.0 + tab_ref[2:3, :]) + tab_ref[1:2, :]).astype(BF16)

    return pl.pallas_call(
        body, name="norm2_fwd", grid=(t // TR,),
        in_specs=[_rb(TR, D), _rb(TR, D), _bc(1, D), _bc(8, D)],
        out_specs=[_rb(TR, D), _rb(TR, D)],
        out_shape=[jax.ShapeDtypeStruct((t, D), F32), jax.ShapeDtypeStruct((t, D), BF16)],
        compiler_params=_cparams(("parallel",)),
    )(x, out, g, tab)


def _halo_specs(tm, w, col, n_rows, hb):
    per = tm // hb
    last = n_rows // hb - 1
    prev = pl.BlockSpec((hb, w), lambda i: (jnp.maximum(i * per - 1, 0), col))
    nxt = pl.BlockSpec((hb, w), lambda i: (jnp.minimum((i + 1) * per, last), col))
    return prev, nxt


def _shift_rows(x, prev_row, next_row):
    tm = x.shape[0]
    rid = lax.broadcasted_iota(jnp.int32, x.shape, 0)
    xm1 = jnp.where(rid == 0, prev_row, pltpu.roll(x, 1, 0))
    xp1 = jnp.where(rid == tm - 1, next_row, pltpu.roll(x, tm - 1, 0))
    return xm1, xp1


def _seq_edges(i, n_ctx_tiles, n_tiles):
    first = jnp.logical_or(i == 0, i == n_ctx_tiles)
    last = jnp.logical_or(i == n_ctx_tiles - 1, i == n_tiles - 1)
    return first, last


def qkconv_fwd(zmain, zg, conv_w, b_gate, n_ctx_tiles):
    tp = zmain.shape[0]
    nt = tp // TR
    w2 = 2 * D
    prev_s, next_s = _halo_specs(TR, w2, CB_Q // 2, tp, 16)

    def body(z_ref, zp_ref, zn_ref, w_ref, zg_ref, bg_ref, q_ref, k_ref, g_ref, zgb_ref):
        i = pl.program_id(0)
        first, last = _seq_edges(i, n_ctx_tiles, nt)
        z = z_ref[...].astype(F32)
        pr = jnp.where(first, 0.0, zp_ref[15:16, :].astype(F32))
        nx = jnp.where(last, 0.0, zn_ref[0:1, :].astype(F32))
        zm1, zp1 = _shift_rows(z, pr, nx)
        cv = w_ref[0:1, :] * zm1 + w_ref[1:2, :] * z + w_ref[2:3, :] * zp1
        a = _silu(cv)
        q_ref[...] = (a[:, :D] * (DH ** -0.5)).astype(BF16)
        k_ref[...] = a[:, D:].astype(BF16)
        zgb = zg_ref[...] + bg_ref[...]
        zgb_ref[...] = zgb
        logf = jnp.minimum(zgb, 0.0) - jnp.log(1.0 + jnp.exp(-jnp.abs(zgb)))
        rr = lax.broadcasted_iota(jnp.int32, (TR, TR), 0)
        cc = lax.broadcasted_iota(jnp.int32, (TR, TR), 1)
        same = (rr // LC) == (cc // LC)
        low = jnp.where(jnp.logical_and(same, cc <= rr), 1.0, 0.0).astype(BF16)
        upp = jnp.where(jnp.logical_and(same, cc >= rr), 1.0, 0.0).astype(BF16)
        bf = _exact_dot(low, logf)
        bb = _exact_dot(upp, logf)
        lane = lax.broadcasted_iota(jnp.int32, (TR, LANE), 1)
        g = jnp.where(jnp.logical_and(lane >= 4, lane < 8), bf,
                      jnp.where(jnp.logical_and(lane >= 12, lane < 16), bb, zgb))
        g_ref[...] = g

    return pl.pallas_call(
        body, name="qkconv_fwd", grid=(nt,),
        in_specs=[_rb(TR, w2, CB_Q // 2), prev_s, next_s, _bc(8, w2), _rb(TR, LANE), _bc(1, LANE)],
        out_specs=[_rb(TR, D), _rb(TR, D), _rb(TR, LANE), _rb(TR, LANE)],
        out_shape=[jax.ShapeDtypeStruct((tp, D), BF16), jax.ShapeDtypeStruct((tp, D), BF16),
                   jax.ShapeDtypeStruct((tp, LANE), F32), jax.ShapeDtypeStruct((tp, LANE), F32)],
        compiler_params=_cparams(("parallel",)),
    )(zmain, zmain, zmain, conv_w, zg, b_gate)


def qkconv_bwd_a(zmain, dqf, dqb, dkf, dkb, conv_w, n_ctx_tiles):
    tp = zmain.shape[0]
    nt = tp // TR
    w2 = 2 * D
    prev_s, next_s = _halo_specs(TR, w2, CB_Q // 2, tp, 16)

    def body(z_ref, zp_ref, zn_ref, w_ref, dqf_ref, dqb_ref, dkf_ref, dkb_ref, dc_ref, dw_ref):
        i = pl.program_id(0)
        first, last = _seq_edges(i, n_ctx_tiles, nt)
        z = z_ref[...].astype(F32)
        pr = jnp.where(first, 0.0, zp_ref[15:16, :].astype(F32))
        nx = jnp.where(last, 0.0, zn_ref[0:1, :].astype(F32))
        zm1, zp1 = _shift_rows(z, pr, nx)
        cv = w_ref[0:1, :] * zm1 + w_ref[1:2, :] * z + w_ref[2:3, :] * zp1
        da = jnp.concatenate(
            [(dqf_ref[...].astype(F32) + dqb_ref[...].astype(F32)) * (DH ** -0.5),
             dkf_ref[...].astype(F32) + dkb_ref[...].astype(F32)], axis=1)
        dc = da * _dsilu(cv)
        dc_ref[...] = dc.astype(BF16)

        @pl.when(i == 0)
        def _():
            dw_ref[...] = jnp.zeros_like(dw_ref)

        dw_ref[0:1, :] += jnp.sum(zm1 * dc, axis=0, keepdims=True)
        dw_ref[1:2, :] += jnp.sum(z * dc, axis=0, keepdims=True)
        dw_ref[2:3, :] += jnp.sum(zp1 * dc, axis=0, keepdims=True)

    return pl.pallas_call(
        body, name="qkconv_bwd_a", grid=(nt,),
        in_specs=[_rb(TR, w2, CB_Q // 2), prev_s, next_s, _bc(8, w2), _rb(TR, D), _rb(TR, D), _rb(TR, D), _rb(TR, D)],
        out_specs=[_rb(TR, w2), _bc(8, w2)],
        out_shape=[jax.ShapeDtypeStruct((tp, w2), BF16), jax.ShapeDtypeStruct((8, w2), F32)],
        compiler_params=_cparams(("arbitrary",)),
    )(zmain, zmain, zmain, conv_w, dqf, dqb, dkf, dkb)


def qkconv_bwd_b(dz, dc, conv_w, n_ctx_tiles):
    tp = dc.shape[0]
    nt = tp // TR
    w2 = 2 * D
    prev_s, next_s = _halo_specs(TR, w2, 0, tp, 16)

    def body(dz_in_ref, d_ref, dp_ref, dn_ref, w_ref, o_ref):
        del dz_in_ref
        i = pl.program_id(0)
        first, last = _seq_edges(i, n_ctx_tiles, nt)
        d = d_ref[...].astype(F32)
        pr = jnp.where(first, 0.0, dp_ref[15:16, :].astype(F32))
        nx = jnp.where(last, 0.0, dn_ref[0:1, :].astype(F32))
        dm1, dp1 = _shift_rows(d, pr, nx)
        o_ref[...] = (w_ref[0:1, :] * dp1 + w_ref[1:2, :] * d + w_ref[2:3, :] * dm1).astype(BF16)

    return pl.pallas_call(
        body, name="qkconv_bwd_b", grid=(nt,),
        in_specs=[pl.BlockSpec(memory_space=pl.ANY), _rb(TR, w2), prev_s, next_s, _bc(8, w2)],
        out_specs=_rb(TR, w2, CB_Q // 2),
        out_shape=jax.ShapeDtypeStruct(dz.shape, BF16),
        input_output_aliases={0: 0},
        compiler_params=_cparams(("parallel",)),
    )(dz, dc, dc, dc, conv_w)


def add_into_dz(dz, a, b, col):
    tp = a.shape[0]

    def body(dz_in_ref, a_ref, b_ref, o_ref):
        del dz_in_ref
        o_ref[...] = (a_ref[...].astype(F32) + b_ref[...].astype(F32)).astype(BF16)

    return pl.pallas_call(
        body, name="add_into_dz", grid=(tp // TR,),
        in_specs=[pl.BlockSpec(memory_space=pl.ANY), _rb(TR, D), _rb(TR, D)],
        out_specs=_rb(TR, D, col),
        out_shape=jax.ShapeDtypeStruct(dz.shape, BF16),
        input_output_aliases={0: 0},
        compiler_params=_cparams(("parallel",)),
    )(dz, a, b)


def _chunk_maps(nc, ncc):
    def fwd(t):
        return t

    def bwd(t):
        return jnp.where(t < ncc, ncc - 1 - t, nc - 1 + ncc - t)

    return fwd, bwd


def _mlstm_chunk(d, h, gc, gr, q_ref, k_ref, v_ref, cp, npv, m_prev, mask):
    ic, bcol = 8 * d + h, 8 * d + 4 + h
    i_col, b_col = gc[:, ic:ic + 1], gc[:, bcol:bcol + 1]
    i_row, b_row = gr[ic:ic + 1, :], gr[bcol:bcol + 1, :]
    g = b_row[:, LC - 1:LC] if d == 0 else b_row[:, 0:1]
    a_row = g - b_row + i_row
    m_loc = jnp.max(a_row, axis=1, keepdims=True)
    dmat = jnp.where(mask, b_col - b_row + i_row, -jnp.inf)
    inter = b_col + m_prev
    m_row = jnp.maximum(inter, jnp.max(dmat, axis=1, keepdims=True))
    e = jnp.exp(dmat - m_row)
    w = jnp.exp(inter - m_row)
    hs = slice(h * DH, (h + 1) * DH)
    qh, kh, vh = q_ref[:, hs], k_ref[:, hs], v_ref[:, hs]
    p = _dot_nt(qh, kh)
    s = p * e
    cpb = cp.astype(BF16)
    qc = _dot(qh, cpb)
    num = _dot(s, vh) + w * qc
    qn = jnp.sum(qh.astype(F32) * npv, axis=1, keepdims=True)
    den = jnp.sum(s, axis=1, keepdims=True) + w * qn
    thr = jnp.exp(-m_row)
    m_new = jnp.maximum(g + m_prev, m_loc)
    a_old = jnp.exp(g + m_prev - m_new)
    a_col = g - b_col + i_col
    return dict(qh=qh, kh=kh, vh=vh, e=e, w=w, s=s, cpb=cpb, qc=qc, num=num, qn=qn, den=den, thr=thr,
                m_loc=m_loc, m_new=m_new, a_old=a_old, a_col=a_col, hs=hs)


def mlstm_fwd(qa, ka, zmain, gcol, grow, ncc):
    tp = qa.shape[0]
    nc = tp // LC
    cf, cb = _chunk_maps(nc, ncc)

    def body(qf, kf, vf, gcf, grf, qb, kb, vb, gcb, grb,
             hf_o, hb_o, cf_o, cb_o, nf_o, nb_o, mf_o, mb_o, c_sc, n_sc, m_sc):
        t = pl.program_id(0)

        @pl.when(t == 0)
        def _():
            c_sc[...] = jnp.zeros_like(c_sc)
            n_sc[...] = jnp.zeros_like(n_sc)
            m_sc[...] = jnp.full(m_sc.shape, M_INIT, F32)

        row = lax.broadcasted_iota(jnp.int32, (LC, LC), 0)
        col = lax.broadcasted_iota(jnp.int32, (LC, LC), 1)
        dirs = ((qf, kf, vf, gcf, grf, hf_o, cf_o, nf_o, mf_o), (qb, kb, vb, gcb, grb, hb_o, cb_o, nb_o, mb_o))
        for d, (q_ref, k_ref, v_ref, gc_ref, gr_ref, h_o, c_o, n_o, m_o) in enumerate(dirs):
            mask = (col <= row) if d == 0 else (col >= row)
            gc = gc_ref[...]
            gr = gr_ref[0]
            for h in range(NH):
                idx = d * NH + h
                cp = c_sc[idx]
                npv = n_sc[idx]
                m_full = m_sc[idx]
                m_prev = m_full[:, 0:1]
                r = _mlstm_chunk(d, h, gc, gr, q_ref, k_ref, v_ref, cp, npv, m_prev, mask)
                hs = r["hs"]
                h_o[:, hs] = (r["num"] / jnp.maximum(jnp.abs(r["den"]), r["thr"])).astype(BF16)
                c_o[0, hs, :] = r["cpb"]
                n_o[0, h:h + 1, :] = npv
                m_o[0, h:h + 1, :] = m_full
                a_new = jnp.exp(r["m_loc"] - r["m_new"])
                kw = r["kh"].astype(F32) * jnp.exp(r["a_col"] - r["m_loc"])
                kv = _dot_tn(kw, r["vh"])
                kn = jnp.sum(kw, axis=0, keepdims=True)
                c_sc[idx] = r["a_old"] * cp + a_new * kv
                n_sc[idx] = r["a_old"] * npv + a_new * kn
                m_sc[idx] = jnp.broadcast_to(r["m_new"], (1, LANE))

    def dspecs(cm):
        return [pl.BlockSpec((LC, D), lambda t: (cm(t), 0)),
                pl.BlockSpec((LC, D), lambda t: (cm(t), 0)),
                pl.BlockSpec((LC, D), lambda t: (cm(t), CB_V)),
                pl.BlockSpec((LC, LANE), lambda t: (cm(t), 0)),
                pl.BlockSpec((1, 16, LC), lambda t: (cm(t), 0, 0))]

    def ospec(cm, shp):
        return pl.BlockSpec((1,) + shp, lambda t: (cm(t), 0, 0))

    return pl.pallas_call(
        body, name="mlstm_fwd", grid=(nc,),
        in_specs=dspecs(cf) + dspecs(cb),
        out_specs=[pl.BlockSpec((LC, D), lambda t: (cf(t), 0)), pl.BlockSpec((LC, D), lambda t: (cb(t), 0)),
                   ospec(cf, (D, DH)), ospec(cb, (D, DH)), ospec(cf, (NH, DH)), ospec(cb, (NH, DH)),
                   ospec(cf, (NH, LANE)), ospec(cb, (NH, LANE))],
        out_shape=[jax.ShapeDtypeStruct((tp, D), BF16), jax.ShapeDtypeStruct((tp, D), BF16),
                   jax.ShapeDtypeStruct((nc, D, DH), BF16), jax.ShapeDtypeStruct((nc, D, DH), BF16),
                   jax.ShapeDtypeStruct((nc, NH, DH), F32), jax.ShapeDtypeStruct((nc, NH, DH), F32),
                   jax.ShapeDtypeStruct((nc, NH, LANE), F32), jax.ShapeDtypeStruct((nc, NH, LANE), F32)],
        scratch_shapes=[pltpu.VMEM((2 * NH, DH, DH), F32), pltpu.VMEM((2 * NH, 1, DH), F32),
                        pltpu.VMEM((2 * NH, 1, LANE), F32)],
        compiler_params=_cparams(("arbitrary",)),
    )(qa, ka, zmain, gcol, grow, qa, ka, zmain, gcol, grow)


def mlstm_bwd(qa, ka, zmain, gcol, grow, states, dhm, ncc):
    tp = qa.shape[0]
    nc = tp // LC
    cf0, cb0 = _chunk_maps(nc, ncc)
    cf = lambda t: cf0(nc - 1 - t)
    cb = lambda t: cb0(nc - 1 - t)
    csf, csb, nsf, nsb, msf, msb = states

    def body(qf, kf, vf, gcf, grf, cpf, npf, mpf, dhf, qb, kb, vb, gcb, grb, cpb_, npb, mpb, dhb,
             dqf_o, dkf_o, dvf_o, colf_o, rowf_o, dqb_o, dkb_o, dvb_o, colb_o, rowb_o, dc_sc, dn_sc):
        t = pl.program_id(0)

        @pl.when(t == 0)
        def _():
            dc_sc[...] = jnp.zeros_like(dc_sc)
            dn_sc[...] = jnp.zeros_like(dn_sc)

        row = lax.broadcasted_iota(jnp.int32, (LC, LC), 0)
        col = lax.broadcasted_iota(jnp.int32, (LC, LC), 1)
        dirs = ((qf, kf, vf, gcf, grf, cpf, npf, mpf, dhf, dqf_o, dkf_o, dvf_o, colf_o, rowf_o, cf),
                (qb, kb, vb, gcb, grb, cpb_, npb, mpb, dhb, dqb_o, dkb_o, dvb_o, colb_o, rowb_o, cb))
        for d, (q_ref, k_ref, v_ref, gc_ref, gr_ref, cp_ref, np_ref, mp_ref, dh_ref,
                dq_o, dk_o, dv_o, col_o, row_o, cm) in enumerate(dirs):
            mask = (col <= row) if d == 0 else (col >= row)
            live = jnp.where(cm(t) >= ncc, 1.0, 0.0).astype(F32)
            gc = gc_ref[...]
            gr = gr_ref[0]
            col_o[...] = jnp.zeros_like(col_o)
            row_o[...] = jnp.zeros_like(row_o)
            for h in range(NH):
                idx = d * NH + h
                hs = slice(h * DH, (h + 1) * DH)
                cp = cp_ref[0, hs, :]
                npv = np_ref[0, h:h + 1, :]
                m_prev = mp_ref[0, h:h + 1, 0:1]
                r = _mlstm_chunk(d, h, gc, gr, q_ref, k_ref, v_ref, cp, npv, m_prev, mask)
                qh, kh, vh, e, w, s = r["qh"], r["kh"], r["vh"], r["e"], r["w"], r["s"]
                qf32, kf32 = qh.astype(F32), kh.astype(F32)
                den, thr = r["den"], r["thr"]
                rden = 1.0 / jnp.maximum(jnp.abs(den), thr)
                hh = r["num"] * rden
                dh = dh_ref[:, hs].astype(F32) * live
                dnum = dh * rden
                sgn = jnp.where(jnp.abs(den) > thr, jnp.sign(den), 0.0)
                dden = -jnp.sum(dh * hh, axis=1, keepdims=True) * rden * sgn
                ds = _dot_nt(dnum, vh) + dden
                dp = ds * e
                gm = ds * s
                rowsum = jnp.sum(gm, axis=1, keepdims=True)
                colsum = jnp.sum(gm, axis=0, keepdims=True)
                dq = _dot(dp, kh) + w * (_dot_nt(dnum, r["cpb"]) + dden * npv)
                dcs = dc_sc[idx]
                dns = dn_sc[idx]
                kfac = jnp.exp(r["a_col"] - r["m_new"])
                vdc = _dot_nt(vh, dcs)
                dk = _dot_tn(dp, qh) + kfac * (vdc + dns)
                dv = _dot_tn(s, dnum) + kfac * _dot(kh, dcs)
                beta = w * (jnp.sum(dnum * r["qc"], axis=1, keepdims=True) + dden * r["qn"])
                alpha = kfac * (jnp.sum(kf32 * vdc, axis=1, keepdims=True) + jnp.sum(kf32 * dns, axis=1, keepdims=True))
                dq_o[:, hs] = dq.astype(BF16)
                dk_o[:, hs] = dk.astype(BF16)
                dv_o[:, hs] = dv.astype(BF16)
                cpf = r["cpb"].astype(F32)
                inner = (jnp.sum(jnp.sum(dcs * cpf, axis=1, keepdims=True), axis=0, keepdims=True)
                         + jnp.sum(dns * npv, axis=1, keepdims=True))
                gam = jnp.sum(alpha, axis=0, keepdims=True) + r["a_old"] * inner
                col_o[:, h:h + 1] = rowsum + beta - alpha
                col_o[:, 4 + h:5 + h] = alpha
                col_o[:, 8 + h:9 + h] = jnp.broadcast_to(gam, (LC, 1))
                row_o[0, h:h + 1, :] = colsum
                wq = qf32 * w
                dc_sc[idx] = r["a_old"] * dcs + _dot_tn(wq, dnum)
                dn_sc[idx] = r["a_old"] * dns + jnp.sum(wq * dden, axis=0, keepdims=True)

    def dspecs(cm):
        return [pl.BlockSpec((LC, D), lambda t: (cm(t), 0)),
                pl.BlockSpec((LC, D), lambda t: (cm(t), 0)),
                pl.BlockSpec((LC, D), lambda t: (cm(t), CB_V)),
                pl.BlockSpec((LC, LANE), lambda t: (cm(t), 0)),
                pl.BlockSpec((1, 16, LC), lambda t: (cm(t), 0, 0)),
                pl.BlockSpec((1, D, DH), lambda t: (cm(t), 0, 0)),
                pl.BlockSpec((1, NH, DH), lambda t: (cm(t), 0, 0)),
                pl.BlockSpec((1, NH, LANE), lambda t: (cm(t), 0, 0)),
                pl.BlockSpec((LC, D), lambda t: (jnp.maximum(cm(t) - ncc, 0), 0))]

    def ospecs(cm):
        return [pl.BlockSpec((LC, D), lambda t: (cm(t), 0)),
                pl.BlockSpec((LC, D), lambda t: (cm(t), 0)),
                pl.BlockSpec((LC, D), lambda t: (cm(t), 0)),
                pl.BlockSpec((LC, LANE), lambda t: (cm(t), 0)),
                pl.BlockSpec((1, 8, LC), lambda t: (cm(t), 0, 0))]

    oshape = [jax.ShapeDtypeStruct((tp, D), BF16)] * 3 + [jax.ShapeDtypeStruct((tp, LANE), F32),
                                                        jax.ShapeDtypeStruct((nc, 8, LC), F32)]
    return pl.pallas_call(
        body, name="mlstm_bwd", grid=(nc,),
        in_specs=dspecs(cf) + dspecs(cb),
        out_specs=ospecs(cf) + ospecs(cb),
        out_shape=oshape + oshape,
        scratch_shapes=[pltpu.VMEM((2 * NH, DH, DH), F32), pltpu.VMEM((2 * NH, 1, DH), F32)],
        compiler_params=_cparams(("arbitrary",)),
    )(qa, ka, zmain, gcol, grow, csf, nsf, msf, dhm, qa, ka, zmain, gcol, grow, csb, nsb, msb, dhm)


def gates_bwd(rg, gam, csi, csf, ali, zgb):
    tp = rg.shape[0]

    def body(rg_ref, gam_ref, csi_ref, csf_ref, ali_ref, zgb_ref, o_ref, db_ref):
        i = pl.program_id(0)

        @pl.when(i == 0)
        def _():
            db_ref[...] = jnp.zeros_like(db_ref)

        lane = lax.broadcasted_iota(jnp.int32, (TR, LANE), 1)
        i_l = jnp.logical_or(lane < 4, jnp.logical_and(lane >= 8, lane < 12))
        f_l = jnp.logical_or(jnp.logical_and(lane >= 4, lane < 8), jnp.logical_and(lane >= 12, lane < 16))
        dbh = jnp.where(f_l, rg_ref[...] - csf_ref[...], 0.0)
        rr = lax.broadcasted_iota(jnp.int32, (TR, TR), 0)
        cc = lax.broadcasted_iota(jnp.int32, (TR, TR), 1)
        same = (rr // LC) == (cc // LC)
        low = jnp.where(jnp.logical_and(same, cc <= rr), 1.0, 0.0).astype(BF16)
        upp = jnp.where(jnp.logical_and(same, cc >= rr), 1.0, 0.0).astype(BF16)
        dlogf = jnp.where(lane < 8, _exact_dot(upp, dbh), _exact_dot(low, dbh)) + gam_ref[...]
        out = (jnp.where(i_l, csi_ref[...] + ali_ref[...], 0.0)
               + jnp.where(f_l, dlogf * _sigmoid(-zgb_ref[...]), 0.0))
        o_ref[...] = out
        db_ref[...] += jnp.sum(out, axis=0, keepdims=True)

    spec = _rb(TR, LANE)
    return pl.pallas_call(
        body, name="gates_bwd", grid=(tp // TR,),
        in_specs=[spec] * 6,
        out_specs=[spec, _bc(1, LANE)],
        out_shape=[jax.ShapeDtypeStruct((tp, LANE), F32), jax.ShapeDtypeStruct((1, LANE), F32)],
        compiler_params=_cparams(("arbitrary",)),
    )(rg, gam, csi, csf, ali, zgb)


def _head_norm(hm, gh):
    xs, rs = [], []
    for h in range(NH):
        seg = hm[:, h * DH:(h + 1) * DH]
        r = lax.rsqrt(jnp.mean(seg * seg, axis=-1, keepdims=True) + EPS)
        xs.append(seg * r)
        rs.append(r)
    xh = jnp.concatenate(xs, axis=1)
    return xh, rs, xh * gh


def _sgu_norm(zvg, ln_g, ln_b):
    vg = _gelu(zvg)
    mu = jnp.mean(vg, axis=-1, keepdims=True)
    vc = vg - mu
    rstd = lax.rsqrt(jnp.mean(vc * vc, axis=-1, keepdims=True) + EPS)
    vhat = vc * rstd
    return vhat, rstd, vhat * ln_g + ln_b


def _sgu_mix(vn, ws_ref, bs_ref):
    rows = []
    for c in range(TR // SCH):
        cols = []
        for g in range(SG):
            blk = vn[c * SCH:(c + 1) * SCH, g * SGD:(g + 1) * SGD]
            cols.append(_dot(ws_ref[g * SCH:(g + 1) * SCH, :], blk) + bs_ref[:, g:g + 1])
        rows.append(jnp.concatenate(cols, axis=1))
    return jnp.concatenate(rows, axis=0)


def mixer_fwd(hf, hb, zmain, gh, ln_g, ln_b, ws, bs_t, n_ctx_tiles):
    t = hf.shape[0] - n_ctx_tiles * TR
    off = n_ctx_tiles

    def body(hf_ref, hb_ref, zo_ref, zu_ref, zv_ref, gh_ref, lg_ref, lb_ref, ws_ref, bs_ref, o_ref):
        hm = hf_ref[...].astype(F32) + hb_ref[...].astype(F32)
        _, _, hn = _head_norm(hm, gh_ref[...])
        o_ref[0] = (_sigmoid(zo_ref[...].astype(F32)) * hn).astype(BF16)
        _, _, vn = _sgu_norm(zv_ref[...].astype(F32), lg_ref[...], lb_ref[...])
        mixed = _sgu_mix(vn, ws_ref, bs_ref)
        o_ref[1] = (_gelu(zu_ref[...].astype(F32)) * mixed).astype(BF16)

    return pl.pallas_call(
        body, name="mixer_fwd", grid=(t // TR,),
        in_specs=[_rb(TR, D, 0, off), _rb(TR, D, 0, off), _rb(TR, D, CB_O, off), _rb(TR, D, CB_U, off),
                  _rb(TR, D, CB_VG, off), _bc(1, D), _bc(1, D), _bc(1, D), _bc(SG * SCH, SCH), _bc(SCH, LANE)],
        out_specs=pl.BlockSpec((2, TR, D), lambda i: (0, i, 0)),
        out_shape=jax.ShapeDtypeStruct((2, t, D), BF16),
        compiler_params=_cparams(("parallel",)),
    )(hf, hb, zmain, zmain, zmain, gh, ln_g, ln_b, ws, bs_t)


def merge_fwd(zmain, pp, n_ctx_tiles):
    t = pp.shape[1]
    off = n_ctx_tiles

    def body(zgm_ref, zgg_ref, pp_ref, o_ref):
        y = (_sigmoid(zgm_ref[...].astype(F32)) * pp_ref[0].astype(F32)
             + _sigmoid(zgg_ref[...].astype(F32)) * pp_ref[1].astype(F32))
        o_ref[...] = y.astype(BF16)

    return pl.pallas_call(
        body, name="merge_fwd", grid=(t // TR,),
        in_specs=[_rb(TR, D, CB_GM, off), _rb(TR, D, CB_GG, off), pl.BlockSpec((2, TR, D), lambda i: (0, i, 0))],
        out_specs=_rb(TR, D),
        out_shape=jax.ShapeDtypeStruct((t, D), BF16),
        compiler_params=_cparams(("parallel",)),
    )(zmain, zmain, pp)


def merge_bwd(zmain, pp, dy, tp, n_ctx_tiles):
    t = dy.shape[0]
    nt = tp // TR
    xrow = lambda i: jnp.maximum(i - n_ctx_tiles, 0)

    def body(zg_ref, pp_ref, dy_ref, dpp_ref, dz_ref):
        i = pl.program_id(1)
        zg = zg_ref[...].astype(F32)
        sg = _sigmoid(zg)
        dyv = dy_ref[...].astype(F32)
        dpp_ref[0] = (dyv * sg).astype(BF16)
        dzv = dyv * pp_ref[0].astype(F32) * sg * (1.0 - sg)
        dz_ref[...] = jnp.where(i >= n_ctx_tiles, dzv, 0.0).astype(BF16)

    return pl.pallas_call(
        body, name="merge_bwd", grid=(2, nt),
        in_specs=[pl.BlockSpec((TR, D), lambda j, i: (i, CB_GM + j)),
                  pl.BlockSpec((1, TR, D), lambda j, i: (j, xrow(i), 0)),
                  pl.BlockSpec((TR, D), lambda j, i: (xrow(i), 0))],
        out_specs=[pl.BlockSpec((1, TR, D), lambda j, i: (j, xrow(i), 0)),
                   pl.BlockSpec((TR, D), lambda j, i: (i, CB_GM + j))],
        out_shape=[jax.ShapeDtypeStruct((2, t, D), BF16), jax.ShapeDtypeStruct((tp, 8 * D), BF16)],
        compiler_params=_cparams(("arbitrary", "arbitrary")),
    )(zmain, pp, dy)


def mixer_bwd(dz, hf, hb, zmain, dyms, gh, ln_g, ln_b, ws, bs_t, n_ctx_tiles):
    tp = hf.shape[0]
    t = tp - n_ctx_tiles * TR
    nt = tp // TR
    xrow = lambda i: jnp.maximum(i - n_ctx_tiles, 0)

    def body(dz_in_ref, hf_ref, hb_ref, zo_ref, zu_ref, zv_ref, dy_ref, gh_ref, lg_ref, lb_ref, ws_ref, bs_ref,
             dz_ref, dhm_ref, dgh_ref, dlg_ref, dlb_ref, dws_ref, dbs_ref):
        del dz_in_ref
        i = pl.program_id(0)

        @pl.when(i == 0)
        def _():
            for ref in (dgh_ref, dlg_ref, dlb_ref, dws_ref, dbs_ref):
                ref[...] = jnp.zeros_like(ref)

        @pl.when(i < n_ctx_tiles)
        def _():
            dz_ref[...] = jnp.zeros_like(dz_ref)

        @pl.when(i >= n_ctx_tiles)
        def _():
            gh_v = gh_ref[...]
            hm = hf_ref[...].astype(F32) + hb_ref[...].astype(F32)
            xh, rs, hn = _head_norm(hm, gh_v)
            zo = zo_ref[...].astype(F32)
            so = _sigmoid(zo)
            dym = dy_ref[0].astype(F32)
            d_zo = dym * hn * so * (1.0 - so)
            d_hn = dym * so
            dgh_ref[...] += jnp.sum(d_hn * xh, axis=0, keepdims=True)
            d_xh = d_hn * gh_v
            segs = []
            for h in range(NH):
                hs = slice(h * DH, (h + 1) * DH)
                dx, xs = d_xh[:, hs], xh[:, hs]
                segs.append(rs[h] * (dx - xs * jnp.mean(dx * xs, axis=-1, keepdims=True)))
            dhm_ref[...] = jnp.concatenate(segs, axis=1).astype(BF16)
            zu = zu_ref[...].astype(F32)
            zv = zv_ref[...].astype(F32)
            lg = lg_ref[...]
            vhat, rstd, vn = _sgu_norm(zv, lg, lb_ref[...])
            mixed = _sgu_mix(vn, ws_ref, bs_ref)
            dys = dy_ref[1].astype(F32)
            d_zu = dys * mixed * _dgelu(zu)
            d_mixed = dys * _gelu(zu)
            rows = []
            for c in range(TR // SCH):
                cols = []
                for g in range(SG):
                    rsl, csl = slice(c * SCH, (c + 1) * SCH), slice(g * SGD, (g + 1) * SGD)
                    dm = d_mixed[rsl, csl]
                    cols.append(_dot_tn(ws_ref[g * SCH:(g + 1) * SCH, :], dm))
                    dws_ref[g * SCH:(g + 1) * SCH, :] += _dot_nt(dm, vn[rsl, csl])
                    dbs_ref[:, g:g + 1] += jnp.sum(dm, axis=1, keepdims=True)
                rows.append(jnp.concatenate(cols, axis=1))
            d_vn = jnp.concatenate(rows, axis=0)
            dlg_ref[...] += jnp.sum(d_vn * vhat, axis=0, keepdims=True)
            dlb_ref[...] += jnp.sum(d_vn, axis=0, keepdims=True)
            d_vhat = d_vn * lg
            d_vg = rstd * (d_vhat - jnp.mean(d_vhat, axis=-1, keepdims=True)
                           - vhat * jnp.mean(d_vhat * vhat, axis=-1, keepdims=True))
            d_zv = d_vg * _dgelu(zv)
            dz_ref[...] = jnp.concatenate([d_zo, d_zu, d_zv], axis=1).astype(BF16)

    return pl.pallas_call(
        body, name="mixer_bwd", grid=(nt,),
        in_specs=[pl.BlockSpec(memory_space=pl.ANY), _rb(TR, D), _rb(TR, D), _rb(TR, D, CB_O), _rb(TR, D, CB_U),
                  _rb(TR, D, CB_VG), pl.BlockSpec((2, TR, D), lambda i: (0, xrow(i), 0)),
                  _bc(1, D), _bc(1, D), _bc(1, D), _bc(SG * SCH, SCH), _bc(SCH, LANE)],
        out_specs=[_rb(TR, 3 * D), pl.BlockSpec((TR, D), lambda i: (xrow(i), 0)),
                   _bc(1, D), _bc(1, D), _bc(1, D), _bc(SG * SCH, SCH), _bc(SCH, LANE)],
        out_shape=[jax.ShapeDtypeStruct(dz.shape, BF16), jax.ShapeDtypeStruct((t, D), BF16),
                   jax.ShapeDtypeStruct((1, D), F32), jax.ShapeDtypeStruct((1, D), F32),
                   jax.ShapeDtypeStruct((1, D), F32), jax.ShapeDtypeStruct((SG * SCH, SCH), F32),
                   jax.ShapeDtypeStruct((SCH, LANE), F32)],
        input_output_aliases={0: 0},
        compiler_params=_cparams(("arbitrary",)),
    )(dz, hf, hb, zmain, zmain, zmain, dyms, gh, ln_g, ln_b, ws, bs_t)


FCB = DFF // 2
TF = 512


def _ffn_halo(col, t):
    per = TF // GW
    last = t // GW - 1
    prev = pl.BlockSpec((GW, FCB), lambda i, j: (jnp.maximum(i * per - 1, 0), col(j)))
    nxt = pl.BlockSpec((GW, FCB), lambda i, j: (jnp.minimum((i + 1) * per, last), col(j)))
    return prev, nxt


def _conv_taps(ext):
    n = ext.shape[0]
    colid = lax.broadcasted_iota(jnp.int32, (n, 1), 0) % GW
    left = pltpu.roll(jnp.where(colid != GW - 1, ext, 0.0), 1, 0)
    right = pltpu.roll(jnp.where(colid != 0, ext, 0.0), n - 1, 0)
    views = (left, ext, right)
    return {(ky, kx): views[kx][GW * ky:GW * ky + TF] for ky in range(3) for kx in range(3)}


def _ext(c_ref, p_ref, n_ref, i, nt):
    pr = jnp.where(i == 0, 0.0, p_ref[...].astype(F32))
    nx = jnp.where(i == nt - 1, 0.0, n_ref[...].astype(F32))
    return jnp.concatenate([pr, c_ref[...].astype(F32), nx], axis=0)


def ffn_act_fwd(up, wc):
    t = up.shape[0]
    nt = t // TF
    prev_s, next_s = _ffn_halo(lambda j: j, t)

    def body(a_ref, ap_ref, an_ref, b_ref, w_ref, o_ref, ac_ref):
        i = pl.program_id(0)
        taps = _conv_taps(_ext(a_ref, ap_ref, an_ref, i, nt))
        ac = sum(w_ref[3 * ky + kx:3 * ky + kx + 1, :] * taps[(ky, kx)] for ky in range(3) for kx in range(3))
        ac_ref[...] = ac.astype(BF16)
        o_ref[...] = (_silu(ac) * b_ref[...].astype(F32)).astype(BF16)

    spec = pl.BlockSpec((TF, FCB), lambda i, j: (i, j))
    return pl.pallas_call(
        body, name="ffn_act_fwd", grid=(nt, 2),
        in_specs=[spec, prev_s, next_s,
                  pl.BlockSpec((TF, FCB), lambda i, j: (i, 2 + j)), pl.BlockSpec((16, FCB), lambda i, j: (0, j))],
        out_specs=[spec, spec],
        out_shape=[jax.ShapeDtypeStruct((t, DFF), BF16), jax.ShapeDtypeStruct((t, DFF), BF16)],
        compiler_params=_cparams(("parallel", "parallel")),
    )(up, up, up, up, wc)


def ffn_act_bwd(up, ac, dact):
    t = up.shape[0]
    nt = t // TF

    def body(b_ref, ac_ref, da_ref, dup_ref, dac_ref):
        acv = ac_ref[...].astype(F32)
        da = da_ref[...].astype(F32)
        s = _sigmoid(acv)
        dup_ref[...] = (da * acv * s).astype(BF16)
        dac_ref[...] = (da * b_ref[...].astype(F32) * s * (1.0 + acv * (1.0 - s))).astype(BF16)

    spec = pl.BlockSpec((TF, FCB), lambda i, j: (i, j))
    bspec = pl.BlockSpec((TF, FCB), lambda i, j: (i, 2 + j))
    return pl.pallas_call(
        body, name="ffn_act_bwd", grid=(nt, 2),
        in_specs=[bspec, spec, spec],
        out_specs=[bspec, spec],
        out_shape=[jax.ShapeDtypeStruct((t, 2 * DFF), BF16), jax.ShapeDtypeStruct((t, DFF), BF16)],
        compiler_params=_cparams(("parallel", "parallel")),
    )(up, ac, dact)


def ffn_conv_bwd(dup, up, dac, wc):
    t = up.shape[0]
    nt = t // TF
    prev_g, next_g = _ffn_halo(lambda j: j, t)

    def body(dup_in_ref, a_ref, g_ref, gp_ref, gn_ref, w_ref, o_ref, dw_ref):
        del dup_in_ref
        i = pl.program_id(1)

        @pl.when(i == 0)
        def _():
            dw_ref[...] = jnp.zeros_like(dw_ref)

        gtaps = _conv_taps(_ext(g_ref, gp_ref, gn_ref, i, nt))
        a = a_ref[...].astype(F32)
        acc = None
        for ky in range(3):
            for kx in range(3):
                k = 3 * ky + kx
                kf = 3 * (2 - ky) + (2 - kx)
                tap = gtaps[(ky, kx)]
                term = w_ref[kf:kf + 1, :] * tap
                acc = term if acc is None else acc + term
                dw_ref[kf:kf + 1, :] += jnp.sum(a * tap, axis=0, keepdims=True)
        o_ref[...] = acc.astype(BF16)

    sw = lambda s: pl.BlockSpec(s.block_shape, lambda j, i, f=s.index_map: f(i, j))
    spec = pl.BlockSpec((TF, FCB), lambda j, i: (i, j))
    return pl.pallas_call(
        body, name="ffn_conv_bwd", grid=(2, nt),
        in_specs=[pl.BlockSpec(memory_space=pl.ANY), spec, spec, sw(prev_g), sw(next_g),
                  pl.BlockSpec((16, FCB), lambda j, i: (0, j))],
        out_specs=[spec, pl.BlockSpec((16, FCB), lambda j, i: (0, j))],
        out_shape=[jax.ShapeDtypeStruct(dup.shape, BF16), jax.ShapeDtypeStruct((16, DFF), F32)],
        input_output_aliases={0: 0},
        compiler_params=_cparams(("arbitrary", "arbitrary")),
    )(dup, up, dac, dac, dac, wc)


def head_fwd_bwd(h1, f, target, gfin, tab):
    t = h1.shape[0]

    def body(h1_ref, f_ref, t_ref, g_ref, tab_ref, dh2_ref, df_ref, acc_ref):
        i = pl.program_id(0)

        @pl.when(i == 0)
        def _():
            acc_ref[...] = jnp.zeros_like(acc_ref)

        gate = tab_ref[0:1, :]
        fv = f_ref[...]
        h2 = h1_ref[...] + gate * fv
        r = lax.rsqrt(jnp.mean(h2 * h2, axis=-1, keepdims=True) + EPS)
        xh = h2 * r
        gv = g_ref[...]
        err = xh * gv - t_ref[...]
        acc_ref[0:1, :] += jnp.sum(0.5 * jnp.mean(err * err, axis=-1, keepdims=True), axis=0, keepdims=True)
        dy = err * (1.0 / D)
        acc_ref[1:2, :] += jnp.sum(dy * xh, axis=0, keepdims=True)
        dxh = dy * gv
        dh2 = r * (dxh - xh * jnp.mean(dxh * xh, axis=-1, keepdims=True))
        dh2_ref[...] = dh2
        acc_ref[2:3, :] += jnp.sum(dh2 * fv, axis=0, keepdims=True)
        df_ref[...] = (dh2 * gate).astype(BF16)

    return pl.pallas_call(
        body, name="head_fwd_bwd", grid=(t // TR,),
        in_specs=[_rb(TR, D), _rb(TR, D), _rb(TR, D), _bc(1, D), _bc(8, D)],
        out_specs=[_rb(TR, D), _rb(TR, D), _bc(8, D)],
        out_shape=[jax.ShapeDtypeStruct((t, D), F32), jax.ShapeDtypeStruct((t, D), BF16),
                   jax.ShapeDtypeStruct((8, D), F32)],
        compiler_params=_cparams(("arbitrary",)),
    )(h1, f, target, gfin, tab)


def norm2_bwd(h1, dhn2, dh2, out, g, tab):
    t = h1.shape[0]

    def body(h1_ref, dhn_ref, dh2_ref, out_ref, g_ref, tab_ref, dh1_ref, dout_ref, acc_ref):
        i = pl.program_id(0)

        @pl.when(i == 0)
        def _():
            acc_ref[...] = jnp.zeros_like(acc_ref)

        h1v = h1_ref[...]
        r = lax.rsqrt(jnp.mean(h1v * h1v, axis=-1, keepdims=True) + EPS)
        xh = h1v * r
        gv = g_ref[...]
        dhn = dhn_ref[...]
        acc_ref[0:1, :] += jnp.sum(dhn, axis=0, keepdims=True)
        acc_ref[1:2, :] += jnp.sum(dhn * xh * gv, axis=0, keepdims=True)
        dn = dhn * (1.0 + tab_ref[2:3, :])
        acc_ref[2:3, :] += jnp.sum(dn * xh, axis=0, keepdims=True)
        dxh = dn * gv
        dh1 = dh2_ref[...] + r * (dxh - xh * jnp.mean(dxh * xh, axis=-1, keepdims=True))
        dh1_ref[...] = dh1
        acc_ref[3:4, :] += jnp.sum(dh1 * out_ref[...], axis=0, keepdims=True)
        dout_ref[...] = (dh1 * tab_ref[0:1, :]).astype(BF16)

    return pl.pallas_call(
        body, name="norm2_bwd", grid=(t // TR,),
        in_specs=[_rb(TR, D), _rb(TR, D), _rb(TR, D), _rb(TR, D), _bc(1, D), _bc(8, D)],
        out_specs=[_rb(TR, D), _rb(TR, D), _bc(8, D)],
        out_shape=[jax.ShapeDtypeStruct((t, D), F32), jax.ShapeDtypeStruct((t, D), BF16),
                   jax.ShapeDtypeStruct((8, D), F32)],
        compiler_params=_cparams(("arbitrary",)),
    )(h1, dhn2, dh2, out, g, tab)


def norm1_bwd(ctx, x, da, db, dh1, g, tab, n_ctx_tiles):
    t = x.shape[0]
    tp = t + ctx.shape[0]
    xrow = lambda i: jnp.maximum(i - n_ctx_tiles, 0)

    def body(c_ref, x_ref, da_ref, db_ref, dh1_ref, g_ref, tab_ref, dx_ref, acc_ref):
        i = pl.program_id(0)

        @pl.when(i == 0)
        def _():
            acc_ref[...] = jnp.zeros_like(acc_ref)

        is_ctx = i < n_ctx_tiles
        x = jnp.where(is_ctx, c_ref[...], x_ref[...])
        r = lax.rsqrt(jnp.mean(x * x, axis=-1, keepdims=True) + EPS)
        xh = x * r
        gv = g_ref[...]
        dhn = da_ref[...] + db_ref[...]
        s_shift = jnp.sum(dhn, axis=0, keepdims=True)
        s_scale = jnp.sum(dhn * xh * gv, axis=0, keepdims=True)

        @pl.when(is_ctx)
        def _():
            acc_ref[0:1, :] += s_shift
            acc_ref[1:2, :] += s_scale

        @pl.when(jnp.logical_not(is_ctx))
        def _():
            acc_ref[2:3, :] += s_shift
            acc_ref[3:4, :] += s_scale

        acc_ref[5:6, :] += s_shift
        acc_ref[6:7, :] += s_scale
        sc = jnp.where(is_ctx, tab_ref[1:2, :], tab_ref[3:4, :])
        dn = dhn * (1.0 + sc)
        acc_ref[4:5, :] += jnp.sum(dn * xh, axis=0, keepdims=True)
        dxh = dn * gv
        dx_ref[...] = dh1_ref[...] + r * (dxh - xh * jnp.mean(dxh * xh, axis=-1, keepdims=True))

    return pl.pallas_call(
        body, name="norm1_bwd", grid=(tp // TR,),
        in_specs=_ctx_x_specs(n_ctx_tiles) + [_rb(TR, D), _rb(TR, D), pl.BlockSpec((TR, D), lambda i: (xrow(i), 0)),
                                              _bc(1, D), _bc(8, D)],
        out_specs=[pl.BlockSpec((TR, D), lambda i: (xrow(i), 0)), _bc(8, D)],
        out_shape=[jax.ShapeDtypeStruct((t, D), F32), jax.ShapeDtypeStruct((8, D), F32)],
        compiler_params=_cparams(("arbitrary",)),
    )(ctx, x, da, db, dh1, g, tab)


def adamw(w, g, m, v, name):
    rows, cols = w.shape
    tm = rows if rows * cols <= 256 * 1024 else _pick(rows, (256, 176, 128, 64, 8))
    c1 = 1.0 / (1.0 - ADAM_B1 ** ADAM_STEP)
    c2 = 1.0 / (1.0 - ADAM_B2 ** ADAM_STEP)

    def body(w_ref, g_ref, m_ref, v_ref, d_ref, mo_ref, vo_ref):
        gv = g_ref[...]
        mn = ADAM_B1 * m_ref[...] + (1.0 - ADAM_B1) * gv
        vn = ADAM_B2 * v_ref[...] + (1.0 - ADAM_B2) * (gv * gv)
        mo_ref[...] = mn
        vo_ref[...] = vn
        d_ref[...] = -ADAM_LR * ((mn * c1) / (jnp.sqrt(vn * c2) + ADAM_EPS) + ADAM_WD * w_ref[...])

    spec = pl.BlockSpec((tm, cols), lambda i: (i, 0))
    sds = jax.ShapeDtypeStruct((rows, cols), F32)
    return pl.pallas_call(
        body, name=name, grid=(rows // tm,),
        in_specs=[spec] * 4, out_specs=[spec] * 3, out_shape=[sds] * 3,
        compiler_params=_cparams(("parallel",)),
    )(w, g, m, v)


def add_n(arrs, out_dtype, name):
    shp = arrs[0].shape
    cols = shp[-1]
    flat = [a.reshape(-1, cols) for a in arrs]
    rows = flat[0].shape[0]
    tm = max(t for t in range(16, rows + 1, 16) if rows % t == 0 and t * cols <= 512 * 1024)

    def body(*refs):
        acc = refs[0][...].astype(F32)
        for r in refs[1:-1]:
            acc = acc + r[...].astype(F32)
        refs[-1][...] = acc.astype(refs[-1].dtype)

    spec = pl.BlockSpec((tm, cols), lambda i: (i, 0))
    out = pl.pallas_call(
        body, name=name, grid=(rows // tm,),
        in_specs=[spec] * len(flat), out_specs=spec, out_shape=jax.ShapeDtypeStruct((rows, cols), out_dtype),
        compiler_params=_cparams(("parallel",)),
    )(*flat)
    return out.reshape(shp)


def sum8(stack, name):
    _, rows, cols = stack.shape
    tm = rows if rows <= 2048 else _pick(rows, (512, 256, 128, 64, 8))

    def body(s_ref, o_ref):
        acc = s_ref[0]
        for k in range(1, 8):
            acc = acc + s_ref[k]
        o_ref[...] = acc

    return pl.pallas_call(
        body, name=name, grid=(rows // tm,),
        in_specs=[pl.BlockSpec((8, tm, cols), lambda i: (0, i, 0))],
        out_specs=pl.BlockSpec((tm, cols), lambda i: (i, 0)),
        out_shape=jax.ShapeDtypeStruct((rows, cols), F32),
        compiler_params=_cparams(("parallel",)),
    )(stack)


def _coords():
    return lax.axis_index("x"), lax.axis_index("y"), lax.axis_index("c")


def _other_chips(x, y):
    return [(1 - x, y), (x, 1 - y), (1 - x, 1 - y)]


_ANY = pl.BlockSpec(memory_space=pl.ANY)


def gather_chips(slabs):
    ns = len(slabs)

    def body(*refs):
        x_refs, out_refs = refs[:ns], refs[ns:2 * ns]
        send_sems, recv_sems = refs[2 * ns:]
        x, y, c = _coords()
        me = 2 * x + y
        sibling = (x, y, 1 - c)
        chips = _other_chips(x, y)

        def half(s, chip, hc):
            rh = slabs[s].shape[0] // 2
            return out_refs[s].at[chip, pl.ds(hc * rh, rh), :]

        def own_half(s):
            rh = slabs[s].shape[0] // 2
            return x_refs[s].at[pl.ds(c * rh, rh), :]

        sends = []
        for s in range(ns):
            for j, (px, py) in enumerate(chips):
                cp = pltpu.make_async_remote_copy(
                    src_ref=own_half(s), dst_ref=half(s, me, c), send_sem=send_sems.at[6 * s + j],
                    recv_sem=recv_sems.at[6 * s + j], device_id=(px, py, c), device_id_type=MESH)
                cp.start()
                sends.append(cp)
        for s in range(ns):
            for j, (px, py) in enumerate(chips):
                src = 2 * px + py
                landed = pltpu.make_async_remote_copy(
                    src_ref=half(s, src, c), dst_ref=half(s, src, c), send_sem=send_sems.at[6 * s + j],
                    recv_sem=recv_sems.at[6 * s + j], device_id=(px, py, c), device_id_type=MESH)
                landed.wait_recv()
                fw = pltpu.make_async_remote_copy(
                    src_ref=half(s, src, c), dst_ref=half(s, src, c), send_sem=send_sems.at[6 * s + 3 + j],
                    recv_sem=recv_sems.at[6 * s + 3 + j], device_id=sibling, device_id_type=MESH)
                fw.start()
                sends.append(fw)
        for s in range(ns):
            for j, (px, py) in enumerate(chips):
                src = 2 * px + py
                got = pltpu.make_async_remote_copy(
                    src_ref=half(s, src, 1 - c), dst_ref=half(s, src, 1 - c), send_sem=send_sems.at[6 * s + 3 + j],
                    recv_sem=recv_sems.at[6 * s + 3 + j], device_id=sibling, device_id_type=MESH)
                got.wait_recv()
        for cp in sends:
            cp.wait_send()

    return pl.pallas_call(
        body, name="gather_chips",
        in_specs=[_ANY] * ns, out_specs=[_ANY] * ns,
        out_shape=[jax.ShapeDtypeStruct((4,) + s.shape, s.dtype) for s in slabs],
        scratch_shapes=[pltpu.SemaphoreType.DMA((6 * ns,)), pltpu.SemaphoreType.DMA((6 * ns,))],
    )(*slabs)


def swap_halves(gss):
    ns = len(gss)

    def body(*refs):
        g_refs, out_refs = refs[:ns], refs[ns:2 * ns]
        send_sems, recv_sems = refs[2 * ns:]
        x, y, c = _coords()
        cps = []
        for s in range(ns):
            rh = gss[s].shape[1] // 2
            cp = pltpu.make_async_remote_copy(
                src_ref=g_refs[s].at[:, pl.ds((1 - c) * rh, rh), :], dst_ref=out_refs[s],
                send_sem=send_sems.at[s], recv_sem=recv_sems.at[s], device_id=(x, y, 1 - c), device_id_type=MESH)
            cp.start()
            cps.append(cp)
        for cp in cps:
            cp.wait()

    return pl.pallas_call(
        body, name="swap_halves",
        in_specs=[_ANY] * ns, out_specs=[_ANY] * ns,
        out_shape=[jax.ShapeDtypeStruct((4, g.shape[1] // 2, g.shape[2]), g.dtype) for g in gss],
        scratch_shapes=[pltpu.SemaphoreType.DMA((ns,)), pltpu.SemaphoreType.DMA((ns,))],
    )(*gss)


def scatter_chips(pbs):
    ns = len(pbs)

    def body(*refs):
        p_refs, out_refs = refs[:ns], refs[ns:2 * ns]
        send_sems, recv_sems = refs[2 * ns:]
        x, y, c = _coords()
        cps = []
        for s in range(ns):
            for j, (px, py) in enumerate(_other_chips(x, y)):
                cp = pltpu.make_async_remote_copy(
                    src_ref=p_refs[s].at[2 * px + py], dst_ref=out_refs[s].at[j], send_sem=send_sems.at[3 * s + j],
                    recv_sem=recv_sems.at[3 * s + j], device_id=(px, py, c), device_id_type=MESH)
                cp.start()
                cps.append(cp)
        for cp in cps:
            cp.wait()

    return pl.pallas_call(
        body, name="scatter_chips",
        in_specs=[_ANY] * ns, out_specs=[_ANY] * ns,
        out_shape=[jax.ShapeDtypeStruct((3,) + p.shape[1:], p.dtype) for p in pbs],
        scratch_shapes=[pltpu.SemaphoreType.DMA((3 * ns,)), pltpu.SemaphoreType.DMA((3 * ns,))],
    )(*pbs)


def join_halves(reds):
    ns = len(reds)

    def body(*refs):
        r_refs, out_refs = refs[:ns], refs[ns:2 * ns]
        send_sems, recv_sems = refs[2 * ns:]
        x, y, c = _coords()
        cps = []
        for s in range(ns):
            cp = pltpu.make_async_remote_copy(
                src_ref=r_refs[s], dst_ref=out_refs[s], send_sem=send_sems.at[s], recv_sem=recv_sems.at[s],
                device_id=(x, y, 1 - c), device_id_type=MESH)
            cp.start()
            cps.append(cp)
        for cp in cps:
            cp.wait()

    return pl.pallas_call(
        body, name="join_halves",
        in_specs=[_ANY] * ns, out_specs=[_ANY] * ns,
        out_shape=[jax.ShapeDtypeStruct(r.shape, r.dtype) for r in reds],
        scratch_shapes=[pltpu.SemaphoreType.DMA((ns,)), pltpu.SemaphoreType.DMA((ns,))],
    )(*reds)


def gather_all(vec, name):
    r, wd = vec.shape

    def body(v_ref, out_ref, send_sems, recv_sems):
        x, y, c = _coords()
        me = 4 * x + 2 * y + c
        cps = []
        for k in range(1, 8):
            mx, my, mc = (k >> 2) & 1, (k >> 1) & 1, k & 1
            peer = (x ^ mx, y ^ my, c ^ mc)
            cp = pltpu.make_async_remote_copy(
                src_ref=v_ref, dst_ref=out_ref.at[me],
                send_sem=send_sems.at[k - 1], recv_sem=recv_sems.at[k - 1], device_id=peer, device_id_type=MESH)
            cp.start()
            cps.append(cp)
        for k in range(1, 8):
            mx, my, mc = (k >> 2) & 1, (k >> 1) & 1, k & 1
            peer = (x ^ mx, y ^ my, c ^ mc)
            src = 4 * peer[0] + 2 * peer[1] + peer[2]
            got = pltpu.make_async_remote_copy(
                src_ref=v_ref, dst_ref=out_ref.at[src],
                send_sem=send_sems.at[k - 1], recv_sem=recv_sems.at[k - 1], device_id=peer, device_id_type=MESH)
            got.wait_recv()
        for cp in cps:
            cp.wait_send()

    return pl.pallas_call(
        body, name=name,
        in_specs=[_ANY], out_specs=_ANY,
        out_shape=jax.ShapeDtypeStruct((8, r, wd), vec.dtype),
        scratch_shapes=[pltpu.SemaphoreType.DMA((7,)), pltpu.SemaphoreType.DMA((7,))],
    )(vec)


def _pad_rows(a, rows):
    return jnp.pad(a, ((0, rows - a.shape[0]), (0, 0)))


def _pad_cols(a, cols):
    return jnp.pad(a, ((0, 0), (0, cols - a.shape[1])))


def local_step(x, c, ctx, c_ctx, target, wt, sm):
    t, tc = x.shape[0], ctx.shape[0]
    tp = t + tc
    nct = tc // TR
    ncc = tc // LC
    nc = tp // LC

    w_in = wt["w_in"]
    segs = {"q": (0, D), "k": (D, 2 * D), "v": (2 * D, 3 * D), "g": (3 * D, 3 * D + NGATE)}
    base = 3 * D + NGATE
    for n_i, nm in enumerate(("o", "u", "vg", "gm", "gg")):
        segs[nm] = (base + n_i * D, base + (n_i + 1) * D)
    order = ("o", "u", "vg", "gm", "gg", "v", "q", "k")
    w_main = jnp.concatenate([w_in[:, segs[nm][0]:segs[nm][1]] for nm in order], axis=1)
    w_g = _pad_cols(w_in[:, segs["g"][0]:segs["g"][1]], LANE)
    w_main_t = w_main.T
    w_g_t = w_g.T

    cc = _pad_rows(jnp.concatenate([c.reshape(1, D), c_ctx.reshape(1, D)], axis=0), 16)
    modv = mod_fwd(cc, wt["w_mod"], sm["b_mod"].reshape(1, NMOD * D))
    mx = modv[0].reshape(NMOD, D)
    mc = modv[1].reshape(NMOD, D)
    tab1 = _pad_rows(jnp.stack([mc[0], mc[1], mx[0], mx[1]]), 8)
    tab2 = _pad_rows(jnp.stack([mx[2], mx[3], mx[4]]), 8)
    tab3 = _pad_rows(mx[5:6], 8)

    g1 = sm["norm1_g"].reshape(1, D)
    g2 = sm["norm2_g"].reshape(1, D)
    gfin = sm["final_g"].reshape(1, D)
    gh = sm["head_norm_g"].reshape(1, D)
    ln_g = sm["sgu_ln_g"].reshape(1, D)
    ln_b = sm["sgu_ln_b"].reshape(1, D)
    ws = sm["w_s"].reshape(SG * SCH, SCH).astype(BF16)
    bs_t = _pad_cols(sm["b_s"].reshape(SG, SCH).T, LANE)
    conv_w = _pad_rows(sm["conv_qk"].reshape(3, 2 * D), 8)
    b_gate = _pad_cols(sm["b_gate"].reshape(1, NGATE), LANE)
    wc = _pad_rows(sm["w_ffn_conv"].reshape(9, DFF), 16)

    hn1 = norm1_fwd(ctx, x, g1, tab1, nct)
    zmain = mm_nn(hn1, w_main, BF16, "mm_zmain")
    zg = mm_nn(hn1, w_g, F32, "mm_zg")
    qa, ka, gcol, zgb = qkconv_fwd(zmain, zg, conv_w, b_gate, nct)
    grow = gcol[:, :16].reshape(nc, LC, 16).transpose(0, 2, 1)
    hf, hb, csf, csb, nsf, nsb, msf, msb = mlstm_fwd(qa, ka, zmain, gcol, grow, ncc)
    yms = mixer_fwd(hf, hb, zmain, gh, ln_g, ln_b, ws, bs_t, nct)
    w_br = jnp.stack([wt["w_branch_mlstm"], wt["w_branch_sgu"]])
    pp = mm_nn(yms, w_br, BF16, "mm_branch")
    y = merge_fwd(zmain, pp, nct)
    out = mm_nn(y, wt["w_out"], F32, "mm_out")
    h1, hn2 = norm2_fwd(x, out, g2, tab2)
    up = mm_nn(hn2, wt["w_up"], BF16, "mm_up")
    act, ac = ffn_act_fwd(up, wc)
    f = mm_nn(act, wt["w_down"], F32, "mm_down")
    dh2, df, acc_h = head_fwd_bwd(h1, f, target, gfin, tab3)
    loss = acc_h[0, 0]

    g_w_down = mm_tn(act, df, "mmt_down", BF16)
    dact = mm_nn(df, wt["w_down"].T, BF16, "mm_ddown")
    dup, dac = ffn_act_bwd(up, ac, dact)
    dup, g_wc = ffn_conv_bwd(dup, up, dac, wc)
    g_w_up = mm_tn(hn2, dup, "mmt_up", BF16)
    dhn2 = mm_nn(dup, wt["w_up"].T, F32, "mm_dup")
    dh1, dout, acc_2 = norm2_bwd(h1, dhn2, dh2, out, g2, tab2)
    g_w_out = mm_tn(y, dout, "mmt_out", BF16)
    dy = mm_nn(dout, wt["w_out"].T, BF16, "mm_dout")
    dpp, dz = merge_bwd(zmain, pp, dy, tp, nct)
    g_w_br = mm_tn(yms, dpp, "mmt_branch", BF16)
    dyms = mm_nn(dpp, jnp.stack([wt["w_branch_mlstm"].T, wt["w_branch_sgu"].T]), BF16, "mm_dbranch")
    dz, dhm, g_gh, g_lng, g_lnb, g_ws, g_bs = mixer_bwd(dz, hf, hb, zmain, dyms, gh, ln_g, ln_b, ws, bs_t, nct)
    (dqf, dkf, dvf, colf, rowf, dqb, dkb, dvb, colb, rowb) = mlstm_bwd(
        qa, ka, zmain, gcol, grow, (csf, csb, nsf, nsb, msf, msb), dhm, ncc)

    csum_f = rowf[:, :4, :].transpose(0, 2, 1).reshape(tp, 4)
    csum_b = rowb[:, :4, :].transpose(0, 2, 1).reshape(tp, 4)
    z4 = jnp.zeros((tp, 4), F32)

    def lanes(parts):
        return _pad_cols(jnp.concatenate(parts, axis=1), LANE)

    rg = lanes([z4, colf[:, 0:4], z4, colb[:, 0:4]])
    csi = lanes([csum_f, z4, csum_b, z4])
    csf_l = lanes([z4, csum_f, z4, csum_b])
    ali = lanes([colf[:, 4:8], z4, colb[:, 4:8], z4])
    gam = lanes([z4, colf[:, 8:12], z4, colb[:, 8:12]])
    dzg, g_bgate = gates_bwd(rg, gam, csi, csf_l, ali, zgb)

    dc, g_convw = qkconv_bwd_a(zmain, dqf, dqb, dkf, dkb, conv_w, nct)
    dz = qkconv_bwd_b(dz, dc, conv_w, nct)
    dz = add_into_dz(dz, dvf, dvb, CB_V)

    g_w_main = mm_tn(hn1, dz, "mmt_main", BF16)
    g_w_g = mm_tn(hn1, dzg, "mmt_g", BF16)
    da = mm_nn(dz, w_main_t, F32, "mm_dmain")
    db = mm_nn(dzg, w_g_t, F32, "mm_dg")
    grad_x, acc_1 = norm1_bwd(ctx, x, da, db, dh1, g1, tab1, nct)

    blk = lambda cb: g_w_main[:, cb * D:(cb + 1) * D]
    g_w_in = jnp.concatenate([blk(CB_Q), blk(CB_K), blk(CB_V), g_w_g[:, :NGATE], blk(CB_O), blk(CB_U), blk(CB_VG),
                              blk(CB_GM), blk(CB_GG)], axis=1)

    d_modx = jnp.concatenate([acc_1[2], acc_1[3], acc_2[3], acc_2[0], acc_2[1], acc_h[2]])
    d_modc = jnp.concatenate([acc_1[0], acc_1[1], jnp.zeros((4 * D,), F32)])
    d_modb = jnp.concatenate([acc_1[5], acc_1[6], acc_2[3], acc_2[0], acc_2[1], acc_h[2]])

    big = {"w_in": g_w_in, "w_branch_mlstm": g_w_br[0], "w_branch_sgu": g_w_br[1], "w_out": g_w_out,
           "w_up": g_w_up, "w_down": g_w_down}
    small = {"b_mod": d_modb, "norm1_g": acc_1[4], "b_gate": g_bgate[0, :NGATE], "conv_qk": g_convw[:3].reshape(-1),
             "head_norm_g": g_gh[0], "sgu_ln_g": g_lng[0], "sgu_ln_b": g_lnb[0], "w_s": g_ws.reshape(-1),
             "b_s": g_bs[:, :SG].T.reshape(-1), "norm2_g": acc_2[2], "w_ffn_conv": g_wc[:9].reshape(-1),
             "final_g": acc_h[1]}
    return loss, grad_x, big, small, d_modx, d_modc


def mod_bwd_w(a_all, dm_all, name):
    n = dm_all.shape[1]
    tn = _pick(n, (512, 128))

    def body(a_ref, d_ref, o_ref):
        o_ref[...] = _dot_tn(_silu(a_ref[...]), d_ref[...])

    return pl.pallas_call(
        body, name=name, grid=(n // tn,),
        in_specs=[_bc(16, D), pl.BlockSpec((16, tn), lambda j: (0, j))],
        out_specs=pl.BlockSpec((D, tn), lambda j: (0, j)),
        out_shape=jax.ShapeDtypeStruct((D, n), F32),
        compiler_params=_cparams(("parallel",)),
    )(a_all, dm_all)


def mod_bwd_cctx(dmc, w_mod_t, c_ctx):
    def body(d_ref, w_ref, c_ref, o_ref):
        o_ref[...] = _dot(d_ref[...], w_ref[...]) * _dsilu(c_ref[...])

    return pl.pallas_call(
        body, name="mod_bwd_cctx", grid=(1,),
        in_specs=[_bc(16, 2 * D), _bc(2 * D, D), _bc(1, D)],
        out_specs=_bc(16, D),
        out_shape=jax.ShapeDtypeStruct((16, D), F32),
        compiler_params=_cparams(("arbitrary",)),
    )(dmc, w_mod_t, c_ctx)


BIG = ("w_mod", "w_in", "w_branch_mlstm", "w_branch_sgu", "w_out", "w_up", "w_down")
BIG_AXIS = {"w_mod": 1, "w_in": 1, "w_branch_mlstm": 0, "w_branch_sgu": 0, "w_out": 0, "w_up": 1, "w_down": 0}
SMALL = ("c_ctx", "b_mod", "norm1_g", "b_gate", "conv_qk", "head_norm_g", "sgu_ln_g", "sgu_ln_b", "w_s", "b_s",
         "norm2_g", "w_ffn_conv", "final_g")
SMALL_SHARDED = {"conv_qk": (3, 2 * D), "w_ffn_conv": (9, DFF)}
PACK_ALIGN = 32 * D


def _pack(arrs, dtype, align=PACK_ALIGN, width=D):
    flat = jnp.concatenate([a.reshape(-1).astype(dtype) for a in arrs])
    n = flat.shape[0]
    padded = -(-n // align) * align
    return jnp.pad(flat, (0, padded - n)).reshape(padded // width, width)


def _unpack(slab, shapes):
    flat = slab.reshape(-1)
    outs, off = [], 0
    for shp in shapes:
        n = math.prod(shp)
        outs.append(flat[off:off + n].reshape(shp))
        off += n
    return outs


def _round_up(n, m):
    return -(-n // m) * m


def kernel(x, c, ctx, c_ctx, w_mod, b_mod, norm1_g, w_in, b_gate, conv_qk, head_norm_g, sgu_ln_g, sgu_ln_b, w_s, b_s, w_branch_mlstm, w_branch_sgu, w_out, norm2_g, w_up, w_ffn_conv, w_down, final_g, loss_target, m_c_ctx, m_w_mod, m_b_mod, m_norm1_g, m_w_in, m_b_gate, m_conv_qk, m_head_norm_g, m_sgu_ln_g, m_sgu_ln_b, m_w_s, m_b_s, m_w_branch_mlstm, m_w_branch_sgu, m_w_out, m_norm2_g, m_w_up, m_w_ffn_conv, m_w_down, m_final_g, v_c_ctx, v_w_mod, v_b_mod, v_norm1_g, v_w_in, v_b_gate, v_conv_qk, v_head_norm_g, v_sgu_ln_g, v_sgu_ln_b, v_w_s, v_b_s, v_w_branch_mlstm, v_w_branch_sgu, v_w_out, v_norm2_g, v_w_up, v_w_ffn_conv, v_w_down, v_final_g):
    params = dict(c_ctx=c_ctx, w_mod=w_mod, b_mod=b_mod, norm1_g=norm1_g, w_in=w_in, b_gate=b_gate, conv_qk=conv_qk,
                  head_norm_g=head_norm_g, sgu_ln_g=sgu_ln_g, sgu_ln_b=sgu_ln_b, w_s=w_s, b_s=b_s,
                  w_branch_mlstm=w_branch_mlstm, w_branch_sgu=w_branch_sgu, w_out=w_out, norm2_g=norm2_g, w_up=w_up,
                  w_ffn_conv=w_ffn_conv, w_down=w_down, final_g=final_g)
    mom_m = dict(c_ctx=m_c_ctx, w_mod=m_w_mod, b_mod=m_b_mod, norm1_g=m_norm1_g, w_in=m_w_in, b_gate=m_b_gate,
                 conv_qk=m_conv_qk, head_norm_g=m_head_norm_g, sgu_ln_g=m_sgu_ln_g, sgu_ln_b=m_sgu_ln_b, w_s=m_w_s,
                 b_s=m_b_s, w_branch_mlstm=m_w_branch_mlstm, w_branch_sgu=m_w_branch_sgu, w_out=m_w_out,
                 norm2_g=m_norm2_g, w_up=m_w_up, w_ffn_conv=m_w_ffn_conv, w_down=m_w_down, final_g=m_final_g)
    mom_v = dict(c_ctx=v_c_ctx, w_mod=v_w_mod, b_mod=v_b_mod, norm1_g=v_norm1_g, w_in=v_w_in, b_gate=v_b_gate,
                 conv_qk=v_conv_qk, head_norm_g=v_head_norm_g, sgu_ln_g=v_sgu_ln_g, sgu_ln_b=v_sgu_ln_b, w_s=v_w_s,
                 b_s=v_b_s, w_branch_mlstm=v_w_branch_mlstm, w_branch_sgu=v_w_branch_sgu, w_out=v_w_out,
                 norm2_g=v_norm2_g, w_up=v_w_up, w_ffn_conv=v_w_ffn_conv, w_down=v_w_down, final_g=v_final_g)
    chip = 2 * lax.axis_index("x") + lax.axis_index("y")

    shard2d = {n: params[n].reshape(params[n].shape[-2:]) for n in BIG}
    conv_sh = conv_qk.reshape(3, -1)
    fconv_sh = w_ffn_conv.reshape(9, -1)

    dev = 2 * chip + lax.axis_index("c")

    col_names, row_names = ("w_mod", "w_up", "w_in"), ("w_branch_mlstm", "w_branch_sgu", "w_out", "w_down")
    col_w = [shard2d[n].shape[1] for n in col_names]
    row_h = [shard2d[n].shape[0] for n in row_names]
    col_slab = _pad_cols(jnp.concatenate([shard2d[n].astype(BF16) for n in col_names], axis=1),
                         _round_up(sum(col_w), LANE))
    row_slab = jnp.concatenate([shard2d[n].astype(BF16) for n in row_names], axis=0)
    col_all, row_all = gather_chips([col_slab, row_slab])
    col_all = jnp.stack([jnp.where(chip == j, col_slab, col_all[j]) for j in range(4)])
    row_all = jnp.stack([jnp.where(chip == j, row_slab, row_all[j]) for j in range(4)])
    wt = {}
    off = 0
    for n, wd in zip(col_names, col_w):
        wt[n] = jnp.concatenate([col_all[j, :, off:off + wd] for j in range(4)], axis=1)
        off += wd
    off = 0
    for n, ht in zip(row_names, row_h):
        wt[n] = jnp.concatenate([row_all[j, off:off + ht, :] for j in range(4)], axis=0)
        off += ht

    cvec = _pack([conv_sh, fconv_sh], F32, align=8 * LANE, width=LANE)
    call = gather_all(cvec, "gather_conv")
    cparts = [_unpack(jnp.where(dev == 2 * j, cvec, call[2 * j]), [conv_sh.shape, fconv_sh.shape]) for j in range(4)]
    conv_full = jnp.concatenate([p[0] for p in cparts], axis=1)
    fconv_full = jnp.concatenate([p[1] for p in cparts], axis=1)

    sm = dict(b_mod=b_mod, norm1_g=norm1_g, b_gate=b_gate, conv_qk=conv_full, head_norm_g=head_norm_g,
              sgu_ln_g=sgu_ln_g, sgu_ln_b=sgu_ln_b, w_s=w_s, b_s=b_s, norm2_g=norm2_g, w_ffn_conv=fconv_full,
              final_g=final_g)

    loss_l, grad_x, gbig, gsmall, d_modx, d_modc = local_step(
        x[0], c, ctx[0], c_ctx, loss_target[0], wt, sm)

    gcol_names, gcol_w = col_names[1:], col_w[1:]
    gcol_pad = _round_up(sum(gcol_w), LANE)

    def chip_cols(j):
        return _pad_cols(jnp.concatenate([gbig[n][:, j * wd:(j + 1) * wd] for n, wd in zip(gcol_names, gcol_w)],
                                         axis=1), gcol_pad)

    def chip_rows(j):
        return jnp.concatenate([gbig[n][j * ht:(j + 1) * ht] for n, ht in zip(row_names, row_h)], axis=0)

    gss = [jnp.stack([chip_cols(j) for j in range(4)]), jnp.stack([chip_rows(j) for j in range(4)])]
    cidx = lax.axis_index("c")
    from_sib = swap_halves(gss)
    reds = []
    pair_bf, own_terms = [], []
    for s, (gs, fs) in enumerate(zip(gss, from_sib)):
        rh = gs.shape[1] // 2
        my_half = lax.dynamic_slice_in_dim(gs, cidx * rh, rh, axis=1)
        pair_bf.append(add_n([my_half, fs], BF16, "pair_sum_%d" % s))
        own_terms.append([lax.dynamic_index_in_dim(my_half, chip, axis=0, keepdims=False),
                          lax.dynamic_index_in_dim(fs, chip, axis=0, keepdims=False)])
    recv = scatter_chips(pair_bf)
    for s in range(2):
        reds.append(add_n(own_terms[s] + [recv[s][0], recv[s][1], recv[s][2]], F32, "chip_sum_%d" % s))
    others = join_halves(reds)
    full_red = [jnp.where(cidx == 0, jnp.concatenate([m, o], axis=0), jnp.concatenate([o, m], axis=0))
                for m, o in zip(reds, others)]
    g_shard = {}
    off = 0
    for n, wd in zip(gcol_names, gcol_w):
        g_shard[n] = full_red[0][:, off:off + wd]
        off += wd
    off = 0
    for n, ht in zip(row_names, row_h):
        g_shard[n] = full_red[1][off:off + ht]
        off += ht

    small_order = ("b_mod", "norm1_g", "b_gate", "conv_qk", "head_norm_g", "sgu_ln_g", "sgu_ln_b", "w_s", "b_s", "norm2_g",
                   "w_ffn_conv", "final_g")
    vec_parts = [gsmall[n] for n in small_order] + [d_modx, d_modc, c.reshape(-1), loss_l.reshape(1)]
    vec_shapes = [a.shape for a in vec_parts]
    vec = _pack(vec_parts, F32, align=8 * LANE, width=LANE)
    allv = gather_all(vec, "gather_small")
    allv = jnp.stack([jnp.where(dev == k, vec, allv[k]) for k in range(8)])
    summed = sum8(allv, "small_sum")
    s_parts = _unpack(summed, vec_shapes)
    g_small = dict(zip(small_order, s_parts[:len(small_order)]))
    dmc_sum = s_parts[len(small_order) + 1]
    loss = s_parts[-1][0]
    per_dev = [_unpack(allv[k], vec_shapes) for k in range(8)]
    dmx_all = jnp.stack([p[len(small_order)] for p in per_dev])
    c_all = jnp.stack([p[len(small_order) + 2] for p in per_dev])

    a_all = _pad_rows(jnp.concatenate([c_all, c_ctx.reshape(1, D)], axis=0), 16)
    dm_all = _pad_rows(jnp.concatenate([dmx_all, dmc_sum.reshape(1, NMOD * D)], axis=0), 16)
    ncol = NMOD * D // 4
    dm_shard = lax.dynamic_slice_in_dim(dm_all, chip * ncol, ncol, axis=1)
    g_shard["w_mod"] = mod_bwd_w(a_all, dm_shard, "mod_bwd_w")
    w_mod_t = wt["w_mod"][:, :2 * D].T
    g_cctx = mod_bwd_cctx(_pad_rows(dmc_sum[:2 * D].reshape(1, 2 * D), 16), w_mod_t, c_ctx.reshape(1, D))[0]
    g_small["c_ctx"] = g_cctx

    results = {}
    for n in BIG:
        shp = params[n].shape
        d_, m_, v_ = adamw(shard2d[n], g_shard[n], mom_m[n].reshape(shard2d[n].shape),
                           mom_v[n].reshape(shard2d[n].shape), "adamw_" + n)
        results[n] = (g_shard[n].reshape(shp), d_.reshape(shp), m_.reshape(shp), v_.reshape(shp))

    conv_g = lax.dynamic_slice_in_dim(g_small["conv_qk"].reshape(3, 2 * D), chip * (2 * D // 4), 2 * D // 4, axis=1)
    fconv_g = lax.dynamic_slice_in_dim(g_small["w_ffn_conv"].reshape(9, DFF), chip * (DFF // 4), DFF // 4, axis=1)
    g_small["conv_qk"] = conv_g
    g_small["w_ffn_conv"] = fconv_g
    w_list = [params[n].reshape(-1) for n in SMALL]
    g_list = [g_small[n].reshape(-1) for n in SMALL]
    m_list = [mom_m[n].reshape(-1) for n in SMALL]
    v_list = [mom_v[n].reshape(-1) for n in SMALL]
    sm_shapes = [params[n].shape for n in SMALL]
    pk = lambda lst: _pack(lst, F32, align=8 * LANE, width=LANE)
    gp = pk(g_list)
    d_s, m_s, v_s = adamw(pk(w_list), gp, pk(m_list), pk(v_list), "adamw_small")
    for n, gg, dd, mm, vv in zip(SMALL, _unpack(gp, sm_shapes), _unpack(d_s, sm_shapes), _unpack(m_s, sm_shapes),
                                 _unpack(v_s, sm_shapes)):
        results[n] = (gg, dd, mm, vv)

    order = ("c_ctx", "w_mod", "b_mod", "norm1_g", "w_in", "b_gate", "conv_qk", "head_norm_g", "sgu_ln_g", "sgu_ln_b",
             "w_s", "b_s", "w_branch_mlstm", "w_branch_sgu", "w_out", "norm2_g", "w_up", "w_ffn_conv", "w_down",
             "final_g")
    outs = [loss, grad_x[None]]
    for k in range(4):
        outs += [results[n][k] for n in order]
    return tuple(outs)
```

```python
import functools
import math

import jax
import jax.numpy as jnp
from jax import lax
from jax.experimental import pallas as pl
from jax.experimental.pallas import tpu as pltpu

F32 = jnp.float32
BF16 = jnp.bfloat16

D = 1024
NH = 4
DH = 256
LC = 256
GW = 64
SG = 4
SGD = 256
SCH = 128
DFF = 2816
NMOD = 6
NGATE = 16
NIN = 8208
EPS = 1e-6
M_INIT = -1e30
TR = 256
LANE = 128
VMEM_LIMIT = 56 * 1024 * 1024
MESH = pl.DeviceIdType.MESH

ADAM_LR = 0.001
ADAM_B1 = 0.9
ADAM_B2 = 0.999
ADAM_EPS = 1e-08
ADAM_WD = 0.01
ADAM_STEP = 10

CB_O, CB_U, CB_VG, CB_GM, CB_GG, CB_V, CB_Q, CB_K = range(8)


def _pick(n, cands):
    for c in cands:
        if n % c == 0:
            return c
    return n


def _cparams(sem):
    return pltpu.CompilerParams(dimension_semantics=sem, vmem_limit_bytes=VMEM_LIMIT)


def _sigmoid(x):
    return 1.0 / (1.0 + jnp.exp(-x))


def _silu(x):
    return x * _sigmoid(x)


def _dsilu(x):
    s = _sigmoid(x)
    return s * (1.0 + x * (1.0 - s))


_GC = math.sqrt(2.0 / math.pi)


def _gelu(x):
    return 0.5 * x * (1.0 + jnp.tanh(_GC * (x + 0.044715 * x * x * x)))


def _dgelu(x):
    t = jnp.tanh(_GC * (x + 0.044715 * x * x * x))
    return 0.5 * (1.0 + t) + 0.5 * x * (1.0 - t * t) * _GC * (1.0 + 3.0 * 0.044715 * x * x)


def _dot(a, b):
    return jnp.dot(a.astype(BF16), b.astype(BF16), preferred_element_type=F32)


def _dot_nt(a, b):
    return lax.dot_general(a.astype(BF16), b.astype(BF16), (((1,), (1,)), ((), ())), preferred_element_type=F32)


def _dot_tn(a, b):
    return lax.dot_general(a.astype(BF16), b.astype(BF16), (((0,), (0,)), ((), ())), preferred_element_type=F32)


def _dot_tn_mxu(a, b):
    m = a.shape[1]
    eye = (lax.broadcasted_iota(jnp.int32, (m, m), 0) == lax.broadcasted_iota(jnp.int32, (m, m), 1)).astype(BF16)
    return _dot(_dot_nt(eye, a), b)


def _exact_dot(tri, x):
    x1 = x.astype(BF16)
    r1 = x - x1.astype(F32)
    x2 = r1.astype(BF16)
    x3 = (r1 - x2.astype(F32)).astype(BF16)
    return (jnp.dot(tri, x1, preferred_element_type=F32) + jnp.dot(tri, x2, preferred_element_type=F32)
            + jnp.dot(tri, x3, preferred_element_type=F32))


def _rb(tm, w, col=0, off=0):
    return pl.BlockSpec((tm, w), lambda i: (i + off, col))


def _bc(r, w):
    return pl.BlockSpec((r, w), lambda i: (0, 0))


def mm_nn(a, b, out_dtype, name):
    squeeze = a.ndim == 2
    if squeeze:
        a, b = a[None], b[None]
    g, m, k = a.shape
    n = b.shape[2]
    tm = _pick(m, (1280, 1024, 512, 256, 128))
    tn = _pick(n, (2048, 1408, 1024, 512, 128))
    tk = _pick(k, (2048, 1408, 1024, 512, 128))
    nk = k // tk

    def body(a_ref, b_ref, o_ref, *scr):
        if nk == 1:
            o_ref[0] = _dot(a_ref[0], b_ref[0]).astype(o_ref.dtype)
        else:
            acc_ref, = scr
            kk = pl.program_id(3)

            @pl.when(kk == 0)
            def _():
                acc_ref[...] = jnp.zeros_like(acc_ref)

            acc_ref[...] += _dot(a_ref[0], b_ref[0])

            @pl.when(kk == nk - 1)
            def _():
                o_ref[0] = acc_ref[...].astype(o_ref.dtype)

    out = pl.pallas_call(
        body, name=name, grid=(g, n // tn, m // tm, nk),
        in_specs=[pl.BlockSpec((1, tm, tk), lambda gi, j, i, kk: (gi, i, kk)),
                  pl.BlockSpec((1, tk, tn), lambda gi, j, i, kk: (gi, kk, j))],
        out_specs=pl.BlockSpec((1, tm, tn), lambda gi, j, i, kk: (gi, i, j)),
        out_shape=jax.ShapeDtypeStruct((g, m, n), out_dtype),
        scratch_shapes=[] if nk == 1 else [pltpu.VMEM((tm, tn), F32)],
        compiler_params=_cparams(("parallel", "parallel", "parallel", "arbitrary")),
    )(a, b)
    return out[0] if squeeze else out


def mm_tn(a, b, name, out_dtype=F32):
    squeeze = a.ndim == 2
    if squeeze:
        a, b = a[None], b[None]
    g, t, ka = a.shape
    n = b.shape[2]
    tka = _pick(ka, (1024, 1408, 512, 128))
    tn = _pick(n, (2048, 1408, 1024, 512, 128))
    tt = _pick(t, (1280, 1024, 512, 256, 128))
    nt = t // tt

    def body(a_ref, b_ref, o_ref, acc_ref):
        tt_i = pl.program_id(3)

        @pl.when(tt_i == 0)
        def _():
            acc_ref[...] = jnp.zeros_like(acc_ref)

        acc_ref[...] += _dot_tn(a_ref[0], b_ref[0])

        @pl.when(tt_i == nt - 1)
        def _():
            o_ref[0] = acc_ref[...].astype(o_ref.dtype)

    out = pl.pallas_call(
        body, name=name, grid=(g, ka // tka, n // tn, nt),
        in_specs=[pl.BlockSpec((1, tt, tka), lambda gi, i, j, ti: (gi, ti, i)),
                  pl.BlockSpec((1, tt, tn), lambda gi, i, j, ti: (gi, ti, j))],
        out_specs=pl.BlockSpec((1, tka, tn), lambda gi, i, j, ti: (gi, i, j)),
        out_shape=jax.ShapeDtypeStruct((g, ka, n), out_dtype),
        scratch_shapes=[pltpu.VMEM((tka, tn), F32)],
        compiler_params=_cparams(("parallel", "parallel", "parallel", "arbitrary")),
    )(a, b)
    return out[0] if squeeze else out


def mod_fwd(cc, w_mod, b_mod):
    n = w_mod.shape[1]

    def body(c_ref, w_ref, b_ref, o_ref):
        o_ref[...] = _dot(_silu(c_ref[...]), w_ref[...]) + b_ref[...]

    return pl.pallas_call(
        body, name="mod_fwd", grid=(n // D,),
        in_specs=[_bc(16, D), pl.BlockSpec((D, D), lambda j: (0, j)), pl.BlockSpec((1, D), lambda j: (0, j))],
        out_specs=pl.BlockSpec((16, D), lambda j: (0, j)),
        out_shape=jax.ShapeDtypeStruct((16, n), F32),
        compiler_params=_cparams(("parallel",)),
    )(cc, w_mod, b_mod)


def _ctx_x_specs(n_ctx_tiles):
    return [pl.BlockSpec((TR, D), lambda i: (jnp.minimum(i, n_ctx_tiles - 1), 0)),
            pl.BlockSpec((TR, D), lambda i: (jnp.maximum(i - n_ctx_tiles, 0), 0))]


def norm1_fwd(ctx, x, g, tab, n_ctx_tiles):
    tp = ctx.shape[0] + x.shape[0]

    def body(c_ref, x_ref, g_ref, tab_ref, o_ref):
        is_ctx = pl.program_id(0) < n_ctx_tiles
        x = jnp.where(is_ctx, c_ref[...], x_ref[...])
        r = lax.rsqrt(jnp.mean(x * x, axis=-1, keepdims=True) + EPS)
        nrm = x * r * g_ref[...]
        sh = jnp.where(is_ctx, tab_ref[0:1, :], tab_ref[2:3, :])
        sc = jnp.where(is_ctx, tab_ref[1:2, :], tab_ref[3:4, :])
        o_ref[...] = (nrm * (1.0 + sc) + sh).astype(BF16)

    return pl.pallas_call(
        body, name="norm1_fwd", grid=(tp // TR,),
        in_specs=_ctx_x_specs(n_ctx_tiles) + [_bc(1, D), _bc(8, D)],
        out_specs=_rb(TR, D),
        out_shape=jax.ShapeDtypeStruct((tp, D), BF16),
        compiler_params=_cparams(("parallel",)),
    )(ctx, x, g, tab)


def norm2_fwd(x, out, g, tab):
    t = x.shape[0]

    def body(x_ref, o_in_ref, g_ref, tab_ref, h1_ref, hn_ref):
        h1 = x_ref[...] + tab_ref[0:1, :] * o_in_ref[...].astype(F32)
        h1_ref[...] = h1
        r = lax.rsqrt(jnp.mean(h1 * h1, axis=-1, keepdims=True) + EPS)
        nrm = h1 * r * g_ref[...]
        hn_ref[...] = (nrm * (1.0 + tab_ref[2:3, :]) + tab_ref[1:2, :]).astype(BF16)

    return pl.pallas_call(
        body, name="norm2_fwd", grid=(t // TR,),
        in_specs=[_rb(TR, D), _rb(TR, D), _bc(1, D), _bc(8, D)],
        out_specs=[_rb(TR, D), _rb(TR, D)],
        out_shape=[jax.ShapeDtypeStruct((t, D), F32), jax.ShapeDtypeStruct((t, D), BF16)],
        compiler_params=_cparams(("parallel",)),
    )(x, out, g, tab)


def _halo_specs(tm, w, col, n_rows, hb):
    per = tm // hb
    last = n_rows // hb - 1
    prev = pl.BlockSpec((hb, w), lambda i: (jnp.maximum(i * per - 1, 0), col))
    nxt = pl.BlockSpec((hb, w), lambda i: (jnp.minimum((i + 1) * per, last), col))
    return prev, nxt


def _shift_rows(x, prev_row, next_row):
    tm = x.shape[0]
    rid = lax.broadcasted_iota(jnp.int32, x.shape, 0)
    xm1 = jnp.where(rid == 0, prev_row, pltpu.roll(x, 1, 0))
    xp1 = jnp.where(rid == tm - 1, next_row, pltpu.roll(x, tm - 1, 0))
    return xm1, xp1


def _seq_edges(i, n_ctx_tiles, n_tiles):
    first = jnp.logical_or(i == 0, i == n_ctx_tiles)
    last = jnp.logical_or(i == n_ctx_tiles - 1, i == n_tiles - 1)
    return first, last


def qkconv_fwd(zmain, zg, conv_w, b_gate, n_ctx_tiles):
    tp = zmain.shape[0]
    nt = tp // TR
    w2 = 2 * D
    prev_s, next_s = _halo_specs(TR, w2, CB_Q // 2, tp, 16)

    def body(z_ref, zp_ref, zn_ref, w_ref, zg_ref, bg_ref, q_ref, k_ref, g_ref, zgb_ref):
        i = pl.program_id(0)
        first, last = _seq_edges(i, n_ctx_tiles, nt)
        z = z_ref[...].astype(F32)
        pr = jnp.where(first, 0.0, zp_ref[15:16, :].astype(F32))
        nx = jnp.where(last, 0.0, zn_ref[0:1, :].astype(F32))
        zm1, zp1 = _shift_rows(z, pr, nx)
        cv = w_ref[0:1, :] * zm1 + w_ref[1:2, :] * z + w_ref[2:3, :] * zp1
        a = _silu(cv)
        q_ref[...] = (a[:, :D] * (DH ** -0.5)).astype(BF16)
        k_ref[...] = a[:, D:].astype(BF16)
        zgb = zg_ref[...] + bg_ref[...]
        zgb_ref[...] = zgb
        logf = jnp.minimum(zgb, 0.0) - jnp.log(1.0 + jnp.exp(-jnp.abs(zgb)))
        rr = lax.broadcasted_iota(jnp.int32, (TR, TR), 0)
        cc = lax.broadcasted_iota(jnp.int32, (TR, TR), 1)
        same = (rr // LC) == (cc // LC)
        low = jnp.where(jnp.logical_and(same, cc <= rr), 1.0, 0.0).astype(BF16)
        upp = jnp.where(jnp.logical_and(same, cc >= rr), 1.0, 0.0).astype(BF16)
        bf = _exact_dot(low, logf)
        bb = _exact_dot(upp, logf)
        lane = lax.broadcasted_iota(jnp.int32, (TR, LANE), 1)
        g = jnp.where(jnp.logical_and(lane >= 4, lane < 8), bf,
                      jnp.where(jnp.logical_and(lane >= 12, lane < 16), bb, zgb))
        g_ref[...] = g

    return pl.pallas_call(
        body, name="qkconv_fwd", grid=(nt,),
        in_specs=[_rb(TR, w2, CB_Q // 2), prev_s, next_s, _bc(8, w2), _rb(TR, LANE), _bc(1, LANE)],
        out_specs=[_rb(TR, D), _rb(TR, D), _rb(TR, LANE), _rb(TR, LANE)],
        out_shape=[jax.ShapeDtypeStruct((tp, D), BF16), jax.ShapeDtypeStruct((tp, D), BF16),
                   jax.ShapeDtypeStruct((tp, LANE), F32), jax.ShapeDtypeStruct((tp, LANE), F32)],
        compiler_params=_cparams(("parallel",)),
    )(zmain, zmain, zmain, conv_w, zg, b_gate)


def qkconv_bwd_a(zmain, dqf, dqb, dkf, dkb, conv_w, n_ctx_tiles):
    tp = zmain.shape[0]
    nt = tp // TR
    w2 = 2 * D
    prev_s, next_s = _halo_specs(TR, w2, CB_Q // 2, tp, 16)

    def body(z_ref, zp_ref, zn_ref, w_ref, dqf_ref, dqb_ref, dkf_ref, dkb_ref, dc_ref, dw_ref):
        i = pl.program_id(0)
        first, last = _seq_edges(i, n_ctx_tiles, nt)
        z = z_ref[...].astype(F32)
        pr = jnp.where(first, 0.0, zp_ref[15:16, :].astype(F32))
        nx = jnp.where(last, 0.0, zn_ref[0:1, :].astype(F32))
        zm1, zp1 = _shift_rows(z, pr, nx)
        cv = w_ref[0:1, :] * zm1 + w_ref[1:2, :] * z + w_ref[2:3, :] * zp1
        da = jnp.concatenate(
            [(dqf_ref[...].astype(F32) + dqb_ref[...].astype(F32)) * (DH ** -0.5),
             dkf_ref[...].astype(F32) + dkb_ref[...].astype(F32)], axis=1)
        dc = da * _dsilu(cv)
        dc_ref[...] = dc.astype(BF16)

        @pl.when(i == 0)
        def _():
            dw_ref[...] = jnp.zeros_like(dw_ref)

        dw_ref[0:1, :] += jnp.sum(zm1 * dc, axis=0, keepdims=True)
        dw_ref[1:2, :] += jnp.sum(z * dc, axis=0, keepdims=True)
        dw_ref[2:3, :] += jnp.sum(zp1 * dc, axis=0, keepdims=True)

    return pl.pallas_call(
        body, name="qkconv_bwd_a", grid=(nt,),
        in_specs=[_rb(TR, w2, CB_Q // 2), prev_s, next_s, _bc(8, w2), _rb(TR, D), _rb(TR, D), _rb(TR, D), _rb(TR, D)],
        out_specs=[_rb(TR, w2), _bc(8, w2)],
        out_shape=[jax.ShapeDtypeStruct((tp, w2), BF16), jax.ShapeDtypeStruct((8, w2), F32)],
        compiler_params=_cparams(("arbitrary",)),
    )(zmain, zmain, zmain, conv_w, dqf, dqb, dkf, dkb)


def qkconv_bwd_b(dz, dc, conv_w, n_ctx_tiles):
    tp = dc.shape[0]
    nt = tp // TR
    w2 = 2 * D
    prev_s, next_s = _halo_specs(TR, w2, 0, tp, 16)

    def body(dz_in_ref, d_ref, dp_ref, dn_ref, w_ref, o_ref):
        del dz_in_ref
        i = pl.program_id(0)
        first, last = _seq_edges(i, n_ctx_tiles, nt)
        d = d_ref[...].astype(F32)
        pr = jnp.where(first, 0.0, dp_ref[15:16, :].astype(F32))
        nx = jnp.where(last, 0.0, dn_ref[0:1, :].astype(F32))
        dm1, dp1 = _shift_rows(d, pr, nx)
        o_ref[...] = (w_ref[0:1, :] * dp1 + w_ref[1:2, :] * d + w_ref[2:3, :] * dm1).astype(BF16)

    return pl.pallas_call(
        body, name="qkconv_bwd_b", grid=(nt,),
        in_specs=[pl.BlockSpec(memory_space=pl.ANY), _rb(TR, w2), prev_s, next_s, _bc(8, w2)],
        out_specs=_rb(TR, w2, CB_Q // 2),
        out_shape=jax.ShapeDtypeStruct(dz.shape, BF16),
        input_output_aliases={0: 0},
        compiler_params=_cparams(("parallel",)),
    )(dz, dc, dc, dc, conv_w)


def add_into_dz(dz, a, b, col):
    tp = a.shape[0]

    def body(dz_in_ref, a_ref, b_ref, o_ref):
        del dz_in_ref
        o_ref[...] = (a_ref[...].astype(F32) + b_ref[...].astype(F32)).astype(BF16)

    return pl.pallas_call(
        body, name="add_into_dz", grid=(tp // TR,),
        in_specs=[pl.BlockSpec(memory_space=pl.ANY), _rb(TR, D), _rb(TR, D)],
        out_specs=_rb(TR, D, col),
        out_shape=jax.ShapeDtypeStruct(dz.shape, BF16),
        input_output_aliases={0: 0},
        compiler_params=_cparams(("parallel",)),
    )(dz, a, b)


def _chunk_maps(nc, ncc):
    def fwd(t):
        return t

    def bwd(t):
        return jnp.where(t < ncc, ncc - 1 - t, nc - 1 + ncc - t)

    return fwd, bwd


def _mlstm_chunk(d, h, gc, gr, q_ref, k_ref, v_ref, cp, npv, m_prev, mask):
    ic, bcol = 8 * d + h, 8 * d + 4 + h
    i_col, b_col = gc[:, ic:ic + 1], gc[:, bcol:bcol + 1]
    i_row, b_row = gr[ic:ic + 1, :], gr[bcol:bcol + 1, :]
    g = b_row[:, LC - 1:LC] if d == 0 else b_row[:, 0:1]
    a_row = g - b_row + i_row
    m_loc = jnp.max(a_row, axis=1, keepdims=True)
    dmat = jnp.where(mask, b_col - b_row + i_row, -jnp.inf)
    inter = b_col + m_prev
    m_row = jnp.maximum(inter, jnp.max(dmat, axis=1, keepdims=True))
    e = jnp.exp(dmat - m_row)
    w = jnp.exp(inter - m_row)
    hs = slice(h * DH, (h + 1) * DH)
    qh, kh, vh = q_ref[:, hs], k_ref[:, hs], v_ref[:, hs]
    p = _dot_nt(qh, kh)
    s = p * e
    cpb = cp.astype(BF16)
    qc = _dot(qh, cpb)
    num = _dot(s, vh) + w * qc
    qn = jnp.sum(qh.astype(F32) * npv, axis=1, keepdims=True)
    den = jnp.sum(s, axis=1, keepdims=True) + w * qn
    thr = jnp.exp(-m_row)
    m_new = jnp.maximum(g + m_prev, m_loc)
    a_old = jnp.exp(g + m_prev - m_new)
    a_col = g - b_col + i_col
    return dict(qh=qh, kh=kh, vh=vh, e=e, w=w, s=s, cpb=cpb, qc=qc, num=num, qn=qn, den=den, thr=thr,
                m_loc=m_loc, m_new=m_new, a_old=a_old, a_col=a_col, hs=hs)


def mlstm_fwd(qa, ka, zmain, gcol, grow, ncc):
    tp = qa.shape[0]
    nc = tp // LC
    cf, cb = _chunk_maps(nc, ncc)

    def body(qf, kf, vf, gcf, grf, qb, kb, vb, gcb, grb,
             hf_o, hb_o, cf_o, cb_o, nf_o, nb_o, mf_o, mb_o, c_sc, n_sc, m_sc):
        t = pl.program_id(0)

        @pl.when(t == 0)
        def _():
            c_sc[...] = jnp.zeros_like(c_sc)
            n_sc[...] = jnp.zeros_like(n_sc)
            m_sc[...] = jnp.full(m_sc.shape, M_INIT, F32)

        row = lax.broadcasted_iota(jnp.int32, (LC, LC), 0)
        col = lax.broadcasted_iota(jnp.int32, (LC, LC), 1)
        dirs = ((qf, kf, vf, gcf, grf, hf_o, cf_o, nf_o, mf_o), (qb, kb, vb, gcb, grb, hb_o, cb_o, nb_o, mb_o))
        for d, (q_ref, k_ref, v_ref, gc_ref, gr_ref, h_o, c_o, n_o, m_o) in enumerate(dirs):
            mask = (col <= row) if d == 0 else (col >= row)
            gc = gc_ref[...]
            gr = gr_ref[0]
            for h in range(NH):
                idx = d * NH + h
                cp = c_sc[idx]
                npv = n_sc[idx]
                m_full = m_sc[idx]
                m_prev = m_full[:, 0:1]
                r = _mlstm_chunk(d, h, gc, gr, q_ref, k_ref, v_ref, cp, npv, m_prev, mask)
                hs = r["hs"]
                h_o[:, hs] = (r["num"] / jnp.maximum(jnp.abs(r["den"]), r["thr"])).astype(BF16)
                c_o[0, hs, :] = r["cpb"]
                n_o[0, h:h + 1, :] = npv
                m_o[0, h:h + 1, :] = m_full
                a_new = jnp.exp(r["m_loc"] - r["m_new"])
                kw = r["kh"].astype(F32) * jnp.exp(r["a_col"] - r["m_loc"])
                kv = _dot_tn_mxu(kw, r["vh"])
                kn = jnp.sum(kw, axis=0, keepdims=True)
                c_sc[idx] = r["a_old"] * cp + a_new * kv
                n_sc[idx] = r["a_old"] * npv + a_new * kn
                m_sc[idx] = jnp.broadcast_to(r["m_new"], (1, LANE))

    def dspecs(cm):
        return [pl.BlockSpec((LC, D), lambda t: (cm(t), 0)),
                pl.BlockSpec((LC, D), lambda t: (cm(t), 0)),
                pl.BlockSpec((LC, D), lambda t: (cm(t), CB_V)),
                pl.BlockSpec((LC, LANE), lambda t: (cm(t), 0)),
                pl.BlockSpec((1, 16, LC), lambda t: (cm(t), 0, 0))]

    def ospec(cm, shp):
        return pl.BlockSpec((1,) + shp, lambda t: (cm(t), 0, 0))

    return pl.pallas_call(
        body, name="mlstm_fwd", grid=(nc,),
        in_specs=dspecs(cf) + dspecs(cb),
        out_specs=[pl.BlockSpec((LC, D), lambda t: (cf(t), 0)), pl.BlockSpec((LC, D), lambda t: (cb(t), 0)),
                   ospec(cf, (D, DH)), ospec(cb, (D, DH)), ospec(cf, (NH, DH)), ospec(cb, (NH, DH)),
                   ospec(cf, (NH, LANE)), ospec(cb, (NH, LANE))],
        out_shape=[jax.ShapeDtypeStruct((tp, D), BF16), jax.ShapeDtypeStruct((tp, D), BF16),
                   jax.ShapeDtypeStruct((nc, D, DH), BF16), jax.ShapeDtypeStruct((nc, D, DH), BF16),
                   jax.ShapeDtypeStruct((nc, NH, DH), F32), jax.ShapeDtypeStruct((nc, NH, DH), F32),
                   jax.ShapeDtypeStruct((nc, NH, LANE), F32), jax.ShapeDtypeStruct((nc, NH, LANE), F32)],
        scratch_shapes=[pltpu.VMEM((2 * NH, DH, DH), F32), pltpu.VMEM((2 * NH, 1, DH), F32),
                        pltpu.VMEM((2 * NH, 1, LANE), F32)],
        compiler_params=_cparams(("arbitrary",)),
    )(qa, ka, zmain, gcol, grow, qa, ka, zmain, gcol, grow)


def mlstm_bwd(qa, ka, zmain, gcol, grow, states, dhm, ncc):
    tp = qa.shape[0]
    nc = tp // LC
    cf0, cb0 = _chunk_maps(nc, ncc)
    cf = lambda t: cf0(nc - 1 - t)
    cb = lambda t: cb0(nc - 1 - t)
    csf, csb, nsf, nsb, msf, msb = states

    def body(qf, kf, vf, gcf, grf, cpf, npf, mpf, dhf, qb, kb, vb, gcb, grb, cpb_, npb, mpb, dhb,
             dqf_o, dkf_o, dvf_o, colf_o, rowf_o, dqb_o, dkb_o, dvb_o, colb_o, rowb_o, dc_sc, dn_sc):
        t = pl.program_id(0)

        @pl.when(t == 0)
        def _():
            dc_sc[...] = jnp.zeros_like(dc_sc)
            dn_sc[...] = jnp.zeros_like(dn_sc)

        row = lax.broadcasted_iota(jnp.int32, (LC, LC), 0)
        col = lax.broadcasted_iota(jnp.int32, (LC, LC), 1)
        dirs = ((qf, kf, vf, gcf, grf, cpf, npf, mpf, dhf, dqf_o, dkf_o, dvf_o, colf_o, rowf_o, cf),
                (qb, kb, vb, gcb, grb, cpb_, npb, mpb, dhb, dqb_o, dkb_o, dvb_o, colb_o, rowb_o, cb))
        for d, (q_ref, k_ref, v_ref, gc_ref, gr_ref, cp_ref, np_ref, mp_ref, dh_ref,
                dq_o, dk_o, dv_o, col_o, row_o, cm) in enumerate(dirs):
            mask = (col <= row) if d == 0 else (col >= row)
            live = jnp.where(cm(t) >= ncc, 1.0, 0.0).astype(F32)
            gc = gc_ref[...]
            gr = gr_ref[0]
            col_o[...] = jnp.zeros_like(col_o)
            row_o[...] = jnp.zeros_like(row_o)
            for h in range(NH):
                idx = d * NH + h
                hs = slice(h * DH, (h + 1) * DH)
                cp = cp_ref[0, hs, :]
                npv = np_ref[0, h:h + 1, :]
                m_prev = mp_ref[0, h:h + 1, 0:1]
                r = _mlstm_chunk(d, h, gc, gr, q_ref, k_ref, v_ref, cp, npv, m_prev, mask)
                qh, kh, vh, e, w, s = r["qh"], r["kh"], r["vh"], r["e"], r["w"], r["s"]
                qf32, kf32 = qh.astype(F32), kh.astype(F32)
                den, thr = r["den"], r["thr"]
                rden = 1.0 / jnp.maximum(jnp.abs(den), thr)
                hh = r["num"] * rden
                dh = dh_ref[:, hs].astype(F32) * live
                dnum = dh * rden
                sgn = jnp.where(jnp.abs(den) > thr, jnp.sign(den), 0.0)
                dden = -jnp.sum(dh * hh, axis=1, keepdims=True) * rden * sgn
                ds = _dot_nt(dnum, vh) + dden
                dp = ds * e
                gm = ds * s
                rowsum = jnp.sum(gm, axis=1, keepdims=True)
                colsum = jnp.sum(gm, axis=0, keepdims=True)
                dq = _dot(dp, kh) + w * (_dot_nt(dnum, r["cpb"]) + dden * npv)
                dcs = dc_sc[idx]
                dns = dn_sc[idx]
                kfac = jnp.exp(r["a_col"] - r["m_new"])
                vdc = _dot_nt(vh, dcs)
                dk = _dot_tn(dp, qh) + kfac * (vdc + dns)
                dv = _dot_tn(s, dnum) + kfac * _dot(kh, dcs)
                beta = w * (jnp.sum(dnum * r["qc"], axis=1, keepdims=True) + dden * r["qn"])
                alpha = kfac * (jnp.sum(kf32 * vdc, axis=1, keepdims=True) + jnp.sum(kf32 * dns, axis=1, keepdims=True))
                dq_o[:, hs] = dq.astype(BF16)
                dk_o[:, hs] = dk.astype(BF16)
                dv_o[:, hs] = dv.astype(BF16)
                cpf = r["cpb"].astype(F32)
                inner = (jnp.sum(jnp.sum(dcs * cpf, axis=1, keepdims=True), axis=0, keepdims=True)
                         + jnp.sum(dns * npv, axis=1, keepdims=True))
                gam = jnp.sum(alpha, axis=0, keepdims=True) + r["a_old"] * inner
                lo = 8 * d + h
                col_o[:, lo:lo + 1] = alpha
                col_o[:, lo + 4:lo + 5] = rowsum + beta - alpha
                col_o[:, lo + 36:lo + 37] = jnp.broadcast_to(gam, (LC, 1))
                row_o[0, h:h + 1, :] = colsum
                wq = qf32 * w
                dc_sc[idx] = r["a_old"] * dcs + _dot_tn(wq, dnum)
                dn_sc[idx] = r["a_old"] * dns + jnp.sum(wq * dden, axis=0, keepdims=True)

    def dspecs(cm):
        return [pl.BlockSpec((LC, D), lambda t: (cm(t), 0)),
                pl.BlockSpec((LC, D), lambda t: (cm(t), 0)),
                pl.BlockSpec((LC, D), lambda t: (cm(t), CB_V)),
                pl.BlockSpec((LC, LANE), lambda t: (cm(t), 0)),
                pl.BlockSpec((1, 16, LC), lambda t: (cm(t), 0, 0)),
                pl.BlockSpec((1, D, DH), lambda t: (cm(t), 0, 0)),
                pl.BlockSpec((1, NH, DH), lambda t: (cm(t), 0, 0)),
                pl.BlockSpec((1, NH, LANE), lambda t: (cm(t), 0, 0)),
                pl.BlockSpec((LC, D), lambda t: (jnp.maximum(cm(t) - ncc, 0), 0))]

    def ospecs(cm):
        return [pl.BlockSpec((LC, D), lambda t: (cm(t), 0)),
                pl.BlockSpec((LC, D), lambda t: (cm(t), 0)),
                pl.BlockSpec((LC, D), lambda t: (cm(t), 0)),
                pl.BlockSpec((LC, LANE), lambda t: (cm(t), 0)),
                pl.BlockSpec((1, 8, LC), lambda t: (cm(t), 0, 0))]

    oshape = [jax.ShapeDtypeStruct((tp, D), BF16)] * 3 + [jax.ShapeDtypeStruct((tp, LANE), F32),
                                                        jax.ShapeDtypeStruct((nc, 8, LC), F32)]
    return pl.pallas_call(
        body, name="mlstm_bwd", grid=(nc,),
        in_specs=dspecs(cf) + dspecs(cb),
        out_specs=ospecs(cf) + ospecs(cb),
        out_shape=oshape + oshape,
        scratch_shapes=[pltpu.VMEM((2 * NH, DH, DH), F32), pltpu.VMEM((2 * NH, 1, DH), F32)],
        compiler_params=_cparams(("arbitrary",)),
    )(qa, ka, zmain, gcol, grow, csf, nsf, msf, dhm, qa, ka, zmain, gcol, grow, csb, nsb, msb, dhm)


def gates_bwd(colf, colb, cs, zgb):
    tp = colf.shape[0]

    def body(cf_ref, cb_ref, cs_ref, zgb_ref, o_ref, db_ref):
        i = pl.program_id(0)

        @pl.when(i == 0)
        def _():
            db_ref[...] = jnp.zeros_like(db_ref)

        lane = lax.broadcasted_iota(jnp.int32, (TR, LANE), 1)
        i_l = jnp.logical_or(lane < 4, jnp.logical_and(lane >= 8, lane < 12))
        f_l = jnp.logical_or(jnp.logical_and(lane >= 4, lane < 8), jnp.logical_and(lane >= 12, lane < 16))
        cv = cf_ref[...] + cb_ref[...]
        csv = cs_ref[...]
        gam = pltpu.roll(cv, LANE - 32, 1)
        dbh = jnp.where(f_l, cv - csv, 0.0)
        rr = lax.broadcasted_iota(jnp.int32, (TR, TR), 0)
        cc = lax.broadcasted_iota(jnp.int32, (TR, TR), 1)
        same = (rr // LC) == (cc // LC)
        low = jnp.where(jnp.logical_and(same, cc <= rr), 1.0, 0.0).astype(BF16)
        upp = jnp.where(jnp.logical_and(same, cc >= rr), 1.0, 0.0).astype(BF16)
        dlogf = jnp.where(lane < 8, _exact_dot(upp, dbh), _exact_dot(low, dbh)) + gam
        out = jnp.where(i_l, csv + cv, 0.0) + jnp.where(f_l, dlogf * _sigmoid(-zgb_ref[...]), 0.0)
        o_ref[...] = out
        db_ref[...] += jnp.sum(out, axis=0, keepdims=True)

    spec = _rb(TR, LANE)
    return pl.pallas_call(
        body, name="gates_bwd", grid=(tp // TR,),
        in_specs=[spec] * 4,
        out_specs=[spec, _bc(1, LANE)],
        out_shape=[jax.ShapeDtypeStruct((tp, LANE), F32), jax.ShapeDtypeStruct((1, LANE), F32)],
        compiler_params=_cparams(("arbitrary",)),
    )(colf, colb, cs, zgb)


def _head_norm(hm, gh):
    xs, rs = [], []
    for h in range(NH):
        seg = hm[:, h * DH:(h + 1) * DH]
        r = lax.rsqrt(jnp.mean(seg * seg, axis=-1, keepdims=True) + EPS)
        xs.append(seg * r)
        rs.append(r)
    xh = jnp.concatenate(xs, axis=1)
    return xh, rs, xh * gh


def _sgu_norm(zvg, ln_g, ln_b):
    vg = _gelu(zvg)
    mu = jnp.mean(vg, axis=-1, keepdims=True)
    vc = vg - mu
    rstd = lax.rsqrt(jnp.mean(vc * vc, axis=-1, keepdims=True) + EPS)
    vhat = vc * rstd
    return vhat, rstd, vhat * ln_g + ln_b


def _sgu_mix(vn, ws_ref, bs_ref):
    rows = []
    for c in range(TR // SCH):
        cols = []
        for g in range(SG):
            blk = vn[c * SCH:(c + 1) * SCH, g * SGD:(g + 1) * SGD]
            cols.append(_dot(ws_ref[g * SCH:(g + 1) * SCH, :], blk) + bs_ref[:, g:g + 1])
        rows.append(jnp.concatenate(cols, axis=1))
    return jnp.concatenate(rows, axis=0)


def mixer_fwd(hf, hb, zmain, gh, ln_g, ln_b, ws, bs_t, n_ctx_tiles):
    t = hf.shape[0] - n_ctx_tiles * TR
    off = n_ctx_tiles

    def body(hf_ref, hb_ref, zo_ref, zu_ref, zv_ref, gh_ref, lg_ref, lb_ref, ws_ref, bs_ref, o_ref):
        hm = hf_ref[...].astype(F32) + hb_ref[...].astype(F32)
        _, _, hn = _head_norm(hm, gh_ref[...])
        o_ref[0] = (_sigmoid(zo_ref[...].astype(F32)) * hn).astype(BF16)
        _, _, vn = _sgu_norm(zv_ref[...].astype(F32), lg_ref[...], lb_ref[...])
        mixed = _sgu_mix(vn, ws_ref, bs_ref)
        o_ref[1] = (_gelu(zu_ref[...].astype(F32)) * mixed).astype(BF16)

    return pl.pallas_call(
        body, name="mixer_fwd", grid=(t // TR,),
        in_specs=[_rb(TR, D, 0, off), _rb(TR, D, 0, off), _rb(TR, D, CB_O, off), _rb(TR, D, CB_U, off),
                  _rb(TR, D, CB_VG, off), _bc(1, D), _bc(1, D), _bc(1, D), _bc(SG * SCH, SCH), _bc(SCH, LANE)],
        out_specs=pl.BlockSpec((2, TR, D), lambda i: (0, i, 0)),
        out_shape=jax.ShapeDtypeStruct((2, t, D), BF16),
        compiler_params=_cparams(("parallel",)),
    )(hf, hb, zmain, zmain, zmain, gh, ln_g, ln_b, ws, bs_t)


def merge_fwd(zmain, pp, n_ctx_tiles):
    t = pp.shape[1]
    off = n_ctx_tiles

    def body(zgm_ref, zgg_ref, pp_ref, o_ref):
        y = (_sigmoid(zgm_ref[...].astype(F32)) * pp_ref[0].astype(F32)
             + _sigmoid(zgg_ref[...].astype(F32)) * pp_ref[1].astype(F32))
        o_ref[...] = y.astype(BF16)

    return pl.pallas_call(
        body, name="merge_fwd", grid=(t // TR,),
        in_specs=[_rb(TR, D, CB_GM, off), _rb(TR, D, CB_GG, off), pl.BlockSpec((2, TR, D), lambda i: (0, i, 0))],
        out_specs=_rb(TR, D),
        out_shape=jax.ShapeDtypeStruct((t, D), BF16),
        compiler_params=_cparams(("parallel",)),
    )(zmain, zmain, pp)


def merge_bwd(zmain, pp, dy, tp, n_ctx_tiles):
    t = dy.shape[0]
    nt = tp // TR
    xrow = lambda i: jnp.maximum(i - n_ctx_tiles, 0)

    def body(zg_ref, pp_ref, dy_ref, dpp_ref, dz_ref):
        i = pl.program_id(1)
        zg = zg_ref[...].astype(F32)
        sg = _sigmoid(zg)
        dyv = dy_ref[...].astype(F32)
        dpp_ref[0] = (dyv * sg).astype(BF16)
        dzv = dyv * pp_ref[0].astype(F32) * sg * (1.0 - sg)
        dz_ref[...] = jnp.where(i >= n_ctx_tiles, dzv, 0.0).astype(BF16)

    return pl.pallas_call(
        body, name="merge_bwd", grid=(2, nt),
        in_specs=[pl.BlockSpec((TR, D), lambda j, i: (i, CB_GM + j)),
                  pl.BlockSpec((1, TR, D), lambda j, i: (j, xrow(i), 0)),
                  pl.BlockSpec((TR, D), lambda j, i: (xrow(i), 0))],
        out_specs=[pl.BlockSpec((1, TR, D), lambda j, i: (j, xrow(i), 0)),
                   pl.BlockSpec((TR, D), lambda j, i: (i, CB_GM + j))],
        out_shape=[jax.ShapeDtypeStruct((2, t, D), BF16), jax.ShapeDtypeStruct((tp, 8 * D), BF16)],
        compiler_params=_cparams(("arbitrary", "arbitrary")),
    )(zmain, pp, dy)


def mixer_bwd(dz, hf, hb, zmain, dyms, gh, ln_g, ln_b, ws, bs_t, n_ctx_tiles):
    tp = hf.shape[0]
    t = tp - n_ctx_tiles * TR
    nt = tp // TR
    xrow = lambda i: jnp.maximum(i - n_ctx_tiles, 0)

    def body(dz_in_ref, hf_ref, hb_ref, zo_ref, zu_ref, zv_ref, dy_ref, gh_ref, lg_ref, lb_ref, ws_ref, bs_ref,
             dz_ref, dhm_ref, dgh_ref, dlg_ref, dlb_ref, dws_ref, dbs_ref):
        del dz_in_ref
        i = pl.program_id(0)

        @pl.when(i == 0)
        def _():
            for ref in (dgh_ref, dlg_ref, dlb_ref, dws_ref, dbs_ref):
                ref[...] = jnp.zeros_like(ref)

        @pl.when(i < n_ctx_tiles)
        def _():
            dz_ref[...] = jnp.zeros_like(dz_ref)

        @pl.when(i >= n_ctx_tiles)
        def _():
            gh_v = gh_ref[...]
            hm = hf_ref[...].astype(F32) + hb_ref[...].astype(F32)
            xh, rs, hn = _head_norm(hm, gh_v)
            zo = zo_ref[...].astype(F32)
            so = _sigmoid(zo)
            dym = dy_ref[0].astype(F32)
            d_zo = dym * hn * so * (1.0 - so)
            d_hn = dym * so
            dgh_ref[...] += jnp.sum(d_hn * xh, axis=0, keepdims=True)
            d_xh = d_hn * gh_v
            segs = []
            for h in range(NH):
                hs = slice(h * DH, (h + 1) * DH)
                dx, xs = d_xh[:, hs], xh[:, hs]
                segs.append(rs[h] * (dx - xs * jnp.mean(dx * xs, axis=-1, keepdims=True)))
            dhm_ref[...] = jnp.concatenate(segs, axis=1).astype(BF16)
            zu = zu_ref[...].astype(F32)
            zv = zv_ref[...].astype(F32)
            lg = lg_ref[...]
            vhat, rstd, vn = _sgu_norm(zv, lg, lb_ref[...])
            mixed = _sgu_mix(vn, ws_ref, bs_ref)
            dys = dy_ref[1].astype(F32)
            d_zu = dys * mixed * _dgelu(zu)
            d_mixed = dys * _gelu(zu)
            rows = []
            for c in range(TR // SCH):
                cols = []
                for g in range(SG):
                    rsl, csl = slice(c * SCH, (c + 1) * SCH), slice(g * SGD, (g + 1) * SGD)
                    dm = d_mixed[rsl, csl]
                    cols.append(_dot_tn(ws_ref[g * SCH:(g + 1) * SCH, :], dm))
                    dws_ref[g * SCH:(g + 1) * SCH, :] += _dot_nt(dm, vn[rsl, csl])
                    dbs_ref[:, g:g + 1] += jnp.sum(dm, axis=1, keepdims=True)
                rows.append(jnp.concatenate(cols, axis=1))
            d_vn = jnp.concatenate(rows, axis=0)
            dlg_ref[...] += jnp.sum(d_vn * vhat, axis=0, keepdims=True)
            dlb_ref[...] += jnp.sum(d_vn, axis=0, keepdims=True)
            d_vhat = d_vn * lg
            d_vg = rstd * (d_vhat - jnp.mean(d_vhat, axis=-1, keepdims=True)
                           - vhat * jnp.mean(d_vhat * vhat, axis=-1, keepdims=True))
            d_zv = d_vg * _dgelu(zv)
            dz_ref[...] = jnp.concatenate([d_zo, d_zu, d_zv], axis=1).astype(BF16)

    return pl.pallas_call(
        body, name="mixer_bwd", grid=(nt,),
        in_specs=[pl.BlockSpec(memory_space=pl.ANY), _rb(TR, D), _rb(TR, D), _rb(TR, D, CB_O), _rb(TR, D, CB_U),
                  _rb(TR, D, CB_VG), pl.BlockSpec((2, TR, D), lambda i: (0, xrow(i), 0)),
                  _bc(1, D), _bc(1, D), _bc(1, D), _bc(SG * SCH, SCH), _bc(SCH, LANE)],
        out_specs=[_rb(TR, 3 * D), pl.BlockSpec((TR, D), lambda i: (xrow(i), 0)),
                   _bc(1, D), _bc(1, D), _bc(1, D), _bc(SG * SCH, SCH), _bc(SCH, LANE)],
        out_shape=[jax.ShapeDtypeStruct(dz.shape, BF16), jax.ShapeDtypeStruct((t, D), BF16),
                   jax.ShapeDtypeStruct((1, D), F32), jax.ShapeDtypeStruct((1, D), F32),
                   jax.ShapeDtypeStruct((1, D), F32), jax.ShapeDtypeStruct((SG * SCH, SCH), F32),
                   jax.ShapeDtypeStruct((SCH, LANE), F32)],
        input_output_aliases={0: 0},
        compiler_params=_cparams(("arbitrary",)),
    )(dz, hf, hb, zmain, zmain, zmain, dyms, gh, ln_g, ln_b, ws, bs_t)


FCB = DFF // 2
TF = 512


def _ffn_halo(col, t):
    per = TF // GW
    last = t // GW - 1
    prev = pl.BlockSpec((GW, FCB), lambda i, j: (jnp.maximum(i * per - 1, 0), col(j)))
    nxt = pl.BlockSpec((GW, FCB), lambda i, j: (jnp.minimum((i + 1) * per, last), col(j)))
    return prev, nxt


def _conv_taps(ext):
    n = ext.shape[0]
    colid = lax.broadcasted_iota(jnp.int32, (n, 1), 0) % GW
    left = pltpu.roll(jnp.where(colid != GW - 1, ext, 0.0), 1, 0)
    right = pltpu.roll(jnp.where(colid != 0, ext, 0.0), n - 1, 0)
    views = (left, ext, right)
    return {(ky, kx): views[kx][GW * ky:GW * ky + TF] for ky in range(3) for kx in range(3)}


def _ext(c_ref, p_ref, n_ref, i, nt):
    pr = jnp.where(i == 0, 0.0, p_ref[...].astype(F32))
    nx = jnp.where(i == nt - 1, 0.0, n_ref[...].astype(F32))
    return jnp.concatenate([pr, c_ref[...].astype(F32), nx], axis=0)


def ffn_act_fwd(up, wc):
    t = up.shape[0]
    nt = t // TF
    prev_s, next_s = _ffn_halo(lambda j: j, t)

    def body(a_ref, ap_ref, an_ref, b_ref, w_ref, o_ref, ac_ref):
        i = pl.program_id(0)
        taps = _conv_taps(_ext(a_ref, ap_ref, an_ref, i, nt))
        ac = sum(w_ref[3 * ky + kx:3 * ky + kx + 1, :] * taps[(ky, kx)] for ky in range(3) for kx in range(3))
        ac_ref[...] = ac.astype(BF16)
        o_ref[...] = (_silu(ac) * b_ref[...].astype(F32)).astype(BF16)

    spec = pl.BlockSpec((TF, FCB), lambda i, j: (i, j))
    return pl.pallas_call(
        body, name="ffn_act_fwd", grid=(nt, 2),
        in_specs=[spec, prev_s, next_s,
                  pl.BlockSpec((TF, FCB), lambda i, j: (i, 2 + j)), pl.BlockSpec((16, FCB), lambda i, j: (0, j))],
        out_specs=[spec, spec],
        out_shape=[jax.ShapeDtypeStruct((t, DFF), BF16), jax.ShapeDtypeStruct((t, DFF), BF16)],
        compiler_params=_cparams(("parallel", "parallel")),
    )(up, up, up, up, wc)


def ffn_act_bwd(up, ac, dact):
    t = up.shape[0]
    nt = t // TF

    def body(b_ref, ac_ref, da_ref, dup_ref, dac_ref):
        acv = ac_ref[...].astype(F32)
        da = da_ref[...].astype(F32)
        s = _sigmoid(acv)
        dup_ref[...] = (da * acv * s).astype(BF16)
        dac_ref[...] = (da * b_ref[...].astype(F32) * s * (1.0 + acv * (1.0 - s))).astype(BF16)

    spec = pl.BlockSpec((TF, FCB), lambda i, j: (i, j))
    bspec = pl.BlockSpec((TF, FCB), lambda i, j: (i, 2 + j))
    return pl.pallas_call(
        body, name="ffn_act_bwd", grid=(nt, 2),
        in_specs=[bspec, spec, spec],
        out_specs=[bspec, spec],
        out_shape=[jax.ShapeDtypeStruct((t, 2 * DFF), BF16), jax.ShapeDtypeStruct((t, DFF), BF16)],
        compiler_params=_cparams(("parallel", "parallel")),
    )(up, ac, dact)


def ffn_conv_bwd(dup, up, dac, wc):
    t = up.shape[0]
    nt = t // TF
    prev_g, next_g = _ffn_halo(lambda j: j, t)

    def body(dup_in_ref, a_ref, g_ref, gp_ref, gn_ref, w_ref, o_ref, dw_ref):
        del dup_in_ref
        i = pl.program_id(1)

        @pl.when(i == 0)
        def _():
            dw_ref[...] = jnp.zeros_like(dw_ref)

        gtaps = _conv_taps(_ext(g_ref, gp_ref, gn_ref, i, nt))
        a = a_ref[...].astype(F32)
        acc = None
        for ky in range(3):
            for kx in range(3):
                k = 3 * ky + kx
                kf = 3 * (2 - ky) + (2 - kx)
                tap = gtaps[(ky, kx)]
                term = w_ref[kf:kf + 1, :] * tap
                acc = term if acc is None else acc + term
                dw_ref[kf:kf + 1, :] += jnp.sum(a * tap, axis=0, keepdims=True)
        o_ref[...] = acc.astype(BF16)

    sw = lambda s: pl.BlockSpec(s.block_shape, lambda j, i, f=s.index_map: f(i, j))
    spec = pl.BlockSpec((TF, FCB), lambda j, i: (i, j))
    return pl.pallas_call(
        body, name="ffn_conv_bwd", grid=(2, nt),
        in_specs=[pl.BlockSpec(memory_space=pl.ANY), spec, spec, sw(prev_g), sw(next_g),
                  pl.BlockSpec((16, FCB), lambda j, i: (0, j))],
        out_specs=[spec, pl.BlockSpec((16, FCB), lambda j, i: (0, j))],
        out_shape=[jax.ShapeDtypeStruct(dup.shape, BF16), jax.ShapeDtypeStruct((16, DFF), F32)],
        input_output_aliases={0: 0},
        compiler_params=_cparams(("arbitrary", "arbitrary")),
    )(dup, up, dac, dac, dac, wc)


def head_fwd_bwd(h1, f, target, gfin, tab):
    t = h1.shape[0]

    def body(h1_ref, f_ref, t_ref, g_ref, tab_ref, dh2_ref, df_ref, acc_ref):
        i = pl.program_id(0)

        @pl.when(i == 0)
        def _():
            acc_ref[...] = jnp.zeros_like(acc_ref)

        gate = tab_ref[0:1, :]
        fv = f_ref[...].astype(F32)
        h2 = h1_ref[...] + gate * fv
        r = lax.rsqrt(jnp.mean(h2 * h2, axis=-1, keepdims=True) + EPS)
        xh = h2 * r
        gv = g_ref[...]
        err = xh * gv - t_ref[...]
        acc_ref[0:1, :] += jnp.sum(0.5 * jnp.mean(err * err, axis=-1, keepdims=True), axis=0, keepdims=True)
        dy = err * (1.0 / D)
        acc_ref[1:2, :] += jnp.sum(dy * xh, axis=0, keepdims=True)
        dxh = dy * gv
        dh2 = r * (dxh - xh * jnp.mean(dxh * xh, axis=-1, keepdims=True))
        dh2_ref[...] = dh2
        acc_ref[2:3, :] += jnp.sum(dh2 * fv, axis=0, keepdims=True)
        df_ref[...] = (dh2 * gate).astype(BF16)

    return pl.pallas_call(
        body, name="head_fwd_bwd", grid=(t // TR,),
        in_specs=[_rb(TR, D), _rb(TR, D), _rb(TR, D), _bc(1, D), _bc(8, D)],
        out_specs=[_rb(TR, D), _rb(TR, D), _bc(8, D)],
        out_shape=[jax.ShapeDtypeStruct((t, D), F32), jax.ShapeDtypeStruct((t, D), BF16),
                   jax.ShapeDtypeStruct((8, D), F32)],
        compiler_params=_cparams(("arbitrary",)),
    )(h1, f, target, gfin, tab)


def norm2_bwd(h1, dhn2, dh2, out, g, tab):
    t = h1.shape[0]

    def body(h1_ref, dhn_ref, dh2_ref, out_ref, g_ref, tab_ref, dh1_ref, dout_ref, acc_ref):
        i = pl.program_id(0)

        @pl.when(i == 0)
        def _():
            acc_ref[...] = jnp.zeros_like(acc_ref)

        h1v = h1_ref[...]
        r = lax.rsqrt(jnp.mean(h1v * h1v, axis=-1, keepdims=True) + EPS)
        xh = h1v * r
        gv = g_ref[...]
        dhn = dhn_ref[...].astype(F32)
        acc_ref[0:1, :] += jnp.sum(dhn, axis=0, keepdims=True)
        acc_ref[1:2, :] += jnp.sum(dhn * xh * gv, axis=0, keepdims=True)
        dn = dhn * (1.0 + tab_ref[2:3, :])
        acc_ref[2:3, :] += jnp.sum(dn * xh, axis=0, keepdims=True)
        dxh = dn * gv
        dh1 = dh2_ref[...] + r * (dxh - xh * jnp.mean(dxh * xh, axis=-1, keepdims=True))
        dh1_ref[...] = dh1
        acc_ref[3:4, :] += jnp.sum(dh1 * out_ref[...].astype(F32), axis=0, keepdims=True)
        dout_ref[...] = (dh1 * tab_ref[0:1, :]).astype(BF16)

    return pl.pallas_call(
        body, name="norm2_bwd", grid=(t // TR,),
        in_specs=[_rb(TR, D), _rb(TR, D), _rb(TR, D), _rb(TR, D), _bc(1, D), _bc(8, D)],
        out_specs=[_rb(TR, D), _rb(TR, D), _bc(8, D)],
        out_shape=[jax.ShapeDtypeStruct((t, D), F32), jax.ShapeDtypeStruct((t, D), BF16),
                   jax.ShapeDtypeStruct((8, D), F32)],
        compiler_params=_cparams(("arbitrary",)),
    )(h1, dhn2, dh2, out, g, tab)


def norm1_bwd(ctx, x, da, db, dh1, g, tab, n_ctx_tiles):
    t = x.shape[0]
    tp = t + ctx.shape[0]
    xrow = lambda i: jnp.maximum(i - n_ctx_tiles, 0)

    def body(c_ref, x_ref, da_ref, db_ref, dh1_ref, g_ref, tab_ref, dx_ref, acc_ref):
        i = pl.program_id(0)

        @pl.when(i == 0)
        def _():
            acc_ref[...] = jnp.zeros_like(acc_ref)

        is_ctx = i < n_ctx_tiles
        x = jnp.where(is_ctx, c_ref[...], x_ref[...])
        r = lax.rsqrt(jnp.mean(x * x, axis=-1, keepdims=True) + EPS)
        xh = x * r
        gv = g_ref[...]
        dhn = da_ref[...].astype(F32) + db_ref[...]
        s_shift = jnp.sum(dhn, axis=0, keepdims=True)
        s_scale = jnp.sum(dhn * xh * gv, axis=0, keepdims=True)

        @pl.when(is_ctx)
        def _():
            acc_ref[0:1, :] += s_shift
            acc_ref[1:2, :] += s_scale

        @pl.when(jnp.logical_not(is_ctx))
        def _():
            acc_ref[2:3, :] += s_shift
            acc_ref[3:4, :] += s_scale

        acc_ref[5:6, :] += s_shift
        acc_ref[6:7, :] += s_scale
        sc = jnp.where(is_ctx, tab_ref[1:2, :], tab_ref[3:4, :])
        dn = dhn * (1.0 + sc)
        acc_ref[4:5, :] += jnp.sum(dn * xh, axis=0, keepdims=True)
        dxh = dn * gv
        dx_ref[...] = dh1_ref[...] + r * (dxh - xh * jnp.mean(dxh * xh, axis=-1, keepdims=True))

    return pl.pallas_call(
        body, name="norm1_bwd", grid=(tp // TR,),
        in_specs=_ctx_x_specs(n_ctx_tiles) + [_rb(TR, D), _rb(TR, D), pl.BlockSpec((TR, D), lambda i: (xrow(i), 0)),
                                              _bc(1, D), _bc(8, D)],
        out_specs=[pl.BlockSpec((TR, D), lambda i: (xrow(i), 0)), _bc(8, D)],
        out_shape=[jax.ShapeDtypeStruct((t, D), F32), jax.ShapeDtypeStruct((8, D), F32)],
        compiler_params=_cparams(("arbitrary",)),
    )(ctx, x, da, db, dh1, g, tab)


def adamw(w, g, m, v, name):
    lead = w.ndim - 2
    rows, cols = w.shape[-2:]
    tm = rows if rows * cols <= 256 * 1024 else _pick(rows, (256, 176, 128, 64, 8))
    c1 = 1.0 / (1.0 - ADAM_B1 ** ADAM_STEP)
    c2 = 1.0 / (1.0 - ADAM_B2 ** ADAM_STEP)

    def body(w_ref, g_ref, m_ref, v_ref, d_ref, mo_ref, vo_ref):
        gv = g_ref[...]
        mn = ADAM_B1 * m_ref[...] + (1.0 - ADAM_B1) * gv
        vn = ADAM_B2 * v_ref[...] + (1.0 - ADAM_B2) * (gv * gv)
        mo_ref[...] = mn
        vo_ref[...] = vn
        d_ref[...] = -ADAM_LR * ((mn * c1) / (jnp.sqrt(vn * c2) + ADAM_EPS) + ADAM_WD * w_ref[...])

    spec = pl.BlockSpec((1,) * lead + (tm, cols), lambda i: (0,) * lead + (i, 0))
    sds = jax.ShapeDtypeStruct(w.shape, F32)
    return pl.pallas_call(
        body, name=name, grid=(rows // tm,),
        in_specs=[spec] * 4, out_specs=[spec] * 3, out_shape=[sds] * 3,
        compiler_params=_cparams(("parallel",)),
    )(w, g, m, v)


def add_n(arrs, out_dtype, name):
    shp = arrs[0].shape
    cols = shp[-1]
    flat = [a.reshape(-1, cols) for a in arrs]
    rows = flat[0].shape[0]
    tm = max(t for t in range(16, rows + 1, 16) if rows % t == 0 and t * cols <= 512 * 1024)

    def body(*refs):
        acc = refs[0][...].astype(F32)
        for r in refs[1:-1]:
            acc = acc + r[...].astype(F32)
        refs[-1][...] = acc.astype(refs[-1].dtype)

    spec = pl.BlockSpec((tm, cols), lambda i: (i, 0))
    out = pl.pallas_call(
        body, name=name, grid=(rows // tm,),
        in_specs=[spec] * len(flat), out_specs=spec, out_shape=jax.ShapeDtypeStruct((rows, cols), out_dtype),
        compiler_params=_cparams(("parallel",)),
    )(*flat)
    return out.reshape(shp)


def sum8(stack, name):
    _, rows, cols = stack.shape
    tm = rows if rows <= 2048 else _pick(rows, (512, 256, 128, 64, 8))

    def body(s_ref, o_ref):
        acc = s_ref[0]
        for k in range(1, 8):
            acc = acc + s_ref[k]
        o_ref[...] = acc

    return pl.pallas_call(
        body, name=name, grid=(rows // tm,),
        in_specs=[pl.BlockSpec((8, tm, cols), lambda i: (0, i, 0))],
        out_specs=pl.BlockSpec((tm, cols), lambda i: (i, 0)),
        out_shape=jax.ShapeDtypeStruct((rows, cols), F32),
        compiler_params=_cparams(("parallel",)),
    )(stack)


def _coords():
    return lax.axis_index("x"), lax.axis_index("y"), lax.axis_index("c")


def _other_chips(x, y):
    return [(1 - x, y), (x, 1 - y), (1 - x, 1 - y)]


_ANY = pl.BlockSpec(memory_space=pl.ANY)


def gather_chips(slabs):
    ns = len(slabs)

    def body(*refs):
        x_refs, out_refs = refs[:ns], refs[ns:2 * ns]
        send_sems, recv_sems = refs[2 * ns:]
        x, y, c = _coords()
        me = 2 * x + y
        sibling = (x, y, 1 - c)
        chips = _other_chips(x, y)

        def half(s, chip, hc):
            rh = slabs[s].shape[0] // 2
            return out_refs[s].at[chip, pl.ds(hc * rh, rh), :]

        def own_half(s):
            rh = slabs[s].shape[0] // 2
            return x_refs[s].at[pl.ds(c * rh, rh), :]

        sends = []
        for s in range(ns):
            for j, (px, py) in enumerate(chips):
                cp = pltpu.make_async_remote_copy(
                    src_ref=own_half(s), dst_ref=half(s, me, c), send_sem=send_sems.at[6 * s + j],
                    recv_sem=recv_sems.at[6 * s + j], device_id=(px, py, c), device_id_type=MESH)
                cp.start()
                sends.append(cp)
        for s in range(ns):
            for j, (px, py) in enumerate(chips):
                src = 2 * px + py
                landed = pltpu.make_async_remote_copy(
                    src_ref=half(s, src, c), dst_ref=half(s, src, c), send_sem=send_sems.at[6 * s + j],
                    recv_sem=recv_sems.at[6 * s + j], device_id=(px, py, c), device_id_type=MESH)
                landed.wait_recv()
                fw = pltpu.make_async_remote_copy(
                    src_ref=half(s, src, c), dst_ref=half(s, src, c), send_sem=send_sems.at[6 * s + 3 + j],
                    recv_sem=recv_sems.at[6 * s + 3 + j], device_id=sibling, device_id_type=MESH)
                fw.start()
                sends.append(fw)
        for s in range(ns):
            for j, (px, py) in enumerate(chips):
                src = 2 * px + py
                got = pltpu.make_async_remote_copy(
                    src_ref=half(s, src, 1 - c), dst_ref=half(s, src, 1 - c), send_sem=send_sems.at[6 * s + 3 + j],
                    recv_sem=recv_sems.at[6 * s + 3 + j], device_id=sibling, device_id_type=MESH)
                got.wait_recv()
        for cp in sends:
            cp.wait_send()

    return pl.pallas_call(
        body, name="gather_chips",
        in_specs=[_ANY] * ns, out_specs=[_ANY] * ns,
        out_shape=[jax.ShapeDtypeStruct((4,) + s.shape, s.dtype) for s in slabs],
        scratch_shapes=[pltpu.SemaphoreType.DMA((6 * ns,)), pltpu.SemaphoreType.DMA((6 * ns,))],
    )(*slabs)


def swap_halves(gss):
    ns = len(gss)

    def body(*refs):
        g_refs, out_refs = refs[:ns], refs[ns:2 * ns]
        send_sems, recv_sems = refs[2 * ns:]
        x, y, c = _coords()
        cps = []
        for s in range(ns):
            rh = gss[s].shape[1] // 2
            cp = pltpu.make_async_remote_copy(
                src_ref=g_refs[s].at[:, pl.ds((1 - c) * rh, rh), :], dst_ref=out_refs[s],
                send_sem=send_sems.at[s], recv_sem=recv_sems.at[s], device_id=(x, y, 1 - c), device_id_type=MESH)
            cp.start()
            cps.append(cp)
        for cp in cps:
            cp.wait()

    return pl.pallas_call(
        body, name="swap_halves",
        in_specs=[_ANY] * ns, out_specs=[_ANY] * ns,
        out_shape=[jax.ShapeDtypeStruct((4, g.shape[1] // 2, g.shape[2]), g.dtype) for g in gss],
        scratch_shapes=[pltpu.SemaphoreType.DMA((ns,)), pltpu.SemaphoreType.DMA((ns,))],
    )(*gss)


def scatter_chips(pbs):
    ns = len(pbs)

    def body(*refs):
        p_refs, out_refs = refs[:ns], refs[ns:2 * ns]
        send_sems, recv_sems = refs[2 * ns:]
        x, y, c = _coords()
        cps = []
        for s in range(ns):
            for j, (px, py) in enumerate(_other_chips(x, y)):
                cp = pltpu.make_async_remote_copy(
                    src_ref=p_refs[s].at[2 * px + py], dst_ref=out_refs[s].at[j], send_sem=send_sems.at[3 * s + j],
                    recv_sem=recv_sems.at[3 * s + j], device_id=(px, py, c), device_id_type=MESH)
                cp.start()
                cps.append(cp)
        for cp in cps:
            cp.wait()

    return pl.pallas_call(
        body, name="scatter_chips",
        in_specs=[_ANY] * ns, out_specs=[_ANY] * ns,
        out_shape=[jax.ShapeDtypeStruct((3,) + p.shape[1:], p.dtype) for p in pbs],
        scratch_shapes=[pltpu.SemaphoreType.DMA((3 * ns,)), pltpu.SemaphoreType.DMA((3 * ns,))],
    )(*pbs)


def join_halves(reds):
    ns = len(reds)

    def body(*refs):
        r_refs, out_refs = refs[:ns], refs[ns:2 * ns]
        send_sems, recv_sems = refs[2 * ns:]
        x, y, c = _coords()
        cps = []
        for s in range(ns):
            cp = pltpu.make_async_remote_copy(
                src_ref=r_refs[s], dst_ref=out_refs[s], send_sem=send_sems.at[s], recv_sem=recv_sems.at[s],
                device_id=(x, y, 1 - c), device_id_type=MESH)
            cp.start()
            cps.append(cp)
        for cp in cps:
            cp.wait()

    return pl.pallas_call(
        body, name="join_halves",
        in_specs=[_ANY] * ns, out_specs=[_ANY] * ns,
        out_shape=[jax.ShapeDtypeStruct(r.shape, r.dtype) for r in reds],
        scratch_shapes=[pltpu.SemaphoreType.DMA((ns,)), pltpu.SemaphoreType.DMA((ns,))],
    )(*reds)


def gather_all(vec, name):
    r, wd = vec.shape

    def body(v_ref, out_ref, send_sems, recv_sems):
        x, y, c = _coords()
        me = 4 * x + 2 * y + c
        cps = []
        for k in range(1, 8):
            mx, my, mc = (k >> 2) & 1, (k >> 1) & 1, k & 1
            peer = (x ^ mx, y ^ my, c ^ mc)
            cp = pltpu.make_async_remote_copy(
                src_ref=v_ref, dst_ref=out_ref.at[me],
                send_sem=send_sems.at[k - 1], recv_sem=recv_sems.at[k - 1], device_id=peer, device_id_type=MESH)
            cp.start()
            cps.append(cp)
        for k in range(1, 8):
            mx, my, mc = (k >> 2) & 1, (k >> 1) & 1, k & 1
            peer = (x ^ mx, y ^ my, c ^ mc)
            src = 4 * peer[0] + 2 * peer[1] + peer[2]
            got = pltpu.make_async_remote_copy(
                src_ref=v_ref, dst_ref=out_ref.at[src],
                send_sem=send_sems.at[k - 1], recv_sem=recv_sems.at[k - 1], device_id=peer, device_id_type=MESH)
            got.wait_recv()
        for cp in cps:
            cp.wait_send()

    return pl.pallas_call(
        body, name=name,
        in_specs=[_ANY], out_specs=_ANY,
        out_shape=jax.ShapeDtypeStruct((8, r, wd), vec.dtype),
        scratch_shapes=[pltpu.SemaphoreType.DMA((7,)), pltpu.SemaphoreType.DMA((7,))],
    )(vec)


def _pad_rows(a, rows):
    return jnp.pad(a, ((0, rows - a.shape[0]), (0, 0)))


def _pad_cols(a, cols):
    return jnp.pad(a, ((0, 0), (0, cols - a.shape[1])))


def local_step(x, c, ctx, c_ctx, target, wt, sm):
    t, tc = x.shape[0], ctx.shape[0]
    tp = t + tc
    nct = tc // TR
    ncc = tc // LC
    nc = tp // LC

    w_in = wt["w_in"]
    segs = {"q": (0, D), "k": (D, 2 * D), "v": (2 * D, 3 * D), "g": (3 * D, 3 * D + NGATE)}
    base = 3 * D + NGATE
    for n_i, nm in enumerate(("o", "u", "vg", "gm", "gg")):
        segs[nm] = (base + n_i * D, base + (n_i + 1) * D)
    order = ("o", "u", "vg", "gm", "gg", "v", "q", "k")
    w_main = jnp.concatenate([w_in[:, segs[nm][0]:segs[nm][1]] for nm in order], axis=1)
    w_g = _pad_cols(w_in[:, segs["g"][0]:segs["g"][1]], LANE)
    w_main_t = w_main.T
    w_g_t = w_g.T

    cc = _pad_rows(jnp.concatenate([c.reshape(1, D), c_ctx.reshape(1, D)], axis=0), 16)
    modv = mod_fwd(cc, wt["w_mod"], sm["b_mod"].reshape(1, NMOD * D))
    mx = modv[0].reshape(NMOD, D)
    mc = modv[1].reshape(NMOD, D)
    tab1 = _pad_rows(jnp.stack([mc[0], mc[1], mx[0], mx[1]]), 8)
    tab2 = _pad_rows(jnp.stack([mx[2], mx[3], mx[4]]), 8)
    tab3 = _pad_rows(mx[5:6], 8)

    g1 = sm["norm1_g"].reshape(1, D)
    g2 = sm["norm2_g"].reshape(1, D)
    gfin = sm["final_g"].reshape(1, D)
    gh = sm["head_norm_g"].reshape(1, D)
    ln_g = sm["sgu_ln_g"].reshape(1, D)
    ln_b = sm["sgu_ln_b"].reshape(1, D)
    ws = sm["w_s"].reshape(SG * SCH, SCH).astype(BF16)
    bs_t = _pad_cols(sm["b_s"].reshape(SG, SCH).T, LANE)
    conv_w = _pad_rows(sm["conv_qk"].reshape(3, 2 * D), 8)
    b_gate = _pad_cols(sm["b_gate"].reshape(1, NGATE), LANE)
    wc = _pad_rows(sm["w_ffn_conv"].reshape(9, DFF), 16)

    hn1 = norm1_fwd(ctx, x, g1, tab1, nct)
    zmain = mm_nn(hn1, w_main, BF16, "mm_zmain")
    zg = mm_nn(hn1, w_g, F32, "mm_zg")
    qa, ka, gcol, zgb = qkconv_fwd(zmain, zg, conv_w, b_gate, nct)
    grow = gcol[:, :16].reshape(nc, LC, 16).transpose(0, 2, 1)
    hf, hb, csf, csb, nsf, nsb, msf, msb = mlstm_fwd(qa, ka, zmain, gcol, grow, ncc)
    yms = mixer_fwd(hf, hb, zmain, gh, ln_g, ln_b, ws, bs_t, nct)
    w_br = jnp.stack([wt["w_branch_mlstm"], wt["w_branch_sgu"]])
    pp = mm_nn(yms, w_br, BF16, "mm_branch")
    y = merge_fwd(zmain, pp, nct)
    out = mm_nn(y, wt["w_out"], BF16, "mm_out")
    h1, hn2 = norm2_fwd(x, out, g2, tab2)
    up = mm_nn(hn2, wt["w_up"], BF16, "mm_up")
    act, ac = ffn_act_fwd(up, wc)
    f = mm_nn(act, wt["w_down"], BF16, "mm_down")
    dh2, df, acc_h = head_fwd_bwd(h1, f, target, gfin, tab3)
    loss = acc_h[0, 0]

    g_w_down = mm_tn(act, df, "mmt_down", BF16)
    dact = mm_nn(df, wt["w_down"].T, BF16, "mm_ddown")
    dup, dac = ffn_act_bwd(up, ac, dact)
    dup, g_wc = ffn_conv_bwd(dup, up, dac, wc)
    g_w_up = mm_tn(hn2, dup, "mmt_up", BF16)
    dhn2 = mm_nn(dup, wt["w_up"].T, BF16, "mm_dup")
    dh1, dout, acc_2 = norm2_bwd(h1, dhn2, dh2, out, g2, tab2)
    g_w_out = mm_tn(y, dout, "mmt_out", BF16)
    dy = mm_nn(dout, wt["w_out"].T, BF16, "mm_dout")
    dpp, dz = merge_bwd(zmain, pp, dy, tp, nct)
    g_w_br = mm_tn(yms, dpp, "mmt_branch", BF16)
    dyms = mm_nn(dpp, jnp.stack([wt["w_branch_mlstm"].T, wt["w_branch_sgu"].T]), BF16, "mm_dbranch")
    dz, dhm, g_gh, g_lng, g_lnb, g_ws, g_bs = mixer_bwd(dz, hf, hb, zmain, dyms, gh, ln_g, ln_b, ws, bs_t, nct)
    (dqf, dkf, dvf, colf, rowf, dqb, dkb, dvb, colb, rowb) = mlstm_bwd(
        qa, ka, zmain, gcol, grow, (csf, csb, nsf, nsb, msf, msb), dhm, ncc)

    csum_f = rowf[:, :4, :].transpose(0, 2, 1).reshape(tp, 4)
    csum_b = rowb[:, :4, :].transpose(0, 2, 1).reshape(tp, 4)
    cs = _pad_cols(jnp.concatenate([csum_f, csum_f, csum_b, csum_b], axis=1), LANE)
    dzg, g_bgate = gates_bwd(colf, colb, cs, zgb)

    dc, g_convw = qkconv_bwd_a(zmain, dqf, dqb, dkf, dkb, conv_w, nct)
    dz = qkconv_bwd_b(dz, dc, conv_w, nct)
    dz = add_into_dz(dz, dvf, dvb, CB_V)

    g_w_main = mm_tn(hn1, dz, "mmt_main", BF16)
    g_w_g = mm_tn(hn1, dzg, "mmt_g", BF16)
    da = mm_nn(dz, w_main_t, BF16, "mm_dmain")
    db = mm_nn(dzg, w_g_t, F32, "mm_dg")
    grad_x, acc_1 = norm1_bwd(ctx, x, da, db, dh1, g1, tab1, nct)

    blk = lambda cb: g_w_main[:, cb * D:(cb + 1) * D]
    g_w_in = jnp.concatenate([blk(CB_Q), blk(CB_K), blk(CB_V), g_w_g[:, :NGATE], blk(CB_O), blk(CB_U), blk(CB_VG),
                              blk(CB_GM), blk(CB_GG)], axis=1)

    d_modx = jnp.concatenate([acc_1[2], acc_1[3], acc_2[3], acc_2[0], acc_2[1], acc_h[2]])
    d_modc = jnp.concatenate([acc_1[0], acc_1[1], jnp.zeros((4 * D,), F32)])
    d_modb = jnp.concatenate([acc_1[5], acc_1[6], acc_2[3], acc_2[0], acc_2[1], acc_h[2]])

    big = {"w_in": g_w_in, "w_branch_mlstm": g_w_br[0], "w_branch_sgu": g_w_br[1], "w_out": g_w_out,
           "w_up": g_w_up, "w_down": g_w_down}
    small = {"b_mod": d_modb, "norm1_g": acc_1[4], "b_gate": g_bgate[0, :NGATE], "conv_qk": g_convw[:3].reshape(-1),
             "head_norm_g": g_gh[0], "sgu_ln_g": g_lng[0], "sgu_ln_b": g_lnb[0], "w_s": g_ws.reshape(-1),
             "b_s": g_bs[:, :SG].T.reshape(-1), "norm2_g": acc_2[2], "w_ffn_conv": g_wc[:9].reshape(-1),
             "final_g": acc_h[1]}
    return loss, grad_x, big, small, d_modx, d_modc


def mod_bwd_w(a_all, dm_all, name):
    n = dm_all.shape[1]
    tn = _pick(n, (512, 128))

    def body(a_ref, d_ref, o_ref):
        o_ref[...] = _dot_tn(_silu(a_ref[...]), d_ref[...])

    return pl.pallas_call(
        body, name=name, grid=(n // tn,),
        in_specs=[_bc(16, D), pl.BlockSpec((16, tn), lambda j: (0, j))],
        out_specs=pl.BlockSpec((D, tn), lambda j: (0, j)),
        out_shape=jax.ShapeDtypeStruct((D, n), F32),
        compiler_params=_cparams(("parallel",)),
    )(a_all, dm_all)


def mod_bwd_cctx(dmc, w_mod_t, c_ctx):
    def body(d_ref, w_ref, c_ref, o_ref):
        o_ref[...] = _dot(d_ref[...], w_ref[...]) * _dsilu(c_ref[...])

    return pl.pallas_call(
        body, name="mod_bwd_cctx", grid=(1,),
        in_specs=[_bc(16, 2 * D), _bc(2 * D, D), _bc(1, D)],
        out_specs=_bc(16, D),
        out_shape=jax.ShapeDtypeStruct((16, D), F32),
        compiler_params=_cparams(("arbitrary",)),
    )(dmc, w_mod_t, c_ctx)


BIG = ("w_mod", "w_in", "w_branch_mlstm", "w_branch_sgu", "w_out", "w_up", "w_down")
BIG_AXIS = {"w_mod": 1, "w_in": 1, "w_branch_mlstm": 0, "w_branch_sgu": 0, "w_out": 0, "w_up": 1, "w_down": 0}
SMALL = ("c_ctx", "b_mod", "norm1_g", "b_gate", "conv_qk", "head_norm_g", "sgu_ln_g", "sgu_ln_b", "w_s", "b_s",
         "norm2_g", "w_ffn_conv", "final_g")
SMALL_SHARDED = {"conv_qk": (3, 2 * D), "w_ffn_conv": (9, DFF)}
PACK_ALIGN = 32 * D


def _pack(arrs, dtype, align=PACK_ALIGN, width=D):
    flat = jnp.concatenate([a.reshape(-1).astype(dtype) for a in arrs])
    n = flat.shape[0]
    padded = -(-n // align) * align
    return jnp.pad(flat, (0, padded - n)).reshape(padded // width, width)


def _unpack(slab, shapes):
    flat = slab.reshape(-1)
    outs, off = [], 0
    for shp in shapes:
        n = math.prod(shp)
        outs.append(flat[off:off + n].reshape(shp))
        off += n
    return outs


def _round_up(n, m):
    return -(-n // m) * m


def kernel(x, c, ctx, c_ctx, w_mod, b_mod, norm1_g, w_in, b_gate, conv_qk, head_norm_g, sgu_ln_g, sgu_ln_b, w_s, b_s, w_branch_mlstm, w_branch_sgu, w_out, norm2_g, w_up, w_ffn_conv, w_down, final_g, loss_target, m_c_ctx, m_w_mod, m_b_mod, m_norm1_g, m_w_in, m_b_gate, m_conv_qk, m_head_norm_g, m_sgu_ln_g, m_sgu_ln_b, m_w_s, m_b_s, m_w_branch_mlstm, m_w_branch_sgu, m_w_out, m_norm2_g, m_w_up, m_w_ffn_conv, m_w_down, m_final_g, v_c_ctx, v_w_mod, v_b_mod, v_norm1_g, v_w_in, v_b_gate, v_conv_qk, v_head_norm_g, v_sgu_ln_g, v_sgu_ln_b, v_w_s, v_b_s, v_w_branch_mlstm, v_w_branch_sgu, v_w_out, v_norm2_g, v_w_up, v_w_ffn_conv, v_w_down, v_final_g):
    params = dict(c_ctx=c_ctx, w_mod=w_mod, b_mod=b_mod, norm1_g=norm1_g, w_in=w_in, b_gate=b_gate, conv_qk=conv_qk,
                  head_norm_g=head_norm_g, sgu_ln_g=sgu_ln_g, sgu_ln_b=sgu_ln_b, w_s=w_s, b_s=b_s,
                  w_branch_mlstm=w_branch_mlstm, w_branch_sgu=w_branch_sgu, w_out=w_out, norm2_g=norm2_g, w_up=w_up,
                  w_ffn_conv=w_ffn_conv, w_down=w_down, final_g=final_g)
    mom_m = dict(c_ctx=m_c_ctx, w_mod=m_w_mod, b_mod=m_b_mod, norm1_g=m_norm1_g, w_in=m_w_in, b_gate=m_b_gate,
                 conv_qk=m_conv_qk, head_norm_g=m_head_norm_g, sgu_ln_g=m_sgu_ln_g, sgu_ln_b=m_sgu_ln_b, w_s=m_w_s,
                 b_s=m_b_s, w_branch_mlstm=m_w_branch_mlstm, w_branch_sgu=m_w_branch_sgu, w_out=m_w_out,
                 norm2_g=m_norm2_g, w_up=m_w_up, w_ffn_conv=m_w_ffn_conv, w_down=m_w_down, final_g=m_final_g)
    mom_v = dict(c_ctx=v_c_ctx, w_mod=v_w_mod, b_mod=v_b_mod, norm1_g=v_norm1_g, w_in=v_w_in, b_gate=v_b_gate,
                 conv_qk=v_conv_qk, head_norm_g=v_head_norm_g, sgu_ln_g=v_sgu_ln_g, sgu_ln_b=v_sgu_ln_b, w_s=v_w_s,
                 b_s=v_b_s, w_branch_mlstm=v_w_branch_mlstm, w_branch_sgu=v_w_branch_sgu, w_out=v_w_out,
                 norm2_g=v_norm2_g, w_up=v_w_up, w_ffn_conv=v_w_ffn_conv, w_down=v_w_down, final_g=v_final_g)
    chip = 2 * lax.axis_index("x") + lax.axis_index("y")

    shard2d = {n: params[n].reshape(params[n].shape[-2:]) for n in BIG}
    conv_sh = conv_qk.reshape(3, -1)
    fconv_sh = w_ffn_conv.reshape(9, -1)

    dev = 2 * chip + lax.axis_index("c")

    col_names, row_names = ("w_mod", "w_up", "w_in"), ("w_branch_mlstm", "w_branch_sgu", "w_out", "w_down")
    col_w = [shard2d[n].shape[1] for n in col_names]
    row_h = [shard2d[n].shape[0] for n in row_names]
    col_slab = _pad_cols(jnp.concatenate([shard2d[n].astype(BF16) for n in col_names], axis=1),
                         _round_up(sum(col_w), LANE))
    row_slab = jnp.concatenate([shard2d[n].astype(BF16) for n in row_names], axis=0)
    col_all, row_all = gather_chips([col_slab, row_slab])
    col_all = jnp.stack([jnp.where(chip == j, col_slab, col_all[j]) for j in range(4)])
    row_all = jnp.stack([jnp.where(chip == j, row_slab, row_all[j]) for j in range(4)])
    wt = {}
    off = 0
    for n, wd in zip(col_names, col_w):
        wt[n] = jnp.concatenate([col_all[j, :, off:off + wd] for j in range(4)], axis=1)
        off += wd
    off = 0
    for n, ht in zip(row_names, row_h):
        wt[n] = jnp.concatenate([row_all[j, off:off + ht, :] for j in range(4)], axis=0)
        off += ht

    cvec = _pack([conv_sh, fconv_sh], F32, align=8 * LANE, width=LANE)
    call = gather_all(cvec, "gather_conv")
    cparts = [_unpack(jnp.where(dev == 2 * j, cvec, call[2 * j]), [conv_sh.shape, fconv_sh.shape]) for j in range(4)]
    conv_full = jnp.concatenate([p[0] for p in cparts], axis=1)
    fconv_full = jnp.concatenate([p[1] for p in cparts], axis=1)

    sm = dict(b_mod=b_mod, norm1_g=norm1_g, b_gate=b_gate, conv_qk=conv_full, head_norm_g=head_norm_g,
              sgu_ln_g=sgu_ln_g, sgu_ln_b=sgu_ln_b, w_s=w_s, b_s=b_s, norm2_g=norm2_g, w_ffn_conv=fconv_full,
              final_g=final_g)

    loss_l, grad_x, gbig, gsmall, d_modx, d_modc = local_step(
        x[0], c, ctx[0], c_ctx, loss_target[0], wt, sm)

    gcol_names, gcol_w = col_names[1:], col_w[1:]
    gcol_pad = _round_up(sum(gcol_w), LANE)

    def chip_cols(j):
        return _pad_cols(jnp.concatenate([gbig[n][:, j * wd:(j + 1) * wd] for n, wd in zip(gcol_names, gcol_w)],
                                         axis=1), gcol_pad)

    def chip_rows(j):
        return jnp.concatenate([gbig[n][j * ht:(j + 1) * ht] for n, ht in zip(row_names, row_h)], axis=0)

    gss = [jnp.stack([chip_cols(j) for j in range(4)]), jnp.stack([chip_rows(j) for j in range(4)])]
    cidx = lax.axis_index("c")
    from_sib = swap_halves(gss)
    reds = []
    pair_bf, own_terms = [], []
    for s, (gs, fs) in enumerate(zip(gss, from_sib)):
        rh = gs.shape[1] // 2
        my_half = lax.dynamic_slice_in_dim(gs, cidx * rh, rh, axis=1)
        pair_bf.append(add_n([my_half, fs], BF16, "pair_sum_%d" % s))
        own_terms.append([lax.dynamic_index_in_dim(my_half, chip, axis=0, keepdims=False),
                          lax.dynamic_index_in_dim(fs, chip, axis=0, keepdims=False)])
    recv = scatter_chips(pair_bf)
    for s in range(2):
        reds.append(add_n(own_terms[s] + [recv[s][0], recv[s][1], recv[s][2]], F32, "chip_sum_%d" % s))
    others = join_halves(reds)
    full_red = [jnp.where(cidx == 0, jnp.concatenate([m, o], axis=0), jnp.concatenate([o, m], axis=0))
                for m, o in zip(reds, others)]
    g_shard = {}
    off = 0
    for n, wd in zip(gcol_names, gcol_w):
        g_shard[n] = full_red[0][:, off:off + wd]
        off += wd
    off = 0
    for n, ht in zip(row_names, row_h):
        g_shard[n] = full_red[1][off:off + ht]
        off += ht

    small_order = ("b_mod", "norm1_g", "b_gate", "conv_qk", "head_norm_g", "sgu_ln_g", "sgu_ln_b", "w_s", "b_s", "norm2_g",
                   "w_ffn_conv", "final_g")
    vec_parts = [gsmall[n] for n in small_order] + [d_modx, d_modc, c.reshape(-1), loss_l.reshape(1)]
    vec_shapes = [a.shape for a in vec_parts]
    vec = _pack(vec_parts, F32, align=8 * LANE, width=LANE)
    allv = gather_all(vec, "gather_small")
    allv = jnp.stack([jnp.where(dev == k, vec, allv[k]) for k in range(8)])
    summed = sum8(allv, "small_sum")
    s_parts = _unpack(summed, vec_shapes)
    g_small = dict(zip(small_order, s_parts[:len(small_order)]))
    dmc_sum = s_parts[len(small_order) + 1]
    loss = s_parts[-1][0]
    per_dev = [_unpack(allv[k], vec_shapes) for k in range(8)]
    dmx_all = jnp.stack([p[len(small_order)] for p in per_dev])
    c_all = jnp.stack([p[len(small_order) + 2] for p in per_dev])

    a_all = _pad_rows(jnp.concatenate([c_all, c_ctx.reshape(1, D)], axis=0), 16)
    dm_all = _pad_rows(jnp.concatenate([dmx_all, dmc_sum.reshape(1, NMOD * D)], axis=0), 16)
    ncol = NMOD * D // 4
    dm_shard = lax.dynamic_slice_in_dim(dm_all, chip * ncol, ncol, axis=1)
    g_shard["w_mod"] = mod_bwd_w(a_all, dm_shard, "mod_bwd_w")
    w_mod_t = wt["w_mod"][:, :2 * D].T
    g_cctx = mod_bwd_cctx(_pad_rows(dmc_sum[:2 * D].reshape(1, 2 * D), 16), w_mod_t, c_ctx.reshape(1, D))[0]
    g_small["c_ctx"] = g_cctx

    results = {}
    for n in BIG:
        shp = params[n].shape
        g_ = g_shard[n].reshape(shp)
        d_, m_, v_ = adamw(params[n], g_, mom_m[n], mom_v[n], "adamw_" + n)
        results[n] = (g_, d_, m_, v_)

    conv_g = lax.dynamic_slice_in_dim(g_small["conv_qk"].reshape(3, 2 * D), chip * (2 * D // 4), 2 * D // 4, axis=1)
    fconv_g = lax.dynamic_slice_in_dim(g_small["w_ffn_conv"].reshape(9, DFF), chip * (DFF // 4), DFF // 4, axis=1)
    g_small["conv_qk"] = conv_g
    g_small["w_ffn_conv"] = fconv_g
    w_list = [params[n].reshape(-1) for n in SMALL]
    g_list = [g_small[n].reshape(-1) for n in SMALL]
    m_list = [mom_m[n].reshape(-1) for n in SMALL]
    v_list = [mom_v[n].reshape(-1) for n in SMALL]
    sm_shapes = [params[n].shape for n in SMALL]
    pk = lambda lst: _pack(lst, F32, align=8 * LANE, width=LANE)
    gp = pk(g_list)
    d_s, m_s, v_s = adamw(pk(w_list), gp, pk(m_list), pk(v_list), "adamw_small")
    for n, gg, dd, mm, vv in zip(SMALL, _unpack(gp, sm_shapes), _unpack(d_s, sm_shapes), _unpack(m_s, sm_shapes),
                                 _unpack(v_s, sm_shapes)):
        results[n] = (gg, dd, mm, vv)

    order = ("c_ctx", "w_mod", "b_mod", "norm1_g", "w_in", "b_gate", "conv_qk", "head_norm_g", "sgu_ln_g", "sgu_ln_b",
             "w_s", "b_s", "w_branch_mlstm", "w_branch_sgu", "w_out", "norm2_g", "w_up", "w_ffn_conv", "w_down",
             "final_g")
    outs = [loss, grad_x[None]]
    for k in range(4):
        outs += [results[n][k] for n in order]
    return tuple(outs)
```

```python
import functools
import math

import jax
import jax.numpy as jnp
from jax import lax
from jax.experimental import pallas as pl
from jax.experimental.pallas import tpu as pltpu

F32 = jnp.float32
BF16 = jnp.bfloat16

D = 1024
NH = 4
DH = 256
LC = 256
GW = 64
SG = 4
SGD = 256
SCH = 128
DFF = 2816
NMOD = 6
NGATE = 16
NIN = 8208
EPS = 1e-6
M_INIT = -1e30
TR = 256
LANE = 128
VMEM_LIMIT = 56 * 1024 * 1024
MESH = pl.DeviceIdType.MESH

ADAM_LR = 0.001
ADAM_B1 = 0.9
ADAM_B2 = 0.999
ADAM_EPS = 1e-08
ADAM_WD = 0.01
ADAM_STEP = 10

CB_O, CB_U, CB_VG, CB_GM, CB_GG, CB_V, CB_Q, CB_K = range(8)


def _pick(n, cands):
    for c in cands:
        if n % c == 0:
            return c
    return n


def _cparams(sem):
    return pltpu.CompilerParams(dimension_semantics=sem, vmem_limit_bytes=VMEM_LIMIT)


def _sigmoid(x):
    return 1.0 / (1.0 + jnp.exp(-x))


def _silu(x):
    return x * _sigmoid(x)


def _dsilu(x):
    s = _sigmoid(x)
    return s * (1.0 + x * (1.0 - s))


_GC = math.sqrt(2.0 / math.pi)


def _gelu(x):
    return 0.5 * x * (1.0 + jnp.tanh(_GC * (x + 0.044715 * x * x * x)))


def _dgelu(x):
    t = jnp.tanh(_GC * (x + 0.044715 * x * x * x))
    return 0.5 * (1.0 + t) + 0.5 * x * (1.0 - t * t) * _GC * (1.0 + 3.0 * 0.044715 * x * x)


def _dot(a, b):
    return jnp.dot(a.astype(BF16), b.astype(BF16), preferred_element_type=F32)


def _dot_nt(a, b):
    return lax.dot_general(a.astype(BF16), b.astype(BF16), (((1,), (1,)), ((), ())), preferred_element_type=F32)


def _dot_tn(a, b):
    return lax.dot_general(a.astype(BF16), b.astype(BF16), (((0,), (0,)), ((), ())), preferred_element_type=F32)


def _dot_tn_mxu(a, b):
    m = a.shape[1]
    eye = (lax.broadcasted_iota(jnp.int32, (m, m), 0) == lax.broadcasted_iota(jnp.int32, (m, m), 1)).astype(BF16)
    return _dot(_dot_nt(eye, a), b)


def _exact_dot(tri, x):
    x1 = x.astype(BF16)
    r1 = x - x1.astype(F32)
    x2 = r1.astype(BF16)
    x3 = (r1 - x2.astype(F32)).astype(BF16)
    return (jnp.dot(tri, x1, preferred_element_type=F32) + jnp.dot(tri, x2, preferred_element_type=F32)
            + jnp.dot(tri, x3, preferred_element_type=F32))


def _rb(tm, w, col=0, off=0):
    return pl.BlockSpec((tm, w), lambda i: (i + off, col))


def _bc(r, w):
    return pl.BlockSpec((r, w), lambda i: (0, 0))


class Rider:
    def __init__(self, ins, out_shapes, n_sems, copies):
        self.ins, self.out_shapes, self.n_sems, self.copies = list(ins), list(out_shapes), n_sems, copies


def mm_nn(a, b, out_dtype, name, rider=None):
    squeeze = a.ndim == 2
    if squeeze:
        a, b = a[None], b[None]
    g, m, k = a.shape
    n = b.shape[2]
    tm = _pick(m, (1280, 1024, 512, 256, 128))
    tn = _pick(n, (2048, 1408, 1024, 512, 128))
    tk = _pick(k, (2048, 1408, 1024, 512, 128))
    nk = k // tk
    grid = (g, n // tn, m // tm, nk)
    n_rin = len(rider.ins) if rider else 0
    n_rout = len(rider.out_shapes) if rider else 0

    def body(*refs):
        a_ref, b_ref = refs[0], refs[1]
        r_in = refs[2:2 + n_rin]
        o_ref = refs[2 + n_rin]
        r_out = refs[3 + n_rin:3 + n_rin + n_rout]
        scr = refs[3 + n_rin + n_rout:]
        if rider:
            ids = [pl.program_id(d) for d in range(4)]
            first = functools.reduce(jnp.logical_and, [i == 0 for i in ids])
            last = functools.reduce(jnp.logical_and, [i == e - 1 for i, e in zip(ids, grid)])
            send_sems, recv_sems = scr[-2], scr[-1]

            @pl.when(first)
            def _():
                for cp in rider.copies(r_in, r_out, send_sems, recv_sems):
                    cp.start()

        if nk == 1:
            o_ref[0] = _dot(a_ref[0], b_ref[0]).astype(o_ref.dtype)
        else:
            acc_ref = scr[0]
            kk = pl.program_id(3)

            @pl.when(kk == 0)
            def _():
                acc_ref[...] = jnp.zeros_like(acc_ref)

            acc_ref[...] += _dot(a_ref[0], b_ref[0])

            @pl.when(kk == nk - 1)
            def _():
                o_ref[0] = acc_ref[...].astype(o_ref.dtype)

        if rider:
            @pl.when(last)
            def _():
                for cp in rider.copies(r_in, r_out, send_sems, recv_sems):
                    cp.wait()

    scratch = [] if nk == 1 else [pltpu.VMEM((tm, tn), F32)]
    if rider:
        scratch += [pltpu.SemaphoreType.DMA((rider.n_sems,)), pltpu.SemaphoreType.DMA((rider.n_sems,))]
    outs = pl.pallas_call(
        body, name=name, grid=grid,
        in_specs=[pl.BlockSpec((1, tm, tk), lambda gi, j, i, kk: (gi, i, kk)),
                  pl.BlockSpec((1, tk, tn), lambda gi, j, i, kk: (gi, kk, j))] + [_ANY] * n_rin,
        out_specs=[pl.BlockSpec((1, tm, tn), lambda gi, j, i, kk: (gi, i, j))] + [_ANY] * n_rout,
        out_shape=[jax.ShapeDtypeStruct((g, m, n), out_dtype)] + (rider.out_shapes if rider else []),
        scratch_shapes=scratch,
        compiler_params=_cparams(("arbitrary",) * 4 if rider else ("parallel", "parallel", "parallel", "arbitrary")),
    )(a, b, *(rider.ins if rider else []))
    out = outs[0][0] if squeeze else outs[0]
    return (out, list(outs[1:])) if rider else out


def mm_tn(a, b, name, out_dtype=F32):
    squeeze = a.ndim == 2
    if squeeze:
        a, b = a[None], b[None]
    g, t, ka = a.shape
    n = b.shape[2]
    tka = _pick(ka, (1024, 1408, 512, 128))
    tn = _pick(n, (2048, 1408, 1024, 512, 128))
    tt = _pick(t, (1280, 1024, 512, 256, 128))
    nt = t // tt

    def body(a_ref, b_ref, o_ref, acc_ref):
        tt_i = pl.program_id(3)

        @pl.when(tt_i == 0)
        def _():
            acc_ref[...] = jnp.zeros_like(acc_ref)

        acc_ref[...] += _dot_tn(a_ref[0], b_ref[0])

        @pl.when(tt_i == nt - 1)
        def _():
            o_ref[0] = acc_ref[...].astype(o_ref.dtype)

    out = pl.pallas_call(
        body, name=name, grid=(g, ka // tka, n // tn, nt),
        in_specs=[pl.BlockSpec((1, tt, tka), lambda gi, i, j, ti: (gi, ti, i)),
                  pl.BlockSpec((1, tt, tn), lambda gi, i, j, ti: (gi, ti, j))],
        out_specs=pl.BlockSpec((1, tka, tn), lambda gi, i, j, ti: (gi, i, j)),
        out_shape=jax.ShapeDtypeStruct((g, ka, n), out_dtype),
        scratch_shapes=[pltpu.VMEM((tka, tn), F32)],
        compiler_params=_cparams(("parallel", "parallel", "parallel", "arbitrary")),
    )(a, b)
    return out[0] if squeeze else out


def mod_fwd(cc, w_mod, b_mod):
    n = w_mod.shape[1]

    def body(c_ref, w_ref, b_ref, o_ref):
        o_ref[...] = _dot(_silu(c_ref[...]), w_ref[...]) + b_ref[...]

    return pl.pallas_call(
        body, name="mod_fwd", grid=(n // D,),
        in_specs=[_bc(16, D), pl.BlockSpec((D, D), lambda j: (0, j)), pl.BlockSpec((1, D), lambda j: (0, j))],
        out_specs=pl.BlockSpec((16, D), lambda j: (0, j)),
        out_shape=jax.ShapeDtypeStruct((16, n), F32),
        compiler_params=_cparams(("parallel",)),
    )(cc, w_mod, b_mod)


def _ctx_x_specs(n_ctx_tiles):
    return [pl.BlockSpec((TR, D), lambda i: (jnp.minimum(i, n_ctx_tiles - 1), 0)),
            pl.BlockSpec((TR, D), lambda i: (jnp.maximum(i - n_ctx_tiles, 0), 0))]


def norm1_fwd(ctx, x, g, tab, n_ctx_tiles):
    tp = ctx.shape[0] + x.shape[0]

    def body(c_ref, x_ref, g_ref, tab_ref, o_ref):
        is_ctx = pl.program_id(0) < n_ctx_tiles
        x = jnp.where(is_ctx, c_ref[...], x_ref[...])
        r = lax.rsqrt(jnp.mean(x * x, axis=-1, keepdims=True) + EPS)
        nrm = x * r * g_ref[...]
        sh = jnp.where(is_ctx, tab_ref[0:1, :], tab_ref[2:3, :])
        sc = jnp.where(is_ctx, tab_ref[1:2, :], tab_ref[3:4, :])
        o_ref[...] = (nrm * (1.0 + sc) + sh).astype(BF16)

    return pl.pallas_call(
        body, name="norm1_fwd", grid=(tp // TR,),
        in_specs=_ctx_x_specs(n_ctx_tiles) + [_bc(1, D), _bc(8, D)],
        out_specs=_rb(TR, D),
        out_shape=jax.ShapeDtypeStruct((tp, D), BF16),
        compiler_params=_cparams(("parallel",)),
    )(ctx, x, g, tab)


def norm2_fwd(x, out, g, tab):
    t = x.shape[0]

    def body(x_ref, o_in_ref, g_ref, tab_ref, h1_ref, hn_ref):
        h1 = x_ref[...] + tab_ref[0:1, :] * o_in_ref[...].astype(F32)
        h1_ref[...] = h1
        r = lax.rsqrt(jnp.mean(h1 * h1, axis=-1, keepdims=True) + EPS)
        nrm = h1 * r * g_ref[...]
        hn_ref[...] = (nrm * (1.0 + tab_ref[2:3, :]) + tab_ref[1:2, :]).astype(BF16)

    return pl.pallas_call(
        body, name="norm2_fwd", grid=(t // TR,),
        in_specs=[_rb(TR, D), _rb(TR, D), _bc(1, D), _bc(8, D)],
        out_specs=[_rb(TR, D), _rb(TR, D)],
        out_shape=[jax.ShapeDtypeStruct((t, D), F32), jax.ShapeDtypeStruct((t, D), BF16)],
        compiler_params=_cparams(("parallel",)),
    )(x, out, g, tab)


def _halo_specs(tm, w, col, n_rows, hb):
    per = tm // hb
    last = n_rows // hb - 1
    prev = pl.BlockSpec((hb, w), lambda i: (jnp.maximum(i * per - 1, 0), col))
    nxt = pl.BlockSpec((hb, w), lambda i: (jnp.minimum((i + 1) * per, last), col))
    return prev, nxt


def _shift_rows(x, prev_row, next_row):
    tm = x.shape[0]
    rid = lax.broadcasted_iota(jnp.int32, x.shape, 0)
    xm1 = jnp.where(rid == 0, prev_row, pltpu.roll(x, 1, 0))
    xp1 = jnp.where(rid == tm - 1, next_row, pltpu.roll(x, tm - 1, 0))
    return xm1, xp1


def _seq_edges(i, n_ctx_tiles, n_tiles):
    first = jnp.logical_or(i == 0, i == n_ctx_tiles)
    last = jnp.logical_or(i == n_ctx_tiles - 1, i == n_tiles - 1)
    return first, last


def qkconv_fwd(zmain, zg, conv_w, b_gate, n_ctx_tiles):
    tp = zmain.shape[0]
    nt = tp // TR
    w2 = 2 * D
    prev_s, next_s = _halo_specs(TR, w2, CB_Q // 2, tp, 16)

    def body(z_ref, zp_ref, zn_ref, w_ref, zg_ref, bg_ref, q_ref, k_ref, g_ref, zgb_ref):
        i = pl.program_id(0)
        first, last = _seq_edges(i, n_ctx_tiles, nt)
        z = z_ref[...].astype(F32)
        pr = jnp.where(first, 0.0, zp_ref[15:16, :].astype(F32))
        nx = jnp.where(last, 0.0, zn_ref[0:1, :].astype(F32))
        zm1, zp1 = _shift_rows(z, pr, nx)
        cv = w_ref[0:1, :] * zm1 + w_ref[1:2, :] * z + w_ref[2:3, :] * zp1
        a = _silu(cv)
        q_ref[...] = (a[:, :D] * (DH ** -0.5)).astype(BF16)
        k_ref[...] = a[:, D:].astype(BF16)
        zgb = zg_ref[...] + bg_ref[...]
        zgb_ref[...] = zgb
        logf = jnp.minimum(zgb, 0.0) - jnp.log(1.0 + jnp.exp(-jnp.abs(zgb)))
        rr = lax.broadcasted_iota(jnp.int32, (TR, TR), 0)
        cc = lax.broadcasted_iota(jnp.int32, (TR, TR), 1)
        same = (rr // LC) == (cc // LC)
        low = jnp.where(jnp.logical_and(same, cc <= rr), 1.0, 0.0).astype(BF16)
        upp = jnp.where(jnp.logical_and(same, cc >= rr), 1.0, 0.0).astype(BF16)
        bf = _exact_dot(low, logf)
        bb = _exact_dot(upp, logf)
        lane = lax.broadcasted_iota(jnp.int32, (TR, LANE), 1)
        g = jnp.where(jnp.logical_and(lane >= 4, lane < 8), bf,
                      jnp.where(jnp.logical_and(lane >= 12, lane < 16), bb, zgb))
        g_ref[...] = g

    return pl.pallas_call(
        body, name="qkconv_fwd", grid=(nt,),
        in_specs=[_rb(TR, w2, CB_Q // 2), prev_s, next_s, _bc(8, w2), _rb(TR, LANE), _bc(1, LANE)],
        out_specs=[_rb(TR, D), _rb(TR, D), _rb(TR, LANE), _rb(TR, LANE)],
        out_shape=[jax.ShapeDtypeStruct((tp, D), BF16), jax.ShapeDtypeStruct((tp, D), BF16),
                   jax.ShapeDtypeStruct((tp, LANE), F32), jax.ShapeDtypeStruct((tp, LANE), F32)],
        compiler_params=_cparams(("parallel",)),
    )(zmain, zmain, zmain, conv_w, zg, b_gate)


def qkconv_bwd_a(zmain, dqf, dqb, dkf, dkb, conv_w, n_ctx_tiles):
    tp = zmain.shape[0]
    nt = tp // TR
    w2 = 2 * D
    prev_s, next_s = _halo_specs(TR, w2, CB_Q // 2, tp, 16)

    def body(z_ref, zp_ref, zn_ref, w_ref, dqf_ref, dqb_ref, dkf_ref, dkb_ref, dc_ref, dw_ref):
        i = pl.program_id(0)
        first, last = _seq_edges(i, n_ctx_tiles, nt)
        z = z_ref[...].astype(F32)
        pr = jnp.where(first, 0.0, zp_ref[15:16, :].astype(F32))
        nx = jnp.where(last, 0.0, zn_ref[0:1, :].astype(F32))
        zm1, zp1 = _shift_rows(z, pr, nx)
        cv = w_ref[0:1, :] * zm1 + w_ref[1:2, :] * z + w_ref[2:3, :] * zp1
        da = jnp.concatenate(
            [(dqf_ref[...].astype(F32) + dqb_ref[...].astype(F32)) * (DH ** -0.5),
             dkf_ref[...].astype(F32) + dkb_ref[...].astype(F32)], axis=1)
        dc = da * _dsilu(cv)
        dc_ref[...] = dc.astype(BF16)

        @pl.when(i == 0)
        def _():
            dw_ref[...] = jnp.zeros_like(dw_ref)

        dw_ref[0:1, :] += jnp.sum(zm1 * dc, axis=0, keepdims=True)
        dw_ref[1:2, :] += jnp.sum(z * dc, axis=0, keepdims=True)
        dw_ref[2:3, :] += jnp.sum(zp1 * dc, axis=0, keepdims=True)

    return pl.pallas_call(
        body, name="qkconv_bwd_a", grid=(nt,),
        in_specs=[_rb(TR, w2, CB_Q // 2), prev_s, next_s, _bc(8, w2), _rb(TR, D), _rb(TR, D), _rb(TR, D), _rb(TR, D)],
        out_specs=[_rb(TR, w2), _bc(8, w2)],
        out_shape=[jax.ShapeDtypeStruct((tp, w2), BF16), jax.ShapeDtypeStruct((8, w2), F32)],
        compiler_params=_cparams(("arbitrary",)),
    )(zmain, zmain, zmain, conv_w, dqf, dqb, dkf, dkb)


def qkconv_bwd_b(dz, dc, conv_w, n_ctx_tiles):
    tp = dc.shape[0]
    nt = tp // TR
    w2 = 2 * D
    prev_s, next_s = _halo_specs(TR, w2, 0, tp, 16)

    def body(dz_in_ref, d_ref, dp_ref, dn_ref, w_ref, o_ref):
        del dz_in_ref
        i = pl.program_id(0)
        first, last = _seq_edges(i, n_ctx_tiles, nt)
        d = d_ref[...].astype(F32)
        pr = jnp.where(first, 0.0, dp_ref[15:16, :].astype(F32))
        nx = jnp.where(last, 0.0, dn_ref[0:1, :].astype(F32))
        dm1, dp1 = _shift_rows(d, pr, nx)
        o_ref[...] = (w_ref[0:1, :] * dp1 + w_ref[1:2, :] * d + w_ref[2:3, :] * dm1).astype(BF16)

    return pl.pallas_call(
        body, name="qkconv_bwd_b", grid=(nt,),
        in_specs=[pl.BlockSpec(memory_space=pl.ANY), _rb(TR, w2), prev_s, next_s, _bc(8, w2)],
        out_specs=_rb(TR, w2, CB_Q // 2),
        out_shape=jax.ShapeDtypeStruct(dz.shape, BF16),
        input_output_aliases={0: 0},
        compiler_params=_cparams(("parallel",)),
    )(dz, dc, dc, dc, conv_w)


def add_into_dz(dz, a, b, col):
    tp = a.shape[0]

    def body(dz_in_ref, a_ref, b_ref, o_ref):
        del dz_in_ref
        o_ref[...] = (a_ref[...].astype(F32) + b_ref[...].astype(F32)).astype(BF16)

    return pl.pallas_call(
        body, name="add_into_dz", grid=(tp // TR,),
        in_specs=[pl.BlockSpec(memory_space=pl.ANY), _rb(TR, D), _rb(TR, D)],
        out_specs=_rb(TR, D, col),
        out_shape=jax.ShapeDtypeStruct(dz.shape, BF16),
        input_output_aliases={0: 0},
        compiler_params=_cparams(("parallel",)),
    )(dz, a, b)


def _chunk_maps(nc, ncc):
    def fwd(t):
        return t

    def bwd(t):
        return jnp.where(t < ncc, ncc - 1 - t, nc - 1 + ncc - t)

    return fwd, bwd


def _mlstm_chunk(d, h, gc, gr, q_ref, k_ref, v_ref, cp, npv, m_prev, mask):
    ic, bcol = 8 * d + h, 8 * d + 4 + h
    i_col, b_col = gc[:, ic:ic + 1], gc[:, bcol:bcol + 1]
    i_row, b_row = gr[ic:ic + 1, :], gr[bcol:bcol + 1, :]
    g = b_row[:, LC - 1:LC] if d == 0 else b_row[:, 0:1]
    a_row = g - b_row + i_row
    m_loc = jnp.max(a_row, axis=1, keepdims=True)
    dmat = jnp.where(mask, b_col - b_row + i_row, -jnp.inf)
    inter = b_col + m_prev
    m_row = jnp.maximum(inter, jnp.max(dmat, axis=1, keepdims=True))
    e = jnp.exp(dmat - m_row)
    w = jnp.exp(inter - m_row)
    hs = slice(h * DH, (h + 1) * DH)
    qh, kh, vh = q_ref[:, hs], k_ref[:, hs], v_ref[:, hs]
    p = _dot_nt(qh, kh)
    s = p * e
    cpb = cp.astype(BF16)
    qc = _dot(qh, cpb)
    num = _dot(s, vh) + w * qc
    qn = jnp.sum(qh.astype(F32) * npv, axis=1, keepdims=True)
    den = jnp.sum(s, axis=1, keepdims=True) + w * qn
    thr = jnp.exp(-m_row)
    m_new = jnp.maximum(g + m_prev, m_loc)
    a_old = jnp.exp(g + m_prev - m_new)
    a_col = g - b_col + i_col
    return dict(qh=qh, kh=kh, vh=vh, e=e, w=w, s=s, cpb=cpb, qc=qc, num=num, qn=qn, den=den, thr=thr,
                m_loc=m_loc, m_new=m_new, a_old=a_old, a_col=a_col, hs=hs)


def mlstm_fwd(qa, ka, zmain, gcol, grow, ncc):
    tp = qa.shape[0]
    nc = tp // LC
    cf, cb = _chunk_maps(nc, ncc)

    def body(qf, kf, vf, gcf, grf, qb, kb, vb, gcb, grb,
             hf_o, hb_o, cf_o, cb_o, nf_o, nb_o, mf_o, mb_o, c_sc, n_sc, m_sc):
        t = pl.program_id(0)

        @pl.when(t == 0)
        def _():
            c_sc[...] = jnp.zeros_like(c_sc)
            n_sc[...] = jnp.zeros_like(n_sc)
            m_sc[...] = jnp.full(m_sc.shape, M_INIT, F32)

        row = lax.broadcasted_iota(jnp.int32, (LC, LC), 0)
        col = lax.broadcasted_iota(jnp.int32, (LC, LC), 1)
        dirs = ((qf, kf, vf, gcf, grf, hf_o, cf_o, nf_o, mf_o), (qb, kb, vb, gcb, grb, hb_o, cb_o, nb_o, mb_o))
        for d, (q_ref, k_ref, v_ref, gc_ref, gr_ref, h_o, c_o, n_o, m_o) in enumerate(dirs):
            mask = (col <= row) if d == 0 else (col >= row)
            gc = gc_ref[...]
            gr = gr_ref[0]
            for h in range(NH):
                idx = d * NH + h
                cp = c_sc[idx]
                npv = n_sc[idx]
                m_full = m_sc[idx]
                m_prev = m_full[:, 0:1]
                r = _mlstm_chunk(d, h, gc, gr, q_ref, k_ref, v_ref, cp, npv, m_prev, mask)
                hs = r["hs"]
                h_o[:, hs] = (r["num"] / jnp.maximum(jnp.abs(r["den"]), r["thr"])).astype(BF16)
                c_o[0, hs, :] = r["cpb"]
                n_o[0, h:h + 1, :] = npv
                m_o[0, h:h + 1, :] = m_full
                a_new = jnp.exp(r["m_loc"] - r["m_new"])
                kw = r["kh"].astype(F32) * jnp.exp(r["a_col"] - r["m_loc"])
                kv = _dot_tn_mxu(kw, r["vh"])
                kn = jnp.sum(kw, axis=0, keepdims=True)
                c_sc[idx] = r["a_old"] * cp + a_new * kv
                n_sc[idx] = r["a_old"] * npv + a_new * kn
                m_sc[idx] = jnp.broadcast_to(r["m_new"], (1, LANE))

    def dspecs(cm):
        return [pl.BlockSpec((LC, D), lambda t: (cm(t), 0)),
                pl.BlockSpec((LC, D), lambda t: (cm(t), 0)),
                pl.BlockSpec((LC, D), lambda t: (cm(t), CB_V)),
                pl.BlockSpec((LC, LANE), lambda t: (cm(t), 0)),
                pl.BlockSpec((1, 16, LC), lambda t: (cm(t), 0, 0))]

    def ospec(cm, shp):
        return pl.BlockSpec((1,) + shp, lambda t: (cm(t), 0, 0))

    return pl.pallas_call(
        body, name="mlstm_fwd", grid=(nc,),
        in_specs=dspecs(cf) + dspecs(cb),
        out_specs=[pl.BlockSpec((LC, D), lambda t: (cf(t), 0)), pl.BlockSpec((LC, D), lambda t: (cb(t), 0)),
                   ospec(cf, (D, DH)), ospec(cb, (D, DH)), ospec(cf, (NH, DH)), ospec(cb, (NH, DH)),
                   ospec(cf, (NH, LANE)), ospec(cb, (NH, LANE))],
        out_shape=[jax.ShapeDtypeStruct((tp, D), BF16), jax.ShapeDtypeStruct((tp, D), BF16),
                   jax.ShapeDtypeStruct((nc, D, DH), BF16), jax.ShapeDtypeStruct((nc, D, DH), BF16),
                   jax.ShapeDtypeStruct((nc, NH, DH), F32), jax.ShapeDtypeStruct((nc, NH, DH), F32),
                   jax.ShapeDtypeStruct((nc, NH, LANE), F32), jax.ShapeDtypeStruct((nc, NH, LANE), F32)],
        scratch_shapes=[pltpu.VMEM((2 * NH, DH, DH), F32), pltpu.VMEM((2 * NH, 1, DH), F32),
                        pltpu.VMEM((2 * NH, 1, LANE), F32)],
        compiler_params=_cparams(("arbitrary",)),
    )(qa, ka, zmain, gcol, grow, qa, ka, zmain, gcol, grow)


def mlstm_bwd(qa, ka, zmain, gcol, grow, states, dhm, ncc):
    tp = qa.shape[0]
    nc = tp // LC
    cf0, cb0 = _chunk_maps(nc, ncc)
    cf = lambda t: cf0(nc - 1 - t)
    cb = lambda t: cb0(nc - 1 - t)
    csf, csb, nsf, nsb, msf, msb = states

    def body(qf, kf, vf, gcf, grf, cpf, npf, mpf, dhf, qb, kb, vb, gcb, grb, cpb_, npb, mpb, dhb,
             dqf_o, dkf_o, dvf_o, colf_o, rowf_o, dqb_o, dkb_o, dvb_o, colb_o, rowb_o, dc_sc, dn_sc):
        t = pl.program_id(0)

        @pl.when(t == 0)
        def _():
            dc_sc[...] = jnp.zeros_like(dc_sc)
            dn_sc[...] = jnp.zeros_like(dn_sc)

        row = lax.broadcasted_iota(jnp.int32, (LC, LC), 0)
        col = lax.broadcasted_iota(jnp.int32, (LC, LC), 1)
        dirs = ((qf, kf, vf, gcf, grf, cpf, npf, mpf, dhf, dqf_o, dkf_o, dvf_o, colf_o, rowf_o, cf),
                (qb, kb, vb, gcb, grb, cpb_, npb, mpb, dhb, dqb_o, dkb_o, dvb_o, colb_o, rowb_o, cb))
        for d, (q_ref, k_ref, v_ref, gc_ref, gr_ref, cp_ref, np_ref, mp_ref, dh_ref,
                dq_o, dk_o, dv_o, col_o, row_o, cm) in enumerate(dirs):
            mask = (col <= row) if d == 0 else (col >= row)
            live = jnp.where(cm(t) >= ncc, 1.0, 0.0).astype(F32)
            gc = gc_ref[...]
            gr = gr_ref[0]
            col_o[...] = jnp.zeros_like(col_o)
            row_o[...] = jnp.zeros_like(row_o)
            for h in range(NH):
                idx = d * NH + h
                hs = slice(h * DH, (h + 1) * DH)
                cp = cp_ref[0, hs, :]
                npv = np_ref[0, h:h + 1, :]
                m_prev = mp_ref[0, h:h + 1, 0:1]
                r = _mlstm_chunk(d, h, gc, gr, q_ref, k_ref, v_ref, cp, npv, m_prev, mask)
                qh, kh, vh, e, w, s = r["qh"], r["kh"], r["vh"], r["e"], r["w"], r["s"]
                qf32, kf32 = qh.astype(F32), kh.astype(F32)
                den, thr = r["den"], r["thr"]
                rden = 1.0 / jnp.maximum(jnp.abs(den), thr)
                hh = r["num"] * rden
                dh = dh_ref[:, hs].astype(F32) * live
                dnum = dh * rden
                sgn = jnp.where(jnp.abs(den) > thr, jnp.sign(den), 0.0)
                dden = -jnp.sum(dh * hh, axis=1, keepdims=True) * rden * sgn
                ds = _dot_nt(dnum, vh) + dden
                dp = ds * e
                gm = ds * s
                rowsum = jnp.sum(gm, axis=1, keepdims=True)
                colsum = jnp.sum(gm, axis=0, keepdims=True)
                dq = _dot(dp, kh) + w * (_dot_nt(dnum, r["cpb"]) + dden * npv)
                dcs = dc_sc[idx]
                dns = dn_sc[idx]
                kfac = jnp.exp(r["a_col"] - r["m_new"])
                vdc = _dot_nt(vh, dcs)
                dk = _dot_tn(dp, qh) + kfac * (vdc + dns)
                dv = _dot_tn(s, dnum) + kfac * _dot(kh, dcs)
                beta = w * (jnp.sum(dnum * r["qc"], axis=1, keepdims=True) + dden * r["qn"])
                alpha = kfac * (jnp.sum(kf32 * vdc, axis=1, keepdims=True) + jnp.sum(kf32 * dns, axis=1, keepdims=True))
                dq_o[:, hs] = dq.astype(BF16)
                dk_o[:, hs] = dk.astype(BF16)
                dv_o[:, hs] = dv.astype(BF16)
                cpf = r["cpb"].astype(F32)
                inner = (jnp.sum(jnp.sum(dcs * cpf, axis=1, keepdims=True), axis=0, keepdims=True)
                         + jnp.sum(dns * npv, axis=1, keepdims=True))
                gam = jnp.sum(alpha, axis=0, keepdims=True) + r["a_old"] * inner
                lo = 8 * d + h
                col_o[:, lo:lo + 1] = alpha
                col_o[:, lo + 4:lo + 5] = rowsum + beta - alpha
                col_o[:, lo + 36:lo + 37] = jnp.broadcast_to(gam, (LC, 1))
                row_o[0, h:h + 1, :] = colsum
                wq = qf32 * w
                dc_sc[idx] = r["a_old"] * dcs + _dot_tn(wq, dnum)
                dn_sc[idx] = r["a_old"] * dns + jnp.sum(wq * dden, axis=0, keepdims=True)

    def dspecs(cm):
        return [pl.BlockSpec((LC, D), lambda t: (cm(t), 0)),
                pl.BlockSpec((LC, D), lambda t: (cm(t), 0)),
                pl.BlockSpec((LC, D), lambda t: (cm(t), CB_V)),
                pl.BlockSpec((LC, LANE), lambda t: (cm(t), 0)),
                pl.BlockSpec((1, 16, LC), lambda t: (cm(t), 0, 0)),
                pl.BlockSpec((1, D, DH), lambda t: (cm(t), 0, 0)),
                pl.BlockSpec((1, NH, DH), lambda t: (cm(t), 0, 0)),
                pl.BlockSpec((1, NH, LANE), lambda t: (cm(t), 0, 0)),
                pl.BlockSpec((LC, D), lambda t: (jnp.maximum(cm(t) - ncc, 0), 0))]

    def ospecs(cm):
        return [pl.BlockSpec((LC, D), lambda t: (cm(t), 0)),
                pl.BlockSpec((LC, D), lambda t: (cm(t), 0)),
                pl.BlockSpec((LC, D), lambda t: (cm(t), 0)),
                pl.BlockSpec((LC, LANE), lambda t: (cm(t), 0)),
                pl.BlockSpec((1, 8, LC), lambda t: (cm(t), 0, 0))]

    oshape = [jax.ShapeDtypeStruct((tp, D), BF16)] * 3 + [jax.ShapeDtypeStruct((tp, LANE), F32),
                                                        jax.ShapeDtypeStruct((nc, 8, LC), F32)]
    return pl.pallas_call(
        body, name="mlstm_bwd", grid=(nc,),
        in_specs=dspecs(cf) + dspecs(cb),
        out_specs=ospecs(cf) + ospecs(cb),
        out_shape=oshape + oshape,
        scratch_shapes=[pltpu.VMEM((2 * NH, DH, DH), F32), pltpu.VMEM((2 * NH, 1, DH), F32)],
        compiler_params=_cparams(("arbitrary",)),
    )(qa, ka, zmain, gcol, grow, csf, nsf, msf, dhm, qa, ka, zmain, gcol, grow, csb, nsb, msb, dhm)


def gates_bwd(colf, colb, cs, zgb):
    tp = colf.shape[0]

    def body(cf_ref, cb_ref, cs_ref, zgb_ref, o_ref, db_ref):
        i = pl.program_id(0)

        @pl.when(i == 0)
        def _():
            db_ref[...] = jnp.zeros_like(db_ref)

        lane = lax.broadcasted_iota(jnp.int32, (TR, LANE), 1)
        i_l = jnp.logical_or(lane < 4, jnp.logical_and(lane >= 8, lane < 12))
        f_l = jnp.logical_or(jnp.logical_and(lane >= 4, lane < 8), jnp.logical_and(lane >= 12, lane < 16))
        cv = cf_ref[...] + cb_ref[...]
        csv = cs_ref[...]
        gam = pltpu.roll(cv, LANE - 32, 1)
        dbh = jnp.where(f_l, cv - csv, 0.0)
        rr = lax.broadcasted_iota(jnp.int32, (TR, TR), 0)
        cc = lax.broadcasted_iota(jnp.int32, (TR, TR), 1)
        same = (rr // LC) == (cc // LC)
        low = jnp.where(jnp.logical_and(same, cc <= rr), 1.0, 0.0).astype(BF16)
        upp = jnp.where(jnp.logical_and(same, cc >= rr), 1.0, 0.0).astype(BF16)
        dlogf = jnp.where(lane < 8, _exact_dot(upp, dbh), _exact_dot(low, dbh)) + gam
        out = jnp.where(i_l, csv + cv, 0.0) + jnp.where(f_l, dlogf * _sigmoid(-zgb_ref[...]), 0.0)
        o_ref[...] = out
        db_ref[...] += jnp.sum(out, axis=0, keepdims=True)

    spec = _rb(TR, LANE)
    return pl.pallas_call(
        body, name="gates_bwd", grid=(tp // TR,),
        in_specs=[spec] * 4,
        out_specs=[spec, _bc(1, LANE)],
        out_shape=[jax.ShapeDtypeStruct((tp, LANE), F32), jax.ShapeDtypeStruct((1, LANE), F32)],
        compiler_params=_cparams(("arbitrary",)),
    )(colf, colb, cs, zgb)


def _head_norm(hm, gh):
    xs, rs = [], []
    for h in range(NH):
        seg = hm[:, h * DH:(h + 1) * DH]
        r = lax.rsqrt(jnp.mean(seg * seg, axis=-1, keepdims=True) + EPS)
        xs.append(seg * r)
        rs.append(r)
    xh = jnp.concatenate(xs, axis=1)
    return xh, rs, xh * gh


def _sgu_norm(zvg, ln_g, ln_b):
    vg = _gelu(zvg)
    mu = jnp.mean(vg, axis=-1, keepdims=True)
    vc = vg - mu
    rstd = lax.rsqrt(jnp.mean(vc * vc, axis=-1, keepdims=True) + EPS)
    vhat = vc * rstd
    return vhat, rstd, vhat * ln_g + ln_b


def _sgu_mix(vn, ws_ref, bs_ref):
    rows = []
    for c in range(TR // SCH):
        cols = []
        for g in range(SG):
            blk = vn[c * SCH:(c + 1) * SCH, g * SGD:(g + 1) * SGD]
            cols.append(_dot(ws_ref[g * SCH:(g + 1) * SCH, :], blk) + bs_ref[:, g:g + 1])
        rows.append(jnp.concatenate(cols, axis=1))
    return jnp.concatenate(rows, axis=0)


def mixer_fwd(hf, hb, zmain, gh, ln_g, ln_b, ws, bs_t, n_ctx_tiles):
    t = hf.shape[0] - n_ctx_tiles * TR
    off = n_ctx_tiles

    def body(hf_ref, hb_ref, zo_ref, zu_ref, zv_ref, gh_ref, lg_ref, lb_ref, ws_ref, bs_ref, o_ref):
        hm = hf_ref[...].astype(F32) + hb_ref[...].astype(F32)
        _, _, hn = _head_norm(hm, gh_ref[...])
        o_ref[0] = (_sigmoid(zo_ref[...].astype(F32)) * hn).astype(BF16)
        _, _, vn = _sgu_norm(zv_ref[...].astype(F32), lg_ref[...], lb_ref[...])
        mixed = _sgu_mix(vn, ws_ref, bs_ref)
        o_ref[1] = (_gelu(zu_ref[...].astype(F32)) * mixed).astype(BF16)

    return pl.pallas_call(
        body, name="mixer_fwd", grid=(t // TR,),
        in_specs=[_rb(TR, D, 0, off), _rb(TR, D, 0, off), _rb(TR, D, CB_O, off), _rb(TR, D, CB_U, off),
                  _rb(TR, D, CB_VG, off), _bc(1, D), _bc(1, D), _bc(1, D), _bc(SG * SCH, SCH), _bc(SCH, LANE)],
        out_specs=pl.BlockSpec((2, TR, D), lambda i: (0, i, 0)),
        out_shape=jax.ShapeDtypeStruct((2, t, D), BF16),
        compiler_params=_cparams(("parallel",)),
    )(hf, hb, zmain, zmain, zmain, gh, ln_g, ln_b, ws, bs_t)


def merge_fwd(zmain, pp, n_ctx_tiles):
    t = pp.shape[1]
    off = n_ctx_tiles

    def body(zgm_ref, zgg_ref, pp_ref, o_ref):
        y = (_sigmoid(zgm_ref[...].astype(F32)) * pp_ref[0].astype(F32)
             + _sigmoid(zgg_ref[...].astype(F32)) * pp_ref[1].astype(F32))
        o_ref[...] = y.astype(BF16)

    return pl.pallas_call(
        body, name="merge_fwd", grid=(t // TR,),
        in_specs=[_rb(TR, D, CB_GM, off), _rb(TR, D, CB_GG, off), pl.BlockSpec((2, TR, D), lambda i: (0, i, 0))],
        out_specs=_rb(TR, D),
        out_shape=jax.ShapeDtypeStruct((t, D), BF16),
        compiler_params=_cparams(("parallel",)),
    )(zmain, zmain, pp)


def merge_bwd(zmain, pp, dy, tp, n_ctx_tiles):
    t = dy.shape[0]
    nt = tp // TR
    xrow = lambda i: jnp.maximum(i - n_ctx_tiles, 0)

    def body(zg_ref, pp_ref, dy_ref, dpp_ref, dz_ref):
        i = pl.program_id(1)
        zg = zg_ref[...].astype(F32)
        sg = _sigmoid(zg)
        dyv = dy_ref[...].astype(F32)
        dpp_ref[0] = (dyv * sg).astype(BF16)
        dzv = dyv * pp_ref[0].astype(F32) * sg * (1.0 - sg)
        dz_ref[...] = jnp.where(i >= n_ctx_tiles, dzv, 0.0).astype(BF16)

    return pl.pallas_call(
        body, name="merge_bwd", grid=(2, nt),
        in_specs=[pl.BlockSpec((TR, D), lambda j, i: (i, CB_GM + j)),
                  pl.BlockSpec((1, TR, D), lambda j, i: (j, xrow(i), 0)),
                  pl.BlockSpec((TR, D), lambda j, i: (xrow(i), 0))],
        out_specs=[pl.BlockSpec((1, TR, D), lambda j, i: (j, xrow(i), 0)),
                   pl.BlockSpec((TR, D), lambda j, i: (i, CB_GM + j))],
        out_shape=[jax.ShapeDtypeStruct((2, t, D), BF16), jax.ShapeDtypeStruct((tp, 8 * D), BF16)],
        compiler_params=_cparams(("arbitrary", "arbitrary")),
    )(zmain, pp, dy)


def mixer_bwd(dz, hf, hb, zmain, dyms, gh, ln_g, ln_b, ws, bs_t, n_ctx_tiles):
    tp = hf.shape[0]
    t = tp - n_ctx_tiles * TR
    nt = tp // TR
    xrow = lambda i: jnp.maximum(i - n_ctx_tiles, 0)

    def body(dz_in_ref, hf_ref, hb_ref, zo_ref, zu_ref, zv_ref, dy_ref, gh_ref, lg_ref, lb_ref, ws_ref, bs_ref,
             dz_ref, dhm_ref, dgh_ref, dlg_ref, dlb_ref, dws_ref, dbs_ref):
        del dz_in_ref
        i = pl.program_id(0)

        @pl.when(i == 0)
        def _():
            for ref in (dgh_ref, dlg_ref, dlb_ref, dws_ref, dbs_ref):
                ref[...] = jnp.zeros_like(ref)

        @pl.when(i < n_ctx_tiles)
        def _():
            dz_ref[...] = jnp.zeros_like(dz_ref)

        @pl.when(i >= n_ctx_tiles)
        def _():
            gh_v = gh_ref[...]
            hm = hf_ref[...].astype(F32) + hb_ref[...].astype(F32)
            xh, rs, hn = _head_norm(hm, gh_v)
            zo = zo_ref[...].astype(F32)
            so = _sigmoid(zo)
            dym = dy_ref[0].astype(F32)
            d_zo = dym * hn * so * (1.0 - so)
            d_hn = dym * so
            dgh_ref[...] += jnp.sum(d_hn * xh, axis=0, keepdims=True)
            d_xh = d_hn * gh_v
            segs = []
            for h in range(NH):
                hs = slice(h * DH, (h + 1) * DH)
                dx, xs = d_xh[:, hs], xh[:, hs]
                segs.append(rs[h] * (dx - xs * jnp.mean(dx * xs, axis=-1, keepdims=True)))
            dhm_ref[...] = jnp.concatenate(segs, axis=1).astype(BF16)
            zu = zu_ref[...].astype(F32)
            zv = zv_ref[...].astype(F32)
            lg = lg_ref[...]
            vhat, rstd, vn = _sgu_norm(zv, lg, lb_ref[...])
            mixed = _sgu_mix(vn, ws_ref, bs_ref)
            dys = dy_ref[1].astype(F32)
            d_zu = dys * mixed * _dgelu(zu)
            d_mixed = dys * _gelu(zu)
            rows = []
            for c in range(TR // SCH):
                cols = []
                for g in range(SG):
                    rsl, csl = slice(c * SCH, (c + 1) * SCH), slice(g * SGD, (g + 1) * SGD)
                    dm = d_mixed[rsl, csl]
                    cols.append(_dot_tn(ws_ref[g * SCH:(g + 1) * SCH, :], dm))
                    dws_ref[g * SCH:(g + 1) * SCH, :] += _dot_nt(dm, vn[rsl, csl])
                    dbs_ref[:, g:g + 1] += jnp.sum(dm, axis=1, keepdims=True)
                rows.append(jnp.concatenate(cols, axis=1))
            d_vn = jnp.concatenate(rows, axis=0)
            dlg_ref[...] += jnp.sum(d_vn * vhat, axis=0, keepdims=True)
            dlb_ref[...] += jnp.sum(d_vn, axis=0, keepdims=True)
            d_vhat = d_vn * lg
            d_vg = rstd * (d_vhat - jnp.mean(d_vhat, axis=-1, keepdims=True)
                           - vhat * jnp.mean(d_vhat * vhat, axis=-1, keepdims=True))
            d_zv = d_vg * _dgelu(zv)
            dz_ref[...] = jnp.concatenate([d_zo, d_zu, d_zv], axis=1).astype(BF16)

    return pl.pallas_call(
        body, name="mixer_bwd", grid=(nt,),
        in_specs=[pl.BlockSpec(memory_space=pl.ANY), _rb(TR, D), _rb(TR, D), _rb(TR, D, CB_O), _rb(TR, D, CB_U),
                  _rb(TR, D, CB_VG), pl.BlockSpec((2, TR, D), lambda i: (0, xrow(i), 0)),
                  _bc(1, D), _bc(1, D), _bc(1, D), _bc(SG * SCH, SCH), _bc(SCH, LANE)],
        out_specs=[_rb(TR, 3 * D), pl.BlockSpec((TR, D), lambda i: (xrow(i), 0)),
                   _bc(1, D), _bc(1, D), _bc(1, D), _bc(SG * SCH, SCH), _bc(SCH, LANE)],
        out_shape=[jax.ShapeDtypeStruct(dz.shape, BF16), jax.ShapeDtypeStruct((t, D), BF16),
                   jax.ShapeDtypeStruct((1, D), F32), jax.ShapeDtypeStruct((1, D), F32),
                   jax.ShapeDtypeStruct((1, D), F32), jax.ShapeDtypeStruct((SG * SCH, SCH), F32),
                   jax.ShapeDtypeStruct((SCH, LANE), F32)],
        input_output_aliases={0: 0},
        compiler_params=_cparams(("arbitrary",)),
    )(dz, hf, hb, zmain, zmain, zmain, dyms, gh, ln_g, ln_b, ws, bs_t)


FCB = DFF // 2
TF = 512


def _ffn_halo(col, t):
    per = TF // GW
    last = t // GW - 1
    prev = pl.BlockSpec((GW, FCB), lambda i, j: (jnp.maximum(i * per - 1, 0), col(j)))
    nxt = pl.BlockSpec((GW, FCB), lambda i, j: (jnp.minimum((i + 1) * per, last), col(j)))
    return prev, nxt


def _conv_taps(ext):
    n = ext.shape[0]
    colid = lax.broadcasted_iota(jnp.int32, (n, 1), 0) % GW
    left = pltpu.roll(jnp.where(colid != GW - 1, ext, 0.0), 1, 0)
    right = pltpu.roll(jnp.where(colid != 0, ext, 0.0), n - 1, 0)
    views = (left, ext, right)
    return {(ky, kx): views[kx][GW * ky:GW * ky + TF] for ky in range(3) for kx in range(3)}


def _ext(c_ref, p_ref, n_ref, i, nt):
    pr = jnp.where(i == 0, 0.0, p_ref[...].astype(F32))
    nx = jnp.where(i == nt - 1, 0.0, n_ref[...].astype(F32))
    return jnp.concatenate([pr, c_ref[...].astype(F32), nx], axis=0)


def ffn_act_fwd(up, wc):
    t = up.shape[0]
    nt = t // TF
    prev_s, next_s = _ffn_halo(lambda j: j, t)

    def body(a_ref, ap_ref, an_ref, b_ref, w_ref, o_ref, ac_ref):
        i = pl.program_id(0)
        taps = _conv_taps(_ext(a_ref, ap_ref, an_ref, i, nt))
        ac = sum(w_ref[3 * ky + kx:3 * ky + kx + 1, :] * taps[(ky, kx)] for ky in range(3) for kx in range(3))
        ac_ref[...] = ac.astype(BF16)
        o_ref[...] = (_silu(ac) * b_ref[...].astype(F32)).astype(BF16)

    spec = pl.BlockSpec((TF, FCB), lambda i, j: (i, j))
    return pl.pallas_call(
        body, name="ffn_act_fwd", grid=(nt, 2),
        in_specs=[spec, prev_s, next_s,
                  pl.BlockSpec((TF, FCB), lambda i, j: (i, 2 + j)), pl.BlockSpec((16, FCB), lambda i, j: (0, j))],
        out_specs=[spec, spec],
        out_shape=[jax.ShapeDtypeStruct((t, DFF), BF16), jax.ShapeDtypeStruct((t, DFF), BF16)],
        compiler_params=_cparams(("parallel", "parallel")),
    )(up, up, up, up, wc)


def ffn_act_bwd(up, ac, dact):
    t = up.shape[0]
    nt = t // TF

    def body(b_ref, ac_ref, da_ref, dup_ref, dac_ref):
        acv = ac_ref[...].astype(F32)
        da = da_ref[...].astype(F32)
        s = _sigmoid(acv)
        dup_ref[...] = (da * acv * s).astype(BF16)
        dac_ref[...] = (da * b_ref[...].astype(F32) * s * (1.0 + acv * (1.0 - s))).astype(BF16)

    spec = pl.BlockSpec((TF, FCB), lambda i, j: (i, j))
    bspec = pl.BlockSpec((TF, FCB), lambda i, j: (i, 2 + j))
    return pl.pallas_call(
        body, name="ffn_act_bwd", grid=(nt, 2),
        in_specs=[bspec, spec, spec],
        out_specs=[bspec, spec],
        out_shape=[jax.ShapeDtypeStruct((t, 2 * DFF), BF16), jax.ShapeDtypeStruct((t, DFF), BF16)],
        compiler_params=_cparams(("parallel", "parallel")),
    )(up, ac, dact)


def ffn_conv_bwd(dup, up, dac, wc):
    t = up.shape[0]
    nt = t // TF
    prev_g, next_g = _ffn_halo(lambda j: j, t)

    def body(dup_in_ref, a_ref, g_ref, gp_ref, gn_ref, w_ref, o_ref, dw_ref):
        del dup_in_ref
        i = pl.program_id(1)

        @pl.when(i == 0)
        def _():
            dw_ref[...] = jnp.zeros_like(dw_ref)

        gtaps = _conv_taps(_ext(g_ref, gp_ref, gn_ref, i, nt))
        a = a_ref[...].astype(F32)
        acc = None
        for ky in range(3):
            for kx in range(3):
                k = 3 * ky + kx
                kf = 3 * (2 - ky) + (2 - kx)
                tap = gtaps[(ky, kx)]
                term = w_ref[kf:kf + 1, :] * tap
                acc = term if acc is None else acc + term
                dw_ref[kf:kf + 1, :] += jnp.sum(a * tap, axis=0, keepdims=True)
        o_ref[...] = acc.astype(BF16)

    sw = lambda s: pl.BlockSpec(s.block_shape, lambda j, i, f=s.index_map: f(i, j))
    spec = pl.BlockSpec((TF, FCB), lambda j, i: (i, j))
    return pl.pallas_call(
        body, name="ffn_conv_bwd", grid=(2, nt),
        in_specs=[pl.BlockSpec(memory_space=pl.ANY), spec, spec, sw(prev_g), sw(next_g),
                  pl.BlockSpec((16, FCB), lambda j, i: (0, j))],
        out_specs=[spec, pl.BlockSpec((16, FCB), lambda j, i: (0, j))],
        out_shape=[jax.ShapeDtypeStruct(dup.shape, BF16), jax.ShapeDtypeStruct((16, DFF), F32)],
        input_output_aliases={0: 0},
        compiler_params=_cparams(("arbitrary", "arbitrary")),
    )(dup, up, dac, dac, dac, wc)


def head_fwd_bwd(h1, f, target, gfin, tab):
    t = h1.shape[0]

    def body(h1_ref, f_ref, t_ref, g_ref, tab_ref, dh2_ref, df_ref, acc_ref):
        i = pl.program_id(0)

        @pl.when(i == 0)
        def _():
            acc_ref[...] = jnp.zeros_like(acc_ref)

        gate = tab_ref[0:1, :]
        fv = f_ref[...].astype(F32)
        h2 = h1_ref[...] + gate * fv
        r = lax.rsqrt(jnp.mean(h2 * h2, axis=-1, keepdims=True) + EPS)
        xh = h2 * r
        gv = g_ref[...]
        err = xh * gv - t_ref[...]
        acc_ref[0:1, :] += jnp.sum(0.5 * jnp.mean(err * err, axis=-1, keepdims=True), axis=0, keepdims=True)
        dy = err * (1.0 / D)
        acc_ref[1:2, :] += jnp.sum(dy * xh, axis=0, keepdims=True)
        dxh = dy * gv
        dh2 = r * (dxh - xh * jnp.mean(dxh * xh, axis=-1, keepdims=True))
        dh2_ref[...] = dh2
        acc_ref[2:3, :] += jnp.sum(dh2 * fv, axis=0, keepdims=True)
        df_ref[...] = (dh2 * gate).astype(BF16)

    return pl.pallas_call(
        body, name="head_fwd_bwd", grid=(t // TR,),
        in_specs=[_rb(TR, D), _rb(TR, D), _rb(TR, D), _bc(1, D), _bc(8, D)],
        out_specs=[_rb(TR, D), _rb(TR, D), _bc(8, D)],
        out_shape=[jax.ShapeDtypeStruct((t, D), F32), jax.ShapeDtypeStruct((t, D), BF16),
                   jax.ShapeDtypeStruct((8, D), F32)],
        compiler_params=_cparams(("arbitrary",)),
    )(h1, f, target, gfin, tab)


def norm2_bwd(h1, dhn2, dh2, out, g, tab):
    t = h1.shape[0]

    def body(h1_ref, dhn_ref, dh2_ref, out_ref, g_ref, tab_ref, dh1_ref, dout_ref, acc_ref):
        i = pl.program_id(0)

        @pl.when(i == 0)
        def _():
            acc_ref[...] = jnp.zeros_like(acc_ref)

        h1v = h1_ref[...]
        r = lax.rsqrt(jnp.mean(h1v * h1v, axis=-1, keepdims=True) + EPS)
        xh = h1v * r
        gv = g_ref[...]
        dhn = dhn_ref[...].astype(F32)
        acc_ref[0:1, :] += jnp.sum(dhn, axis=0, keepdims=True)
        acc_ref[1:2, :] += jnp.sum(dhn * xh * gv, axis=0, keepdims=True)
        dn = dhn * (1.0 + tab_ref[2:3, :])
        acc_ref[2:3, :] += jnp.sum(dn * xh, axis=0, keepdims=True)
        dxh = dn * gv
        dh1 = dh2_ref[...] + r * (dxh - xh * jnp.mean(dxh * xh, axis=-1, keepdims=True))
        dh1_ref[...] = dh1
        acc_ref[3:4, :] += jnp.sum(dh1 * out_ref[...].astype(F32), axis=0, keepdims=True)
        dout_ref[...] = (dh1 * tab_ref[0:1, :]).astype(BF16)

    return pl.pallas_call(
        body, name="norm2_bwd", grid=(t // TR,),
        in_specs=[_rb(TR, D), _rb(TR, D), _rb(TR, D), _rb(TR, D), _bc(1, D), _bc(8, D)],
        out_specs=[_rb(TR, D), _rb(TR, D), _bc(8, D)],
        out_shape=[jax.ShapeDtypeStruct((t, D), F32), jax.ShapeDtypeStruct((t, D), BF16),
                   jax.ShapeDtypeStruct((8, D), F32)],
        compiler_params=_cparams(("arbitrary",)),
    )(h1, dhn2, dh2, out, g, tab)


def norm1_bwd(ctx, x, da, db, dh1, g, tab, n_ctx_tiles):
    t = x.shape[0]
    tp = t + ctx.shape[0]
    xrow = lambda i: jnp.maximum(i - n_ctx_tiles, 0)

    def body(c_ref, x_ref, da_ref, db_ref, dh1_ref, g_ref, tab_ref, dx_ref, acc_ref):
        i = pl.program_id(0)

        @pl.when(i == 0)
        def _():
            acc_ref[...] = jnp.zeros_like(acc_ref)

        is_ctx = i < n_ctx_tiles
        x = jnp.where(is_ctx, c_ref[...], x_ref[...])
        r = lax.rsqrt(jnp.mean(x * x, axis=-1, keepdims=True) + EPS)
        xh = x * r
        gv = g_ref[...]
        dhn = da_ref[...].astype(F32) + db_ref[...]
        s_shift = jnp.sum(dhn, axis=0, keepdims=True)
        s_scale = jnp.sum(dhn * xh * gv, axis=0, keepdims=True)

        @pl.when(is_ctx)
        def _():
            acc_ref[0:1, :] += s_shift
            acc_ref[1:2, :] += s_scale

        @pl.when(jnp.logical_not(is_ctx))
        def _():
            acc_ref[2:3, :] += s_shift
            acc_ref[3:4, :] += s_scale

        acc_ref[5:6, :] += s_shift
        acc_ref[6:7, :] += s_scale
        sc = jnp.where(is_ctx, tab_ref[1:2, :], tab_ref[3:4, :])
        dn = dhn * (1.0 + sc)
        acc_ref[4:5, :] += jnp.sum(dn * xh, axis=0, keepdims=True)
        dxh = dn * gv
        dx_ref[...] = dh1_ref[...] + r * (dxh - xh * jnp.mean(dxh * xh, axis=-1, keepdims=True))

    return pl.pallas_call(
        body, name="norm1_bwd", grid=(tp // TR,),
        in_specs=_ctx_x_specs(n_ctx_tiles) + [_rb(TR, D), _rb(TR, D), pl.BlockSpec((TR, D), lambda i: (xrow(i), 0)),
                                              _bc(1, D), _bc(8, D)],
        out_specs=[pl.BlockSpec((TR, D), lambda i: (xrow(i), 0)), _bc(8, D)],
        out_shape=[jax.ShapeDtypeStruct((t, D), F32), jax.ShapeDtypeStruct((8, D), F32)],
        compiler_params=_cparams(("arbitrary",)),
    )(ctx, x, da, db, dh1, g, tab)


def adamw(w, g, m, v, name):
    lead = w.ndim - 2
    rows, cols = w.shape[-2:]
    tm = rows if rows * cols <= 256 * 1024 else _pick(rows, (256, 176, 128, 64, 8))
    c1 = 1.0 / (1.0 - ADAM_B1 ** ADAM_STEP)
    c2 = 1.0 / (1.0 - ADAM_B2 ** ADAM_STEP)

    def body(w_ref, g_ref, m_ref, v_ref, d_ref, mo_ref, vo_ref):
        gv = g_ref[...]
        mn = ADAM_B1 * m_ref[...] + (1.0 - ADAM_B1) * gv
        vn = ADAM_B2 * v_ref[...] + (1.0 - ADAM_B2) * (gv * gv)
        mo_ref[...] = mn
        vo_ref[...] = vn
        d_ref[...] = -ADAM_LR * ((mn * c1) / (jnp.sqrt(vn * c2) + ADAM_EPS) + ADAM_WD * w_ref[...])

    spec = pl.BlockSpec((1,) * lead + (tm, cols), lambda i: (0,) * lead + (i, 0))
    sds = jax.ShapeDtypeStruct(w.shape, F32)
    return pl.pallas_call(
        body, name=name, grid=(rows // tm,),
        in_specs=[spec] * 4, out_specs=[spec] * 3, out_shape=[sds] * 3,
        compiler_params=_cparams(("parallel",)),
    )(w, g, m, v)


def add_n(arrs, out_dtype, name):
    shp = arrs[0].shape
    cols = shp[-1]
    flat = [a.reshape(-1, cols) for a in arrs]
    rows = flat[0].shape[0]
    tm = max(t for t in range(16, rows + 1, 16) if rows % t == 0 and t * cols <= 512 * 1024)

    def body(*refs):
        acc = refs[0][...].astype(F32)
        for r in refs[1:-1]:
            acc = acc + r[...].astype(F32)
        refs[-1][...] = acc.astype(refs[-1].dtype)

    spec = pl.BlockSpec((tm, cols), lambda i: (i, 0))
    out = pl.pallas_call(
        body, name=name, grid=(rows // tm,),
        in_specs=[spec] * len(flat), out_specs=spec, out_shape=jax.ShapeDtypeStruct((rows, cols), out_dtype),
        compiler_params=_cparams(("parallel",)),
    )(*flat)
    return out.reshape(shp)


def sum8(stack, name):
    _, rows, cols = stack.shape
    tm = rows if rows <= 2048 else _pick(rows, (512, 256, 128, 64, 8))

    def body(s_ref, o_ref):
        acc = s_ref[0]
        for k in range(1, 8):
            acc = acc + s_ref[k]
        o_ref[...] = acc

    return pl.pallas_call(
        body, name=name, grid=(rows // tm,),
        in_specs=[pl.BlockSpec((8, tm, cols), lambda i: (0, i, 0))],
        out_specs=pl.BlockSpec((tm, cols), lambda i: (i, 0)),
        out_shape=jax.ShapeDtypeStruct((rows, cols), F32),
        compiler_params=_cparams(("parallel",)),
    )(stack)


def _coords():
    return lax.axis_index("x"), lax.axis_index("y"), lax.axis_index("c")


def _other_chips(x, y):
    return [(1 - x, y), (x, 1 - y), (1 - x, 1 - y)]


_ANY = pl.BlockSpec(memory_space=pl.ANY)


def gather_chips(slabs):
    ns = len(slabs)

    def body(*refs):
        x_refs, out_refs = refs[:ns], refs[ns:2 * ns]
        send_sems, recv_sems = refs[2 * ns:]
        x, y, c = _coords()
        me = 2 * x + y
        sibling = (x, y, 1 - c)
        chips = _other_chips(x, y)

        def half(s, chip, hc):
            rh = slabs[s].shape[0] // 2
            return out_refs[s].at[chip, pl.ds(hc * rh, rh), :]

        def own_half(s):
            rh = slabs[s].shape[0] // 2
            return x_refs[s].at[pl.ds(c * rh, rh), :]

        sends = []
        for s in range(ns):
            for j, (px, py) in enumerate(chips):
                cp = pltpu.make_async_remote_copy(
                    src_ref=own_half(s), dst_ref=half(s, me, c), send_sem=send_sems.at[6 * s + j],
                    recv_sem=recv_sems.at[6 * s + j], device_id=(px, py, c), device_id_type=MESH)
                cp.start()
                sends.append(cp)
        for s in range(ns):
            for j, (px, py) in enumerate(chips):
                src = 2 * px + py
                landed = pltpu.make_async_remote_copy(
                    src_ref=half(s, src, c), dst_ref=half(s, src, c), send_sem=send_sems.at[6 * s + j],
                    recv_sem=recv_sems.at[6 * s + j], device_id=(px, py, c), device_id_type=MESH)
                landed.wait_recv()
                fw = pltpu.make_async_remote_copy(
                    src_ref=half(s, src, c), dst_ref=half(s, src, c), send_sem=send_sems.at[6 * s + 3 + j],
                    recv_sem=recv_sems.at[6 * s + 3 + j], device_id=sibling, device_id_type=MESH)
                fw.start()
                sends.append(fw)
        for s in range(ns):
            for j, (px, py) in enumerate(chips):
                src = 2 * px + py
                got = pltpu.make_async_remote_copy(
                    src_ref=half(s, src, 1 - c), dst_ref=half(s, src, 1 - c), send_sem=send_sems.at[6 * s + 3 + j],
                    recv_sem=recv_sems.at[6 * s + 3 + j], device_id=sibling, device_id_type=MESH)
                got.wait_recv()
        for cp in sends:
            cp.wait_send()

    return pl.pallas_call(
        body, name="gather_chips",
        in_specs=[_ANY] * ns, out_specs=[_ANY] * ns,
        out_shape=[jax.ShapeDtypeStruct((4,) + s.shape, s.dtype) for s in slabs],
        scratch_shapes=[pltpu.SemaphoreType.DMA((6 * ns,)), pltpu.SemaphoreType.DMA((6 * ns,))],
    )(*slabs)


def swap_halves(gss):
    ns = len(gss)

    def body(*refs):
        g_refs, out_refs = refs[:ns], refs[ns:2 * ns]
        send_sems, recv_sems = refs[2 * ns:]
        x, y, c = _coords()
        cps = []
        for s in range(ns):
            rh = gss[s].shape[1] // 2
            cp = pltpu.make_async_remote_copy(
                src_ref=g_refs[s].at[:, pl.ds((1 - c) * rh, rh), :], dst_ref=out_refs[s],
                send_sem=send_sems.at[s], recv_sem=recv_sems.at[s], device_id=(x, y, 1 - c), device_id_type=MESH)
            cp.start()
            cps.append(cp)
        for cp in cps:
            cp.wait()

    return pl.pallas_call(
        body, name="swap_halves",
        in_specs=[_ANY] * ns, out_specs=[_ANY] * ns,
        out_shape=[jax.ShapeDtypeStruct((4, g.shape[1] // 2, g.shape[2]), g.dtype) for g in gss],
        scratch_shapes=[pltpu.SemaphoreType.DMA((ns,)), pltpu.SemaphoreType.DMA((ns,))],
    )(*gss)


def join_halves(reds):
    ns = len(reds)

    def body(*refs):
        r_refs, out_refs = refs[:ns], refs[ns:2 * ns]
        send_sems, recv_sems = refs[2 * ns:]
        x, y, c = _coords()
        cps = []
        for s in range(ns):
            cp = pltpu.make_async_remote_copy(
                src_ref=r_refs[s], dst_ref=out_refs[s], send_sem=send_sems.at[s], recv_sem=recv_sems.at[s],
                device_id=(x, y, 1 - c), device_id_type=MESH)
            cp.start()
            cps.append(cp)
        for cp in cps:
            cp.wait()

    return pl.pallas_call(
        body, name="join_halves",
        in_specs=[_ANY] * ns, out_specs=[_ANY] * ns,
        out_shape=[jax.ShapeDtypeStruct(r.shape, r.dtype) for r in reds],
        scratch_shapes=[pltpu.SemaphoreType.DMA((ns,)), pltpu.SemaphoreType.DMA((ns,))],
    )(*reds)


def gather_all(vec, name):
    r, wd = vec.shape

    def body(v_ref, out_ref, send_sems, recv_sems):
        x, y, c = _coords()
        me = 4 * x + 2 * y + c
        cps = []
        for k in range(1, 8):
            mx, my, mc = (k >> 2) & 1, (k >> 1) & 1, k & 1
            peer = (x ^ mx, y ^ my, c ^ mc)
            cp = pltpu.make_async_remote_copy(
                src_ref=v_ref, dst_ref=out_ref.at[me],
                send_sem=send_sems.at[k - 1], recv_sem=recv_sems.at[k - 1], device_id=peer, device_id_type=MESH)
            cp.start()
            cps.append(cp)
        for k in range(1, 8):
            mx, my, mc = (k >> 2) & 1, (k >> 1) & 1, k & 1
            peer = (x ^ mx, y ^ my, c ^ mc)
            src = 4 * peer[0] + 2 * peer[1] + peer[2]
            got = pltpu.make_async_remote_copy(
                src_ref=v_ref, dst_ref=out_ref.at[src],
                send_sem=send_sems.at[k - 1], recv_sem=recv_sems.at[k - 1], device_id=peer, device_id_type=MESH)
            got.wait_recv()
        for cp in cps:
            cp.wait_send()

    return pl.pallas_call(
        body, name=name,
        in_specs=[_ANY], out_specs=_ANY,
        out_shape=jax.ShapeDtypeStruct((8, r, wd), vec.dtype),
        scratch_shapes=[pltpu.SemaphoreType.DMA((7,)), pltpu.SemaphoreType.DMA((7,))],
    )(vec)


def _pad_rows(a, rows):
    return jnp.pad(a, ((0, rows - a.shape[0]), (0, 0)))


def _pad_cols(a, cols):
    return jnp.pad(a, ((0, 0), (0, cols - a.shape[1])))


def local_step(x, c, ctx, c_ctx, target, wt, sm, late_weights=None, grad_hook=None):
    t, tc = x.shape[0], ctx.shape[0]
    tp = t + tc
    nct = tc // TR
    ncc = tc // LC
    nc = tp // LC

    w_in = wt["w_in"]
    segs = {"q": (0, D), "k": (D, 2 * D), "v": (2 * D, 3 * D), "g": (3 * D, 3 * D + NGATE)}
    base = 3 * D + NGATE
    for n_i, nm in enumerate(("o", "u", "vg", "gm", "gg")):
        segs[nm] = (base + n_i * D, base + (n_i + 1) * D)
    order = ("o", "u", "vg", "gm", "gg", "v", "q", "k")
    w_main = jnp.concatenate([w_in[:, segs[nm][0]:segs[nm][1]] for nm in order], axis=1)
    w_g = _pad_cols(w_in[:, segs["g"][0]:segs["g"][1]], LANE)
    w_main_t = w_main.T
    w_g_t = w_g.T

    cc = _pad_rows(jnp.concatenate([c.reshape(1, D), c_ctx.reshape(1, D)], axis=0), 16)
    modv = mod_fwd(cc, wt["w_mod"], sm["b_mod"].reshape(1, NMOD * D))
    mx = modv[0].reshape(NMOD, D)
    mc = modv[1].reshape(NMOD, D)
    tab1 = _pad_rows(jnp.stack([mc[0], mc[1], mx[0], mx[1]]), 8)
    tab2 = _pad_rows(jnp.stack([mx[2], mx[3], mx[4]]), 8)
    tab3 = _pad_rows(mx[5:6], 8)

    g1 = sm["norm1_g"].reshape(1, D)
    g2 = sm["norm2_g"].reshape(1, D)
    gfin = sm["final_g"].reshape(1, D)
    gh = sm["head_norm_g"].reshape(1, D)
    ln_g = sm["sgu_ln_g"].reshape(1, D)
    ln_b = sm["sgu_ln_b"].reshape(1, D)
    ws = sm["w_s"].reshape(SG * SCH, SCH).astype(BF16)
    bs_t = _pad_cols(sm["b_s"].reshape(SG, SCH).T, LANE)
    conv_w = _pad_rows(sm["conv_qk"].reshape(3, 2 * D), 8)
    b_gate = _pad_cols(sm["b_gate"].reshape(1, NGATE), LANE)
    wc = _pad_rows(sm["w_ffn_conv"].reshape(9, DFF), 16)

    hn1 = norm1_fwd(ctx, x, g1, tab1, nct)
    if late_weights is None:
        zmain = mm_nn(hn1, w_main, BF16, "mm_zmain")
    else:
        zmain, landed = mm_nn(hn1, w_main, BF16, "mm_zmain", rider=late_weights[0])
        wt = {**wt, **late_weights[1](landed)}
    zg = mm_nn(hn1, w_g, F32, "mm_zg")
    qa, ka, gcol, zgb = qkconv_fwd(zmain, zg, conv_w, b_gate, nct)
    grow = gcol[:, :16].reshape(nc, LC, 16).transpose(0, 2, 1)
    hf, hb, csf, csb, nsf, nsb, msf, msb = mlstm_fwd(qa, ka, zmain, gcol, grow, ncc)
    yms = mixer_fwd(hf, hb, zmain, gh, ln_g, ln_b, ws, bs_t, nct)
    w_br = jnp.stack([wt["w_branch_mlstm"], wt["w_branch_sgu"]])
    pp = mm_nn(yms, w_br, BF16, "mm_branch")
    y = merge_fwd(zmain, pp, nct)
    out = mm_nn(y, wt["w_out"], BF16, "mm_out")
    h1, hn2 = norm2_fwd(x, out, g2, tab2)
    up = mm_nn(hn2, wt["w_up"], BF16, "mm_up")
    act, ac = ffn_act_fwd(up, wc)
    f = mm_nn(act, wt["w_down"], BF16, "mm_down")
    dh2, df, acc_h = head_fwd_bwd(h1, f, target, gfin, tab3)
    loss = acc_h[0, 0]

    g_w_down = mm_tn(act, df, "mmt_down", BF16)
    dact = mm_nn(df, wt["w_down"].T, BF16, "mm_ddown")
    dup, dac = ffn_act_bwd(up, ac, dact)
    dup, g_wc = ffn_conv_bwd(dup, up, dac, wc)
    g_w_up = mm_tn(hn2, dup, "mmt_up", BF16)
    dhn2 = mm_nn(dup, wt["w_up"].T, BF16, "mm_dup")
    dh1, dout, acc_2 = norm2_bwd(h1, dhn2, dh2, out, g2, tab2)
    g_w_out = mm_tn(y, dout, "mmt_out", BF16)
    dy = mm_nn(dout, wt["w_out"].T, BF16, "mm_dout")
    dpp, dz = merge_bwd(zmain, pp, dy, tp, nct)
    g_w_br = mm_tn(yms, dpp, "mmt_branch", BF16)
    dyms = mm_nn(dpp, jnp.stack([wt["w_branch_mlstm"].T, wt["w_branch_sgu"].T]), BF16, "mm_dbranch")
    dz, dhm, g_gh, g_lng, g_lnb, g_ws, g_bs = mixer_bwd(dz, hf, hb, zmain, dyms, gh, ln_g, ln_b, ws, bs_t, nct)
    (dqf, dkf, dvf, colf, rowf, dqb, dkb, dvb, colb, rowb) = mlstm_bwd(
        qa, ka, zmain, gcol, grow, (csf, csb, nsf, nsb, msf, msb), dhm, ncc)

    csum_f = rowf[:, :4, :].transpose(0, 2, 1).reshape(tp, 4)
    csum_b = rowb[:, :4, :].transpose(0, 2, 1).reshape(tp, 4)
    cs = _pad_cols(jnp.concatenate([csum_f, csum_f, csum_b, csum_b], axis=1), LANE)
    dzg, g_bgate = gates_bwd(colf, colb, cs, zgb)

    dc, g_convw = qkconv_bwd_a(zmain, dqf, dqb, dkf, dkb, conv_w, nct)
    dz = qkconv_bwd_b(dz, dc, conv_w, nct)
    dz = add_into_dz(dz, dvf, dvb, CB_V)

    g_w_main = mm_tn(hn1, dz, "mmt_main", BF16)
    g_w_g = mm_tn(hn1, dzg, "mmt_g", BF16)
    blk = lambda cb: g_w_main[:, cb * D:(cb + 1) * D]
    g_w_in = jnp.concatenate([blk(CB_Q), blk(CB_K), blk(CB_V), g_w_g[:, :NGATE], blk(CB_O), blk(CB_U), blk(CB_VG),
                              blk(CB_GM), blk(CB_GG)], axis=1)
    big = {"w_in": g_w_in, "w_branch_mlstm": g_w_br[0], "w_branch_sgu": g_w_br[1], "w_out": g_w_out,
           "w_up": g_w_up, "w_down": g_w_down}
    if grad_hook is None:
        da, received = mm_nn(dz, w_main_t, BF16, "mm_dmain"), None
    else:
        da, received = mm_nn(dz, w_main_t, BF16, "mm_dmain", rider=grad_hook(big))
    db = mm_nn(dzg, w_g_t, F32, "mm_dg")
    grad_x, acc_1 = norm1_bwd(ctx, x, da, db, dh1, g1, tab1, nct)

    d_modx = jnp.concatenate([acc_1[2], acc_1[3], acc_2[3], acc_2[0], acc_2[1], acc_h[2]])
    d_modc = jnp.concatenate([acc_1[0], acc_1[1], jnp.zeros((4 * D,), F32)])
    d_modb = jnp.concatenate([acc_1[5], acc_1[6], acc_2[3], acc_2[0], acc_2[1], acc_h[2]])

    small = {"b_mod": d_modb, "norm1_g": acc_1[4], "b_gate": g_bgate[0, :NGATE], "conv_qk": g_convw[:3].reshape(-1),
             "head_norm_g": g_gh[0], "sgu_ln_g": g_lng[0], "sgu_ln_b": g_lnb[0], "w_s": g_ws.reshape(-1),
             "b_s": g_bs[:, :SG].T.reshape(-1), "norm2_g": acc_2[2], "w_ffn_conv": g_wc[:9].reshape(-1),
             "final_g": acc_h[1]}
    return loss, grad_x, big, small, d_modx, d_modc, received


def mod_bwd_w(a_all, dm_all, name):
    n = dm_all.shape[1]
    tn = _pick(n, (512, 128))

    def body(a_ref, d_ref, o_ref):
        o_ref[...] = _dot_tn(_silu(a_ref[...]), d_ref[...])

    return pl.pallas_call(
        body, name=name, grid=(n // tn,),
        in_specs=[_bc(16, D), pl.BlockSpec((16, tn), lambda j: (0, j))],
        out_specs=pl.BlockSpec((D, tn), lambda j: (0, j)),
        out_shape=jax.ShapeDtypeStruct((D, n), F32),
        compiler_params=_cparams(("parallel",)),
    )(a_all, dm_all)


def mod_bwd_cctx(dmc, w_mod_t, c_ctx):
    def body(d_ref, w_ref, c_ref, o_ref):
        o_ref[...] = _dot(d_ref[...], w_ref[...]) * _dsilu(c_ref[...])

    return pl.pallas_call(
        body, name="mod_bwd_cctx", grid=(1,),
        in_specs=[_bc(16, 2 * D), _bc(2 * D, D), _bc(1, D)],
        out_specs=_bc(16, D),
        out_shape=jax.ShapeDtypeStruct((16, D), F32),
        compiler_params=_cparams(("arbitrary",)),
    )(dmc, w_mod_t, c_ctx)


BIG = ("w_mod", "w_in", "w_branch_mlstm", "w_branch_sgu", "w_out", "w_up", "w_down")
BIG_AXIS = {"w_mod": 1, "w_in": 1, "w_branch_mlstm": 0, "w_branch_sgu": 0, "w_out": 0, "w_up": 1, "w_down": 0}
SMALL = ("c_ctx", "b_mod", "norm1_g", "b_gate", "conv_qk", "head_norm_g", "sgu_ln_g", "sgu_ln_b", "w_s", "b_s",
         "norm2_g", "w_ffn_conv", "final_g")
SMALL_SHARDED = {"conv_qk": (3, 2 * D), "w_ffn_conv": (9, DFF)}
PACK_ALIGN = 32 * D


def _pack(arrs, dtype, align=PACK_ALIGN, width=D):
    flat = jnp.concatenate([a.reshape(-1).astype(dtype) for a in arrs])
    n = flat.shape[0]
    padded = -(-n // align) * align
    return jnp.pad(flat, (0, padded - n)).reshape(padded // width, width)


def _unpack(slab, shapes):
    flat = slab.reshape(-1)
    outs, off = [], 0
    for shp in shapes:
        n = math.prod(shp)
        outs.append(flat[off:off + n].reshape(shp))
        off += n
    return outs


def _round_up(n, m):
    return -(-n // m) * m


def kernel(x, c, ctx, c_ctx, w_mod, b_mod, norm1_g, w_in, b_gate, conv_qk, head_norm_g, sgu_ln_g, sgu_ln_b, w_s, b_s, w_branch_mlstm, w_branch_sgu, w_out, norm2_g, w_up, w_ffn_conv, w_down, final_g, loss_target, m_c_ctx, m_w_mod, m_b_mod, m_norm1_g, m_w_in, m_b_gate, m_conv_qk, m_head_norm_g, m_sgu_ln_g, m_sgu_ln_b, m_w_s, m_b_s, m_w_branch_mlstm, m_w_branch_sgu, m_w_out, m_norm2_g, m_w_up, m_w_ffn_conv, m_w_down, m_final_g, v_c_ctx, v_w_mod, v_b_mod, v_norm1_g, v_w_in, v_b_gate, v_conv_qk, v_head_norm_g, v_sgu_ln_g, v_sgu_ln_b, v_w_s, v_b_s, v_w_branch_mlstm, v_w_branch_sgu, v_w_out, v_norm2_g, v_w_up, v_w_ffn_conv, v_w_down, v_final_g):
    params = dict(c_ctx=c_ctx, w_mod=w_mod, b_mod=b_mod, norm1_g=norm1_g, w_in=w_in, b_gate=b_gate, conv_qk=conv_qk,
                  head_norm_g=head_norm_g, sgu_ln_g=sgu_ln_g, sgu_ln_b=sgu_ln_b, w_s=w_s, b_s=b_s,
                  w_branch_mlstm=w_branch_mlstm, w_branch_sgu=w_branch_sgu, w_out=w_out, norm2_g=norm2_g, w_up=w_up,
                  w_ffn_conv=w_ffn_conv, w_down=w_down, final_g=final_g)
    mom_m = dict(c_ctx=m_c_ctx, w_mod=m_w_mod, b_mod=m_b_mod, norm1_g=m_norm1_g, w_in=m_w_in, b_gate=m_b_gate,
                 conv_qk=m_conv_qk, head_norm_g=m_head_norm_g, sgu_ln_g=m_sgu_ln_g, sgu_ln_b=m_sgu_ln_b, w_s=m_w_s,
                 b_s=m_b_s, w_branch_mlstm=m_w_branch_mlstm, w_branch_sgu=m_w_branch_sgu, w_out=m_w_out,
                 norm2_g=m_norm2_g, w_up=m_w_up, w_ffn_conv=m_w_ffn_conv, w_down=m_w_down, final_g=m_final_g)
    mom_v = dict(c_ctx=v_c_ctx, w_mod=v_w_mod, b_mod=v_b_mod, norm1_g=v_norm1_g, w_in=v_w_in, b_gate=v_b_gate,
                 conv_qk=v_conv_qk, head_norm_g=v_head_norm_g, sgu_ln_g=v_sgu_ln_g, sgu_ln_b=v_sgu_ln_b, w_s=v_w_s,
                 b_s=v_b_s, w_branch_mlstm=v_w_branch_mlstm, w_branch_sgu=v_w_branch_sgu, w_out=v_w_out,
                 norm2_g=v_norm2_g, w_up=v_w_up, w_ffn_conv=v_w_ffn_conv, w_down=v_w_down, final_g=v_final_g)
    chip = 2 * lax.axis_index("x") + lax.axis_index("y")

    shard2d = {n: params[n].reshape(params[n].shape[-2:]) for n in BIG}
    conv_sh = conv_qk.reshape(3, -1)
    fconv_sh = w_ffn_conv.reshape(9, -1)

    dev = 2 * chip + lax.axis_index("c")

    first_names, row_names = ("w_mod", "w_in"), ("w_branch_mlstm", "w_branch_sgu", "w_out", "w_down")
    first_w = [shard2d[n].shape[1] for n in first_names]
    row_h = [shard2d[n].shape[0] for n in row_names]
    first_slab = _pad_cols(jnp.concatenate([shard2d[n].astype(BF16) for n in first_names], axis=1),
                           _round_up(sum(first_w), LANE))
    up_slab = shard2d["w_up"].astype(BF16)
    row_slab = jnp.concatenate([shard2d[n].astype(BF16) for n in row_names], axis=0)

    def own_in(slab, gathered):
        return jnp.stack([jnp.where(chip == j, slab, gathered[j]) for j in range(4)])

    first_all = own_in(first_slab, gather_chips([first_slab])[0])
    wt = {}
    off = 0
    for n, wd in zip(first_names, first_w):
        wt[n] = jnp.concatenate([first_all[j, :, off:off + wd] for j in range(4)], axis=1)
        off += wd

    late_slabs = [up_slab, row_slab]

    def late_copies(in_refs, out_refs, send_sems, recv_sems):
        xx, yy, cc = _coords()
        me = 2 * xx + yy
        cps = []
        for s, slab in enumerate(late_slabs):
            rh = slab.shape[0] // 2
            for j, (px, py) in enumerate(_other_chips(xx, yy)):
                for o in range(2):
                    k = 6 * s + 2 * j + o
                    cps.append(pltpu.make_async_remote_copy(
                        src_ref=in_refs[s].at[pl.ds(cc * rh, rh), :], dst_ref=out_refs[s].at[me, pl.ds(cc * rh, rh), :],
                        send_sem=send_sems.at[k], recv_sem=recv_sems.at[k],
                        device_id=(px, py, cc if o == 0 else 1 - cc), device_id_type=MESH))
        return cps

    def late_finish(landed):
        up_all, row_all = own_in(up_slab, landed[0]), own_in(row_slab, landed[1])
        got = {"w_up": jnp.concatenate([up_all[j] for j in range(4)], axis=1)}
        o = 0
        for n, ht in zip(row_names, row_h):
            got[n] = jnp.concatenate([row_all[j, o:o + ht, :] for j in range(4)], axis=0)
            o += ht
        return got

    late_rider = Rider(late_slabs, [jax.ShapeDtypeStruct((4,) + s_.shape, s_.dtype) for s_ in late_slabs],
                       6 * len(late_slabs), late_copies)

    cvec = _pack([conv_sh, fconv_sh], F32, align=8 * LANE, width=LANE)
    call = gather_all(cvec, "gather_conv")
    cparts = [_unpack(jnp.where(dev == 2 * j, cvec, call[2 * j]), [conv_sh.shape, fconv_sh.shape]) for j in range(4)]
    conv_full = jnp.concatenate([p[0] for p in cparts], axis=1)
    fconv_full = jnp.concatenate([p[1] for p in cparts], axis=1)

    sm = dict(b_mod=b_mod, norm1_g=norm1_g, b_gate=b_gate, conv_qk=conv_full, head_norm_g=head_norm_g,
              sgu_ln_g=sgu_ln_g, sgu_ln_b=sgu_ln_b, w_s=w_s, b_s=b_s, norm2_g=norm2_g, w_ffn_conv=fconv_full,
              final_g=final_g)

    gcol_names = ("w_up", "w_in")
    gcol_w = [shard2d[n].shape[1] for n in gcol_names]
    gcol_pad = _round_up(sum(gcol_w), LANE)
    cidx = lax.axis_index("c")
    kept = {}

    def grad_hook(gbig):
        def chip_cols(j):
            return _pad_cols(jnp.concatenate([gbig[n][:, j * wd:(j + 1) * wd] for n, wd in zip(gcol_names, gcol_w)],
                                             axis=1), gcol_pad)

        def chip_rows(j):
            return jnp.concatenate([gbig[n][j * ht:(j + 1) * ht] for n, ht in zip(row_names, row_h)], axis=0)

        gss = [jnp.stack([chip_cols(j) for j in range(4)]), jnp.stack([chip_rows(j) for j in range(4)])]
        from_sib = swap_halves(gss)
        pair_bf, own_terms = [], []
        for s, (gs, fs) in enumerate(zip(gss, from_sib)):
            rh = gs.shape[1] // 2
            my_half = lax.dynamic_slice_in_dim(gs, cidx * rh, rh, axis=1)
            pair_bf.append(add_n([my_half, fs], BF16, "pair_sum_%d" % s))
            own_terms.append([lax.dynamic_index_in_dim(my_half, chip, axis=0, keepdims=False),
                              lax.dynamic_index_in_dim(fs, chip, axis=0, keepdims=False)])
        kept["own_terms"] = own_terms

        def scatter_copies(in_refs, out_refs, send_sems, recv_sems):
            xx, yy, cc = _coords()
            cps = []
            for s in range(len(pair_bf)):
                for j, (px, py) in enumerate(_other_chips(xx, yy)):
                    cps.append(pltpu.make_async_remote_copy(
                        src_ref=in_refs[s].at[2 * px + py], dst_ref=out_refs[s].at[j],
                        send_sem=send_sems.at[3 * s + j], recv_sem=recv_sems.at[3 * s + j],
                        device_id=(px, py, cc), device_id_type=MESH))
            return cps

        return Rider(pair_bf, [jax.ShapeDtypeStruct((3,) + p.shape[1:], p.dtype) for p in pair_bf],
                     3 * len(pair_bf), scatter_copies)

    loss_l, grad_x, _, gsmall, d_modx, d_modc, recv = local_step(
        x[0], c, ctx[0], c_ctx, loss_target[0], wt, sm, late_weights=(late_rider, late_finish), grad_hook=grad_hook)

    reds = [add_n(kept["own_terms"][s] + [recv[s][0], recv[s][1], recv[s][2]], F32, "chip_sum_%d" % s)
            for s in range(2)]
    others = join_halves(reds)
    full_red = [jnp.where(cidx == 0, jnp.concatenate([m, o], axis=0), jnp.concatenate([o, m], axis=0))
                for m, o in zip(reds, others)]
    g_shard = {}
    off = 0
    for n, wd in zip(gcol_names, gcol_w):
        g_shard[n] = full_red[0][:, off:off + wd]
        off += wd
    off = 0
    for n, ht in zip(row_names, row_h):
        g_shard[n] = full_red[1][off:off + ht]
        off += ht

    small_order = ("b_mod", "norm1_g", "b_gate", "conv_qk", "head_norm_g", "sgu_ln_g", "sgu_ln_b", "w_s", "b_s", "norm2_g",
                   "w_ffn_conv", "final_g")
    vec_parts = [gsmall[n] for n in small_order] + [d_modx, d_modc, c.reshape(-1), loss_l.reshape(1)]
    vec_shapes = [a.shape for a in vec_parts]
    vec = _pack(vec_parts, F32, align=8 * LANE, width=LANE)
    allv = gather_all(vec, "gather_small")
    allv = jnp.stack([jnp.where(dev == k, vec, allv[k]) for k in range(8)])
    summed = sum8(allv, "small_sum")
    s_parts = _unpack(summed, vec_shapes)
    g_small = dict(zip(small_order, s_parts[:len(small_order)]))
    dmc_sum = s_parts[len(small_order) + 1]
    loss = s_parts[-1][0]
    per_dev = [_unpack(allv[k], vec_shapes) for k in range(8)]
    dmx_all = jnp.stack([p[len(small_order)] for p in per_dev])
    c_all = jnp.stack([p[len(small_order) + 2] for p in per_dev])

    a_all = _pad_rows(jnp.concatenate([c_all, c_ctx.reshape(1, D)], axis=0), 16)
    dm_all = _pad_rows(jnp.concatenate([dmx_all, dmc_sum.reshape(1, NMOD * D)], axis=0), 16)
    ncol = NMOD * D // 4
    dm_shard = lax.dynamic_slice_in_dim(dm_all, chip * ncol, ncol, axis=1)
    g_shard["w_mod"] = mod_bwd_w(a_all, dm_shard, "mod_bwd_w")
    w_mod_t = wt["w_mod"][:, :2 * D].T
    g_cctx = mod_bwd_cctx(_pad_rows(dmc_sum[:2 * D].reshape(1, 2 * D), 16), w_mod_t, c_ctx.reshape(1, D))[0]
    g_small["c_ctx"] = g_cctx

    results = {}
    for n in BIG:
        shp = params[n].shape
        g_ = g_shard[n].reshape(shp)
        d_, m_, v_ = adamw(params[n], g_, mom_m[n], mom_v[n], "adamw_" + n)
        results[n] = (g_, d_, m_, v_)

    conv_g = lax.dynamic_slice_in_dim(g_small["conv_qk"].reshape(3, 2 * D), chip * (2 * D // 4), 2 * D // 4, axis=1)
    fconv_g = lax.dynamic_slice_in_dim(g_small["w_ffn_conv"].reshape(9, DFF), chip * (DFF // 4), DFF // 4, axis=1)
    g_small["conv_qk"] = conv_g
    g_small["w_ffn_conv"] = fconv_g
    w_list = [params[n].reshape(-1) for n in SMALL]
    g_list = [g_small[n].reshape(-1) for n in SMALL]
    m_list = [mom_m[n].reshape(-1) for n in SMALL]
    v_list = [mom_v[n].reshape(-1) for n in SMALL]
    sm_shapes = [params[n].shape for n in SMALL]
    pk = lambda lst: _pack(lst, F32, align=8 * LANE, width=LANE)
    gp = pk(g_list)
    d_s, m_s, v_s = adamw(pk(w_list), gp, pk(m_list), pk(v_list), "adamw_small")
    for n, gg, dd, mm, vv in zip(SMALL, _unpack(gp, sm_shapes), _unpack(d_s, sm_shapes), _unpack(m_s, sm_shapes),
                                 _unpack(v_s, sm_shapes)):
        results[n] = (gg, dd, mm, vv)

    order = ("c_ctx", "w_mod", "b_mod", "norm1_g", "w_in", "b_gate", "conv_qk", "head_norm_g", "sgu_ln_g", "sgu_ln_b",
             "w_s", "b_s", "w_branch_mlstm", "w_branch_sgu", "w_out", "norm2_g", "w_up", "w_ffn_conv", "w_down",
             "final_g")
    outs = [loss, grad_x[None]]
    for k in range(4):
        outs += [results[n][k] for n in order]
    return tuple(outs)
```

```python
import functools
import math

import jax
import jax.numpy as jnp
from jax import lax
from jax.experimental import pallas as pl
from jax.experimental.pallas import tpu as pltpu

F32 = jnp.float32
BF16 = jnp.bfloat16

D = 1024
NH = 4
DH = 256
LC = 256
GW = 64
SG = 4
SGD = 256
SCH = 128
DFF = 2816
NMOD = 6
NGATE = 16
NIN = 8208
EPS = 1e-6
M_INIT = -1e30
TR = 256
TR2 = 512
LANE = 128
VMEM_LIMIT = 56 * 1024 * 1024
MESH = pl.DeviceIdType.MESH

ADAM_LR = 0.001
ADAM_B1 = 0.9
ADAM_B2 = 0.999
ADAM_EPS = 1e-08
ADAM_WD = 0.01
ADAM_STEP = 10

CB_O, CB_U, CB_VG, CB_GM, CB_GG, CB_V, CB_Q, CB_K = range(8)


def _pick(n, cands):
    for c in cands:
        if n % c == 0:
            return c
    return n


def _cparams(sem):
    return pltpu.CompilerParams(dimension_semantics=sem, vmem_limit_bytes=VMEM_LIMIT)


def _sigmoid(x):
    return 1.0 / (1.0 + jnp.exp(-x))


def _silu(x):
    return x * _sigmoid(x)


def _dsilu(x):
    s = _sigmoid(x)
    return s * (1.0 + x * (1.0 - s))


_GC = math.sqrt(2.0 / math.pi)


def _gelu(x):
    return 0.5 * x * (1.0 + jnp.tanh(_GC * (x + 0.044715 * x * x * x)))


def _dgelu(x):
    t = jnp.tanh(_GC * (x + 0.044715 * x * x * x))
    return 0.5 * (1.0 + t) + 0.5 * x * (1.0 - t * t) * _GC * (1.0 + 3.0 * 0.044715 * x * x)


def _dot(a, b):
    return jnp.dot(a.astype(BF16), b.astype(BF16), preferred_element_type=F32)


def _dot_nt(a, b):
    return lax.dot_general(a.astype(BF16), b.astype(BF16), (((1,), (1,)), ((), ())), preferred_element_type=F32)


def _dot_tn(a, b):
    return lax.dot_general(a.astype(BF16), b.astype(BF16), (((0,), (0,)), ((), ())), preferred_element_type=F32)


def _dot_tn_mxu(a, b):
    m = a.shape[1]
    eye = (lax.broadcasted_iota(jnp.int32, (m, m), 0) == lax.broadcasted_iota(jnp.int32, (m, m), 1)).astype(BF16)
    return _dot(_dot_nt(eye, a), b)


def _exact_dot(tri, x):
    x1 = x.astype(BF16)
    r1 = x - x1.astype(F32)
    x2 = r1.astype(BF16)
    x3 = (r1 - x2.astype(F32)).astype(BF16)
    return (jnp.dot(tri, x1, preferred_element_type=F32) + jnp.dot(tri, x2, preferred_element_type=F32)
            + jnp.dot(tri, x3, preferred_element_type=F32))


def _rb(tm, w, col=0, off=0):
    return pl.BlockSpec((tm, w), lambda i: (i + off, col))


def _bc(r, w):
    return pl.BlockSpec((r, w), lambda i: (0, 0))


class Rider:
    def __init__(self, ins, out_shapes, n_sems, copies):
        self.ins, self.out_shapes, self.n_sems, self.copies = list(ins), list(out_shapes), n_sems, copies


def mm_nn(a, b, out_dtype, name, rider=None):
    squeeze = a.ndim == 2
    if squeeze:
        a, b = a[None], b[None]
    g, m, k = a.shape
    n = b.shape[2]
    tm = _pick(m, (2048, 1280, 1024, 512, 256, 128))
    tn = _pick(n, (2048, 1408, 1024, 512, 128))
    tk = _pick(k, (2048, 1408, 1024, 512, 128))
    nk = k // tk
    grid = (g, n // tn, m // tm, nk)
    n_rin = len(rider.ins) if rider else 0
    n_rout = len(rider.out_shapes) if rider else 0

    def body(*refs):
        a_ref, b_ref = refs[0], refs[1]
        r_in = refs[2:2 + n_rin]
        o_ref = refs[2 + n_rin]
        r_out = refs[3 + n_rin:3 + n_rin + n_rout]
        scr = refs[3 + n_rin + n_rout:]
        if rider:
            ids = [pl.program_id(d) for d in range(4)]
            first = functools.reduce(jnp.logical_and, [i == 0 for i in ids])
            last = functools.reduce(jnp.logical_and, [i == e - 1 for i, e in zip(ids, grid)])
            send_sems, recv_sems = scr[-2], scr[-1]

            @pl.when(first)
            def _():
                for cp in rider.copies(r_in, r_out, send_sems, recv_sems):
                    cp.start()

        if nk == 1:
            o_ref[0] = _dot(a_ref[0], b_ref[0]).astype(o_ref.dtype)
        else:
            acc_ref = scr[0]
            kk = pl.program_id(3)

            @pl.when(kk == 0)
            def _():
                acc_ref[...] = jnp.zeros_like(acc_ref)

            acc_ref[...] += _dot(a_ref[0], b_ref[0])

            @pl.when(kk == nk - 1)
            def _():
                o_ref[0] = acc_ref[...].astype(o_ref.dtype)

        if rider:
            @pl.when(last)
            def _():
                for cp in rider.copies(r_in, r_out, send_sems, recv_sems):
                    cp.wait()

    scratch = [] if nk == 1 else [pltpu.VMEM((tm, tn), F32)]
    if rider:
        scratch += [pltpu.SemaphoreType.DMA((rider.n_sems,)), pltpu.SemaphoreType.DMA((rider.n_sems,))]
    outs = pl.pallas_call(
        body, name=name, grid=grid,
        in_specs=[pl.BlockSpec((1, tm, tk), lambda gi, j, i, kk: (gi, i, kk)),
                  pl.BlockSpec((1, tk, tn), lambda gi, j, i, kk: (gi, kk, j))] + [_ANY] * n_rin,
        out_specs=[pl.BlockSpec((1, tm, tn), lambda gi, j, i, kk: (gi, i, j))] + [_ANY] * n_rout,
        out_shape=[jax.ShapeDtypeStruct((g, m, n), out_dtype)] + (rider.out_shapes if rider else []),
        scratch_shapes=scratch,
        compiler_params=_cparams(("arbitrary",) * 4 if rider else ("parallel", "parallel", "parallel", "arbitrary")),
    )(a, b, *(rider.ins if rider else []))
    out = outs[0][0] if squeeze else outs[0]
    return (out, list(outs[1:])) if rider else out


def mm_tn(a, b, name, out_dtype=F32):
    squeeze = a.ndim == 2
    if squeeze:
        a, b = a[None], b[None]
    g, t, ka = a.shape
    n = b.shape[2]
    tka = _pick(ka, (1024, 1408, 512, 128))
    tn = _pick(n, (2048, 1408, 1024, 512, 128))
    tt = _pick(t, (2048, 1280, 1024, 512, 256, 128))
    nt = t // tt

    def body(a_ref, b_ref, o_ref, acc_ref):
        tt_i = pl.program_id(3)

        @pl.when(tt_i == 0)
        def _():
            acc_ref[...] = jnp.zeros_like(acc_ref)

        acc_ref[...] += _dot_tn(a_ref[0], b_ref[0])

        @pl.when(tt_i == nt - 1)
        def _():
            o_ref[0] = acc_ref[...].astype(o_ref.dtype)

    out = pl.pallas_call(
        body, name=name, grid=(g, ka // tka, n // tn, nt),
        in_specs=[pl.BlockSpec((1, tt, tka), lambda gi, i, j, ti: (gi, ti, i)),
                  pl.BlockSpec((1, tt, tn), lambda gi, i, j, ti: (gi, ti, j))],
        out_specs=pl.BlockSpec((1, tka, tn), lambda gi, i, j, ti: (gi, i, j)),
        out_shape=jax.ShapeDtypeStruct((g, ka, n), out_dtype),
        scratch_shapes=[pltpu.VMEM((tka, tn), F32)],
        compiler_params=_cparams(("parallel", "parallel", "parallel", "arbitrary")),
    )(a, b)
    return out[0] if squeeze else out


def mod_fwd(cc, w_mod, b_mod):
    n = w_mod.shape[1]

    def body(c_ref, w_ref, b_ref, o_ref):
        o_ref[...] = _dot(_silu(c_ref[...]), w_ref[...]) + b_ref[...]

    return pl.pallas_call(
        body, name="mod_fwd", grid=(n // D,),
        in_specs=[_bc(16, D), pl.BlockSpec((D, D), lambda j: (0, j)), pl.BlockSpec((1, D), lambda j: (0, j))],
        out_specs=pl.BlockSpec((16, D), lambda j: (0, j)),
        out_shape=jax.ShapeDtypeStruct((16, n), F32),
        compiler_params=_cparams(("parallel",)),
    )(cc, w_mod, b_mod)


def _ctx_x_specs(n_ctx_tiles):
    return [pl.BlockSpec((TR, D), lambda i: (jnp.minimum(i, n_ctx_tiles - 1), 0)),
            pl.BlockSpec((TR, D), lambda i: (jnp.maximum(i - n_ctx_tiles, 0), 0))]


def norm1_fwd(ctx, x, g, tab, n_ctx_tiles):
    tp = ctx.shape[0] + x.shape[0]

    def body(c_ref, x_ref, g_ref, tab_ref, o_ref):
        is_ctx = pl.program_id(0) < n_ctx_tiles
        x = jnp.where(is_ctx, c_ref[...], x_ref[...])
        r = lax.rsqrt(jnp.mean(x * x, axis=-1, keepdims=True) + EPS)
        nrm = x * r * g_ref[...]
        sh = jnp.where(is_ctx, tab_ref[0:1, :], tab_ref[2:3, :])
        sc = jnp.where(is_ctx, tab_ref[1:2, :], tab_ref[3:4, :])
        o_ref[...] = (nrm * (1.0 + sc) + sh).astype(BF16)

    return pl.pallas_call(
        body, name="norm1_fwd", grid=(tp // TR,),
        in_specs=_ctx_x_specs(n_ctx_tiles) + [_bc(1, D), _bc(8, D)],
        out_specs=_rb(TR, D),
        out_shape=jax.ShapeDtypeStruct((tp, D), BF16),
        compiler_params=_cparams(("parallel",)),
    )(ctx, x, g, tab)


def norm2_fwd(x, out, g, tab):
    t = x.shape[0]

    def body(x_ref, o_in_ref, g_ref, tab_ref, h1_ref, hn_ref):
        h1 = x_ref[...] + tab_ref[0:1, :] * o_in_ref[...].astype(F32)
        h1_ref[...] = h1
        r = lax.rsqrt(jnp.mean(h1 * h1, axis=-1, keepdims=True) + EPS)
        nrm = h1 * r * g_ref[...]
        hn_ref[...] = (nrm * (1.0 + tab_ref[2:3, :]) + tab_ref[1:2, :]).astype(BF16)

    return pl.pallas_call(
        body, name="norm2_fwd", grid=(t // TR2,),
        in_specs=[_rb(TR2, D), _rb(TR2, D), _bc(1, D), _bc(8, D)],
        out_specs=[_rb(TR2, D), _rb(TR2, D)],
        out_shape=[jax.ShapeDtypeStruct((t, D), F32), jax.ShapeDtypeStruct((t, D), BF16)],
        compiler_params=_cparams(("parallel",)),
    )(x, out, g, tab)


def _halo_specs(tm, w, col, n_rows, hb):
    per = tm // hb
    last = n_rows // hb - 1
    prev = pl.BlockSpec((hb, w), lambda i: (jnp.maximum(i * per - 1, 0), col))
    nxt = pl.BlockSpec((hb, w), lambda i: (jnp.minimum((i + 1) * per, last), col))
    return prev, nxt


def _shift_rows(x, prev_row, next_row):
    tm = x.shape[0]
    rid = lax.broadcasted_iota(jnp.int32, x.shape, 0)
    xm1 = jnp.where(rid == 0, prev_row, pltpu.roll(x, 1, 0))
    xp1 = jnp.where(rid == tm - 1, next_row, pltpu.roll(x, tm - 1, 0))
    return xm1, xp1


def _seq_edges(i, n_ctx_tiles, n_tiles):
    first = jnp.logical_or(i == 0, i == n_ctx_tiles)
    last = jnp.logical_or(i == n_ctx_tiles - 1, i == n_tiles - 1)
    return first, last


def qkconv_fwd(zmain, zg, conv_w, b_gate, n_ctx_tiles):
    tp = zmain.shape[0]
    nt = tp // TR
    w2 = 2 * D
    prev_s, next_s = _halo_specs(TR, w2, CB_Q // 2, tp, 16)

    def body(z_ref, zp_ref, zn_ref, w_ref, zg_ref, bg_ref, q_ref, k_ref, g_ref, zgb_ref):
        i = pl.program_id(0)
        first, last = _seq_edges(i, n_ctx_tiles, nt)
        z = z_ref[...].astype(F32)
        pr = jnp.where(first, 0.0, zp_ref[15:16, :].astype(F32))
        nx = jnp.where(last, 0.0, zn_ref[0:1, :].astype(F32))
        zm1, zp1 = _shift_rows(z, pr, nx)
        cv = w_ref[0:1, :] * zm1 + w_ref[1:2, :] * z + w_ref[2:3, :] * zp1
        a = _silu(cv)
        q_ref[...] = (a[:, :D] * (DH ** -0.5)).astype(BF16)
        k_ref[...] = a[:, D:].astype(BF16)
        zgb = zg_ref[...] + bg_ref[...]
        zgb_ref[...] = zgb
        logf = jnp.minimum(zgb, 0.0) - jnp.log(1.0 + jnp.exp(-jnp.abs(zgb)))
        rr = lax.broadcasted_iota(jnp.int32, (TR, TR), 0)
        cc = lax.broadcasted_iota(jnp.int32, (TR, TR), 1)
        same = (rr // LC) == (cc // LC)
        low = jnp.where(jnp.logical_and(same, cc <= rr), 1.0, 0.0).astype(BF16)
        upp = jnp.where(jnp.logical_and(same, cc >= rr), 1.0, 0.0).astype(BF16)
        bf = _exact_dot(low, logf)
        bb = _exact_dot(upp, logf)
        lane = lax.broadcasted_iota(jnp.int32, (TR, LANE), 1)
        g = jnp.where(jnp.logical_and(lane >= 4, lane < 8), bf,
                      jnp.where(jnp.logical_and(lane >= 12, lane < 16), bb, zgb))
        g_ref[...] = g

    return pl.pallas_call(
        body, name="qkconv_fwd", grid=(nt,),
        in_specs=[_rb(TR, w2, CB_Q // 2), prev_s, next_s, _bc(8, w2), _rb(TR, LANE), _bc(1, LANE)],
        out_specs=[_rb(TR, D), _rb(TR, D), _rb(TR, LANE), _rb(TR, LANE)],
        out_shape=[jax.ShapeDtypeStruct((tp, D), BF16), jax.ShapeDtypeStruct((tp, D), BF16),
                   jax.ShapeDtypeStruct((tp, LANE), F32), jax.ShapeDtypeStruct((tp, LANE), F32)],
        compiler_params=_cparams(("parallel",)),
    )(zmain, zmain, zmain, conv_w, zg, b_gate)


def qkconv_bwd_a(zmain, dqf, dqb, dkf, dkb, conv_w, n_ctx_tiles):
    tp = zmain.shape[0]
    nt = tp // TR
    w2 = 2 * D
    prev_s, next_s = _halo_specs(TR, w2, CB_Q // 2, tp, 16)

    def body(z_ref, zp_ref, zn_ref, w_ref, dqf_ref, dqb_ref, dkf_ref, dkb_ref, dc_ref, dw_ref):
        i = pl.program_id(0)
        first, last = _seq_edges(i, n_ctx_tiles, nt)
        z = z_ref[...].astype(F32)
        pr = jnp.where(first, 0.0, zp_ref[15:16, :].astype(F32))
        nx = jnp.where(last, 0.0, zn_ref[0:1, :].astype(F32))
        zm1, zp1 = _shift_rows(z, pr, nx)
        cv = w_ref[0:1, :] * zm1 + w_ref[1:2, :] * z + w_ref[2:3, :] * zp1
        da = jnp.concatenate(
            [(dqf_ref[...].astype(F32) + dqb_ref[...].astype(F32)) * (DH ** -0.5),
             dkf_ref[...].astype(F32) + dkb_ref[...].astype(F32)], axis=1)
        dc = da * _dsilu(cv)
        dc_ref[...] = dc.astype(BF16)

        @pl.when(i == 0)
        def _():
            dw_ref[...] = jnp.zeros_like(dw_ref)

        dw_ref[0:1, :] += jnp.sum(zm1 * dc, axis=0, keepdims=True)
        dw_ref[1:2, :] += jnp.sum(z * dc, axis=0, keepdims=True)
        dw_ref[2:3, :] += jnp.sum(zp1 * dc, axis=0, keepdims=True)

    return pl.pallas_call(
        body, name="qkconv_bwd_a", grid=(nt,),
        in_specs=[_rb(TR, w2, CB_Q // 2), prev_s, next_s, _bc(8, w2), _rb(TR, D), _rb(TR, D), _rb(TR, D), _rb(TR, D)],
        out_specs=[_rb(TR, w2), _bc(8, w2)],
        out_shape=[jax.ShapeDtypeStruct((tp, w2), BF16), jax.ShapeDtypeStruct((8, w2), F32)],
        compiler_params=_cparams(("arbitrary",)),
    )(zmain, zmain, zmain, conv_w, dqf, dqb, dkf, dkb)


def qkconv_bwd_b(dz, dc, conv_w, n_ctx_tiles):
    tp = dc.shape[0]
    nt = tp // TR
    w2 = 2 * D
    prev_s, next_s = _halo_specs(TR, w2, 0, tp, 16)

    def body(dz_in_ref, d_ref, dp_ref, dn_ref, w_ref, o_ref):
        del dz_in_ref
        i = pl.program_id(0)
        first, last = _seq_edges(i, n_ctx_tiles, nt)
        d = d_ref[...].astype(F32)
        pr = jnp.where(first, 0.0, dp_ref[15:16, :].astype(F32))
        nx = jnp.where(last, 0.0, dn_ref[0:1, :].astype(F32))
        dm1, dp1 = _shift_rows(d, pr, nx)
        o_ref[...] = (w_ref[0:1, :] * dp1 + w_ref[1:2, :] * d + w_ref[2:3, :] * dm1).astype(BF16)

    return pl.pallas_call(
        body, name="qkconv_bwd_b", grid=(nt,),
        in_specs=[pl.BlockSpec(memory_space=pl.ANY), _rb(TR, w2), prev_s, next_s, _bc(8, w2)],
        out_specs=_rb(TR, w2, CB_Q // 2),
        out_shape=jax.ShapeDtypeStruct(dz.shape, BF16),
        input_output_aliases={0: 0},
        compiler_params=_cparams(("parallel",)),
    )(dz, dc, dc, dc, conv_w)


def add_into_dz(dz, a, b, col):
    tp = a.shape[0]

    def body(dz_in_ref, a_ref, b_ref, o_ref):
        del dz_in_ref
        o_ref[...] = (a_ref[...].astype(F32) + b_ref[...].astype(F32)).astype(BF16)

    return pl.pallas_call(
        body, name="add_into_dz", grid=(tp // TR,),
        in_specs=[pl.BlockSpec(memory_space=pl.ANY), _rb(TR, D), _rb(TR, D)],
        out_specs=_rb(TR, D, col),
        out_shape=jax.ShapeDtypeStruct(dz.shape, BF16),
        input_output_aliases={0: 0},
        compiler_params=_cparams(("parallel",)),
    )(dz, a, b)


def _chunk_maps(nc, ncc):
    def fwd(t):
        return t

    def bwd(t):
        return jnp.where(t < ncc, ncc - 1 - t, nc - 1 + ncc - t)

    return fwd, bwd


def _mlstm_chunk(d, h, gc, gr, q_ref, k_ref, v_ref, cp, npv, m_prev, mask):
    ic, bcol = 8 * d + h, 8 * d + 4 + h
    i_col, b_col = gc[:, ic:ic + 1], gc[:, bcol:bcol + 1]
    i_row, b_row = gr[ic:ic + 1, :], gr[bcol:bcol + 1, :]
    g = b_row[:, LC - 1:LC] if d == 0 else b_row[:, 0:1]
    a_row = g - b_row + i_row
    m_loc = jnp.max(a_row, axis=1, keepdims=True)
    dmat = jnp.where(mask, b_col - b_row + i_row, -jnp.inf)
    inter = b_col + m_prev
    m_row = jnp.maximum(inter, jnp.max(dmat, axis=1, keepdims=True))
    e = jnp.exp(dmat - m_row)
    w = jnp.exp(inter - m_row)
    hs = slice(h * DH, (h + 1) * DH)
    qh, kh, vh = q_ref[:, hs], k_ref[:, hs], v_ref[:, hs]
    p = _dot_nt(qh, kh)
    s = p * e
    cpb = cp.astype(BF16)
    qc = _dot(qh, cpb)
    num = _dot(s, vh) + w * qc
    qn = jnp.sum(qh.astype(F32) * npv, axis=1, keepdims=True)
    den = jnp.sum(s, axis=1, keepdims=True) + w * qn
    thr = jnp.exp(-m_row)
    m_new = jnp.maximum(g + m_prev, m_loc)
    a_old = jnp.exp(g + m_prev - m_new)
    a_col = g - b_col + i_col
    return dict(qh=qh, kh=kh, vh=vh, e=e, w=w, s=s, cpb=cpb, qc=qc, num=num, qn=qn, den=den, thr=thr,
                m_loc=m_loc, m_new=m_new, a_old=a_old, a_col=a_col, hs=hs)


def mlstm_fwd(qa, ka, zmain, gcol, grow, ncc):
    tp = qa.shape[0]
    nc = tp // LC
    cf, cb = _chunk_maps(nc, ncc)

    def body(qf, kf, vf, gcf, grf, qb, kb, vb, gcb, grb,
             hf_o, hb_o, cf_o, cb_o, nf_o, nb_o, mf_o, mb_o, c_sc, n_sc, m_sc):
        t = pl.program_id(0)

        @pl.when(t == 0)
        def _():
            c_sc[...] = jnp.zeros_like(c_sc)
            n_sc[...] = jnp.zeros_like(n_sc)
            m_sc[...] = jnp.full(m_sc.shape, M_INIT, F32)

        row = lax.broadcasted_iota(jnp.int32, (LC, LC), 0)
        col = lax.broadcasted_iota(jnp.int32, (LC, LC), 1)
        dirs = ((qf, kf, vf, gcf, grf, hf_o, cf_o, nf_o, mf_o), (qb, kb, vb, gcb, grb, hb_o, cb_o, nb_o, mb_o))
        for d, (q_ref, k_ref, v_ref, gc_ref, gr_ref, h_o, c_o, n_o, m_o) in enumerate(dirs):
            mask = (col <= row) if d == 0 else (col >= row)
            gc = gc_ref[...]
            gr = gr_ref[0]
            for h in range(NH):
                idx = d * NH + h
                cp = c_sc[idx]
                npv = n_sc[idx]
                m_full = m_sc[idx]
                m_prev = m_full[:, 0:1]
                r = _mlstm_chunk(d, h, gc, gr, q_ref, k_ref, v_ref, cp, npv, m_prev, mask)
                hs = r["hs"]
                h_o[:, hs] = (r["num"] / jnp.maximum(jnp.abs(r["den"]), r["thr"])).astype(BF16)
                c_o[0, hs, :] = r["cpb"]
                n_o[0, h:h + 1, :] = npv
                m_o[0, h:h + 1, :] = m_full
                a_new = jnp.exp(r["m_loc"] - r["m_new"])
                kw = r["kh"].astype(F32) * jnp.exp(r["a_col"] - r["m_loc"])
                kv = _dot_tn_mxu(kw, r["vh"])
                kn = jnp.sum(kw, axis=0, keepdims=True)
                c_sc[idx] = r["a_old"] * cp + a_new * kv
                n_sc[idx] = r["a_old"] * npv + a_new * kn
                m_sc[idx] = jnp.broadcast_to(r["m_new"], (1, LANE))

    def dspecs(cm):
        return [pl.BlockSpec((LC, D), lambda t: (cm(t), 0)),
                pl.BlockSpec((LC, D), lambda t: (cm(t), 0)),
                pl.BlockSpec((LC, D), lambda t: (cm(t), CB_V)),
                pl.BlockSpec((LC, LANE), lambda t: (cm(t), 0)),
                pl.BlockSpec((1, 16, LC), lambda t: (cm(t), 0, 0))]

    def ospec(cm, shp):
        return pl.BlockSpec((1,) + shp, lambda t: (cm(t), 0, 0))

    return pl.pallas_call(
        body, name="mlstm_fwd", grid=(nc,),
        in_specs=dspecs(cf) + dspecs(cb),
        out_specs=[pl.BlockSpec((LC, D), lambda t: (cf(t), 0)), pl.BlockSpec((LC, D), lambda t: (cb(t), 0)),
                   ospec(cf, (D, DH)), ospec(cb, (D, DH)), ospec(cf, (NH, DH)), ospec(cb, (NH, DH)),
                   ospec(cf, (NH, LANE)), ospec(cb, (NH, LANE))],
        out_shape=[jax.ShapeDtypeStruct((tp, D), BF16), jax.ShapeDtypeStruct((tp, D), BF16),
                   jax.ShapeDtypeStruct((nc, D, DH), BF16), jax.ShapeDtypeStruct((nc, D, DH), BF16),
                   jax.ShapeDtypeStruct((nc, NH, DH), F32), jax.ShapeDtypeStruct((nc, NH, DH), F32),
                   jax.ShapeDtypeStruct((nc, NH, LANE), F32), jax.ShapeDtypeStruct((nc, NH, LANE), F32)],
        scratch_shapes=[pltpu.VMEM((2 * NH, DH, DH), F32), pltpu.VMEM((2 * NH, 1, DH), F32),
                        pltpu.VMEM((2 * NH, 1, LANE), F32)],
        compiler_params=_cparams(("arbitrary",)),
    )(qa, ka, zmain, gcol, grow, qa, ka, zmain, gcol, grow)


def mlstm_bwd(qa, ka, zmain, gcol, grow, states, dhm, ncc):
    tp = qa.shape[0]
    nc = tp // LC
    cf0, cb0 = _chunk_maps(nc, ncc)
    cf = lambda t: cf0(nc - 1 - t)
    cb = lambda t: cb0(nc - 1 - t)
    csf, csb, nsf, nsb, msf, msb = states

    def body(qf, kf, vf, gcf, grf, cpf, npf, mpf, dhf, qb, kb, vb, gcb, grb, cpb_, npb, mpb, dhb,
             dqf_o, dkf_o, dvf_o, colf_o, rowf_o, dqb_o, dkb_o, dvb_o, colb_o, rowb_o, dc_sc, dn_sc):
        t = pl.program_id(0)

        @pl.when(t == 0)
        def _():
            dc_sc[...] = jnp.zeros_like(dc_sc)
            dn_sc[...] = jnp.zeros_like(dn_sc)

        row = lax.broadcasted_iota(jnp.int32, (LC, LC), 0)
        col = lax.broadcasted_iota(jnp.int32, (LC, LC), 1)
        dirs = ((qf, kf, vf, gcf, grf, cpf, npf, mpf, dhf, dqf_o, dkf_o, dvf_o, colf_o, rowf_o, cf),
                (qb, kb, vb, gcb, grb, cpb_, npb, mpb, dhb, dqb_o, dkb_o, dvb_o, colb_o, rowb_o, cb))
        for d, (q_ref, k_ref, v_ref, gc_ref, gr_ref, cp_ref, np_ref, mp_ref, dh_ref,
                dq_o, dk_o, dv_o, col_o, row_o, cm) in enumerate(dirs):
            mask = (col <= row) if d == 0 else (col >= row)
            live = jnp.where(cm(t) >= ncc, 1.0, 0.0).astype(F32)
            gc = gc_ref[...]
            gr = gr_ref[0]
            col_o[...] = jnp.zeros_like(col_o)
            row_o[...] = jnp.zeros_like(row_o)
            for h in range(NH):
                idx = d * NH + h
                hs = slice(h * DH, (h + 1) * DH)
                cp = cp_ref[0, hs, :]
                npv = np_ref[0, h:h + 1, :]
                m_prev = mp_ref[0, h:h + 1, 0:1]
                r = _mlstm_chunk(d, h, gc, gr, q_ref, k_ref, v_ref, cp, npv, m_prev, mask)
                qh, kh, vh, e, w, s = r["qh"], r["kh"], r["vh"], r["e"], r["w"], r["s"]
                qf32, kf32 = qh.astype(F32), kh.astype(F32)
                den, thr = r["den"], r["thr"]
                rden = 1.0 / jnp.maximum(jnp.abs(den), thr)
                hh = r["num"] * rden
                dh = dh_ref[:, hs].astype(F32) * live
                dnum = dh * rden
                sgn = jnp.where(jnp.abs(den) > thr, jnp.sign(den), 0.0)
                dden = -jnp.sum(dh * hh, axis=1, keepdims=True) * rden * sgn
                ds = _dot_nt(dnum, vh) + dden
                dp = ds * e
                gm = ds * s
                rowsum = jnp.sum(gm, axis=1, keepdims=True)
                colsum = jnp.sum(gm, axis=0, keepdims=True)
                dq = _dot(dp, kh) + w * (_dot_nt(dnum, r["cpb"]) + dden * npv)
                dcs = dc_sc[idx]
                dns = dn_sc[idx]
                kfac = jnp.exp(r["a_col"] - r["m_new"])
                vdc = _dot_nt(vh, dcs)
                dk = _dot_tn(dp, qh) + kfac * (vdc + dns)
                dv = _dot_tn(s, dnum) + kfac * _dot(kh, dcs)
                beta = w * (jnp.sum(dnum * r["qc"], axis=1, keepdims=True) + dden * r["qn"])
                alpha = kfac * (jnp.sum(kf32 * vdc, axis=1, keepdims=True) + jnp.sum(kf32 * dns, axis=1, keepdims=True))
                dq_o[:, hs] = dq.astype(BF16)
                dk_o[:, hs] = dk.astype(BF16)
                dv_o[:, hs] = dv.astype(BF16)
                cpf = r["cpb"].astype(F32)
                inner = (jnp.sum(jnp.sum(dcs * cpf, axis=1, keepdims=True), axis=0, keepdims=True)
                         + jnp.sum(dns * npv, axis=1, keepdims=True))
                gam = jnp.sum(alpha, axis=0, keepdims=True) + r["a_old"] * inner
                lo = 8 * d + h
                col_o[:, lo:lo + 1] = alpha
                col_o[:, lo + 4:lo + 5] = rowsum + beta - alpha
                col_o[:, lo + 36:lo + 37] = jnp.broadcast_to(gam, (LC, 1))
                row_o[0, h:h + 1, :] = colsum
                wq = qf32 * w
                dc_sc[idx] = r["a_old"] * dcs + _dot_tn(wq, dnum)
                dn_sc[idx] = r["a_old"] * dns + jnp.sum(wq * dden, axis=0, keepdims=True)

    def dspecs(cm):
        return [pl.BlockSpec((LC, D), lambda t: (cm(t), 0)),
                pl.BlockSpec((LC, D), lambda t: (cm(t), 0)),
                pl.BlockSpec((LC, D), lambda t: (cm(t), CB_V)),
                pl.BlockSpec((LC, LANE), lambda t: (cm(t), 0)),
                pl.BlockSpec((1, 16, LC), lambda t: (cm(t), 0, 0)),
                pl.BlockSpec((1, D, DH), lambda t: (cm(t), 0, 0)),
                pl.BlockSpec((1, NH, DH), lambda t: (cm(t), 0, 0)),
                pl.BlockSpec((1, NH, LANE), lambda t: (cm(t), 0, 0)),
                pl.BlockSpec((LC, D), lambda t: (jnp.maximum(cm(t) - ncc, 0), 0))]

    def ospecs(cm):
        return [pl.BlockSpec((LC, D), lambda t: (cm(t), 0)),
                pl.BlockSpec((LC, D), lambda t: (cm(t), 0)),
                pl.BlockSpec((LC, D), lambda t: (cm(t), 0)),
                pl.BlockSpec((LC, LANE), lambda t: (cm(t), 0)),
                pl.BlockSpec((1, 8, LC), lambda t: (cm(t), 0, 0))]

    oshape = [jax.ShapeDtypeStruct((tp, D), BF16)] * 3 + [jax.ShapeDtypeStruct((tp, LANE), F32),
                                                        jax.ShapeDtypeStruct((nc, 8, LC), F32)]
    return pl.pallas_call(
        body, name="mlstm_bwd", grid=(nc,),
        in_specs=dspecs(cf) + dspecs(cb),
        out_specs=ospecs(cf) + ospecs(cb),
        out_shape=oshape + oshape,
        scratch_shapes=[pltpu.VMEM((2 * NH, DH, DH), F32), pltpu.VMEM((2 * NH, 1, DH), F32)],
        compiler_params=_cparams(("arbitrary",)),
    )(qa, ka, zmain, gcol, grow, csf, nsf, msf, dhm, qa, ka, zmain, gcol, grow, csb, nsb, msb, dhm)


def gates_bwd(colf, colb, cs, zgb):
    tp = colf.shape[0]

    def body(cf_ref, cb_ref, cs_ref, zgb_ref, o_ref, db_ref):
        i = pl.program_id(0)

        @pl.when(i == 0)
        def _():
            db_ref[...] = jnp.zeros_like(db_ref)

        lane = lax.broadcasted_iota(jnp.int32, (TR, LANE), 1)
        i_l = jnp.logical_or(lane < 4, jnp.logical_and(lane >= 8, lane < 12))
        f_l = jnp.logical_or(jnp.logical_and(lane >= 4, lane < 8), jnp.logical_and(lane >= 12, lane < 16))
        cv = cf_ref[...] + cb_ref[...]
        csv = cs_ref[...]
        gam = pltpu.roll(cv, LANE - 32, 1)
        dbh = jnp.where(f_l, cv - csv, 0.0)
        rr = lax.broadcasted_iota(jnp.int32, (TR, TR), 0)
        cc = lax.broadcasted_iota(jnp.int32, (TR, TR), 1)
        same = (rr // LC) == (cc // LC)
        low = jnp.where(jnp.logical_and(same, cc <= rr), 1.0, 0.0).astype(BF16)
        upp = jnp.where(jnp.logical_and(same, cc >= rr), 1.0, 0.0).astype(BF16)
        dlogf = jnp.where(lane < 8, _exact_dot(upp, dbh), _exact_dot(low, dbh)) + gam
        out = jnp.where(i_l, csv + cv, 0.0) + jnp.where(f_l, dlogf * _sigmoid(-zgb_ref[...]), 0.0)
        o_ref[...] = out
        db_ref[...] += jnp.sum(out, axis=0, keepdims=True)

    spec = _rb(TR, LANE)
    return pl.pallas_call(
        body, name="gates_bwd", grid=(tp // TR,),
        in_specs=[spec] * 4,
        out_specs=[spec, _bc(1, LANE)],
        out_shape=[jax.ShapeDtypeStruct((tp, LANE), F32), jax.ShapeDtypeStruct((1, LANE), F32)],
        compiler_params=_cparams(("arbitrary",)),
    )(colf, colb, cs, zgb)


def _head_norm(hm, gh):
    xs, rs = [], []
    for h in range(NH):
        seg = hm[:, h * DH:(h + 1) * DH]
        r = lax.rsqrt(jnp.mean(seg * seg, axis=-1, keepdims=True) + EPS)
        xs.append(seg * r)
        rs.append(r)
    xh = jnp.concatenate(xs, axis=1)
    return xh, rs, xh * gh


def _sgu_norm(zvg, ln_g, ln_b):
    vg = _gelu(zvg)
    mu = jnp.mean(vg, axis=-1, keepdims=True)
    vc = vg - mu
    rstd = lax.rsqrt(jnp.mean(vc * vc, axis=-1, keepdims=True) + EPS)
    vhat = vc * rstd
    return vhat, rstd, vhat * ln_g + ln_b


def _sgu_mix(vn, ws_ref, bs_ref):
    rows = []
    for c in range(TR // SCH):
        cols = []
        for g in range(SG):
            blk = vn[c * SCH:(c + 1) * SCH, g * SGD:(g + 1) * SGD]
            cols.append(_dot(ws_ref[g * SCH:(g + 1) * SCH, :], blk) + bs_ref[:, g:g + 1])
        rows.append(jnp.concatenate(cols, axis=1))
    return jnp.concatenate(rows, axis=0)


def mixer_fwd(hf, hb, zmain, gh, ln_g, ln_b, ws, bs_t, n_ctx_tiles):
    t = hf.shape[0] - n_ctx_tiles * TR
    off = n_ctx_tiles

    def body(hf_ref, hb_ref, zo_ref, zu_ref, zv_ref, gh_ref, lg_ref, lb_ref, ws_ref, bs_ref, o_ref):
        hm = hf_ref[...].astype(F32) + hb_ref[...].astype(F32)
        _, _, hn = _head_norm(hm, gh_ref[...])
        o_ref[0] = (_sigmoid(zo_ref[...].astype(F32)) * hn).astype(BF16)
        _, _, vn = _sgu_norm(zv_ref[...].astype(F32), lg_ref[...], lb_ref[...])
        mixed = _sgu_mix(vn, ws_ref, bs_ref)
        o_ref[1] = (_gelu(zu_ref[...].astype(F32)) * mixed).astype(BF16)

    return pl.pallas_call(
        body, name="mixer_fwd", grid=(t // TR,),
        in_specs=[_rb(TR, D, 0, off), _rb(TR, D, 0, off), _rb(TR, D, CB_O, off), _rb(TR, D, CB_U, off),
                  _rb(TR, D, CB_VG, off), _bc(1, D), _bc(1, D), _bc(1, D), _bc(SG * SCH, SCH), _bc(SCH, LANE)],
        out_specs=pl.BlockSpec((2, TR, D), lambda i: (0, i, 0)),
        out_shape=jax.ShapeDtypeStruct((2, t, D), BF16),
        compiler_params=_cparams(("parallel",)),
    )(hf, hb, zmain, zmain, zmain, gh, ln_g, ln_b, ws, bs_t)


def merge_fwd(zmain, pp, n_ctx_tiles):
    t = pp.shape[1]
    off = n_ctx_tiles

    def body(zgm_ref, zgg_ref, pp_ref, o_ref):
        y = (_sigmoid(zgm_ref[...].astype(F32)) * pp_ref[0].astype(F32)
             + _sigmoid(zgg_ref[...].astype(F32)) * pp_ref[1].astype(F32))
        o_ref[...] = y.astype(BF16)

    return pl.pallas_call(
        body, name="merge_fwd", grid=(t // TR,),
        in_specs=[_rb(TR, D, CB_GM, off), _rb(TR, D, CB_GG, off), pl.BlockSpec((2, TR, D), lambda i: (0, i, 0))],
        out_specs=_rb(TR, D),
        out_shape=jax.ShapeDtypeStruct((t, D), BF16),
        compiler_params=_cparams(("parallel",)),
    )(zmain, zmain, pp)


def merge_bwd(zmain, pp, dy, tp, n_ctx_tiles):
    t = dy.shape[0]
    nt = tp // TR
    xrow = lambda i: jnp.maximum(i - n_ctx_tiles, 0)

    def body(zg_ref, pp_ref, dy_ref, dpp_ref, dz_ref):
        i = pl.program_id(1)
        zg = zg_ref[...].astype(F32)
        sg = _sigmoid(zg)
        dyv = dy_ref[...].astype(F32)
        dpp_ref[0] = (dyv * sg).astype(BF16)
        dzv = dyv * pp_ref[0].astype(F32) * sg * (1.0 - sg)
        dz_ref[...] = jnp.where(i >= n_ctx_tiles, dzv, 0.0).astype(BF16)

    return pl.pallas_call(
        body, name="merge_bwd", grid=(2, nt),
        in_specs=[pl.BlockSpec((TR, D), lambda j, i: (i, CB_GM + j)),
                  pl.BlockSpec((1, TR, D), lambda j, i: (j, xrow(i), 0)),
                  pl.BlockSpec((TR, D), lambda j, i: (xrow(i), 0))],
        out_specs=[pl.BlockSpec((1, TR, D), lambda j, i: (j, xrow(i), 0)),
                   pl.BlockSpec((TR, D), lambda j, i: (i, CB_GM + j))],
        out_shape=[jax.ShapeDtypeStruct((2, t, D), BF16), jax.ShapeDtypeStruct((tp, 8 * D), BF16)],
        compiler_params=_cparams(("arbitrary", "arbitrary")),
    )(zmain, pp, dy)


def mixer_bwd(dz, hf, hb, zmain, dyms, gh, ln_g, ln_b, ws, bs_t, n_ctx_tiles):
    tp = hf.shape[0]
    t = tp - n_ctx_tiles * TR
    nt = tp // TR
    xrow = lambda i: jnp.maximum(i - n_ctx_tiles, 0)

    def body(dz_in_ref, hf_ref, hb_ref, zo_ref, zu_ref, zv_ref, dy_ref, gh_ref, lg_ref, lb_ref, ws_ref, bs_ref,
             dz_ref, dhm_ref, dgh_ref, dlg_ref, dlb_ref, dws_ref, dbs_ref):
        del dz_in_ref
        i = pl.program_id(0)

        @pl.when(i == 0)
        def _():
            for ref in (dgh_ref, dlg_ref, dlb_ref, dws_ref, dbs_ref):
                ref[...] = jnp.zeros_like(ref)

        @pl.when(i < n_ctx_tiles)
        def _():
            dz_ref[...] = jnp.zeros_like(dz_ref)

        @pl.when(i >= n_ctx_tiles)
        def _():
            gh_v = gh_ref[...]
            hm = hf_ref[...].astype(F32) + hb_ref[...].astype(F32)
            xh, rs, hn = _head_norm(hm, gh_v)
            zo = zo_ref[...].astype(F32)
            so = _sigmoid(zo)
            dym = dy_ref[0].astype(F32)
            d_zo = dym * hn * so * (1.0 - so)
            d_hn = dym * so
            dgh_ref[...] += jnp.sum(d_hn * xh, axis=0, keepdims=True)
            d_xh = d_hn * gh_v
            segs = []
            for h in range(NH):
                hs = slice(h * DH, (h + 1) * DH)
                dx, xs = d_xh[:, hs], xh[:, hs]
                segs.append(rs[h] * (dx - xs * jnp.mean(dx * xs, axis=-1, keepdims=True)))
            dhm_ref[...] = jnp.concatenate(segs, axis=1).astype(BF16)
            zu = zu_ref[...].astype(F32)
            zv = zv_ref[...].astype(F32)
            lg = lg_ref[...]
            vhat, rstd, vn = _sgu_norm(zv, lg, lb_ref[...])
            mixed = _sgu_mix(vn, ws_ref, bs_ref)
            dys = dy_ref[1].astype(F32)
            d_zu = dys * mixed * _dgelu(zu)
            d_mixed = dys * _gelu(zu)
            rows = []
            for c in range(TR // SCH):
                cols = []
                for g in range(SG):
                    rsl, csl = slice(c * SCH, (c + 1) * SCH), slice(g * SGD, (g + 1) * SGD)
                    dm = d_mixed[rsl, csl]
                    cols.append(_dot_tn(ws_ref[g * SCH:(g + 1) * SCH, :], dm))
                    dws_ref[g * SCH:(g + 1) * SCH, :] += _dot_nt(dm, vn[rsl, csl])
                    dbs_ref[:, g:g + 1] += jnp.sum(dm, axis=1, keepdims=True)
                rows.append(jnp.concatenate(cols, axis=1))
            d_vn = jnp.concatenate(rows, axis=0)
            dlg_ref[...] += jnp.sum(d_vn * vhat, axis=0, keepdims=True)
            dlb_ref[...] += jnp.sum(d_vn, axis=0, keepdims=True)
            d_vhat = d_vn * lg
            d_vg = rstd * (d_vhat - jnp.mean(d_vhat, axis=-1, keepdims=True)
                           - vhat * jnp.mean(d_vhat * vhat, axis=-1, keepdims=True))
            d_zv = d_vg * _dgelu(zv)
            dz_ref[...] = jnp.concatenate([d_zo, d_zu, d_zv], axis=1).astype(BF16)

    return pl.pallas_call(
        body, name="mixer_bwd", grid=(nt,),
        in_specs=[pl.BlockSpec(memory_space=pl.ANY), _rb(TR, D), _rb(TR, D), _rb(TR, D, CB_O), _rb(TR, D, CB_U),
                  _rb(TR, D, CB_VG), pl.BlockSpec((2, TR, D), lambda i: (0, xrow(i), 0)),
                  _bc(1, D), _bc(1, D), _bc(1, D), _bc(SG * SCH, SCH), _bc(SCH, LANE)],
        out_specs=[_rb(TR, 3 * D), pl.BlockSpec((TR, D), lambda i: (xrow(i), 0)),
                   _bc(1, D), _bc(1, D), _bc(1, D), _bc(SG * SCH, SCH), _bc(SCH, LANE)],
        out_shape=[jax.ShapeDtypeStruct(dz.shape, BF16), jax.ShapeDtypeStruct((t, D), BF16),
                   jax.ShapeDtypeStruct((1, D), F32), jax.ShapeDtypeStruct((1, D), F32),
                   jax.ShapeDtypeStruct((1, D), F32), jax.ShapeDtypeStruct((SG * SCH, SCH), F32),
                   jax.ShapeDtypeStruct((SCH, LANE), F32)],
        input_output_aliases={0: 0},
        compiler_params=_cparams(("arbitrary",)),
    )(dz, hf, hb, zmain, zmain, zmain, dyms, gh, ln_g, ln_b, ws, bs_t)


FCB = DFF // 2
TF = 512


def _ffn_halo(col, t):
    per = TF // GW
    last = t // GW - 1
    prev = pl.BlockSpec((GW, FCB), lambda i, j: (jnp.maximum(i * per - 1, 0), col(j)))
    nxt = pl.BlockSpec((GW, FCB), lambda i, j: (jnp.minimum((i + 1) * per, last), col(j)))
    return prev, nxt


def _conv_taps(ext):
    n = ext.shape[0]
    colid = lax.broadcasted_iota(jnp.int32, (n, 1), 0) % GW
    left = pltpu.roll(jnp.where(colid != GW - 1, ext, 0.0), 1, 0)
    right = pltpu.roll(jnp.where(colid != 0, ext, 0.0), n - 1, 0)
    views = (left, ext, right)
    return {(ky, kx): views[kx][GW * ky:GW * ky + TF] for ky in range(3) for kx in range(3)}


def _ext(c_ref, p_ref, n_ref, i, nt):
    pr = jnp.where(i == 0, 0.0, p_ref[...].astype(F32))
    nx = jnp.where(i == nt - 1, 0.0, n_ref[...].astype(F32))
    return jnp.concatenate([pr, c_ref[...].astype(F32), nx], axis=0)


def ffn_act_fwd(up, wc):
    t = up.shape[0]
    nt = t // TF
    prev_s, next_s = _ffn_halo(lambda j: j, t)

    def body(a_ref, ap_ref, an_ref, b_ref, w_ref, o_ref, ac_ref):
        i = pl.program_id(0)
        taps = _conv_taps(_ext(a_ref, ap_ref, an_ref, i, nt))
        ac = sum(w_ref[3 * ky + kx:3 * ky + kx + 1, :] * taps[(ky, kx)] for ky in range(3) for kx in range(3))
        ac_ref[...] = ac.astype(BF16)
        o_ref[...] = (_silu(ac) * b_ref[...].astype(F32)).astype(BF16)

    spec = pl.BlockSpec((TF, FCB), lambda i, j: (i, j))
    return pl.pallas_call(
        body, name="ffn_act_fwd", grid=(nt, 2),
        in_specs=[spec, prev_s, next_s,
                  pl.BlockSpec((TF, FCB), lambda i, j: (i, 2 + j)), pl.BlockSpec((16, FCB), lambda i, j: (0, j))],
        out_specs=[spec, spec],
        out_shape=[jax.ShapeDtypeStruct((t, DFF), BF16), jax.ShapeDtypeStruct((t, DFF), BF16)],
        compiler_params=_cparams(("parallel", "parallel")),
    )(up, up, up, up, wc)


def ffn_act_bwd(up, ac, dact):
    t = up.shape[0]
    nt = t // TF

    def body(b_ref, ac_ref, da_ref, dup_ref, dac_ref):
        acv = ac_ref[...].astype(F32)
        da = da_ref[...].astype(F32)
        s = _sigmoid(acv)
        dup_ref[...] = (da * acv * s).astype(BF16)
        dac_ref[...] = (da * b_ref[...].astype(F32) * s * (1.0 + acv * (1.0 - s))).astype(BF16)

    spec = pl.BlockSpec((TF, FCB), lambda i, j: (i, j))
    bspec = pl.BlockSpec((TF, FCB), lambda i, j: (i, 2 + j))
    return pl.pallas_call(
        body, name="ffn_act_bwd", grid=(nt, 2),
        in_specs=[bspec, spec, spec],
        out_specs=[bspec, spec],
        out_shape=[jax.ShapeDtypeStruct((t, 2 * DFF), BF16), jax.ShapeDtypeStruct((t, DFF), BF16)],
        compiler_params=_cparams(("parallel", "parallel")),
    )(up, ac, dact)


def ffn_conv_bwd(dup, up, dac, wc):
    t = up.shape[0]
    nt = t // TF
    prev_g, next_g = _ffn_halo(lambda j: j, t)

    def body(dup_in_ref, a_ref, g_ref, gp_ref, gn_ref, w_ref, o_ref, dw_ref):
        del dup_in_ref
        i = pl.program_id(1)

        @pl.when(i == 0)
        def _():
            dw_ref[...] = jnp.zeros_like(dw_ref)

        gtaps = _conv_taps(_ext(g_ref, gp_ref, gn_ref, i, nt))
        a = a_ref[...].astype(F32)
        acc = None
        for ky in range(3):
            for kx in range(3):
                k = 3 * ky + kx
                kf = 3 * (2 - ky) + (2 - kx)
                tap = gtaps[(ky, kx)]
                term = w_ref[kf:kf + 1, :] * tap
                acc = term if acc is None else acc + term
                dw_ref[kf:kf + 1, :] += jnp.sum(a * tap, axis=0, keepdims=True)
        o_ref[...] = acc.astype(BF16)

    sw = lambda s: pl.BlockSpec(s.block_shape, lambda j, i, f=s.index_map: f(i, j))
    spec = pl.BlockSpec((TF, FCB), lambda j, i: (i, j))
    return pl.pallas_call(
        body, name="ffn_conv_bwd", grid=(2, nt),
        in_specs=[pl.BlockSpec(memory_space=pl.ANY), spec, spec, sw(prev_g), sw(next_g),
                  pl.BlockSpec((16, FCB), lambda j, i: (0, j))],
        out_specs=[spec, pl.BlockSpec((16, FCB), lambda j, i: (0, j))],
        out_shape=[jax.ShapeDtypeStruct(dup.shape, BF16), jax.ShapeDtypeStruct((16, DFF), F32)],
        input_output_aliases={0: 0},
        compiler_params=_cparams(("arbitrary", "arbitrary")),
    )(dup, up, dac, dac, dac, wc)


def head_fwd_bwd(h1, f, target, gfin, tab):
    t = h1.shape[0]

    def body(h1_ref, f_ref, t_ref, g_ref, tab_ref, dh2_ref, df_ref, acc_ref):
        i = pl.program_id(0)

        @pl.when(i == 0)
        def _():
            acc_ref[...] = jnp.zeros_like(acc_ref)

        gate = tab_ref[0:1, :]
        fv = f_ref[...].astype(F32)
        h2 = h1_ref[...] + gate * fv
        r = lax.rsqrt(jnp.mean(h2 * h2, axis=-1, keepdims=True) + EPS)
        xh = h2 * r
        gv = g_ref[...]
        err = xh * gv - t_ref[...]
        acc_ref[0:1, :] += jnp.sum(0.5 * jnp.mean(err * err, axis=-1, keepdims=True), axis=0, keepdims=True)
        dy = err * (1.0 / D)
        acc_ref[1:2, :] += jnp.sum(dy * xh, axis=0, keepdims=True)
        dxh = dy * gv
        dh2 = r * (dxh - xh * jnp.mean(dxh * xh, axis=-1, keepdims=True))
        dh2_ref[...] = dh2
        acc_ref[2:3, :] += jnp.sum(dh2 * fv, axis=0, keepdims=True)
        df_ref[...] = (dh2 * gate).astype(BF16)

    return pl.pallas_call(
        body, name="head_fwd_bwd", grid=(t // TR2,),
        in_specs=[_rb(TR2, D), _rb(TR2, D), _rb(TR2, D), _bc(1, D), _bc(8, D)],
        out_specs=[_rb(TR2, D), _rb(TR2, D), _bc(8, D)],
        out_shape=[jax.ShapeDtypeStruct((t, D), F32), jax.ShapeDtypeStruct((t, D), BF16),
                   jax.ShapeDtypeStruct((8, D), F32)],
        compiler_params=_cparams(("arbitrary",)),
    )(h1, f, target, gfin, tab)


def norm2_bwd(h1, dhn2, dh2, out, g, tab):
    t = h1.shape[0]

    def body(h1_ref, dhn_ref, dh2_ref, out_ref, g_ref, tab_ref, dh1_ref, dout_ref, acc_ref):
        i = pl.program_id(0)

        @pl.when(i == 0)
        def _():
            acc_ref[...] = jnp.zeros_like(acc_ref)

        h1v = h1_ref[...]
        r = lax.rsqrt(jnp.mean(h1v * h1v, axis=-1, keepdims=True) + EPS)
        xh = h1v * r
        gv = g_ref[...]
        dhn = dhn_ref[...].astype(F32)
        acc_ref[0:1, :] += jnp.sum(dhn, axis=0, keepdims=True)
        acc_ref[1:2, :] += jnp.sum(dhn * xh * gv, axis=0, keepdims=True)
        dn = dhn * (1.0 + tab_ref[2:3, :])
        acc_ref[2:3, :] += jnp.sum(dn * xh, axis=0, keepdims=True)
        dxh = dn * gv
        dh1 = dh2_ref[...] + r * (dxh - xh * jnp.mean(dxh * xh, axis=-1, keepdims=True))
        dh1_ref[...] = dh1
        acc_ref[3:4, :] += jnp.sum(dh1 * out_ref[...].astype(F32), axis=0, keepdims=True)
        dout_ref[...] = (dh1 * tab_ref[0:1, :]).astype(BF16)

    return pl.pallas_call(
        body, name="norm2_bwd", grid=(t // TR2,),
        in_specs=[_rb(TR2, D), _rb(TR2, D), _rb(TR2, D), _rb(TR2, D), _bc(1, D), _bc(8, D)],
        out_specs=[_rb(TR2, D), _rb(TR2, D), _bc(8, D)],
        out_shape=[jax.ShapeDtypeStruct((t, D), F32), jax.ShapeDtypeStruct((t, D), BF16),
                   jax.ShapeDtypeStruct((8, D), F32)],
        compiler_params=_cparams(("arbitrary",)),
    )(h1, dhn2, dh2, out, g, tab)


def norm1_bwd(ctx, x, da, db, dh1, g, tab, n_ctx_tiles):
    t = x.shape[0]
    tp = t + ctx.shape[0]
    xrow = lambda i: jnp.maximum(i - n_ctx_tiles, 0)

    def body(c_ref, x_ref, da_ref, db_ref, dh1_ref, g_ref, tab_ref, dx_ref, acc_ref):
        i = pl.program_id(0)

        @pl.when(i == 0)
        def _():
            acc_ref[...] = jnp.zeros_like(acc_ref)

        is_ctx = i < n_ctx_tiles
        x = jnp.where(is_ctx, c_ref[...], x_ref[...])
        r = lax.rsqrt(jnp.mean(x * x, axis=-1, keepdims=True) + EPS)
        xh = x * r
        gv = g_ref[...]
        dhn = da_ref[...].astype(F32) + db_ref[...]
        s_shift = jnp.sum(dhn, axis=0, keepdims=True)
        s_scale = jnp.sum(dhn * xh * gv, axis=0, keepdims=True)

        @pl.when(is_ctx)
        def _():
            acc_ref[0:1, :] += s_shift
            acc_ref[1:2, :] += s_scale

        @pl.when(jnp.logical_not(is_ctx))
        def _():
            acc_ref[2:3, :] += s_shift
            acc_ref[3:4, :] += s_scale

        acc_ref[5:6, :] += s_shift
        acc_ref[6:7, :] += s_scale
        sc = jnp.where(is_ctx, tab_ref[1:2, :], tab_ref[3:4, :])
        dn = dhn * (1.0 + sc)
        acc_ref[4:5, :] += jnp.sum(dn * xh, axis=0, keepdims=True)
        dxh = dn * gv
        dx_ref[...] = dh1_ref[...] + r * (dxh - xh * jnp.mean(dxh * xh, axis=-1, keepdims=True))

    return pl.pallas_call(
        body, name="norm1_bwd", grid=(tp // TR,),
        in_specs=_ctx_x_specs(n_ctx_tiles) + [_rb(TR, D), _rb(TR, D), pl.BlockSpec((TR, D), lambda i: (xrow(i), 0)),
                                              _bc(1, D), _bc(8, D)],
        out_specs=[pl.BlockSpec((TR, D), lambda i: (xrow(i), 0)), _bc(8, D)],
        out_shape=[jax.ShapeDtypeStruct((t, D), F32), jax.ShapeDtypeStruct((8, D), F32)],
        compiler_params=_cparams(("arbitrary",)),
    )(ctx, x, da, db, dh1, g, tab)


def adamw(w, g, m, v, name):
    lead = w.ndim - 2
    rows, cols = w.shape[-2:]
    tm = max(t for t in range(8, rows + 1, 8) if rows % t == 0 and (t * cols <= 512 * 1024 or t == 8))
    c1 = 1.0 / (1.0 - ADAM_B1 ** ADAM_STEP)
    c2 = 1.0 / (1.0 - ADAM_B2 ** ADAM_STEP)

    def body(w_ref, g_ref, m_ref, v_ref, d_ref, mo_ref, vo_ref):
        gv = g_ref[...]
        mn = ADAM_B1 * m_ref[...] + (1.0 - ADAM_B1) * gv
        vn = ADAM_B2 * v_ref[...] + (1.0 - ADAM_B2) * (gv * gv)
        mo_ref[...] = mn
        vo_ref[...] = vn
        d_ref[...] = -ADAM_LR * ((mn * c1) / (jnp.sqrt(vn * c2) + ADAM_EPS) + ADAM_WD * w_ref[...])

    spec = pl.BlockSpec((1,) * lead + (tm, cols), lambda i: (0,) * lead + (i, 0))
    sds = jax.ShapeDtypeStruct(w.shape, F32)
    return pl.pallas_call(
        body, name=name, grid=(rows // tm,),
        in_specs=[spec] * 4, out_specs=[spec] * 3, out_shape=[sds] * 3,
        compiler_params=_cparams(("parallel",)),
    )(w, g, m, v)


def add_n(arrs, out_dtype, name):
    shp = arrs[0].shape
    cols = shp[-1]
    flat = [a.reshape(-1, cols) for a in arrs]
    rows = flat[0].shape[0]
    tm = max(t for t in range(16, rows + 1, 16) if rows % t == 0 and t * cols <= 512 * 1024)

    def body(*refs):
        acc = refs[0][...].astype(F32)
        for r in refs[1:-1]:
            acc = acc + r[...].astype(F32)
        refs[-1][...] = acc.astype(refs[-1].dtype)

    spec = pl.BlockSpec((tm, cols), lambda i: (i, 0))
    out = pl.pallas_call(
        body, name=name, grid=(rows // tm,),
        in_specs=[spec] * len(flat), out_specs=spec, out_shape=jax.ShapeDtypeStruct((rows, cols), out_dtype),
        compiler_params=_cparams(("parallel",)),
    )(*flat)
    return out.reshape(shp)


def sum8(stack, name):
    _, rows, cols = stack.shape
    tm = rows if rows <= 2048 else _pick(rows, (512, 256, 128, 64, 8))

    def body(s_ref, o_ref):
        acc = s_ref[0]
        for k in range(1, 8):
            acc = acc + s_ref[k]
        o_ref[...] = acc

    return pl.pallas_call(
        body, name=name, grid=(rows // tm,),
        in_specs=[pl.BlockSpec((8, tm, cols), lambda i: (0, i, 0))],
        out_specs=pl.BlockSpec((tm, cols), lambda i: (i, 0)),
        out_shape=jax.ShapeDtypeStruct((rows, cols), F32),
        compiler_params=_cparams(("parallel",)),
    )(stack)


def _coords():
    return lax.axis_index("x"), lax.axis_index("y"), lax.axis_index("c")


def _other_chips(x, y):
    return [(1 - x, y), (x, 1 - y), (1 - x, 1 - y)]


_ANY = pl.BlockSpec(memory_space=pl.ANY)


def gather_chips(slabs):
    ns = len(slabs)

    def body(*refs):
        x_refs, out_refs = refs[:ns], refs[ns:2 * ns]
        send_sems, recv_sems = refs[2 * ns:]
        x, y, c = _coords()
        me = 2 * x + y
        sibling = (x, y, 1 - c)
        chips = _other_chips(x, y)

        def half(s, chip, hc):
            rh = slabs[s].shape[0] // 2
            return out_refs[s].at[chip, pl.ds(hc * rh, rh), :]

        def own_half(s):
            rh = slabs[s].shape[0] // 2
            return x_refs[s].at[pl.ds(c * rh, rh), :]

        sends = []
        for s in range(ns):
            for j, (px, py) in enumerate(chips):
                cp = pltpu.make_async_remote_copy(
                    src_ref=own_half(s), dst_ref=half(s, me, c), send_sem=send_sems.at[6 * s + j],
                    recv_sem=recv_sems.at[6 * s + j], device_id=(px, py, c), device_id_type=MESH)
                cp.start()
                sends.append(cp)
        for s in range(ns):
            for j, (px, py) in enumerate(chips):
                src = 2 * px + py
                landed = pltpu.make_async_remote_copy(
                    src_ref=half(s, src, c), dst_ref=half(s, src, c), send_sem=send_sems.at[6 * s + j],
                    recv_sem=recv_sems.at[6 * s + j], device_id=(px, py, c), device_id_type=MESH)
                landed.wait_recv()
                fw = pltpu.make_async_remote_copy(
                    src_ref=half(s, src, c), dst_ref=half(s, src, c), send_sem=send_sems.at[6 * s + 3 + j],
                    recv_sem=recv_sems.at[6 * s + 3 + j], device_id=sibling, device_id_type=MESH)
                fw.start()
                sends.append(fw)
        for s in range(ns):
            for j, (px, py) in enumerate(chips):
                src = 2 * px + py
                got = pltpu.make_async_remote_copy(
                    src_ref=half(s, src, 1 - c), dst_ref=half(s, src, 1 - c), send_sem=send_sems.at[6 * s + 3 + j],
                    recv_sem=recv_sems.at[6 * s + 3 + j], device_id=sibling, device_id_type=MESH)
                got.wait_recv()
        for cp in sends:
            cp.wait_send()

    return pl.pallas_call(
        body, name="gather_chips",
        in_specs=[_ANY] * ns, out_specs=[_ANY] * ns,
        out_shape=[jax.ShapeDtypeStruct((4,) + s.shape, s.dtype) for s in slabs],
        scratch_shapes=[pltpu.SemaphoreType.DMA((6 * ns,)), pltpu.SemaphoreType.DMA((6 * ns,))],
    )(*slabs)


def swap_halves(gss):
    ns = len(gss)

    def body(*refs):
        g_refs, out_refs = refs[:ns], refs[ns:2 * ns]
        send_sems, recv_sems = refs[2 * ns:]
        x, y, c = _coords()
        cps = []
        for s in range(ns):
            rh = gss[s].shape[1] // 2
            cp = pltpu.make_async_remote_copy(
                src_ref=g_refs[s].at[:, pl.ds((1 - c) * rh, rh), :], dst_ref=out_refs[s],
                send_sem=send_sems.at[s], recv_sem=recv_sems.at[s], device_id=(x, y, 1 - c), device_id_type=MESH)
            cp.start()
            cps.append(cp)
        for cp in cps:
            cp.wait()

    return pl.pallas_call(
        body, name="swap_halves",
        in_specs=[_ANY] * ns, out_specs=[_ANY] * ns,
        out_shape=[jax.ShapeDtypeStruct((4, g.shape[1] // 2, g.shape[2]), g.dtype) for g in gss],
        scratch_shapes=[pltpu.SemaphoreType.DMA((ns,)), pltpu.SemaphoreType.DMA((ns,))],
    )(*gss)


def join_halves(reds):
    ns = len(reds)

    def body(*refs):
        r_refs, out_refs = refs[:ns], refs[ns:2 * ns]
        send_sems, recv_sems = refs[2 * ns:]
        x, y, c = _coords()
        cps = []
        for s in range(ns):
            cp = pltpu.make_async_remote_copy(
                src_ref=r_refs[s], dst_ref=out_refs[s], send_sem=send_sems.at[s], recv_sem=recv_sems.at[s],
                device_id=(x, y, 1 - c), device_id_type=MESH)
            cp.start()
            cps.append(cp)
        for cp in cps:
            cp.wait()

    return pl.pallas_call(
        body, name="join_halves",
        in_specs=[_ANY] * ns, out_specs=[_ANY] * ns,
        out_shape=[jax.ShapeDtypeStruct(r.shape, r.dtype) for r in reds],
        scratch_shapes=[pltpu.SemaphoreType.DMA((ns,)), pltpu.SemaphoreType.DMA((ns,))],
    )(*reds)


def gather_all(vec, name):
    r, wd = vec.shape

    def body(v_ref, out_ref, send_sems, recv_sems):
        x, y, c = _coords()
        me = 4 * x + 2 * y + c
        cps = []
        for k in range(1, 8):
            mx, my, mc = (k >> 2) & 1, (k >> 1) & 1, k & 1
            peer = (x ^ mx, y ^ my, c ^ mc)
            cp = pltpu.make_async_remote_copy(
                src_ref=v_ref, dst_ref=out_ref.at[me],
                send_sem=send_sems.at[k - 1], recv_sem=recv_sems.at[k - 1], device_id=peer, device_id_type=MESH)
            cp.start()
            cps.append(cp)
        for k in range(1, 8):
            mx, my, mc = (k >> 2) & 1, (k >> 1) & 1, k & 1
            peer = (x ^ mx, y ^ my, c ^ mc)
            src = 4 * peer[0] + 2 * peer[1] + peer[2]
            got = pltpu.make_async_remote_copy(
                src_ref=v_ref, dst_ref=out_ref.at[src],
                send_sem=send_sems.at[k - 1], recv_sem=recv_sems.at[k - 1], device_id=peer, device_id_type=MESH)
            got.wait_recv()
        for cp in cps:
            cp.wait_send()

    return pl.pallas_call(
        body, name=name,
        in_specs=[_ANY], out_specs=_ANY,
        out_shape=jax.ShapeDtypeStruct((8, r, wd), vec.dtype),
        scratch_shapes=[pltpu.SemaphoreType.DMA((7,)), pltpu.SemaphoreType.DMA((7,))],
    )(vec)


def _pad_rows(a, rows):
    return jnp.pad(a, ((0, rows - a.shape[0]), (0, 0)))


def _pad_cols(a, cols):
    return jnp.pad(a, ((0, 0), (0, cols - a.shape[1])))


def local_step(x, c, ctx, c_ctx, target, wt, sm, late_weights=None, grad_hook=None):
    t, tc = x.shape[0], ctx.shape[0]
    tp = t + tc
    nct = tc // TR
    ncc = tc // LC
    nc = tp // LC

    w_in = wt["w_in"]
    segs = {"q": (0, D), "k": (D, 2 * D), "v": (2 * D, 3 * D), "g": (3 * D, 3 * D + NGATE)}
    base = 3 * D + NGATE
    for n_i, nm in enumerate(("o", "u", "vg", "gm", "gg")):
        segs[nm] = (base + n_i * D, base + (n_i + 1) * D)
    order = ("o", "u", "vg", "gm", "gg", "v", "q", "k")
    w_main = jnp.concatenate([w_in[:, segs[nm][0]:segs[nm][1]] for nm in order], axis=1)
    w_g = _pad_cols(w_in[:, segs["g"][0]:segs["g"][1]], LANE)
    w_main_t = w_main.T
    w_g_t = w_g.T

    cc = _pad_rows(jnp.concatenate([c.reshape(1, D), c_ctx.reshape(1, D)], axis=0), 16)
    modv = mod_fwd(cc, wt["w_mod"], sm["b_mod"].reshape(1, NMOD * D))
    mx = modv[0].reshape(NMOD, D)
    mc = modv[1].reshape(NMOD, D)
    tab1 = _pad_rows(jnp.stack([mc[0], mc[1], mx[0], mx[1]]), 8)
    tab2 = _pad_rows(jnp.stack([mx[2], mx[3], mx[4]]), 8)
    tab3 = _pad_rows(mx[5:6], 8)

    g1 = sm["norm1_g"].reshape(1, D)
    g2 = sm["norm2_g"].reshape(1, D)
    gfin = sm["final_g"].reshape(1, D)
    gh = sm["head_norm_g"].reshape(1, D)
    ln_g = sm["sgu_ln_g"].reshape(1, D)
    ln_b = sm["sgu_ln_b"].reshape(1, D)
    ws = sm["w_s"].reshape(SG * SCH, SCH).astype(BF16)
    bs_t = _pad_cols(sm["b_s"].reshape(SG, SCH).T, LANE)
    conv_w = _pad_rows(sm["conv_qk"].reshape(3, 2 * D), 8)
    b_gate = _pad_cols(sm["b_gate"].reshape(1, NGATE), LANE)
    wc = _pad_rows(sm["w_ffn_conv"].reshape(9, DFF), 16)

    hn1 = norm1_fwd(ctx, x, g1, tab1, nct)
    if late_weights is None:
        zmain = mm_nn(hn1, w_main, BF16, "mm_zmain")
    else:
        zmain, landed = mm_nn(hn1, w_main, BF16, "mm_zmain", rider=late_weights[0])
        wt = {**wt, **late_weights[1](landed)}
    zg = mm_nn(hn1, w_g, F32, "mm_zg")
    qa, ka, gcol, zgb = qkconv_fwd(zmain, zg, conv_w, b_gate, nct)
    grow = gcol[:, :16].reshape(nc, LC, 16).transpose(0, 2, 1)
    hf, hb, csf, csb, nsf, nsb, msf, msb = mlstm_fwd(qa, ka, zmain, gcol, grow, ncc)
    yms = mixer_fwd(hf, hb, zmain, gh, ln_g, ln_b, ws, bs_t, nct)
    w_br = jnp.stack([wt["w_branch_mlstm"], wt["w_branch_sgu"]])
    pp = mm_nn(yms, w_br, BF16, "mm_branch")
    y = merge_fwd(zmain, pp, nct)
    out = mm_nn(y, wt["w_out"], BF16, "mm_out")
    h1, hn2 = norm2_fwd(x, out, g2, tab2)
    up = mm_nn(hn2, wt["w_up"], BF16, "mm_up")
    act, ac = ffn_act_fwd(up, wc)
    f = mm_nn(act, wt["w_down"], BF16, "mm_down")
    dh2, df, acc_h = head_fwd_bwd(h1, f, target, gfin, tab3)
    loss = acc_h[0, 0]

    g_w_down = mm_tn(act, df, "mmt_down", BF16)
    dact = mm_nn(df, wt["w_down"].T, BF16, "mm_ddown")
    dup, dac = ffn_act_bwd(up, ac, dact)
    dup, g_wc = ffn_conv_bwd(dup, up, dac, wc)
    g_w_up = mm_tn(hn2, dup, "mmt_up", BF16)
    dhn2 = mm_nn(dup, wt["w_up"].T, BF16, "mm_dup")
    dh1, dout, acc_2 = norm2_bwd(h1, dhn2, dh2, out, g2, tab2)
    g_w_out = mm_tn(y, dout, "mmt_out", BF16)
    dy = mm_nn(dout, wt["w_out"].T, BF16, "mm_dout")
    dpp, dz = merge_bwd(zmain, pp, dy, tp, nct)
    g_w_br = mm_tn(yms, dpp, "mmt_branch", BF16)
    dyms = mm_nn(dpp, jnp.stack([wt["w_branch_mlstm"].T, wt["w_branch_sgu"].T]), BF16, "mm_dbranch")
    dz, dhm, g_gh, g_lng, g_lnb, g_ws, g_bs = mixer_bwd(dz, hf, hb, zmain, dyms, gh, ln_g, ln_b, ws, bs_t, nct)
    (dqf, dkf, dvf, colf, rowf, dqb, dkb, dvb, colb, rowb) = mlstm_bwd(
        qa, ka, zmain, gcol, grow, (csf, csb, nsf, nsb, msf, msb), dhm, ncc)

    csum_f = rowf[:, :4, :].transpose(0, 2, 1).reshape(tp, 4)
    csum_b = rowb[:, :4, :].transpose(0, 2, 1).reshape(tp, 4)
    cs = _pad_cols(jnp.concatenate([csum_f, csum_f, csum_b, csum_b], axis=1), LANE)
    dzg, g_bgate = gates_bwd(colf, colb, cs, zgb)

    dc, g_convw = qkconv_bwd_a(zmain, dqf, dqb, dkf, dkb, conv_w, nct)
    dz = qkconv_bwd_b(dz, dc, conv_w, nct)
    dz = add_into_dz(dz, dvf, dvb, CB_V)

    g_w_main = mm_tn(hn1, dz, "mmt_main", BF16)
    g_w_g = mm_tn(hn1, dzg, "mmt_g", BF16)
    blk = lambda cb: g_w_main[:, cb * D:(cb + 1) * D]
    g_w_in = jnp.concatenate([blk(CB_Q), blk(CB_K), blk(CB_V), g_w_g[:, :NGATE], blk(CB_O), blk(CB_U), blk(CB_VG),
                              blk(CB_GM), blk(CB_GG)], axis=1)
    big = {"w_in": g_w_in, "w_branch_mlstm": g_w_br[0], "w_branch_sgu": g_w_br[1], "w_out": g_w_out,
           "w_up": g_w_up, "w_down": g_w_down}
    if grad_hook is None:
        da, received = mm_nn(dz, w_main_t, BF16, "mm_dmain"), None
    else:
        da, received = mm_nn(dz, w_main_t, BF16, "mm_dmain", rider=grad_hook(big))
    db = mm_nn(dzg, w_g_t, F32, "mm_dg")
    grad_x, acc_1 = norm1_bwd(ctx, x, da, db, dh1, g1, tab1, nct)

    d_modx = jnp.concatenate([acc_1[2], acc_1[3], acc_2[3], acc_2[0], acc_2[1], acc_h[2]])
    d_modc = jnp.concatenate([acc_1[0], acc_1[1], jnp.zeros((4 * D,), F32)])
    d_modb = jnp.concatenate([acc_1[5], acc_1[6], acc_2[3], acc_2[0], acc_2[1], acc_h[2]])

    small = {"b_mod": d_modb, "norm1_g": acc_1[4], "b_gate": g_bgate[0, :NGATE], "conv_qk": g_convw[:3].reshape(-1),
             "head_norm_g": g_gh[0], "sgu_ln_g": g_lng[0], "sgu_ln_b": g_lnb[0], "w_s": g_ws.reshape(-1),
             "b_s": g_bs[:, :SG].T.reshape(-1), "norm2_g": acc_2[2], "w_ffn_conv": g_wc[:9].reshape(-1),
             "final_g": acc_h[1]}
    return loss, grad_x, big, small, d_modx, d_modc, received


def mod_bwd_w(a_all, dm_all, name):
    n = dm_all.shape[1]
    tn = _pick(n, (512, 128))

    def body(a_ref, d_ref, o_ref):
        o_ref[...] = _dot_tn(_silu(a_ref[...]), d_ref[...])

    return pl.pallas_call(
        body, name=name, grid=(n // tn,),
        in_specs=[_bc(16, D), pl.BlockSpec((16, tn), lambda j: (0, j))],
        out_specs=pl.BlockSpec((D, tn), lambda j: (0, j)),
        out_shape=jax.ShapeDtypeStruct((D, n), F32),
        compiler_params=_cparams(("parallel",)),
    )(a_all, dm_all)


def mod_bwd_cctx(dmc, w_mod_t, c_ctx):
    def body(d_ref, w_ref, c_ref, o_ref):
        o_ref[...] = _dot(d_ref[...], w_ref[...]) * _dsilu(c_ref[...])

    return pl.pallas_call(
        body, name="mod_bwd_cctx", grid=(1,),
        in_specs=[_bc(16, 2 * D), _bc(2 * D, D), _bc(1, D)],
        out_specs=_bc(16, D),
        out_shape=jax.ShapeDtypeStruct((16, D), F32),
        compiler_params=_cparams(("arbitrary",)),
    )(dmc, w_mod_t, c_ctx)


BIG = ("w_mod", "w_in", "w_branch_mlstm", "w_branch_sgu", "w_out", "w_up", "w_down")
BIG_AXIS = {"w_mod": 1, "w_in": 1, "w_branch_mlstm": 0, "w_branch_sgu": 0, "w_out": 0, "w_up": 1, "w_down": 0}
SMALL = ("c_ctx", "b_mod", "norm1_g", "b_gate", "conv_qk", "head_norm_g", "sgu_ln_g", "sgu_ln_b", "w_s", "b_s",
         "norm2_g", "w_ffn_conv", "final_g")
SMALL_SHARDED = {"conv_qk": (3, 2 * D), "w_ffn_conv": (9, DFF)}
PACK_ALIGN = 32 * D


def _pack(arrs, dtype, align=PACK_ALIGN, width=D):
    flat = jnp.concatenate([a.reshape(-1).astype(dtype) for a in arrs])
    n = flat.shape[0]
    padded = -(-n // align) * align
    return jnp.pad(flat, (0, padded - n)).reshape(padded // width, width)


def _unpack(slab, shapes):
    flat = slab.reshape(-1)
    outs, off = [], 0
    for shp in shapes:
        n = math.prod(shp)
        outs.append(flat[off:off + n].reshape(shp))
        off += n
    return outs


def _round_up(n, m):
    return -(-n // m) * m


def kernel(x, c, ctx, c_ctx, w_mod, b_mod, norm1_g, w_in, b_gate, conv_qk, head_norm_g, sgu_ln_g, sgu_ln_b, w_s, b_s, w_branch_mlstm, w_branch_sgu, w_out, norm2_g, w_up, w_ffn_conv, w_down, final_g, loss_target, m_c_ctx, m_w_mod, m_b_mod, m_norm1_g, m_w_in, m_b_gate, m_conv_qk, m_head_norm_g, m_sgu_ln_g, m_sgu_ln_b, m_w_s, m_b_s, m_w_branch_mlstm, m_w_branch_sgu, m_w_out, m_norm2_g, m_w_up, m_w_ffn_conv, m_w_down, m_final_g, v_c_ctx, v_w_mod, v_b_mod, v_norm1_g, v_w_in, v_b_gate, v_conv_qk, v_head_norm_g, v_sgu_ln_g, v_sgu_ln_b, v_w_s, v_b_s, v_w_branch_mlstm, v_w_branch_sgu, v_w_out, v_norm2_g, v_w_up, v_w_ffn_conv, v_w_down, v_final_g):
    params = dict(c_ctx=c_ctx, w_mod=w_mod, b_mod=b_mod, norm1_g=norm1_g, w_in=w_in, b_gate=b_gate, conv_qk=conv_qk,
                  head_norm_g=head_norm_g, sgu_ln_g=sgu_ln_g, sgu_ln_b=sgu_ln_b, w_s=w_s, b_s=b_s,
                  w_branch_mlstm=w_branch_mlstm, w_branch_sgu=w_branch_sgu, w_out=w_out, norm2_g=norm2_g, w_up=w_up,
                  w_ffn_conv=w_ffn_conv, w_down=w_down, final_g=final_g)
    mom_m = dict(c_ctx=m_c_ctx, w_mod=m_w_mod, b_mod=m_b_mod, norm1_g=m_norm1_g, w_in=m_w_in, b_gate=m_b_gate,
                 conv_qk=m_conv_qk, head_norm_g=m_head_norm_g, sgu_ln_g=m_sgu_ln_g, sgu_ln_b=m_sgu_ln_b, w_s=m_w_s,
                 b_s=m_b_s, w_branch_mlstm=m_w_branch_mlstm, w_branch_sgu=m_w_branch_sgu, w_out=m_w_out,
                 norm2_g=m_norm2_g, w_up=m_w_up, w_ffn_conv=m_w_ffn_conv, w_down=m_w_down, final_g=m_final_g)
    mom_v = dict(c_ctx=v_c_ctx, w_mod=v_w_mod, b_mod=v_b_mod, norm1_g=v_norm1_g, w_in=v_w_in, b_gate=v_b_gate,
                 conv_qk=v_conv_qk, head_norm_g=v_head_norm_g, sgu_ln_g=v_sgu_ln_g, sgu_ln_b=v_sgu_ln_b, w_s=v_w_s,
                 b_s=v_b_s, w_branch_mlstm=v_w_branch_mlstm, w_branch_sgu=v_w_branch_sgu, w_out=v_w_out,
                 norm2_g=v_norm2_g, w_up=v_w_up, w_ffn_conv=v_w_ffn_conv, w_down=v_w_down, final_g=v_final_g)
    chip = 2 * lax.axis_index("x") + lax.axis_index("y")

    shard2d = {n: params[n].reshape(params[n].shape[-2:]) for n in BIG}
    conv_sh = conv_qk.reshape(3, -1)
    fconv_sh = w_ffn_conv.reshape(9, -1)

    dev = 2 * chip + lax.axis_index("c")

    first_names, row_names = ("w_mod", "w_in"), ("w_branch_mlstm", "w_branch_sgu", "w_out", "w_down")
    first_w = [shard2d[n].shape[1] for n in first_names]
    row_h = [shard2d[n].shape[0] for n in row_names]
    first_slab = _pad_cols(jnp.concatenate([shard2d[n].astype(BF16) for n in first_names], axis=1),
                           _round_up(sum(first_w), LANE))
    up_slab = shard2d["w_up"].astype(BF16)
    row_slab = jnp.concatenate([shard2d[n].astype(BF16) for n in row_names], axis=0)

    def own_in(slab, gathered):
        return jnp.stack([jnp.where(chip == j, slab, gathered[j]) for j in range(4)])

    first_all = own_in(first_slab, gather_chips([first_slab])[0])
    wt = {}
    off = 0
    for n, wd in zip(first_names, first_w):
        wt[n] = jnp.concatenate([first_all[j, :, off:off + wd] for j in range(4)], axis=1)
        off += wd

    late_slabs = [up_slab, row_slab]

    def late_copies(in_refs, out_refs, send_sems, recv_sems):
        xx, yy, cc = _coords()
        me = 2 * xx + yy
        cps = []
        for s, slab in enumerate(late_slabs):
            rh = slab.shape[0] // 2
            for j, (px, py) in enumerate(_other_chips(xx, yy)):
                for o in range(2):
                    k = 6 * s + 2 * j + o
                    cps.append(pltpu.make_async_remote_copy(
                        src_ref=in_refs[s].at[pl.ds(cc * rh, rh), :], dst_ref=out_refs[s].at[me, pl.ds(cc * rh, rh), :],
                        send_sem=send_sems.at[k], recv_sem=recv_sems.at[k],
                        device_id=(px, py, cc if o == 0 else 1 - cc), device_id_type=MESH))
        return cps

    def late_finish(landed):
        up_all, row_all = own_in(up_slab, landed[0]), own_in(row_slab, landed[1])
        got = {"w_up": jnp.concatenate([up_all[j] for j in range(4)], axis=1)}
        o = 0
        for n, ht in zip(row_names, row_h):
            got[n] = jnp.concatenate([row_all[j, o:o + ht, :] for j in range(4)], axis=0)
            o += ht
        return got

    late_rider = Rider(late_slabs, [jax.ShapeDtypeStruct((4,) + s_.shape, s_.dtype) for s_ in late_slabs],
                       6 * len(late_slabs), late_copies)

    cvec = _pack([conv_sh, fconv_sh], F32, align=8 * LANE, width=LANE)
    call = gather_all(cvec, "gather_conv")
    cparts = [_unpack(jnp.where(dev == 2 * j, cvec, call[2 * j]), [conv_sh.shape, fconv_sh.shape]) for j in range(4)]
    conv_full = jnp.concatenate([p[0] for p in cparts], axis=1)
    fconv_full = jnp.concatenate([p[1] for p in cparts], axis=1)

    sm = dict(b_mod=b_mod, norm1_g=norm1_g, b_gate=b_gate, conv_qk=conv_full, head_norm_g=head_norm_g,
              sgu_ln_g=sgu_ln_g, sgu_ln_b=sgu_ln_b, w_s=w_s, b_s=b_s, norm2_g=norm2_g, w_ffn_conv=fconv_full,
              final_g=final_g)

    gcol_names = ("w_up", "w_in")
    gcol_w = [shard2d[n].shape[1] for n in gcol_names]
    gcol_pad = _round_up(sum(gcol_w), LANE)
    cidx = lax.axis_index("c")
    kept = {}

    def grad_hook(gbig):
        def chip_cols(j):
            return _pad_cols(jnp.concatenate([gbig[n][:, j * wd:(j + 1) * wd] for n, wd in zip(gcol_names, gcol_w)],
                                             axis=1), gcol_pad)

        def chip_rows(j):
            return jnp.concatenate([gbig[n][j * ht:(j + 1) * ht] for n, ht in zip(row_names, row_h)], axis=0)

        gss = [jnp.stack([chip_cols(j) for j in range(4)]), jnp.stack([chip_rows(j) for j in range(4)])]
        from_sib = swap_halves(gss)
        pair_bf, own_terms = [], []
        for s, (gs, fs) in enumerate(zip(gss, from_sib)):
            rh = gs.shape[1] // 2
            my_half = lax.dynamic_slice_in_dim(gs, cidx * rh, rh, axis=1)
            pair_bf.append(add_n([my_half, fs], BF16, "pair_sum_%d" % s))
            own_terms.append([lax.dynamic_index_in_dim(my_half, chip, axis=0, keepdims=False),
                              lax.dynamic_index_in_dim(fs, chip, axis=0, keepdims=False)])
        kept["own_terms"] = own_terms

        def scatter_copies(in_refs, out_refs, send_sems, recv_sems):
            xx, yy, cc = _coords()
            cps = []
            for s in range(len(pair_bf)):
                for j, (px, py) in enumerate(_other_chips(xx, yy)):
                    cps.append(pltpu.make_async_remote_copy(
                        src_ref=in_refs[s].at[2 * px + py], dst_ref=out_refs[s].at[j],
                        send_sem=send_sems.at[3 * s + j], recv_sem=recv_sems.at[3 * s + j],
                        device_id=(px, py, cc), device_id_type=MESH))
            return cps

        return Rider(pair_bf, [jax.ShapeDtypeStruct((3,) + p.shape[1:], p.dtype) for p in pair_bf],
                     3 * len(pair_bf), scatter_copies)

    loss_l, grad_x, _, gsmall, d_modx, d_modc, recv = local_step(
        x[0], c, ctx[0], c_ctx, loss_target[0], wt, sm, late_weights=(late_rider, late_finish), grad_hook=grad_hook)

    reds = [add_n(kept["own_terms"][s] + [recv[s][0], recv[s][1], recv[s][2]], F32, "chip_sum_%d" % s)
            for s in range(2)]
    others = join_halves(reds)
    full_red = [jnp.where(cidx == 0, jnp.concatenate([m, o], axis=0), jnp.concatenate([o, m], axis=0))
                for m, o in zip(reds, others)]
    g_shard = {}
    off = 0
    for n, wd in zip(gcol_names, gcol_w):
        g_shard[n] = full_red[0][:, off:off + wd]
        off += wd
    off = 0
    for n, ht in zip(row_names, row_h):
        g_shard[n] = full_red[1][off:off + ht]
        off += ht

    small_order = ("b_mod", "norm1_g", "b_gate", "conv_qk", "head_norm_g", "sgu_ln_g", "sgu_ln_b", "w_s", "b_s", "norm2_g",
                   "w_ffn_conv", "final_g")
    vec_parts = [gsmall[n] for n in small_order] + [d_modx, d_modc, c.reshape(-1), loss_l.reshape(1)]
    vec_shapes = [a.shape for a in vec_parts]
    vec = _pack(vec_parts, F32, align=8 * LANE, width=LANE)
    allv = gather_all(vec, "gather_small")
    allv = jnp.stack([jnp.where(dev == k, vec, allv[k]) for k in range(8)])
    summed = sum8(allv, "small_sum")
    s_parts = _unpack(summed, vec_shapes)
    g_small = dict(zip(small_order, s_parts[:len(small_order)]))
    dmc_sum = s_parts[len(small_order) + 1]
    loss = s_parts[-1][0]
    per_dev = [_unpack(allv[k], vec_shapes) for k in range(8)]
    dmx_all = jnp.stack([p[len(small_order)] for p in per_dev])
    c_all = jnp.stack([p[len(small_order) + 2] for p in per_dev])

    a_all = _pad_rows(jnp.concatenate([c_all, c_ctx.reshape(1, D)], axis=0), 16)
    dm_all = _pad_rows(jnp.concatenate([dmx_all, dmc_sum.reshape(1, NMOD * D)], axis=0), 16)
    ncol = NMOD * D // 4
    dm_shard = lax.dynamic_slice_in_dim(dm_all, chip * ncol, ncol, axis=1)
    g_shard["w_mod"] = mod_bwd_w(a_all, dm_shard, "mod_bwd_w")
    w_mod_t = wt["w_mod"][:, :2 * D].T
    g_cctx = mod_bwd_cctx(_pad_rows(dmc_sum[:2 * D].reshape(1, 2 * D), 16), w_mod_t, c_ctx.reshape(1, D))[0]
    g_small["c_ctx"] = g_cctx

    results = {}
    for n in BIG:
        shp = params[n].shape
        g_ = g_shard[n].reshape(shp)
        d_, m_, v_ = adamw(params[n], g_, mom_m[n], mom_v[n], "adamw_" + n)
        results[n] = (g_, d_, m_, v_)

    conv_g = lax.dynamic_slice_in_dim(g_small["conv_qk"].reshape(3, 2 * D), chip * (2 * D // 4), 2 * D // 4, axis=1)
    fconv_g = lax.dynamic_slice_in_dim(g_small["w_ffn_conv"].reshape(9, DFF), chip * (DFF // 4), DFF // 4, axis=1)
    g_small["conv_qk"] = conv_g
    g_small["w_ffn_conv"] = fconv_g
    w_list = [params[n].reshape(-1) for n in SMALL]
    g_list = [g_small[n].reshape(-1) for n in SMALL]
    m_list = [mom_m[n].reshape(-1) for n in SMALL]
    v_list = [mom_v[n].reshape(-1) for n in SMALL]
    sm_shapes = [params[n].shape for n in SMALL]
    pk = lambda lst: _pack(lst, F32, align=8 * LANE, width=LANE)
    gp = pk(g_list)
    d_s, m_s, v_s = adamw(pk(w_list), gp, pk(m_list), pk(v_list), "adamw_small")
    for n, gg, dd, mm, vv in zip(SMALL, _unpack(gp, sm_shapes), _unpack(d_s, sm_shapes), _unpack(m_s, sm_shapes),
                                 _unpack(v_s, sm_shapes)):
        results[n] = (gg, dd, mm, vv)

    order = ("c_ctx", "w_mod", "b_mod", "norm1_g", "w_in", "b_gate", "conv_qk", "head_norm_g", "sgu_ln_g", "sgu_ln_b",
             "w_s", "b_s", "w_branch_mlstm", "w_branch_sgu", "w_out", "norm2_g", "w_up", "w_ffn_conv", "w_down",
             "final_g")
    outs = [loss, grad_x[None]]
    for k in range(4):
        outs += [results[n][k] for n in order]
    return tuple(outs)
```

```python
import functools
import math

import jax
import jax.numpy as jnp
from jax import lax
from jax.experimental import pallas as pl
from jax.experimental.pallas import tpu as pltpu

F32 = jnp.float32
BF16 = jnp.bfloat16

D = 1024
NH = 4
DH = 256
LC = 256
GW = 64
SG = 4
SGD = 256
SCH = 128
DFF = 2816
NMOD = 6
NGATE = 16
NIN = 8208
EPS = 1e-6
M_INIT = -1e30
TR = 256
TR2 = 512
LANE = 128
VMEM_LIMIT = 56 * 1024 * 1024
MESH = pl.DeviceIdType.MESH

ADAM_LR = 0.001
ADAM_B1 = 0.9
ADAM_B2 = 0.999
ADAM_EPS = 1e-08
ADAM_WD = 0.01
ADAM_STEP = 10

CB_O, CB_U, CB_VG, CB_GM, CB_GG, CB_V, CB_Q, CB_K = range(8)


def _pick(n, cands):
    for c in cands:
        if n % c == 0:
            return c
    return n


def _cparams(sem):
    return pltpu.CompilerParams(dimension_semantics=sem, vmem_limit_bytes=VMEM_LIMIT)


def _sigmoid(x):
    return 1.0 / (1.0 + jnp.exp(-x))


def _silu(x):
    return x * _sigmoid(x)


def _dsilu(x):
    s = _sigmoid(x)
    return s * (1.0 + x * (1.0 - s))


_GC = math.sqrt(2.0 / math.pi)


def _gelu(x):
    return 0.5 * x * (1.0 + jnp.tanh(_GC * (x + 0.044715 * x * x * x)))


def _gelu_and_grad(x):
    x2 = x * x
    t = jnp.tanh(_GC * x * (1.0 + 0.044715 * x2))
    half = 0.5 * (1.0 + t)
    return x * half, half + 0.5 * x * (1.0 - t * t) * _GC * (1.0 + 3.0 * 0.044715 * x2)


def _dot(a, b):
    return jnp.dot(a.astype(BF16), b.astype(BF16), preferred_element_type=F32)


def _dot_nt(a, b):
    return lax.dot_general(a.astype(BF16), b.astype(BF16), (((1,), (1,)), ((), ())), preferred_element_type=F32)


def _dot_tn(a, b):
    return lax.dot_general(a.astype(BF16), b.astype(BF16), (((0,), (0,)), ((), ())), preferred_element_type=F32)


def _dot_tn_mxu(a, b):
    m = a.shape[1]
    eye = (lax.broadcasted_iota(jnp.int32, (m, m), 0) == lax.broadcasted_iota(jnp.int32, (m, m), 1)).astype(BF16)
    return _dot(_dot_nt(eye, a), b)


def _exact_dot(tri, x):
    x1 = x.astype(BF16)
    r1 = x - x1.astype(F32)
    x2 = r1.astype(BF16)
    x3 = (r1 - x2.astype(F32)).astype(BF16)
    return (jnp.dot(tri, x1, preferred_element_type=F32) + jnp.dot(tri, x2, preferred_element_type=F32)
            + jnp.dot(tri, x3, preferred_element_type=F32))


def _rb(tm, w, col=0, off=0):
    return pl.BlockSpec((tm, w), lambda i: (i + off, col))


def _bc(r, w):
    return pl.BlockSpec((r, w), lambda i: (0, 0))


class Rider:
    def __init__(self, ins, out_shapes, n_sems, copies):
        self.ins, self.out_shapes, self.n_sems, self.copies = list(ins), list(out_shapes), n_sems, copies


def mm_nn(a, b, out_dtype, name, rider=None):
    squeeze = a.ndim == 2
    if squeeze:
        a, b = a[None], b[None]
    g, m, k = a.shape
    n = b.shape[2]
    tm = _pick(m, (2048, 1280, 1024, 512, 256, 128))
    tn = _pick(n, (2048, 1408, 1024, 512, 128))
    tk = _pick(k, (2048, 1408, 1024, 512, 128))
    nk = k // tk
    grid = (g, n // tn, m // tm, nk)
    n_rin = len(rider.ins) if rider else 0
    n_rout = len(rider.out_shapes) if rider else 0

    def body(*refs):
        a_ref, b_ref = refs[0], refs[1]
        r_in = refs[2:2 + n_rin]
        o_ref = refs[2 + n_rin]
        r_out = refs[3 + n_rin:3 + n_rin + n_rout]
        scr = refs[3 + n_rin + n_rout:]
        if rider:
            ids = [pl.program_id(d) for d in range(4)]
            first = functools.reduce(jnp.logical_and, [i == 0 for i in ids])
            last = functools.reduce(jnp.logical_and, [i == e - 1 for i, e in zip(ids, grid)])
            send_sems, recv_sems = scr[-2], scr[-1]

            @pl.when(first)
            def _():
                for cp in rider.copies(r_in, r_out, send_sems, recv_sems):
                    cp.start()

        if nk == 1:
            o_ref[0] = _dot(a_ref[0], b_ref[0]).astype(o_ref.dtype)
        else:
            acc_ref = scr[0]
            kk = pl.program_id(3)

            @pl.when(kk == 0)
            def _():
                acc_ref[...] = jnp.zeros_like(acc_ref)

            acc_ref[...] += _dot(a_ref[0], b_ref[0])

            @pl.when(kk == nk - 1)
            def _():
                o_ref[0] = acc_ref[...].astype(o_ref.dtype)

        if rider:
            @pl.when(last)
            def _():
                for cp in rider.copies(r_in, r_out, send_sems, recv_sems):
                    cp.wait()

    scratch = [] if nk == 1 else [pltpu.VMEM((tm, tn), F32)]
    if rider:
        scratch += [pltpu.SemaphoreType.DMA((rider.n_sems,)), pltpu.SemaphoreType.DMA((rider.n_sems,))]
    outs = pl.pallas_call(
        body, name=name, grid=grid,
        in_specs=[pl.BlockSpec((1, tm, tk), lambda gi, j, i, kk: (gi, i, kk)),
                  pl.BlockSpec((1, tk, tn), lambda gi, j, i, kk: (gi, kk, j))] + [_ANY] * n_rin,
        out_specs=[pl.BlockSpec((1, tm, tn), lambda gi, j, i, kk: (gi, i, j))] + [_ANY] * n_rout,
        out_shape=[jax.ShapeDtypeStruct((g, m, n), out_dtype)] + (rider.out_shapes if rider else []),
        scratch_shapes=scratch,
        compiler_params=_cparams(("arbitrary",) * 4 if rider else ("parallel", "parallel", "parallel", "arbitrary")),
    )(a, b, *(rider.ins if rider else []))
    out = outs[0][0] if squeeze else outs[0]
    return (out, list(outs[1:])) if rider else out


def mm_tn(a, b, name, out_dtype=F32):
    squeeze = a.ndim == 2
    if squeeze:
        a, b = a[None], b[None]
    g, t, ka = a.shape
    n = b.shape[2]
    tka = _pick(ka, (1024, 1408, 512, 128))
    tn = _pick(n, (2048, 1408, 1024, 512, 128))
    tt = _pick(t, (2048, 1280, 1024, 512, 256, 128))
    nt = t // tt

    def body(a_ref, b_ref, o_ref, acc_ref):
        tt_i = pl.program_id(3)

        @pl.when(tt_i == 0)
        def _():
            acc_ref[...] = jnp.zeros_like(acc_ref)

        acc_ref[...] += _dot_tn(a_ref[0], b_ref[0])

        @pl.when(tt_i == nt - 1)
        def _():
            o_ref[0] = acc_ref[...].astype(o_ref.dtype)

    out = pl.pallas_call(
        body, name=name, grid=(g, ka // tka, n // tn, nt),
        in_specs=[pl.BlockSpec((1, tt, tka), lambda gi, i, j, ti: (gi, ti, i)),
                  pl.BlockSpec((1, tt, tn), lambda gi, i, j, ti: (gi, ti, j))],
        out_specs=pl.BlockSpec((1, tka, tn), lambda gi, i, j, ti: (gi, i, j)),
        out_shape=jax.ShapeDtypeStruct((g, ka, n), out_dtype),
        scratch_shapes=[pltpu.VMEM((tka, tn), F32)],
        compiler_params=_cparams(("parallel", "parallel", "parallel", "arbitrary")),
    )(a, b)
    return out[0] if squeeze else out


def mod_fwd(cc, w_mod, b_mod):
    n = w_mod.shape[1]

    def body(c_ref, w_ref, b_ref, o_ref):
        o_ref[...] = _dot(_silu(c_ref[...]), w_ref[...]) + b_ref[...]

    return pl.pallas_call(
        body, name="mod_fwd", grid=(n // D,),
        in_specs=[_bc(16, D), pl.BlockSpec((D, D), lambda j: (0, j)), pl.BlockSpec((1, D), lambda j: (0, j))],
        out_specs=pl.BlockSpec((16, D), lambda j: (0, j)),
        out_shape=jax.ShapeDtypeStruct((16, n), F32),
        compiler_params=_cparams(("parallel",)),
    )(cc, w_mod, b_mod)


def _ctx_x_specs(n_ctx_tiles):
    return [pl.BlockSpec((TR, D), lambda i: (jnp.minimum(i, n_ctx_tiles - 1), 0)),
            pl.BlockSpec((TR, D), lambda i: (jnp.maximum(i - n_ctx_tiles, 0), 0))]


def norm1_fwd(ctx, x, g, tab, n_ctx_tiles):
    tp = ctx.shape[0] + x.shape[0]

    def body(c_ref, x_ref, g_ref, tab_ref, o_ref):
        is_ctx = pl.program_id(0) < n_ctx_tiles
        x = jnp.where(is_ctx, c_ref[...], x_ref[...])
        r = lax.rsqrt(jnp.mean(x * x, axis=-1, keepdims=True) + EPS)
        nrm = x * r * g_ref[...]
        sh = jnp.where(is_ctx, tab_ref[0:1, :], tab_ref[2:3, :])
        sc = jnp.where(is_ctx, tab_ref[1:2, :], tab_ref[3:4, :])
        o_ref[...] = (nrm * (1.0 + sc) + sh).astype(BF16)

    return pl.pallas_call(
        body, name="norm1_fwd", grid=(tp // TR,),
        in_specs=_ctx_x_specs(n_ctx_tiles) + [_bc(1, D), _bc(8, D)],
        out_specs=_rb(TR, D),
        out_shape=jax.ShapeDtypeStruct((tp, D), BF16),
        compiler_params=_cparams(("parallel",)),
    )(ctx, x, g, tab)


def norm2_fwd(x, out, g, tab):
    t = x.shape[0]

    def body(x_ref, o_in_ref, g_ref, tab_ref, h1_ref, hn_ref):
        h1 = x_ref[...] + tab_ref[0:1, :] * o_in_ref[...].astype(F32)
        h1_ref[...] = h1
        r = lax.rsqrt(jnp.mean(h1 * h1, axis=-1, keepdims=True) + EPS)
        nrm = h1 * r * g_ref[...]
        hn_ref[...] = (nrm * (1.0 + tab_ref[2:3, :]) + tab_ref[1:2, :]).astype(BF16)

    return pl.pallas_call(
        body, name="norm2_fwd", grid=(t // TR2,),
        in_specs=[_rb(TR2, D), _rb(TR2, D), _bc(1, D), _bc(8, D)],
        out_specs=[_rb(TR2, D), _rb(TR2, D)],
        out_shape=[jax.ShapeDtypeStruct((t, D), F32), jax.ShapeDtypeStruct((t, D), BF16)],
        compiler_params=_cparams(("parallel",)),
    )(x, out, g, tab)


def _halo_specs(tm, w, col, n_rows, hb):
    per = tm // hb
    last = n_rows // hb - 1
    prev = pl.BlockSpec((hb, w), lambda i: (jnp.maximum(i * per - 1, 0), col))
    nxt = pl.BlockSpec((hb, w), lambda i: (jnp.minimum((i + 1) * per, last), col))
    return prev, nxt


def _shift_rows(x, prev_row, next_row):
    tm = x.shape[0]
    rid = lax.broadcasted_iota(jnp.int32, x.shape, 0)
    xm1 = jnp.where(rid == 0, prev_row, pltpu.roll(x, 1, 0))
    xp1 = jnp.where(rid == tm - 1, next_row, pltpu.roll(x, tm - 1, 0))
    return xm1, xp1


def _seq_edges(i, n_ctx_tiles, n_tiles):
    first = jnp.logical_or(i == 0, i == n_ctx_tiles)
    last = jnp.logical_or(i == n_ctx_tiles - 1, i == n_tiles - 1)
    return first, last


def qkconv_fwd(zmain, zg, conv_w, b_gate, n_ctx_tiles):
    tp = zmain.shape[0]
    nt = tp // TR
    w2 = 2 * D
    prev_s, next_s = _halo_specs(TR, w2, CB_Q // 2, tp, 16)

    def body(z_ref, zp_ref, zn_ref, w_ref, zg_ref, bg_ref, q_ref, k_ref, g_ref, zgb_ref):
        i = pl.program_id(0)
        first, last = _seq_edges(i, n_ctx_tiles, nt)
        z = z_ref[...].astype(F32)
        pr = jnp.where(first, 0.0, zp_ref[15:16, :].astype(F32))
        nx = jnp.where(last, 0.0, zn_ref[0:1, :].astype(F32))
        zm1, zp1 = _shift_rows(z, pr, nx)
        cv = w_ref[0:1, :] * zm1 + w_ref[1:2, :] * z + w_ref[2:3, :] * zp1
        a = _silu(cv)
        q_ref[...] = (a[:, :D] * (DH ** -0.5)).astype(BF16)
        k_ref[...] = a[:, D:].astype(BF16)
        zgb = zg_ref[...] + bg_ref[...]
        zgb_ref[...] = zgb
        logf = jnp.minimum(zgb, 0.0) - jnp.log(1.0 + jnp.exp(-jnp.abs(zgb)))
        rr = lax.broadcasted_iota(jnp.int32, (TR, TR), 0)
        cc = lax.broadcasted_iota(jnp.int32, (TR, TR), 1)
        same = (rr // LC) == (cc // LC)
        low = jnp.where(jnp.logical_and(same, cc <= rr), 1.0, 0.0).astype(BF16)
        upp = jnp.where(jnp.logical_and(same, cc >= rr), 1.0, 0.0).astype(BF16)
        bf = _exact_dot(low, logf)
        bb = _exact_dot(upp, logf)
        lane = lax.broadcasted_iota(jnp.int32, (TR, LANE), 1)
        g = jnp.where(jnp.logical_and(lane >= 4, lane < 8), bf,
                      jnp.where(jnp.logical_and(lane >= 12, lane < 16), bb, zgb))
        g_ref[...] = g

    return pl.pallas_call(
        body, name="qkconv_fwd", grid=(nt,),
        in_specs=[_rb(TR, w2, CB_Q // 2), prev_s, next_s, _bc(8, w2), _rb(TR, LANE), _bc(1, LANE)],
        out_specs=[_rb(TR, D), _rb(TR, D), _rb(TR, LANE), _rb(TR, LANE)],
        out_shape=[jax.ShapeDtypeStruct((tp, D), BF16), jax.ShapeDtypeStruct((tp, D), BF16),
                   jax.ShapeDtypeStruct((tp, LANE), F32), jax.ShapeDtypeStruct((tp, LANE), F32)],
        compiler_params=_cparams(("parallel",)),
    )(zmain, zmain, zmain, conv_w, zg, b_gate)


def qkconv_bwd_a(zmain, dqf, dqb, dkf, dkb, conv_w, n_ctx_tiles):
    tp = zmain.shape[0]
    nt = tp // TR
    w2 = 2 * D
    prev_s, next_s = _halo_specs(TR, w2, CB_Q // 2, tp, 16)

    def body(z_ref, zp_ref, zn_ref, w_ref, dqf_ref, dqb_ref, dkf_ref, dkb_ref, dc_ref, dw_ref):
        i = pl.program_id(0)
        first, last = _seq_edges(i, n_ctx_tiles, nt)
        z = z_ref[...].astype(F32)
        pr = jnp.where(first, 0.0, zp_ref[15:16, :].astype(F32))
        nx = jnp.where(last, 0.0, zn_ref[0:1, :].astype(F32))
        zm1, zp1 = _shift_rows(z, pr, nx)
        cv = w_ref[0:1, :] * zm1 + w_ref[1:2, :] * z + w_ref[2:3, :] * zp1
        da = jnp.concatenate(
            [(dqf_ref[...].astype(F32) + dqb_ref[...].astype(F32)) * (DH ** -0.5),
             dkf_ref[...].astype(F32) + dkb_ref[...].astype(F32)], axis=1)
        dc = da * _dsilu(cv)
        dc_ref[...] = dc.astype(BF16)

        @pl.when(i == 0)
        def _():
            dw_ref[...] = jnp.zeros_like(dw_ref)

        dw_ref[0:1, :] += jnp.sum(zm1 * dc, axis=0, keepdims=True)
        dw_ref[1:2, :] += jnp.sum(z * dc, axis=0, keepdims=True)
        dw_ref[2:3, :] += jnp.sum(zp1 * dc, axis=0, keepdims=True)

    return pl.pallas_call(
        body, name="qkconv_bwd_a", grid=(nt,),
        in_specs=[_rb(TR, w2, CB_Q // 2), prev_s, next_s, _bc(8, w2), _rb(TR, D), _rb(TR, D), _rb(TR, D), _rb(TR, D)],
        out_specs=[_rb(TR, w2), _bc(8, w2)],
        out_shape=[jax.ShapeDtypeStruct((tp, w2), BF16), jax.ShapeDtypeStruct((8, w2), F32)],
        compiler_params=_cparams(("arbitrary",)),
    )(zmain, zmain, zmain, conv_w, dqf, dqb, dkf, dkb)


def qkconv_bwd_b(dz, dc, conv_w, n_ctx_tiles):
    tp = dc.shape[0]
    nt = tp // TR
    w2 = 2 * D
    prev_s, next_s = _halo_specs(TR, w2, 0, tp, 16)

    def body(dz_in_ref, d_ref, dp_ref, dn_ref, w_ref, o_ref):
        del dz_in_ref
        i = pl.program_id(0)
        first, last = _seq_edges(i, n_ctx_tiles, nt)
        d = d_ref[...].astype(F32)
        pr = jnp.where(first, 0.0, dp_ref[15:16, :].astype(F32))
        nx = jnp.where(last, 0.0, dn_ref[0:1, :].astype(F32))
        dm1, dp1 = _shift_rows(d, pr, nx)
        o_ref[...] = (w_ref[0:1, :] * dp1 + w_ref[1:2, :] * d + w_ref[2:3, :] * dm1).astype(BF16)

    return pl.pallas_call(
        body, name="qkconv_bwd_b", grid=(nt,),
        in_specs=[pl.BlockSpec(memory_space=pl.ANY), _rb(TR, w2), prev_s, next_s, _bc(8, w2)],
        out_specs=_rb(TR, w2, CB_Q // 2),
        out_shape=jax.ShapeDtypeStruct(dz.shape, BF16),
        input_output_aliases={0: 0},
        compiler_params=_cparams(("parallel",)),
    )(dz, dc, dc, dc, conv_w)


def add_into_dz(dz, a, b, col):
    tp = a.shape[0]

    def body(dz_in_ref, a_ref, b_ref, o_ref):
        del dz_in_ref
        o_ref[...] = (a_ref[...].astype(F32) + b_ref[...].astype(F32)).astype(BF16)

    return pl.pallas_call(
        body, name="add_into_dz", grid=(tp // TR,),
        in_specs=[pl.BlockSpec(memory_space=pl.ANY), _rb(TR, D), _rb(TR, D)],
        out_specs=_rb(TR, D, col),
        out_shape=jax.ShapeDtypeStruct(dz.shape, BF16),
        input_output_aliases={0: 0},
        compiler_params=_cparams(("parallel",)),
    )(dz, a, b)


def _chunk_maps(nc, ncc):
    def fwd(t):
        return t

    def bwd(t):
        return jnp.where(t < ncc, ncc - 1 - t, nc - 1 + ncc - t)

    return fwd, bwd


def _split2(x):
    hi = x.astype(BF16)
    return hi, (x - hi.astype(F32)).astype(BF16)


def _mlstm_chunk(d, h, gc, gr, q_ref, k_ref, v_ref, cp, npv, m_prev, mask, precise=False):
    ic, bcol = 8 * d + h, 8 * d + 4 + h
    i_col, b_col = gc[:, ic:ic + 1], gc[:, bcol:bcol + 1]
    i_row, b_row = gr[ic:ic + 1, :], gr[bcol:bcol + 1, :]
    g = b_row[:, LC - 1:LC] if d == 0 else b_row[:, 0:1]
    a_row = g - b_row + i_row
    m_loc = jnp.max(a_row, axis=1, keepdims=True)
    dmat = jnp.where(mask, b_col - b_row + i_row, -jnp.inf)
    inter = b_col + m_prev
    m_row = jnp.maximum(inter, jnp.max(dmat, axis=1, keepdims=True))
    e = jnp.exp(dmat - m_row)
    w = jnp.exp(inter - m_row)
    hs = slice(h * DH, (h + 1) * DH)
    qh, kh, vh = q_ref[:, hs], k_ref[:, hs], v_ref[:, hs]
    p = _dot_nt(qh, kh)
    s = p * e
    cpb = cp.astype(BF16)
    qc = _dot(qh, cpb)
    if precise:
        s_hi, s_lo = _split2(s)
        num = _dot(s_hi, vh) + _dot(s_lo, vh) + w * qc
    else:
        num = _dot(s, vh) + w * qc
    qn = jnp.sum(qh.astype(F32) * npv, axis=1, keepdims=True)
    den = jnp.sum(s, axis=1, keepdims=True) + w * qn
    thr = jnp.exp(-m_row)
    m_new = jnp.maximum(g + m_prev, m_loc)
    a_old = jnp.exp(g + m_prev - m_new)
    a_col = g - b_col + i_col
    return dict(qh=qh, kh=kh, vh=vh, e=e, w=w, s=s, cpb=cpb, qc=qc, num=num, qn=qn, den=den, thr=thr,
                m_loc=m_loc, m_new=m_new, a_old=a_old, a_col=a_col, hs=hs)


def mlstm_fwd(qa, ka, zmain, gcol, grow, ncc):
    tp = qa.shape[0]
    nc = tp // LC
    cf, cb = _chunk_maps(nc, ncc)

    def body(qf, kf, vf, gcf, grf, qb, kb, vb, gcb, grb,
             hf_o, hb_o, cf_o, cb_o, nf_o, nb_o, mf_o, mb_o, c_sc, n_sc, m_sc):
        t = pl.program_id(0)

        @pl.when(t == 0)
        def _():
            c_sc[...] = jnp.zeros_like(c_sc)
            n_sc[...] = jnp.zeros_like(n_sc)
            m_sc[...] = jnp.full(m_sc.shape, M_INIT, F32)

        row = lax.broadcasted_iota(jnp.int32, (LC, LC), 0)
        col = lax.broadcasted_iota(jnp.int32, (LC, LC), 1)
        dirs = ((qf, kf, vf, gcf, grf, hf_o, cf_o, nf_o, mf_o), (qb, kb, vb, gcb, grb, hb_o, cb_o, nb_o, mb_o))
        for d, (q_ref, k_ref, v_ref, gc_ref, gr_ref, h_o, c_o, n_o, m_o) in enumerate(dirs):
            mask = (col <= row) if d == 0 else (col >= row)
            gc = gc_ref[...]
            gr = gr_ref[0]
            for h in range(NH):
                idx = d * NH + h
                cp = c_sc[idx]
                npv = n_sc[idx]
                m_full = m_sc[idx]
                m_prev = m_full[:, 0:1]
                r = _mlstm_chunk(d, h, gc, gr, q_ref, k_ref, v_ref, cp, npv, m_prev, mask)
                hs = r["hs"]
                h_o[:, hs] = (r["num"] / jnp.maximum(jnp.abs(r["den"]), r["thr"])).astype(BF16)
                c_o[0, hs, :] = r["cpb"]
                n_o[0, h:h + 1, :] = npv
                m_o[0, h:h + 1, :] = m_full
                a_new = jnp.exp(r["m_loc"] - r["m_new"])
                kw = r["kh"].astype(F32) * jnp.exp(r["a_col"] - r["m_loc"])
                kw_hi, kw_lo = _split2(kw)
                kv = _dot_tn_mxu(kw_hi, r["vh"]) + _dot_tn_mxu(kw_lo, r["vh"])
                kn = jnp.sum(kw, axis=0, keepdims=True)
                c_sc[idx] = r["a_old"] * cp + a_new * kv
                n_sc[idx] = r["a_old"] * npv + a_new * kn
                m_sc[idx] = jnp.broadcast_to(r["m_new"], (1, LANE))

    def dspecs(cm):
        return [pl.BlockSpec((LC, D), lambda t: (cm(t), 0)),
                pl.BlockSpec((LC, D), lambda t: (cm(t), 0)),
                pl.BlockSpec((LC, D), lambda t: (cm(t), CB_V)),
                pl.BlockSpec((LC, LANE), lambda t: (cm(t), 0)),
                pl.BlockSpec((1, 16, LC), lambda t: (cm(t), 0, 0))]

    def ospec(cm, shp):
        return pl.BlockSpec((1,) + shp, lambda t: (cm(t), 0, 0))

    return pl.pallas_call(
        body, name="mlstm_fwd", grid=(nc,),
        in_specs=dspecs(cf) + dspecs(cb),
        out_specs=[pl.BlockSpec((LC, D), lambda t: (cf(t), 0)), pl.BlockSpec((LC, D), lambda t: (cb(t), 0)),
                   ospec(cf, (D, DH)), ospec(cb, (D, DH)), ospec(cf, (NH, DH)), ospec(cb, (NH, DH)),
                   ospec(cf, (NH, LANE)), ospec(cb, (NH, LANE))],
        out_shape=[jax.ShapeDtypeStruct((tp, D), BF16), jax.ShapeDtypeStruct((tp, D), BF16),
                   jax.ShapeDtypeStruct((nc, D, DH), BF16), jax.ShapeDtypeStruct((nc, D, DH), BF16),
                   jax.ShapeDtypeStruct((nc, NH, DH), F32), jax.ShapeDtypeStruct((nc, NH, DH), F32),
                   jax.ShapeDtypeStruct((nc, NH, LANE), F32), jax.ShapeDtypeStruct((nc, NH, LANE), F32)],
        scratch_shapes=[pltpu.VMEM((2 * NH, DH, DH), F32), pltpu.VMEM((2 * NH, 1, DH), F32),
                        pltpu.VMEM((2 * NH, 1, LANE), F32)],
        compiler_params=_cparams(("arbitrary",)),
    )(qa, ka, zmain, gcol, grow, qa, ka, zmain, gcol, grow)


def mlstm_bwd(qa, ka, zmain, gcol, grow, states, dhm, ncc):
    tp = qa.shape[0]
    nc = tp // LC
    cf0, cb0 = _chunk_maps(nc, ncc)
    cf = lambda t: cf0(nc - 1 - t)
    cb = lambda t: cb0(nc - 1 - t)
    csf, csb, nsf, nsb, msf, msb = states

    def body(qf, kf, vf, gcf, grf, cpf, npf, mpf, dhf, qb, kb, vb, gcb, grb, cpb_, npb, mpb, dhb,
             dqf_o, dkf_o, dvf_o, colf_o, rowf_o, dqb_o, dkb_o, dvb_o, colb_o, rowb_o, dc_sc, dn_sc):
        t = pl.program_id(0)

        @pl.when(t == 0)
        def _():
            dc_sc[...] = jnp.zeros_like(dc_sc)
            dn_sc[...] = jnp.zeros_like(dn_sc)

        row = lax.broadcasted_iota(jnp.int32, (LC, LC), 0)
        col = lax.broadcasted_iota(jnp.int32, (LC, LC), 1)
        dirs = ((qf, kf, vf, gcf, grf, cpf, npf, mpf, dhf, dqf_o, dkf_o, dvf_o, colf_o, rowf_o, cf),
                (qb, kb, vb, gcb, grb, cpb_, npb, mpb, dhb, dqb_o, dkb_o, dvb_o, colb_o, rowb_o, cb))
        for d, (q_ref, k_ref, v_ref, gc_ref, gr_ref, cp_ref, np_ref, mp_ref, dh_ref,
                dq_o, dk_o, dv_o, col_o, row_o, cm) in enumerate(dirs):
            mask = (col <= row) if d == 0 else (col >= row)
            live = jnp.where(cm(t) >= ncc, 1.0, 0.0).astype(F32)
            gc = gc_ref[...]
            gr = gr_ref[0]
            col_o[...] = jnp.zeros_like(col_o)
            row_o[...] = jnp.zeros_like(row_o)
            for h in range(NH):
                idx = d * NH + h
                hs = slice(h * DH, (h + 1) * DH)
                cp = cp_ref[0, hs, :]
                npv = np_ref[0, h:h + 1, :]
                m_prev = mp_ref[0, h:h + 1, 0:1]
                r = _mlstm_chunk(d, h, gc, gr, q_ref, k_ref, v_ref, cp, npv, m_prev, mask, precise=True)
                qh, kh, vh, e, w, s = r["qh"], r["kh"], r["vh"], r["e"], r["w"], r["s"]
                qf32, kf32 = qh.astype(F32), kh.astype(F32)
                den, thr = r["den"], r["thr"]
                rden = 1.0 / jnp.maximum(jnp.abs(den), thr)
                hh = r["num"] * rden
                dh = dh_ref[:, hs].astype(F32) * live
                dnum = dh * rden
                sgn = jnp.where(jnp.abs(den) > thr, jnp.sign(den), 0.0)
                dden = -jnp.sum(dh * hh, axis=1, keepdims=True) * rden * sgn
                dn_hi, dn_lo = _split2(dnum)
                ds = _dot_nt(dn_hi, vh) + _dot_nt(dn_lo, vh) + dden
                dp = ds * e
                gm = ds * s
                rowsum = jnp.sum(gm, axis=1, keepdims=True)
                colsum = jnp.sum(gm, axis=0, keepdims=True)
                dq = _dot(dp, kh) + w * (_dot_nt(dnum, r["cpb"]) + dden * npv)
                dcs = dc_sc[idx]
                dns = dn_sc[idx]
                kfac = jnp.exp(r["a_col"] - r["m_new"])
                dc_hi, dc_lo = _split2(dcs)
                vdc = _dot_nt(vh, dc_hi) + _dot_nt(vh, dc_lo)
                dk = _dot_tn(dp, qh) + kfac * (vdc + dns)
                dv = _dot_tn(s, dnum) + kfac * _dot(kh, dcs)
                beta = w * (jnp.sum(dnum * r["qc"], axis=1, keepdims=True) + dden * r["qn"])
                alpha = kfac * (jnp.sum(kf32 * vdc, axis=1, keepdims=True) + jnp.sum(kf32 * dns, axis=1, keepdims=True))
                dq_o[:, hs] = dq.astype(BF16)
                dk_o[:, hs] = dk.astype(BF16)
                dv_o[:, hs] = dv.astype(BF16)
                cpf = r["cpb"].astype(F32)
                inner = (jnp.sum(jnp.sum(dcs * cpf, axis=1, keepdims=True), axis=0, keepdims=True)
                         + jnp.sum(dns * npv, axis=1, keepdims=True))
                gam = jnp.sum(alpha, axis=0, keepdims=True) + r["a_old"] * inner
                lo = 8 * d + h
                col_o[:, lo:lo + 1] = alpha
                col_o[:, lo + 4:lo + 5] = rowsum + beta - alpha
                col_o[:, lo + 36:lo + 37] = jnp.broadcast_to(gam, (LC, 1))
                row_o[0, h:h + 1, :] = colsum
                wq = qf32 * w
                wq_hi, wq_lo = _split2(wq)
                dc_sc[idx] = (r["a_old"] * dcs + _dot_tn(wq_hi, dn_hi) + _dot_tn(wq_hi, dn_lo)
                              + _dot_tn(wq_lo, dn_hi))
                dn_sc[idx] = r["a_old"] * dns + jnp.sum(wq * dden, axis=0, keepdims=True)

    def dspecs(cm):
        return [pl.BlockSpec((LC, D), lambda t: (cm(t), 0)),
                pl.BlockSpec((LC, D), lambda t: (cm(t), 0)),
                pl.BlockSpec((LC, D), lambda t: (cm(t), CB_V)),
                pl.BlockSpec((LC, LANE), lambda t: (cm(t), 0)),
                pl.BlockSpec((1, 16, LC), lambda t: (cm(t), 0, 0)),
                pl.BlockSpec((1, D, DH), lambda t: (cm(t), 0, 0)),
                pl.BlockSpec((1, NH, DH), lambda t: (cm(t), 0, 0)),
                pl.BlockSpec((1, NH, LANE), lambda t: (cm(t), 0, 0)),
                pl.BlockSpec((LC, D), lambda t: (jnp.maximum(cm(t) - ncc, 0), 0))]

    def ospecs(cm):
        return [pl.BlockSpec((LC, D), lambda t: (cm(t), 0)),
                pl.BlockSpec((LC, D), lambda t: (cm(t), 0)),
                pl.BlockSpec((LC, D), lambda t: (cm(t), 0)),
                pl.BlockSpec((LC, LANE), lambda t: (cm(t), 0)),
                pl.BlockSpec((1, 8, LC), lambda t: (cm(t), 0, 0))]

    oshape = [jax.ShapeDtypeStruct((tp, D), BF16)] * 3 + [jax.ShapeDtypeStruct((tp, LANE), F32),
                                                        jax.ShapeDtypeStruct((nc, 8, LC), F32)]
    return pl.pallas_call(
        body, name="mlstm_bwd", grid=(nc,),
        in_specs=dspecs(cf) + dspecs(cb),
        out_specs=ospecs(cf) + ospecs(cb),
        out_shape=oshape + oshape,
        scratch_shapes=[pltpu.VMEM((2 * NH, DH, DH), F32), pltpu.VMEM((2 * NH, 1, DH), F32)],
        compiler_params=_cparams(("arbitrary",)),
    )(qa, ka, zmain, gcol, grow, csf, nsf, msf, dhm, qa, ka, zmain, gcol, grow, csb, nsb, msb, dhm)


def gates_bwd(colf, colb, cs, zgb):
    tp = colf.shape[0]

    def body(cf_ref, cb_ref, cs_ref, zgb_ref, o_ref, db_ref):
        i = pl.program_id(0)

        @pl.when(i == 0)
        def _():
            db_ref[...] = jnp.zeros_like(db_ref)

        lane = lax.broadcasted_iota(jnp.int32, (TR, LANE), 1)
        i_l = jnp.logical_or(lane < 4, jnp.logical_and(lane >= 8, lane < 12))
        f_l = jnp.logical_or(jnp.logical_and(lane >= 4, lane < 8), jnp.logical_and(lane >= 12, lane < 16))
        cv = cf_ref[...] + cb_ref[...]
        csv = cs_ref[...]
        gam = pltpu.roll(cv, LANE - 32, 1)
        dbh = jnp.where(f_l, cv - csv, 0.0)
        rr = lax.broadcasted_iota(jnp.int32, (TR, TR), 0)
        cc = lax.broadcasted_iota(jnp.int32, (TR, TR), 1)
        same = (rr // LC) == (cc // LC)
        low = jnp.where(jnp.logical_and(same, cc <= rr), 1.0, 0.0).astype(BF16)
        upp = jnp.where(jnp.logical_and(same, cc >= rr), 1.0, 0.0).astype(BF16)
        dlogf = jnp.where(lane < 8, _exact_dot(upp, dbh), _exact_dot(low, dbh)) + gam
        out = jnp.where(i_l, csv + cv, 0.0) + jnp.where(f_l, dlogf * _sigmoid(-zgb_ref[...]), 0.0)
        o_ref[...] = out
        db_ref[...] += jnp.sum(out, axis=0, keepdims=True)

    spec = _rb(TR, LANE)
    return pl.pallas_call(
        body, name="gates_bwd", grid=(tp // TR,),
        in_specs=[spec] * 4,
        out_specs=[spec, _bc(1, LANE)],
        out_shape=[jax.ShapeDtypeStruct((tp, LANE), F32), jax.ShapeDtypeStruct((1, LANE), F32)],
        compiler_params=_cparams(("arbitrary",)),
    )(colf, colb, cs, zgb)


def _head_norm(hm, gh):
    xs, rs = [], []
    for h in range(NH):
        seg = hm[:, h * DH:(h + 1) * DH]
        r = lax.rsqrt(jnp.mean(seg * seg, axis=-1, keepdims=True) + EPS)
        xs.append(seg * r)
        rs.append(r)
    xh = jnp.concatenate(xs, axis=1)
    return xh, rs, xh * gh


def _sgu_norm(vg, ln_g, ln_b):
    mu = jnp.mean(vg, axis=-1, keepdims=True)
    vc = vg - mu
    rstd = lax.rsqrt(jnp.mean(vc * vc, axis=-1, keepdims=True) + EPS)
    vhat = vc * rstd
    return vhat, rstd, vhat * ln_g + ln_b


def _sgu_mix(vn, ws_ref, bs_ref):
    rows = []
    for c in range(TR // SCH):
        cols = []
        for g in range(SG):
            blk = vn[c * SCH:(c + 1) * SCH, g * SGD:(g + 1) * SGD]
            cols.append(_dot(ws_ref[g * SCH:(g + 1) * SCH, :], blk) + bs_ref[:, g:g + 1])
        rows.append(jnp.concatenate(cols, axis=1))
    return jnp.concatenate(rows, axis=0)


def mixer_fwd(hf, hb, zmain, gh, ln_g, ln_b, ws, bs_t, n_ctx_tiles):
    t = hf.shape[0] - n_ctx_tiles * TR
    off = n_ctx_tiles

    def body(hf_ref, hb_ref, zo_ref, zu_ref, zv_ref, gh_ref, lg_ref, lb_ref, ws_ref, bs_ref, o_ref):
        hm = hf_ref[...].astype(F32) + hb_ref[...].astype(F32)
        _, _, hn = _head_norm(hm, gh_ref[...])
        o_ref[0] = (_sigmoid(zo_ref[...].astype(F32)) * hn).astype(BF16)
        _, _, vn = _sgu_norm(_gelu(zv_ref[...].astype(F32)), lg_ref[...], lb_ref[...])
        mixed = _sgu_mix(vn, ws_ref, bs_ref)
        o_ref[1] = (_gelu(zu_ref[...].astype(F32)) * mixed).astype(BF16)

    return pl.pallas_call(
        body, name="mixer_fwd", grid=(t // TR,),
        in_specs=[_rb(TR, D, 0, off), _rb(TR, D, 0, off), _rb(TR, D, CB_O, off), _rb(TR, D, CB_U, off),
                  _rb(TR, D, CB_VG, off), _bc(1, D), _bc(1, D), _bc(1, D), _bc(SG * SCH, SCH), _bc(SCH, LANE)],
        out_specs=pl.BlockSpec((2, TR, D), lambda i: (0, i, 0)),
        out_shape=jax.ShapeDtypeStruct((2, t, D), BF16),
        compiler_params=_cparams(("parallel",)),
    )(hf, hb, zmain, zmain, zmain, gh, ln_g, ln_b, ws, bs_t)


def merge_fwd(zmain, pp, n_ctx_tiles):
    t = pp.shape[1]
    off = n_ctx_tiles

    def body(zgm_ref, zgg_ref, pp_ref, o_ref):
        y = (_sigmoid(zgm_ref[...].astype(F32)) * pp_ref[0].astype(F32)
             + _sigmoid(zgg_ref[...].astype(F32)) * pp_ref[1].astype(F32))
        o_ref[...] = y.astype(BF16)

    return pl.pallas_call(
        body, name="merge_fwd", grid=(t // TR,),
        in_specs=[_rb(TR, D, CB_GM, off), _rb(TR, D, CB_GG, off), pl.BlockSpec((2, TR, D), lambda i: (0, i, 0))],
        out_specs=_rb(TR, D),
        out_shape=jax.ShapeDtypeStruct((t, D), BF16),
        compiler_params=_cparams(("parallel",)),
    )(zmain, zmain, pp)


def merge_bwd(zmain, pp, dy, tp, n_ctx_tiles):
    t = dy.shape[0]
    nt = tp // TR
    xrow = lambda i: jnp.maximum(i - n_ctx_tiles, 0)

    def body(zg_ref, pp_ref, dy_ref, dpp_ref, dz_ref):
        i = pl.program_id(1)
        zg = zg_ref[...].astype(F32)
        sg = _sigmoid(zg)
        dyv = dy_ref[...].astype(F32)
        dpp_ref[0] = (dyv * sg).astype(BF16)
        dzv = dyv * pp_ref[0].astype(F32) * sg * (1.0 - sg)
        dz_ref[...] = jnp.where(i >= n_ctx_tiles, dzv, 0.0).astype(BF16)

    return pl.pallas_call(
        body, name="merge_bwd", grid=(2, nt),
        in_specs=[pl.BlockSpec((TR, D), lambda j, i: (i, CB_GM + j)),
                  pl.BlockSpec((1, TR, D), lambda j, i: (j, xrow(i), 0)),
                  pl.BlockSpec((TR, D), lambda j, i: (xrow(i), 0))],
        out_specs=[pl.BlockSpec((1, TR, D), lambda j, i: (j, xrow(i), 0)),
                   pl.BlockSpec((TR, D), lambda j, i: (i, CB_GM + j))],
        out_shape=[jax.ShapeDtypeStruct((2, t, D), BF16), jax.ShapeDtypeStruct((tp, 8 * D), BF16)],
        compiler_params=_cparams(("arbitrary", "arbitrary")),
    )(zmain, pp, dy)


def mixer_bwd(dz, hf, hb, zmain, dyms, gh, ln_g, ln_b, ws, bs_t, n_ctx_tiles):
    tp = hf.shape[0]
    t = tp - n_ctx_tiles * TR
    nt = tp // TR
    xrow = lambda i: jnp.maximum(i - n_ctx_tiles, 0)

    def body(dz_in_ref, hf_ref, hb_ref, zo_ref, zu_ref, zv_ref, dy_ref, gh_ref, lg_ref, lb_ref, ws_ref, bs_ref,
             dz_ref, dhm_ref, dgh_ref, dlg_ref, dlb_ref, dws_ref, dbs_ref):
        del dz_in_ref
        i = pl.program_id(0)

        @pl.when(i == 0)
        def _():
            for ref in (dgh_ref, dlg_ref, dlb_ref, dws_ref, dbs_ref):
                ref[...] = jnp.zeros_like(ref)

        @pl.when(i < n_ctx_tiles)
        def _():
            dz_ref[...] = jnp.zeros_like(dz_ref)

        @pl.when(i >= n_ctx_tiles)
        def _():
            gh_v = gh_ref[...]
            hm = hf_ref[...].astype(F32) + hb_ref[...].astype(F32)
            xh, rs, hn = _head_norm(hm, gh_v)
            zo = zo_ref[...].astype(F32)
            so = _sigmoid(zo)
            dym = dy_ref[0].astype(F32)
            d_zo = dym * hn * so * (1.0 - so)
            d_hn = dym * so
            dgh_ref[...] += jnp.sum(d_hn * xh, axis=0, keepdims=True)
            d_xh = d_hn * gh_v
            segs = []
            for h in range(NH):
                hs = slice(h * DH, (h + 1) * DH)
                dx, xs = d_xh[:, hs], xh[:, hs]
                segs.append(rs[h] * (dx - xs * jnp.mean(dx * xs, axis=-1, keepdims=True)))
            dhm_ref[...] = jnp.concatenate(segs, axis=1).astype(BF16)
            zu = zu_ref[...].astype(F32)
            zv = zv_ref[...].astype(F32)
            lg = lg_ref[...]
            gu, dgu = _gelu_and_grad(zu)
            gv, dgv = _gelu_and_grad(zv)
            vhat, rstd, vn = _sgu_norm(gv, lg, lb_ref[...])
            mixed = _sgu_mix(vn, ws_ref, bs_ref)
            dys = dy_ref[1].astype(F32)
            d_zu = dys * mixed * dgu
            d_mixed = dys * gu
            rows = []
            for c in range(TR // SCH):
                cols = []
                for g in range(SG):
                    rsl, csl = slice(c * SCH, (c + 1) * SCH), slice(g * SGD, (g + 1) * SGD)
                    dm = d_mixed[rsl, csl]
                    cols.append(_dot_tn(ws_ref[g * SCH:(g + 1) * SCH, :], dm))
                    dws_ref[g * SCH:(g + 1) * SCH, :] += _dot_nt(dm, vn[rsl, csl])
                    dbs_ref[:, g:g + 1] += jnp.sum(dm, axis=1, keepdims=True)
                rows.append(jnp.concatenate(cols, axis=1))
            d_vn = jnp.concatenate(rows, axis=0)
            dlg_ref[...] += jnp.sum(d_vn * vhat, axis=0, keepdims=True)
            dlb_ref[...] += jnp.sum(d_vn, axis=0, keepdims=True)
            d_vhat = d_vn * lg
            d_vg = rstd * (d_vhat - jnp.mean(d_vhat, axis=-1, keepdims=True)
                           - vhat * jnp.mean(d_vhat * vhat, axis=-1, keepdims=True))
            d_zv = d_vg * dgv
            dz_ref[...] = jnp.concatenate([d_zo, d_zu, d_zv], axis=1).astype(BF16)

    return pl.pallas_call(
        body, name="mixer_bwd", grid=(nt,),
        in_specs=[pl.BlockSpec(memory_space=pl.ANY), _rb(TR, D), _rb(TR, D), _rb(TR, D, CB_O), _rb(TR, D, CB_U),
                  _rb(TR, D, CB_VG), pl.BlockSpec((2, TR, D), lambda i: (0, xrow(i), 0)),
                  _bc(1, D), _bc(1, D), _bc(1, D), _bc(SG * SCH, SCH), _bc(SCH, LANE)],
        out_specs=[_rb(TR, 3 * D), pl.BlockSpec((TR, D), lambda i: (xrow(i), 0)),
                   _bc(1, D), _bc(1, D), _bc(1, D), _bc(SG * SCH, SCH), _bc(SCH, LANE)],
        out_shape=[jax.ShapeDtypeStruct(dz.shape, BF16), jax.ShapeDtypeStruct((t, D), BF16),
                   jax.ShapeDtypeStruct((1, D), F32), jax.ShapeDtypeStruct((1, D), F32),
                   jax.ShapeDtypeStruct((1, D), F32), jax.ShapeDtypeStruct((SG * SCH, SCH), F32),
                   jax.ShapeDtypeStruct((SCH, LANE), F32)],
        input_output_aliases={0: 0},
        compiler_params=_cparams(("arbitrary",)),
    )(dz, hf, hb, zmain, zmain, zmain, dyms, gh, ln_g, ln_b, ws, bs_t)


FCB = DFF // 2
TF = 512


def _ffn_halo(col, t):
    per = TF // GW
    last = t // GW - 1
    prev = pl.BlockSpec((GW, FCB), lambda i, j: (jnp.maximum(i * per - 1, 0), col(j)))
    nxt = pl.BlockSpec((GW, FCB), lambda i, j: (jnp.minimum((i + 1) * per, last), col(j)))
    return prev, nxt


def _conv_taps(ext):
    n = ext.shape[0]
    colid = lax.broadcasted_iota(jnp.int32, (n, 1), 0) % GW
    left = pltpu.roll(jnp.where(colid != GW - 1, ext, 0.0), 1, 0)
    right = pltpu.roll(jnp.where(colid != 0, ext, 0.0), n - 1, 0)
    views = (left, ext, right)
    return {(ky, kx): views[kx][GW * ky:GW * ky + TF] for ky in range(3) for kx in range(3)}


def _ext(c_ref, p_ref, n_ref, i, nt):
    pr = jnp.where(i == 0, 0.0, p_ref[...].astype(F32))
    nx = jnp.where(i == nt - 1, 0.0, n_ref[...].astype(F32))
    return jnp.concatenate([pr, c_ref[...].astype(F32), nx], axis=0)


def ffn_act_fwd(up, wc):
    t = up.shape[0]
    nt = t // TF
    prev_s, next_s = _ffn_halo(lambda j: j, t)

    def body(a_ref, ap_ref, an_ref, b_ref, w_ref, o_ref, ac_ref):
        i = pl.program_id(0)
        taps = _conv_taps(_ext(a_ref, ap_ref, an_ref, i, nt))
        ac = sum(w_ref[3 * ky + kx:3 * ky + kx + 1, :] * taps[(ky, kx)] for ky in range(3) for kx in range(3))
        ac_ref[...] = ac.astype(BF16)
        o_ref[...] = (_silu(ac) * b_ref[...].astype(F32)).astype(BF16)

    spec = pl.BlockSpec((TF, FCB), lambda i, j: (i, j))
    return pl.pallas_call(
        body, name="ffn_act_fwd", grid=(nt, 2),
        in_specs=[spec, prev_s, next_s,
                  pl.BlockSpec((TF, FCB), lambda i, j: (i, 2 + j)), pl.BlockSpec((16, FCB), lambda i, j: (0, j))],
        out_specs=[spec, spec],
        out_shape=[jax.ShapeDtypeStruct((t, DFF), BF16), jax.ShapeDtypeStruct((t, DFF), BF16)],
        compiler_params=_cparams(("parallel", "parallel")),
    )(up, up, up, up, wc)


def ffn_act_bwd(up, ac, dact):
    t = up.shape[0]
    nt = t // TF

    def body(b_ref, ac_ref, da_ref, dup_ref, dac_ref):
        acv = ac_ref[...].astype(F32)
        da = da_ref[...].astype(F32)
        s = _sigmoid(acv)
        dup_ref[...] = (da * acv * s).astype(BF16)
        dac_ref[...] = (da * b_ref[...].astype(F32) * s * (1.0 + acv * (1.0 - s))).astype(BF16)

    spec = pl.BlockSpec((TF, FCB), lambda i, j: (i, j))
    bspec = pl.BlockSpec((TF, FCB), lambda i, j: (i, 2 + j))
    return pl.pallas_call(
        body, name="ffn_act_bwd", grid=(nt, 2),
        in_specs=[bspec, spec, spec],
        out_specs=[bspec, spec],
        out_shape=[jax.ShapeDtypeStruct((t, 2 * DFF), BF16), jax.ShapeDtypeStruct((t, DFF), BF16)],
        compiler_params=_cparams(("parallel", "parallel")),
    )(up, ac, dact)


def ffn_conv_bwd(dup, up, dac, wc):
    t = up.shape[0]
    nt = t // TF
    prev_g, next_g = _ffn_halo(lambda j: j, t)

    strip = 16

    def body(dup_in_ref, a_ref, g_ref, gp_ref, gn_ref, w_ref, o_ref, dw_ref, lv_ref, cv_ref, rv_ref, part_ref):
        del dup_in_ref
        i = pl.program_id(1)

        @pl.when(i == 0)
        def _():
            dw_ref[...] = jnp.zeros_like(dw_ref)

        ext = _ext(g_ref, gp_ref, gn_ref, i, nt)
        n = ext.shape[0]
        colid = lax.broadcasted_iota(jnp.int32, (n, 1), 0) % GW
        lv_ref[...] = pltpu.roll(jnp.where(colid != GW - 1, ext, 0.0), 1, 0)
        cv_ref[...] = ext
        rv_ref[...] = pltpu.roll(jnp.where(colid != 0, ext, 0.0), n - 1, 0)
        part_ref[...] = jnp.zeros_like(part_ref)
        views = (lv_ref, cv_ref, rv_ref)

        def one_strip(r, carry):
            r0 = pl.multiple_of(r * strip, strip)
            a = a_ref[pl.ds(r0, strip), :].astype(F32)
            acc = jnp.zeros((strip, FCB), F32)
            for ky in range(3):
                for kx in range(3):
                    kf = 3 * (2 - ky) + (2 - kx)
                    tap = views[kx][pl.ds(r0 + GW * ky, strip), :]
                    acc = acc + w_ref[kf:kf + 1, :] * tap
                    p = a * tap
                    part_ref[8 * kf:8 * kf + 8, :] += p[0:8] + p[8:16]
            o_ref[pl.ds(r0, strip), :] = acc.astype(BF16)
            return carry

        lax.fori_loop(0, TF // strip, one_strip, 0)
        for k in range(9):
            dw_ref[k:k + 1, :] += jnp.sum(part_ref[8 * k:8 * k + 8, :], axis=0, keepdims=True)

    sw = lambda s: pl.BlockSpec(s.block_shape, lambda j, i, f=s.index_map: f(i, j))
    spec = pl.BlockSpec((TF, FCB), lambda j, i: (i, j))
    return pl.pallas_call(
        body, name="ffn_conv_bwd", grid=(2, nt),
        in_specs=[pl.BlockSpec(memory_space=pl.ANY), spec, spec, sw(prev_g), sw(next_g),
                  pl.BlockSpec((16, FCB), lambda j, i: (0, j))],
        out_specs=[spec, pl.BlockSpec((16, FCB), lambda j, i: (0, j))],
        out_shape=[jax.ShapeDtypeStruct(dup.shape, BF16), jax.ShapeDtypeStruct((16, DFF), F32)],
        scratch_shapes=[pltpu.VMEM((TF + 2 * GW, FCB), F32)] * 3 + [pltpu.VMEM((72, FCB), F32)],
        input_output_aliases={0: 0},
        compiler_params=_cparams(("arbitrary", "arbitrary")),
    )(dup, up, dac, dac, dac, wc)


def head_fwd_bwd(h1, f, target, gfin, tab):
    t = h1.shape[0]

    def body(h1_ref, f_ref, t_ref, g_ref, tab_ref, dh2_ref, df_ref, acc_ref):
        i = pl.program_id(0)

        @pl.when(i == 0)
        def _():
            acc_ref[...] = jnp.zeros_like(acc_ref)

        gate = tab_ref[0:1, :]
        fv = f_ref[...].astype(F32)
        h2 = h1_ref[...] + gate * fv
        r = lax.rsqrt(jnp.mean(h2 * h2, axis=-1, keepdims=True) + EPS)
        xh = h2 * r
        gv = g_ref[...]
        err = xh * gv - t_ref[...]
        acc_ref[0:1, :] += jnp.sum(0.5 * jnp.mean(err * err, axis=-1, keepdims=True), axis=0, keepdims=True)
        dy = err * (1.0 / D)
        acc_ref[1:2, :] += jnp.sum(dy * xh, axis=0, keepdims=True)
        dxh = dy * gv
        dh2 = r * (dxh - xh * jnp.mean(dxh * xh, axis=-1, keepdims=True))
        dh2_ref[...] = dh2
        acc_ref[2:3, :] += jnp.sum(dh2 * fv, axis=0, keepdims=True)
        df_ref[...] = (dh2 * gate).astype(BF16)

    return pl.pallas_call(
        body, name="head_fwd_bwd", grid=(t // TR2,),
        in_specs=[_rb(TR2, D), _rb(TR2, D), _rb(TR2, D), _bc(1, D), _bc(8, D)],
        out_specs=[_rb(TR2, D), _rb(TR2, D), _bc(8, D)],
        out_shape=[jax.ShapeDtypeStruct((t, D), F32), jax.ShapeDtypeStruct((t, D), BF16),
                   jax.ShapeDtypeStruct((8, D), F32)],
        compiler_params=_cparams(("arbitrary",)),
    )(h1, f, target, gfin, tab)


def norm2_bwd(h1, dhn2, dh2, out, g, tab):
    t = h1.shape[0]

    def body(h1_ref, dhn_ref, dh2_ref, out_ref, g_ref, tab_ref, dh1_ref, dout_ref, acc_ref):
        i = pl.program_id(0)

        @pl.when(i == 0)
        def _():
            acc_ref[...] = jnp.zeros_like(acc_ref)

        h1v = h1_ref[...]
        r = lax.rsqrt(jnp.mean(h1v * h1v, axis=-1, keepdims=True) + EPS)
        xh = h1v * r
        gv = g_ref[...]
        dhn = dhn_ref[...].astype(F32)
        acc_ref[0:1, :] += jnp.sum(dhn, axis=0, keepdims=True)
        acc_ref[1:2, :] += jnp.sum(dhn * xh * gv, axis=0, keepdims=True)
        dn = dhn * (1.0 + tab_ref[2:3, :])
        acc_ref[2:3, :] += jnp.sum(dn * xh, axis=0, keepdims=True)
        dxh = dn * gv
        dh1 = dh2_ref[...] + r * (dxh - xh * jnp.mean(dxh * xh, axis=-1, keepdims=True))
        dh1_ref[...] = dh1
        acc_ref[3:4, :] += jnp.sum(dh1 * out_ref[...].astype(F32), axis=0, keepdims=True)
        dout_ref[...] = (dh1 * tab_ref[0:1, :]).astype(BF16)

    return pl.pallas_call(
        body, name="norm2_bwd", grid=(t // TR2,),
        in_specs=[_rb(TR2, D), _rb(TR2, D), _rb(TR2, D), _rb(TR2, D), _bc(1, D), _bc(8, D)],
        out_specs=[_rb(TR2, D), _rb(TR2, D), _bc(8, D)],
        out_shape=[jax.ShapeDtypeStruct((t, D), F32), jax.ShapeDtypeStruct((t, D), BF16),
                   jax.ShapeDtypeStruct((8, D), F32)],
        compiler_params=_cparams(("arbitrary",)),
    )(h1, dhn2, dh2, out, g, tab)


def norm1_bwd(ctx, x, da, db, dh1, g, tab, n_ctx_tiles):
    t = x.shape[0]
    tp = t + ctx.shape[0]
    xrow = lambda i: jnp.maximum(i - n_ctx_tiles, 0)

    def body(c_ref, x_ref, da_ref, db_ref, dh1_ref, g_ref, tab_ref, dx_ref, acc_ref):
        i = pl.program_id(0)

        @pl.when(i == 0)
        def _():
            acc_ref[...] = jnp.zeros_like(acc_ref)

        is_ctx = i < n_ctx_tiles
        x = jnp.where(is_ctx, c_ref[...], x_ref[...])
        r = lax.rsqrt(jnp.mean(x * x, axis=-1, keepdims=True) + EPS)
        xh = x * r
        gv = g_ref[...]
        dhn = da_ref[...].astype(F32) + db_ref[...]
        s_shift = jnp.sum(dhn, axis=0, keepdims=True)
        s_scale = jnp.sum(dhn * xh * gv, axis=0, keepdims=True)

        @pl.when(is_ctx)
        def _():
            acc_ref[0:1, :] += s_shift
            acc_ref[1:2, :] += s_scale

        @pl.when(jnp.logical_not(is_ctx))
        def _():
            acc_ref[2:3, :] += s_shift
            acc_ref[3:4, :] += s_scale

        acc_ref[5:6, :] += s_shift
        acc_ref[6:7, :] += s_scale
        sc = jnp.where(is_ctx, tab_ref[1:2, :], tab_ref[3:4, :])
        dn = dhn * (1.0 + sc)
        acc_ref[4:5, :] += jnp.sum(dn * xh, axis=0, keepdims=True)
        dxh = dn * gv
        dx_ref[...] = dh1_ref[...] + r * (dxh - xh * jnp.mean(dxh * xh, axis=-1, keepdims=True))

    return pl.pallas_call(
        body, name="norm1_bwd", grid=(tp // TR,),
        in_specs=_ctx_x_specs(n_ctx_tiles) + [_rb(TR, D), _rb(TR, D), pl.BlockSpec((TR, D), lambda i: (xrow(i), 0)),
                                              _bc(1, D), _bc(8, D)],
        out_specs=[pl.BlockSpec((TR, D), lambda i: (xrow(i), 0)), _bc(8, D)],
        out_shape=[jax.ShapeDtypeStruct((t, D), F32), jax.ShapeDtypeStruct((8, D), F32)],
        compiler_params=_cparams(("arbitrary",)),
    )(ctx, x, da, db, dh1, g, tab)


def adamw(w, g, m, v, name):
    lead = w.ndim - 2
    rows, cols = w.shape[-2:]
    tm = max(t for t in range(8, rows + 1, 8) if rows % t == 0 and (t * cols <= 512 * 1024 or t == 8))
    c1 = 1.0 / (1.0 - ADAM_B1 ** ADAM_STEP)
    c2 = 1.0 / (1.0 - ADAM_B2 ** ADAM_STEP)

    def body(w_ref, g_ref, m_ref, v_ref, d_ref, mo_ref, vo_ref):
        gv = g_ref[...]
        mn = ADAM_B1 * m_ref[...] + (1.0 - ADAM_B1) * gv
        vn = ADAM_B2 * v_ref[...] + (1.0 - ADAM_B2) * (gv * gv)
        mo_ref[...] = mn
        vo_ref[...] = vn
        d_ref[...] = -ADAM_LR * ((mn * c1) / (jnp.sqrt(vn * c2) + ADAM_EPS) + ADAM_WD * w_ref[...])

    spec = pl.BlockSpec((1,) * lead + (tm, cols), lambda i: (0,) * lead + (i, 0))
    sds = jax.ShapeDtypeStruct(w.shape, F32)
    return pl.pallas_call(
        body, name=name, grid=(rows // tm,),
        in_specs=[spec] * 4, out_specs=[spec] * 3, out_shape=[sds] * 3,
        compiler_params=_cparams(("parallel",)),
    )(w, g, m, v)


def add_n(arrs, out_dtype, name):
    shp = arrs[0].shape
    cols = shp[-1]
    flat = [a.reshape(-1, cols) for a in arrs]
    rows = flat[0].shape[0]
    tm = max(t for t in range(16, rows + 1, 16) if rows % t == 0 and t * cols <= 512 * 1024)

    def body(*refs):
        acc = refs[0][...].astype(F32)
        for r in refs[1:-1]:
            acc = acc + r[...].astype(F32)
        refs[-1][...] = acc.astype(refs[-1].dtype)

    spec = pl.BlockSpec((tm, cols), lambda i: (i, 0))
    out = pl.pallas_call(
        body, name=name, grid=(rows // tm,),
        in_specs=[spec] * len(flat), out_specs=spec, out_shape=jax.ShapeDtypeStruct((rows, cols), out_dtype),
        compiler_params=_cparams(("parallel",)),
    )(*flat)
    return out.reshape(shp)


def sum8(stack, name):
    _, rows, cols = stack.shape
    tm = rows if rows <= 2048 else _pick(rows, (512, 256, 128, 64, 8))

    def body(s_ref, o_ref):
        acc = s_ref[0]
        for k in range(1, 8):
            acc = acc + s_ref[k]
        o_ref[...] = acc

    return pl.pallas_call(
        body, name=name, grid=(rows // tm,),
        in_specs=[pl.BlockSpec((8, tm, cols), lambda i: (0, i, 0))],
        out_specs=pl.BlockSpec((tm, cols), lambda i: (i, 0)),
        out_shape=jax.ShapeDtypeStruct((rows, cols), F32),
        compiler_params=_cparams(("parallel",)),
    )(stack)


def _coords():
    return lax.axis_index("x"), lax.axis_index("y"), lax.axis_index("c")


def _other_chips(x, y):
    return [(1 - x, y), (x, 1 - y), (1 - x, 1 - y)]


_ANY = pl.BlockSpec(memory_space=pl.ANY)


def gather_chips(slabs):
    ns = len(slabs)

    def body(*refs):
        x_refs, out_refs = refs[:ns], refs[ns:2 * ns]
        send_sems, recv_sems = refs[2 * ns:]
        x, y, c = _coords()
        me = 2 * x + y
        sibling = (x, y, 1 - c)
        chips = _other_chips(x, y)

        def half(s, chip, hc):
            rh = slabs[s].shape[0] // 2
            return out_refs[s].at[chip, pl.ds(hc * rh, rh), :]

        def own_half(s):
            rh = slabs[s].shape[0] // 2
            return x_refs[s].at[pl.ds(c * rh, rh), :]

        sends = []
        for s in range(ns):
            for j, (px, py) in enumerate(chips):
                cp = pltpu.make_async_remote_copy(
                    src_ref=own_half(s), dst_ref=half(s, me, c), send_sem=send_sems.at[6 * s + j],
                    recv_sem=recv_sems.at[6 * s + j], device_id=(px, py, c), device_id_type=MESH)
                cp.start()
                sends.append(cp)
        for s in range(ns):
            for j, (px, py) in enumerate(chips):
                src = 2 * px + py
                landed = pltpu.make_async_remote_copy(
                    src_ref=half(s, src, c), dst_ref=half(s, src, c), send_sem=send_sems.at[6 * s + j],
                    recv_sem=recv_sems.at[6 * s + j], device_id=(px, py, c), device_id_type=MESH)
                landed.wait_recv()
                fw = pltpu.make_async_remote_copy(
                    src_ref=half(s, src, c), dst_ref=half(s, src, c), send_sem=send_sems.at[6 * s + 3 + j],
                    recv_sem=recv_sems.at[6 * s + 3 + j], device_id=sibling, device_id_type=MESH)
                fw.start()
                sends.append(fw)
        for s in range(ns):
            for j, (px, py) in enumerate(chips):
                src = 2 * px + py
                got = pltpu.make_async_remote_copy(
                    src_ref=half(s, src, 1 - c), dst_ref=half(s, src, 1 - c), send_sem=send_sems.at[6 * s + 3 + j],
                    recv_sem=recv_sems.at[6 * s + 3 + j], device_id=sibling, device_id_type=MESH)
                got.wait_recv()
        for cp in sends:
            cp.wait_send()

    return pl.pallas_call(
        body, name="gather_chips",
        in_specs=[_ANY] * ns, out_specs=[_ANY] * ns,
        out_shape=[jax.ShapeDtypeStruct((4,) + s.shape, s.dtype) for s in slabs],
        scratch_shapes=[pltpu.SemaphoreType.DMA((6 * ns,)), pltpu.SemaphoreType.DMA((6 * ns,))],
    )(*slabs)


def swap_halves(gss):
    ns = len(gss)

    def body(*refs):
        g_refs, out_refs = refs[:ns], refs[ns:2 * ns]
        send_sems, recv_sems = refs[2 * ns:]
        x, y, c = _coords()
        cps = []
        for s in range(ns):
            rh = gss[s].shape[1] // 2
            cp = pltpu.make_async_remote_copy(
                src_ref=g_refs[s].at[:, pl.ds((1 - c) * rh, rh), :], dst_ref=out_refs[s],
                send_sem=send_sems.at[s], recv_sem=recv_sems.at[s], device_id=(x, y, 1 - c), device_id_type=MESH)
            cp.start()
            cps.append(cp)
        for cp in cps:
            cp.wait()

    return pl.pallas_call(
        body, name="swap_halves",
        in_specs=[_ANY] * ns, out_specs=[_ANY] * ns,
        out_shape=[jax.ShapeDtypeStruct((4, g.shape[1] // 2, g.shape[2]), g.dtype) for g in gss],
        scratch_shapes=[pltpu.SemaphoreType.DMA((ns,)), pltpu.SemaphoreType.DMA((ns,))],
    )(*gss)


def join_halves(reds):
    ns = len(reds)

    def body(*refs):
        r_refs, out_refs = refs[:ns], refs[ns:2 * ns]
        send_sems, recv_sems = refs[2 * ns:]
        x, y, c = _coords()
        cps = []
        for s in range(ns):
            cp = pltpu.make_async_remote_copy(
                src_ref=r_refs[s], dst_ref=out_refs[s], send_sem=send_sems.at[s], recv_sem=recv_sems.at[s],
                device_id=(x, y, 1 - c), device_id_type=MESH)
            cp.start()
            cps.append(cp)
        for cp in cps:
            cp.wait()

    return pl.pallas_call(
        body, name="join_halves",
        in_specs=[_ANY] * ns, out_specs=[_ANY] * ns,
        out_shape=[jax.ShapeDtypeStruct(r.shape, r.dtype) for r in reds],
        scratch_shapes=[pltpu.SemaphoreType.DMA((ns,)), pltpu.SemaphoreType.DMA((ns,))],
    )(*reds)


def gather_all(vec, name):
    r, wd = vec.shape

    def body(v_ref, out_ref, send_sems, recv_sems):
        x, y, c = _coords()
        me = 4 * x + 2 * y + c
        cps = []
        for k in range(1, 8):
            mx, my, mc = (k >> 2) & 1, (k >> 1) & 1, k & 1
            peer = (x ^ mx, y ^ my, c ^ mc)
            cp = pltpu.make_async_remote_copy(
                src_ref=v_ref, dst_ref=out_ref.at[me],
                send_sem=send_sems.at[k - 1], recv_sem=recv_sems.at[k - 1], device_id=peer, device_id_type=MESH)
            cp.start()
            cps.append(cp)
        for k in range(1, 8):
            mx, my, mc = (k >> 2) & 1, (k >> 1) & 1, k & 1
            peer = (x ^ mx, y ^ my, c ^ mc)
            src = 4 * peer[0] + 2 * peer[1] + peer[2]
            got = pltpu.make_async_remote_copy(
                src_ref=v_ref, dst_ref=out_ref.at[src],
                send_sem=send_sems.at[k - 1], recv_sem=recv_sems.at[k - 1], device_id=peer, device_id_type=MESH)
            got.wait_recv()
        for cp in cps:
            cp.wait_send()

    return pl.pallas_call(
        body, name=name,
        in_specs=[_ANY], out_specs=_ANY,
        out_shape=jax.ShapeDtypeStruct((8, r, wd), vec.dtype),
        scratch_shapes=[pltpu.SemaphoreType.DMA((7,)), pltpu.SemaphoreType.DMA((7,))],
    )(vec)


def _pad_rows(a, rows):
    return jnp.pad(a, ((0, rows - a.shape[0]), (0, 0)))


def _pad_cols(a, cols):
    return jnp.pad(a, ((0, 0), (0, cols - a.shape[1])))


def local_step(x, c, ctx, c_ctx, target, wt, sm, late_weights=None, grad_hook=None):
    t, tc = x.shape[0], ctx.shape[0]
    tp = t + tc
    nct = tc // TR
    ncc = tc // LC
    nc = tp // LC

    w_in = wt["w_in"]
    segs = {"q": (0, D), "k": (D, 2 * D), "v": (2 * D, 3 * D), "g": (3 * D, 3 * D + NGATE)}
    base = 3 * D + NGATE
    for n_i, nm in enumerate(("o", "u", "vg", "gm", "gg")):
        segs[nm] = (base + n_i * D, base + (n_i + 1) * D)
    order = ("o", "u", "vg", "gm", "gg", "v", "q", "k")
    w_main = jnp.concatenate([w_in[:, segs[nm][0]:segs[nm][1]] for nm in order], axis=1)
    w_g = _pad_cols(w_in[:, segs["g"][0]:segs["g"][1]], LANE)
    w_main_t = w_main.T
    w_g_t = w_g.T

    cc = _pad_rows(jnp.concatenate([c.reshape(1, D), c_ctx.reshape(1, D)], axis=0), 16)
    modv = mod_fwd(cc, wt["w_mod"], sm["b_mod"].reshape(1, NMOD * D))
    mx = modv[0].reshape(NMOD, D)
    mc = modv[1].reshape(NMOD, D)
    tab1 = _pad_rows(jnp.stack([mc[0], mc[1], mx[0], mx[1]]), 8)
    tab2 = _pad_rows(jnp.stack([mx[2], mx[3], mx[4]]), 8)
    tab3 = _pad_rows(mx[5:6], 8)

    g1 = sm["norm1_g"].reshape(1, D)
    g2 = sm["norm2_g"].reshape(1, D)
    gfin = sm["final_g"].reshape(1, D)
    gh = sm["head_norm_g"].reshape(1, D)
    ln_g = sm["sgu_ln_g"].reshape(1, D)
    ln_b = sm["sgu_ln_b"].reshape(1, D)
    ws = sm["w_s"].reshape(SG * SCH, SCH).astype(BF16)
    bs_t = _pad_cols(sm["b_s"].reshape(SG, SCH).T, LANE)
    conv_w = _pad_rows(sm["conv_qk"].reshape(3, 2 * D), 8)
    b_gate = _pad_cols(sm["b_gate"].reshape(1, NGATE), LANE)
    wc = _pad_rows(sm["w_ffn_conv"].reshape(9, DFF), 16)

    hn1 = norm1_fwd(ctx, x, g1, tab1, nct)
    if late_weights is None:
        zmain = mm_nn(hn1, w_main, BF16, "mm_zmain")
    else:
        zmain, landed = mm_nn(hn1, w_main, BF16, "mm_zmain", rider=late_weights[0])
        wt = {**wt, **late_weights[1](landed)}
    zg = mm_nn(hn1, w_g, F32, "mm_zg")
    qa, ka, gcol, zgb = qkconv_fwd(zmain, zg, conv_w, b_gate, nct)
    grow = gcol[:, :16].reshape(nc, LC, 16).transpose(0, 2, 1)
    hf, hb, csf, csb, nsf, nsb, msf, msb = mlstm_fwd(qa, ka, zmain, gcol, grow, ncc)
    yms = mixer_fwd(hf, hb, zmain, gh, ln_g, ln_b, ws, bs_t, nct)
    w_br = jnp.stack([wt["w_branch_mlstm"], wt["w_branch_sgu"]])
    pp = mm_nn(yms, w_br, BF16, "mm_branch")
    y = merge_fwd(zmain, pp, nct)
    out = mm_nn(y, wt["w_out"], BF16, "mm_out")
    h1, hn2 = norm2_fwd(x, out, g2, tab2)
    up = mm_nn(hn2, wt["w_up"], BF16, "mm_up")
    act, ac = ffn_act_fwd(up, wc)
    f = mm_nn(act, wt["w_down"], BF16, "mm_down")
    dh2, df, acc_h = head_fwd_bwd(h1, f, target, gfin, tab3)
    loss = acc_h[0, 0]

    g_w_down = mm_tn(act, df, "mmt_down", BF16)
    dact = mm_nn(df, wt["w_down"].T, BF16, "mm_ddown")
    dup, dac = ffn_act_bwd(up, ac, dact)
    dup, g_wc = ffn_conv_bwd(dup, up, dac, wc)
    g_w_up = mm_tn(hn2, dup, "mmt_up", BF16)
    dhn2 = mm_nn(dup, wt["w_up"].T, BF16, "mm_dup")
    dh1, dout, acc_2 = norm2_bwd(h1, dhn2, dh2, out, g2, tab2)
    g_w_out = mm_tn(y, dout, "mmt_out", BF16)
    dy = mm_nn(dout, wt["w_out"].T, BF16, "mm_dout")
    dpp, dz = merge_bwd(zmain, pp, dy, tp, nct)
    g_w_br = mm_tn(yms, dpp, "mmt_branch", BF16)
    dyms = mm_nn(dpp, jnp.stack([wt["w_branch_mlstm"].T, wt["w_branch_sgu"].T]), BF16, "mm_dbranch")
    dz, dhm, g_gh, g_lng, g_lnb, g_ws, g_bs = mixer_bwd(dz, hf, hb, zmain, dyms, gh, ln_g, ln_b, ws, bs_t, nct)
    (dqf, dkf, dvf, colf, rowf, dqb, dkb, dvb, colb, rowb) = mlstm_bwd(
        qa, ka, zmain, gcol, grow, (csf, csb, nsf, nsb, msf, msb), dhm, ncc)

    csum_f = rowf[:, :4, :].transpose(0, 2, 1).reshape(tp, 4)
    csum_b = rowb[:, :4, :].transpose(0, 2, 1).reshape(tp, 4)
    cs = _pad_cols(jnp.concatenate([csum_f, csum_f, csum_b, csum_b], axis=1), LANE)
    dzg, g_bgate = gates_bwd(colf, colb, cs, zgb)

    dc, g_convw = qkconv_bwd_a(zmain, dqf, dqb, dkf, dkb, conv_w, nct)
    dz = qkconv_bwd_b(dz, dc, conv_w, nct)
    dz = add_into_dz(dz, dvf, dvb, CB_V)

    g_w_main = mm_tn(hn1, dz, "mmt_main", BF16)
    g_w_g = mm_tn(hn1, dzg, "mmt_g", BF16)
    blk = lambda cb: g_w_main[:, cb * D:(cb + 1) * D]
    g_w_in = jnp.concatenate([blk(CB_Q), blk(CB_K), blk(CB_V), g_w_g[:, :NGATE], blk(CB_O), blk(CB_U), blk(CB_VG),
                              blk(CB_GM), blk(CB_GG)], axis=1)
    big = {"w_in": g_w_in, "w_branch_mlstm": g_w_br[0], "w_branch_sgu": g_w_br[1], "w_out": g_w_out,
           "w_up": g_w_up, "w_down": g_w_down}
    if grad_hook is None:
        da, received = mm_nn(dz, w_main_t, BF16, "mm_dmain"), None
    else:
        da, received = mm_nn(dz, w_main_t, BF16, "mm_dmain", rider=grad_hook(big))
    db = mm_nn(dzg, w_g_t, F32, "mm_dg")
    grad_x, acc_1 = norm1_bwd(ctx, x, da, db, dh1, g1, tab1, nct)

    d_modx = jnp.concatenate([acc_1[2], acc_1[3], acc_2[3], acc_2[0], acc_2[1], acc_h[2]])
    d_modc = jnp.concatenate([acc_1[0], acc_1[1], jnp.zeros((4 * D,), F32)])
    d_modb = jnp.concatenate([acc_1[5], acc_1[6], acc_2[3], acc_2[0], acc_2[1], acc_h[2]])

    small = {"b_mod": d_modb, "norm1_g": acc_1[4], "b_gate": g_bgate[0, :NGATE], "conv_qk": g_convw[:3].reshape(-1),
             "head_norm_g": g_gh[0], "sgu_ln_g": g_lng[0], "sgu_ln_b": g_lnb[0], "w_s": g_ws.reshape(-1),
             "b_s": g_bs[:, :SG].T.reshape(-1), "norm2_g": acc_2[2], "w_ffn_conv": g_wc[:9].reshape(-1),
             "final_g": acc_h[1]}
    return loss, grad_x, big, small, d_modx, d_modc, received


def mod_bwd_w(a_all, dm_all, name):
    n = dm_all.shape[1]
    tn = _pick(n, (512, 128))

    def body(a_ref, d_ref, o_ref):
        o_ref[...] = _dot_tn(_silu(a_ref[...]), d_ref[...])

    return pl.pallas_call(
        body, name=name, grid=(n // tn,),
        in_specs=[_bc(16, D), pl.BlockSpec((16, tn), lambda j: (0, j))],
        out_specs=pl.BlockSpec((D, tn), lambda j: (0, j)),
        out_shape=jax.ShapeDtypeStruct((D, n), F32),
        compiler_params=_cparams(("parallel",)),
    )(a_all, dm_all)


def mod_bwd_cctx(dmc, w_mod_t, c_ctx):
    def body(d_ref, w_ref, c_ref, o_ref):
        o_ref[...] = _dot(d_ref[...], w_ref[...]) * _dsilu(c_ref[...])

    return pl.pallas_call(
        body, name="mod_bwd_cctx", grid=(1,),
        in_specs=[_bc(16, 2 * D), _bc(2 * D, D), _bc(1, D)],
        out_specs=_bc(16, D),
        out_shape=jax.ShapeDtypeStruct((16, D), F32),
        compiler_params=_cparams(("arbitrary",)),
    )(dmc, w_mod_t, c_ctx)


BIG = ("w_mod", "w_in", "w_branch_mlstm", "w_branch_sgu", "w_out", "w_up", "w_down")
BIG_AXIS = {"w_mod": 1, "w_in": 1, "w_branch_mlstm": 0, "w_branch_sgu": 0, "w_out": 0, "w_up": 1, "w_down": 0}
SMALL = ("c_ctx", "b_mod", "norm1_g", "b_gate", "conv_qk", "head_norm_g", "sgu_ln_g", "sgu_ln_b", "w_s", "b_s",
         "norm2_g", "w_ffn_conv", "final_g")
SMALL_SHARDED = {"conv_qk": (3, 2 * D), "w_ffn_conv": (9, DFF)}
PACK_ALIGN = 32 * D


def _pack(arrs, dtype, align=PACK_ALIGN, width=D):
    flat = jnp.concatenate([a.reshape(-1).astype(dtype) for a in arrs])
    n = flat.shape[0]
    padded = -(-n // align) * align
    return jnp.pad(flat, (0, padded - n)).reshape(padded // width, width)


def _unpack(slab, shapes):
    flat = slab.reshape(-1)
    outs, off = [], 0
    for shp in shapes:
        n = math.prod(shp)
        outs.append(flat[off:off + n].reshape(shp))
        off += n
    return outs


def _round_up(n, m):
    return -(-n // m) * m


def kernel(x, c, ctx, c_ctx, w_mod, b_mod, norm1_g, w_in, b_gate, conv_qk, head_norm_g, sgu_ln_g, sgu_ln_b, w_s, b_s, w_branch_mlstm, w_branch_sgu, w_out, norm2_g, w_up, w_ffn_conv, w_down, final_g, loss_target, m_c_ctx, m_w_mod, m_b_mod, m_norm1_g, m_w_in, m_b_gate, m_conv_qk, m_head_norm_g, m_sgu_ln_g, m_sgu_ln_b, m_w_s, m_b_s, m_w_branch_mlstm, m_w_branch_sgu, m_w_out, m_norm2_g, m_w_up, m_w_ffn_conv, m_w_down, m_final_g, v_c_ctx, v_w_mod, v_b_mod, v_norm1_g, v_w_in, v_b_gate, v_conv_qk, v_head_norm_g, v_sgu_ln_g, v_sgu_ln_b, v_w_s, v_b_s, v_w_branch_mlstm, v_w_branch_sgu, v_w_out, v_norm2_g, v_w_up, v_w_ffn_conv, v_w_down, v_final_g):
    params = dict(c_ctx=c_ctx, w_mod=w_mod, b_mod=b_mod, norm1_g=norm1_g, w_in=w_in, b_gate=b_gate, conv_qk=conv_qk,
                  head_norm_g=head_norm_g, sgu_ln_g=sgu_ln_g, sgu_ln_b=sgu_ln_b, w_s=w_s, b_s=b_s,
                  w_branch_mlstm=w_branch_mlstm, w_branch_sgu=w_branch_sgu, w_out=w_out, norm2_g=norm2_g, w_up=w_up,
                  w_ffn_conv=w_ffn_conv, w_down=w_down, final_g=final_g)
    mom_m = dict(c_ctx=m_c_ctx, w_mod=m_w_mod, b_mod=m_b_mod, norm1_g=m_norm1_g, w_in=m_w_in, b_gate=m_b_gate,
                 conv_qk=m_conv_qk, head_norm_g=m_head_norm_g, sgu_ln_g=m_sgu_ln_g, sgu_ln_b=m_sgu_ln_b, w_s=m_w_s,
                 b_s=m_b_s, w_branch_mlstm=m_w_branch_mlstm, w_branch_sgu=m_w_branch_sgu, w_out=m_w_out,
                 norm2_g=m_norm2_g, w_up=m_w_up, w_ffn_conv=m_w_ffn_conv, w_down=m_w_down, final_g=m_final_g)
    mom_v = dict(c_ctx=v_c_ctx, w_mod=v_w_mod, b_mod=v_b_mod, norm1_g=v_norm1_g, w_in=v_w_in, b_gate=v_b_gate,
                 conv_qk=v_conv_qk, head_norm_g=v_head_norm_g, sgu_ln_g=v_sgu_ln_g, sgu_ln_b=v_sgu_ln_b, w_s=v_w_s,
                 b_s=v_b_s, w_branch_mlstm=v_w_branch_mlstm, w_branch_sgu=v_w_branch_sgu, w_out=v_w_out,
                 norm2_g=v_norm2_g, w_up=v_w_up, w_ffn_conv=v_w_ffn_conv, w_down=v_w_down, final_g=v_final_g)
    chip = 2 * lax.axis_index("x") + lax.axis_index("y")

    shard2d = {n: params[n].reshape(params[n].shape[-2:]) for n in BIG}
    conv_sh = conv_qk.reshape(3, -1)
    fconv_sh = w_ffn_conv.reshape(9, -1)

    dev = 2 * chip + lax.axis_index("c")

    first_names, row_names = ("w_mod", "w_in"), ("w_branch_mlstm", "w_branch_sgu", "w_out", "w_down")
    first_w = [shard2d[n].shape[1] for n in first_names]
    row_h = [shard2d[n].shape[0] for n in row_names]
    first_slab = _pad_cols(jnp.concatenate([shard2d[n].astype(BF16) for n in first_names], axis=1),
                           _round_up(sum(first_w), LANE))
    up_slab = shard2d["w_up"].astype(BF16)
    row_slab = jnp.concatenate([shard2d[n].astype(BF16) for n in row_names], axis=0)

    def own_in(slab, gathered):
        return jnp.where((jnp.arange(4) == chip)[:, None, None], slab[None], gathered)

    first_all = own_in(first_slab, gather_chips([first_slab])[0])
    wt = {}
    off = 0
    for n, wd in zip(first_names, first_w):
        wt[n] = jnp.concatenate([first_all[j, :, off:off + wd] for j in range(4)], axis=1)
        off += wd

    late_slabs = [up_slab, row_slab]

    def late_copies(in_refs, out_refs, send_sems, recv_sems):
        xx, yy, cc = _coords()
        me = 2 * xx + yy
        cps = []
        for s, slab in enumerate(late_slabs):
            rh = slab.shape[0] // 2
            for j, (px, py) in enumerate(_other_chips(xx, yy)):
                for o in range(2):
                    k = 6 * s + 2 * j + o
                    cps.append(pltpu.make_async_remote_copy(
                        src_ref=in_refs[s].at[pl.ds(cc * rh, rh), :], dst_ref=out_refs[s].at[me, pl.ds(cc * rh, rh), :],
                        send_sem=send_sems.at[k], recv_sem=recv_sems.at[k],
                        device_id=(px, py, cc if o == 0 else 1 - cc), device_id_type=MESH))
        return cps

    def late_finish(landed):
        up_all, row_all = own_in(up_slab, landed[0]), own_in(row_slab, landed[1])
        got = {"w_up": jnp.concatenate([up_all[j] for j in range(4)], axis=1)}
        o = 0
        for n, ht in zip(row_names, row_h):
            got[n] = jnp.concatenate([row_all[j, o:o + ht, :] for j in range(4)], axis=0)
            o += ht
        return got

    late_rider = Rider(late_slabs, [jax.ShapeDtypeStruct((4,) + s_.shape, s_.dtype) for s_ in late_slabs],
                       6 * len(late_slabs), late_copies)

    cvec = _pack([conv_sh, fconv_sh], F32, align=8 * LANE, width=LANE)
    call = gather_all(cvec, "gather_conv")
    cparts = [_unpack(jnp.where(dev == 2 * j, cvec, call[2 * j]), [conv_sh.shape, fconv_sh.shape]) for j in range(4)]
    conv_full = jnp.concatenate([p[0] for p in cparts], axis=1)
    fconv_full = jnp.concatenate([p[1] for p in cparts], axis=1)

    sm = dict(b_mod=b_mod, norm1_g=norm1_g, b_gate=b_gate, conv_qk=conv_full, head_norm_g=head_norm_g,
              sgu_ln_g=sgu_ln_g, sgu_ln_b=sgu_ln_b, w_s=w_s, b_s=b_s, norm2_g=norm2_g, w_ffn_conv=fconv_full,
              final_g=final_g)

    gcol_names = ("w_up", "w_in")
    gcol_w = [shard2d[n].shape[1] for n in gcol_names]
    gcol_pad = _round_up(sum(gcol_w), LANE)
    cidx = lax.axis_index("c")
    kept = {}

    def grad_hook(gbig):
        def chip_cols(j):
            return _pad_cols(jnp.concatenate([gbig[n][:, j * wd:(j + 1) * wd] for n, wd in zip(gcol_names, gcol_w)],
                                             axis=1), gcol_pad)

        def chip_rows(j):
            return jnp.concatenate([gbig[n][j * ht:(j + 1) * ht] for n, ht in zip(row_names, row_h)], axis=0)

        gss = [jnp.stack([chip_cols(j) for j in range(4)]), jnp.stack([chip_rows(j) for j in range(4)])]
        from_sib = swap_halves(gss)
        pair_bf, own_terms = [], []
        for s, (gs, fs) in enumerate(zip(gss, from_sib)):
            rh = gs.shape[1] // 2
            my_half = lax.dynamic_slice_in_dim(gs, cidx * rh, rh, axis=1)
            pair_bf.append(add_n([my_half, fs], BF16, "pair_sum_%d" % s))
            own_terms.append([lax.dynamic_index_in_dim(my_half, chip, axis=0, keepdims=False),
                              lax.dynamic_index_in_dim(fs, chip, axis=0, keepdims=False)])
        kept["own_terms"] = own_terms

        def scatter_copies(in_refs, out_refs, send_sems, recv_sems):
            xx, yy, cc = _coords()
            cps = []
            for s in range(len(pair_bf)):
                for j, (px, py) in enumerate(_other_chips(xx, yy)):
                    cps.append(pltpu.make_async_remote_copy(
                        src_ref=in_refs[s].at[2 * px + py], dst_ref=out_refs[s].at[j],
                        send_sem=send_sems.at[3 * s + j], recv_sem=recv_sems.at[3 * s + j],
                        device_id=(px, py, cc), device_id_type=MESH))
            return cps

        return Rider(pair_bf, [jax.ShapeDtypeStruct((3,) + p.shape[1:], p.dtype) for p in pair_bf],
                     3 * len(pair_bf), scatter_copies)

    loss_l, grad_x, _, gsmall, d_modx, d_modc, recv = local_step(
        x[0], c, ctx[0], c_ctx, loss_target[0], wt, sm, late_weights=(late_rider, late_finish), grad_hook=grad_hook)

    reds = [add_n(kept["own_terms"][s] + [recv[s][0], recv[s][1], recv[s][2]], F32, "chip_sum_%d" % s)
            for s in range(2)]
    others = join_halves(reds)
    full_red = [jnp.where(cidx == 0, jnp.concatenate([m, o], axis=0), jnp.concatenate([o, m], axis=0))
                for m, o in zip(reds, others)]
    g_shard = {}
    off = 0
    for n, wd in zip(gcol_names, gcol_w):
        g_shard[n] = full_red[0][:, off:off + wd]
        off += wd
    off = 0
    for n, ht in zip(row_names, row_h):
        g_shard[n] = full_red[1][off:off + ht]
        off += ht

    small_order = ("b_mod", "norm1_g", "b_gate", "conv_qk", "head_norm_g", "sgu_ln_g", "sgu_ln_b", "w_s", "b_s", "norm2_g",
                   "w_ffn_conv", "final_g")
    vec_parts = [gsmall[n] for n in small_order] + [d_modx, d_modc, c.reshape(-1), loss_l.reshape(1)]
    vec_shapes = [a.shape for a in vec_parts]
    vec = _pack(vec_parts, F32, align=8 * LANE, width=LANE)
    allv = gather_all(vec, "gather_small")
    allv = jnp.where((jnp.arange(8) == dev)[:, None, None], vec[None], allv)
    summed = sum8(allv, "small_sum")
    s_parts = _unpack(summed, vec_shapes)
    g_small = dict(zip(small_order, s_parts[:len(small_order)]))
    dmc_sum = s_parts[len(small_order) + 1]
    loss = s_parts[-1][0]
    flat_all = allv.reshape(8, -1)
    starts = [0]
    for shp_ in vec_shapes:
        starts.append(starts[-1] + math.prod(shp_))
    i_dmx, i_c = len(small_order), len(small_order) + 2
    dmx_all = flat_all[:, starts[i_dmx]:starts[i_dmx + 1]]
    c_all = flat_all[:, starts[i_c]:starts[i_c + 1]]

    a_all = _pad_rows(jnp.concatenate([c_all, c_ctx.reshape(1, D)], axis=0), 16)
    dm_all = _pad_rows(jnp.concatenate([dmx_all, dmc_sum.reshape(1, NMOD * D)], axis=0), 16)
    ncol = NMOD * D // 4
    dm_shard = lax.dynamic_slice_in_dim(dm_all, chip * ncol, ncol, axis=1)
    g_shard["w_mod"] = mod_bwd_w(a_all, dm_shard, "mod_bwd_w")
    w_mod_t = wt["w_mod"][:, :2 * D].T
    g_cctx = mod_bwd_cctx(_pad_rows(dmc_sum[:2 * D].reshape(1, 2 * D), 16), w_mod_t, c_ctx.reshape(1, D))[0]
    g_small["c_ctx"] = g_cctx

    results = {}
    for n in BIG:
        shp = params[n].shape
        g_ = g_shard[n].reshape(shp)
        d_, m_, v_ = adamw(params[n], g_, mom_m[n], mom_v[n], "adamw_" + n)
        results[n] = (g_, d_, m_, v_)

    conv_g = lax.dynamic_slice_in_dim(g_small["conv_qk"].reshape(3, 2 * D), chip * (2 * D // 4), 2 * D // 4, axis=1)
    fconv_g = lax.dynamic_slice_in_dim(g_small["w_ffn_conv"].reshape(9, DFF), chip * (DFF // 4), DFF // 4, axis=1)
    g_small["conv_qk"] = conv_g
    g_small["w_ffn_conv"] = fconv_g
    w_list = [params[n].reshape(-1) for n in SMALL]
    g_list = [g_small[n].reshape(-1) for n in SMALL]
    m_list = [mom_m[n].reshape(-1) for n in SMALL]
    v_list = [mom_v[n].reshape(-1) for n in SMALL]
    sm_shapes = [params[n].shape for n in SMALL]
    pk = lambda lst: _pack(lst, F32, align=8 * LANE, width=LANE)
    gp = pk(g_list)
    d_s, m_s, v_s = adamw(pk(w_list), gp, pk(m_list), pk(v_list), "adamw_small")
    for n, gg, dd, mm, vv in zip(SMALL, _unpack(gp, sm_shapes), _unpack(d_s, sm_shapes), _unpack(m_s, sm_shapes),
                                 _unpack(v_s, sm_shapes)):
        results[n] = (gg, dd, mm, vv)

    order = ("c_ctx", "w_mod", "b_mod", "norm1_g", "w_in", "b_gate", "conv_qk", "head_norm_g", "sgu_ln_g", "sgu_ln_b",
             "w_s", "b_s", "w_branch_mlstm", "w_branch_sgu", "w_out", "norm2_g", "w_up", "w_ffn_conv", "w_down",
             "final_g")
    outs = [loss, grad_x[None]]
    for k in range(4):
        outs += [results[n][k] for n in order]
    return tuple(outs)
```

```python
import functools
import math

import jax
import jax.numpy as jnp
from jax import lax
from jax.experimental import pallas as pl
from jax.experimental.pallas import tpu as pltpu

F32 = jnp.float32
BF16 = jnp.bfloat16

D = 1024
NH = 4
DH = 256
LC = 256
GW = 64
SG = 4
SGD = 256
SCH = 128
DFF = 2816
NMOD = 6
NGATE = 16
NIN = 8208
EPS = 1e-6
M_INIT = -1e30
TR = 256
TR2 = 512
LANE = 128
VMEM_LIMIT = 56 * 1024 * 1024
MESH = pl.DeviceIdType.MESH

ADAM_LR = 0.001
ADAM_B1 = 0.9
ADAM_B2 = 0.999
ADAM_EPS = 1e-08
ADAM_WD = 0.01
ADAM_STEP = 10

CB_O, CB_U, CB_VG, CB_GM, CB_GG, CB_V, CB_Q, CB_K = range(8)


def _pick(n, cands):
    for c in cands:
        if n % c == 0:
            return c
    return n


def _cparams(sem):
    return pltpu.CompilerParams(dimension_semantics=sem, vmem_limit_bytes=VMEM_LIMIT)


def _sigmoid(x):
    return 1.0 / (1.0 + jnp.exp(-x))


def _silu(x):
    return x * _sigmoid(x)


def _dsilu(x):
    s = _sigmoid(x)
    return s * (1.0 + x * (1.0 - s))


_GC = math.sqrt(2.0 / math.pi)


def _gelu(x):
    return 0.5 * x * (1.0 + jnp.tanh(_GC * (x + 0.044715 * x * x * x)))


def _gelu_and_grad(x):
    x2 = x * x
    t = jnp.tanh(_GC * x * (1.0 + 0.044715 * x2))
    half = 0.5 * (1.0 + t)
    return x * half, half + 0.5 * x * (1.0 - t * t) * _GC * (1.0 + 3.0 * 0.044715 * x2)


def _dot(a, b):
    return jnp.dot(a.astype(BF16), b.astype(BF16), preferred_element_type=F32)


def _dot_nt(a, b):
    return lax.dot_general(a.astype(BF16), b.astype(BF16), (((1,), (1,)), ((), ())), preferred_element_type=F32)


def _dot_tn(a, b):
    return lax.dot_general(a.astype(BF16), b.astype(BF16), (((0,), (0,)), ((), ())), preferred_element_type=F32)


def _dot_tn_mxu(a, b):
    m = a.shape[1]
    eye = (lax.broadcasted_iota(jnp.int32, (m, m), 0) == lax.broadcasted_iota(jnp.int32, (m, m), 1)).astype(BF16)
    return _dot(_dot_nt(eye, a), b)


def _exact_dot(tri, x):
    x1 = x.astype(BF16)
    r1 = x - x1.astype(F32)
    x2 = r1.astype(BF16)
    x3 = (r1 - x2.astype(F32)).astype(BF16)
    return (jnp.dot(tri, x1, preferred_element_type=F32) + jnp.dot(tri, x2, preferred_element_type=F32)
            + jnp.dot(tri, x3, preferred_element_type=F32))


def _rb(tm, w, col=0, off=0):
    return pl.BlockSpec((tm, w), lambda i: (i + off, col))


def _bc(r, w):
    return pl.BlockSpec((r, w), lambda i: (0, 0))


class Rider:
    def __init__(self, ins, out_shapes, n_sems, copies):
        self.ins, self.out_shapes, self.n_sems, self.copies = list(ins), list(out_shapes), n_sems, copies


def mm_nn(a, b, out_dtype, name, rider=None):
    squeeze = a.ndim == 2
    if squeeze:
        a, b = a[None], b[None]
    g, m, k = a.shape
    n = b.shape[2]
    tm = _pick(m, (2048, 1280, 1024, 512, 256, 128))
    tn = _pick(n, (2048, 1408, 1024, 512, 128))
    tk = _pick(k, (2048, 1408, 1024, 512, 128))
    nk = k // tk
    grid = (g, n // tn, m // tm, nk)
    n_rin = len(rider.ins) if rider else 0
    n_rout = len(rider.out_shapes) if rider else 0

    def body(*refs):
        a_ref, b_ref = refs[0], refs[1]
        r_in = refs[2:2 + n_rin]
        o_ref = refs[2 + n_rin]
        r_out = refs[3 + n_rin:3 + n_rin + n_rout]
        scr = refs[3 + n_rin + n_rout:]
        if rider:
            ids = [pl.program_id(d) for d in range(4)]
            first = functools.reduce(jnp.logical_and, [i == 0 for i in ids])
            last = functools.reduce(jnp.logical_and, [i == e - 1 for i, e in zip(ids, grid)])
            send_sems, recv_sems = scr[-2], scr[-1]

            @pl.when(first)
            def _():
                for cp in rider.copies(r_in, r_out, send_sems, recv_sems):
                    cp.start()

        if nk == 1:
            o_ref[0] = _dot(a_ref[0], b_ref[0]).astype(o_ref.dtype)
        else:
            acc_ref = scr[0]
            kk = pl.program_id(3)

            @pl.when(kk == 0)
            def _():
                acc_ref[...] = jnp.zeros_like(acc_ref)

            acc_ref[...] += _dot(a_ref[0], b_ref[0])

            @pl.when(kk == nk - 1)
            def _():
                o_ref[0] = acc_ref[...].astype(o_ref.dtype)

        if rider:
            @pl.when(last)
            def _():
                for cp in rider.copies(r_in, r_out, send_sems, recv_sems):
                    cp.wait()

    scratch = [] if nk == 1 else [pltpu.VMEM((tm, tn), F32)]
    if rider:
        scratch += [pltpu.SemaphoreType.DMA((rider.n_sems,)), pltpu.SemaphoreType.DMA((rider.n_sems,))]
    outs = pl.pallas_call(
        body, name=name, grid=grid,
        in_specs=[pl.BlockSpec((1, tm, tk), lambda gi, j, i, kk: (gi, i, kk)),
                  pl.BlockSpec((1, tk, tn), lambda gi, j, i, kk: (gi, kk, j))] + [_ANY] * n_rin,
        out_specs=[pl.BlockSpec((1, tm, tn), lambda gi, j, i, kk: (gi, i, j))] + [_ANY] * n_rout,
        out_shape=[jax.ShapeDtypeStruct((g, m, n), out_dtype)] + (rider.out_shapes if rider else []),
        scratch_shapes=scratch,
        compiler_params=_cparams(("arbitrary",) * 4 if rider else ("parallel", "parallel", "parallel", "arbitrary")),
    )(a, b, *(rider.ins if rider else []))
    out = outs[0][0] if squeeze else outs[0]
    return (out, list(outs[1:])) if rider else out


def mm_tn(a, b, name, out_dtype=F32):
    squeeze = a.ndim == 2
    if squeeze:
        a, b = a[None], b[None]
    g, t, ka = a.shape
    n = b.shape[2]
    tka = _pick(ka, (1024, 1408, 512, 128))
    tn = _pick(n, (2048, 1408, 1024, 512, 128))
    tt = _pick(t, (2048, 1280, 1024, 512, 256, 128))
    nt = t // tt

    def body(a_ref, b_ref, o_ref, acc_ref):
        tt_i = pl.program_id(3)

        @pl.when(tt_i == 0)
        def _():
            acc_ref[...] = jnp.zeros_like(acc_ref)

        acc_ref[...] += _dot_tn(a_ref[0], b_ref[0])

        @pl.when(tt_i == nt - 1)
        def _():
            o_ref[0] = acc_ref[...].astype(o_ref.dtype)

    out = pl.pallas_call(
        body, name=name, grid=(g, ka // tka, n // tn, nt),
        in_specs=[pl.BlockSpec((1, tt, tka), lambda gi, i, j, ti: (gi, ti, i)),
                  pl.BlockSpec((1, tt, tn), lambda gi, i, j, ti: (gi, ti, j))],
        out_specs=pl.BlockSpec((1, tka, tn), lambda gi, i, j, ti: (gi, i, j)),
        out_shape=jax.ShapeDtypeStruct((g, ka, n), out_dtype),
        scratch_shapes=[pltpu.VMEM((tka, tn), F32)],
        compiler_params=_cparams(("parallel", "parallel", "parallel", "arbitrary")),
    )(a, b)
    return out[0] if squeeze else out


def mod_fwd(cc, w_mod, b_mod):
    n = w_mod.shape[1]

    def body(c_ref, w_ref, b_ref, o_ref):
        o_ref[...] = _dot(_silu(c_ref[...]), w_ref[...]) + b_ref[...]

    return pl.pallas_call(
        body, name="mod_fwd", grid=(n // D,),
        in_specs=[_bc(16, D), pl.BlockSpec((D, D), lambda j: (0, j)), pl.BlockSpec((1, D), lambda j: (0, j))],
        out_specs=pl.BlockSpec((16, D), lambda j: (0, j)),
        out_shape=jax.ShapeDtypeStruct((16, n), F32),
        compiler_params=_cparams(("parallel",)),
    )(cc, w_mod, b_mod)


def _ctx_x_specs(n_ctx_tiles):
    return [pl.BlockSpec((TR, D), lambda i: (jnp.minimum(i, n_ctx_tiles - 1), 0)),
            pl.BlockSpec((TR, D), lambda i: (jnp.maximum(i - n_ctx_tiles, 0), 0))]


def norm1_fwd(ctx, x, g, tab, n_ctx_tiles):
    tp = ctx.shape[0] + x.shape[0]

    def body(c_ref, x_ref, g_ref, tab_ref, o_ref):
        is_ctx = pl.program_id(0) < n_ctx_tiles
        x = jnp.where(is_ctx, c_ref[...], x_ref[...])
        r = lax.rsqrt(jnp.mean(x * x, axis=-1, keepdims=True) + EPS)
        nrm = x * r * g_ref[...]
        sh = jnp.where(is_ctx, tab_ref[0:1, :], tab_ref[2:3, :])
        sc = jnp.where(is_ctx, tab_ref[1:2, :], tab_ref[3:4, :])
        o_ref[...] = (nrm * (1.0 + sc) + sh).astype(BF16)

    return pl.pallas_call(
        body, name="norm1_fwd", grid=(tp // TR,),
        in_specs=_ctx_x_specs(n_ctx_tiles) + [_bc(1, D), _bc(8, D)],
        out_specs=_rb(TR, D),
        out_shape=jax.ShapeDtypeStruct((tp, D), BF16),
        compiler_params=_cparams(("parallel",)),
    )(ctx, x, g, tab)


def norm2_fwd(x, out, g, tab):
    t = x.shape[0]

    def body(x_ref, o_in_ref, g_ref, tab_ref, h1_ref, hn_ref):
        h1 = x_ref[...] + tab_ref[0:1, :] * o_in_ref[...].astype(F32)
        h1_ref[...] = h1
        r = lax.rsqrt(jnp.mean(h1 * h1, axis=-1, keepdims=True) + EPS)
        nrm = h1 * r * g_ref[...]
        hn_ref[...] = (nrm * (1.0 + tab_ref[2:3, :]) + tab_ref[1:2, :]).astype(BF16)

    return pl.pallas_call(
        body, name="norm2_fwd", grid=(t // TR2,),
        in_specs=[_rb(TR2, D), _rb(TR2, D), _bc(1, D), _bc(8, D)],
        out_specs=[_rb(TR2, D), _rb(TR2, D)],
        out_shape=[jax.ShapeDtypeStruct((t, D), F32), jax.ShapeDtypeStruct((t, D), BF16)],
        compiler_params=_cparams(("parallel",)),
    )(x, out, g, tab)


def _halo_specs(tm, w, col, n_rows, hb):
    per = tm // hb
    last = n_rows // hb - 1
    prev = pl.BlockSpec((hb, w), lambda i: (jnp.maximum(i * per - 1, 0), col))
    nxt = pl.BlockSpec((hb, w), lambda i: (jnp.minimum((i + 1) * per, last), col))
    return prev, nxt


def _shift_rows(x, prev_row, next_row):
    tm = x.shape[0]
    rid = lax.broadcasted_iota(jnp.int32, x.shape, 0)
    xm1 = jnp.where(rid == 0, prev_row, pltpu.roll(x, 1, 0))
    xp1 = jnp.where(rid == tm - 1, next_row, pltpu.roll(x, tm - 1, 0))
    return xm1, xp1


def _seq_edges(i, n_ctx_tiles, n_tiles):
    first = jnp.logical_or(i == 0, i == n_ctx_tiles)
    last = jnp.logical_or(i == n_ctx_tiles - 1, i == n_tiles - 1)
    return first, last


def qkconv_fwd(zmain, zg, conv_w, b_gate, n_ctx_tiles):
    tp = zmain.shape[0]
    nt = tp // TR
    w2 = 2 * D
    prev_s, next_s = _halo_specs(TR, w2, CB_Q // 2, tp, 16)

    def body(z_ref, zp_ref, zn_ref, w_ref, zg_ref, bg_ref, q_ref, k_ref, g_ref, zgb_ref):
        i = pl.program_id(0)
        first, last = _seq_edges(i, n_ctx_tiles, nt)
        z = z_ref[...].astype(F32)
        pr = jnp.where(first, 0.0, zp_ref[15:16, :].astype(F32))
        nx = jnp.where(last, 0.0, zn_ref[0:1, :].astype(F32))
        zm1, zp1 = _shift_rows(z, pr, nx)
        cv = w_ref[0:1, :] * zm1 + w_ref[1:2, :] * z + w_ref[2:3, :] * zp1
        a = _silu(cv)
        q_ref[...] = (a[:, :D] * (DH ** -0.5)).astype(BF16)
        k_ref[...] = a[:, D:].astype(BF16)
        zgb = zg_ref[...] + bg_ref[...]
        zgb_ref[...] = zgb
        logf = jnp.minimum(zgb, 0.0) - jnp.log(1.0 + jnp.exp(-jnp.abs(zgb)))
        rr = lax.broadcasted_iota(jnp.int32, (TR, TR), 0)
        cc = lax.broadcasted_iota(jnp.int32, (TR, TR), 1)
        same = (rr // LC) == (cc // LC)
        low = jnp.where(jnp.logical_and(same, cc <= rr), 1.0, 0.0).astype(BF16)
        upp = jnp.where(jnp.logical_and(same, cc >= rr), 1.0, 0.0).astype(BF16)
        bf = _exact_dot(low, logf)
        bb = _exact_dot(upp, logf)
        lane = lax.broadcasted_iota(jnp.int32, (TR, LANE), 1)
        g = jnp.where(jnp.logical_and(lane >= 4, lane < 8), bf,
                      jnp.where(jnp.logical_and(lane >= 12, lane < 16), bb, zgb))
        g_ref[...] = g

    return pl.pallas_call(
        body, name="qkconv_fwd", grid=(nt,),
        in_specs=[_rb(TR, w2, CB_Q // 2), prev_s, next_s, _bc(8, w2), _rb(TR, LANE), _bc(1, LANE)],
        out_specs=[_rb(TR, D), _rb(TR, D), _rb(TR, LANE), _rb(TR, LANE)],
        out_shape=[jax.ShapeDtypeStruct((tp, D), BF16), jax.ShapeDtypeStruct((tp, D), BF16),
                   jax.ShapeDtypeStruct((tp, LANE), F32), jax.ShapeDtypeStruct((tp, LANE), F32)],
        compiler_params=_cparams(("parallel",)),
    )(zmain, zmain, zmain, conv_w, zg, b_gate)


def qkconv_bwd_a(zmain, dqf, dqb, dkf, dkb, conv_w, n_ctx_tiles):
    tp = zmain.shape[0]
    nt = tp // TR
    w2 = 2 * D
    prev_s, next_s = _halo_specs(TR, w2, CB_Q // 2, tp, 16)

    def body(z_ref, zp_ref, zn_ref, w_ref, dqf_ref, dqb_ref, dkf_ref, dkb_ref, dc_ref, dw_ref):
        i = pl.program_id(0)
        first, last = _seq_edges(i, n_ctx_tiles, nt)
        z = z_ref[...].astype(F32)
        pr = jnp.where(first, 0.0, zp_ref[15:16, :].astype(F32))
        nx = jnp.where(last, 0.0, zn_ref[0:1, :].astype(F32))
        zm1, zp1 = _shift_rows(z, pr, nx)
        cv = w_ref[0:1, :] * zm1 + w_ref[1:2, :] * z + w_ref[2:3, :] * zp1
        da = jnp.concatenate(
            [(dqf_ref[...].astype(F32) + dqb_ref[...].astype(F32)) * (DH ** -0.5),
             dkf_ref[...].astype(F32) + dkb_ref[...].astype(F32)], axis=1)
        dc = da * _dsilu(cv)
        dc_ref[...] = dc.astype(BF16)

        @pl.when(i == 0)
        def _():
            dw_ref[...] = jnp.zeros_like(dw_ref)

        dw_ref[0:1, :] += jnp.sum(zm1 * dc, axis=0, keepdims=True)
        dw_ref[1:2, :] += jnp.sum(z * dc, axis=0, keepdims=True)
        dw_ref[2:3, :] += jnp.sum(zp1 * dc, axis=0, keepdims=True)

    return pl.pallas_call(
        body, name="qkconv_bwd_a", grid=(nt,),
        in_specs=[_rb(TR, w2, CB_Q // 2), prev_s, next_s, _bc(8, w2), _rb(TR, D), _rb(TR, D), _rb(TR, D), _rb(TR, D)],
        out_specs=[_rb(TR, w2), _bc(8, w2)],
        out_shape=[jax.ShapeDtypeStruct((tp, w2), BF16), jax.ShapeDtypeStruct((8, w2), F32)],
        compiler_params=_cparams(("arbitrary",)),
    )(zmain, zmain, zmain, conv_w, dqf, dqb, dkf, dkb)


def qkconv_bwd_b(dz, dc, conv_w, n_ctx_tiles):
    tp = dc.shape[0]
    nt = tp // TR
    w2 = 2 * D
    prev_s, next_s = _halo_specs(TR, w2, 0, tp, 16)

    def body(dz_in_ref, d_ref, dp_ref, dn_ref, w_ref, o_ref):
        del dz_in_ref
        i = pl.program_id(0)
        first, last = _seq_edges(i, n_ctx_tiles, nt)
        d = d_ref[...].astype(F32)
        pr = jnp.where(first, 0.0, dp_ref[15:16, :].astype(F32))
        nx = jnp.where(last, 0.0, dn_ref[0:1, :].astype(F32))
        dm1, dp1 = _shift_rows(d, pr, nx)
        o_ref[...] = (w_ref[0:1, :] * dp1 + w_ref[1:2, :] * d + w_ref[2:3, :] * dm1).astype(BF16)

    return pl.pallas_call(
        body, name="qkconv_bwd_b", grid=(nt,),
        in_specs=[pl.BlockSpec(memory_space=pl.ANY), _rb(TR, w2), prev_s, next_s, _bc(8, w2)],
        out_specs=_rb(TR, w2, CB_Q // 2),
        out_shape=jax.ShapeDtypeStruct(dz.shape, BF16),
        input_output_aliases={0: 0},
        compiler_params=_cparams(("parallel",)),
    )(dz, dc, dc, dc, conv_w)


def add_into_dz(dz, a, b, col):
    tp = a.shape[0]

    def body(dz_in_ref, a_ref, b_ref, o_ref):
        del dz_in_ref
        o_ref[...] = (a_ref[...].astype(F32) + b_ref[...].astype(F32)).astype(BF16)

    return pl.pallas_call(
        body, name="add_into_dz", grid=(tp // TR,),
        in_specs=[pl.BlockSpec(memory_space=pl.ANY), _rb(TR, D), _rb(TR, D)],
        out_specs=_rb(TR, D, col),
        out_shape=jax.ShapeDtypeStruct(dz.shape, BF16),
        input_output_aliases={0: 0},
        compiler_params=_cparams(("parallel",)),
    )(dz, a, b)


def _chunk_maps(nc, ncc):
    def fwd(t):
        return t

    def bwd(t):
        return jnp.where(t < ncc, ncc - 1 - t, nc - 1 + ncc - t)

    return fwd, bwd


def _split2(x):
    hi = x.astype(BF16)
    return hi, (x - hi.astype(F32)).astype(BF16)


def _mlstm_chunk(d, h, gc, gr, q_ref, k_ref, v_ref, cp, npv, m_prev, mask, precise=False):
    ic, bcol = 8 * d + h, 8 * d + 4 + h
    i_col, b_col = gc[:, ic:ic + 1], gc[:, bcol:bcol + 1]
    i_row, b_row = gr[ic:ic + 1, :], gr[bcol:bcol + 1, :]
    g = b_row[:, LC - 1:LC] if d == 0 else b_row[:, 0:1]
    a_row = g - b_row + i_row
    m_loc = jnp.max(a_row, axis=1, keepdims=True)
    dmat = jnp.where(mask, b_col - b_row + i_row, -jnp.inf)
    inter = b_col + m_prev
    m_row = jnp.maximum(inter, jnp.max(dmat, axis=1, keepdims=True))
    e = jnp.exp(dmat - m_row)
    w = jnp.exp(inter - m_row)
    hs = slice(h * DH, (h + 1) * DH)
    qh, kh, vh = q_ref[:, hs], k_ref[:, hs], v_ref[:, hs]
    p = _dot_nt(qh, kh)
    s = p * e
    cpb = cp.astype(BF16)
    qc = _dot(qh, cpb)
    if precise:
        s_hi, s_lo = _split2(s)
        num = _dot(s_hi, vh) + _dot(s_lo, vh) + w * qc
    else:
        num = _dot(s, vh) + w * qc
    qn = jnp.sum(qh.astype(F32) * npv, axis=1, keepdims=True)
    den = jnp.sum(s, axis=1, keepdims=True) + w * qn
    thr = jnp.exp(-m_row)
    m_new = jnp.maximum(g + m_prev, m_loc)
    a_old = jnp.exp(g + m_prev - m_new)
    a_col = g - b_col + i_col
    return dict(qh=qh, kh=kh, vh=vh, e=e, w=w, s=s, cpb=cpb, qc=qc, num=num, qn=qn, den=den, thr=thr,
                m_loc=m_loc, m_new=m_new, a_old=a_old, a_col=a_col, hs=hs)


def mlstm_fwd(qa, ka, zmain, gcol, grow, ncc):
    tp = qa.shape[0]
    nc = tp // LC
    cf, cb = _chunk_maps(nc, ncc)

    def body(qf, kf, vf, gcf, grf, qb, kb, vb, gcb, grb,
             hf_o, hb_o, cf_o, cb_o, nf_o, nb_o, mf_o, mb_o, c_sc, n_sc, m_sc):
        t = pl.program_id(0)

        @pl.when(t == 0)
        def _():
            c_sc[...] = jnp.zeros_like(c_sc)
            n_sc[...] = jnp.zeros_like(n_sc)
            m_sc[...] = jnp.full(m_sc.shape, M_INIT, F32)

        row = lax.broadcasted_iota(jnp.int32, (LC, LC), 0)
        col = lax.broadcasted_iota(jnp.int32, (LC, LC), 1)
        dirs = ((qf, kf, vf, gcf, grf, hf_o, cf_o, nf_o, mf_o), (qb, kb, vb, gcb, grb, hb_o, cb_o, nb_o, mb_o))
        for d, (q_ref, k_ref, v_ref, gc_ref, gr_ref, h_o, c_o, n_o, m_o) in enumerate(dirs):
            mask = (col <= row) if d == 0 else (col >= row)
            gc = gc_ref[...]
            gr = gr_ref[0]
            for h in range(NH):
                idx = d * NH + h
                cp = c_sc[idx]
                npv = n_sc[idx]
                m_full = m_sc[idx]
                m_prev = m_full[:, 0:1]
                r = _mlstm_chunk(d, h, gc, gr, q_ref, k_ref, v_ref, cp, npv, m_prev, mask)
                hs = r["hs"]
                h_o[:, hs] = (r["num"] / jnp.maximum(jnp.abs(r["den"]), r["thr"])).astype(BF16)
                c_o[0, hs, :] = r["cpb"]
                n_o[0, h:h + 1, :] = npv
                m_o[0, h:h + 1, :] = m_full
                a_new = jnp.exp(r["m_loc"] - r["m_new"])
                kw = r["kh"].astype(F32) * jnp.exp(r["a_col"] - r["m_loc"])
                kv = _dot_tn_mxu(kw, r["vh"])
                kn = jnp.sum(kw, axis=0, keepdims=True)
                c_sc[idx] = r["a_old"] * cp + a_new * kv
                n_sc[idx] = r["a_old"] * npv + a_new * kn
                m_sc[idx] = jnp.broadcast_to(r["m_new"], (1, LANE))

    def dspecs(cm):
        return [pl.BlockSpec((LC, D), lambda t: (cm(t), 0)),
                pl.BlockSpec((LC, D), lambda t: (cm(t), 0)),
                pl.BlockSpec((LC, D), lambda t: (cm(t), CB_V)),
                pl.BlockSpec((LC, LANE), lambda t: (cm(t), 0)),
                pl.BlockSpec((1, 16, LC), lambda t: (cm(t), 0, 0))]

    def ospec(cm, shp):
        return pl.BlockSpec((1,) + shp, lambda t: (cm(t), 0, 0))

    return pl.pallas_call(
        body, name="mlstm_fwd", grid=(nc,),
        in_specs=dspecs(cf) + dspecs(cb),
        out_specs=[pl.BlockSpec((LC, D), lambda t: (cf(t), 0)), pl.BlockSpec((LC, D), lambda t: (cb(t), 0)),
                   ospec(cf, (D, DH)), ospec(cb, (D, DH)), ospec(cf, (NH, DH)), ospec(cb, (NH, DH)),
                   ospec(cf, (NH, LANE)), ospec(cb, (NH, LANE))],
        out_shape=[jax.ShapeDtypeStruct((tp, D), BF16), jax.ShapeDtypeStruct((tp, D), BF16),
                   jax.ShapeDtypeStruct((nc, D, DH), BF16), jax.ShapeDtypeStruct((nc, D, DH), BF16),
                   jax.ShapeDtypeStruct((nc, NH, DH), F32), jax.ShapeDtypeStruct((nc, NH, DH), F32),
                   jax.ShapeDtypeStruct((nc, NH, LANE), F32), jax.ShapeDtypeStruct((nc, NH, LANE), F32)],
        scratch_shapes=[pltpu.VMEM((2 * NH, DH, DH), F32), pltpu.VMEM((2 * NH, 1, DH), F32),
                        pltpu.VMEM((2 * NH, 1, LANE), F32)],
        compiler_params=_cparams(("arbitrary",)),
    )(qa, ka, zmain, gcol, grow, qa, ka, zmain, gcol, grow)


def mlstm_bwd(qa, ka, zmain, gcol, grow, states, dhm, ncc):
    tp = qa.shape[0]
    nc = tp // LC
    cf0, cb0 = _chunk_maps(nc, ncc)
    cf = lambda t: cf0(nc - 1 - t)
    cb = lambda t: cb0(nc - 1 - t)
    csf, csb, nsf, nsb, msf, msb = states

    def body(qf, kf, vf, gcf, grf, cpf, npf, mpf, dhf, qb, kb, vb, gcb, grb, cpb_, npb, mpb, dhb,
             dqf_o, dkf_o, dvf_o, colf_o, rowf_o, dqb_o, dkb_o, dvb_o, colb_o, rowb_o, dc_sc, dn_sc):
        t = pl.program_id(0)

        @pl.when(t == 0)
        def _():
            dc_sc[...] = jnp.zeros_like(dc_sc)
            dn_sc[...] = jnp.zeros_like(dn_sc)

        row = lax.broadcasted_iota(jnp.int32, (LC, LC), 0)
        col = lax.broadcasted_iota(jnp.int32, (LC, LC), 1)
        dirs = ((qf, kf, vf, gcf, grf, cpf, npf, mpf, dhf, dqf_o, dkf_o, dvf_o, colf_o, rowf_o, cf),
                (qb, kb, vb, gcb, grb, cpb_, npb, mpb, dhb, dqb_o, dkb_o, dvb_o, colb_o, rowb_o, cb))
        for d, (q_ref, k_ref, v_ref, gc_ref, gr_ref, cp_ref, np_ref, mp_ref, dh_ref,
                dq_o, dk_o, dv_o, col_o, row_o, cm) in enumerate(dirs):
            mask = (col <= row) if d == 0 else (col >= row)
            live = jnp.where(cm(t) >= ncc, 1.0, 0.0).astype(F32)
            gc = gc_ref[...]
            gr = gr_ref[0]
            col_o[...] = jnp.zeros_like(col_o)
            row_o[...] = jnp.zeros_like(row_o)
            for h in range(NH):
                idx = d * NH + h
                hs = slice(h * DH, (h + 1) * DH)
                cp = cp_ref[0, hs, :]
                npv = np_ref[0, h:h + 1, :]
                m_prev = mp_ref[0, h:h + 1, 0:1]
                r = _mlstm_chunk(d, h, gc, gr, q_ref, k_ref, v_ref, cp, npv, m_prev, mask, precise=True)
                qh, kh, vh, e, w, s = r["qh"], r["kh"], r["vh"], r["e"], r["w"], r["s"]
                qf32, kf32 = qh.astype(F32), kh.astype(F32)
                den, thr = r["den"], r["thr"]
                rden = 1.0 / jnp.maximum(jnp.abs(den), thr)
                hh = r["num"] * rden
                dh = dh_ref[:, hs].astype(F32) * live
                dnum = dh * rden
                sgn = jnp.where(jnp.abs(den) > thr, jnp.sign(den), 0.0)
                dden = -jnp.sum(dh * hh, axis=1, keepdims=True) * rden * sgn
                dn_hi, dn_lo = _split2(dnum)
                ds = _dot_nt(dn_hi, vh) + _dot_nt(dn_lo, vh) + dden
                dp = ds * e
                gm = ds * s
                rowsum = jnp.sum(gm, axis=1, keepdims=True)
                colsum = jnp.sum(gm, axis=0, keepdims=True)
                dq = _dot(dp, kh) + w * (_dot_nt(dnum, r["cpb"]) + dden * npv)
                dcs = dc_sc[idx]
                dns = dn_sc[idx]
                kfac = jnp.exp(r["a_col"] - r["m_new"])
                vdc = _dot_nt(vh, dcs)
                dk = _dot_tn(dp, qh) + kfac * (vdc + dns)
                dv = _dot_tn(s, dnum) + kfac * _dot(kh, dcs)
                beta = w * (jnp.sum(dnum * r["qc"], axis=1, keepdims=True) + dden * r["qn"])
                alpha = kfac * (jnp.sum(kf32 * vdc, axis=1, keepdims=True) + jnp.sum(kf32 * dns, axis=1, keepdims=True))
                dq_o[:, hs] = dq.astype(BF16)
                dk_o[:, hs] = dk.astype(BF16)
                dv_o[:, hs] = dv.astype(BF16)
                cpf = r["cpb"].astype(F32)
                inner = (jnp.sum(jnp.sum(dcs * cpf, axis=1, keepdims=True), axis=0, keepdims=True)
                         + jnp.sum(dns * npv, axis=1, keepdims=True))
                gam = jnp.sum(alpha, axis=0, keepdims=True) + r["a_old"] * inner
                lo = 8 * d + h
                col_o[:, lo:lo + 1] = alpha
                col_o[:, lo + 4:lo + 5] = rowsum + beta - alpha
                col_o[:, lo + 36:lo + 37] = jnp.broadcast_to(gam, (LC, 1))
                row_o[0, h:h + 1, :] = colsum
                wq = qf32 * w
                dc_sc[idx] = r["a_old"] * dcs + _dot_tn(wq, dnum)
                dn_sc[idx] = r["a_old"] * dns + jnp.sum(wq * dden, axis=0, keepdims=True)

    def dspecs(cm):
        return [pl.BlockSpec((LC, D), lambda t: (cm(t), 0)),
                pl.BlockSpec((LC, D), lambda t: (cm(t), 0)),
                pl.BlockSpec((LC, D), lambda t: (cm(t), CB_V)),
                pl.BlockSpec((LC, LANE), lambda t: (cm(t), 0)),
                pl.BlockSpec((1, 16, LC), lambda t: (cm(t), 0, 0)),
                pl.BlockSpec((1, D, DH), lambda t: (cm(t), 0, 0)),
                pl.BlockSpec((1, NH, DH), lambda t: (cm(t), 0, 0)),
                pl.BlockSpec((1, NH, LANE), lambda t: (cm(t), 0, 0)),
                pl.BlockSpec((LC, D), lambda t: (jnp.maximum(cm(t) - ncc, 0), 0))]

    def ospecs(cm):
        return [pl.BlockSpec((LC, D), lambda t: (cm(t), 0)),
                pl.BlockSpec((LC, D), lambda t: (cm(t), 0)),
                pl.BlockSpec((LC, D), lambda t: (cm(t), 0)),
                pl.BlockSpec((LC, LANE), lambda t: (cm(t), 0)),
                pl.BlockSpec((1, 8, LC), lambda t: (cm(t), 0, 0))]

    oshape = [jax.ShapeDtypeStruct((tp, D), BF16)] * 3 + [jax.ShapeDtypeStruct((tp, LANE), F32),
                                                        jax.ShapeDtypeStruct((nc, 8, LC), F32)]
    return pl.pallas_call(
        body, name="mlstm_bwd", grid=(nc,),
        in_specs=dspecs(cf) + dspecs(cb),
        out_specs=ospecs(cf) + ospecs(cb),
        out_shape=oshape + oshape,
        scratch_shapes=[pltpu.VMEM((2 * NH, DH, DH), F32), pltpu.VMEM((2 * NH, 1, DH), F32)],
        compiler_params=_cparams(("arbitrary",)),
    )(qa, ka, zmain, gcol, grow, csf, nsf, msf, dhm, qa, ka, zmain, gcol, grow, csb, nsb, msb, dhm)


def gates_bwd(colf, colb, cs, zgb):
    tp = colf.shape[0]

    def body(cf_ref, cb_ref, cs_ref, zgb_ref, o_ref, db_ref):
        i = pl.program_id(0)

        @pl.when(i == 0)
        def _():
            db_ref[...] = jnp.zeros_like(db_ref)

        lane = lax.broadcasted_iota(jnp.int32, (TR, LANE), 1)
        i_l = jnp.logical_or(lane < 4, jnp.logical_and(lane >= 8, lane < 12))
        f_l = jnp.logical_or(jnp.logical_and(lane >= 4, lane < 8), jnp.logical_and(lane >= 12, lane < 16))
        cv = cf_ref[...] + cb_ref[...]
        csv = cs_ref[...]
        gam = pltpu.roll(cv, LANE - 32, 1)
        dbh = jnp.where(f_l, cv - csv, 0.0)
        rr = lax.broadcasted_iota(jnp.int32, (TR, TR), 0)
        cc = lax.broadcasted_iota(jnp.int32, (TR, TR), 1)
        same = (rr // LC) == (cc // LC)
        low = jnp.where(jnp.logical_and(same, cc <= rr), 1.0, 0.0).astype(BF16)
        upp = jnp.where(jnp.logical_and(same, cc >= rr), 1.0, 0.0).astype(BF16)
        dlogf = jnp.where(lane < 8, _exact_dot(upp, dbh), _exact_dot(low, dbh)) + gam
        out = jnp.where(i_l, csv + cv, 0.0) + jnp.where(f_l, dlogf * _sigmoid(-zgb_ref[...]), 0.0)
        o_ref[...] = out
        db_ref[...] += jnp.sum(out, axis=0, keepdims=True)

    spec = _rb(TR, LANE)
    return pl.pallas_call(
        body, name="gates_bwd", grid=(tp // TR,),
        in_specs=[spec] * 4,
        out_specs=[spec, _bc(1, LANE)],
        out_shape=[jax.ShapeDtypeStruct((tp, LANE), F32), jax.ShapeDtypeStruct((1, LANE), F32)],
        compiler_params=_cparams(("arbitrary",)),
    )(colf, colb, cs, zgb)


def _head_norm(hm, gh):
    xs, rs = [], []
    for h in range(NH):
        seg = hm[:, h * DH:(h + 1) * DH]
        r = lax.rsqrt(jnp.mean(seg * seg, axis=-1, keepdims=True) + EPS)
        xs.append(seg * r)
        rs.append(r)
    xh = jnp.concatenate(xs, axis=1)
    return xh, rs, xh * gh


def _sgu_norm(vg, ln_g, ln_b):
    mu = jnp.mean(vg, axis=-1, keepdims=True)
    vc = vg - mu
    rstd = lax.rsqrt(jnp.mean(vc * vc, axis=-1, keepdims=True) + EPS)
    vhat = vc * rstd
    return vhat, rstd, vhat * ln_g + ln_b


def _sgu_mix(vn, ws_ref, bs_ref):
    rows = []
    for c in range(TR // SCH):
        cols = []
        for g in range(SG):
            blk = vn[c * SCH:(c + 1) * SCH, g * SGD:(g + 1) * SGD]
            cols.append(_dot(ws_ref[g * SCH:(g + 1) * SCH, :], blk) + bs_ref[:, g:g + 1])
        rows.append(jnp.concatenate(cols, axis=1))
    return jnp.concatenate(rows, axis=0)


def mixer_fwd(hf, hb, zmain, gh, ln_g, ln_b, ws, bs_t, n_ctx_tiles):
    t = hf.shape[0] - n_ctx_tiles * TR
    off = n_ctx_tiles

    def body(hf_ref, hb_ref, zo_ref, zu_ref, zv_ref, gh_ref, lg_ref, lb_ref, ws_ref, bs_ref, o_ref):
        hm = hf_ref[...].astype(F32) + hb_ref[...].astype(F32)
        _, _, hn = _head_norm(hm, gh_ref[...])
        o_ref[0] = (_sigmoid(zo_ref[...].astype(F32)) * hn).astype(BF16)
        _, _, vn = _sgu_norm(_gelu(zv_ref[...].astype(F32)), lg_ref[...], lb_ref[...])
        mixed = _sgu_mix(vn, ws_ref, bs_ref)
        o_ref[1] = (_gelu(zu_ref[...].astype(F32)) * mixed).astype(BF16)

    return pl.pallas_call(
        body, name="mixer_fwd", grid=(t // TR,),
        in_specs=[_rb(TR, D, 0, off), _rb(TR, D, 0, off), _rb(TR, D, CB_O, off), _rb(TR, D, CB_U, off),
                  _rb(TR, D, CB_VG, off), _bc(1, D), _bc(1, D), _bc(1, D), _bc(SG * SCH, SCH), _bc(SCH, LANE)],
        out_specs=pl.BlockSpec((2, TR, D), lambda i: (0, i, 0)),
        out_shape=jax.ShapeDtypeStruct((2, t, D), BF16),
        compiler_params=_cparams(("parallel",)),
    )(hf, hb, zmain, zmain, zmain, gh, ln_g, ln_b, ws, bs_t)


def merge_fwd(zmain, pp, n_ctx_tiles):
    t = pp.shape[1]
    off = n_ctx_tiles

    def body(zgm_ref, zgg_ref, pp_ref, o_ref):
        y = (_sigmoid(zgm_ref[...].astype(F32)) * pp_ref[0].astype(F32)
             + _sigmoid(zgg_ref[...].astype(F32)) * pp_ref[1].astype(F32))
        o_ref[...] = y.astype(BF16)

    return pl.pallas_call(
        body, name="merge_fwd", grid=(t // TR,),
        in_specs=[_rb(TR, D, CB_GM, off), _rb(TR, D, CB_GG, off), pl.BlockSpec((2, TR, D), lambda i: (0, i, 0))],
        out_specs=_rb(TR, D),
        out_shape=jax.ShapeDtypeStruct((t, D), BF16),
        compiler_params=_cparams(("parallel",)),
    )(zmain, zmain, pp)


def merge_bwd(zmain, pp, dy, tp, n_ctx_tiles):
    t = dy.shape[0]
    nt = tp // TR
    xrow = lambda i: jnp.maximum(i - n_ctx_tiles, 0)

    def body(zg_ref, pp_ref, dy_ref, dpp_ref, dz_ref):
        i = pl.program_id(1)
        zg = zg_ref[...].astype(F32)
        sg = _sigmoid(zg)
        dyv = dy_ref[...].astype(F32)
        dpp_ref[0] = (dyv * sg).astype(BF16)
        dzv = dyv * pp_ref[0].astype(F32) * sg * (1.0 - sg)
        dz_ref[...] = jnp.where(i >= n_ctx_tiles, dzv, 0.0).astype(BF16)

    return pl.pallas_call(
        body, name="merge_bwd", grid=(2, nt),
        in_specs=[pl.BlockSpec((TR, D), lambda j, i: (i, CB_GM + j)),
                  pl.BlockSpec((1, TR, D), lambda j, i: (j, xrow(i), 0)),
                  pl.BlockSpec((TR, D), lambda j, i: (xrow(i), 0))],
        out_specs=[pl.BlockSpec((1, TR, D), lambda j, i: (j, xrow(i), 0)),
                   pl.BlockSpec((TR, D), lambda j, i: (i, CB_GM + j))],
        out_shape=[jax.ShapeDtypeStruct((2, t, D), BF16), jax.ShapeDtypeStruct((tp, 8 * D), BF16)],
        compiler_params=_cparams(("arbitrary", "arbitrary")),
    )(zmain, pp, dy)


def mixer_bwd(dz, hf, hb, zmain, dyms, gh, ln_g, ln_b, ws, bs_t, n_ctx_tiles):
    tp = hf.shape[0]
    t = tp - n_ctx_tiles * TR
    nt = tp // TR
    xrow = lambda i: jnp.maximum(i - n_ctx_tiles, 0)

    def body(dz_in_ref, hf_ref, hb_ref, zo_ref, zu_ref, zv_ref, dy_ref, gh_ref, lg_ref, lb_ref, ws_ref, bs_ref,
             dz_ref, dhm_ref, dgh_ref, dlg_ref, dlb_ref, dws_ref, dbs_ref):
        del dz_in_ref
        i = pl.program_id(0)

        @pl.when(i == 0)
        def _():
            for ref in (dgh_ref, dlg_ref, dlb_ref, dws_ref, dbs_ref):
                ref[...] = jnp.zeros_like(ref)

        @pl.when(i < n_ctx_tiles)
        def _():
            dz_ref[...] = jnp.zeros_like(dz_ref)

        @pl.when(i >= n_ctx_tiles)
        def _():
            gh_v = gh_ref[...]
            hm = hf_ref[...].astype(F32) + hb_ref[...].astype(F32)
            xh, rs, hn = _head_norm(hm, gh_v)
            zo = zo_ref[...].astype(F32)
            so = _sigmoid(zo)
            dym = dy_ref[0].astype(F32)
            d_zo = dym * hn * so * (1.0 - so)
            d_hn = dym * so
            dgh_ref[...] += jnp.sum(d_hn * xh, axis=0, keepdims=True)
            d_xh = d_hn * gh_v
            segs = []
            for h in range(NH):
                hs = slice(h * DH, (h + 1) * DH)
                dx, xs = d_xh[:, hs], xh[:, hs]
                segs.append(rs[h] * (dx - xs * jnp.mean(dx * xs, axis=-1, keepdims=True)))
            dhm_ref[...] = jnp.concatenate(segs, axis=1).astype(BF16)
            zu = zu_ref[...].astype(F32)
            zv = zv_ref[...].astype(F32)
            lg = lg_ref[...]
            gu, dgu = _gelu_and_grad(zu)
            gv, dgv = _gelu_and_grad(zv)
            vhat, rstd, vn = _sgu_norm(gv, lg, lb_ref[...])
            mixed = _sgu_mix(vn, ws_ref, bs_ref)
            dys = dy_ref[1].astype(F32)
            d_zu = dys * mixed * dgu
            d_mixed = dys * gu
            rows = []
            for c in range(TR // SCH):
                cols = []
                for g in range(SG):
                    rsl, csl = slice(c * SCH, (c + 1) * SCH), slice(g * SGD, (g + 1) * SGD)
                    dm = d_mixed[rsl, csl]
                    cols.append(_dot_tn(ws_ref[g * SCH:(g + 1) * SCH, :], dm))
                    dws_ref[g * SCH:(g + 1) * SCH, :] += _dot_nt(dm, vn[rsl, csl])
                    dbs_ref[:, g:g + 1] += jnp.sum(dm, axis=1, keepdims=True)
                rows.append(jnp.concatenate(cols, axis=1))
            d_vn = jnp.concatenate(rows, axis=0)
            dlg_ref[...] += jnp.sum(d_vn * vhat, axis=0, keepdims=True)
            dlb_ref[...] += jnp.sum(d_vn, axis=0, keepdims=True)
            d_vhat = d_vn * lg
            d_vg = rstd * (d_vhat - jnp.mean(d_vhat, axis=-1, keepdims=True)
                           - vhat * jnp.mean(d_vhat * vhat, axis=-1, keepdims=True))
            d_zv = d_vg * dgv
            dz_ref[...] = jnp.concatenate([d_zo, d_zu, d_zv], axis=1).astype(BF16)

    return pl.pallas_call(
        body, name="mixer_bwd", grid=(nt,),
        in_specs=[pl.BlockSpec(memory_space=pl.ANY), _rb(TR, D), _rb(TR, D), _rb(TR, D, CB_O), _rb(TR, D, CB_U),
                  _rb(TR, D, CB_VG), pl.BlockSpec((2, TR, D), lambda i: (0, xrow(i), 0)),
                  _bc(1, D), _bc(1, D), _bc(1, D), _bc(SG * SCH, SCH), _bc(SCH, LANE)],
        out_specs=[_rb(TR, 3 * D), pl.BlockSpec((TR, D), lambda i: (xrow(i), 0)),
                   _bc(1, D), _bc(1, D), _bc(1, D), _bc(SG * SCH, SCH), _bc(SCH, LANE)],
        out_shape=[jax.ShapeDtypeStruct(dz.shape, BF16), jax.ShapeDtypeStruct((t, D), BF16),
                   jax.ShapeDtypeStruct((1, D), F32), jax.ShapeDtypeStruct((1, D), F32),
                   jax.ShapeDtypeStruct((1, D), F32), jax.ShapeDtypeStruct((SG * SCH, SCH), F32),
                   jax.ShapeDtypeStruct((SCH, LANE), F32)],
        input_output_aliases={0: 0},
        compiler_params=_cparams(("arbitrary",)),
    )(dz, hf, hb, zmain, zmain, zmain, dyms, gh, ln_g, ln_b, ws, bs_t)


FCB = DFF // 2
TF = 512


def _ffn_halo(col, t):
    per = TF // GW
    last = t // GW - 1
    prev = pl.BlockSpec((GW, FCB), lambda i, j: (jnp.maximum(i * per - 1, 0), col(j)))
    nxt = pl.BlockSpec((GW, FCB), lambda i, j: (jnp.minimum((i + 1) * per, last), col(j)))
    return prev, nxt


def _conv_taps(ext):
    n = ext.shape[0]
    colid = lax.broadcasted_iota(jnp.int32, (n, 1), 0) % GW
    left = pltpu.roll(jnp.where(colid != GW - 1, ext, 0.0), 1, 0)
    right = pltpu.roll(jnp.where(colid != 0, ext, 0.0), n - 1, 0)
    views = (left, ext, right)
    return {(ky, kx): views[kx][GW * ky:GW * ky + TF] for ky in range(3) for kx in range(3)}


def _ext(c_ref, p_ref, n_ref, i, nt):
    pr = jnp.where(i == 0, 0.0, p_ref[...].astype(F32))
    nx = jnp.where(i == nt - 1, 0.0, n_ref[...].astype(F32))
    return jnp.concatenate([pr, c_ref[...].astype(F32), nx], axis=0)


def ffn_act_fwd(up, wc):
    t = up.shape[0]
    nt = t // TF
    prev_s, next_s = _ffn_halo(lambda j: j, t)

    def body(a_ref, ap_ref, an_ref, b_ref, w_ref, o_ref, ac_ref):
        i = pl.program_id(0)
        taps = _conv_taps(_ext(a_ref, ap_ref, an_ref, i, nt))
        ac = sum(w_ref[3 * ky + kx:3 * ky + kx + 1, :] * taps[(ky, kx)] for ky in range(3) for kx in range(3))
        ac_ref[...] = ac.astype(BF16)
        o_ref[...] = (_silu(ac) * b_ref[...].astype(F32)).astype(BF16)

    spec = pl.BlockSpec((TF, FCB), lambda i, j: (i, j))
    return pl.pallas_call(
        body, name="ffn_act_fwd", grid=(nt, 2),
        in_specs=[spec, prev_s, next_s,
                  pl.BlockSpec((TF, FCB), lambda i, j: (i, 2 + j)), pl.BlockSpec((16, FCB), lambda i, j: (0, j))],
        out_specs=[spec, spec],
        out_shape=[jax.ShapeDtypeStruct((t, DFF), BF16), jax.ShapeDtypeStruct((t, DFF), BF16)],
        compiler_params=_cparams(("parallel", "parallel")),
    )(up, up, up, up, wc)


def ffn_act_bwd(up, ac, dact):
    t = up.shape[0]
    nt = t // TF

    def body(b_ref, ac_ref, da_ref, dup_ref, dac_ref):
        acv = ac_ref[...].astype(F32)
        da = da_ref[...].astype(F32)
        s = _sigmoid(acv)
        dup_ref[...] = (da * acv * s).astype(BF16)
        dac_ref[...] = (da * b_ref[...].astype(F32) * s * (1.0 + acv * (1.0 - s))).astype(BF16)

    spec = pl.BlockSpec((TF, FCB), lambda i, j: (i, j))
    bspec = pl.BlockSpec((TF, FCB), lambda i, j: (i, 2 + j))
    return pl.pallas_call(
        body, name="ffn_act_bwd", grid=(nt, 2),
        in_specs=[bspec, spec, spec],
        out_specs=[bspec, spec],
        out_shape=[jax.ShapeDtypeStruct((t, 2 * DFF), BF16), jax.ShapeDtypeStruct((t, DFF), BF16)],
        compiler_params=_cparams(("parallel", "parallel")),
    )(up, ac, dact)


def ffn_conv_bwd(dup, up, dac, wc):
    t = up.shape[0]
    nt = t // TF
    prev_g, next_g = _ffn_halo(lambda j: j, t)

    strip = 16

    def body(dup_in_ref, a_ref, g_ref, gp_ref, gn_ref, w_ref, o_ref, dw_ref, lv_ref, cv_ref, rv_ref, part_ref):
        del dup_in_ref
        i = pl.program_id(1)

        @pl.when(i == 0)
        def _():
            dw_ref[...] = jnp.zeros_like(dw_ref)

        ext = _ext(g_ref, gp_ref, gn_ref, i, nt)
        n = ext.shape[0]
        colid = lax.broadcasted_iota(jnp.int32, (n, 1), 0) % GW
        lv_ref[...] = pltpu.roll(jnp.where(colid != GW - 1, ext, 0.0), 1, 0)
        cv_ref[...] = ext
        rv_ref[...] = pltpu.roll(jnp.where(colid != 0, ext, 0.0), n - 1, 0)
        part_ref[...] = jnp.zeros_like(part_ref)
        views = (lv_ref, cv_ref, rv_ref)

        def one_strip(r, carry):
            r0 = pl.multiple_of(r * strip, strip)
            a = a_ref[pl.ds(r0, strip), :].astype(F32)
            acc = jnp.zeros((strip, FCB), F32)
            for ky in range(3):
                for kx in range(3):
                    kf = 3 * (2 - ky) + (2 - kx)
                    tap = views[kx][pl.ds(r0 + GW * ky, strip), :]
                    acc = acc + w_ref[kf:kf + 1, :] * tap
                    p = a * tap
                    part_ref[8 * kf:8 * kf + 8, :] += p[0:8] + p[8:16]
            o_ref[pl.ds(r0, strip), :] = acc.astype(BF16)
            return carry

        lax.fori_loop(0, TF // strip, one_strip, 0)
        for k in range(9):
            dw_ref[k:k + 1, :] += jnp.sum(part_ref[8 * k:8 * k + 8, :], axis=0, keepdims=True)

    sw = lambda s: pl.BlockSpec(s.block_shape, lambda j, i, f=s.index_map: f(i, j))
    spec = pl.BlockSpec((TF, FCB), lambda j, i: (i, j))
    return pl.pallas_call(
        body, name="ffn_conv_bwd", grid=(2, nt),
        in_specs=[pl.BlockSpec(memory_space=pl.ANY), spec, spec, sw(prev_g), sw(next_g),
                  pl.BlockSpec((16, FCB), lambda j, i: (0, j))],
        out_specs=[spec, pl.BlockSpec((16, FCB), lambda j, i: (0, j))],
        out_shape=[jax.ShapeDtypeStruct(dup.shape, BF16), jax.ShapeDtypeStruct((16, DFF), F32)],
        scratch_shapes=[pltpu.VMEM((TF + 2 * GW, FCB), F32)] * 3 + [pltpu.VMEM((72, FCB), F32)],
        input_output_aliases={0: 0},
        compiler_params=_cparams(("arbitrary", "arbitrary")),
    )(dup, up, dac, dac, dac, wc)


def head_fwd_bwd(h1, f, target, gfin, tab):
    t = h1.shape[0]

    def body(h1_ref, f_ref, t_ref, g_ref, tab_ref, dh2_ref, df_ref, acc_ref):
        i = pl.program_id(0)

        @pl.when(i == 0)
        def _():
            acc_ref[...] = jnp.zeros_like(acc_ref)

        gate = tab_ref[0:1, :]
        fv = f_ref[...].astype(F32)
        h2 = h1_ref[...] + gate * fv
        r = lax.rsqrt(jnp.mean(h2 * h2, axis=-1, keepdims=True) + EPS)
        xh = h2 * r
        gv = g_ref[...]
        err = xh * gv - t_ref[...]
        acc_ref[0:1, :] += jnp.sum(0.5 * jnp.mean(err * err, axis=-1, keepdims=True), axis=0, keepdims=True)
        dy = err * (1.0 / D)
        acc_ref[1:2, :] += jnp.sum(dy * xh, axis=0, keepdims=True)
        dxh = dy * gv
        dh2 = r * (dxh - xh * jnp.mean(dxh * xh, axis=-1, keepdims=True))
        dh2_ref[...] = dh2
        acc_ref[2:3, :] += jnp.sum(dh2 * fv, axis=0, keepdims=True)
        df_ref[...] = (dh2 * gate).astype(BF16)

    return pl.pallas_call(
        body, name="head_fwd_bwd", grid=(t // TR2,),
        in_specs=[_rb(TR2, D), _rb(TR2, D), _rb(TR2, D), _bc(1, D), _bc(8, D)],
        out_specs=[_rb(TR2, D), _rb(TR2, D), _bc(8, D)],
        out_shape=[jax.ShapeDtypeStruct((t, D), F32), jax.ShapeDtypeStruct((t, D), BF16),
                   jax.ShapeDtypeStruct((8, D), F32)],
        compiler_params=_cparams(("arbitrary",)),
    )(h1, f, target, gfin, tab)


def norm2_bwd(h1, dhn2, dh2, out, g, tab):
    t = h1.shape[0]

    def body(h1_ref, dhn_ref, dh2_ref, out_ref, g_ref, tab_ref, dh1_ref, dout_ref, acc_ref):
        i = pl.program_id(0)

        @pl.when(i == 0)
        def _():
            acc_ref[...] = jnp.zeros_like(acc_ref)

        h1v = h1_ref[...]
        r = lax.rsqrt(jnp.mean(h1v * h1v, axis=-1, keepdims=True) + EPS)
        xh = h1v * r
        gv = g_ref[...]
        dhn = dhn_ref[...].astype(F32)
        acc_ref[0:1, :] += jnp.sum(dhn, axis=0, keepdims=True)
        acc_ref[1:2, :] += jnp.sum(dhn * xh * gv, axis=0, keepdims=True)
        dn = dhn * (1.0 + tab_ref[2:3, :])
        acc_ref[2:3, :] += jnp.sum(dn * xh, axis=0, keepdims=True)
        dxh = dn * gv
        dh1 = dh2_ref[...] + r * (dxh - xh * jnp.mean(dxh * xh, axis=-1, keepdims=True))
        dh1_ref[...] = dh1
        acc_ref[3:4, :] += jnp.sum(dh1 * out_ref[...].astype(F32), axis=0, keepdims=True)
        dout_ref[...] = (dh1 * tab_ref[0:1, :]).astype(BF16)

    return pl.pallas_call(
        body, name="norm2_bwd", grid=(t // TR2,),
        in_specs=[_rb(TR2, D), _rb(TR2, D), _rb(TR2, D), _rb(TR2, D), _bc(1, D), _bc(8, D)],
        out_specs=[_rb(TR2, D), _rb(TR2, D), _bc(8, D)],
        out_shape=[jax.ShapeDtypeStruct((t, D), F32), jax.ShapeDtypeStruct((t, D), BF16),
                   jax.ShapeDtypeStruct((8, D), F32)],
        compiler_params=_cparams(("arbitrary",)),
    )(h1, dhn2, dh2, out, g, tab)


def norm1_bwd(ctx, x, da, db, dh1, g, tab, n_ctx_tiles):
    t = x.shape[0]
    tp = t + ctx.shape[0]
    xrow = lambda i: jnp.maximum(i - n_ctx_tiles, 0)

    def body(c_ref, x_ref, da_ref, db_ref, dh1_ref, g_ref, tab_ref, dx_ref, acc_ref):
        i = pl.program_id(0)

        @pl.when(i == 0)
        def _():
            acc_ref[...] = jnp.zeros_like(acc_ref)

        is_ctx = i < n_ctx_tiles
        x = jnp.where(is_ctx, c_ref[...], x_ref[...])
        r = lax.rsqrt(jnp.mean(x * x, axis=-1, keepdims=True) + EPS)
        xh = x * r
        gv = g_ref[...]
        dhn = da_ref[...].astype(F32) + db_ref[...]
        s_shift = jnp.sum(dhn, axis=0, keepdims=True)
        s_scale = jnp.sum(dhn * xh * gv, axis=0, keepdims=True)

        @pl.when(is_ctx)
        def _():
            acc_ref[0:1, :] += s_shift
            acc_ref[1:2, :] += s_scale

        @pl.when(jnp.logical_not(is_ctx))
        def _():
            acc_ref[2:3, :] += s_shift
            acc_ref[3:4, :] += s_scale

        acc_ref[5:6, :] += s_shift
        acc_ref[6:7, :] += s_scale
        sc = jnp.where(is_ctx, tab_ref[1:2, :], tab_ref[3:4, :])
        dn = dhn * (1.0 + sc)
        acc_ref[4:5, :] += jnp.sum(dn * xh, axis=0, keepdims=True)
        dxh = dn * gv
        dx_ref[...] = dh1_ref[...] + r * (dxh - xh * jnp.mean(dxh * xh, axis=-1, keepdims=True))

    return pl.pallas_call(
        body, name="norm1_bwd", grid=(tp // TR,),
        in_specs=_ctx_x_specs(n_ctx_tiles) + [_rb(TR, D), _rb(TR, D), pl.BlockSpec((TR, D), lambda i: (xrow(i), 0)),
                                              _bc(1, D), _bc(8, D)],
        out_specs=[pl.BlockSpec((TR, D), lambda i: (xrow(i), 0)), _bc(8, D)],
        out_shape=[jax.ShapeDtypeStruct((t, D), F32), jax.ShapeDtypeStruct((8, D), F32)],
        compiler_params=_cparams(("arbitrary",)),
    )(ctx, x, da, db, dh1, g, tab)


def adamw(w, g, m, v, name):
    lead = w.ndim - 2
    rows, cols = w.shape[-2:]
    tm = max(t for t in range(8, rows + 1, 8) if rows % t == 0 and (t * cols <= 512 * 1024 or t == 8))
    c1 = 1.0 / (1.0 - ADAM_B1 ** ADAM_STEP)
    c2 = 1.0 / (1.0 - ADAM_B2 ** ADAM_STEP)

    def body(w_ref, g_ref, m_ref, v_ref, d_ref, mo_ref, vo_ref):
        gv = g_ref[...]
        mn = ADAM_B1 * m_ref[...] + (1.0 - ADAM_B1) * gv
        vn = ADAM_B2 * v_ref[...] + (1.0 - ADAM_B2) * (gv * gv)
        mo_ref[...] = mn
        vo_ref[...] = vn
        d_ref[...] = -ADAM_LR * ((mn * c1) / (jnp.sqrt(vn * c2) + ADAM_EPS) + ADAM_WD * w_ref[...])

    spec = pl.BlockSpec((1,) * lead + (tm, cols), lambda i: (0,) * lead + (i, 0))
    sds = jax.ShapeDtypeStruct(w.shape, F32)
    return pl.pallas_call(
        body, name=name, grid=(rows // tm,),
        in_specs=[spec] * 4, out_specs=[spec] * 3, out_shape=[sds] * 3,
        compiler_params=_cparams(("parallel",)),
    )(w, g, m, v)


def add_n(arrs, out_dtype, name):
    shp = arrs[0].shape
    cols = shp[-1]
    flat = [a.reshape(-1, cols) for a in arrs]
    rows = flat[0].shape[0]
    tm = max(t for t in range(16, rows + 1, 16) if rows % t == 0 and t * cols <= 512 * 1024)

    def body(*refs):
        acc = refs[0][...].astype(F32)
        for r in refs[1:-1]:
            acc = acc + r[...].astype(F32)
        refs[-1][...] = acc.astype(refs[-1].dtype)

    spec = pl.BlockSpec((tm, cols), lambda i: (i, 0))
    out = pl.pallas_call(
        body, name=name, grid=(rows // tm,),
        in_specs=[spec] * len(flat), out_specs=spec, out_shape=jax.ShapeDtypeStruct((rows, cols), out_dtype),
        compiler_params=_cparams(("parallel",)),
    )(*flat)
    return out.reshape(shp)


def sum8(stack, name):
    _, rows, cols = stack.shape
    tm = rows if rows <= 2048 else _pick(rows, (512, 256, 128, 64, 8))

    def body(s_ref, o_ref):
        acc = s_ref[0]
        for k in range(1, 8):
            acc = acc + s_ref[k]
        o_ref[...] = acc

    return pl.pallas_call(
        body, name=name, grid=(rows // tm,),
        in_specs=[pl.BlockSpec((8, tm, cols), lambda i: (0, i, 0))],
        out_specs=pl.BlockSpec((tm, cols), lambda i: (i, 0)),
        out_shape=jax.ShapeDtypeStruct((rows, cols), F32),
        compiler_params=_cparams(("parallel",)),
    )(stack)


def _coords():
    return lax.axis_index("x"), lax.axis_index("y"), lax.axis_index("c")


def _other_chips(x, y):
    return [(1 - x, y), (x, 1 - y), (1 - x, 1 - y)]


_ANY = pl.BlockSpec(memory_space=pl.ANY)


def gather_chips(slabs):
    ns = len(slabs)

    def body(*refs):
        x_refs, out_refs = refs[:ns], refs[ns:2 * ns]
        send_sems, recv_sems = refs[2 * ns:]
        x, y, c = _coords()
        me = 2 * x + y
        sibling = (x, y, 1 - c)
        chips = _other_chips(x, y)

        def half(s, chip, hc):
            rh = slabs[s].shape[0] // 2
            return out_refs[s].at[chip, pl.ds(hc * rh, rh), :]

        def own_half(s):
            rh = slabs[s].shape[0] // 2
            return x_refs[s].at[pl.ds(c * rh, rh), :]

        sends = []
        for s in range(ns):
            for j, (px, py) in enumerate(chips):
                cp = pltpu.make_async_remote_copy(
                    src_ref=own_half(s), dst_ref=half(s, me, c), send_sem=send_sems.at[6 * s + j],
                    recv_sem=recv_sems.at[6 * s + j], device_id=(px, py, c), device_id_type=MESH)
                cp.start()
                sends.append(cp)
        for s in range(ns):
            for j, (px, py) in enumerate(chips):
                src = 2 * px + py
                landed = pltpu.make_async_remote_copy(
                    src_ref=half(s, src, c), dst_ref=half(s, src, c), send_sem=send_sems.at[6 * s + j],
                    recv_sem=recv_sems.at[6 * s + j], device_id=(px, py, c), device_id_type=MESH)
                landed.wait_recv()
                fw = pltpu.make_async_remote_copy(
                    src_ref=half(s, src, c), dst_ref=half(s, src, c), send_sem=send_sems.at[6 * s + 3 + j],
                    recv_sem=recv_sems.at[6 * s + 3 + j], device_id=sibling, device_id_type=MESH)
                fw.start()
                sends.append(fw)
        for s in range(ns):
            for j, (px, py) in enumerate(chips):
                src = 2 * px + py
                got = pltpu.make_async_remote_copy(
                    src_ref=half(s, src, 1 - c), dst_ref=half(s, src, 1 - c), send_sem=send_sems.at[6 * s + 3 + j],
                    recv_sem=recv_sems.at[6 * s + 3 + j], device_id=sibling, device_id_type=MESH)
                got.wait_recv()
        for cp in sends:
            cp.wait_send()

    return pl.pallas_call(
        body, name="gather_chips",
        in_specs=[_ANY] * ns, out_specs=[_ANY] * ns,
        out_shape=[jax.ShapeDtypeStruct((4,) + s.shape, s.dtype) for s in slabs],
        scratch_shapes=[pltpu.SemaphoreType.DMA((6 * ns,)), pltpu.SemaphoreType.DMA((6 * ns,))],
    )(*slabs)


def swap_halves(gss):
    ns = len(gss)

    def body(*refs):
        g_refs, out_refs = refs[:ns], refs[ns:2 * ns]
        send_sems, recv_sems = refs[2 * ns:]
        x, y, c = _coords()
        cps = []
        for s in range(ns):
            rh = gss[s].shape[1] // 2
            cp = pltpu.make_async_remote_copy(
                src_ref=g_refs[s].at[:, pl.ds((1 - c) * rh, rh), :], dst_ref=out_refs[s],
                send_sem=send_sems.at[s], recv_sem=recv_sems.at[s], device_id=(x, y, 1 - c), device_id_type=MESH)
            cp.start()
            cps.append(cp)
        for cp in cps:
            cp.wait()

    return pl.pallas_call(
        body, name="swap_halves",
        in_specs=[_ANY] * ns, out_specs=[_ANY] * ns,
        out_shape=[jax.ShapeDtypeStruct((4, g.shape[1] // 2, g.shape[2]), g.dtype) for g in gss],
        scratch_shapes=[pltpu.SemaphoreType.DMA((ns,)), pltpu.SemaphoreType.DMA((ns,))],
    )(*gss)


def join_halves(reds):
    ns = len(reds)

    def body(*refs):
        r_refs, out_refs = refs[:ns], refs[ns:2 * ns]
        send_sems, recv_sems = refs[2 * ns:]
        x, y, c = _coords()
        cps = []
        for s in range(ns):
            cp = pltpu.make_async_remote_copy(
                src_ref=r_refs[s], dst_ref=out_refs[s], send_sem=send_sems.at[s], recv_sem=recv_sems.at[s],
                device_id=(x, y, 1 - c), device_id_type=MESH)
            cp.start()
            cps.append(cp)
        for cp in cps:
            cp.wait()

    return pl.pallas_call(
        body, name="join_halves",
        in_specs=[_ANY] * ns, out_specs=[_ANY] * ns,
        out_shape=[jax.ShapeDtypeStruct(r.shape, r.dtype) for r in reds],
        scratch_shapes=[pltpu.SemaphoreType.DMA((ns,)), pltpu.SemaphoreType.DMA((ns,))],
    )(*reds)


def gather_all(vec, name):
    r, wd = vec.shape

    def body(v_ref, out_ref, send_sems, recv_sems):
        x, y, c = _coords()
        me = 4 * x + 2 * y + c
        cps = []
        for k in range(1, 8):
            mx, my, mc = (k >> 2) & 1, (k >> 1) & 1, k & 1
            peer = (x ^ mx, y ^ my, c ^ mc)
            cp = pltpu.make_async_remote_copy(
                src_ref=v_ref, dst_ref=out_ref.at[me],
                send_sem=send_sems.at[k - 1], recv_sem=recv_sems.at[k - 1], device_id=peer, device_id_type=MESH)
            cp.start()
            cps.append(cp)
        for k in range(1, 8):
            mx, my, mc = (k >> 2) & 1, (k >> 1) & 1, k & 1
            peer = (x ^ mx, y ^ my, c ^ mc)
            src = 4 * peer[0] + 2 * peer[1] + peer[2]
            got = pltpu.make_async_remote_copy(
                src_ref=v_ref, dst_ref=out_ref.at[src],
                send_sem=send_sems.at[k - 1], recv_sem=recv_sems.at[k - 1], device_id=peer, device_id_type=MESH)
            got.wait_recv()
        for cp in cps:
            cp.wait_send()

    return pl.pallas_call(
        body, name=name,
        in_specs=[_ANY], out_specs=_ANY,
        out_shape=jax.ShapeDtypeStruct((8, r, wd), vec.dtype),
        scratch_shapes=[pltpu.SemaphoreType.DMA((7,)), pltpu.SemaphoreType.DMA((7,))],
    )(vec)


def _pad_rows(a, rows):
    return jnp.pad(a, ((0, rows - a.shape[0]), (0, 0)))


def _pad_cols(a, cols):
    return jnp.pad(a, ((0, 0), (0, cols - a.shape[1])))


def local_step(x, c, ctx, c_ctx, target, wt, sm, late_weights=None, grad_hook=None):
    t, tc = x.shape[0], ctx.shape[0]
    tp = t + tc
    nct = tc // TR
    ncc = tc // LC
    nc = tp // LC

    w_in = wt["w_in"]
    segs = {"q": (0, D), "k": (D, 2 * D), "v": (2 * D, 3 * D), "g": (3 * D, 3 * D + NGATE)}
    base = 3 * D + NGATE
    for n_i, nm in enumerate(("o", "u", "vg", "gm", "gg")):
        segs[nm] = (base + n_i * D, base + (n_i + 1) * D)
    order = ("o", "u", "vg", "gm", "gg", "v", "q", "k")
    w_main = jnp.concatenate([w_in[:, segs[nm][0]:segs[nm][1]] for nm in order], axis=1)
    w_g = _pad_cols(w_in[:, segs["g"][0]:segs["g"][1]], LANE)
    w_main_t = w_main.T
    w_g_t = w_g.T

    cc = _pad_rows(jnp.concatenate([c.reshape(1, D), c_ctx.reshape(1, D)], axis=0), 16)
    modv = mod_fwd(cc, wt["w_mod"], sm["b_mod"].reshape(1, NMOD * D))
    mx = modv[0].reshape(NMOD, D)
    mc = modv[1].reshape(NMOD, D)
    tab1 = _pad_rows(jnp.stack([mc[0], mc[1], mx[0], mx[1]]), 8)
    tab2 = _pad_rows(jnp.stack([mx[2], mx[3], mx[4]]), 8)
    tab3 = _pad_rows(mx[5:6], 8)

    g1 = sm["norm1_g"].reshape(1, D)
    g2 = sm["norm2_g"].reshape(1, D)
    gfin = sm["final_g"].reshape(1, D)
    gh = sm["head_norm_g"].reshape(1, D)
    ln_g = sm["sgu_ln_g"].reshape(1, D)
    ln_b = sm["sgu_ln_b"].reshape(1, D)
    ws = sm["w_s"].reshape(SG * SCH, SCH).astype(BF16)
    bs_t = _pad_cols(sm["b_s"].reshape(SG, SCH).T, LANE)
    conv_w = _pad_rows(sm["conv_qk"].reshape(3, 2 * D), 8)
    b_gate = _pad_cols(sm["b_gate"].reshape(1, NGATE), LANE)
    wc = _pad_rows(sm["w_ffn_conv"].reshape(9, DFF), 16)

    hn1 = norm1_fwd(ctx, x, g1, tab1, nct)
    if late_weights is None:
        zmain = mm_nn(hn1, w_main, BF16, "mm_zmain")
    else:
        zmain, landed = mm_nn(hn1, w_main, BF16, "mm_zmain", rider=late_weights[0])
        wt = {**wt, **late_weights[1](landed)}
    zg = mm_nn(hn1, w_g, F32, "mm_zg")
    qa, ka, gcol, zgb = qkconv_fwd(zmain, zg, conv_w, b_gate, nct)
    grow = gcol[:, :16].reshape(nc, LC, 16).transpose(0, 2, 1)
    hf, hb, csf, csb, nsf, nsb, msf, msb = mlstm_fwd(qa, ka, zmain, gcol, grow, ncc)
    yms = mixer_fwd(hf, hb, zmain, gh, ln_g, ln_b, ws, bs_t, nct)
    w_br = jnp.stack([wt["w_branch_mlstm"], wt["w_branch_sgu"]])
    pp = mm_nn(yms, w_br, BF16, "mm_branch")
    y = merge_fwd(zmain, pp, nct)
    out = mm_nn(y, wt["w_out"], BF16, "mm_out")
    h1, hn2 = norm2_fwd(x, out, g2, tab2)
    up = mm_nn(hn2, wt["w_up"], BF16, "mm_up")
    act, ac = ffn_act_fwd(up, wc)
    f = mm_nn(act, wt["w_down"], BF16, "mm_down")
    dh2, df, acc_h = head_fwd_bwd(h1, f, target, gfin, tab3)
    loss = acc_h[0, 0]

    g_w_down = mm_tn(act, df, "mmt_down", BF16)
    dact = mm_nn(df, wt["w_down"].T, BF16, "mm_ddown")
    dup, dac = ffn_act_bwd(up, ac, dact)
    dup, g_wc = ffn_conv_bwd(dup, up, dac, wc)
    g_w_up = mm_tn(hn2, dup, "mmt_up", BF16)
    dhn2 = mm_nn(dup, wt["w_up"].T, BF16, "mm_dup")
    dh1, dout, acc_2 = norm2_bwd(h1, dhn2, dh2, out, g2, tab2)
    g_w_out = mm_tn(y, dout, "mmt_out", BF16)
    dy = mm_nn(dout, wt["w_out"].T, BF16, "mm_dout")
    dpp, dz = merge_bwd(zmain, pp, dy, tp, nct)
    g_w_br = mm_tn(yms, dpp, "mmt_branch", BF16)
    dyms = mm_nn(dpp, jnp.stack([wt["w_branch_mlstm"].T, wt["w_branch_sgu"].T]), BF16, "mm_dbranch")
    dz, dhm, g_gh, g_lng, g_lnb, g_ws, g_bs = mixer_bwd(dz, hf, hb, zmain, dyms, gh, ln_g, ln_b, ws, bs_t, nct)
    (dqf, dkf, dvf, colf, rowf, dqb, dkb, dvb, colb, rowb) = mlstm_bwd(
        qa, ka, zmain, gcol, grow, (csf, csb, nsf, nsb, msf, msb), dhm, ncc)

    csum_f = rowf[:, :4, :].transpose(0, 2, 1).reshape(tp, 4)
    csum_b = rowb[:, :4, :].transpose(0, 2, 1).reshape(tp, 4)
    cs = _pad_cols(jnp.concatenate([csum_f, csum_f, csum_b, csum_b], axis=1), LANE)
    dzg, g_bgate = gates_bwd(colf, colb, cs, zgb)

    dc, g_convw = qkconv_bwd_a(zmain, dqf, dqb, dkf, dkb, conv_w, nct)
    dz = qkconv_bwd_b(dz, dc, conv_w, nct)
    dz = add_into_dz(dz, dvf, dvb, CB_V)

    g_w_main = mm_tn(hn1, dz, "mmt_main", BF16)
    g_w_g = mm_tn(hn1, dzg, "mmt_g", BF16)
    blk = lambda cb: g_w_main[:, cb * D:(cb + 1) * D]
    g_w_in = jnp.concatenate([blk(CB_Q), blk(CB_K), blk(CB_V), g_w_g[:, :NGATE], blk(CB_O), blk(CB_U), blk(CB_VG),
                              blk(CB_GM), blk(CB_GG)], axis=1)
    big = {"w_in": g_w_in, "w_branch_mlstm": g_w_br[0], "w_branch_sgu": g_w_br[1], "w_out": g_w_out,
           "w_up": g_w_up, "w_down": g_w_down}
    if grad_hook is None:
        da, received = mm_nn(dz, w_main_t, BF16, "mm_dmain"), None
    else:
        da, received = mm_nn(dz, w_main_t, BF16, "mm_dmain", rider=grad_hook(big))
    db = mm_nn(dzg, w_g_t, F32, "mm_dg")
    grad_x, acc_1 = norm1_bwd(ctx, x, da, db, dh1, g1, tab1, nct)

    d_modx = jnp.concatenate([acc_1[2], acc_1[3], acc_2[3], acc_2[0], acc_2[1], acc_h[2]])
    d_modc = jnp.concatenate([acc_1[0], acc_1[1], jnp.zeros((4 * D,), F32)])
    d_modb = jnp.concatenate([acc_1[5], acc_1[6], acc_2[3], acc_2[0], acc_2[1], acc_h[2]])

    small = {"b_mod": d_modb, "norm1_g": acc_1[4], "b_gate": g_bgate[0, :NGATE], "conv_qk": g_convw[:3].reshape(-1),
             "head_norm_g": g_gh[0], "sgu_ln_g": g_lng[0], "sgu_ln_b": g_lnb[0], "w_s": g_ws.reshape(-1),
             "b_s": g_bs[:, :SG].T.reshape(-1), "norm2_g": acc_2[2], "w_ffn_conv": g_wc[:9].reshape(-1),
             "final_g": acc_h[1]}
    return loss, grad_x, big, small, d_modx, d_modc, received


def mod_bwd_w(a_all, dm_all, name):
    n = dm_all.shape[1]
    tn = _pick(n, (512, 128))

    def body(a_ref, d_ref, o_ref):
        o_ref[...] = _dot_tn(_silu(a_ref[...]), d_ref[...])

    return pl.pallas_call(
        body, name=name, grid=(n // tn,),
        in_specs=[_bc(16, D), pl.BlockSpec((16, tn), lambda j: (0, j))],
        out_specs=pl.BlockSpec((D, tn), lambda j: (0, j)),
        out_shape=jax.ShapeDtypeStruct((D, n), F32),
        compiler_params=_cparams(("parallel",)),
    )(a_all, dm_all)


def mod_bwd_cctx(dmc, w_mod_t, c_ctx):
    def body(d_ref, w_ref, c_ref, o_ref):
        o_ref[...] = _dot(d_ref[...], w_ref[...]) * _dsilu(c_ref[...])

    return pl.pallas_call(
        body, name="mod_bwd_cctx", grid=(1,),
        in_specs=[_bc(16, 2 * D), _bc(2 * D, D), _bc(1, D)],
        out_specs=_bc(16, D),
        out_shape=jax.ShapeDtypeStruct((16, D), F32),
        compiler_params=_cparams(("arbitrary",)),
    )(dmc, w_mod_t, c_ctx)


BIG = ("w_mod", "w_in", "w_branch_mlstm", "w_branch_sgu", "w_out", "w_up", "w_down")
BIG_AXIS = {"w_mod": 1, "w_in": 1, "w_branch_mlstm": 0, "w_branch_sgu": 0, "w_out": 0, "w_up": 1, "w_down": 0}
SMALL = ("c_ctx", "b_mod", "norm1_g", "b_gate", "conv_qk", "head_norm_g", "sgu_ln_g", "sgu_ln_b", "w_s", "b_s",
         "norm2_g", "w_ffn_conv", "final_g")
SMALL_SHARDED = {"conv_qk": (3, 2 * D), "w_ffn_conv": (9, DFF)}
PACK_ALIGN = 32 * D


def _pack(arrs, dtype, align=PACK_ALIGN, width=D):
    flat = jnp.concatenate([a.reshape(-1).astype(dtype) for a in arrs])
    n = flat.shape[0]
    padded = -(-n // align) * align
    return jnp.pad(flat, (0, padded - n)).reshape(padded // width, width)


def _unpack(slab, shapes):
    flat = slab.reshape(-1)
    outs, off = [], 0
    for shp in shapes:
        n = math.prod(shp)
        outs.append(flat[off:off + n].reshape(shp))
        off += n
    return outs


def _round_up(n, m):
    return -(-n // m) * m


def kernel(x, c, ctx, c_ctx, w_mod, b_mod, norm1_g, w_in, b_gate, conv_qk, head_norm_g, sgu_ln_g, sgu_ln_b, w_s, b_s, w_branch_mlstm, w_branch_sgu, w_out, norm2_g, w_up, w_ffn_conv, w_down, final_g, loss_target, m_c_ctx, m_w_mod, m_b_mod, m_norm1_g, m_w_in, m_b_gate, m_conv_qk, m_head_norm_g, m_sgu_ln_g, m_sgu_ln_b, m_w_s, m_b_s, m_w_branch_mlstm, m_w_branch_sgu, m_w_out, m_norm2_g, m_w_up, m_w_ffn_conv, m_w_down, m_final_g, v_c_ctx, v_w_mod, v_b_mod, v_norm1_g, v_w_in, v_b_gate, v_conv_qk, v_head_norm_g, v_sgu_ln_g, v_sgu_ln_b, v_w_s, v_b_s, v_w_branch_mlstm, v_w_branch_sgu, v_w_out, v_norm2_g, v_w_up, v_w_ffn_conv, v_w_down, v_final_g):
    params = dict(c_ctx=c_ctx, w_mod=w_mod, b_mod=b_mod, norm1_g=norm1_g, w_in=w_in, b_gate=b_gate, conv_qk=conv_qk,
                  head_norm_g=head_norm_g, sgu_ln_g=sgu_ln_g, sgu_ln_b=sgu_ln_b, w_s=w_s, b_s=b_s,
                  w_branch_mlstm=w_branch_mlstm, w_branch_sgu=w_branch_sgu, w_out=w_out, norm2_g=norm2_g, w_up=w_up,
                  w_ffn_conv=w_ffn_conv, w_down=w_down, final_g=final_g)
    mom_m = dict(c_ctx=m_c_ctx, w_mod=m_w_mod, b_mod=m_b_mod, norm1_g=m_norm1_g, w_in=m_w_in, b_gate=m_b_gate,
                 conv_qk=m_conv_qk, head_norm_g=m_head_norm_g, sgu_ln_g=m_sgu_ln_g, sgu_ln_b=m_sgu_ln_b, w_s=m_w_s,
                 b_s=m_b_s, w_branch_mlstm=m_w_branch_mlstm, w_branch_sgu=m_w_branch_sgu, w_out=m_w_out,
                 norm2_g=m_norm2_g, w_up=m_w_up, w_ffn_conv=m_w_ffn_conv, w_down=m_w_down, final_g=m_final_g)
    mom_v = dict(c_ctx=v_c_ctx, w_mod=v_w_mod, b_mod=v_b_mod, norm1_g=v_norm1_g, w_in=v_w_in, b_gate=v_b_gate,
                 conv_qk=v_conv_qk, head_norm_g=v_head_norm_g, sgu_ln_g=v_sgu_ln_g, sgu_ln_b=v_sgu_ln_b, w_s=v_w_s,
                 b_s=v_b_s, w_branch_mlstm=v_w_branch_mlstm, w_branch_sgu=v_w_branch_sgu, w_out=v_w_out,
                 norm2_g=v_norm2_g, w_up=v_w_up, w_ffn_conv=v_w_ffn_conv, w_down=v_w_down, final_g=v_final_g)
    chip = 2 * lax.axis_index("x") + lax.axis_index("y")

    shard2d = {n: params[n].reshape(params[n].shape[-2:]) for n in BIG}
    conv_sh = conv_qk.reshape(3, -1)
    fconv_sh = w_ffn_conv.reshape(9, -1)

    dev = 2 * chip + lax.axis_index("c")

    first_names, row_names = ("w_mod", "w_in"), ("w_branch_mlstm", "w_branch_sgu", "w_out", "w_down")
    first_w = [shard2d[n].shape[1] for n in first_names]
    row_h = [shard2d[n].shape[0] for n in row_names]
    first_slab = _pad_cols(jnp.concatenate([shard2d[n].astype(BF16) for n in first_names], axis=1),
                           _round_up(sum(first_w), LANE))
    up_slab = shard2d["w_up"].astype(BF16)
    row_slab = jnp.concatenate([shard2d[n].astype(BF16) for n in row_names], axis=0)

    def own_in(slab, gathered):
        return jnp.where((jnp.arange(4) == chip)[:, None, None], slab[None], gathered)

    first_all = own_in(first_slab, gather_chips([first_slab])[0])
    wt = {}
    off = 0
    for n, wd in zip(first_names, first_w):
        wt[n] = jnp.concatenate([first_all[j, :, off:off + wd] for j in range(4)], axis=1)
        off += wd

    late_slabs = [up_slab, row_slab]

    def late_copies(in_refs, out_refs, send_sems, recv_sems):
        xx, yy, cc = _coords()
        me = 2 * xx + yy
        cps = []
        for s, slab in enumerate(late_slabs):
            rh = slab.shape[0] // 2
            for j, (px, py) in enumerate(_other_chips(xx, yy)):
                for o in range(2):
                    k = 6 * s + 2 * j + o
                    cps.append(pltpu.make_async_remote_copy(
                        src_ref=in_refs[s].at[pl.ds(cc * rh, rh), :], dst_ref=out_refs[s].at[me, pl.ds(cc * rh, rh), :],
                        send_sem=send_sems.at[k], recv_sem=recv_sems.at[k],
                        device_id=(px, py, cc if o == 0 else 1 - cc), device_id_type=MESH))
        return cps

    def late_finish(landed):
        up_all, row_all = own_in(up_slab, landed[0]), own_in(row_slab, landed[1])
        got = {"w_up": jnp.concatenate([up_all[j] for j in range(4)], axis=1)}
        o = 0
        for n, ht in zip(row_names, row_h):
            got[n] = jnp.concatenate([row_all[j, o:o + ht, :] for j in range(4)], axis=0)
            o += ht
        return got

    late_rider = Rider(late_slabs, [jax.ShapeDtypeStruct((4,) + s_.shape, s_.dtype) for s_ in late_slabs],
                       6 * len(late_slabs), late_copies)

    cvec = _pack([conv_sh, fconv_sh], F32, align=8 * LANE, width=LANE)
    call = gather_all(cvec, "gather_conv")
    cparts = [_unpack(jnp.where(dev == 2 * j, cvec, call[2 * j]), [conv_sh.shape, fconv_sh.shape]) for j in range(4)]
    conv_full = jnp.concatenate([p[0] for p in cparts], axis=1)
    fconv_full = jnp.concatenate([p[1] for p in cparts], axis=1)

    sm = dict(b_mod=b_mod, norm1_g=norm1_g, b_gate=b_gate, conv_qk=conv_full, head_norm_g=head_norm_g,
              sgu_ln_g=sgu_ln_g, sgu_ln_b=sgu_ln_b, w_s=w_s, b_s=b_s, norm2_g=norm2_g, w_ffn_conv=fconv_full,
              final_g=final_g)

    gcol_names = ("w_up", "w_in")
    gcol_w = [shard2d[n].shape[1] for n in gcol_names]
    gcol_pad = _round_up(sum(gcol_w), LANE)
    cidx = lax.axis_index("c")
    kept = {}

    def grad_hook(gbig):
        def chip_cols(j):
            return _pad_cols(jnp.concatenate([gbig[n][:, j * wd:(j + 1) * wd] for n, wd in zip(gcol_names, gcol_w)],
                                             axis=1), gcol_pad)

        def chip_rows(j):
            return jnp.concatenate([gbig[n][j * ht:(j + 1) * ht] for n, ht in zip(row_names, row_h)], axis=0)

        gss = [jnp.stack([chip_cols(j) for j in range(4)]), jnp.stack([chip_rows(j) for j in range(4)])]
        from_sib = swap_halves(gss)
        pair_bf, own_terms = [], []
        for s, (gs, fs) in enumerate(zip(gss, from_sib)):
            rh = gs.shape[1] // 2
            my_half = lax.dynamic_slice_in_dim(gs, cidx * rh, rh, axis=1)
            pair_bf.append(add_n([my_half, fs], BF16, "pair_sum_%d" % s))
            own_terms.append([lax.dynamic_index_in_dim(my_half, chip, axis=0, keepdims=False),
                              lax.dynamic_index_in_dim(fs, chip, axis=0, keepdims=False)])
        kept["own_terms"] = own_terms

        def scatter_copies(in_refs, out_refs, send_sems, recv_sems):
            xx, yy, cc = _coords()
            cps = []
            for s in range(len(pair_bf)):
                for j, (px, py) in enumerate(_other_chips(xx, yy)):
                    cps.append(pltpu.make_async_remote_copy(
                        src_ref=in_refs[s].at[2 * px + py], dst_ref=out_refs[s].at[j],
                        send_sem=send_sems.at[3 * s + j], recv_sem=recv_sems.at[3 * s + j],
                        device_id=(px, py, cc), device_id_type=MESH))
            return cps

        return Rider(pair_bf, [jax.ShapeDtypeStruct((3,) + p.shape[1:], p.dtype) for p in pair_bf],
                     3 * len(pair_bf), scatter_copies)

    loss_l, grad_x, _, gsmall, d_modx, d_modc, recv = local_step(
        x[0], c, ctx[0], c_ctx, loss_target[0], wt, sm, late_weights=(late_rider, late_finish), grad_hook=grad_hook)

    reds = [add_n(kept["own_terms"][s] + [recv[s][0], recv[s][1], recv[s][2]], F32, "chip_sum_%d" % s)
            for s in range(2)]
    others = join_halves(reds)
    full_red = [jnp.where(cidx == 0, jnp.concatenate([m, o], axis=0), jnp.concatenate([o, m], axis=0))
                for m, o in zip(reds, others)]
    g_shard = {}
    off = 0
    for n, wd in zip(gcol_names, gcol_w):
        g_shard[n] = full_red[0][:, off:off + wd]
        off += wd
    off = 0
    for n, ht in zip(row_names, row_h):
        g_shard[n] = full_red[1][off:off + ht]
        off += ht

    small_order = ("b_mod", "norm1_g", "b_gate", "conv_qk", "head_norm_g", "sgu_ln_g", "sgu_ln_b", "w_s", "b_s", "norm2_g",
                   "w_ffn_conv", "final_g")
    vec_parts = [gsmall[n] for n in small_order] + [d_modx, d_modc, c.reshape(-1), loss_l.reshape(1)]
    vec_shapes = [a.shape for a in vec_parts]
    vec = _pack(vec_parts, F32, align=8 * LANE, width=LANE)
    allv = gather_all(vec, "gather_small")
    allv = jnp.where((jnp.arange(8) == dev)[:, None, None], vec[None], allv)
    summed = sum8(allv, "small_sum")
    s_parts = _unpack(summed, vec_shapes)
    g_small = dict(zip(small_order, s_parts[:len(small_order)]))
    dmc_sum = s_parts[len(small_order) + 1]
    loss = s_parts[-1][0]
    flat_all = allv.reshape(8, -1)
    starts = [0]
    for shp_ in vec_shapes:
        starts.append(starts[-1] + math.prod(shp_))
    i_dmx, i_c = len(small_order), len(small_order) + 2
    dmx_all = flat_all[:, starts[i_dmx]:starts[i_dmx + 1]]
    c_all = flat_all[:, starts[i_c]:starts[i_c + 1]]

    a_all = _pad_rows(jnp.concatenate([c_all, c_ctx.reshape(1, D)], axis=0), 16)
    dm_all = _pad_rows(jnp.concatenate([dmx_all, dmc_sum.reshape(1, NMOD * D)], axis=0), 16)
    ncol = NMOD * D // 4
    dm_shard = lax.dynamic_slice_in_dim(dm_all, chip * ncol, ncol, axis=1)
    g_shard["w_mod"] = mod_bwd_w(a_all, dm_shard, "mod_bwd_w")
    w_mod_t = wt["w_mod"][:, :2 * D].T
    g_cctx = mod_bwd_cctx(_pad_rows(dmc_sum[:2 * D].reshape(1, 2 * D), 16), w_mod_t, c_ctx.reshape(1, D))[0]
    g_small["c_ctx"] = g_cctx

    results = {}
    for n in BIG:
        shp = params[n].shape
        g_ = g_shard[n].reshape(shp)
        d_, m_, v_ = adamw(params[n], g_, mom_m[n], mom_v[n], "adamw_" + n)
        results[n] = (g_, d_, m_, v_)

    conv_g = lax.dynamic_slice_in_dim(g_small["conv_qk"].reshape(3, 2 * D), chip * (2 * D // 4), 2 * D // 4, axis=1)
    fconv_g = lax.dynamic_slice_in_dim(g_small["w_ffn_conv"].reshape(9, DFF), chip * (DFF // 4), DFF // 4, axis=1)
    g_small["conv_qk"] = conv_g
    g_small["w_ffn_conv"] = fconv_g
    w_list = [params[n].reshape(-1) for n in SMALL]
    g_list = [g_small[n].reshape(-1) for n in SMALL]
    m_list = [mom_m[n].reshape(-1) for n in SMALL]
    v_list = [mom_v[n].reshape(-1) for n in SMALL]
    sm_shapes = [params[n].shape for n in SMALL]
    pk = lambda lst: _pack(lst, F32, align=8 * LANE, width=LANE)
    gp = pk(g_list)
    d_s, m_s, v_s = adamw(pk(w_list), gp, pk(m_list), pk(v_list), "adamw_small")
    for n, gg, dd, mm, vv in zip(SMALL, _unpack(gp, sm_shapes), _unpack(d_s, sm_shapes), _unpack(m_s, sm_shapes),
                                 _unpack(v_s, sm_shapes)):
        results[n] = (gg, dd, mm, vv)

    order = ("c_ctx", "w_mod", "b_mod", "norm1_g", "w_in", "b_gate", "conv_qk", "head_norm_g", "sgu_ln_g", "sgu_ln_b",
             "w_s", "b_s", "w_branch_mlstm", "w_branch_sgu", "w_out", "norm2_g", "w_up", "w_ffn_conv", "w_down",
             "final_g")
    outs = [loss, grad_x[None]]
    for k in range(4):
        outs += [results[n][k] for n in order]
    return tuple(outs)
```

```python
import functools
import math

import jax
import jax.numpy as jnp
from jax import lax
from jax.experimental import pallas as pl
from jax.experimental.pallas import tpu as pltpu

F32 = jnp.float32
BF16 = jnp.bfloat16

D = 1024
NH = 4
DH = 256
LC = 256
GW = 64
SG = 4
SGD = 256
SCH = 128
DFF = 2816
NMOD = 6
NGATE = 16
NIN = 8208
EPS = 1e-6
M_INIT = -1e30
TR = 256
TR2 = 512
LANE = 128
VMEM_LIMIT = 56 * 1024 * 1024
MESH = pl.DeviceIdType.MESH

ADAM_LR = 0.001
ADAM_B1 = 0.9
ADAM_B2 = 0.999
ADAM_EPS = 1e-08
ADAM_WD = 0.01
ADAM_STEP = 10

CB_O, CB_U, CB_VG, CB_GM, CB_GG, CB_V, CB_Q, CB_K = range(8)


def _pick(n, cands):
    for c in cands:
        if n % c == 0:
            return c
    return n


def _cparams(sem):
    return pltpu.CompilerParams(dimension_semantics=sem, vmem_limit_bytes=VMEM_LIMIT)


def _sigmoid(x):
    return 1.0 / (1.0 + jnp.exp(-x))


def _silu(x):
    return x * _sigmoid(x)


def _dsilu(x):
    s = _sigmoid(x)
    return s * (1.0 + x * (1.0 - s))


_GC = math.sqrt(2.0 / math.pi)


def _gelu(x):
    return 0.5 * x * (1.0 + jnp.tanh(_GC * (x + 0.044715 * x * x * x)))


def _gelu_and_grad(x):
    x2 = x * x
    t = jnp.tanh(_GC * x * (1.0 + 0.044715 * x2))
    half = 0.5 * (1.0 + t)
    return x * half, half + 0.5 * x * (1.0 - t * t) * _GC * (1.0 + 3.0 * 0.044715 * x2)


def _dot(a, b):
    return jnp.dot(a.astype(BF16), b.astype(BF16), preferred_element_type=F32)


def _dot_nt(a, b):
    return lax.dot_general(a.astype(BF16), b.astype(BF16), (((1,), (1,)), ((), ())), preferred_element_type=F32)


def _dot_tn(a, b):
    return lax.dot_general(a.astype(BF16), b.astype(BF16), (((0,), (0,)), ((), ())), preferred_element_type=F32)


def _dot_tn_mxu(a, b):
    m = a.shape[1]
    eye = (lax.broadcasted_iota(jnp.int32, (m, m), 0) == lax.broadcasted_iota(jnp.int32, (m, m), 1)).astype(BF16)
    return _dot(_dot_nt(eye, a), b)


def _exact_dot(tri, x):
    x1 = x.astype(BF16)
    r1 = x - x1.astype(F32)
    x2 = r1.astype(BF16)
    x3 = (r1 - x2.astype(F32)).astype(BF16)
    return (jnp.dot(tri, x1, preferred_element_type=F32) + jnp.dot(tri, x2, preferred_element_type=F32)
            + jnp.dot(tri, x3, preferred_element_type=F32))


def _rb(tm, w, col=0, off=0):
    return pl.BlockSpec((tm, w), lambda i: (i + off, col))


def _bc(r, w):
    return pl.BlockSpec((r, w), lambda i: (0, 0))


class Rider:
    def __init__(self, ins, out_shapes, n_sems, copies):
        self.ins, self.out_shapes, self.n_sems, self.copies = list(ins), list(out_shapes), n_sems, copies


def mm_nn(a, b, out_dtype, name, rider=None):
    squeeze = a.ndim == 2
    if squeeze:
        a, b = a[None], b[None]
    g, m, k = a.shape
    n = b.shape[2]
    tm = _pick(m, (2048, 1280, 1024, 512, 256, 128))
    tn = _pick(n, (2048, 1408, 1024, 512, 128))
    tk = _pick(k, (2048, 1408, 1024, 512, 128))
    nk = k // tk
    grid = (g, n // tn, m // tm, nk)
    n_rin = len(rider.ins) if rider else 0
    n_rout = len(rider.out_shapes) if rider else 0

    def body(*refs):
        a_ref, b_ref = refs[0], refs[1]
        r_in = refs[2:2 + n_rin]
        o_ref = refs[2 + n_rin]
        r_out = refs[3 + n_rin:3 + n_rin + n_rout]
        scr = refs[3 + n_rin + n_rout:]
        if rider:
            ids = [pl.program_id(d) for d in range(4)]
            first = functools.reduce(jnp.logical_and, [i == 0 for i in ids])
            last = functools.reduce(jnp.logical_and, [i == e - 1 for i, e in zip(ids, grid)])
            send_sems, recv_sems = scr[-2], scr[-1]

            @pl.when(first)
            def _():
                for cp in rider.copies(r_in, r_out, send_sems, recv_sems):
                    cp.start()

        if nk == 1:
            o_ref[0] = _dot(a_ref[0], b_ref[0]).astype(o_ref.dtype)
        else:
            acc_ref = scr[0]
            kk = pl.program_id(3)

            @pl.when(kk == 0)
            def _():
                acc_ref[...] = jnp.zeros_like(acc_ref)

            acc_ref[...] += _dot(a_ref[0], b_ref[0])

            @pl.when(kk == nk - 1)
            def _():
                o_ref[0] = acc_ref[...].astype(o_ref.dtype)

        if rider:
            @pl.when(last)
            def _():
                for cp in rider.copies(r_in, r_out, send_sems, recv_sems):
                    cp.wait()

    scratch = [] if nk == 1 else [pltpu.VMEM((tm, tn), F32)]
    if rider:
        scratch += [pltpu.SemaphoreType.DMA((rider.n_sems,)), pltpu.SemaphoreType.DMA((rider.n_sems,))]
    outs = pl.pallas_call(
        body, name=name, grid=grid,
        in_specs=[pl.BlockSpec((1, tm, tk), lambda gi, j, i, kk: (gi, i, kk)),
                  pl.BlockSpec((1, tk, tn), lambda gi, j, i, kk: (gi, kk, j))] + [_ANY] * n_rin,
        out_specs=[pl.BlockSpec((1, tm, tn), lambda gi, j, i, kk: (gi, i, j))] + [_ANY] * n_rout,
        out_shape=[jax.ShapeDtypeStruct((g, m, n), out_dtype)] + (rider.out_shapes if rider else []),
        scratch_shapes=scratch,
        compiler_params=_cparams(("arbitrary",) * 4 if rider else ("parallel", "parallel", "parallel", "arbitrary")),
    )(a, b, *(rider.ins if rider else []))
    out = outs[0][0] if squeeze else outs[0]
    return (out, list(outs[1:])) if rider else out


def mm_tn(a, b, name, out_dtype=F32):
    squeeze = a.ndim == 2
    if squeeze:
        a, b = a[None], b[None]
    g, t, ka = a.shape
    n = b.shape[2]
    tka = _pick(ka, (1024, 1408, 512, 128))
    tn = _pick(n, (2048, 1408, 1024, 512, 128))
    tt = _pick(t, (2048, 1280, 1024, 512, 256, 128))
    nt = t // tt

    def body(a_ref, b_ref, o_ref, acc_ref):
        tt_i = pl.program_id(3)

        @pl.when(tt_i == 0)
        def _():
            acc_ref[...] = jnp.zeros_like(acc_ref)

        acc_ref[...] += _dot_tn(a_ref[0], b_ref[0])

        @pl.when(tt_i == nt - 1)
        def _():
            o_ref[0] = acc_ref[...].astype(o_ref.dtype)

    out = pl.pallas_call(
        body, name=name, grid=(g, ka // tka, n // tn, nt),
        in_specs=[pl.BlockSpec((1, tt, tka), lambda gi, i, j, ti: (gi, ti, i)),
                  pl.BlockSpec((1, tt, tn), lambda gi, i, j, ti: (gi, ti, j))],
        out_specs=pl.BlockSpec((1, tka, tn), lambda gi, i, j, ti: (gi, i, j)),
        out_shape=jax.ShapeDtypeStruct((g, ka, n), out_dtype),
        scratch_shapes=[pltpu.VMEM((tka, tn), F32)],
        compiler_params=_cparams(("parallel", "parallel", "parallel", "arbitrary")),
    )(a, b)
    return out[0] if squeeze else out


MODB = 512


def mod_fwd(cc, slabs, w_cols, b_mod):
    per = w_cols // MODB
    n = slabs.shape[0] * w_cols

    def body(c_ref, w_ref, b_ref, o_ref):
        o_ref[...] = _dot(_silu(c_ref[...]), w_ref[0]) + b_ref[...]

    return pl.pallas_call(
        body, name="mod_fwd", grid=(n // MODB,),
        in_specs=[_bc(16, D), pl.BlockSpec((1, D, MODB), lambda j: (j // per, 0, j % per)),
                  pl.BlockSpec((1, MODB), lambda j: (0, j))],
        out_specs=pl.BlockSpec((16, MODB), lambda j: (0, j)),
        out_shape=jax.ShapeDtypeStruct((16, n), F32),
        compiler_params=_cparams(("parallel",)),
    )(cc, slabs, b_mod)


def _ctx_x_specs(n_ctx_tiles):
    return [pl.BlockSpec((TR, D), lambda i: (jnp.minimum(i, n_ctx_tiles - 1), 0)),
            pl.BlockSpec((TR, D), lambda i: (jnp.maximum(i - n_ctx_tiles, 0), 0))]


def norm1_fwd(ctx, x, g, tab, n_ctx_tiles):
    tp = ctx.shape[0] + x.shape[0]

    def body(c_ref, x_ref, g_ref, tab_ref, o_ref):
        is_ctx = pl.program_id(0) < n_ctx_tiles
        x = jnp.where(is_ctx, c_ref[...], x_ref[...])
        r = lax.rsqrt(jnp.mean(x * x, axis=-1, keepdims=True) + EPS)
        nrm = x * r * g_ref[...]
        sh = jnp.where(is_ctx, tab_ref[0:1, :], tab_ref[2:3, :])
        sc = jnp.where(is_ctx, tab_ref[1:2, :], tab_ref[3:4, :])
        o_ref[...] = (nrm * (1.0 + sc) + sh).astype(BF16)

    return pl.pallas_call(
        body, name="norm1_fwd", grid=(tp // TR,),
        in_specs=_ctx_x_specs(n_ctx_tiles) + [_bc(1, D), _bc(8, D)],
        out_specs=_rb(TR, D),
        out_shape=jax.ShapeDtypeStruct((tp, D), BF16),
        compiler_params=_cparams(("parallel",)),
    )(ctx, x, g, tab)


def norm2_fwd(x, out, g, tab):
    t = x.shape[0]

    def body(x_ref, o_in_ref, g_ref, tab_ref, h1_ref, hn_ref):
        h1 = x_ref[...] + tab_ref[0:1, :] * o_in_ref[...].astype(F32)
        h1_ref[...] = h1
        r = lax.rsqrt(jnp.mean(h1 * h1, axis=-1, keepdims=True) + EPS)
        nrm = h1 * r * g_ref[...]
        hn_ref[...] = (nrm * (1.0 + tab_ref[2:3, :]) + tab_ref[1:2, :]).astype(BF16)

    return pl.pallas_call(
        body, name="norm2_fwd", grid=(t // TR2,),
        in_specs=[_rb(TR2, D), _rb(TR2, D), _bc(1, D), _bc(8, D)],
        out_specs=[_rb(TR2, D), _rb(TR2, D)],
        out_shape=[jax.ShapeDtypeStruct((t, D), F32), jax.ShapeDtypeStruct((t, D), BF16)],
        compiler_params=_cparams(("parallel",)),
    )(x, out, g, tab)


def _halo_specs(tm, w, col, n_rows, hb):
    per = tm // hb
    last = n_rows // hb - 1
    prev = pl.BlockSpec((hb, w), lambda i: (jnp.maximum(i * per - 1, 0), col))
    nxt = pl.BlockSpec((hb, w), lambda i: (jnp.minimum((i + 1) * per, last), col))
    return prev, nxt


def _shift_rows(x, prev_row, next_row):
    tm = x.shape[0]
    rid = lax.broadcasted_iota(jnp.int32, x.shape, 0)
    xm1 = jnp.where(rid == 0, prev_row, pltpu.roll(x, 1, 0))
    xp1 = jnp.where(rid == tm - 1, next_row, pltpu.roll(x, tm - 1, 0))
    return xm1, xp1


def _seq_edges(i, n_ctx_tiles, n_tiles):
    first = jnp.logical_or(i == 0, i == n_ctx_tiles)
    last = jnp.logical_or(i == n_ctx_tiles - 1, i == n_tiles - 1)
    return first, last


def qkconv_fwd(zmain, zg, conv_w, b_gate, n_ctx_tiles):
    tp = zmain.shape[0]
    nt = tp // TR
    w2 = 2 * D
    prev_s, next_s = _halo_specs(TR, w2, CB_Q // 2, tp, 16)

    def body(z_ref, zp_ref, zn_ref, w_ref, zg_ref, bg_ref, q_ref, k_ref, g_ref, zgb_ref):
        i = pl.program_id(0)
        first, last = _seq_edges(i, n_ctx_tiles, nt)
        z = z_ref[...].astype(F32)
        pr = jnp.where(first, 0.0, zp_ref[15:16, :].astype(F32))
        nx = jnp.where(last, 0.0, zn_ref[0:1, :].astype(F32))
        zm1, zp1 = _shift_rows(z, pr, nx)
        cv = w_ref[0:1, :] * zm1 + w_ref[1:2, :] * z + w_ref[2:3, :] * zp1
        a = _silu(cv)
        q_ref[...] = (a[:, :D] * (DH ** -0.5)).astype(BF16)
        k_ref[...] = a[:, D:].astype(BF16)
        zgb = zg_ref[...] + bg_ref[...]
        zgb_ref[...] = zgb
        logf = jnp.minimum(zgb, 0.0) - jnp.log(1.0 + jnp.exp(-jnp.abs(zgb)))
        rr = lax.broadcasted_iota(jnp.int32, (TR, TR), 0)
        cc = lax.broadcasted_iota(jnp.int32, (TR, TR), 1)
        same = (rr // LC) == (cc // LC)
        low = jnp.where(jnp.logical_and(same, cc <= rr), 1.0, 0.0).astype(BF16)
        upp = jnp.where(jnp.logical_and(same, cc >= rr), 1.0, 0.0).astype(BF16)
        bf = _exact_dot(low, logf)
        bb = _exact_dot(upp, logf)
        lane = lax.broadcasted_iota(jnp.int32, (TR, LANE), 1)
        g = jnp.where(jnp.logical_and(lane >= 4, lane < 8), bf,
                      jnp.where(jnp.logical_and(lane >= 12, lane < 16), bb, zgb))
        g_ref[...] = g

    return pl.pallas_call(
        body, name="qkconv_fwd", grid=(nt,),
        in_specs=[_rb(TR, w2, CB_Q // 2), prev_s, next_s, _bc(8, w2), _rb(TR, LANE), _bc(1, LANE)],
        out_specs=[_rb(TR, D), _rb(TR, D), _rb(TR, LANE), _rb(TR, LANE)],
        out_shape=[jax.ShapeDtypeStruct((tp, D), BF16), jax.ShapeDtypeStruct((tp, D), BF16),
                   jax.ShapeDtypeStruct((tp, LANE), F32), jax.ShapeDtypeStruct((tp, LANE), F32)],
        compiler_params=_cparams(("parallel",)),
    )(zmain, zmain, zmain, conv_w, zg, b_gate)


def qkconv_bwd_a(zmain, dqf, dqb, dkf, dkb, conv_w, n_ctx_tiles):
    tp = zmain.shape[0]
    nt = tp // TR
    w2 = 2 * D
    prev_s, next_s = _halo_specs(TR, w2, CB_Q // 2, tp, 16)

    def body(z_ref, zp_ref, zn_ref, w_ref, dqf_ref, dqb_ref, dkf_ref, dkb_ref, dc_ref, dw_ref):
        i = pl.program_id(0)
        first, last = _seq_edges(i, n_ctx_tiles, nt)
        z = z_ref[...].astype(F32)
        pr = jnp.where(first, 0.0, zp_ref[15:16, :].astype(F32))
        nx = jnp.where(last, 0.0, zn_ref[0:1, :].astype(F32))
        zm1, zp1 = _shift_rows(z, pr, nx)
        cv = w_ref[0:1, :] * zm1 + w_ref[1:2, :] * z + w_ref[2:3, :] * zp1
        da = jnp.concatenate(
            [(dqf_ref[...].astype(F32) + dqb_ref[...].astype(F32)) * (DH ** -0.5),
             dkf_ref[...].astype(F32) + dkb_ref[...].astype(F32)], axis=1)
        dc = da * _dsilu(cv)
        dc_ref[...] = dc.astype(BF16)

        @pl.when(i == 0)
        def _():
            dw_ref[...] = jnp.zeros_like(dw_ref)

        dw_ref[0:1, :] += jnp.sum(zm1 * dc, axis=0, keepdims=True)
        dw_ref[1:2, :] += jnp.sum(z * dc, axis=0, keepdims=True)
        dw_ref[2:3, :] += jnp.sum(zp1 * dc, axis=0, keepdims=True)

    return pl.pallas_call(
        body, name="qkconv_bwd_a", grid=(nt,),
        in_specs=[_rb(TR, w2, CB_Q // 2), prev_s, next_s, _bc(8, w2), _rb(TR, D), _rb(TR, D), _rb(TR, D), _rb(TR, D)],
        out_specs=[_rb(TR, w2), _bc(8, w2)],
        out_shape=[jax.ShapeDtypeStruct((tp, w2), BF16), jax.ShapeDtypeStruct((8, w2), F32)],
        compiler_params=_cparams(("arbitrary",)),
    )(zmain, zmain, zmain, conv_w, dqf, dqb, dkf, dkb)


def qkconv_bwd_b(dz, dc, conv_w, n_ctx_tiles):
    tp = dc.shape[0]
    nt = tp // TR
    w2 = 2 * D
    prev_s, next_s = _halo_specs(TR, w2, 0, tp, 16)

    def body(dz_in_ref, d_ref, dp_ref, dn_ref, w_ref, o_ref):
        del dz_in_ref
        i = pl.program_id(0)
        first, last = _seq_edges(i, n_ctx_tiles, nt)
        d = d_ref[...].astype(F32)
        pr = jnp.where(first, 0.0, dp_ref[15:16, :].astype(F32))
        nx = jnp.where(last, 0.0, dn_ref[0:1, :].astype(F32))
        dm1, dp1 = _shift_rows(d, pr, nx)
        o_ref[...] = (w_ref[0:1, :] * dp1 + w_ref[1:2, :] * d + w_ref[2:3, :] * dm1).astype(BF16)

    return pl.pallas_call(
        body, name="qkconv_bwd_b", grid=(nt,),
        in_specs=[pl.BlockSpec(memory_space=pl.ANY), _rb(TR, w2), prev_s, next_s, _bc(8, w2)],
        out_specs=_rb(TR, w2, CB_Q // 2),
        out_shape=jax.ShapeDtypeStruct(dz.shape, BF16),
        input_output_aliases={0: 0},
        compiler_params=_cparams(("parallel",)),
    )(dz, dc, dc, dc, conv_w)


def add_into_dz(dz, a, b, col):
    tp = a.shape[0]

    def body(dz_in_ref, a_ref, b_ref, o_ref):
        del dz_in_ref
        o_ref[...] = (a_ref[...].astype(F32) + b_ref[...].astype(F32)).astype(BF16)

    return pl.pallas_call(
        body, name="add_into_dz", grid=(tp // TR,),
        in_specs=[pl.BlockSpec(memory_space=pl.ANY), _rb(TR, D), _rb(TR, D)],
        out_specs=_rb(TR, D, col),
        out_shape=jax.ShapeDtypeStruct(dz.shape, BF16),
        input_output_aliases={0: 0},
        compiler_params=_cparams(("parallel",)),
    )(dz, a, b)


def _chunk_maps(nc, ncc):
    def fwd(t):
        return t

    def bwd(t):
        return jnp.where(t < ncc, ncc - 1 - t, nc - 1 + ncc - t)

    return fwd, bwd


def _split2(x):
    hi = x.astype(BF16)
    return hi, (x - hi.astype(F32)).astype(BF16)


def _mlstm_chunk(d, h, gc, gr, q_ref, k_ref, v_ref, cp, npv, m_prev, mask, precise=False):
    ic, bcol = 8 * d + h, 8 * d + 4 + h
    i_col, b_col = gc[:, ic:ic + 1], gc[:, bcol:bcol + 1]
    i_row, b_row = gr[ic:ic + 1, :], gr[bcol:bcol + 1, :]
    g = b_row[:, LC - 1:LC] if d == 0 else b_row[:, 0:1]
    a_row = g - b_row + i_row
    m_loc = jnp.max(a_row, axis=1, keepdims=True)
    dmat = jnp.where(mask, b_col - b_row + i_row, -jnp.inf)
    inter = b_col + m_prev
    m_row = jnp.maximum(inter, jnp.max(dmat, axis=1, keepdims=True))
    e = jnp.exp(dmat - m_row)
    w = jnp.exp(inter - m_row)
    hs = slice(h * DH, (h + 1) * DH)
    qh, kh, vh = q_ref[:, hs], k_ref[:, hs], v_ref[:, hs]
    p = _dot_nt(qh, kh)
    s = p * e
    cpb = cp.astype(BF16)
    qc = _dot(qh, cpb)
    if precise:
        s_hi, s_lo = _split2(s)
        num = _dot(s_hi, vh) + _dot(s_lo, vh) + w * qc
    else:
        num = _dot(s, vh) + w * qc
    qn = jnp.sum(qh.astype(F32) * npv, axis=1, keepdims=True)
    den = jnp.sum(s, axis=1, keepdims=True) + w * qn
    thr = jnp.exp(-m_row)
    m_new = jnp.maximum(g + m_prev, m_loc)
    a_old = jnp.exp(g + m_prev - m_new)
    a_col = g - b_col + i_col
    return dict(qh=qh, kh=kh, vh=vh, e=e, w=w, s=s, cpb=cpb, qc=qc, num=num, qn=qn, den=den, thr=thr,
                m_loc=m_loc, m_new=m_new, a_old=a_old, a_col=a_col, hs=hs)


def mlstm_fwd(qa, ka, zmain, gcol, grow, ncc):
    tp = qa.shape[0]
    nc = tp // LC
    cf, cb = _chunk_maps(nc, ncc)

    def body(qf, kf, vf, gcf, grf, qb, kb, vb, gcb, grb,
             hf_o, hb_o, cf_o, cb_o, nf_o, nb_o, mf_o, mb_o, c_sc, n_sc, m_sc):
        t = pl.program_id(0)

        @pl.when(t == 0)
        def _():
            c_sc[...] = jnp.zeros_like(c_sc)
            n_sc[...] = jnp.zeros_like(n_sc)
            m_sc[...] = jnp.full(m_sc.shape, M_INIT, F32)

        row = lax.broadcasted_iota(jnp.int32, (LC, LC), 0)
        col = lax.broadcasted_iota(jnp.int32, (LC, LC), 1)
        dirs = ((qf, kf, vf, gcf, grf, hf_o, cf_o, nf_o, mf_o), (qb, kb, vb, gcb, grb, hb_o, cb_o, nb_o, mb_o))
        for d, (q_ref, k_ref, v_ref, gc_ref, gr_ref, h_o, c_o, n_o, m_o) in enumerate(dirs):
            mask = (col <= row) if d == 0 else (col >= row)
            gc = gc_ref[...]
            gr = gr_ref[0]
            for h in range(NH):
                idx = d * NH + h
                cp = c_sc[idx]
                npv = n_sc[idx]
                m_full = m_sc[idx]
                m_prev = m_full[:, 0:1]
                r = _mlstm_chunk(d, h, gc, gr, q_ref, k_ref, v_ref, cp, npv, m_prev, mask)
                hs = r["hs"]
                h_o[:, hs] = (r["num"] / jnp.maximum(jnp.abs(r["den"]), r["thr"])).astype(BF16)
                c_o[0, hs, :] = r["cpb"]
                n_o[0, h:h + 1, :] = npv
                m_o[0, h:h + 1, :] = m_full
                a_new = jnp.exp(r["m_loc"] - r["m_new"])
                kw = r["kh"].astype(F32) * jnp.exp(r["a_col"] - r["m_loc"])
                kv = _dot_tn_mxu(kw, r["vh"])
                kn = jnp.sum(kw, axis=0, keepdims=True)
                c_sc[idx] = r["a_old"] * cp + a_new * kv
                n_sc[idx] = r["a_old"] * npv + a_new * kn
                m_sc[idx] = jnp.broadcast_to(r["m_new"], (1, LANE))

    def dspecs(cm):
        return [pl.BlockSpec((LC, D), lambda t: (cm(t), 0)),
                pl.BlockSpec((LC, D), lambda t: (cm(t), 0)),
                pl.BlockSpec((LC, D), lambda t: (cm(t), CB_V)),
                pl.BlockSpec((LC, LANE), lambda t: (cm(t), 0)),
                pl.BlockSpec((1, 16, LC), lambda t: (cm(t), 0, 0))]

    def ospec(cm, shp):
        return pl.BlockSpec((1,) + shp, lambda t: (cm(t), 0, 0))

    return pl.pallas_call(
        body, name="mlstm_fwd", grid=(nc,),
        in_specs=dspecs(cf) + dspecs(cb),
        out_specs=[pl.BlockSpec((LC, D), lambda t: (cf(t), 0)), pl.BlockSpec((LC, D), lambda t: (cb(t), 0)),
                   ospec(cf, (D, DH)), ospec(cb, (D, DH)), ospec(cf, (NH, DH)), ospec(cb, (NH, DH)),
                   ospec(cf, (NH, LANE)), ospec(cb, (NH, LANE))],
        out_shape=[jax.ShapeDtypeStruct((tp, D), BF16), jax.ShapeDtypeStruct((tp, D), BF16),
                   jax.ShapeDtypeStruct((nc, D, DH), BF16), jax.ShapeDtypeStruct((nc, D, DH), BF16),
                   jax.ShapeDtypeStruct((nc, NH, DH), F32), jax.ShapeDtypeStruct((nc, NH, DH), F32),
                   jax.ShapeDtypeStruct((nc, NH, LANE), F32), jax.ShapeDtypeStruct((nc, NH, LANE), F32)],
        scratch_shapes=[pltpu.VMEM((2 * NH, DH, DH), F32), pltpu.VMEM((2 * NH, 1, DH), F32),
                        pltpu.VMEM((2 * NH, 1, LANE), F32)],
        compiler_params=_cparams(("arbitrary",)),
    )(qa, ka, zmain, gcol, grow, qa, ka, zmain, gcol, grow)


def mlstm_bwd(qa, ka, zmain, gcol, grow, states, dhm, ncc):
    tp = qa.shape[0]
    nc = tp // LC
    cf0, cb0 = _chunk_maps(nc, ncc)
    cf = lambda t: cf0(nc - 1 - t)
    cb = lambda t: cb0(nc - 1 - t)
    csf, csb, nsf, nsb, msf, msb = states

    def body(qf, kf, vf, gcf, grf, cpf, npf, mpf, dhf, qb, kb, vb, gcb, grb, cpb_, npb, mpb, dhb,
             dqf_o, dkf_o, dvf_o, colf_o, rowf_o, dqb_o, dkb_o, dvb_o, colb_o, rowb_o, dc_sc, dn_sc):
        t = pl.program_id(0)

        @pl.when(t == 0)
        def _():
            dc_sc[...] = jnp.zeros_like(dc_sc)
            dn_sc[...] = jnp.zeros_like(dn_sc)

        row = lax.broadcasted_iota(jnp.int32, (LC, LC), 0)
        col = lax.broadcasted_iota(jnp.int32, (LC, LC), 1)
        dirs = ((qf, kf, vf, gcf, grf, cpf, npf, mpf, dhf, dqf_o, dkf_o, dvf_o, colf_o, rowf_o, cf),
                (qb, kb, vb, gcb, grb, cpb_, npb, mpb, dhb, dqb_o, dkb_o, dvb_o, colb_o, rowb_o, cb))
        for d, (q_ref, k_ref, v_ref, gc_ref, gr_ref, cp_ref, np_ref, mp_ref, dh_ref,
                dq_o, dk_o, dv_o, col_o, row_o, cm) in enumerate(dirs):
            mask = (col <= row) if d == 0 else (col >= row)
            live = jnp.where(cm(t) >= ncc, 1.0, 0.0).astype(F32)
            gc = gc_ref[...]
            gr = gr_ref[0]
            col_o[...] = jnp.zeros_like(col_o)
            row_o[...] = jnp.zeros_like(row_o)
            for h in range(NH):
                idx = d * NH + h
                hs = slice(h * DH, (h + 1) * DH)
                cp = cp_ref[0, hs, :]
                npv = np_ref[0, h:h + 1, :]
                m_prev = mp_ref[0, h:h + 1, 0:1]
                r = _mlstm_chunk(d, h, gc, gr, q_ref, k_ref, v_ref, cp, npv, m_prev, mask, precise=True)
                qh, kh, vh, e, w, s = r["qh"], r["kh"], r["vh"], r["e"], r["w"], r["s"]
                qf32, kf32 = qh.astype(F32), kh.astype(F32)
                den, thr = r["den"], r["thr"]
                rden = 1.0 / jnp.maximum(jnp.abs(den), thr)
                hh = r["num"] * rden
                dh = dh_ref[:, hs].astype(F32) * live
                dnum = dh * rden
                sgn = jnp.where(jnp.abs(den) > thr, jnp.sign(den), 0.0)
                dden = -jnp.sum(dh * hh, axis=1, keepdims=True) * rden * sgn
                dn_hi, dn_lo = _split2(dnum)
                ds = _dot_nt(dn_hi, vh) + _dot_nt(dn_lo, vh) + dden
                dp = ds * e
                gm = ds * s
                rowsum = jnp.sum(gm, axis=1, keepdims=True)
                colsum = jnp.sum(gm, axis=0, keepdims=True)
                dq = _dot(dp, kh) + w * (_dot_nt(dnum, r["cpb"]) + dden * npv)
                dcs = dc_sc[idx]
                dns = dn_sc[idx]
                kfac = jnp.exp(r["a_col"] - r["m_new"])
                vdc = _dot_nt(vh, dcs)
                dk = _dot_tn(dp, qh) + kfac * (vdc + dns)
                dv = _dot_tn(s, dnum) + kfac * _dot(kh, dcs)
                beta = w * (jnp.sum(dnum * r["qc"], axis=1, keepdims=True) + dden * r["qn"])
                alpha = kfac * (jnp.sum(kf32 * vdc, axis=1, keepdims=True) + jnp.sum(kf32 * dns, axis=1, keepdims=True))
                dq_o[:, hs] = dq.astype(BF16)
                dk_o[:, hs] = dk.astype(BF16)
                dv_o[:, hs] = dv.astype(BF16)
                cpf = r["cpb"].astype(F32)
                inner = (jnp.sum(jnp.sum(dcs * cpf, axis=1, keepdims=True), axis=0, keepdims=True)
                         + jnp.sum(dns * npv, axis=1, keepdims=True))
                gam = jnp.sum(alpha, axis=0, keepdims=True) + r["a_old"] * inner
                lo = 8 * d + h
                col_o[:, lo:lo + 1] = alpha
                col_o[:, lo + 4:lo + 5] = rowsum + beta - alpha
                col_o[:, lo + 36:lo + 37] = jnp.broadcast_to(gam, (LC, 1))
                row_o[0, h:h + 1, :] = colsum
                wq = qf32 * w
                dc_sc[idx] = r["a_old"] * dcs + _dot_tn(wq, dnum)
                dn_sc[idx] = r["a_old"] * dns + jnp.sum(wq * dden, axis=0, keepdims=True)

    def dspecs(cm):
        return [pl.BlockSpec((LC, D), lambda t: (cm(t), 0)),
                pl.BlockSpec((LC, D), lambda t: (cm(t), 0)),
                pl.BlockSpec((LC, D), lambda t: (cm(t), CB_V)),
                pl.BlockSpec((LC, LANE), lambda t: (cm(t), 0)),
                pl.BlockSpec((1, 16, LC), lambda t: (cm(t), 0, 0)),
                pl.BlockSpec((1, D, DH), lambda t: (cm(t), 0, 0)),
                pl.BlockSpec((1, NH, DH), lambda t: (cm(t), 0, 0)),
                pl.BlockSpec((1, NH, LANE), lambda t: (cm(t), 0, 0)),
                pl.BlockSpec((LC, D), lambda t: (jnp.maximum(cm(t) - ncc, 0), 0))]

    def ospecs(cm):
        return [pl.BlockSpec((LC, D), lambda t: (cm(t), 0)),
                pl.BlockSpec((LC, D), lambda t: (cm(t), 0)),
                pl.BlockSpec((LC, D), lambda t: (cm(t), 0)),
                pl.BlockSpec((LC, LANE), lambda t: (cm(t), 0)),
                pl.BlockSpec((1, 8, LC), lambda t: (cm(t), 0, 0))]

    oshape = [jax.ShapeDtypeStruct((tp, D), BF16)] * 3 + [jax.ShapeDtypeStruct((tp, LANE), F32),
                                                        jax.ShapeDtypeStruct((nc, 8, LC), F32)]
    return pl.pallas_call(
        body, name="mlstm_bwd", grid=(nc,),
        in_specs=dspecs(cf) + dspecs(cb),
        out_specs=ospecs(cf) + ospecs(cb),
        out_shape=oshape + oshape,
        scratch_shapes=[pltpu.VMEM((2 * NH, DH, DH), F32), pltpu.VMEM((2 * NH, 1, DH), F32)],
        compiler_params=_cparams(("arbitrary",)),
    )(qa, ka, zmain, gcol, grow, csf, nsf, msf, dhm, qa, ka, zmain, gcol, grow, csb, nsb, msb, dhm)


def gates_bwd(colf, colb, cs, zgb):
    tp = colf.shape[0]

    def body(cf_ref, cb_ref, cs_ref, zgb_ref, o_ref, db_ref):
        i = pl.program_id(0)

        @pl.when(i == 0)
        def _():
            db_ref[...] = jnp.zeros_like(db_ref)

        lane = lax.broadcasted_iota(jnp.int32, (TR, LANE), 1)
        i_l = jnp.logical_or(lane < 4, jnp.logical_and(lane >= 8, lane < 12))
        f_l = jnp.logical_or(jnp.logical_and(lane >= 4, lane < 8), jnp.logical_and(lane >= 12, lane < 16))
        cv = cf_ref[...] + cb_ref[...]
        csv = cs_ref[...]
        gam = pltpu.roll(cv, LANE - 32, 1)
        dbh = jnp.where(f_l, cv - csv, 0.0)
        rr = lax.broadcasted_iota(jnp.int32, (TR, TR), 0)
        cc = lax.broadcasted_iota(jnp.int32, (TR, TR), 1)
        same = (rr // LC) == (cc // LC)
        low = jnp.where(jnp.logical_and(same, cc <= rr), 1.0, 0.0).astype(BF16)
        upp = jnp.where(jnp.logical_and(same, cc >= rr), 1.0, 0.0).astype(BF16)
        dlogf = jnp.where(lane < 8, _exact_dot(upp, dbh), _exact_dot(low, dbh)) + gam
        out = jnp.where(i_l, csv + cv, 0.0) + jnp.where(f_l, dlogf * _sigmoid(-zgb_ref[...]), 0.0)
        o_ref[...] = out
        db_ref[...] += jnp.sum(out, axis=0, keepdims=True)

    spec = _rb(TR, LANE)
    return pl.pallas_call(
        body, name="gates_bwd", grid=(tp // TR,),
        in_specs=[spec] * 4,
        out_specs=[spec, _bc(1, LANE)],
        out_shape=[jax.ShapeDtypeStruct((tp, LANE), F32), jax.ShapeDtypeStruct((1, LANE), F32)],
        compiler_params=_cparams(("arbitrary",)),
    )(colf, colb, cs, zgb)


def _head_norm(hm, gh):
    xs, rs = [], []
    for h in range(NH):
        seg = hm[:, h * DH:(h + 1) * DH]
        r = lax.rsqrt(jnp.mean(seg * seg, axis=-1, keepdims=True) + EPS)
        xs.append(seg * r)
        rs.append(r)
    xh = jnp.concatenate(xs, axis=1)
    return xh, rs, xh * gh


def _sgu_norm(vg, ln_g, ln_b):
    mu = jnp.mean(vg, axis=-1, keepdims=True)
    vc = vg - mu
    rstd = lax.rsqrt(jnp.mean(vc * vc, axis=-1, keepdims=True) + EPS)
    vhat = vc * rstd
    return vhat, rstd, vhat * ln_g + ln_b


def _sgu_mix(vn, ws_ref, bs_ref):
    rows = []
    for c in range(TR // SCH):
        cols = []
        for g in range(SG):
            blk = vn[c * SCH:(c + 1) * SCH, g * SGD:(g + 1) * SGD]
            cols.append(_dot(ws_ref[g * SCH:(g + 1) * SCH, :], blk) + bs_ref[:, g:g + 1])
        rows.append(jnp.concatenate(cols, axis=1))
    return jnp.concatenate(rows, axis=0)


def mixer_fwd(hf, hb, zmain, gh, ln_g, ln_b, ws, bs_t, n_ctx_tiles):
    t = hf.shape[0] - n_ctx_tiles * TR
    off = n_ctx_tiles

    def body(hf_ref, hb_ref, zo_ref, zu_ref, zv_ref, gh_ref, lg_ref, lb_ref, ws_ref, bs_ref, o_ref):
        hm = hf_ref[...].astype(F32) + hb_ref[...].astype(F32)
        _, _, hn = _head_norm(hm, gh_ref[...])
        o_ref[0] = (_sigmoid(zo_ref[...].astype(F32)) * hn).astype(BF16)
        _, _, vn = _sgu_norm(_gelu(zv_ref[...].astype(F32)), lg_ref[...], lb_ref[...])
        mixed = _sgu_mix(vn, ws_ref, bs_ref)
        o_ref[1] = (_gelu(zu_ref[...].astype(F32)) * mixed).astype(BF16)

    return pl.pallas_call(
        body, name="mixer_fwd", grid=(t // TR,),
        in_specs=[_rb(TR, D, 0, off), _rb(TR, D, 0, off), _rb(TR, D, CB_O, off), _rb(TR, D, CB_U, off),
                  _rb(TR, D, CB_VG, off), _bc(1, D), _bc(1, D), _bc(1, D), _bc(SG * SCH, SCH), _bc(SCH, LANE)],
        out_specs=pl.BlockSpec((2, TR, D), lambda i: (0, i, 0)),
        out_shape=jax.ShapeDtypeStruct((2, t, D), BF16),
        compiler_params=_cparams(("parallel",)),
    )(hf, hb, zmain, zmain, zmain, gh, ln_g, ln_b, ws, bs_t)


def merge_fwd(zmain, pp, n_ctx_tiles):
    t = pp.shape[1]
    off = n_ctx_tiles

    def body(zgm_ref, zgg_ref, pp_ref, o_ref):
        y = (_sigmoid(zgm_ref[...].astype(F32)) * pp_ref[0].astype(F32)
             + _sigmoid(zgg_ref[...].astype(F32)) * pp_ref[1].astype(F32))
        o_ref[...] = y.astype(BF16)

    return pl.pallas_call(
        body, name="merge_fwd", grid=(t // TR,),
        in_specs=[_rb(TR, D, CB_GM, off), _rb(TR, D, CB_GG, off), pl.BlockSpec((2, TR, D), lambda i: (0, i, 0))],
        out_specs=_rb(TR, D),
        out_shape=jax.ShapeDtypeStruct((t, D), BF16),
        compiler_params=_cparams(("parallel",)),
    )(zmain, zmain, pp)


def merge_bwd(zmain, pp, dy, tp, n_ctx_tiles):
    t = dy.shape[0]
    nt = tp // TR
    xrow = lambda i: jnp.maximum(i - n_ctx_tiles, 0)

    def body(zg_ref, pp_ref, dy_ref, dpp_ref, dz_ref):
        i = pl.program_id(1)
        zg = zg_ref[...].astype(F32)
        sg = _sigmoid(zg)
        dyv = dy_ref[...].astype(F32)
        dpp_ref[0] = (dyv * sg).astype(BF16)
        dzv = dyv * pp_ref[0].astype(F32) * sg * (1.0 - sg)
        dz_ref[...] = jnp.where(i >= n_ctx_tiles, dzv, 0.0).astype(BF16)

    return pl.pallas_call(
        body, name="merge_bwd", grid=(2, nt),
        in_specs=[pl.BlockSpec((TR, D), lambda j, i: (i, CB_GM + j)),
                  pl.BlockSpec((1, TR, D), lambda j, i: (j, xrow(i), 0)),
                  pl.BlockSpec((TR, D), lambda j, i: (xrow(i), 0))],
        out_specs=[pl.BlockSpec((1, TR, D), lambda j, i: (j, xrow(i), 0)),
                   pl.BlockSpec((TR, D), lambda j, i: (i, CB_GM + j))],
        out_shape=[jax.ShapeDtypeStruct((2, t, D), BF16), jax.ShapeDtypeStruct((tp, 8 * D), BF16)],
        compiler_params=_cparams(("arbitrary", "arbitrary")),
    )(zmain, pp, dy)


def mixer_bwd(dz, hf, hb, zmain, dyms, gh, ln_g, ln_b, ws, bs_t, n_ctx_tiles):
    tp = hf.shape[0]
    t = tp - n_ctx_tiles * TR
    nt = tp // TR
    xrow = lambda i: jnp.maximum(i - n_ctx_tiles, 0)

    def body(dz_in_ref, hf_ref, hb_ref, zo_ref, zu_ref, zv_ref, dy_ref, gh_ref, lg_ref, lb_ref, ws_ref, bs_ref,
             dz_ref, dhm_ref, dgh_ref, dlg_ref, dlb_ref, dws_ref, dbs_ref):
        del dz_in_ref
        i = pl.program_id(0)

        @pl.when(i == 0)
        def _():
            for ref in (dgh_ref, dlg_ref, dlb_ref, dws_ref, dbs_ref):
                ref[...] = jnp.zeros_like(ref)

        @pl.when(i < n_ctx_tiles)
        def _():
            dz_ref[...] = jnp.zeros_like(dz_ref)

        @pl.when(i >= n_ctx_tiles)
        def _():
            gh_v = gh_ref[...]
            hm = hf_ref[...].astype(F32) + hb_ref[...].astype(F32)
            xh, rs, hn = _head_norm(hm, gh_v)
            zo = zo_ref[...].astype(F32)
            so = _sigmoid(zo)
            dym = dy_ref[0].astype(F32)
            d_zo = dym * hn * so * (1.0 - so)
            d_hn = dym * so
            dgh_ref[...] += jnp.sum(d_hn * xh, axis=0, keepdims=True)
            d_xh = d_hn * gh_v
            segs = []
            for h in range(NH):
                hs = slice(h * DH, (h + 1) * DH)
                dx, xs = d_xh[:, hs], xh[:, hs]
                segs.append(rs[h] * (dx - xs * jnp.mean(dx * xs, axis=-1, keepdims=True)))
            dhm_ref[...] = jnp.concatenate(segs, axis=1).astype(BF16)
            zu = zu_ref[...].astype(F32)
            zv = zv_ref[...].astype(F32)
            lg = lg_ref[...]
            gu, dgu = _gelu_and_grad(zu)
            gv, dgv = _gelu_and_grad(zv)
            vhat, rstd, vn = _sgu_norm(gv, lg, lb_ref[...])
            mixed = _sgu_mix(vn, ws_ref, bs_ref)
            dys = dy_ref[1].astype(F32)
            d_zu = dys * mixed * dgu
            d_mixed = dys * gu
            rows = []
            for c in range(TR // SCH):
                cols = []
                for g in range(SG):
                    rsl, csl = slice(c * SCH, (c + 1) * SCH), slice(g * SGD, (g + 1) * SGD)
                    dm = d_mixed[rsl, csl]
                    cols.append(_dot_tn(ws_ref[g * SCH:(g + 1) * SCH, :], dm))
                    dws_ref[g * SCH:(g + 1) * SCH, :] += _dot_nt(dm, vn[rsl, csl])
                    dbs_ref[:, g:g + 1] += jnp.sum(dm, axis=1, keepdims=True)
                rows.append(jnp.concatenate(cols, axis=1))
            d_vn = jnp.concatenate(rows, axis=0)
            dlg_ref[...] += jnp.sum(d_vn * vhat, axis=0, keepdims=True)
            dlb_ref[...] += jnp.sum(d_vn, axis=0, keepdims=True)
            d_vhat = d_vn * lg
            d_vg = rstd * (d_vhat - jnp.mean(d_vhat, axis=-1, keepdims=True)
                           - vhat * jnp.mean(d_vhat * vhat, axis=-1, keepdims=True))
            d_zv = d_vg * dgv
            dz_ref[...] = jnp.concatenate([d_zo, d_zu, d_zv], axis=1).astype(BF16)

    return pl.pallas_call(
        body, name="mixer_bwd", grid=(nt,),
        in_specs=[pl.BlockSpec(memory_space=pl.ANY), _rb(TR, D), _rb(TR, D), _rb(TR, D, CB_O), _rb(TR, D, CB_U),
                  _rb(TR, D, CB_VG), pl.BlockSpec((2, TR, D), lambda i: (0, xrow(i), 0)),
                  _bc(1, D), _bc(1, D), _bc(1, D), _bc(SG * SCH, SCH), _bc(SCH, LANE)],
        out_specs=[_rb(TR, 3 * D), pl.BlockSpec((TR, D), lambda i: (xrow(i), 0)),
                   _bc(1, D), _bc(1, D), _bc(1, D), _bc(SG * SCH, SCH), _bc(SCH, LANE)],
        out_shape=[jax.ShapeDtypeStruct(dz.shape, BF16), jax.ShapeDtypeStruct((t, D), BF16),
                   jax.ShapeDtypeStruct((1, D), F32), jax.ShapeDtypeStruct((1, D), F32),
                   jax.ShapeDtypeStruct((1, D), F32), jax.ShapeDtypeStruct((SG * SCH, SCH), F32),
                   jax.ShapeDtypeStruct((SCH, LANE), F32)],
        input_output_aliases={0: 0},
        compiler_params=_cparams(("arbitrary",)),
    )(dz, hf, hb, zmain, zmain, zmain, dyms, gh, ln_g, ln_b, ws, bs_t)


FCB = DFF // 2
TF = 512


def _ffn_halo(col, t):
    per = TF // GW
    last = t // GW - 1
    prev = pl.BlockSpec((GW, FCB), lambda i, j: (jnp.maximum(i * per - 1, 0), col(j)))
    nxt = pl.BlockSpec((GW, FCB), lambda i, j: (jnp.minimum((i + 1) * per, last), col(j)))
    return prev, nxt


def _conv_taps(ext):
    n = ext.shape[0]
    colid = lax.broadcasted_iota(jnp.int32, (n, 1), 0) % GW
    left = pltpu.roll(jnp.where(colid != GW - 1, ext, 0.0), 1, 0)
    right = pltpu.roll(jnp.where(colid != 0, ext, 0.0), n - 1, 0)
    views = (left, ext, right)
    return {(ky, kx): views[kx][GW * ky:GW * ky + TF] for ky in range(3) for kx in range(3)}


def _ext(c_ref, p_ref, n_ref, i, nt):
    pr = jnp.where(i == 0, 0.0, p_ref[...].astype(F32))
    nx = jnp.where(i == nt - 1, 0.0, n_ref[...].astype(F32))
    return jnp.concatenate([pr, c_ref[...].astype(F32), nx], axis=0)


def ffn_act_fwd(up, wc):
    t = up.shape[0]
    nt = t // TF
    prev_s, next_s = _ffn_halo(lambda j: j, t)

    def body(a_ref, ap_ref, an_ref, b_ref, w_ref, o_ref, ac_ref):
        i = pl.program_id(0)
        taps = _conv_taps(_ext(a_ref, ap_ref, an_ref, i, nt))
        ac = sum(w_ref[3 * ky + kx:3 * ky + kx + 1, :] * taps[(ky, kx)] for ky in range(3) for kx in range(3))
        ac_ref[...] = ac.astype(BF16)
        o_ref[...] = (_silu(ac) * b_ref[...].astype(F32)).astype(BF16)

    spec = pl.BlockSpec((TF, FCB), lambda i, j: (i, j))
    return pl.pallas_call(
        body, name="ffn_act_fwd", grid=(nt, 2),
        in_specs=[spec, prev_s, next_s,
                  pl.BlockSpec((TF, FCB), lambda i, j: (i, 2 + j)), pl.BlockSpec((16, FCB), lambda i, j: (0, j))],
        out_specs=[spec, spec],
        out_shape=[jax.ShapeDtypeStruct((t, DFF), BF16), jax.ShapeDtypeStruct((t, DFF), BF16)],
        compiler_params=_cparams(("parallel", "parallel")),
    )(up, up, up, up, wc)


def ffn_act_bwd(up, ac, dact):
    t = up.shape[0]
    nt = t // TF

    def body(b_ref, ac_ref, da_ref, dup_ref, dac_ref):
        acv = ac_ref[...].astype(F32)
        da = da_ref[...].astype(F32)
        s = _sigmoid(acv)
        dup_ref[...] = (da * acv * s).astype(BF16)
        dac_ref[...] = (da * b_ref[...].astype(F32) * s * (1.0 + acv * (1.0 - s))).astype(BF16)

    spec = pl.BlockSpec((TF, FCB), lambda i, j: (i, j))
    bspec = pl.BlockSpec((TF, FCB), lambda i, j: (i, 2 + j))
    return pl.pallas_call(
        body, name="ffn_act_bwd", grid=(nt, 2),
        in_specs=[bspec, spec, spec],
        out_specs=[bspec, spec],
        out_shape=[jax.ShapeDtypeStruct((t, 2 * DFF), BF16), jax.ShapeDtypeStruct((t, DFF), BF16)],
        compiler_params=_cparams(("parallel", "parallel")),
    )(up, ac, dact)


def ffn_conv_bwd(dup, up, dac, wc):
    t = up.shape[0]
    nt = t // TF
    prev_g, next_g = _ffn_halo(lambda j: j, t)

    strip = 16

    def body(dup_in_ref, a_ref, g_ref, gp_ref, gn_ref, w_ref, o_ref, dw_ref, lv_ref, cv_ref, rv_ref, part_ref):
        del dup_in_ref
        i = pl.program_id(1)

        @pl.when(i == 0)
        def _():
            dw_ref[...] = jnp.zeros_like(dw_ref)

        ext = _ext(g_ref, gp_ref, gn_ref, i, nt)
        n = ext.shape[0]
        colid = lax.broadcasted_iota(jnp.int32, (n, 1), 0) % GW
        lv_ref[...] = pltpu.roll(jnp.where(colid != GW - 1, ext, 0.0), 1, 0)
        cv_ref[...] = ext
        rv_ref[...] = pltpu.roll(jnp.where(colid != 0, ext, 0.0), n - 1, 0)
        part_ref[...] = jnp.zeros_like(part_ref)
        views = (lv_ref, cv_ref, rv_ref)

        def one_strip(r, carry):
            r0 = pl.multiple_of(r * strip, strip)
            a = a_ref[pl.ds(r0, strip), :].astype(F32)
            acc = jnp.zeros((strip, FCB), F32)
            for ky in range(3):
                for kx in range(3):
                    kf = 3 * (2 - ky) + (2 - kx)
                    tap = views[kx][pl.ds(r0 + GW * ky, strip), :]
                    acc = acc + w_ref[kf:kf + 1, :] * tap
                    p = a * tap
                    part_ref[8 * kf:8 * kf + 8, :] += p[0:8] + p[8:16]
            o_ref[pl.ds(r0, strip), :] = acc.astype(BF16)
            return carry

        lax.fori_loop(0, TF // strip, one_strip, 0)
        for k in range(9):
            dw_ref[k:k + 1, :] += jnp.sum(part_ref[8 * k:8 * k + 8, :], axis=0, keepdims=True)

    sw = lambda s: pl.BlockSpec(s.block_shape, lambda j, i, f=s.index_map: f(i, j))
    spec = pl.BlockSpec((TF, FCB), lambda j, i: (i, j))
    return pl.pallas_call(
        body, name="ffn_conv_bwd", grid=(2, nt),
        in_specs=[pl.BlockSpec(memory_space=pl.ANY), spec, spec, sw(prev_g), sw(next_g),
                  pl.BlockSpec((16, FCB), lambda j, i: (0, j))],
        out_specs=[spec, pl.BlockSpec((16, FCB), lambda j, i: (0, j))],
        out_shape=[jax.ShapeDtypeStruct(dup.shape, BF16), jax.ShapeDtypeStruct((16, DFF), F32)],
        scratch_shapes=[pltpu.VMEM((TF + 2 * GW, FCB), F32)] * 3 + [pltpu.VMEM((72, FCB), F32)],
        input_output_aliases={0: 0},
        compiler_params=_cparams(("arbitrary", "arbitrary")),
    )(dup, up, dac, dac, dac, wc)


def head_fwd_bwd(h1, f, target, gfin, tab):
    t = h1.shape[0]

    def body(h1_ref, f_ref, t_ref, g_ref, tab_ref, dh2_ref, df_ref, acc_ref):
        i = pl.program_id(0)

        @pl.when(i == 0)
        def _():
            acc_ref[...] = jnp.zeros_like(acc_ref)

        gate = tab_ref[0:1, :]
        fv = f_ref[...].astype(F32)
        h2 = h1_ref[...] + gate * fv
        r = lax.rsqrt(jnp.mean(h2 * h2, axis=-1, keepdims=True) + EPS)
        xh = h2 * r
        gv = g_ref[...]
        err = xh * gv - t_ref[...]
        acc_ref[0:1, :] += jnp.sum(0.5 * jnp.mean(err * err, axis=-1, keepdims=True), axis=0, keepdims=True)
        dy = err * (1.0 / D)
        acc_ref[1:2, :] += jnp.sum(dy * xh, axis=0, keepdims=True)
        dxh = dy * gv
        dh2 = r * (dxh - xh * jnp.mean(dxh * xh, axis=-1, keepdims=True))
        dh2_ref[...] = dh2
        acc_ref[2:3, :] += jnp.sum(dh2 * fv, axis=0, keepdims=True)
        df_ref[...] = (dh2 * gate).astype(BF16)

    return pl.pallas_call(
        body, name="head_fwd_bwd", grid=(t // TR2,),
        in_specs=[_rb(TR2, D), _rb(TR2, D), _rb(TR2, D), _bc(1, D), _bc(8, D)],
        out_specs=[_rb(TR2, D), _rb(TR2, D), _bc(8, D)],
        out_shape=[jax.ShapeDtypeStruct((t, D), F32), jax.ShapeDtypeStruct((t, D), BF16),
                   jax.ShapeDtypeStruct((8, D), F32)],
        compiler_params=_cparams(("arbitrary",)),
    )(h1, f, target, gfin, tab)


def norm2_bwd(h1, dhn2, dh2, out, g, tab):
    t = h1.shape[0]

    def body(h1_ref, dhn_ref, dh2_ref, out_ref, g_ref, tab_ref, dh1_ref, dout_ref, acc_ref):
        i = pl.program_id(0)

        @pl.when(i == 0)
        def _():
            acc_ref[...] = jnp.zeros_like(acc_ref)

        h1v = h1_ref[...]
        r = lax.rsqrt(jnp.mean(h1v * h1v, axis=-1, keepdims=True) + EPS)
        xh = h1v * r
        gv = g_ref[...]
        dhn = dhn_ref[...].astype(F32)
        acc_ref[0:1, :] += jnp.sum(dhn, axis=0, keepdims=True)
        acc_ref[1:2, :] += jnp.sum(dhn * xh * gv, axis=0, keepdims=True)
        dn = dhn * (1.0 + tab_ref[2:3, :])
        acc_ref[2:3, :] += jnp.sum(dn * xh, axis=0, keepdims=True)
        dxh = dn * gv
        dh1 = dh2_ref[...] + r * (dxh - xh * jnp.mean(dxh * xh, axis=-1, keepdims=True))
        dh1_ref[...] = dh1
        acc_ref[3:4, :] += jnp.sum(dh1 * out_ref[...].astype(F32), axis=0, keepdims=True)
        dout_ref[...] = (dh1 * tab_ref[0:1, :]).astype(BF16)

    return pl.pallas_call(
        body, name="norm2_bwd", grid=(t // TR2,),
        in_specs=[_rb(TR2, D), _rb(TR2, D), _rb(TR2, D), _rb(TR2, D), _bc(1, D), _bc(8, D)],
        out_specs=[_rb(TR2, D), _rb(TR2, D), _bc(8, D)],
        out_shape=[jax.ShapeDtypeStruct((t, D), F32), jax.ShapeDtypeStruct((t, D), BF16),
                   jax.ShapeDtypeStruct((8, D), F32)],
        compiler_params=_cparams(("arbitrary",)),
    )(h1, dhn2, dh2, out, g, tab)


def norm1_bwd(ctx, x, da, db, dh1, g, tab, n_ctx_tiles):
    t = x.shape[0]
    tp = t + ctx.shape[0]
    xrow = lambda i: jnp.maximum(i - n_ctx_tiles, 0)

    def body(c_ref, x_ref, da_ref, db_ref, dh1_ref, g_ref, tab_ref, dx_ref, acc_ref):
        i = pl.program_id(0)

        @pl.when(i == 0)
        def _():
            acc_ref[...] = jnp.zeros_like(acc_ref)

        is_ctx = i < n_ctx_tiles
        x = jnp.where(is_ctx, c_ref[...], x_ref[...])
        r = lax.rsqrt(jnp.mean(x * x, axis=-1, keepdims=True) + EPS)
        xh = x * r
        gv = g_ref[...]
        dhn = da_ref[...].astype(F32) + db_ref[...].astype(F32)
        s_shift = jnp.sum(dhn, axis=0, keepdims=True)
        s_scale = jnp.sum(dhn * xh * gv, axis=0, keepdims=True)

        @pl.when(is_ctx)
        def _():
            acc_ref[0:1, :] += s_shift
            acc_ref[1:2, :] += s_scale

        @pl.when(jnp.logical_not(is_ctx))
        def _():
            acc_ref[2:3, :] += s_shift
            acc_ref[3:4, :] += s_scale

        acc_ref[5:6, :] += s_shift
        acc_ref[6:7, :] += s_scale
        sc = jnp.where(is_ctx, tab_ref[1:2, :], tab_ref[3:4, :])
        dn = dhn * (1.0 + sc)
        acc_ref[4:5, :] += jnp.sum(dn * xh, axis=0, keepdims=True)
        dxh = dn * gv
        dx_ref[...] = dh1_ref[...] + r * (dxh - xh * jnp.mean(dxh * xh, axis=-1, keepdims=True))

    return pl.pallas_call(
        body, name="norm1_bwd", grid=(tp // TR,),
        in_specs=_ctx_x_specs(n_ctx_tiles) + [_rb(TR, D), _rb(TR, D), pl.BlockSpec((TR, D), lambda i: (xrow(i), 0)),
                                              _bc(1, D), _bc(8, D)],
        out_specs=[pl.BlockSpec((TR, D), lambda i: (xrow(i), 0)), _bc(8, D)],
        out_shape=[jax.ShapeDtypeStruct((t, D), F32), jax.ShapeDtypeStruct((8, D), F32)],
        compiler_params=_cparams(("arbitrary",)),
    )(ctx, x, da, db, dh1, g, tab)


def adamw(w, g, m, v, name):
    lead = w.ndim - 2
    rows, cols = w.shape[-2:]
    tm = max(t for t in range(8, rows + 1, 8) if rows % t == 0 and (t * cols <= 512 * 1024 or t == 8))
    c1 = 1.0 / (1.0 - ADAM_B1 ** ADAM_STEP)
    c2 = 1.0 / (1.0 - ADAM_B2 ** ADAM_STEP)

    def body(w_ref, g_ref, m_ref, v_ref, d_ref, mo_ref, vo_ref):
        gv = g_ref[...]
        mn = ADAM_B1 * m_ref[...] + (1.0 - ADAM_B1) * gv
        vn = ADAM_B2 * v_ref[...] + (1.0 - ADAM_B2) * (gv * gv)
        mo_ref[...] = mn
        vo_ref[...] = vn
        d_ref[...] = -ADAM_LR * ((mn * c1) / (jnp.sqrt(vn * c2) + ADAM_EPS) + ADAM_WD * w_ref[...])

    spec = pl.BlockSpec((1,) * lead + (tm, cols), lambda i: (0,) * lead + (i, 0))
    sds = jax.ShapeDtypeStruct(w.shape, F32)
    return pl.pallas_call(
        body, name=name, grid=(rows // tm,),
        in_specs=[spec] * 4, out_specs=[spec] * 3, out_shape=[sds] * 3,
        compiler_params=_cparams(("parallel",)),
    )(w, g, m, v)


def add_n(arrs, out_dtype, name):
    shp = arrs[0].shape
    cols = shp[-1]
    flat = [a.reshape(-1, cols) for a in arrs]
    rows = flat[0].shape[0]
    tm = max(t for t in range(16, rows + 1, 16) if rows % t == 0 and t * cols <= 512 * 1024)

    def body(*refs):
        acc = refs[0][...].astype(F32)
        for r in refs[1:-1]:
            acc = acc + r[...].astype(F32)
        refs[-1][...] = acc.astype(refs[-1].dtype)

    spec = pl.BlockSpec((tm, cols), lambda i: (i, 0))
    out = pl.pallas_call(
        body, name=name, grid=(rows // tm,),
        in_specs=[spec] * len(flat), out_specs=spec, out_shape=jax.ShapeDtypeStruct((rows, cols), out_dtype),
        compiler_params=_cparams(("parallel",)),
    )(*flat)
    return out.reshape(shp)


def sum8(stack, name):
    _, rows, cols = stack.shape
    tm = rows if rows <= 2048 else _pick(rows, (512, 256, 128, 64, 8))

    def body(s_ref, o_ref):
        acc = s_ref[0]
        for k in range(1, 8):
            acc = acc + s_ref[k]
        o_ref[...] = acc

    return pl.pallas_call(
        body, name=name, grid=(rows // tm,),
        in_specs=[pl.BlockSpec((8, tm, cols), lambda i: (0, i, 0))],
        out_specs=pl.BlockSpec((tm, cols), lambda i: (i, 0)),
        out_shape=jax.ShapeDtypeStruct((rows, cols), F32),
        compiler_params=_cparams(("parallel",)),
    )(stack)


def _coords():
    return lax.axis_index("x"), lax.axis_index("y"), lax.axis_index("c")


def _other_chips(x, y):
    return [(1 - x, y), (x, 1 - y), (1 - x, 1 - y)]


_ANY = pl.BlockSpec(memory_space=pl.ANY)


def gather_chips(slabs):
    ns = len(slabs)

    def body(*refs):
        x_refs, out_refs = refs[:ns], refs[ns:2 * ns]
        send_sems, recv_sems = refs[2 * ns:]
        x, y, c = _coords()
        me = 2 * x + y
        sibling = (x, y, 1 - c)
        chips = _other_chips(x, y)

        def half(s, chip, hc):
            rh = slabs[s].shape[0] // 2
            return out_refs[s].at[chip, pl.ds(hc * rh, rh), :]

        def own_half(s):
            rh = slabs[s].shape[0] // 2
            return x_refs[s].at[pl.ds(c * rh, rh), :]

        sends = []
        for s in range(ns):
            for j, (px, py) in enumerate(chips):
                cp = pltpu.make_async_remote_copy(
                    src_ref=own_half(s), dst_ref=half(s, me, c), send_sem=send_sems.at[6 * s + j],
                    recv_sem=recv_sems.at[6 * s + j], device_id=(px, py, c), device_id_type=MESH)
                cp.start()
                sends.append(cp)
        for s in range(ns):
            for j, (px, py) in enumerate(chips):
                src = 2 * px + py
                landed = pltpu.make_async_remote_copy(
                    src_ref=half(s, src, c), dst_ref=half(s, src, c), send_sem=send_sems.at[6 * s + j],
                    recv_sem=recv_sems.at[6 * s + j], device_id=(px, py, c), device_id_type=MESH)
                landed.wait_recv()
                fw = pltpu.make_async_remote_copy(
                    src_ref=half(s, src, c), dst_ref=half(s, src, c), send_sem=send_sems.at[6 * s + 3 + j],
                    recv_sem=recv_sems.at[6 * s + 3 + j], device_id=sibling, device_id_type=MESH)
                fw.start()
                sends.append(fw)
        for s in range(ns):
            for j, (px, py) in enumerate(chips):
                src = 2 * px + py
                got = pltpu.make_async_remote_copy(
                    src_ref=half(s, src, 1 - c), dst_ref=half(s, src, 1 - c), send_sem=send_sems.at[6 * s + 3 + j],
                    recv_sem=recv_sems.at[6 * s + 3 + j], device_id=sibling, device_id_type=MESH)
                got.wait_recv()
        for cp in sends:
            cp.wait_send()

    return pl.pallas_call(
        body, name="gather_chips",
        in_specs=[_ANY] * ns, out_specs=[_ANY] * ns,
        out_shape=[jax.ShapeDtypeStruct((4,) + s.shape, s.dtype) for s in slabs],
        scratch_shapes=[pltpu.SemaphoreType.DMA((6 * ns,)), pltpu.SemaphoreType.DMA((6 * ns,))],
    )(*slabs)


def swap_halves(gss):
    ns = len(gss)

    def body(*refs):
        g_refs, out_refs = refs[:ns], refs[ns:2 * ns]
        send_sems, recv_sems = refs[2 * ns:]
        x, y, c = _coords()
        cps = []
        for s in range(ns):
            rh = gss[s].shape[1] // 2
            cp = pltpu.make_async_remote_copy(
                src_ref=g_refs[s].at[:, pl.ds((1 - c) * rh, rh), :], dst_ref=out_refs[s],
                send_sem=send_sems.at[s], recv_sem=recv_sems.at[s], device_id=(x, y, 1 - c), device_id_type=MESH)
            cp.start()
            cps.append(cp)
        for cp in cps:
            cp.wait()

    return pl.pallas_call(
        body, name="swap_halves",
        in_specs=[_ANY] * ns, out_specs=[_ANY] * ns,
        out_shape=[jax.ShapeDtypeStruct((4, g.shape[1] // 2, g.shape[2]), g.dtype) for g in gss],
        scratch_shapes=[pltpu.SemaphoreType.DMA((ns,)), pltpu.SemaphoreType.DMA((ns,))],
    )(*gss)


def join_halves(reds):
    ns = len(reds)

    def body(*refs):
        r_refs, out_refs = refs[:ns], refs[ns:2 * ns]
        send_sems, recv_sems = refs[2 * ns:]
        x, y, c = _coords()
        cps = []
        for s in range(ns):
            cp = pltpu.make_async_remote_copy(
                src_ref=r_refs[s], dst_ref=out_refs[s], send_sem=send_sems.at[s], recv_sem=recv_sems.at[s],
                device_id=(x, y, 1 - c), device_id_type=MESH)
            cp.start()
            cps.append(cp)
        for cp in cps:
            cp.wait()

    return pl.pallas_call(
        body, name="join_halves",
        in_specs=[_ANY] * ns, out_specs=[_ANY] * ns,
        out_shape=[jax.ShapeDtypeStruct(r.shape, r.dtype) for r in reds],
        scratch_shapes=[pltpu.SemaphoreType.DMA((ns,)), pltpu.SemaphoreType.DMA((ns,))],
    )(*reds)


def gather_all(vec, name):
    r, wd = vec.shape

    def body(v_ref, out_ref, send_sems, recv_sems):
        x, y, c = _coords()
        me = 4 * x + 2 * y + c
        cps = []
        for k in range(1, 8):
            mx, my, mc = (k >> 2) & 1, (k >> 1) & 1, k & 1
            peer = (x ^ mx, y ^ my, c ^ mc)
            cp = pltpu.make_async_remote_copy(
                src_ref=v_ref, dst_ref=out_ref.at[me],
                send_sem=send_sems.at[k - 1], recv_sem=recv_sems.at[k - 1], device_id=peer, device_id_type=MESH)
            cp.start()
            cps.append(cp)
        for k in range(1, 8):
            mx, my, mc = (k >> 2) & 1, (k >> 1) & 1, k & 1
            peer = (x ^ mx, y ^ my, c ^ mc)
            src = 4 * peer[0] + 2 * peer[1] + peer[2]
            got = pltpu.make_async_remote_copy(
                src_ref=v_ref, dst_ref=out_ref.at[src],
                send_sem=send_sems.at[k - 1], recv_sem=recv_sems.at[k - 1], device_id=peer, device_id_type=MESH)
            got.wait_recv()
        for cp in cps:
            cp.wait_send()

    return pl.pallas_call(
        body, name=name,
        in_specs=[_ANY], out_specs=_ANY,
        out_shape=jax.ShapeDtypeStruct((8, r, wd), vec.dtype),
        scratch_shapes=[pltpu.SemaphoreType.DMA((7,)), pltpu.SemaphoreType.DMA((7,))],
    )(vec)


def _pad_rows(a, rows):
    return jnp.pad(a, ((0, rows - a.shape[0]), (0, 0)))


def _pad_cols(a, cols):
    return jnp.pad(a, ((0, 0), (0, cols - a.shape[1])))


def local_step(x, c, ctx, c_ctx, target, wt, sm, late_weights=None, grad_hook=None):
    t, tc = x.shape[0], ctx.shape[0]
    tp = t + tc
    nct = tc // TR
    ncc = tc // LC
    nc = tp // LC

    w_in = wt["w_in"]
    segs = {"q": (0, D), "k": (D, 2 * D), "v": (2 * D, 3 * D), "g": (3 * D, 3 * D + NGATE)}
    base = 3 * D + NGATE
    for n_i, nm in enumerate(("o", "u", "vg", "gm", "gg")):
        segs[nm] = (base + n_i * D, base + (n_i + 1) * D)
    order = ("o", "u", "vg", "gm", "gg", "v", "q", "k")
    w_main = jnp.concatenate([w_in[:, segs[nm][0]:segs[nm][1]] for nm in order], axis=1)
    w_g = _pad_cols(w_in[:, segs["g"][0]:segs["g"][1]], LANE)
    w_main_t = w_main.T
    w_g_t = w_g.T

    cc = _pad_rows(jnp.concatenate([c.reshape(1, D), c_ctx.reshape(1, D)], axis=0), 16)
    modv = mod_fwd(cc, wt["w_mod"][0], wt["w_mod"][1], sm["b_mod"].reshape(1, NMOD * D))
    mx = modv[0].reshape(NMOD, D)
    mc = modv[1].reshape(NMOD, D)
    tab1 = _pad_rows(jnp.stack([mc[0], mc[1], mx[0], mx[1]]), 8)
    tab2 = _pad_rows(jnp.stack([mx[2], mx[3], mx[4]]), 8)
    tab3 = _pad_rows(mx[5:6], 8)

    g1 = sm["norm1_g"].reshape(1, D)
    g2 = sm["norm2_g"].reshape(1, D)
    gfin = sm["final_g"].reshape(1, D)
    gh = sm["head_norm_g"].reshape(1, D)
    ln_g = sm["sgu_ln_g"].reshape(1, D)
    ln_b = sm["sgu_ln_b"].reshape(1, D)
    ws = sm["w_s"].reshape(SG * SCH, SCH).astype(BF16)
    bs_t = _pad_cols(sm["b_s"].reshape(SG, SCH).T, LANE)
    conv_w = _pad_rows(sm["conv_qk"].reshape(3, 2 * D), 8)
    b_gate = _pad_cols(sm["b_gate"].reshape(1, NGATE), LANE)
    wc = _pad_rows(sm["w_ffn_conv"].reshape(9, DFF), 16)

    hn1 = norm1_fwd(ctx, x, g1, tab1, nct)
    if late_weights is None:
        zmain = mm_nn(hn1, w_main, BF16, "mm_zmain")
    else:
        zmain, landed = mm_nn(hn1, w_main, BF16, "mm_zmain", rider=late_weights[0])
        wt = {**wt, **late_weights[1](landed)}
    zg = mm_nn(hn1, w_g, F32, "mm_zg")
    qa, ka, gcol, zgb = qkconv_fwd(zmain, zg, conv_w, b_gate, nct)
    grow = gcol[:, :16].reshape(nc, LC, 16).transpose(0, 2, 1)
    hf, hb, csf, csb, nsf, nsb, msf, msb = mlstm_fwd(qa, ka, zmain, gcol, grow, ncc)
    yms = mixer_fwd(hf, hb, zmain, gh, ln_g, ln_b, ws, bs_t, nct)
    w_br = jnp.stack([wt["w_branch_mlstm"], wt["w_branch_sgu"]])
    pp = mm_nn(yms, w_br, BF16, "mm_branch")
    y = merge_fwd(zmain, pp, nct)
    out = mm_nn(y, wt["w_out"], BF16, "mm_out")
    h1, hn2 = norm2_fwd(x, out, g2, tab2)
    up = mm_nn(hn2, wt["w_up"], BF16, "mm_up")
    act, ac = ffn_act_fwd(up, wc)
    f = mm_nn(act, wt["w_down"], BF16, "mm_down")
    dh2, df, acc_h = head_fwd_bwd(h1, f, target, gfin, tab3)
    loss = acc_h[0, 0]

    g_w_down = mm_tn(act, df, "mmt_down", BF16)
    dact = mm_nn(df, wt["w_down"].T, BF16, "mm_ddown")
    dup, dac = ffn_act_bwd(up, ac, dact)
    dup, g_wc = ffn_conv_bwd(dup, up, dac, wc)
    g_w_up = mm_tn(hn2, dup, "mmt_up", BF16)
    dhn2 = mm_nn(dup, wt["w_up"].T, BF16, "mm_dup")
    dh1, dout, acc_2 = norm2_bwd(h1, dhn2, dh2, out, g2, tab2)
    g_w_out = mm_tn(y, dout, "mmt_out", BF16)
    dy = mm_nn(dout, wt["w_out"].T, BF16, "mm_dout")
    dpp, dz = merge_bwd(zmain, pp, dy, tp, nct)
    g_w_br = mm_tn(yms, dpp, "mmt_branch", BF16)
    dyms = mm_nn(dpp, jnp.stack([wt["w_branch_mlstm"].T, wt["w_branch_sgu"].T]), BF16, "mm_dbranch")
    dz, dhm, g_gh, g_lng, g_lnb, g_ws, g_bs = mixer_bwd(dz, hf, hb, zmain, dyms, gh, ln_g, ln_b, ws, bs_t, nct)
    (dqf, dkf, dvf, colf, rowf, dqb, dkb, dvb, colb, rowb) = mlstm_bwd(
        qa, ka, zmain, gcol, grow, (csf, csb, nsf, nsb, msf, msb), dhm, ncc)

    csum_f = rowf[:, :4, :].transpose(0, 2, 1).reshape(tp, 4)
    csum_b = rowb[:, :4, :].transpose(0, 2, 1).reshape(tp, 4)
    cs = _pad_cols(jnp.concatenate([csum_f, csum_f, csum_b, csum_b], axis=1), LANE)
    dzg, g_bgate = gates_bwd(colf, colb, cs, zgb)

    dc, g_convw = qkconv_bwd_a(zmain, dqf, dqb, dkf, dkb, conv_w, nct)
    dz = qkconv_bwd_b(dz, dc, conv_w, nct)
    dz = add_into_dz(dz, dvf, dvb, CB_V)

    g_w_main = mm_tn(hn1, dz, "mmt_main", BF16)
    g_w_g = mm_tn(hn1, dzg, "mmt_g", BF16)
    blk = lambda cb: g_w_main[:, cb * D:(cb + 1) * D]
    g_w_in = jnp.concatenate([blk(CB_Q), blk(CB_K), blk(CB_V), g_w_g[:, :NGATE], blk(CB_O), blk(CB_U), blk(CB_VG),
                              blk(CB_GM), blk(CB_GG)], axis=1)
    big = {"w_in": g_w_in, "w_branch_mlstm": g_w_br[0], "w_branch_sgu": g_w_br[1], "w_out": g_w_out,
           "w_up": g_w_up, "w_down": g_w_down}
    if grad_hook is None:
        da, received = mm_nn(dz, w_main_t, BF16, "mm_dmain"), None
    else:
        da, received = mm_nn(dz, w_main_t, BF16, "mm_dmain", rider=grad_hook(big))
    db = mm_nn(dzg, w_g_t, BF16, "mm_dg")
    grad_x, acc_1 = norm1_bwd(ctx, x, da, db, dh1, g1, tab1, nct)

    d_modx = jnp.concatenate([acc_1[2], acc_1[3], acc_2[3], acc_2[0], acc_2[1], acc_h[2]])
    d_modc = jnp.concatenate([acc_1[0], acc_1[1], jnp.zeros((4 * D,), F32)])
    d_modb = jnp.concatenate([acc_1[5], acc_1[6], acc_2[3], acc_2[0], acc_2[1], acc_h[2]])

    small = {"b_mod": d_modb, "norm1_g": acc_1[4], "b_gate": g_bgate[0, :NGATE], "conv_qk": g_convw[:3].reshape(-1),
             "head_norm_g": g_gh[0], "sgu_ln_g": g_lng[0], "sgu_ln_b": g_lnb[0], "w_s": g_ws.reshape(-1),
             "b_s": g_bs[:, :SG].T.reshape(-1), "norm2_g": acc_2[2], "w_ffn_conv": g_wc[:9].reshape(-1),
             "final_g": acc_h[1]}
    return loss, grad_x, big, small, d_modx, d_modc, received


def mod_bwd_w(a_all, dm_all, name):
    n = dm_all.shape[1]
    tn = _pick(n, (512, 128))

    def body(a_ref, d_ref, o_ref):
        o_ref[...] = _dot_tn(_silu(a_ref[...]), d_ref[...])

    return pl.pallas_call(
        body, name=name, grid=(n // tn,),
        in_specs=[_bc(16, D), pl.BlockSpec((16, tn), lambda j: (0, j))],
        out_specs=pl.BlockSpec((D, tn), lambda j: (0, j)),
        out_shape=jax.ShapeDtypeStruct((D, n), F32),
        compiler_params=_cparams(("parallel",)),
    )(a_all, dm_all)


def mod_bwd_cctx(dmc, slabs, w_cols, c_ctx):
    per = w_cols // MODB
    nb = 2 * D // MODB

    def body(d_ref, w_ref, c_ref, o_ref):
        j = pl.program_id(0)

        @pl.when(j == 0)
        def _():
            o_ref[...] = jnp.zeros_like(o_ref)

        o_ref[...] += _dot_nt(d_ref[...], w_ref[0])

        @pl.when(j == nb - 1)
        def _():
            o_ref[...] = o_ref[...] * _dsilu(c_ref[...])

    return pl.pallas_call(
        body, name="mod_bwd_cctx", grid=(nb,),
        in_specs=[pl.BlockSpec((16, MODB), lambda j: (0, j)),
                  pl.BlockSpec((1, D, MODB), lambda j: (j // per, 0, j % per)), _bc(1, D)],
        out_specs=_bc(16, D),
        out_shape=jax.ShapeDtypeStruct((16, D), F32),
        compiler_params=_cparams(("arbitrary",)),
    )(dmc, slabs, c_ctx)


BIG = ("w_mod", "w_in", "w_branch_mlstm", "w_branch_sgu", "w_out", "w_up", "w_down")
BIG_AXIS = {"w_mod": 1, "w_in": 1, "w_branch_mlstm": 0, "w_branch_sgu": 0, "w_out": 0, "w_up": 1, "w_down": 0}
SMALL = ("c_ctx", "b_mod", "norm1_g", "b_gate", "conv_qk", "head_norm_g", "sgu_ln_g", "sgu_ln_b", "w_s", "b_s",
         "norm2_g", "w_ffn_conv", "final_g")
SMALL_SHARDED = {"conv_qk": (3, 2 * D), "w_ffn_conv": (9, DFF)}
PACK_ALIGN = 32 * D


def _pack(arrs, dtype, align=PACK_ALIGN, width=D):
    flat = jnp.concatenate([a.reshape(-1).astype(dtype) for a in arrs])
    n = flat.shape[0]
    padded = -(-n // align) * align
    return jnp.pad(flat, (0, padded - n)).reshape(padded // width, width)


def _unpack(slab, shapes):
    flat = slab.reshape(-1)
    outs, off = [], 0
    for shp in shapes:
        n = math.prod(shp)
        outs.append(flat[off:off + n].reshape(shp))
        off += n
    return outs


def _round_up(n, m):
    return -(-n // m) * m


def kernel(x, c, ctx, c_ctx, w_mod, b_mod, norm1_g, w_in, b_gate, conv_qk, head_norm_g, sgu_ln_g, sgu_ln_b, w_s, b_s, w_branch_mlstm, w_branch_sgu, w_out, norm2_g, w_up, w_ffn_conv, w_down, final_g, loss_target, m_c_ctx, m_w_mod, m_b_mod, m_norm1_g, m_w_in, m_b_gate, m_conv_qk, m_head_norm_g, m_sgu_ln_g, m_sgu_ln_b, m_w_s, m_b_s, m_w_branch_mlstm, m_w_branch_sgu, m_w_out, m_norm2_g, m_w_up, m_w_ffn_conv, m_w_down, m_final_g, v_c_ctx, v_w_mod, v_b_mod, v_norm1_g, v_w_in, v_b_gate, v_conv_qk, v_head_norm_g, v_sgu_ln_g, v_sgu_ln_b, v_w_s, v_b_s, v_w_branch_mlstm, v_w_branch_sgu, v_w_out, v_norm2_g, v_w_up, v_w_ffn_conv, v_w_down, v_final_g):
    params = dict(c_ctx=c_ctx, w_mod=w_mod, b_mod=b_mod, norm1_g=norm1_g, w_in=w_in, b_gate=b_gate, conv_qk=conv_qk,
                  head_norm_g=head_norm_g, sgu_ln_g=sgu_ln_g, sgu_ln_b=sgu_ln_b, w_s=w_s, b_s=b_s,
                  w_branch_mlstm=w_branch_mlstm, w_branch_sgu=w_branch_sgu, w_out=w_out, norm2_g=norm2_g, w_up=w_up,
                  w_ffn_conv=w_ffn_conv, w_down=w_down, final_g=final_g)
    mom_m = dict(c_ctx=m_c_ctx, w_mod=m_w_mod, b_mod=m_b_mod, norm1_g=m_norm1_g, w_in=m_w_in, b_gate=m_b_gate,
                 conv_qk=m_conv_qk, head_norm_g=m_head_norm_g, sgu_ln_g=m_sgu_ln_g, sgu_ln_b=m_sgu_ln_b, w_s=m_w_s,
                 b_s=m_b_s, w_branch_mlstm=m_w_branch_mlstm, w_branch_sgu=m_w_branch_sgu, w_out=m_w_out,
                 norm2_g=m_norm2_g, w_up=m_w_up, w_ffn_conv=m_w_ffn_conv, w_down=m_w_down, final_g=m_final_g)
    mom_v = dict(c_ctx=v_c_ctx, w_mod=v_w_mod, b_mod=v_b_mod, norm1_g=v_norm1_g, w_in=v_w_in, b_gate=v_b_gate,
                 conv_qk=v_conv_qk, head_norm_g=v_head_norm_g, sgu_ln_g=v_sgu_ln_g, sgu_ln_b=v_sgu_ln_b, w_s=v_w_s,
                 b_s=v_b_s, w_branch_mlstm=v_w_branch_mlstm, w_branch_sgu=v_w_branch_sgu, w_out=v_w_out,
                 norm2_g=v_norm2_g, w_up=v_w_up, w_ffn_conv=v_w_ffn_conv, w_down=v_w_down, final_g=v_final_g)
    chip = 2 * lax.axis_index("x") + lax.axis_index("y")

    shard2d = {n: params[n].reshape(params[n].shape[-2:]) for n in BIG}
    conv_sh = conv_qk.reshape(3, -1)
    fconv_sh = w_ffn_conv.reshape(9, -1)

    dev = 2 * chip + lax.axis_index("c")

    first_names, row_names = ("w_mod", "w_in"), ("w_branch_mlstm", "w_branch_sgu", "w_out", "w_down")
    first_w = [shard2d[n].shape[1] for n in first_names]
    row_h = [shard2d[n].shape[0] for n in row_names]
    first_slab = _pad_cols(jnp.concatenate([shard2d[n].astype(BF16) for n in first_names], axis=1),
                           _round_up(sum(first_w), LANE))
    up_slab = shard2d["w_up"].astype(BF16)
    row_slab = jnp.concatenate([shard2d[n].astype(BF16) for n in row_names], axis=0)

    def own_in(slab, gathered):
        return jnp.where((jnp.arange(4) == chip)[:, None, None], slab[None], gathered)

    first_all = own_in(first_slab, gather_chips([first_slab])[0])
    wt = {"w_mod": (first_all, first_w[0]),
          "w_in": jnp.concatenate([first_all[j, :, first_w[0]:first_w[0] + first_w[1]] for j in range(4)], axis=1)}

    late_slabs = [up_slab, row_slab]

    def late_copies(in_refs, out_refs, send_sems, recv_sems):
        xx, yy, cc = _coords()
        me = 2 * xx + yy
        cps = []
        for s, slab in enumerate(late_slabs):
            rh = slab.shape[0] // 2
            for j, (px, py) in enumerate(_other_chips(xx, yy)):
                for o in range(2):
                    k = 6 * s + 2 * j + o
                    cps.append(pltpu.make_async_remote_copy(
                        src_ref=in_refs[s].at[pl.ds(cc * rh, rh), :], dst_ref=out_refs[s].at[me, pl.ds(cc * rh, rh), :],
                        send_sem=send_sems.at[k], recv_sem=recv_sems.at[k],
                        device_id=(px, py, cc if o == 0 else 1 - cc), device_id_type=MESH))
        return cps

    def late_finish(landed):
        up_all, row_all = own_in(up_slab, landed[0]), own_in(row_slab, landed[1])
        got = {"w_up": jnp.concatenate([up_all[j] for j in range(4)], axis=1)}
        o = 0
        for n, ht in zip(row_names, row_h):
            got[n] = jnp.concatenate([row_all[j, o:o + ht, :] for j in range(4)], axis=0)
            o += ht
        return got

    late_rider = Rider(late_slabs, [jax.ShapeDtypeStruct((4,) + s_.shape, s_.dtype) for s_ in late_slabs],
                       6 * len(late_slabs), late_copies)

    cvec = _pack([conv_sh, fconv_sh], F32, align=8 * LANE, width=LANE)
    call = gather_all(cvec, "gather_conv")
    cparts = [_unpack(jnp.where(dev == 2 * j, cvec, call[2 * j]), [conv_sh.shape, fconv_sh.shape]) for j in range(4)]
    conv_full = jnp.concatenate([p[0] for p in cparts], axis=1)
    fconv_full = jnp.concatenate([p[1] for p in cparts], axis=1)

    sm = dict(b_mod=b_mod, norm1_g=norm1_g, b_gate=b_gate, conv_qk=conv_full, head_norm_g=head_norm_g,
              sgu_ln_g=sgu_ln_g, sgu_ln_b=sgu_ln_b, w_s=w_s, b_s=b_s, norm2_g=norm2_g, w_ffn_conv=fconv_full,
              final_g=final_g)

    gcol_names = ("w_up", "w_in")
    gcol_w = [shard2d[n].shape[1] for n in gcol_names]
    gcol_pad = _round_up(sum(gcol_w), LANE)
    cidx = lax.axis_index("c")
    kept = {}

    def grad_hook(gbig):
        def chip_cols(j):
            return _pad_cols(jnp.concatenate([gbig[n][:, j * wd:(j + 1) * wd] for n, wd in zip(gcol_names, gcol_w)],
                                             axis=1), gcol_pad)

        def chip_rows(j):
            return jnp.concatenate([gbig[n][j * ht:(j + 1) * ht] for n, ht in zip(row_names, row_h)], axis=0)

        gss = [jnp.stack([chip_cols(j) for j in range(4)]), jnp.stack([chip_rows(j) for j in range(4)])]
        from_sib = swap_halves(gss)
        pair_bf, own_terms = [], []
        for s, (gs, fs) in enumerate(zip(gss, from_sib)):
            rh = gs.shape[1] // 2
            my_half = lax.dynamic_slice_in_dim(gs, cidx * rh, rh, axis=1)
            pair_bf.append(add_n([my_half, fs], BF16, "pair_sum_%d" % s))
            own_terms.append([lax.dynamic_index_in_dim(my_half, chip, axis=0, keepdims=False),
                              lax.dynamic_index_in_dim(fs, chip, axis=0, keepdims=False)])
        kept["own_terms"] = own_terms

        def scatter_copies(in_refs, out_refs, send_sems, recv_sems):
            xx, yy, cc = _coords()
            cps = []
            for s in range(len(pair_bf)):
                for j, (px, py) in enumerate(_other_chips(xx, yy)):
                    cps.append(pltpu.make_async_remote_copy(
                        src_ref=in_refs[s].at[2 * px + py], dst_ref=out_refs[s].at[j],
                        send_sem=send_sems.at[3 * s + j], recv_sem=recv_sems.at[3 * s + j],
                        device_id=(px, py, cc), device_id_type=MESH))
            return cps

        return Rider(pair_bf, [jax.ShapeDtypeStruct((3,) + p.shape[1:], p.dtype) for p in pair_bf],
                     3 * len(pair_bf), scatter_copies)

    loss_l, grad_x, _, gsmall, d_modx, d_modc, recv = local_step(
        x[0], c, ctx[0], c_ctx, loss_target[0], wt, sm, late_weights=(late_rider, late_finish), grad_hook=grad_hook)

    reds = [add_n(kept["own_terms"][s] + [recv[s][0], recv[s][1], recv[s][2]], F32, "chip_sum_%d" % s)
            for s in range(2)]
    others = join_halves(reds)
    full_red = [jnp.where(cidx == 0, jnp.concatenate([m, o], axis=0), jnp.concatenate([o, m], axis=0))
                for m, o in zip(reds, others)]
    g_shard = {}
    off = 0
    for n, wd in zip(gcol_names, gcol_w):
        g_shard[n] = full_red[0][:, off:off + wd]
        off += wd
    off = 0
    for n, ht in zip(row_names, row_h):
        g_shard[n] = full_red[1][off:off + ht]
        off += ht

    small_order = ("b_mod", "norm1_g", "b_gate", "conv_qk", "head_norm_g", "sgu_ln_g", "sgu_ln_b", "w_s", "b_s", "norm2_g",
                   "w_ffn_conv", "final_g")
    vec_parts = [gsmall[n] for n in small_order] + [d_modx, d_modc, c.reshape(-1), loss_l.reshape(1)]
    vec_shapes = [a.shape for a in vec_parts]
    vec = _pack(vec_parts, F32, align=8 * LANE, width=LANE)
    allv = gather_all(vec, "gather_small")
    allv = jnp.where((jnp.arange(8) == dev)[:, None, None], vec[None], allv)
    summed = sum8(allv, "small_sum")
    s_parts = _unpack(summed, vec_shapes)
    g_small = dict(zip(small_order, s_parts[:len(small_order)]))
    dmc_sum = s_parts[len(small_order) + 1]
    loss = s_parts[-1][0]
    flat_all = allv.reshape(8, -1)
    starts = [0]
    for shp_ in vec_shapes:
        starts.append(starts[-1] + math.prod(shp_))
    i_dmx, i_c = len(small_order), len(small_order) + 2
    dmx_all = flat_all[:, starts[i_dmx]:starts[i_dmx + 1]]
    c_all = flat_all[:, starts[i_c]:starts[i_c + 1]]

    a_all = _pad_rows(jnp.concatenate([c_all, c_ctx.reshape(1, D)], axis=0), 16)
    dm_all = _pad_rows(jnp.concatenate([dmx_all, dmc_sum.reshape(1, NMOD * D)], axis=0), 16)
    ncol = NMOD * D // 4
    dm_shard = lax.dynamic_slice_in_dim(dm_all, chip * ncol, ncol, axis=1)
    g_shard["w_mod"] = mod_bwd_w(a_all, dm_shard, "mod_bwd_w")
    g_cctx = mod_bwd_cctx(_pad_rows(dmc_sum[:2 * D].reshape(1, 2 * D), 16), first_all, first_w[0],
                          c_ctx.reshape(1, D))[0]
    g_small["c_ctx"] = g_cctx

    results = {}
    for n in BIG:
        shp = params[n].shape
        g_ = g_shard[n].reshape(shp)
        d_, m_, v_ = adamw(params[n], g_, mom_m[n], mom_v[n], "adamw_" + n)
        results[n] = (g_, d_, m_, v_)

    conv_g = lax.dynamic_slice_in_dim(g_small["conv_qk"].reshape(3, 2 * D), chip * (2 * D // 4), 2 * D // 4, axis=1)
    fconv_g = lax.dynamic_slice_in_dim(g_small["w_ffn_conv"].reshape(9, DFF), chip * (DFF // 4), DFF // 4, axis=1)
    g_small["conv_qk"] = conv_g
    g_small["w_ffn_conv"] = fconv_g
    w_list = [params[n].reshape(-1) for n in SMALL]
    g_list = [g_small[n].reshape(-1) for n in SMALL]
    m_list = [mom_m[n].reshape(-1) for n in SMALL]
    v_list = [mom_v[n].reshape(-1) for n in SMALL]
    sm_shapes = [params[n].shape for n in SMALL]
    pk = lambda lst: _pack(lst, F32, align=8 * LANE, width=LANE)
    gp = pk(g_list)
    d_s, m_s, v_s = adamw(pk(w_list), gp, pk(m_list), pk(v_list), "adamw_small")
    for n, gg, dd, mm, vv in zip(SMALL, _unpack(gp, sm_shapes), _unpack(d_s, sm_shapes), _unpack(m_s, sm_shapes),
                                 _unpack(v_s, sm_shapes)):
        results[n] = (gg, dd, mm, vv)

    order = ("c_ctx", "w_mod", "b_mod", "norm1_g", "w_in", "b_gate", "conv_qk", "head_norm_g", "sgu_ln_g", "sgu_ln_b",
             "w_s", "b_s", "w_branch_mlstm", "w_branch_sgu", "w_out", "norm2_g", "w_up", "w_ffn_conv", "w_down",
             "final_g")
    outs = [loss, grad_x[None]]
    for k in range(4):
        outs += [results[n][k] for n in order]
    return tuple(outs)
```

```python
import functools
import math

import jax
import jax.numpy as jnp
from jax import lax
from jax.experimental import pallas as pl
from jax.experimental.pallas import tpu as pltpu

F32 = jnp.float32
BF16 = jnp.bfloat16

D = 1024
NH = 4
DH = 256
LC = 256
GW = 64
SG = 4
SGD = 256
SCH = 128
DFF = 2816
NMOD = 6
NGATE = 16
NIN = 8208
EPS = 1e-6
M_INIT = -1e30
TR = 256
TR2 = 512
LANE = 128
VMEM_LIMIT = 56 * 1024 * 1024
MESH = pl.DeviceIdType.MESH

ADAM_LR = 0.001
ADAM_B1 = 0.9
ADAM_B2 = 0.999
ADAM_EPS = 1e-08
ADAM_WD = 0.01
ADAM_STEP = 10

CB_O, CB_U, CB_VG, CB_GM, CB_GG, CB_V, CB_Q, CB_K = range(8)


def _pick(n, cands):
    for c in cands:
        if n % c == 0:
            return c
    return n


def _cparams(sem):
    return pltpu.CompilerParams(dimension_semantics=sem, vmem_limit_bytes=VMEM_LIMIT)


def _sigmoid(x):
    return 1.0 / (1.0 + jnp.exp(-x))


def _silu(x):
    return x * _sigmoid(x)


def _dsilu(x):
    s = _sigmoid(x)
    return s * (1.0 + x * (1.0 - s))


_GC = math.sqrt(2.0 / math.pi)


def _gelu(x):
    return 0.5 * x * (1.0 + jnp.tanh(_GC * (x + 0.044715 * x * x * x)))


def _gelu_and_grad(x):
    x2 = x * x
    t = jnp.tanh(_GC * x * (1.0 + 0.044715 * x2))
    half = 0.5 * (1.0 + t)
    return x * half, half + 0.5 * x * (1.0 - t * t) * _GC * (1.0 + 3.0 * 0.044715 * x2)


def _dot(a, b):
    return jnp.dot(a.astype(BF16), b.astype(BF16), preferred_element_type=F32)


def _dot_nt(a, b):
    return lax.dot_general(a.astype(BF16), b.astype(BF16), (((1,), (1,)), ((), ())), preferred_element_type=F32)


def _dot_tn(a, b):
    return lax.dot_general(a.astype(BF16), b.astype(BF16), (((0,), (0,)), ((), ())), preferred_element_type=F32)


def _dot_tn_mxu(a, b):
    m = a.shape[1]
    eye = (lax.broadcasted_iota(jnp.int32, (m, m), 0) == lax.broadcasted_iota(jnp.int32, (m, m), 1)).astype(BF16)
    return _dot(_dot_nt(eye, a), b)


def _exact_dot(tri, x):
    x1 = x.astype(BF16)
    r1 = x - x1.astype(F32)
    x2 = r1.astype(BF16)
    x3 = (r1 - x2.astype(F32)).astype(BF16)
    return (jnp.dot(tri, x1, preferred_element_type=F32) + jnp.dot(tri, x2, preferred_element_type=F32)
            + jnp.dot(tri, x3, preferred_element_type=F32))


def _rb(tm, w, col=0, off=0):
    return pl.BlockSpec((tm, w), lambda i: (i + off, col))


def _bc(r, w):
    return pl.BlockSpec((r, w), lambda i: (0, 0))


class Rider:
    def __init__(self, ins, out_shapes, n_sems, copies):
        self.ins, self.out_shapes, self.n_sems, self.copies = list(ins), list(out_shapes), n_sems, copies


def mm_nn(a, b, out_dtype, name, rider=None, b_transposed=False):
    squeeze = a.ndim == 2
    if squeeze:
        a, b = a[None], b[None]
    g, m, k = a.shape
    n = b.shape[1] if b_transposed else b.shape[2]
    dot = _dot_nt if b_transposed else _dot
    tm = _pick(m, (2048, 1280, 1024, 512, 256, 128))
    tn = _pick(n, (2048, 1408, 1024, 512, 128))
    tk = _pick(k, (2048, 1408, 1024, 512, 128))
    nk = k // tk
    grid = (g, n // tn, m // tm, nk)
    n_rin = len(rider.ins) if rider else 0
    n_rout = len(rider.out_shapes) if rider else 0

    def body(*refs):
        a_ref, b_ref = refs[0], refs[1]
        r_in = refs[2:2 + n_rin]
        o_ref = refs[2 + n_rin]
        r_out = refs[3 + n_rin:3 + n_rin + n_rout]
        scr = refs[3 + n_rin + n_rout:]
        if rider:
            ids = [pl.program_id(d) for d in range(4)]
            first = functools.reduce(jnp.logical_and, [i == 0 for i in ids])
            last = functools.reduce(jnp.logical_and, [i == e - 1 for i, e in zip(ids, grid)])
            send_sems, recv_sems = scr[-2], scr[-1]

            @pl.when(first)
            def _():
                for cp in rider.copies(r_in, r_out, send_sems, recv_sems):
                    cp.start()

        if nk == 1:
            o_ref[0] = dot(a_ref[0], b_ref[0]).astype(o_ref.dtype)
        else:
            acc_ref = scr[0]
            kk = pl.program_id(3)

            @pl.when(kk == 0)
            def _():
                acc_ref[...] = jnp.zeros_like(acc_ref)

            acc_ref[...] += dot(a_ref[0], b_ref[0])

            @pl.when(kk == nk - 1)
            def _():
                o_ref[0] = acc_ref[...].astype(o_ref.dtype)

        if rider:
            @pl.when(last)
            def _():
                for cp in rider.copies(r_in, r_out, send_sems, recv_sems):
                    cp.wait()

    scratch = [] if nk == 1 else [pltpu.VMEM((tm, tn), F32)]
    if rider:
        scratch += [pltpu.SemaphoreType.DMA((rider.n_sems,)), pltpu.SemaphoreType.DMA((rider.n_sems,))]
    outs = pl.pallas_call(
        body, name=name, grid=grid,
        in_specs=[pl.BlockSpec((1, tm, tk), lambda gi, j, i, kk: (gi, i, kk)),
                  (pl.BlockSpec((1, tn, tk), lambda gi, j, i, kk: (gi, j, kk)) if b_transposed else
                   pl.BlockSpec((1, tk, tn), lambda gi, j, i, kk: (gi, kk, j)))] + [_ANY] * n_rin,
        out_specs=[pl.BlockSpec((1, tm, tn), lambda gi, j, i, kk: (gi, i, j))] + [_ANY] * n_rout,
        out_shape=[jax.ShapeDtypeStruct((g, m, n), out_dtype)] + (rider.out_shapes if rider else []),
        scratch_shapes=scratch,
        compiler_params=_cparams(("arbitrary",) * 4 if rider else ("parallel", "parallel", "parallel", "arbitrary")),
    )(a, b, *(rider.ins if rider else []))
    out = outs[0][0] if squeeze else outs[0]
    return (out, list(outs[1:])) if rider else out


def mm_tn(a, b, name, out_dtype=F32):
    squeeze = a.ndim == 2
    if squeeze:
        a, b = a[None], b[None]
    g, t, ka = a.shape
    n = b.shape[2]
    tka = _pick(ka, (1024, 1408, 512, 128))
    tn = _pick(n, (2048, 1408, 1024, 512, 128))
    tt = _pick(t, (2048, 1280, 1024, 512, 256, 128))
    nt = t // tt

    def body(a_ref, b_ref, o_ref, acc_ref):
        tt_i = pl.program_id(3)

        @pl.when(tt_i == 0)
        def _():
            acc_ref[...] = jnp.zeros_like(acc_ref)

        acc_ref[...] += _dot_tn(a_ref[0], b_ref[0])

        @pl.when(tt_i == nt - 1)
        def _():
            o_ref[0] = acc_ref[...].astype(o_ref.dtype)

    out = pl.pallas_call(
        body, name=name, grid=(g, ka // tka, n // tn, nt),
        in_specs=[pl.BlockSpec((1, tt, tka), lambda gi, i, j, ti: (gi, ti, i)),
                  pl.BlockSpec((1, tt, tn), lambda gi, i, j, ti: (gi, ti, j))],
        out_specs=pl.BlockSpec((1, tka, tn), lambda gi, i, j, ti: (gi, i, j)),
        out_shape=jax.ShapeDtypeStruct((g, ka, n), out_dtype),
        scratch_shapes=[pltpu.VMEM((tka, tn), F32)],
        compiler_params=_cparams(("parallel", "parallel", "parallel", "arbitrary")),
    )(a, b)
    return out[0] if squeeze else out


MODB = 512


def mod_fwd(cc, slabs, w_cols, b_mod):
    per = w_cols // MODB
    n = slabs.shape[0] * w_cols

    def body(c_ref, w_ref, b_ref, o_ref):
        o_ref[...] = _dot(_silu(c_ref[...]), w_ref[0]) + b_ref[...]

    return pl.pallas_call(
        body, name="mod_fwd", grid=(n // MODB,),
        in_specs=[_bc(16, D), pl.BlockSpec((1, D, MODB), lambda j: (j // per, 0, j % per)),
                  pl.BlockSpec((1, MODB), lambda j: (0, j))],
        out_specs=pl.BlockSpec((16, MODB), lambda j: (0, j)),
        out_shape=jax.ShapeDtypeStruct((16, n), F32),
        compiler_params=_cparams(("parallel",)),
    )(cc, slabs, b_mod)


def _ctx_x_specs(n_ctx_tiles):
    return [pl.BlockSpec((TR, D), lambda i: (jnp.minimum(i, n_ctx_tiles - 1), 0)),
            pl.BlockSpec((TR, D), lambda i: (jnp.maximum(i - n_ctx_tiles, 0), 0))]


def norm1_fwd(ctx, x, g, tab, n_ctx_tiles):
    tp = ctx.shape[0] + x.shape[0]

    def body(c_ref, x_ref, g_ref, tab_ref, o_ref):
        is_ctx = pl.program_id(0) < n_ctx_tiles
        x = jnp.where(is_ctx, c_ref[...], x_ref[...])
        r = lax.rsqrt(jnp.mean(x * x, axis=-1, keepdims=True) + EPS)
        nrm = x * r * g_ref[...]
        sh = jnp.where(is_ctx, tab_ref[0:1, :], tab_ref[2:3, :])
        sc = jnp.where(is_ctx, tab_ref[1:2, :], tab_ref[3:4, :])
        o_ref[...] = (nrm * (1.0 + sc) + sh).astype(BF16)

    return pl.pallas_call(
        body, name="norm1_fwd", grid=(tp // TR,),
        in_specs=_ctx_x_specs(n_ctx_tiles) + [_bc(1, D), _bc(8, D)],
        out_specs=_rb(TR, D),
        out_shape=jax.ShapeDtypeStruct((tp, D), BF16),
        compiler_params=_cparams(("parallel",)),
    )(ctx, x, g, tab)


def norm2_fwd(x, out, g, tab):
    t = x.shape[0]

    def body(x_ref, o_in_ref, g_ref, tab_ref, h1_ref, hn_ref):
        h1 = x_ref[...] + tab_ref[0:1, :] * o_in_ref[...].astype(F32)
        h1_ref[...] = h1
        r = lax.rsqrt(jnp.mean(h1 * h1, axis=-1, keepdims=True) + EPS)
        nrm = h1 * r * g_ref[...]
        hn_ref[...] = (nrm * (1.0 + tab_ref[2:3, :]) + tab_ref[1:2, :]).astype(BF16)

    return pl.pallas_call(
        body, name="norm2_fwd", grid=(t // TR2,),
        in_specs=[_rb(TR2, D), _rb(TR2, D), _bc(1, D), _bc(8, D)],
        out_specs=[_rb(TR2, D), _rb(TR2, D)],
        out_shape=[jax.ShapeDtypeStruct((t, D), F32), jax.ShapeDtypeStruct((t, D), BF16)],
        compiler_params=_cparams(("parallel",)),
    )(x, out, g, tab)


def _halo_specs(tm, w, col, n_rows, hb):
    per = tm // hb
    last = n_rows // hb - 1
    prev = pl.BlockSpec((hb, w), lambda i: (jnp.maximum(i * per - 1, 0), col))
    nxt = pl.BlockSpec((hb, w), lambda i: (jnp.minimum((i + 1) * per, last), col))
    return prev, nxt


def _shift_rows(x, prev_row, next_row):
    tm = x.shape[0]
    rid = lax.broadcasted_iota(jnp.int32, x.shape, 0)
    xm1 = jnp.where(rid == 0, prev_row, pltpu.roll(x, 1, 0))
    xp1 = jnp.where(rid == tm - 1, next_row, pltpu.roll(x, tm - 1, 0))
    return xm1, xp1


def _seq_edges(i, n_ctx_tiles, n_tiles):
    first = jnp.logical_or(i == 0, i == n_ctx_tiles)
    last = jnp.logical_or(i == n_ctx_tiles - 1, i == n_tiles - 1)
    return first, last


def qkconv_fwd(zmain, zg, conv_w, b_gate, n_ctx_tiles):
    tp = zmain.shape[0]
    nt = tp // TR
    w2 = 2 * D
    prev_s, next_s = _halo_specs(TR, w2, CB_Q // 2, tp, 16)

    def body(z_ref, zp_ref, zn_ref, w_ref, zg_ref, bg_ref, q_ref, k_ref, g_ref, zgb_ref):
        i = pl.program_id(0)
        first, last = _seq_edges(i, n_ctx_tiles, nt)
        z = z_ref[...].astype(F32)
        pr = jnp.where(first, 0.0, zp_ref[15:16, :].astype(F32))
        nx = jnp.where(last, 0.0, zn_ref[0:1, :].astype(F32))
        zm1, zp1 = _shift_rows(z, pr, nx)
        cv = w_ref[0:1, :] * zm1 + w_ref[1:2, :] * z + w_ref[2:3, :] * zp1
        a = _silu(cv)
        q_ref[...] = (a[:, :D] * (DH ** -0.5)).astype(BF16)
        k_ref[...] = a[:, D:].astype(BF16)
        zgb = zg_ref[...] + bg_ref[...]
        zgb_ref[...] = zgb
        logf = jnp.minimum(zgb, 0.0) - jnp.log(1.0 + jnp.exp(-jnp.abs(zgb)))
        rr = lax.broadcasted_iota(jnp.int32, (TR, TR), 0)
        cc = lax.broadcasted_iota(jnp.int32, (TR, TR), 1)
        same = (rr // LC) == (cc // LC)
        low = jnp.where(jnp.logical_and(same, cc <= rr), 1.0, 0.0).astype(BF16)
        upp = jnp.where(jnp.logical_and(same, cc >= rr), 1.0, 0.0).astype(BF16)
        bf = _exact_dot(low, logf)
        bb = _exact_dot(upp, logf)
        lane = lax.broadcasted_iota(jnp.int32, (TR, LANE), 1)
        g = jnp.where(jnp.logical_and(lane >= 4, lane < 8), bf,
                      jnp.where(jnp.logical_and(lane >= 12, lane < 16), bb, zgb))
        g_ref[...] = g

    return pl.pallas_call(
        body, name="qkconv_fwd", grid=(nt,),
        in_specs=[_rb(TR, w2, CB_Q // 2), prev_s, next_s, _bc(8, w2), _rb(TR, LANE), _bc(1, LANE)],
        out_specs=[_rb(TR, D), _rb(TR, D), _rb(TR, LANE), _rb(TR, LANE)],
        out_shape=[jax.ShapeDtypeStruct((tp, D), BF16), jax.ShapeDtypeStruct((tp, D), BF16),
                   jax.ShapeDtypeStruct((tp, LANE), F32), jax.ShapeDtypeStruct((tp, LANE), F32)],
        compiler_params=_cparams(("parallel",)),
    )(zmain, zmain, zmain, conv_w, zg, b_gate)


def qkconv_bwd_a(zmain, dqf, dqb, dkf, dkb, conv_w, n_ctx_tiles):
    tp = zmain.shape[0]
    nt = tp // TR
    w2 = 2 * D
    prev_s, next_s = _halo_specs(TR, w2, CB_Q // 2, tp, 16)

    def body(z_ref, zp_ref, zn_ref, w_ref, dqf_ref, dqb_ref, dkf_ref, dkb_ref, dc_ref, dw_ref):
        i = pl.program_id(0)
        first, last = _seq_edges(i, n_ctx_tiles, nt)
        z = z_ref[...].astype(F32)
        pr = jnp.where(first, 0.0, zp_ref[15:16, :].astype(F32))
        nx = jnp.where(last, 0.0, zn_ref[0:1, :].astype(F32))
        zm1, zp1 = _shift_rows(z, pr, nx)
        cv = w_ref[0:1, :] * zm1 + w_ref[1:2, :] * z + w_ref[2:3, :] * zp1
        da = jnp.concatenate(
            [(dqf_ref[...].astype(F32) + dqb_ref[...].astype(F32)) * (DH ** -0.5),
             dkf_ref[...].astype(F32) + dkb_ref[...].astype(F32)], axis=1)
        dc = da * _dsilu(cv)
        dc_ref[...] = dc.astype(BF16)

        @pl.when(i == 0)
        def _():
            dw_ref[...] = jnp.zeros_like(dw_ref)

        dw_ref[0:1, :] += jnp.sum(zm1 * dc, axis=0, keepdims=True)
        dw_ref[1:2, :] += jnp.sum(z * dc, axis=0, keepdims=True)
        dw_ref[2:3, :] += jnp.sum(zp1 * dc, axis=0, keepdims=True)

    return pl.pallas_call(
        body, name="qkconv_bwd_a", grid=(nt,),
        in_specs=[_rb(TR, w2, CB_Q // 2), prev_s, next_s, _bc(8, w2), _rb(TR, D), _rb(TR, D), _rb(TR, D), _rb(TR, D)],
        out_specs=[_rb(TR, w2), _bc(8, w2)],
        out_shape=[jax.ShapeDtypeStruct((tp, w2), BF16), jax.ShapeDtypeStruct((8, w2), F32)],
        compiler_params=_cparams(("arbitrary",)),
    )(zmain, zmain, zmain, conv_w, dqf, dqb, dkf, dkb)


def qkconv_bwd_b(dz, dc, conv_w, n_ctx_tiles):
    tp = dc.shape[0]
    nt = tp // TR
    w2 = 2 * D
    prev_s, next_s = _halo_specs(TR, w2, 0, tp, 16)

    def body(dz_in_ref, d_ref, dp_ref, dn_ref, w_ref, o_ref):
        del dz_in_ref
        i = pl.program_id(0)
        first, last = _seq_edges(i, n_ctx_tiles, nt)
        d = d_ref[...].astype(F32)
        pr = jnp.where(first, 0.0, dp_ref[15:16, :].astype(F32))
        nx = jnp.where(last, 0.0, dn_ref[0:1, :].astype(F32))
        dm1, dp1 = _shift_rows(d, pr, nx)
        o_ref[...] = (w_ref[0:1, :] * dp1 + w_ref[1:2, :] * d + w_ref[2:3, :] * dm1).astype(BF16)

    return pl.pallas_call(
        body, name="qkconv_bwd_b", grid=(nt,),
        in_specs=[pl.BlockSpec(memory_space=pl.ANY), _rb(TR, w2), prev_s, next_s, _bc(8, w2)],
        out_specs=_rb(TR, w2, CB_Q // 2),
        out_shape=jax.ShapeDtypeStruct(dz.shape, BF16),
        input_output_aliases={0: 0},
        compiler_params=_cparams(("parallel",)),
    )(dz, dc, dc, dc, conv_w)


def add_into_dz(dz, a, b, col):
    tp = a.shape[0]

    def body(dz_in_ref, a_ref, b_ref, o_ref):
        del dz_in_ref
        o_ref[...] = (a_ref[...].astype(F32) + b_ref[...].astype(F32)).astype(BF16)

    return pl.pallas_call(
        body, name="add_into_dz", grid=(tp // TR,),
        in_specs=[pl.BlockSpec(memory_space=pl.ANY), _rb(TR, D), _rb(TR, D)],
        out_specs=_rb(TR, D, col),
        out_shape=jax.ShapeDtypeStruct(dz.shape, BF16),
        input_output_aliases={0: 0},
        compiler_params=_cparams(("parallel",)),
    )(dz, a, b)


def _chunk_maps(nc, ncc):
    def fwd(t):
        return t

    def bwd(t):
        return jnp.where(t < ncc, ncc - 1 - t, nc - 1 + ncc - t)

    return fwd, bwd


def _split2(x):
    hi = x.astype(BF16)
    return hi, (x - hi.astype(F32)).astype(BF16)


def _mlstm_chunk(d, h, gc, gr, q_ref, k_ref, v_ref, cp, npv, m_prev, mask, precise=False):
    ic, bcol = 8 * d + h, 8 * d + 4 + h
    i_col, b_col = gc[:, ic:ic + 1], gc[:, bcol:bcol + 1]
    i_row, b_row = gr[ic:ic + 1, :], gr[bcol:bcol + 1, :]
    g = b_row[:, LC - 1:LC] if d == 0 else b_row[:, 0:1]
    a_row = g - b_row + i_row
    m_loc = jnp.max(a_row, axis=1, keepdims=True)
    dmat = jnp.where(mask, b_col - b_row + i_row, -jnp.inf)
    inter = b_col + m_prev
    m_row = jnp.maximum(inter, jnp.max(dmat, axis=1, keepdims=True))
    e = jnp.exp(dmat - m_row)
    w = jnp.exp(inter - m_row)
    hs = slice(h * DH, (h + 1) * DH)
    qh, kh, vh = q_ref[:, hs], k_ref[:, hs], v_ref[:, hs]
    p = _dot_nt(qh, kh)
    s = p * e
    cpb = cp.astype(BF16)
    qc = _dot(qh, cpb)
    if precise:
        s_hi, s_lo = _split2(s)
        num = _dot(s_hi, vh) + _dot(s_lo, vh) + w * qc
    else:
        num = _dot(s, vh) + w * qc
    qn = jnp.sum(qh.astype(F32) * npv, axis=1, keepdims=True)
    den = jnp.sum(s, axis=1, keepdims=True) + w * qn
    thr = jnp.exp(-m_row)
    m_new = jnp.maximum(g + m_prev, m_loc)
    a_old = jnp.exp(g + m_prev - m_new)
    a_col = g - b_col + i_col
    return dict(qh=qh, kh=kh, vh=vh, e=e, w=w, s=s, cpb=cpb, qc=qc, num=num, qn=qn, den=den, thr=thr,
                m_loc=m_loc, m_new=m_new, a_old=a_old, a_col=a_col, hs=hs)


def mlstm_fwd(qa, ka, zmain, gcol, grow, ncc):
    tp = qa.shape[0]
    nc = tp // LC
    cf, cb = _chunk_maps(nc, ncc)

    def body(qf, kf, vf, gcf, grf, qb, kb, vb, gcb, grb,
             hf_o, hb_o, cf_o, cb_o, nf_o, nb_o, mf_o, mb_o, c_sc, n_sc, m_sc):
        t = pl.program_id(0)

        @pl.when(t == 0)
        def _():
            c_sc[...] = jnp.zeros_like(c_sc)
            n_sc[...] = jnp.zeros_like(n_sc)
            m_sc[...] = jnp.full(m_sc.shape, M_INIT, F32)

        row = lax.broadcasted_iota(jnp.int32, (LC, LC), 0)
        col = lax.broadcasted_iota(jnp.int32, (LC, LC), 1)
        dirs = ((qf, kf, vf, gcf, grf, hf_o, cf_o, nf_o, mf_o), (qb, kb, vb, gcb, grb, hb_o, cb_o, nb_o, mb_o))
        for d, (q_ref, k_ref, v_ref, gc_ref, gr_ref, h_o, c_o, n_o, m_o) in enumerate(dirs):
            mask = (col <= row) if d == 0 else (col >= row)
            gc = gc_ref[...]
            gr = gr_ref[0]
            for h in range(NH):
                idx = d * NH + h
                cp = c_sc[idx]
                npv = n_sc[idx]
                m_full = m_sc[idx]
                m_prev = m_full[:, 0:1]
                r = _mlstm_chunk(d, h, gc, gr, q_ref, k_ref, v_ref, cp, npv, m_prev, mask)
                hs = r["hs"]
                h_o[:, hs] = (r["num"] / jnp.maximum(jnp.abs(r["den"]), r["thr"])).astype(BF16)
                c_o[0, hs, :] = r["cpb"]
                n_o[0, h:h + 1, :] = npv
                m_o[0, h:h + 1, :] = m_full
                a_new = jnp.exp(r["m_loc"] - r["m_new"])
                kw = r["kh"].astype(F32) * jnp.exp(r["a_col"] - r["m_loc"])
                kv = _dot_tn_mxu(kw, r["vh"])
                kn = jnp.sum(kw, axis=0, keepdims=True)
                c_sc[idx] = r["a_old"] * cp + a_new * kv
                n_sc[idx] = r["a_old"] * npv + a_new * kn
                m_sc[idx] = jnp.broadcast_to(r["m_new"], (1, LANE))

    def dspecs(cm):
        return [pl.BlockSpec((LC, D), lambda t: (cm(t), 0)),
                pl.BlockSpec((LC, D), lambda t: (cm(t), 0)),
                pl.BlockSpec((LC, D), lambda t: (cm(t), CB_V)),
                pl.BlockSpec((LC, LANE), lambda t: (cm(t), 0)),
                pl.BlockSpec((1, 16, LC), lambda t: (cm(t), 0, 0))]

    def ospec(cm, shp):
        return pl.BlockSpec((1,) + shp, lambda t: (cm(t), 0, 0))

    return pl.pallas_call(
        body, name="mlstm_fwd", grid=(nc,),
        in_specs=dspecs(cf) + dspecs(cb),
        out_specs=[pl.BlockSpec((LC, D), lambda t: (cf(t), 0)), pl.BlockSpec((LC, D), lambda t: (cb(t), 0)),
                   ospec(cf, (D, DH)), ospec(cb, (D, DH)), ospec(cf, (NH, DH)), ospec(cb, (NH, DH)),
                   ospec(cf, (NH, LANE)), ospec(cb, (NH, LANE))],
        out_shape=[jax.ShapeDtypeStruct((tp, D), BF16), jax.ShapeDtypeStruct((tp, D), BF16),
                   jax.ShapeDtypeStruct((nc, D, DH), BF16), jax.ShapeDtypeStruct((nc, D, DH), BF16),
                   jax.ShapeDtypeStruct((nc, NH, DH), F32), jax.ShapeDtypeStruct((nc, NH, DH), F32),
                   jax.ShapeDtypeStruct((nc, NH, LANE), F32), jax.ShapeDtypeStruct((nc, NH, LANE), F32)],
        scratch_shapes=[pltpu.VMEM((2 * NH, DH, DH), F32), pltpu.VMEM((2 * NH, 1, DH), F32),
                        pltpu.VMEM((2 * NH, 1, LANE), F32)],
        compiler_params=_cparams(("arbitrary",)),
    )(qa, ka, zmain, gcol, grow, qa, ka, zmain, gcol, grow)


def mlstm_bwd(qa, ka, zmain, gcol, grow, states, dhm, ncc):
    tp = qa.shape[0]
    nc = tp // LC
    cf0, cb0 = _chunk_maps(nc, ncc)
    cf = lambda t: cf0(nc - 1 - t)
    cb = lambda t: cb0(nc - 1 - t)
    csf, csb, nsf, nsb, msf, msb = states

    def body(qf, kf, vf, gcf, grf, cpf, npf, mpf, dhf, qb, kb, vb, gcb, grb, cpb_, npb, mpb, dhb,
             dqf_o, dkf_o, dvf_o, colf_o, rowf_o, dqb_o, dkb_o, dvb_o, colb_o, rowb_o, dc_sc, dn_sc):
        t = pl.program_id(0)

        @pl.when(t == 0)
        def _():
            dc_sc[...] = jnp.zeros_like(dc_sc)
            dn_sc[...] = jnp.zeros_like(dn_sc)

        row = lax.broadcasted_iota(jnp.int32, (LC, LC), 0)
        col = lax.broadcasted_iota(jnp.int32, (LC, LC), 1)
        dirs = ((qf, kf, vf, gcf, grf, cpf, npf, mpf, dhf, dqf_o, dkf_o, dvf_o, colf_o, rowf_o, cf),
                (qb, kb, vb, gcb, grb, cpb_, npb, mpb, dhb, dqb_o, dkb_o, dvb_o, colb_o, rowb_o, cb))
        for d, (q_ref, k_ref, v_ref, gc_ref, gr_ref, cp_ref, np_ref, mp_ref, dh_ref,
                dq_o, dk_o, dv_o, col_o, row_o, cm) in enumerate(dirs):
            mask = (col <= row) if d == 0 else (col >= row)
            live = jnp.where(cm(t) >= ncc, 1.0, 0.0).astype(F32)
            gc = gc_ref[...]
            gr = gr_ref[0]
            col_o[...] = jnp.zeros_like(col_o)
            row_o[...] = jnp.zeros_like(row_o)
            for h in range(NH):
                idx = d * NH + h
                hs = slice(h * DH, (h + 1) * DH)
                cp = cp_ref[0, hs, :]
                npv = np_ref[0, h:h + 1, :]
                m_prev = mp_ref[0, h:h + 1, 0:1]
                r = _mlstm_chunk(d, h, gc, gr, q_ref, k_ref, v_ref, cp, npv, m_prev, mask, precise=True)
                qh, kh, vh, e, w, s = r["qh"], r["kh"], r["vh"], r["e"], r["w"], r["s"]
                qf32, kf32 = qh.astype(F32), kh.astype(F32)
                den, thr = r["den"], r["thr"]
                rden = 1.0 / jnp.maximum(jnp.abs(den), thr)
                hh = r["num"] * rden
                dh = dh_ref[:, hs].astype(F32) * live
                dnum = dh * rden
                sgn = jnp.where(jnp.abs(den) > thr, jnp.sign(den), 0.0)
                dden = -jnp.sum(dh * hh, axis=1, keepdims=True) * rden * sgn
                dn_hi, dn_lo = _split2(dnum)
                ds = _dot_nt(dn_hi, vh) + _dot_nt(dn_lo, vh) + dden
                dp = ds * e
                gm = ds * s
                rowsum = jnp.sum(gm, axis=1, keepdims=True)
                colsum = jnp.sum(gm, axis=0, keepdims=True)
                dq = _dot(dp, kh) + w * (_dot_nt(dnum, r["cpb"]) + dden * npv)
                dcs = dc_sc[idx]
                dns = dn_sc[idx]
                kfac = jnp.exp(r["a_col"] - r["m_new"])
                vdc = _dot_nt(vh, dcs)
                dk = _dot_tn(dp, qh) + kfac * (vdc + dns)
                dv = _dot_tn(s, dnum) + kfac * _dot(kh, dcs)
                beta = w * (jnp.sum(dnum * r["qc"], axis=1, keepdims=True) + dden * r["qn"])
                alpha = kfac * (jnp.sum(kf32 * vdc, axis=1, keepdims=True) + jnp.sum(kf32 * dns, axis=1, keepdims=True))
                dq_o[:, hs] = dq.astype(BF16)
                dk_o[:, hs] = dk.astype(BF16)
                dv_o[:, hs] = dv.astype(BF16)
                cpf = r["cpb"].astype(F32)
                inner = (jnp.sum(jnp.sum(dcs * cpf, axis=1, keepdims=True), axis=0, keepdims=True)
                         + jnp.sum(dns * npv, axis=1, keepdims=True))
                gam = jnp.sum(alpha, axis=0, keepdims=True) + r["a_old"] * inner
                lo = 8 * d + h
                col_o[:, lo:lo + 1] = alpha
                col_o[:, lo + 4:lo + 5] = rowsum + beta - alpha
                col_o[:, lo + 36:lo + 37] = jnp.broadcast_to(gam, (LC, 1))
                row_o[0, h:h + 1, :] = colsum
                wq = qf32 * w
                dc_sc[idx] = r["a_old"] * dcs + _dot_tn(wq, dnum)
                dn_sc[idx] = r["a_old"] * dns + jnp.sum(wq * dden, axis=0, keepdims=True)

    def dspecs(cm):
        return [pl.BlockSpec((LC, D), lambda t: (cm(t), 0)),
                pl.BlockSpec((LC, D), lambda t: (cm(t), 0)),
                pl.BlockSpec((LC, D), lambda t: (cm(t), CB_V)),
                pl.BlockSpec((LC, LANE), lambda t: (cm(t), 0)),
                pl.BlockSpec((1, 16, LC), lambda t: (cm(t), 0, 0)),
                pl.BlockSpec((1, D, DH), lambda t: (cm(t), 0, 0)),
                pl.BlockSpec((1, NH, DH), lambda t: (cm(t), 0, 0)),
                pl.BlockSpec((1, NH, LANE), lambda t: (cm(t), 0, 0)),
                pl.BlockSpec((LC, D), lambda t: (jnp.maximum(cm(t) - ncc, 0), 0))]

    def ospecs(cm):
        return [pl.BlockSpec((LC, D), lambda t: (cm(t), 0)),
                pl.BlockSpec((LC, D), lambda t: (cm(t), 0)),
                pl.BlockSpec((LC, D), lambda t: (cm(t), 0)),
                pl.BlockSpec((LC, LANE), lambda t: (cm(t), 0)),
                pl.BlockSpec((1, 8, LC), lambda t: (cm(t), 0, 0))]

    oshape = [jax.ShapeDtypeStruct((tp, D), BF16)] * 3 + [jax.ShapeDtypeStruct((tp, LANE), F32),
                                                        jax.ShapeDtypeStruct((nc, 8, LC), F32)]
    return pl.pallas_call(
        body, name="mlstm_bwd", grid=(nc,),
        in_specs=dspecs(cf) + dspecs(cb),
        out_specs=ospecs(cf) + ospecs(cb),
        out_shape=oshape + oshape,
        scratch_shapes=[pltpu.VMEM((2 * NH, DH, DH), F32), pltpu.VMEM((2 * NH, 1, DH), F32)],
        compiler_params=_cparams(("arbitrary",)),
    )(qa, ka, zmain, gcol, grow, csf, nsf, msf, dhm, qa, ka, zmain, gcol, grow, csb, nsb, msb, dhm)


def gates_bwd(colf, colb, cs, zgb):
    tp = colf.shape[0]

    def body(cf_ref, cb_ref, cs_ref, zgb_ref, o_ref, db_ref):
        i = pl.program_id(0)

        @pl.when(i == 0)
        def _():
            db_ref[...] = jnp.zeros_like(db_ref)

        lane = lax.broadcasted_iota(jnp.int32, (TR, LANE), 1)
        i_l = jnp.logical_or(lane < 4, jnp.logical_and(lane >= 8, lane < 12))
        f_l = jnp.logical_or(jnp.logical_and(lane >= 4, lane < 8), jnp.logical_and(lane >= 12, lane < 16))
        cv = cf_ref[...] + cb_ref[...]
        csv = cs_ref[...]
        gam = pltpu.roll(cv, LANE - 32, 1)
        dbh = jnp.where(f_l, cv - csv, 0.0)
        rr = lax.broadcasted_iota(jnp.int32, (TR, TR), 0)
        cc = lax.broadcasted_iota(jnp.int32, (TR, TR), 1)
        same = (rr // LC) == (cc // LC)
        low = jnp.where(jnp.logical_and(same, cc <= rr), 1.0, 0.0).astype(BF16)
        upp = jnp.where(jnp.logical_and(same, cc >= rr), 1.0, 0.0).astype(BF16)
        dlogf = jnp.where(lane < 8, _exact_dot(upp, dbh), _exact_dot(low, dbh)) + gam
        out = jnp.where(i_l, csv + cv, 0.0) + jnp.where(f_l, dlogf * _sigmoid(-zgb_ref[...]), 0.0)
        o_ref[...] = out
        db_ref[...] += jnp.sum(out, axis=0, keepdims=True)

    spec = _rb(TR, LANE)
    return pl.pallas_call(
        body, name="gates_bwd", grid=(tp // TR,),
        in_specs=[spec] * 4,
        out_specs=[spec, _bc(1, LANE)],
        out_shape=[jax.ShapeDtypeStruct((tp, LANE), F32), jax.ShapeDtypeStruct((1, LANE), F32)],
        compiler_params=_cparams(("arbitrary",)),
    )(colf, colb, cs, zgb)


def _head_norm(hm, gh):
    xs, rs = [], []
    for h in range(NH):
        seg = hm[:, h * DH:(h + 1) * DH]
        r = lax.rsqrt(jnp.mean(seg * seg, axis=-1, keepdims=True) + EPS)
        xs.append(seg * r)
        rs.append(r)
    xh = jnp.concatenate(xs, axis=1)
    return xh, rs, xh * gh


def _sgu_norm(vg, ln_g, ln_b):
    mu = jnp.mean(vg, axis=-1, keepdims=True)
    vc = vg - mu
    rstd = lax.rsqrt(jnp.mean(vc * vc, axis=-1, keepdims=True) + EPS)
    vhat = vc * rstd
    return vhat, rstd, vhat * ln_g + ln_b


def _sgu_mix(vn, ws_ref, bs_ref):
    rows = []
    for c in range(TR // SCH):
        cols = []
        for g in range(SG):
            blk = vn[c * SCH:(c + 1) * SCH, g * SGD:(g + 1) * SGD]
            cols.append(_dot(ws_ref[g * SCH:(g + 1) * SCH, :], blk) + bs_ref[:, g:g + 1])
        rows.append(jnp.concatenate(cols, axis=1))
    return jnp.concatenate(rows, axis=0)


def mixer_fwd(hf, hb, zmain, gh, ln_g, ln_b, ws, bs_t, n_ctx_tiles):
    t = hf.shape[0] - n_ctx_tiles * TR
    off = n_ctx_tiles

    def body(hf_ref, hb_ref, zo_ref, zu_ref, zv_ref, gh_ref, lg_ref, lb_ref, ws_ref, bs_ref, o_ref):
        hm = hf_ref[...].astype(F32) + hb_ref[...].astype(F32)
        _, _, hn = _head_norm(hm, gh_ref[...])
        o_ref[0] = (_sigmoid(zo_ref[...].astype(F32)) * hn).astype(BF16)
        _, _, vn = _sgu_norm(_gelu(zv_ref[...].astype(F32)), lg_ref[...], lb_ref[...])
        mixed = _sgu_mix(vn, ws_ref, bs_ref)
        o_ref[1] = (_gelu(zu_ref[...].astype(F32)) * mixed).astype(BF16)

    return pl.pallas_call(
        body, name="mixer_fwd", grid=(t // TR,),
        in_specs=[_rb(TR, D, 0, off), _rb(TR, D, 0, off), _rb(TR, D, CB_O, off), _rb(TR, D, CB_U, off),
                  _rb(TR, D, CB_VG, off), _bc(1, D), _bc(1, D), _bc(1, D), _bc(SG * SCH, SCH), _bc(SCH, LANE)],
        out_specs=pl.BlockSpec((2, TR, D), lambda i: (0, i, 0)),
        out_shape=jax.ShapeDtypeStruct((2, t, D), BF16),
        compiler_params=_cparams(("parallel",)),
    )(hf, hb, zmain, zmain, zmain, gh, ln_g, ln_b, ws, bs_t)


def merge_fwd(zmain, pp, n_ctx_tiles):
    t = pp.shape[1]
    off = n_ctx_tiles

    def body(zgm_ref, zgg_ref, pp_ref, o_ref):
        y = (_sigmoid(zgm_ref[...].astype(F32)) * pp_ref[0].astype(F32)
             + _sigmoid(zgg_ref[...].astype(F32)) * pp_ref[1].astype(F32))
        o_ref[...] = y.astype(BF16)

    return pl.pallas_call(
        body, name="merge_fwd", grid=(t // TR,),
        in_specs=[_rb(TR, D, CB_GM, off), _rb(TR, D, CB_GG, off), pl.BlockSpec((2, TR, D), lambda i: (0, i, 0))],
        out_specs=_rb(TR, D),
        out_shape=jax.ShapeDtypeStruct((t, D), BF16),
        compiler_params=_cparams(("parallel",)),
    )(zmain, zmain, pp)


def merge_bwd(zmain, pp, dy, tp, n_ctx_tiles):
    t = dy.shape[0]
    nt = tp // TR
    xrow = lambda i: jnp.maximum(i - n_ctx_tiles, 0)

    def body(zg_ref, pp_ref, dy_ref, dpp_ref, dz_ref):
        i = pl.program_id(1)
        zg = zg_ref[...].astype(F32)
        sg = _sigmoid(zg)
        dyv = dy_ref[...].astype(F32)
        dpp_ref[0] = (dyv * sg).astype(BF16)
        dzv = dyv * pp_ref[0].astype(F32) * sg * (1.0 - sg)
        dz_ref[...] = jnp.where(i >= n_ctx_tiles, dzv, 0.0).astype(BF16)

    return pl.pallas_call(
        body, name="merge_bwd", grid=(2, nt),
        in_specs=[pl.BlockSpec((TR, D), lambda j, i: (i, CB_GM + j)),
                  pl.BlockSpec((1, TR, D), lambda j, i: (j, xrow(i), 0)),
                  pl.BlockSpec((TR, D), lambda j, i: (xrow(i), 0))],
        out_specs=[pl.BlockSpec((1, TR, D), lambda j, i: (j, xrow(i), 0)),
                   pl.BlockSpec((TR, D), lambda j, i: (i, CB_GM + j))],
        out_shape=[jax.ShapeDtypeStruct((2, t, D), BF16), jax.ShapeDtypeStruct((tp, 8 * D), BF16)],
        compiler_params=_cparams(("arbitrary", "arbitrary")),
    )(zmain, pp, dy)


def mixer_bwd(dz, hf, hb, zmain, dyms, gh, ln_g, ln_b, ws, bs_t, n_ctx_tiles):
    tp = hf.shape[0]
    t = tp - n_ctx_tiles * TR
    nt = tp // TR
    xrow = lambda i: jnp.maximum(i - n_ctx_tiles, 0)

    def body(dz_in_ref, hf_ref, hb_ref, zo_ref, zu_ref, zv_ref, dy_ref, gh_ref, lg_ref, lb_ref, ws_ref, bs_ref,
             dz_ref, dhm_ref, dgh_ref, dlg_ref, dlb_ref, dws_ref, dbs_ref):
        del dz_in_ref
        i = pl.program_id(0)

        @pl.when(i == 0)
        def _():
            for ref in (dgh_ref, dlg_ref, dlb_ref, dws_ref, dbs_ref):
                ref[...] = jnp.zeros_like(ref)

        @pl.when(i < n_ctx_tiles)
        def _():
            dz_ref[...] = jnp.zeros_like(dz_ref)

        @pl.when(i >= n_ctx_tiles)
        def _():
            gh_v = gh_ref[...]
            hm = hf_ref[...].astype(F32) + hb_ref[...].astype(F32)
            xh, rs, hn = _head_norm(hm, gh_v)
            zo = zo_ref[...].astype(F32)
            so = _sigmoid(zo)
            dym = dy_ref[0].astype(F32)
            d_zo = dym * hn * so * (1.0 - so)
            d_hn = dym * so
            dgh_ref[...] += jnp.sum(d_hn * xh, axis=0, keepdims=True)
            d_xh = d_hn * gh_v
            segs = []
            for h in range(NH):
                hs = slice(h * DH, (h + 1) * DH)
                dx, xs = d_xh[:, hs], xh[:, hs]
                segs.append(rs[h] * (dx - xs * jnp.mean(dx * xs, axis=-1, keepdims=True)))
            dhm_ref[...] = jnp.concatenate(segs, axis=1).astype(BF16)
            zu = zu_ref[...].astype(F32)
            zv = zv_ref[...].astype(F32)
            lg = lg_ref[...]
            gu, dgu = _gelu_and_grad(zu)
            gv, dgv = _gelu_and_grad(zv)
            vhat, rstd, vn = _sgu_norm(gv, lg, lb_ref[...])
            mixed = _sgu_mix(vn, ws_ref, bs_ref)
            dys = dy_ref[1].astype(F32)
            d_zu = dys * mixed * dgu
            d_mixed = dys * gu
            rows = []
            for c in range(TR // SCH):
                cols = []
                for g in range(SG):
                    rsl, csl = slice(c * SCH, (c + 1) * SCH), slice(g * SGD, (g + 1) * SGD)
                    dm = d_mixed[rsl, csl]
                    cols.append(_dot_tn(ws_ref[g * SCH:(g + 1) * SCH, :], dm))
                    dws_ref[g * SCH:(g + 1) * SCH, :] += _dot_nt(dm, vn[rsl, csl])
                    dbs_ref[:, g:g + 1] += jnp.sum(dm, axis=1, keepdims=True)
                rows.append(jnp.concatenate(cols, axis=1))
            d_vn = jnp.concatenate(rows, axis=0)
            dlg_ref[...] += jnp.sum(d_vn * vhat, axis=0, keepdims=True)
            dlb_ref[...] += jnp.sum(d_vn, axis=0, keepdims=True)
            d_vhat = d_vn * lg
            d_vg = rstd * (d_vhat - jnp.mean(d_vhat, axis=-1, keepdims=True)
                           - vhat * jnp.mean(d_vhat * vhat, axis=-1, keepdims=True))
            d_zv = d_vg * dgv
            dz_ref[...] = jnp.concatenate([d_zo, d_zu, d_zv], axis=1).astype(BF16)

    return pl.pallas_call(
        body, name="mixer_bwd", grid=(nt,),
        in_specs=[pl.BlockSpec(memory_space=pl.ANY), _rb(TR, D), _rb(TR, D), _rb(TR, D, CB_O), _rb(TR, D, CB_U),
                  _rb(TR, D, CB_VG), pl.BlockSpec((2, TR, D), lambda i: (0, xrow(i), 0)),
                  _bc(1, D), _bc(1, D), _bc(1, D), _bc(SG * SCH, SCH), _bc(SCH, LANE)],
        out_specs=[_rb(TR, 3 * D), pl.BlockSpec((TR, D), lambda i: (xrow(i), 0)),
                   _bc(1, D), _bc(1, D), _bc(1, D), _bc(SG * SCH, SCH), _bc(SCH, LANE)],
        out_shape=[jax.ShapeDtypeStruct(dz.shape, BF16), jax.ShapeDtypeStruct((t, D), BF16),
                   jax.ShapeDtypeStruct((1, D), F32), jax.ShapeDtypeStruct((1, D), F32),
                   jax.ShapeDtypeStruct((1, D), F32), jax.ShapeDtypeStruct((SG * SCH, SCH), F32),
                   jax.ShapeDtypeStruct((SCH, LANE), F32)],
        input_output_aliases={0: 0},
        compiler_params=_cparams(("arbitrary",)),
    )(dz, hf, hb, zmain, zmain, zmain, dyms, gh, ln_g, ln_b, ws, bs_t)


FCB = DFF // 2
TF = 512


def _ffn_halo(col, t):
    per = TF // GW
    last = t // GW - 1
    prev = pl.BlockSpec((GW, FCB), lambda i, j: (jnp.maximum(i * per - 1, 0), col(j)))
    nxt = pl.BlockSpec((GW, FCB), lambda i, j: (jnp.minimum((i + 1) * per, last), col(j)))
    return prev, nxt


def _conv_taps(ext):
    n = ext.shape[0]
    colid = lax.broadcasted_iota(jnp.int32, (n, 1), 0) % GW
    left = pltpu.roll(jnp.where(colid != GW - 1, ext, 0.0), 1, 0)
    right = pltpu.roll(jnp.where(colid != 0, ext, 0.0), n - 1, 0)
    views = (left, ext, right)
    return {(ky, kx): views[kx][GW * ky:GW * ky + TF] for ky in range(3) for kx in range(3)}


def _ext(c_ref, p_ref, n_ref, i, nt):
    pr = jnp.where(i == 0, 0.0, p_ref[...].astype(F32))
    nx = jnp.where(i == nt - 1, 0.0, n_ref[...].astype(F32))
    return jnp.concatenate([pr, c_ref[...].astype(F32), nx], axis=0)


def ffn_act_fwd(up, wc):
    t = up.shape[0]
    nt = t // TF
    prev_s, next_s = _ffn_halo(lambda j: j, t)

    def body(a_ref, ap_ref, an_ref, b_ref, w_ref, o_ref, ac_ref):
        i = pl.program_id(0)
        taps = _conv_taps(_ext(a_ref, ap_ref, an_ref, i, nt))
        ac = sum(w_ref[3 * ky + kx:3 * ky + kx + 1, :] * taps[(ky, kx)] for ky in range(3) for kx in range(3))
        ac_ref[...] = ac.astype(BF16)
        o_ref[...] = (_silu(ac) * b_ref[...].astype(F32)).astype(BF16)

    spec = pl.BlockSpec((TF, FCB), lambda i, j: (i, j))
    return pl.pallas_call(
        body, name="ffn_act_fwd", grid=(nt, 2),
        in_specs=[spec, prev_s, next_s,
                  pl.BlockSpec((TF, FCB), lambda i, j: (i, 2 + j)), pl.BlockSpec((16, FCB), lambda i, j: (0, j))],
        out_specs=[spec, spec],
        out_shape=[jax.ShapeDtypeStruct((t, DFF), BF16), jax.ShapeDtypeStruct((t, DFF), BF16)],
        compiler_params=_cparams(("parallel", "parallel")),
    )(up, up, up, up, wc)


def ffn_act_bwd(up, ac, dact):
    t = up.shape[0]
    nt = t // TF

    def body(b_ref, ac_ref, da_ref, dup_ref, dac_ref):
        acv = ac_ref[...].astype(F32)
        da = da_ref[...].astype(F32)
        s = _sigmoid(acv)
        dup_ref[...] = (da * acv * s).astype(BF16)
        dac_ref[...] = (da * b_ref[...].astype(F32) * s * (1.0 + acv * (1.0 - s))).astype(BF16)

    spec = pl.BlockSpec((TF, FCB), lambda i, j: (i, j))
    bspec = pl.BlockSpec((TF, FCB), lambda i, j: (i, 2 + j))
    return pl.pallas_call(
        body, name="ffn_act_bwd", grid=(nt, 2),
        in_specs=[bspec, spec, spec],
        out_specs=[bspec, spec],
        out_shape=[jax.ShapeDtypeStruct((t, 2 * DFF), BF16), jax.ShapeDtypeStruct((t, DFF), BF16)],
        compiler_params=_cparams(("parallel", "parallel")),
    )(up, ac, dact)


def ffn_conv_bwd(dup, up, dac, wc):
    t = up.shape[0]
    nt = t // TF
    prev_g, next_g = _ffn_halo(lambda j: j, t)

    strip = 16

    def body(dup_in_ref, a_ref, g_ref, gp_ref, gn_ref, w_ref, o_ref, dw_ref, lv_ref, cv_ref, rv_ref, part_ref):
        del dup_in_ref
        i = pl.program_id(1)

        @pl.when(i == 0)
        def _():
            dw_ref[...] = jnp.zeros_like(dw_ref)

        ext = _ext(g_ref, gp_ref, gn_ref, i, nt)
        n = ext.shape[0]
        colid = lax.broadcasted_iota(jnp.int32, (n, 1), 0) % GW
        lv_ref[...] = pltpu.roll(jnp.where(colid != GW - 1, ext, 0.0), 1, 0)
        cv_ref[...] = ext
        rv_ref[...] = pltpu.roll(jnp.where(colid != 0, ext, 0.0), n - 1, 0)
        part_ref[...] = jnp.zeros_like(part_ref)
        views = (lv_ref, cv_ref, rv_ref)

        def one_strip(r, carry):
            r0 = pl.multiple_of(r * strip, strip)
            a = a_ref[pl.ds(r0, strip), :].astype(F32)
            acc = jnp.zeros((strip, FCB), F32)
            for ky in range(3):
                for kx in range(3):
                    kf = 3 * (2 - ky) + (2 - kx)
                    tap = views[kx][pl.ds(r0 + GW * ky, strip), :]
                    acc = acc + w_ref[kf:kf + 1, :] * tap
                    p = a * tap
                    part_ref[8 * kf:8 * kf + 8, :] += p[0:8] + p[8:16]
            o_ref[pl.ds(r0, strip), :] = acc.astype(BF16)
            return carry

        lax.fori_loop(0, TF // strip, one_strip, 0)
        for k in range(9):
            dw_ref[k:k + 1, :] += jnp.sum(part_ref[8 * k:8 * k + 8, :], axis=0, keepdims=True)

    sw = lambda s: pl.BlockSpec(s.block_shape, lambda j, i, f=s.index_map: f(i, j))
    spec = pl.BlockSpec((TF, FCB), lambda j, i: (i, j))
    return pl.pallas_call(
        body, name="ffn_conv_bwd", grid=(2, nt),
        in_specs=[pl.BlockSpec(memory_space=pl.ANY), spec, spec, sw(prev_g), sw(next_g),
                  pl.BlockSpec((16, FCB), lambda j, i: (0, j))],
        out_specs=[spec, pl.BlockSpec((16, FCB), lambda j, i: (0, j))],
        out_shape=[jax.ShapeDtypeStruct(dup.shape, BF16), jax.ShapeDtypeStruct((16, DFF), F32)],
        scratch_shapes=[pltpu.VMEM((TF + 2 * GW, FCB), F32)] * 3 + [pltpu.VMEM((72, FCB), F32)],
        input_output_aliases={0: 0},
        compiler_params=_cparams(("arbitrary", "arbitrary")),
    )(dup, up, dac, dac, dac, wc)


def head_fwd_bwd(h1, f, target, gfin, tab):
    t = h1.shape[0]

    def body(h1_ref, f_ref, t_ref, g_ref, tab_ref, dh2_ref, df_ref, acc_ref):
        i = pl.program_id(0)

        @pl.when(i == 0)
        def _():
            acc_ref[...] = jnp.zeros_like(acc_ref)

        gate = tab_ref[0:1, :]
        fv = f_ref[...].astype(F32)
        h2 = h1_ref[...] + gate * fv
        r = lax.rsqrt(jnp.mean(h2 * h2, axis=-1, keepdims=True) + EPS)
        xh = h2 * r
        gv = g_ref[...]
        err = xh * gv - t_ref[...]
        acc_ref[0:1, :] += jnp.sum(0.5 * jnp.mean(err * err, axis=-1, keepdims=True), axis=0, keepdims=True)
        dy = err * (1.0 / D)
        acc_ref[1:2, :] += jnp.sum(dy * xh, axis=0, keepdims=True)
        dxh = dy * gv
        dh2 = r * (dxh - xh * jnp.mean(dxh * xh, axis=-1, keepdims=True))
        dh2_ref[...] = dh2
        acc_ref[2:3, :] += jnp.sum(dh2 * fv, axis=0, keepdims=True)
        df_ref[...] = (dh2 * gate).astype(BF16)

    return pl.pallas_call(
        body, name="head_fwd_bwd", grid=(t // TR2,),
        in_specs=[_rb(TR2, D), _rb(TR2, D), _rb(TR2, D), _bc(1, D), _bc(8, D)],
        out_specs=[_rb(TR2, D), _rb(TR2, D), _bc(8, D)],
        out_shape=[jax.ShapeDtypeStruct((t, D), F32), jax.ShapeDtypeStruct((t, D), BF16),
                   jax.ShapeDtypeStruct((8, D), F32)],
        compiler_params=_cparams(("arbitrary",)),
    )(h1, f, target, gfin, tab)


def norm2_bwd(h1, dhn2, dh2, out, g, tab):
    t = h1.shape[0]

    def body(h1_ref, dhn_ref, dh2_ref, out_ref, g_ref, tab_ref, dh1_ref, dout_ref, acc_ref):
        i = pl.program_id(0)

        @pl.when(i == 0)
        def _():
            acc_ref[...] = jnp.zeros_like(acc_ref)

        h1v = h1_ref[...]
        r = lax.rsqrt(jnp.mean(h1v * h1v, axis=-1, keepdims=True) + EPS)
        xh = h1v * r
        gv = g_ref[...]
        dhn = dhn_ref[...].astype(F32)
        acc_ref[0:1, :] += jnp.sum(dhn, axis=0, keepdims=True)
        acc_ref[1:2, :] += jnp.sum(dhn * xh * gv, axis=0, keepdims=True)
        dn = dhn * (1.0 + tab_ref[2:3, :])
        acc_ref[2:3, :] += jnp.sum(dn * xh, axis=0, keepdims=True)
        dxh = dn * gv
        dh1 = dh2_ref[...] + r * (dxh - xh * jnp.mean(dxh * xh, axis=-1, keepdims=True))
        dh1_ref[...] = dh1
        acc_ref[3:4, :] += jnp.sum(dh1 * out_ref[...].astype(F32), axis=0, keepdims=True)
        dout_ref[...] = (dh1 * tab_ref[0:1, :]).astype(BF16)

    return pl.pallas_call(
        body, name="norm2_bwd", grid=(t // TR2,),
        in_specs=[_rb(TR2, D), _rb(TR2, D), _rb(TR2, D), _rb(TR2, D), _bc(1, D), _bc(8, D)],
        out_specs=[_rb(TR2, D), _rb(TR2, D), _bc(8, D)],
        out_shape=[jax.ShapeDtypeStruct((t, D), F32), jax.ShapeDtypeStruct((t, D), BF16),
                   jax.ShapeDtypeStruct((8, D), F32)],
        compiler_params=_cparams(("arbitrary",)),
    )(h1, dhn2, dh2, out, g, tab)


def norm1_bwd(ctx, x, da, db, dh1, g, tab, n_ctx_tiles):
    t = x.shape[0]
    tp = t + ctx.shape[0]
    xrow = lambda i: jnp.maximum(i - n_ctx_tiles, 0)

    def body(c_ref, x_ref, da_ref, db_ref, dh1_ref, g_ref, tab_ref, dx_ref, acc_ref):
        i = pl.program_id(0)

        @pl.when(i == 0)
        def _():
            acc_ref[...] = jnp.zeros_like(acc_ref)

        is_ctx = i < n_ctx_tiles
        x = jnp.where(is_ctx, c_ref[...], x_ref[...])
        r = lax.rsqrt(jnp.mean(x * x, axis=-1, keepdims=True) + EPS)
        xh = x * r
        gv = g_ref[...]
        dhn = da_ref[...].astype(F32) + db_ref[...].astype(F32)
        s_shift = jnp.sum(dhn, axis=0, keepdims=True)
        s_scale = jnp.sum(dhn * xh * gv, axis=0, keepdims=True)

        @pl.when(is_ctx)
        def _():
            acc_ref[0:1, :] += s_shift
            acc_ref[1:2, :] += s_scale

        @pl.when(jnp.logical_not(is_ctx))
        def _():
            acc_ref[2:3, :] += s_shift
            acc_ref[3:4, :] += s_scale

        acc_ref[5:6, :] += s_shift
        acc_ref[6:7, :] += s_scale
        sc = jnp.where(is_ctx, tab_ref[1:2, :], tab_ref[3:4, :])
        dn = dhn * (1.0 + sc)
        acc_ref[4:5, :] += jnp.sum(dn * xh, axis=0, keepdims=True)
        dxh = dn * gv
        dx_ref[...] = dh1_ref[...] + r * (dxh - xh * jnp.mean(dxh * xh, axis=-1, keepdims=True))

    return pl.pallas_call(
        body, name="norm1_bwd", grid=(tp // TR,),
        in_specs=_ctx_x_specs(n_ctx_tiles) + [_rb(TR, D), _rb(TR, D), pl.BlockSpec((TR, D), lambda i: (xrow(i), 0)),
                                              _bc(1, D), _bc(8, D)],
        out_specs=[pl.BlockSpec((TR, D), lambda i: (xrow(i), 0)), _bc(8, D)],
        out_shape=[jax.ShapeDtypeStruct((t, D), F32), jax.ShapeDtypeStruct((8, D), F32)],
        compiler_params=_cparams(("arbitrary",)),
    )(ctx, x, da, db, dh1, g, tab)


def adamw(w, g, m, v, name):
    lead = w.ndim - 2
    rows, cols = w.shape[-2:]
    tm = max(t for t in range(8, rows + 1, 8) if rows % t == 0 and (t * cols <= 512 * 1024 or t == 8))
    c1 = 1.0 / (1.0 - ADAM_B1 ** ADAM_STEP)
    c2 = 1.0 / (1.0 - ADAM_B2 ** ADAM_STEP)

    def body(w_ref, g_ref, m_ref, v_ref, d_ref, mo_ref, vo_ref):
        gv = g_ref[...]
        mn = ADAM_B1 * m_ref[...] + (1.0 - ADAM_B1) * gv
        vn = ADAM_B2 * v_ref[...] + (1.0 - ADAM_B2) * (gv * gv)
        mo_ref[...] = mn
        vo_ref[...] = vn
        d_ref[...] = -ADAM_LR * ((mn * c1) / (jnp.sqrt(vn * c2) + ADAM_EPS) + ADAM_WD * w_ref[...])

    spec = pl.BlockSpec((1,) * lead + (tm, cols), lambda i: (0,) * lead + (i, 0))
    sds = jax.ShapeDtypeStruct(w.shape, F32)
    return pl.pallas_call(
        body, name=name, grid=(rows // tm,),
        in_specs=[spec] * 4, out_specs=[spec] * 3, out_shape=[sds] * 3,
        compiler_params=_cparams(("parallel",)),
    )(w, g, m, v)


def add_n(arrs, out_dtype, name):
    shp = arrs[0].shape
    cols = shp[-1]
    flat = [a.reshape(-1, cols) for a in arrs]
    rows = flat[0].shape[0]
    tm = max(t for t in range(16, rows + 1, 16) if rows % t == 0 and t * cols <= 512 * 1024)

    def body(*refs):
        acc = refs[0][...].astype(F32)
        for r in refs[1:-1]:
            acc = acc + r[...].astype(F32)
        refs[-1][...] = acc.astype(refs[-1].dtype)

    spec = pl.BlockSpec((tm, cols), lambda i: (i, 0))
    out = pl.pallas_call(
        body, name=name, grid=(rows // tm,),
        in_specs=[spec] * len(flat), out_specs=spec, out_shape=jax.ShapeDtypeStruct((rows, cols), out_dtype),
        compiler_params=_cparams(("parallel",)),
    )(*flat)
    return out.reshape(shp)


def sum8(stack, name):
    _, rows, cols = stack.shape
    tm = rows if rows <= 2048 else _pick(rows, (512, 256, 128, 64, 8))

    def body(s_ref, o_ref):
        acc = s_ref[0]
        for k in range(1, 8):
            acc = acc + s_ref[k]
        o_ref[...] = acc

    return pl.pallas_call(
        body, name=name, grid=(rows // tm,),
        in_specs=[pl.BlockSpec((8, tm, cols), lambda i: (0, i, 0))],
        out_specs=pl.BlockSpec((tm, cols), lambda i: (i, 0)),
        out_shape=jax.ShapeDtypeStruct((rows, cols), F32),
        compiler_params=_cparams(("parallel",)),
    )(stack)


def _coords():
    return lax.axis_index("x"), lax.axis_index("y"), lax.axis_index("c")


def _other_chips(x, y):
    return [(1 - x, y), (x, 1 - y), (1 - x, 1 - y)]


_ANY = pl.BlockSpec(memory_space=pl.ANY)


def gather_chips(slabs):
    ns = len(slabs)

    def body(*refs):
        x_refs, out_refs = refs[:ns], refs[ns:2 * ns]
        send_sems, recv_sems = refs[2 * ns:]
        x, y, c = _coords()
        me = 2 * x + y
        sibling = (x, y, 1 - c)
        chips = _other_chips(x, y)

        def half(s, chip, hc):
            rh = slabs[s].shape[0] // 2
            return out_refs[s].at[chip, pl.ds(hc * rh, rh), :]

        def own_half(s):
            rh = slabs[s].shape[0] // 2
            return x_refs[s].at[pl.ds(c * rh, rh), :]

        sends = []
        for s in range(ns):
            for j, (px, py) in enumerate(chips):
                cp = pltpu.make_async_remote_copy(
                    src_ref=own_half(s), dst_ref=half(s, me, c), send_sem=send_sems.at[6 * s + j],
                    recv_sem=recv_sems.at[6 * s + j], device_id=(px, py, c), device_id_type=MESH)
                cp.start()
                sends.append(cp)
        for s in range(ns):
            for j, (px, py) in enumerate(chips):
                src = 2 * px + py
                landed = pltpu.make_async_remote_copy(
                    src_ref=half(s, src, c), dst_ref=half(s, src, c), send_sem=send_sems.at[6 * s + j],
                    recv_sem=recv_sems.at[6 * s + j], device_id=(px, py, c), device_id_type=MESH)
                landed.wait_recv()
                fw = pltpu.make_async_remote_copy(
                    src_ref=half(s, src, c), dst_ref=half(s, src, c), send_sem=send_sems.at[6 * s + 3 + j],
                    recv_sem=recv_sems.at[6 * s + 3 + j], device_id=sibling, device_id_type=MESH)
                fw.start()
                sends.append(fw)
        for s in range(ns):
            for j, (px, py) in enumerate(chips):
                src = 2 * px + py
                got = pltpu.make_async_remote_copy(
                    src_ref=half(s, src, 1 - c), dst_ref=half(s, src, 1 - c), send_sem=send_sems.at[6 * s + 3 + j],
                    recv_sem=recv_sems.at[6 * s + 3 + j], device_id=sibling, device_id_type=MESH)
                got.wait_recv()
        for cp in sends:
            cp.wait_send()

    return pl.pallas_call(
        body, name="gather_chips",
        in_specs=[_ANY] * ns, out_specs=[_ANY] * ns,
        out_shape=[jax.ShapeDtypeStruct((4,) + s.shape, s.dtype) for s in slabs],
        scratch_shapes=[pltpu.SemaphoreType.DMA((6 * ns,)), pltpu.SemaphoreType.DMA((6 * ns,))],
    )(*slabs)


def swap_halves(gss):
    ns = len(gss)

    def body(*refs):
        g_refs, out_refs = refs[:ns], refs[ns:2 * ns]
        send_sems, recv_sems = refs[2 * ns:]
        x, y, c = _coords()
        cps = []
        for s in range(ns):
            rh = gss[s].shape[1] // 2
            cp = pltpu.make_async_remote_copy(
                src_ref=g_refs[s].at[:, pl.ds((1 - c) * rh, rh), :], dst_ref=out_refs[s],
                send_sem=send_sems.at[s], recv_sem=recv_sems.at[s], device_id=(x, y, 1 - c), device_id_type=MESH)
            cp.start()
            cps.append(cp)
        for cp in cps:
            cp.wait()

    return pl.pallas_call(
        body, name="swap_halves",
        in_specs=[_ANY] * ns, out_specs=[_ANY] * ns,
        out_shape=[jax.ShapeDtypeStruct((4, g.shape[1] // 2, g.shape[2]), g.dtype) for g in gss],
        scratch_shapes=[pltpu.SemaphoreType.DMA((ns,)), pltpu.SemaphoreType.DMA((ns,))],
    )(*gss)


def join_halves(reds):
    ns = len(reds)

    def body(*refs):
        r_refs, out_refs = refs[:ns], refs[ns:2 * ns]
        send_sems, recv_sems = refs[2 * ns:]
        x, y, c = _coords()
        cps = []
        for s in range(ns):
            cp = pltpu.make_async_remote_copy(
                src_ref=r_refs[s], dst_ref=out_refs[s], send_sem=send_sems.at[s], recv_sem=recv_sems.at[s],
                device_id=(x, y, 1 - c), device_id_type=MESH)
            cp.start()
            cps.append(cp)
        for cp in cps:
            cp.wait()

    return pl.pallas_call(
        body, name="join_halves",
        in_specs=[_ANY] * ns, out_specs=[_ANY] * ns,
        out_shape=[jax.ShapeDtypeStruct(r.shape, r.dtype) for r in reds],
        scratch_shapes=[pltpu.SemaphoreType.DMA((ns,)), pltpu.SemaphoreType.DMA((ns,))],
    )(*reds)


def gather_all(vec, name):
    r, wd = vec.shape

    def body(v_ref, out_ref, send_sems, recv_sems):
        x, y, c = _coords()
        me = 4 * x + 2 * y + c
        cps = []
        for k in range(1, 8):
            mx, my, mc = (k >> 2) & 1, (k >> 1) & 1, k & 1
            peer = (x ^ mx, y ^ my, c ^ mc)
            cp = pltpu.make_async_remote_copy(
                src_ref=v_ref, dst_ref=out_ref.at[me],
                send_sem=send_sems.at[k - 1], recv_sem=recv_sems.at[k - 1], device_id=peer, device_id_type=MESH)
            cp.start()
            cps.append(cp)
        for k in range(1, 8):
            mx, my, mc = (k >> 2) & 1, (k >> 1) & 1, k & 1
            peer = (x ^ mx, y ^ my, c ^ mc)
            src = 4 * peer[0] + 2 * peer[1] + peer[2]
            got = pltpu.make_async_remote_copy(
                src_ref=v_ref, dst_ref=out_ref.at[src],
                send_sem=send_sems.at[k - 1], recv_sem=recv_sems.at[k - 1], device_id=peer, device_id_type=MESH)
            got.wait_recv()
        for cp in cps:
            cp.wait_send()

    return pl.pallas_call(
        body, name=name,
        in_specs=[_ANY], out_specs=_ANY,
        out_shape=jax.ShapeDtypeStruct((8, r, wd), vec.dtype),
        scratch_shapes=[pltpu.SemaphoreType.DMA((7,)), pltpu.SemaphoreType.DMA((7,))],
    )(vec)


def _pad_rows(a, rows):
    return jnp.pad(a, ((0, rows - a.shape[0]), (0, 0)))


def _pad_cols(a, cols):
    return jnp.pad(a, ((0, 0), (0, cols - a.shape[1])))


def local_step(x, c, ctx, c_ctx, target, wt, sm, late_weights=None, grad_hook=None):
    t, tc = x.shape[0], ctx.shape[0]
    tp = t + tc
    nct = tc // TR
    ncc = tc // LC
    nc = tp // LC

    w_in = wt["w_in"]
    segs = {"q": (0, D), "k": (D, 2 * D), "v": (2 * D, 3 * D), "g": (3 * D, 3 * D + NGATE)}
    base = 3 * D + NGATE
    for n_i, nm in enumerate(("o", "u", "vg", "gm", "gg")):
        segs[nm] = (base + n_i * D, base + (n_i + 1) * D)
    order = ("o", "u", "vg", "gm", "gg", "v", "q", "k")
    w_main = jnp.concatenate([w_in[:, segs[nm][0]:segs[nm][1]] for nm in order], axis=1)
    w_g = _pad_cols(w_in[:, segs["g"][0]:segs["g"][1]], LANE)

    cc = _pad_rows(jnp.concatenate([c.reshape(1, D), c_ctx.reshape(1, D)], axis=0), 16)
    modv = mod_fwd(cc, wt["w_mod"][0], wt["w_mod"][1], sm["b_mod"].reshape(1, NMOD * D))
    mx = modv[0].reshape(NMOD, D)
    mc = modv[1].reshape(NMOD, D)
    tab1 = _pad_rows(jnp.stack([mc[0], mc[1], mx[0], mx[1]]), 8)
    tab2 = _pad_rows(jnp.stack([mx[2], mx[3], mx[4]]), 8)
    tab3 = _pad_rows(mx[5:6], 8)

    g1 = sm["norm1_g"].reshape(1, D)
    g2 = sm["norm2_g"].reshape(1, D)
    gfin = sm["final_g"].reshape(1, D)
    gh = sm["head_norm_g"].reshape(1, D)
    ln_g = sm["sgu_ln_g"].reshape(1, D)
    ln_b = sm["sgu_ln_b"].reshape(1, D)
    ws = sm["w_s"].reshape(SG * SCH, SCH).astype(BF16)
    bs_t = _pad_cols(sm["b_s"].reshape(SG, SCH).T, LANE)
    conv_w = _pad_rows(sm["conv_qk"].reshape(3, 2 * D), 8)
    b_gate = _pad_cols(sm["b_gate"].reshape(1, NGATE), LANE)
    wc = _pad_rows(sm["w_ffn_conv"].reshape(9, DFF), 16)

    hn1 = norm1_fwd(ctx, x, g1, tab1, nct)
    if late_weights is None:
        zmain = mm_nn(hn1, w_main, BF16, "mm_zmain")
    else:
        zmain, landed = mm_nn(hn1, w_main, BF16, "mm_zmain", rider=late_weights[0])
        wt = {**wt, **late_weights[1](landed)}
    zg = mm_nn(hn1, w_g, F32, "mm_zg")
    qa, ka, gcol, zgb = qkconv_fwd(zmain, zg, conv_w, b_gate, nct)
    grow = gcol[:, :16].reshape(nc, LC, 16).transpose(0, 2, 1)
    hf, hb, csf, csb, nsf, nsb, msf, msb = mlstm_fwd(qa, ka, zmain, gcol, grow, ncc)
    yms = mixer_fwd(hf, hb, zmain, gh, ln_g, ln_b, ws, bs_t, nct)
    w_br = jnp.stack([wt["w_branch_mlstm"], wt["w_branch_sgu"]])
    pp = mm_nn(yms, w_br, BF16, "mm_branch")
    y = merge_fwd(zmain, pp, nct)
    out = mm_nn(y, wt["w_out"], BF16, "mm_out")
    h1, hn2 = norm2_fwd(x, out, g2, tab2)
    up = mm_nn(hn2, wt["w_up"], BF16, "mm_up")
    act, ac = ffn_act_fwd(up, wc)
    f = mm_nn(act, wt["w_down"], BF16, "mm_down")
    dh2, df, acc_h = head_fwd_bwd(h1, f, target, gfin, tab3)
    loss = acc_h[0, 0]

    g_w_down = mm_tn(act, df, "mmt_down", BF16)
    dact = mm_nn(df, wt["w_down"], BF16, "mm_ddown", b_transposed=True)
    dup, dac = ffn_act_bwd(up, ac, dact)
    dup, g_wc = ffn_conv_bwd(dup, up, dac, wc)
    g_w_up = mm_tn(hn2, dup, "mmt_up", BF16)
    dhn2 = mm_nn(dup, wt["w_up"], BF16, "mm_dup", b_transposed=True)
    dh1, dout, acc_2 = norm2_bwd(h1, dhn2, dh2, out, g2, tab2)
    g_w_out = mm_tn(y, dout, "mmt_out", BF16)
    dy = mm_nn(dout, wt["w_out"], BF16, "mm_dout", b_transposed=True)
    dpp, dz = merge_bwd(zmain, pp, dy, tp, nct)
    g_w_br = mm_tn(yms, dpp, "mmt_branch", BF16)
    dyms = mm_nn(dpp, w_br, BF16, "mm_dbranch", b_transposed=True)
    dz, dhm, g_gh, g_lng, g_lnb, g_ws, g_bs = mixer_bwd(dz, hf, hb, zmain, dyms, gh, ln_g, ln_b, ws, bs_t, nct)
    (dqf, dkf, dvf, colf, rowf, dqb, dkb, dvb, colb, rowb) = mlstm_bwd(
        qa, ka, zmain, gcol, grow, (csf, csb, nsf, nsb, msf, msb), dhm, ncc)

    csum_f = rowf[:, :4, :].transpose(0, 2, 1).reshape(tp, 4)
    csum_b = rowb[:, :4, :].transpose(0, 2, 1).reshape(tp, 4)
    cs = _pad_cols(jnp.concatenate([csum_f, csum_f, csum_b, csum_b], axis=1), LANE)
    dzg, g_bgate = gates_bwd(colf, colb, cs, zgb)

    dc, g_convw = qkconv_bwd_a(zmain, dqf, dqb, dkf, dkb, conv_w, nct)
    dz = qkconv_bwd_b(dz, dc, conv_w, nct)
    dz = add_into_dz(dz, dvf, dvb, CB_V)

    g_w_main = mm_tn(hn1, dz, "mmt_main", BF16)
    g_w_g = mm_tn(hn1, dzg, "mmt_g", BF16)
    blk = lambda cb: g_w_main[:, cb * D:(cb + 1) * D]
    g_w_in = jnp.concatenate([blk(CB_Q), blk(CB_K), blk(CB_V), g_w_g[:, :NGATE], blk(CB_O), blk(CB_U), blk(CB_VG),
                              blk(CB_GM), blk(CB_GG)], axis=1)
    big = {"w_in": g_w_in, "w_branch_mlstm": g_w_br[0], "w_branch_sgu": g_w_br[1], "w_out": g_w_out,
           "w_up": g_w_up, "w_down": g_w_down}
    if grad_hook is None:
        da, received = mm_nn(dz, w_main, BF16, "mm_dmain", b_transposed=True), None
    else:
        da, received = mm_nn(dz, w_main, BF16, "mm_dmain", rider=grad_hook(big), b_transposed=True)
    db = mm_nn(dzg, w_g, BF16, "mm_dg", b_transposed=True)
    grad_x, acc_1 = norm1_bwd(ctx, x, da, db, dh1, g1, tab1, nct)

    d_modx = jnp.concatenate([acc_1[2], acc_1[3], acc_2[3], acc_2[0], acc_2[1], acc_h[2]])
    d_modc = jnp.concatenate([acc_1[0], acc_1[1], jnp.zeros((4 * D,), F32)])
    d_modb = jnp.concatenate([acc_1[5], acc_1[6], acc_2[3], acc_2[0], acc_2[1], acc_h[2]])

    small = {"b_mod": d_modb, "norm1_g": acc_1[4], "b_gate": g_bgate[0, :NGATE], "conv_qk": g_convw[:3].reshape(-1),
             "head_norm_g": g_gh[0], "sgu_ln_g": g_lng[0], "sgu_ln_b": g_lnb[0], "w_s": g_ws.reshape(-1),
             "b_s": g_bs[:, :SG].T.reshape(-1), "norm2_g": acc_2[2], "w_ffn_conv": g_wc[:9].reshape(-1),
             "final_g": acc_h[1]}
    return loss, grad_x, big, small, d_modx, d_modc, received


def mod_bwd_w(a_all, dm_all, name):
    n = dm_all.shape[1]
    tn = _pick(n, (512, 128))

    def body(a_ref, d_ref, o_ref):
        o_ref[...] = _dot_tn(_silu(a_ref[...]), d_ref[...])

    return pl.pallas_call(
        body, name=name, grid=(n // tn,),
        in_specs=[_bc(16, D), pl.BlockSpec((16, tn), lambda j: (0, j))],
        out_specs=pl.BlockSpec((D, tn), lambda j: (0, j)),
        out_shape=jax.ShapeDtypeStruct((D, n), F32),
        compiler_params=_cparams(("parallel",)),
    )(a_all, dm_all)


def mod_bwd_cctx(dmc, slabs, w_cols, c_ctx):
    per = w_cols // MODB
    nb = 2 * D // MODB

    def body(d_ref, w_ref, c_ref, o_ref):
        j = pl.program_id(0)

        @pl.when(j == 0)
        def _():
            o_ref[...] = jnp.zeros_like(o_ref)

        o_ref[...] += _dot_nt(d_ref[...], w_ref[0])

        @pl.when(j == nb - 1)
        def _():
            o_ref[...] = o_ref[...] * _dsilu(c_ref[...])

    return pl.pallas_call(
        body, name="mod_bwd_cctx", grid=(nb,),
        in_specs=[pl.BlockSpec((16, MODB), lambda j: (0, j)),
                  pl.BlockSpec((1, D, MODB), lambda j: (j // per, 0, j % per)), _bc(1, D)],
        out_specs=_bc(16, D),
        out_shape=jax.ShapeDtypeStruct((16, D), F32),
        compiler_params=_cparams(("arbitrary",)),
    )(dmc, slabs, c_ctx)


BIG = ("w_mod", "w_in", "w_branch_mlstm", "w_branch_sgu", "w_out", "w_up", "w_down")
BIG_AXIS = {"w_mod": 1, "w_in": 1, "w_branch_mlstm": 0, "w_branch_sgu": 0, "w_out": 0, "w_up": 1, "w_down": 0}
SMALL = ("c_ctx", "b_mod", "norm1_g", "b_gate", "conv_qk", "head_norm_g", "sgu_ln_g", "sgu_ln_b", "w_s", "b_s",
         "norm2_g", "w_ffn_conv", "final_g")
SMALL_SHARDED = {"conv_qk": (3, 2 * D), "w_ffn_conv": (9, DFF)}
PACK_ALIGN = 32 * D


def _pack(arrs, dtype, align=PACK_ALIGN, width=D):
    flat = jnp.concatenate([a.reshape(-1).astype(dtype) for a in arrs])
    n = flat.shape[0]
    padded = -(-n // align) * align
    return jnp.pad(flat, (0, padded - n)).reshape(padded // width, width)


def _unpack(slab, shapes):
    flat = slab.reshape(-1)
    outs, off = [], 0
    for shp in shapes:
        n = math.prod(shp)
        outs.append(flat[off:off + n].reshape(shp))
        off += n
    return outs


def _round_up(n, m):
    return -(-n // m) * m


def kernel(x, c, ctx, c_ctx, w_mod, b_mod, norm1_g, w_in, b_gate, conv_qk, head_norm_g, sgu_ln_g, sgu_ln_b, w_s, b_s, w_branch_mlstm, w_branch_sgu, w_out, norm2_g, w_up, w_ffn_conv, w_down, final_g, loss_target, m_c_ctx, m_w_mod, m_b_mod, m_norm1_g, m_w_in, m_b_gate, m_conv_qk, m_head_norm_g, m_sgu_ln_g, m_sgu_ln_b, m_w_s, m_b_s, m_w_branch_mlstm, m_w_branch_sgu, m_w_out, m_norm2_g, m_w_up, m_w_ffn_conv, m_w_down, m_final_g, v_c_ctx, v_w_mod, v_b_mod, v_norm1_g, v_w_in, v_b_gate, v_conv_qk, v_head_norm_g, v_sgu_ln_g, v_sgu_ln_b, v_w_s, v_b_s, v_w_branch_mlstm, v_w_branch_sgu, v_w_out, v_norm2_g, v_w_up, v_w_ffn_conv, v_w_down, v_final_g):
    params = dict(c_ctx=c_ctx, w_mod=w_mod, b_mod=b_mod, norm1_g=norm1_g, w_in=w_in, b_gate=b_gate, conv_qk=conv_qk,
                  head_norm_g=head_norm_g, sgu_ln_g=sgu_ln_g, sgu_ln_b=sgu_ln_b, w_s=w_s, b_s=b_s,
                  w_branch_mlstm=w_branch_mlstm, w_branch_sgu=w_branch_sgu, w_out=w_out, norm2_g=norm2_g, w_up=w_up,
                  w_ffn_conv=w_ffn_conv, w_down=w_down, final_g=final_g)
    mom_m = dict(c_ctx=m_c_ctx, w_mod=m_w_mod, b_mod=m_b_mod, norm1_g=m_norm1_g, w_in=m_w_in, b_gate=m_b_gate,
                 conv_qk=m_conv_qk, head_norm_g=m_head_norm_g, sgu_ln_g=m_sgu_ln_g, sgu_ln_b=m_sgu_ln_b, w_s=m_w_s,
                 b_s=m_b_s, w_branch_mlstm=m_w_branch_mlstm, w_branch_sgu=m_w_branch_sgu, w_out=m_w_out,
                 norm2_g=m_norm2_g, w_up=m_w_up, w_ffn_conv=m_w_ffn_conv, w_down=m_w_down, final_g=m_final_g)
    mom_v = dict(c_ctx=v_c_ctx, w_mod=v_w_mod, b_mod=v_b_mod, norm1_g=v_norm1_g, w_in=v_w_in, b_gate=v_b_gate,
                 conv_qk=v_conv_qk, head_norm_g=v_head_norm_g, sgu_ln_g=v_sgu_ln_g, sgu_ln_b=v_sgu_ln_b, w_s=v_w_s,
                 b_s=v_b_s, w_branch_mlstm=v_w_branch_mlstm, w_branch_sgu=v_w_branch_sgu, w_out=v_w_out,
                 norm2_g=v_norm2_g, w_up=v_w_up, w_ffn_conv=v_w_ffn_conv, w_down=v_w_down, final_g=v_final_g)
    chip = 2 * lax.axis_index("x") + lax.axis_index("y")

    shard2d = {n: params[n].reshape(params[n].shape[-2:]) for n in BIG}
    conv_sh = conv_qk.reshape(3, -1)
    fconv_sh = w_ffn_conv.reshape(9, -1)

    dev = 2 * chip + lax.axis_index("c")

    first_names, row_names = ("w_mod", "w_in"), ("w_branch_mlstm", "w_branch_sgu", "w_out", "w_down")
    first_w = [shard2d[n].shape[1] for n in first_names]
    row_h = [shard2d[n].shape[0] for n in row_names]
    first_slab = _pad_cols(jnp.concatenate([shard2d[n].astype(BF16) for n in first_names], axis=1),
                           _round_up(sum(first_w), LANE))
    up_slab = shard2d["w_up"].astype(BF16)
    row_slab = jnp.concatenate([shard2d[n].astype(BF16) for n in row_names], axis=0)

    def own_in(slab, gathered):
        return jnp.where((jnp.arange(4) == chip)[:, None, None], slab[None], gathered)

    first_all = own_in(first_slab, gather_chips([first_slab])[0])
    wt = {"w_mod": (first_all, first_w[0]),
          "w_in": jnp.concatenate([first_all[j, :, first_w[0]:first_w[0] + first_w[1]] for j in range(4)], axis=1)}

    late_slabs = [up_slab, row_slab]

    def late_copies(in_refs, out_refs, send_sems, recv_sems):
        xx, yy, cc = _coords()
        me = 2 * xx + yy
        cps = []
        for s, slab in enumerate(late_slabs):
            rh = slab.shape[0] // 2
            for j, (px, py) in enumerate(_other_chips(xx, yy)):
                for o in range(2):
                    k = 6 * s + 2 * j + o
                    cps.append(pltpu.make_async_remote_copy(
                        src_ref=in_refs[s].at[pl.ds(cc * rh, rh), :], dst_ref=out_refs[s].at[me, pl.ds(cc * rh, rh), :],
                        send_sem=send_sems.at[k], recv_sem=recv_sems.at[k],
                        device_id=(px, py, cc if o == 0 else 1 - cc), device_id_type=MESH))
        return cps

    def late_finish(landed):
        up_all, row_all = own_in(up_slab, landed[0]), own_in(row_slab, landed[1])
        got = {"w_up": jnp.concatenate([up_all[j] for j in range(4)], axis=1)}
        o = 0
        for n, ht in zip(row_names, row_h):
            got[n] = jnp.concatenate([row_all[j, o:o + ht, :] for j in range(4)], axis=0)
            o += ht
        return got

    late_rider = Rider(late_slabs, [jax.ShapeDtypeStruct((4,) + s_.shape, s_.dtype) for s_ in late_slabs],
                       6 * len(late_slabs), late_copies)

    cvec = _pack([conv_sh, fconv_sh], F32, align=8 * LANE, width=LANE)
    call = gather_all(cvec, "gather_conv")
    cparts = [_unpack(jnp.where(dev == 2 * j, cvec, call[2 * j]), [conv_sh.shape, fconv_sh.shape]) for j in range(4)]
    conv_full = jnp.concatenate([p[0] for p in cparts], axis=1)
    fconv_full = jnp.concatenate([p[1] for p in cparts], axis=1)

    sm = dict(b_mod=b_mod, norm1_g=norm1_g, b_gate=b_gate, conv_qk=conv_full, head_norm_g=head_norm_g,
              sgu_ln_g=sgu_ln_g, sgu_ln_b=sgu_ln_b, w_s=w_s, b_s=b_s, norm2_g=norm2_g, w_ffn_conv=fconv_full,
              final_g=final_g)

    gcol_names = ("w_up", "w_in")
    gcol_w = [shard2d[n].shape[1] for n in gcol_names]
    gcol_pad = _round_up(sum(gcol_w), LANE)
    cidx = lax.axis_index("c")
    kept = {}

    def grad_hook(gbig):
        def chip_cols(j):
            return _pad_cols(jnp.concatenate([gbig[n][:, j * wd:(j + 1) * wd] for n, wd in zip(gcol_names, gcol_w)],
                                             axis=1), gcol_pad)

        def chip_rows(j):
            return jnp.concatenate([gbig[n][j * ht:(j + 1) * ht] for n, ht in zip(row_names, row_h)], axis=0)

        gss = [jnp.stack([chip_cols(j) for j in range(4)]), jnp.stack([chip_rows(j) for j in range(4)])]
        from_sib = swap_halves(gss)
        pair_bf, own_terms = [], []
        for s, (gs, fs) in enumerate(zip(gss, from_sib)):
            rh = gs.shape[1] // 2
            my_half = lax.dynamic_slice_in_dim(gs, cidx * rh, rh, axis=1)
            pair_bf.append(add_n([my_half, fs], BF16, "pair_sum_%d" % s))
            own_terms.append([lax.dynamic_index_in_dim(my_half, chip, axis=0, keepdims=False),
                              lax.dynamic_index_in_dim(fs, chip, axis=0, keepdims=False)])
        kept["own_terms"] = own_terms

        def scatter_copies(in_refs, out_refs, send_sems, recv_sems):
            xx, yy, cc = _coords()
            cps = []
            for s in range(len(pair_bf)):
                for j, (px, py) in enumerate(_other_chips(xx, yy)):
                    cps.append(pltpu.make_async_remote_copy(
                        src_ref=in_refs[s].at[2 * px + py], dst_ref=out_refs[s].at[j],
                        send_sem=send_sems.at[3 * s + j], recv_sem=recv_sems.at[3 * s + j],
                        device_id=(px, py, cc), device_id_type=MESH))
            return cps

        return Rider(pair_bf, [jax.ShapeDtypeStruct((3,) + p.shape[1:], p.dtype) for p in pair_bf],
                     3 * len(pair_bf), scatter_copies)

    loss_l, grad_x, _, gsmall, d_modx, d_modc, recv = local_step(
        x[0], c, ctx[0], c_ctx, loss_target[0], wt, sm, late_weights=(late_rider, late_finish), grad_hook=grad_hook)

    reds = [add_n(kept["own_terms"][s] + [recv[s][0], recv[s][1], recv[s][2]], F32, "chip_sum_%d" % s)
            for s in range(2)]
    others = join_halves(reds)
    full_red = [jnp.where(cidx == 0, jnp.concatenate([m, o], axis=0), jnp.concatenate([o, m], axis=0))
                for m, o in zip(reds, others)]
    g_shard = {}
    off = 0
    for n, wd in zip(gcol_names, gcol_w):
        g_shard[n] = full_red[0][:, off:off + wd]
        off += wd
    off = 0
    for n, ht in zip(row_names, row_h):
        g_shard[n] = full_red[1][off:off + ht]
        off += ht

    small_order = ("b_mod", "norm1_g", "b_gate", "conv_qk", "head_norm_g", "sgu_ln_g", "sgu_ln_b", "w_s", "b_s", "norm2_g",
                   "w_ffn_conv", "final_g")
    vec_parts = [gsmall[n] for n in small_order] + [d_modx, d_modc, c.reshape(-1), loss_l.reshape(1)]
    vec_shapes = [a.shape for a in vec_parts]
    vec = _pack(vec_parts, F32, align=8 * LANE, width=LANE)
    allv = gather_all(vec, "gather_small")
    allv = jnp.where((jnp.arange(8) == dev)[:, None, None], vec[None], allv)
    summed = sum8(allv, "small_sum")
    s_parts = _unpack(summed, vec_shapes)
    g_small = dict(zip(small_order, s_parts[:len(small_order)]))
    dmc_sum = s_parts[len(small_order) + 1]
    loss = s_parts[-1][0]
    flat_all = allv.reshape(8, -1)
    starts = [0]
    for shp_ in vec_shapes:
        starts.append(starts[-1] + math.prod(shp_))
    i_dmx, i_c = len(small_order), len(small_order) + 2
    dmx_all = flat_all[:, starts[i_dmx]:starts[i_dmx + 1]]
    c_all = flat_all[:, starts[i_c]:starts[i_c + 1]]

    a_all = _pad_rows(jnp.concatenate([c_all, c_ctx.reshape(1, D)], axis=0), 16)
    dm_all = _pad_rows(jnp.concatenate([dmx_all, dmc_sum.reshape(1, NMOD * D)], axis=0), 16)
    ncol = NMOD * D // 4
    dm_shard = lax.dynamic_slice_in_dim(dm_all, chip * ncol, ncol, axis=1)
    g_shard["w_mod"] = mod_bwd_w(a_all, dm_shard, "mod_bwd_w")
    g_cctx = mod_bwd_cctx(_pad_rows(dmc_sum[:2 * D].reshape(1, 2 * D), 16), first_all, first_w[0],
                          c_ctx.reshape(1, D))[0]
    g_small["c_ctx"] = g_cctx

    results = {}
    for n in BIG:
        shp = params[n].shape
        g_ = g_shard[n].reshape(shp)
        d_, m_, v_ = adamw(params[n], g_, mom_m[n], mom_v[n], "adamw_" + n)
        results[n] = (g_, d_, m_, v_)

    conv_g = lax.dynamic_slice_in_dim(g_small["conv_qk"].reshape(3, 2 * D), chip * (2 * D // 4), 2 * D // 4, axis=1)
    fconv_g = lax.dynamic_slice_in_dim(g_small["w_ffn_conv"].reshape(9, DFF), chip * (DFF // 4), DFF // 4, axis=1)
    g_small["conv_qk"] = conv_g
    g_small["w_ffn_conv"] = fconv_g
    w_list = [params[n].reshape(-1) for n in SMALL]
    g_list = [g_small[n].reshape(-1) for n in SMALL]
    m_list = [mom_m[n].reshape(-1) for n in SMALL]
    v_list = [mom_v[n].reshape(-1) for n in SMALL]
    sm_shapes = [params[n].shape for n in SMALL]
    pk = lambda lst: _pack(lst, F32, align=8 * LANE, width=LANE)
    gp = pk(g_list)
    d_s, m_s, v_s = adamw(pk(w_list), gp, pk(m_list), pk(v_list), "adamw_small")
    for n, gg, dd, mm, vv in zip(SMALL, _unpack(gp, sm_shapes), _unpack(d_s, sm_shapes), _unpack(m_s, sm_shapes),
                                 _unpack(v_s, sm_shapes)):
        results[n] = (gg, dd, mm, vv)

    order = ("c_ctx", "w_mod", "b_mod", "norm1_g", "w_in", "b_gate", "conv_qk", "head_norm_g", "sgu_ln_g", "sgu_ln_b",
             "w_s", "b_s", "w_branch_mlstm", "w_branch_sgu", "w_out", "norm2_g", "w_up", "w_ffn_conv", "w_down",
             "final_g")
    outs = [loss, grad_x[None]]
    for k in range(4):
        outs += [results[n][k] for n in order]
    return tuple(outs)
```

```python
import functools
import math

import jax
import jax.numpy as jnp
from jax import lax
from jax.experimental import pallas as pl
from jax.experimental.pallas import tpu as pltpu

F32 = jnp.float32
BF16 = jnp.bfloat16

D = 1024
NH = 4
DH = 256
LC = 256
GW = 64
SG = 4
SGD = 256
SCH = 128
DFF = 2816
NMOD = 6
NGATE = 16
NIN = 8208
EPS = 1e-6
M_INIT = -1e30
TR = 256
TR2 = 512
LANE = 128
VMEM_LIMIT = 56 * 1024 * 1024
MESH = pl.DeviceIdType.MESH

ADAM_LR = 0.001
ADAM_B1 = 0.9
ADAM_B2 = 0.999
ADAM_EPS = 1e-08
ADAM_WD = 0.01
ADAM_STEP = 10

CB_O, CB_U, CB_VG, CB_GM, CB_GG, CB_V, CB_Q, CB_K = range(8)


def _pick(n, cands):
    for c in cands:
        if n % c == 0:
            return c
    return n


def _cparams(sem):
    return pltpu.CompilerParams(dimension_semantics=sem, vmem_limit_bytes=VMEM_LIMIT)


def _sigmoid(x):
    return 1.0 / (1.0 + jnp.exp(-x))


def _silu(x):
    return x * _sigmoid(x)


def _dsilu(x):
    s = _sigmoid(x)
    return s * (1.0 + x * (1.0 - s))


_GC = math.sqrt(2.0 / math.pi)


def _gelu(x):
    return 0.5 * x * (1.0 + jnp.tanh(_GC * (x + 0.044715 * x * x * x)))


def _gelu_and_grad(x):
    x2 = x * x
    t = jnp.tanh(_GC * x * (1.0 + 0.044715 * x2))
    half = 0.5 * (1.0 + t)
    return x * half, half + 0.5 * x * (1.0 - t * t) * _GC * (1.0 + 3.0 * 0.044715 * x2)


def _dot(a, b):
    return jnp.dot(a.astype(BF16), b.astype(BF16), preferred_element_type=F32)


def _dot_nt(a, b):
    return lax.dot_general(a.astype(BF16), b.astype(BF16), (((1,), (1,)), ((), ())), preferred_element_type=F32)


def _dot_tn(a, b):
    return lax.dot_general(a.astype(BF16), b.astype(BF16), (((0,), (0,)), ((), ())), preferred_element_type=F32)


def _dot_tn_mxu(a, b):
    m = a.shape[1]
    eye = (lax.broadcasted_iota(jnp.int32, (m, m), 0) == lax.broadcasted_iota(jnp.int32, (m, m), 1)).astype(BF16)
    return _dot(_dot_nt(eye, a), b)


def _exact_dot(tri, x):
    x1 = x.astype(BF16)
    r1 = x - x1.astype(F32)
    x2 = r1.astype(BF16)
    x3 = (r1 - x2.astype(F32)).astype(BF16)
    return (jnp.dot(tri, x1, preferred_element_type=F32) + jnp.dot(tri, x2, preferred_element_type=F32)
            + jnp.dot(tri, x3, preferred_element_type=F32))


def _rb(tm, w, col=0, off=0):
    return pl.BlockSpec((tm, w), lambda i: (i + off, col))


def _bc(r, w):
    return pl.BlockSpec((r, w), lambda i: (0, 0))


class Rider:
    def __init__(self, ins, out_shapes, n_sems, copies):
        self.ins, self.out_shapes, self.n_sems, self.copies = list(ins), list(out_shapes), n_sems, copies


def mm_nn(a, b, out_dtype, name, rider=None, b_transposed=False):
    squeeze = a.ndim == 2
    if squeeze:
        a, b = a[None], b[None]
    g, m, k = a.shape
    n = b.shape[1] if b_transposed else b.shape[2]
    dot = _dot_nt if b_transposed else _dot
    tm = _pick(m, (2048, 1280, 1024, 512, 256, 128))
    tn = _pick(n, (2048, 1408, 1024, 512, 128))
    tk = _pick(k, (2048, 1408, 1024, 512, 128))
    nk = k // tk
    grid = (g, n // tn, m // tm, nk)
    n_rin = len(rider.ins) if rider else 0
    n_rout = len(rider.out_shapes) if rider else 0

    def body(*refs):
        a_ref, b_ref = refs[0], refs[1]
        r_in = refs[2:2 + n_rin]
        o_ref = refs[2 + n_rin]
        r_out = refs[3 + n_rin:3 + n_rin + n_rout]
        scr = refs[3 + n_rin + n_rout:]
        if rider:
            ids = [pl.program_id(d) for d in range(4)]
            first = functools.reduce(jnp.logical_and, [i == 0 for i in ids])
            last = functools.reduce(jnp.logical_and, [i == e - 1 for i, e in zip(ids, grid)])
            send_sems, recv_sems = scr[-2], scr[-1]

            @pl.when(first)
            def _():
                for cp in rider.copies(r_in, r_out, send_sems, recv_sems):
                    cp.start()

        if nk == 1:
            o_ref[0] = dot(a_ref[0], b_ref[0]).astype(o_ref.dtype)
        else:
            acc_ref = scr[0]
            kk = pl.program_id(3)

            @pl.when(kk == 0)
            def _():
                acc_ref[...] = jnp.zeros_like(acc_ref)

            acc_ref[...] += dot(a_ref[0], b_ref[0])

            @pl.when(kk == nk - 1)
            def _():
                o_ref[0] = acc_ref[...].astype(o_ref.dtype)

        if rider:
            @pl.when(last)
            def _():
                for cp in rider.copies(r_in, r_out, send_sems, recv_sems):
                    cp.wait()

    scratch = [] if nk == 1 else [pltpu.VMEM((tm, tn), F32)]
    if rider:
        scratch += [pltpu.SemaphoreType.DMA((rider.n_sems,)), pltpu.SemaphoreType.DMA((rider.n_sems,))]
    outs = pl.pallas_call(
        body, name=name, grid=grid,
        in_specs=[pl.BlockSpec((1, tm, tk), lambda gi, j, i, kk: (gi, i, kk)),
                  (pl.BlockSpec((1, tn, tk), lambda gi, j, i, kk: (gi, j, kk)) if b_transposed else
                   pl.BlockSpec((1, tk, tn), lambda gi, j, i, kk: (gi, kk, j)))] + [_ANY] * n_rin,
        out_specs=[pl.BlockSpec((1, tm, tn), lambda gi, j, i, kk: (gi, i, j))] + [_ANY] * n_rout,
        out_shape=[jax.ShapeDtypeStruct((g, m, n), out_dtype)] + (rider.out_shapes if rider else []),
        scratch_shapes=scratch,
        compiler_params=_cparams(("arbitrary",) * 4 if rider else ("parallel", "parallel", "parallel", "arbitrary")),
    )(a, b, *(rider.ins if rider else []))
    out = outs[0][0] if squeeze else outs[0]
    return (out, list(outs[1:])) if rider else out


def mm_tn(a, b, name, out_dtype=F32):
    squeeze = a.ndim == 2
    if squeeze:
        a, b = a[None], b[None]
    g, t, ka = a.shape
    n = b.shape[2]
    tka = _pick(ka, (1024, 1408, 512, 128))
    tn = _pick(n, (2048, 1408, 1024, 512, 128))
    tt = _pick(t, (2048, 1280, 1024, 512, 256, 128))
    nt = t // tt

    def body(a_ref, b_ref, o_ref, acc_ref):
        tt_i = pl.program_id(3)

        @pl.when(tt_i == 0)
        def _():
            acc_ref[...] = jnp.zeros_like(acc_ref)

        acc_ref[...] += _dot_tn(a_ref[0], b_ref[0])

        @pl.when(tt_i == nt - 1)
        def _():
            o_ref[0] = acc_ref[...].astype(o_ref.dtype)

    out = pl.pallas_call(
        body, name=name, grid=(g, ka // tka, n // tn, nt),
        in_specs=[pl.BlockSpec((1, tt, tka), lambda gi, i, j, ti: (gi, ti, i)),
                  pl.BlockSpec((1, tt, tn), lambda gi, i, j, ti: (gi, ti, j))],
        out_specs=pl.BlockSpec((1, tka, tn), lambda gi, i, j, ti: (gi, i, j)),
        out_shape=jax.ShapeDtypeStruct((g, ka, n), out_dtype),
        scratch_shapes=[pltpu.VMEM((tka, tn), F32)],
        compiler_params=_cparams(("parallel", "parallel", "parallel", "arbitrary")),
    )(a, b)
    return out[0] if squeeze else out


MODB = 512


def mod_fwd(cc, slabs, w_cols, b_mod):
    per = w_cols // MODB
    n = slabs.shape[0] * w_cols

    def body(c_ref, w_ref, b_ref, o_ref):
        o_ref[...] = _dot(_silu(c_ref[...]), w_ref[0]) + b_ref[...]

    return pl.pallas_call(
        body, name="mod_fwd", grid=(n // MODB,),
        in_specs=[_bc(16, D), pl.BlockSpec((1, D, MODB), lambda j: (j // per, 0, j % per)),
                  pl.BlockSpec((1, MODB), lambda j: (0, j))],
        out_specs=pl.BlockSpec((16, MODB), lambda j: (0, j)),
        out_shape=jax.ShapeDtypeStruct((16, n), F32),
        compiler_params=_cparams(("parallel",)),
    )(cc, slabs, b_mod)


def _ctx_x_specs(n_ctx_tiles):
    return [pl.BlockSpec((TR, D), lambda i: (jnp.minimum(i, n_ctx_tiles - 1), 0)),
            pl.BlockSpec((TR, D), lambda i: (jnp.maximum(i - n_ctx_tiles, 0), 0))]


def norm1_fwd(ctx, x, g, tab, n_ctx_tiles):
    tp = ctx.shape[0] + x.shape[0]

    def body(c_ref, x_ref, g_ref, tab_ref, o_ref):
        is_ctx = pl.program_id(0) < n_ctx_tiles
        x = jnp.where(is_ctx, c_ref[...], x_ref[...])
        r = lax.rsqrt(jnp.mean(x * x, axis=-1, keepdims=True) + EPS)
        nrm = x * r * g_ref[...]
        sh = jnp.where(is_ctx, tab_ref[0:1, :], tab_ref[2:3, :])
        sc = jnp.where(is_ctx, tab_ref[1:2, :], tab_ref[3:4, :])
        o_ref[...] = (nrm * (1.0 + sc) + sh).astype(BF16)

    return pl.pallas_call(
        body, name="norm1_fwd", grid=(tp // TR,),
        in_specs=_ctx_x_specs(n_ctx_tiles) + [_bc(1, D), _bc(8, D)],
        out_specs=_rb(TR, D),
        out_shape=jax.ShapeDtypeStruct((tp, D), BF16),
        compiler_params=_cparams(("parallel",)),
    )(ctx, x, g, tab)


def norm2_fwd(x, out, g, tab):
    t = x.shape[0]

    def body(x_ref, o_in_ref, g_ref, tab_ref, h1_ref, hn_ref):
        h1 = x_ref[...] + tab_ref[0:1, :] * o_in_ref[...].astype(F32)
        h1_ref[...] = h1
        r = lax.rsqrt(jnp.mean(h1 * h1, axis=-1, keepdims=True) + EPS)
        nrm = h1 * r * g_ref[...]
        hn_ref[...] = (nrm * (1.0 + tab_ref[2:3, :]) + tab_ref[1:2, :]).astype(BF16)

    return pl.pallas_call(
        body, name="norm2_fwd", grid=(t // TR2,),
        in_specs=[_rb(TR2, D), _rb(TR2, D), _bc(1, D), _bc(8, D)],
        out_specs=[_rb(TR2, D), _rb(TR2, D)],
        out_shape=[jax.ShapeDtypeStruct((t, D), F32), jax.ShapeDtypeStruct((t, D), BF16)],
        compiler_params=_cparams(("parallel",)),
    )(x, out, g, tab)


def _halo_specs(tm, w, col, n_rows, hb):
    per = tm // hb
    last = n_rows // hb - 1
    prev = pl.BlockSpec((hb, w), lambda i: (jnp.maximum(i * per - 1, 0), col))
    nxt = pl.BlockSpec((hb, w), lambda i: (jnp.minimum((i + 1) * per, last), col))
    return prev, nxt


def _shift_rows(x, prev_row, next_row):
    tm = x.shape[0]
    rid = lax.broadcasted_iota(jnp.int32, x.shape, 0)
    xm1 = jnp.where(rid == 0, prev_row, pltpu.roll(x, 1, 0))
    xp1 = jnp.where(rid == tm - 1, next_row, pltpu.roll(x, tm - 1, 0))
    return xm1, xp1


def _seq_edges(i, n_ctx_tiles, n_tiles):
    first = jnp.logical_or(i == 0, i == n_ctx_tiles)
    last = jnp.logical_or(i == n_ctx_tiles - 1, i == n_tiles - 1)
    return first, last


def qkconv_fwd(zmain, zg, conv_w, b_gate, n_ctx_tiles):
    tp = zmain.shape[0]
    nt = tp // TR
    w2 = 2 * D
    prev_s, next_s = _halo_specs(TR, w2, CB_Q // 2, tp, 16)

    def body(z_ref, zp_ref, zn_ref, w_ref, zg_ref, bg_ref, q_ref, k_ref, g_ref, zgb_ref):
        i = pl.program_id(0)
        first, last = _seq_edges(i, n_ctx_tiles, nt)
        z = z_ref[...].astype(F32)
        pr = jnp.where(first, 0.0, zp_ref[15:16, :].astype(F32))
        nx = jnp.where(last, 0.0, zn_ref[0:1, :].astype(F32))
        zm1, zp1 = _shift_rows(z, pr, nx)
        cv = w_ref[0:1, :] * zm1 + w_ref[1:2, :] * z + w_ref[2:3, :] * zp1
        a = _silu(cv)
        q_ref[...] = (a[:, :D] * (DH ** -0.5)).astype(BF16)
        k_ref[...] = a[:, D:].astype(BF16)
        zgb = zg_ref[...] + bg_ref[...]
        zgb_ref[...] = zgb
        logf = jnp.minimum(zgb, 0.0) - jnp.log(1.0 + jnp.exp(-jnp.abs(zgb)))
        rr = lax.broadcasted_iota(jnp.int32, (TR, TR), 0)
        cc = lax.broadcasted_iota(jnp.int32, (TR, TR), 1)
        same = (rr // LC) == (cc // LC)
        low = jnp.where(jnp.logical_and(same, cc <= rr), 1.0, 0.0).astype(BF16)
        upp = jnp.where(jnp.logical_and(same, cc >= rr), 1.0, 0.0).astype(BF16)
        bf = _exact_dot(low, logf)
        bb = _exact_dot(upp, logf)
        lane = lax.broadcasted_iota(jnp.int32, (TR, LANE), 1)
        g = jnp.where(jnp.logical_and(lane >= 4, lane < 8), bf,
                      jnp.where(jnp.logical_and(lane >= 12, lane < 16), bb, zgb))
        g_ref[...] = g

    return pl.pallas_call(
        body, name="qkconv_fwd", grid=(nt,),
        in_specs=[_rb(TR, w2, CB_Q // 2), prev_s, next_s, _bc(8, w2), _rb(TR, LANE), _bc(1, LANE)],
        out_specs=[_rb(TR, D), _rb(TR, D), _rb(TR, LANE), _rb(TR, LANE)],
        out_shape=[jax.ShapeDtypeStruct((tp, D), BF16), jax.ShapeDtypeStruct((tp, D), BF16),
                   jax.ShapeDtypeStruct((tp, LANE), F32), jax.ShapeDtypeStruct((tp, LANE), F32)],
        compiler_params=_cparams(("parallel",)),
    )(zmain, zmain, zmain, conv_w, zg, b_gate)


def qkconv_bwd_a(dz, zmain, dqf, dqb, dkf, dkb, dvf, dvb, conv_w, n_ctx_tiles):
    tp = zmain.shape[0]
    nt = tp // TR
    w2 = 2 * D
    prev_s, next_s = _halo_specs(TR, w2, CB_Q // 2, tp, 16)

    def body(dz_in_ref, z_ref, zp_ref, zn_ref, w_ref, dqf_ref, dqb_ref, dkf_ref, dkb_ref, dvf_ref, dvb_ref,
             dzv_ref, dc_ref, dw_ref):
        del dz_in_ref
        dzv_ref[...] = (dvf_ref[...].astype(F32) + dvb_ref[...].astype(F32)).astype(BF16)
        i = pl.program_id(0)
        first, last = _seq_edges(i, n_ctx_tiles, nt)
        z = z_ref[...].astype(F32)
        pr = jnp.where(first, 0.0, zp_ref[15:16, :].astype(F32))
        nx = jnp.where(last, 0.0, zn_ref[0:1, :].astype(F32))
        zm1, zp1 = _shift_rows(z, pr, nx)
        cv = w_ref[0:1, :] * zm1 + w_ref[1:2, :] * z + w_ref[2:3, :] * zp1
        da = jnp.concatenate(
            [(dqf_ref[...].astype(F32) + dqb_ref[...].astype(F32)) * (DH ** -0.5),
             dkf_ref[...].astype(F32) + dkb_ref[...].astype(F32)], axis=1)
        dc = da * _dsilu(cv)
        dc_ref[...] = dc.astype(BF16)

        @pl.when(i == 0)
        def _():
            dw_ref[...] = jnp.zeros_like(dw_ref)

        dw_ref[0:1, :] += jnp.sum(zm1 * dc, axis=0, keepdims=True)
        dw_ref[1:2, :] += jnp.sum(z * dc, axis=0, keepdims=True)
        dw_ref[2:3, :] += jnp.sum(zp1 * dc, axis=0, keepdims=True)

    return pl.pallas_call(
        body, name="qkconv_bwd_a", grid=(nt,),
        in_specs=[pl.BlockSpec(memory_space=pl.ANY), _rb(TR, w2, CB_Q // 2), prev_s, next_s, _bc(8, w2)]
        + [_rb(TR, D)] * 6,
        out_specs=[_rb(TR, D, CB_V), _rb(TR, w2), _bc(8, w2)],
        out_shape=[jax.ShapeDtypeStruct(dz.shape, BF16), jax.ShapeDtypeStruct((tp, w2), BF16),
                   jax.ShapeDtypeStruct((8, w2), F32)],
        input_output_aliases={0: 0},
        compiler_params=_cparams(("arbitrary",)),
    )(dz, zmain, zmain, zmain, conv_w, dqf, dqb, dkf, dkb, dvf, dvb)


def qkconv_bwd_b(dz, dc, conv_w, n_ctx_tiles):
    tp = dc.shape[0]
    nt = tp // TR
    w2 = 2 * D
    prev_s, next_s = _halo_specs(TR, w2, 0, tp, 16)

    def body(dz_in_ref, d_ref, dp_ref, dn_ref, w_ref, o_ref):
        del dz_in_ref
        i = pl.program_id(0)
        first, last = _seq_edges(i, n_ctx_tiles, nt)
        d = d_ref[...].astype(F32)
        pr = jnp.where(first, 0.0, dp_ref[15:16, :].astype(F32))
        nx = jnp.where(last, 0.0, dn_ref[0:1, :].astype(F32))
        dm1, dp1 = _shift_rows(d, pr, nx)
        o_ref[...] = (w_ref[0:1, :] * dp1 + w_ref[1:2, :] * d + w_ref[2:3, :] * dm1).astype(BF16)

    return pl.pallas_call(
        body, name="qkconv_bwd_b", grid=(nt,),
        in_specs=[pl.BlockSpec(memory_space=pl.ANY), _rb(TR, w2), prev_s, next_s, _bc(8, w2)],
        out_specs=_rb(TR, w2, CB_Q // 2),
        out_shape=jax.ShapeDtypeStruct(dz.shape, BF16),
        input_output_aliases={0: 0},
        compiler_params=_cparams(("parallel",)),
    )(dz, dc, dc, dc, conv_w)


def add_into_dz(dz, a, b, col):
    tp = a.shape[0]

    def body(dz_in_ref, a_ref, b_ref, o_ref):
        del dz_in_ref
        o_ref[...] = (a_ref[...].astype(F32) + b_ref[...].astype(F32)).astype(BF16)

    return pl.pallas_call(
        body, name="add_into_dz", grid=(tp // TR,),
        in_specs=[pl.BlockSpec(memory_space=pl.ANY), _rb(TR, D), _rb(TR, D)],
        out_specs=_rb(TR, D, col),
        out_shape=jax.ShapeDtypeStruct(dz.shape, BF16),
        input_output_aliases={0: 0},
        compiler_params=_cparams(("parallel",)),
    )(dz, a, b)


def _chunk_maps(nc, ncc):
    def fwd(t):
        return t

    def bwd(t):
        return jnp.where(t < ncc, ncc - 1 - t, nc - 1 + ncc - t)

    return fwd, bwd


def _split2(x):
    hi = x.astype(BF16)
    return hi, (x - hi.astype(F32)).astype(BF16)


def _mlstm_chunk(d, h, gc, gr, q_ref, k_ref, v_ref, cp, npv, m_prev, mask, precise=False):
    ic, bcol = 8 * d + h, 8 * d + 4 + h
    i_col, b_col = gc[:, ic:ic + 1], gc[:, bcol:bcol + 1]
    i_row, b_row = gr[ic:ic + 1, :], gr[bcol:bcol + 1, :]
    g = b_row[:, LC - 1:LC] if d == 0 else b_row[:, 0:1]
    a_row = g - b_row + i_row
    m_loc = jnp.max(a_row, axis=1, keepdims=True)
    dmat = jnp.where(mask, b_col - b_row + i_row, -jnp.inf)
    inter = b_col + m_prev
    m_row = jnp.maximum(inter, jnp.max(dmat, axis=1, keepdims=True))
    e = jnp.exp(dmat - m_row)
    w = jnp.exp(inter - m_row)
    hs = slice(h * DH, (h + 1) * DH)
    qh, kh, vh = q_ref[:, hs], k_ref[:, hs], v_ref[:, hs]
    p = _dot_nt(qh, kh)
    s = p * e
    cpb = cp.astype(BF16)
    qc = _dot(qh, cpb)
    if precise:
        s_hi, s_lo = _split2(s)
        num = _dot(s_hi, vh) + _dot(s_lo, vh) + w * qc
    else:
        num = _dot(s, vh) + w * qc
    qn = jnp.sum(qh.astype(F32) * npv, axis=1, keepdims=True)
    den = jnp.sum(s, axis=1, keepdims=True) + w * qn
    thr = jnp.exp(-m_row)
    m_new = jnp.maximum(g + m_prev, m_loc)
    a_old = jnp.exp(g + m_prev - m_new)
    a_col = g - b_col + i_col
    return dict(qh=qh, kh=kh, vh=vh, e=e, w=w, s=s, cpb=cpb, qc=qc, num=num, qn=qn, den=den, thr=thr,
                m_loc=m_loc, m_new=m_new, a_old=a_old, a_col=a_col, hs=hs)


def mlstm_fwd(qa, ka, zmain, gcol, grow, ncc):
    tp = qa.shape[0]
    nc = tp // LC
    cf, cb = _chunk_maps(nc, ncc)

    def body(qf, kf, vf, gcf, grf, qb, kb, vb, gcb, grb,
             hf_o, hb_o, cf_o, cb_o, nf_o, nb_o, mf_o, mb_o, c_sc, n_sc, m_sc):
        t = pl.program_id(0)

        @pl.when(t == 0)
        def _():
            c_sc[...] = jnp.zeros_like(c_sc)
            n_sc[...] = jnp.zeros_like(n_sc)
            m_sc[...] = jnp.full(m_sc.shape, M_INIT, F32)

        row = lax.broadcasted_iota(jnp.int32, (LC, LC), 0)
        col = lax.broadcasted_iota(jnp.int32, (LC, LC), 1)
        dirs = ((qf, kf, vf, gcf, grf, hf_o, cf_o, nf_o, mf_o), (qb, kb, vb, gcb, grb, hb_o, cb_o, nb_o, mb_o))
        for d, (q_ref, k_ref, v_ref, gc_ref, gr_ref, h_o, c_o, n_o, m_o) in enumerate(dirs):
            mask = (col <= row) if d == 0 else (col >= row)
            gc = gc_ref[...]
            gr = gr_ref[0]
            for h in range(NH):
                idx = d * NH + h
                cp = c_sc[idx]
                npv = n_sc[idx]
                m_full = m_sc[idx]
                m_prev = m_full[:, 0:1]
                r = _mlstm_chunk(d, h, gc, gr, q_ref, k_ref, v_ref, cp, npv, m_prev, mask)
                hs = r["hs"]
                h_o[:, hs] = (r["num"] / jnp.maximum(jnp.abs(r["den"]), r["thr"])).astype(BF16)
                c_o[0, hs, :] = r["cpb"]
                n_o[0, h:h + 1, :] = npv
                m_o[0, h:h + 1, :] = m_full
                a_new = jnp.exp(r["m_loc"] - r["m_new"])
                kw = r["kh"].astype(F32) * jnp.exp(r["a_col"] - r["m_loc"])
                kv = _dot_tn_mxu(kw, r["vh"])
                kn = jnp.sum(kw, axis=0, keepdims=True)
                c_sc[idx] = r["a_old"] * cp + a_new * kv
                n_sc[idx] = r["a_old"] * npv + a_new * kn
                m_sc[idx] = jnp.broadcast_to(r["m_new"], (1, LANE))

    def dspecs(cm):
        return [pl.BlockSpec((LC, D), lambda t: (cm(t), 0)),
                pl.BlockSpec((LC, D), lambda t: (cm(t), 0)),
                pl.BlockSpec((LC, D), lambda t: (cm(t), CB_V)),
                pl.BlockSpec((LC, LANE), lambda t: (cm(t), 0)),
                pl.BlockSpec((1, 16, LC), lambda t: (cm(t), 0, 0))]

    def ospec(cm, shp):
        return pl.BlockSpec((1,) + shp, lambda t: (cm(t), 0, 0))

    return pl.pallas_call(
        body, name="mlstm_fwd", grid=(nc,),
        in_specs=dspecs(cf) + dspecs(cb),
        out_specs=[pl.BlockSpec((LC, D), lambda t: (cf(t), 0)), pl.BlockSpec((LC, D), lambda t: (cb(t), 0)),
                   ospec(cf, (D, DH)), ospec(cb, (D, DH)), ospec(cf, (NH, DH)), ospec(cb, (NH, DH)),
                   ospec(cf, (NH, LANE)), ospec(cb, (NH, LANE))],
        out_shape=[jax.ShapeDtypeStruct((tp, D), BF16), jax.ShapeDtypeStruct((tp, D), BF16),
                   jax.ShapeDtypeStruct((nc, D, DH), BF16), jax.ShapeDtypeStruct((nc, D, DH), BF16),
                   jax.ShapeDtypeStruct((nc, NH, DH), F32), jax.ShapeDtypeStruct((nc, NH, DH), F32),
                   jax.ShapeDtypeStruct((nc, NH, LANE), F32), jax.ShapeDtypeStruct((nc, NH, LANE), F32)],
        scratch_shapes=[pltpu.VMEM((2 * NH, DH, DH), F32), pltpu.VMEM((2 * NH, 1, DH), F32),
                        pltpu.VMEM((2 * NH, 1, LANE), F32)],
        compiler_params=_cparams(("arbitrary",)),
    )(qa, ka, zmain, gcol, grow, qa, ka, zmain, gcol, grow)


def mlstm_bwd(qa, ka, zmain, gcol, grow, states, dhm, ncc):
    tp = qa.shape[0]
    nc = tp // LC
    cf0, cb0 = _chunk_maps(nc, ncc)
    cf = lambda t: cf0(nc - 1 - t)
    cb = lambda t: cb0(nc - 1 - t)
    csf, csb, nsf, nsb, msf, msb = states

    def body(qf, kf, vf, gcf, grf, cpf, npf, mpf, dhf, qb, kb, vb, gcb, grb, cpb_, npb, mpb, dhb,
             dqf_o, dkf_o, dvf_o, colf_o, rowf_o, dqb_o, dkb_o, dvb_o, colb_o, rowb_o, dc_sc, dn_sc):
        t = pl.program_id(0)

        @pl.when(t == 0)
        def _():
            dc_sc[...] = jnp.zeros_like(dc_sc)
            dn_sc[...] = jnp.zeros_like(dn_sc)

        row = lax.broadcasted_iota(jnp.int32, (LC, LC), 0)
        col = lax.broadcasted_iota(jnp.int32, (LC, LC), 1)
        dirs = ((qf, kf, vf, gcf, grf, cpf, npf, mpf, dhf, dqf_o, dkf_o, dvf_o, colf_o, rowf_o, cf),
                (qb, kb, vb, gcb, grb, cpb_, npb, mpb, dhb, dqb_o, dkb_o, dvb_o, colb_o, rowb_o, cb))
        for d, (q_ref, k_ref, v_ref, gc_ref, gr_ref, cp_ref, np_ref, mp_ref, dh_ref,
                dq_o, dk_o, dv_o, col_o, row_o, cm) in enumerate(dirs):
            mask = (col <= row) if d == 0 else (col >= row)
            live = jnp.where(cm(t) >= ncc, 1.0, 0.0).astype(F32)
            gc = gc_ref[...]
            gr = gr_ref[0]
            col_o[...] = jnp.zeros_like(col_o)
            row_o[...] = jnp.zeros_like(row_o)
            for h in range(NH):
                idx = d * NH + h
                hs = slice(h * DH, (h + 1) * DH)
                cp = cp_ref[0, hs, :]
                npv = np_ref[0, h:h + 1, :]
                m_prev = mp_ref[0, h:h + 1, 0:1]
                r = _mlstm_chunk(d, h, gc, gr, q_ref, k_ref, v_ref, cp, npv, m_prev, mask, precise=True)
                qh, kh, vh, e, w, s = r["qh"], r["kh"], r["vh"], r["e"], r["w"], r["s"]
                qf32, kf32 = qh.astype(F32), kh.astype(F32)
                den, thr = r["den"], r["thr"]
                rden = 1.0 / jnp.maximum(jnp.abs(den), thr)
                hh = r["num"] * rden
                dh = dh_ref[:, hs].astype(F32) * live
                dnum = dh * rden
                sgn = jnp.where(jnp.abs(den) > thr, jnp.sign(den), 0.0)
                dden = -jnp.sum(dh * hh, axis=1, keepdims=True) * rden * sgn
                dn_hi, dn_lo = _split2(dnum)
                ds = _dot_nt(dn_hi, vh) + _dot_nt(dn_lo, vh) + dden
                dp = ds * e
                gm = ds * s
                rowsum = jnp.sum(gm, axis=1, keepdims=True)
                colsum = jnp.sum(gm, axis=0, keepdims=True)
                dq = _dot(dp, kh) + w * (_dot_nt(dnum, r["cpb"]) + dden * npv)
                dcs = dc_sc[idx]
                dns = dn_sc[idx]
                kfac = jnp.exp(r["a_col"] - r["m_new"])
                vdc = _dot_nt(vh, dcs)
                dk = _dot_tn(dp, qh) + kfac * (vdc + dns)
                dv = _dot_tn(s, dnum) + kfac * _dot(kh, dcs)
                beta = w * (jnp.sum(dnum * r["qc"], axis=1, keepdims=True) + dden * r["qn"])
                alpha = kfac * (jnp.sum(kf32 * vdc, axis=1, keepdims=True) + jnp.sum(kf32 * dns, axis=1, keepdims=True))
                dq_o[:, hs] = dq.astype(BF16)
                dk_o[:, hs] = dk.astype(BF16)
                dv_o[:, hs] = dv.astype(BF16)
                cpf = r["cpb"].astype(F32)
                inner = (jnp.sum(jnp.sum(dcs * cpf, axis=1, keepdims=True), axis=0, keepdims=True)
                         + jnp.sum(dns * npv, axis=1, keepdims=True))
                gam = jnp.sum(alpha, axis=0, keepdims=True) + r["a_old"] * inner
                lo = 8 * d + h
                col_o[:, lo:lo + 1] = alpha
                col_o[:, lo + 4:lo + 5] = rowsum + beta - alpha
                col_o[:, lo + 36:lo + 37] = jnp.broadcast_to(gam, (LC, 1))
                row_o[0, h:h + 1, :] = colsum
                wq = qf32 * w
                dc_sc[idx] = r["a_old"] * dcs + _dot_tn(wq, dnum)
                dn_sc[idx] = r["a_old"] * dns + jnp.sum(wq * dden, axis=0, keepdims=True)

    def dspecs(cm):
        return [pl.BlockSpec((LC, D), lambda t: (cm(t), 0)),
                pl.BlockSpec((LC, D), lambda t: (cm(t), 0)),
                pl.BlockSpec((LC, D), lambda t: (cm(t), CB_V)),
                pl.BlockSpec((LC, LANE), lambda t: (cm(t), 0)),
                pl.BlockSpec((1, 16, LC), lambda t: (cm(t), 0, 0)),
                pl.BlockSpec((1, D, DH), lambda t: (cm(t), 0, 0)),
                pl.BlockSpec((1, NH, DH), lambda t: (cm(t), 0, 0)),
                pl.BlockSpec((1, NH, LANE), lambda t: (cm(t), 0, 0)),
                pl.BlockSpec((LC, D), lambda t: (jnp.maximum(cm(t) - ncc, 0), 0))]

    def ospecs(cm):
        return [pl.BlockSpec((LC, D), lambda t: (cm(t), 0)),
                pl.BlockSpec((LC, D), lambda t: (cm(t), 0)),
                pl.BlockSpec((LC, D), lambda t: (cm(t), 0)),
                pl.BlockSpec((LC, LANE), lambda t: (cm(t), 0)),
                pl.BlockSpec((1, 8, LC), lambda t: (cm(t), 0, 0))]

    oshape = [jax.ShapeDtypeStruct((tp, D), BF16)] * 3 + [jax.ShapeDtypeStruct((tp, LANE), F32),
                                                        jax.ShapeDtypeStruct((nc, 8, LC), F32)]
    return pl.pallas_call(
        body, name="mlstm_bwd", grid=(nc,),
        in_specs=dspecs(cf) + dspecs(cb),
        out_specs=ospecs(cf) + ospecs(cb),
        out_shape=oshape + oshape,
        scratch_shapes=[pltpu.VMEM((2 * NH, DH, DH), F32), pltpu.VMEM((2 * NH, 1, DH), F32)],
        compiler_params=_cparams(("arbitrary",)),
    )(qa, ka, zmain, gcol, grow, csf, nsf, msf, dhm, qa, ka, zmain, gcol, grow, csb, nsb, msb, dhm)


def gates_bwd(colf, colb, cs, zgb):
    tp = colf.shape[0]

    def body(cf_ref, cb_ref, cs_ref, zgb_ref, o_ref, db_ref):
        i = pl.program_id(0)

        @pl.when(i == 0)
        def _():
            db_ref[...] = jnp.zeros_like(db_ref)

        lane = lax.broadcasted_iota(jnp.int32, (TR, LANE), 1)
        i_l = jnp.logical_or(lane < 4, jnp.logical_and(lane >= 8, lane < 12))
        f_l = jnp.logical_or(jnp.logical_and(lane >= 4, lane < 8), jnp.logical_and(lane >= 12, lane < 16))
        cv = cf_ref[...] + cb_ref[...]
        csv = cs_ref[...]
        gam = pltpu.roll(cv, LANE - 32, 1)
        dbh = jnp.where(f_l, cv - csv, 0.0)
        rr = lax.broadcasted_iota(jnp.int32, (TR, TR), 0)
        cc = lax.broadcasted_iota(jnp.int32, (TR, TR), 1)
        same = (rr // LC) == (cc // LC)
        low = jnp.where(jnp.logical_and(same, cc <= rr), 1.0, 0.0).astype(BF16)
        upp = jnp.where(jnp.logical_and(same, cc >= rr), 1.0, 0.0).astype(BF16)
        dlogf = jnp.where(lane < 8, _exact_dot(upp, dbh), _exact_dot(low, dbh)) + gam
        out = jnp.where(i_l, csv + cv, 0.0) + jnp.where(f_l, dlogf * _sigmoid(-zgb_ref[...]), 0.0)
        o_ref[...] = out
        db_ref[...] += jnp.sum(out, axis=0, keepdims=True)

    spec = _rb(TR, LANE)
    return pl.pallas_call(
        body, name="gates_bwd", grid=(tp // TR,),
        in_specs=[spec] * 4,
        out_specs=[spec, _bc(1, LANE)],
        out_shape=[jax.ShapeDtypeStruct((tp, LANE), F32), jax.ShapeDtypeStruct((1, LANE), F32)],
        compiler_params=_cparams(("arbitrary",)),
    )(colf, colb, cs, zgb)


def _head_norm(hm, gh):
    xs, rs = [], []
    for h in range(NH):
        seg = hm[:, h * DH:(h + 1) * DH]
        r = lax.rsqrt(jnp.mean(seg * seg, axis=-1, keepdims=True) + EPS)
        xs.append(seg * r)
        rs.append(r)
    xh = jnp.concatenate(xs, axis=1)
    return xh, rs, xh * gh


def _sgu_norm(vg, ln_g, ln_b):
    mu = jnp.mean(vg, axis=-1, keepdims=True)
    vc = vg - mu
    rstd = lax.rsqrt(jnp.mean(vc * vc, axis=-1, keepdims=True) + EPS)
    vhat = vc * rstd
    return vhat, rstd, vhat * ln_g + ln_b


def _sgu_mix(vn, ws_ref, bs_ref):
    rows = []
    for c in range(TR // SCH):
        cols = []
        for g in range(SG):
            blk = vn[c * SCH:(c + 1) * SCH, g * SGD:(g + 1) * SGD]
            cols.append(_dot(ws_ref[g * SCH:(g + 1) * SCH, :], blk) + bs_ref[:, g:g + 1])
        rows.append(jnp.concatenate(cols, axis=1))
    return jnp.concatenate(rows, axis=0)


def mixer_fwd(hf, hb, zmain, gh, ln_g, ln_b, ws, bs_t, n_ctx_tiles):
    t = hf.shape[0] - n_ctx_tiles * TR
    off = n_ctx_tiles

    def body(hf_ref, hb_ref, zo_ref, zu_ref, zv_ref, gh_ref, lg_ref, lb_ref, ws_ref, bs_ref, o_ref):
        hm = hf_ref[...].astype(F32) + hb_ref[...].astype(F32)
        _, _, hn = _head_norm(hm, gh_ref[...])
        o_ref[0] = (_sigmoid(zo_ref[...].astype(F32)) * hn).astype(BF16)
        _, _, vn = _sgu_norm(_gelu(zv_ref[...].astype(F32)), lg_ref[...], lb_ref[...])
        mixed = _sgu_mix(vn, ws_ref, bs_ref)
        o_ref[1] = (_gelu(zu_ref[...].astype(F32)) * mixed).astype(BF16)

    return pl.pallas_call(
        body, name="mixer_fwd", grid=(t // TR,),
        in_specs=[_rb(TR, D, 0, off), _rb(TR, D, 0, off), _rb(TR, D, CB_O, off), _rb(TR, D, CB_U, off),
                  _rb(TR, D, CB_VG, off), _bc(1, D), _bc(1, D), _bc(1, D), _bc(SG * SCH, SCH), _bc(SCH, LANE)],
        out_specs=pl.BlockSpec((2, TR, D), lambda i: (0, i, 0)),
        out_shape=jax.ShapeDtypeStruct((2, t, D), BF16),
        compiler_params=_cparams(("parallel",)),
    )(hf, hb, zmain, zmain, zmain, gh, ln_g, ln_b, ws, bs_t)


def merge_fwd(zmain, pp, n_ctx_tiles):
    t = pp.shape[1]
    off = n_ctx_tiles

    def body(zgm_ref, zgg_ref, pp_ref, o_ref):
        y = (_sigmoid(zgm_ref[...].astype(F32)) * pp_ref[0].astype(F32)
             + _sigmoid(zgg_ref[...].astype(F32)) * pp_ref[1].astype(F32))
        o_ref[...] = y.astype(BF16)

    return pl.pallas_call(
        body, name="merge_fwd", grid=(t // TR,),
        in_specs=[_rb(TR, D, CB_GM, off), _rb(TR, D, CB_GG, off), pl.BlockSpec((2, TR, D), lambda i: (0, i, 0))],
        out_specs=_rb(TR, D),
        out_shape=jax.ShapeDtypeStruct((t, D), BF16),
        compiler_params=_cparams(("parallel",)),
    )(zmain, zmain, pp)


def merge_bwd(zmain, pp, dy, tp, n_ctx_tiles):
    t = dy.shape[0]
    nt = tp // TR
    xrow = lambda i: jnp.maximum(i - n_ctx_tiles, 0)

    def body(zg_ref, pp_ref, dy_ref, dpp_ref, dz_ref):
        i = pl.program_id(1)
        zg = zg_ref[...].astype(F32)
        sg = _sigmoid(zg)
        dyv = dy_ref[...].astype(F32)
        dpp_ref[0] = (dyv * sg).astype(BF16)
        dzv = dyv * pp_ref[0].astype(F32) * sg * (1.0 - sg)
        dz_ref[...] = jnp.where(i >= n_ctx_tiles, dzv, 0.0).astype(BF16)

    return pl.pallas_call(
        body, name="merge_bwd", grid=(2, nt),
        in_specs=[pl.BlockSpec((TR, D), lambda j, i: (i, CB_GM + j)),
                  pl.BlockSpec((1, TR, D), lambda j, i: (j, xrow(i), 0)),
                  pl.BlockSpec((TR, D), lambda j, i: (xrow(i), 0))],
        out_specs=[pl.BlockSpec((1, TR, D), lambda j, i: (j, xrow(i), 0)),
                   pl.BlockSpec((TR, D), lambda j, i: (i, CB_GM + j))],
        out_shape=[jax.ShapeDtypeStruct((2, t, D), BF16), jax.ShapeDtypeStruct((tp, 8 * D), BF16)],
        compiler_params=_cparams(("arbitrary", "arbitrary")),
    )(zmain, pp, dy)


def mixer_bwd(dz, hf, hb, zmain, dyms, gh, ln_g, ln_b, ws, bs_t, n_ctx_tiles):
    tp = hf.shape[0]
    t = tp - n_ctx_tiles * TR
    nt = tp // TR
    xrow = lambda i: jnp.maximum(i - n_ctx_tiles, 0)

    def body(dz_in_ref, hf_ref, hb_ref, zo_ref, zu_ref, zv_ref, dy_ref, gh_ref, lg_ref, lb_ref, ws_ref, bs_ref,
             dz_ref, dhm_ref, dgh_ref, dlg_ref, dlb_ref, dws_ref, dbs_ref):
        del dz_in_ref
        i = pl.program_id(0)

        @pl.when(i == 0)
        def _():
            for ref in (dgh_ref, dlg_ref, dlb_ref, dws_ref, dbs_ref):
                ref[...] = jnp.zeros_like(ref)

        @pl.when(i < n_ctx_tiles)
        def _():
            dz_ref[...] = jnp.zeros_like(dz_ref)

        @pl.when(i >= n_ctx_tiles)
        def _():
            gh_v = gh_ref[...]
            hm = hf_ref[...].astype(F32) + hb_ref[...].astype(F32)
            xh, rs, hn = _head_norm(hm, gh_v)
            zo = zo_ref[...].astype(F32)
            so = _sigmoid(zo)
            dym = dy_ref[0].astype(F32)
            d_zo = dym * hn * so * (1.0 - so)
            d_hn = dym * so
            dgh_ref[...] += jnp.sum(d_hn * xh, axis=0, keepdims=True)
            d_xh = d_hn * gh_v
            segs = []
            for h in range(NH):
                hs = slice(h * DH, (h + 1) * DH)
                dx, xs = d_xh[:, hs], xh[:, hs]
                segs.append(rs[h] * (dx - xs * jnp.mean(dx * xs, axis=-1, keepdims=True)))
            dhm_ref[...] = jnp.concatenate(segs, axis=1).astype(BF16)
            zu = zu_ref[...].astype(F32)
            zv = zv_ref[...].astype(F32)
            lg = lg_ref[...]
            gu, dgu = _gelu_and_grad(zu)
            gv, dgv = _gelu_and_grad(zv)
            vhat, rstd, vn = _sgu_norm(gv, lg, lb_ref[...])
            mixed = _sgu_mix(vn, ws_ref, bs_ref)
            dys = dy_ref[1].astype(F32)
            d_zu = dys * mixed * dgu
            d_mixed = dys * gu
            rows = []
            for c in range(TR // SCH):
                cols = []
                for g in range(SG):
                    rsl, csl = slice(c * SCH, (c + 1) * SCH), slice(g * SGD, (g + 1) * SGD)
                    dm = d_mixed[rsl, csl]
                    cols.append(_dot_tn(ws_ref[g * SCH:(g + 1) * SCH, :], dm))
                    dws_ref[g * SCH:(g + 1) * SCH, :] += _dot_nt(dm, vn[rsl, csl])
                    dbs_ref[:, g:g + 1] += jnp.sum(dm, axis=1, keepdims=True)
                rows.append(jnp.concatenate(cols, axis=1))
            d_vn = jnp.concatenate(rows, axis=0)
            dlg_ref[...] += jnp.sum(d_vn * vhat, axis=0, keepdims=True)
            dlb_ref[...] += jnp.sum(d_vn, axis=0, keepdims=True)
            d_vhat = d_vn * lg
            d_vg = rstd * (d_vhat - jnp.mean(d_vhat, axis=-1, keepdims=True)
                           - vhat * jnp.mean(d_vhat * vhat, axis=-1, keepdims=True))
            d_zv = d_vg * dgv
            dz_ref[...] = jnp.concatenate([d_zo, d_zu, d_zv], axis=1).astype(BF16)

    return pl.pallas_call(
        body, name="mixer_bwd", grid=(nt,),
        in_specs=[pl.BlockSpec(memory_space=pl.ANY), _rb(TR, D), _rb(TR, D), _rb(TR, D, CB_O), _rb(TR, D, CB_U),
                  _rb(TR, D, CB_VG), pl.BlockSpec((2, TR, D), lambda i: (0, xrow(i), 0)),
                  _bc(1, D), _bc(1, D), _bc(1, D), _bc(SG * SCH, SCH), _bc(SCH, LANE)],
        out_specs=[_rb(TR, 3 * D), pl.BlockSpec((TR, D), lambda i: (xrow(i), 0)),
                   _bc(1, D), _bc(1, D), _bc(1, D), _bc(SG * SCH, SCH), _bc(SCH, LANE)],
        out_shape=[jax.ShapeDtypeStruct(dz.shape, BF16), jax.ShapeDtypeStruct((t, D), BF16),
                   jax.ShapeDtypeStruct((1, D), F32), jax.ShapeDtypeStruct((1, D), F32),
                   jax.ShapeDtypeStruct((1, D), F32), jax.ShapeDtypeStruct((SG * SCH, SCH), F32),
                   jax.ShapeDtypeStruct((SCH, LANE), F32)],
        input_output_aliases={0: 0},
        compiler_params=_cparams(("arbitrary",)),
    )(dz, hf, hb, zmain, zmain, zmain, dyms, gh, ln_g, ln_b, ws, bs_t)


FCB = DFF // 2
TF = 512


def _ffn_halo(col, t):
    per = TF // GW
    last = t // GW - 1
    prev = pl.BlockSpec((GW, FCB), lambda i, j: (jnp.maximum(i * per - 1, 0), col(j)))
    nxt = pl.BlockSpec((GW, FCB), lambda i, j: (jnp.minimum((i + 1) * per, last), col(j)))
    return prev, nxt


def _conv_taps(ext):
    n = ext.shape[0]
    colid = lax.broadcasted_iota(jnp.int32, (n, 1), 0) % GW
    left = pltpu.roll(jnp.where(colid != GW - 1, ext, 0.0), 1, 0)
    right = pltpu.roll(jnp.where(colid != 0, ext, 0.0), n - 1, 0)
    views = (left, ext, right)
    return {(ky, kx): views[kx][GW * ky:GW * ky + TF] for ky in range(3) for kx in range(3)}


def _ext(c_ref, p_ref, n_ref, i, nt):
    pr = jnp.where(i == 0, 0.0, p_ref[...].astype(F32))
    nx = jnp.where(i == nt - 1, 0.0, n_ref[...].astype(F32))
    return jnp.concatenate([pr, c_ref[...].astype(F32), nx], axis=0)


def ffn_act_fwd(up, wc):
    t = up.shape[0]
    nt = t // TF
    prev_s, next_s = _ffn_halo(lambda j: j, t)

    def body(a_ref, ap_ref, an_ref, b_ref, w_ref, o_ref, ac_ref):
        i = pl.program_id(0)
        taps = _conv_taps(_ext(a_ref, ap_ref, an_ref, i, nt))
        ac = sum(w_ref[3 * ky + kx:3 * ky + kx + 1, :] * taps[(ky, kx)] for ky in range(3) for kx in range(3))
        ac_ref[...] = ac.astype(BF16)
        o_ref[...] = (_silu(ac) * b_ref[...].astype(F32)).astype(BF16)

    spec = pl.BlockSpec((TF, FCB), lambda i, j: (i, j))
    return pl.pallas_call(
        body, name="ffn_act_fwd", grid=(nt, 2),
        in_specs=[spec, prev_s, next_s,
                  pl.BlockSpec((TF, FCB), lambda i, j: (i, 2 + j)), pl.BlockSpec((16, FCB), lambda i, j: (0, j))],
        out_specs=[spec, spec],
        out_shape=[jax.ShapeDtypeStruct((t, DFF), BF16), jax.ShapeDtypeStruct((t, DFF), BF16)],
        compiler_params=_cparams(("parallel", "parallel")),
    )(up, up, up, up, wc)


def ffn_act_bwd(up, ac, dact):
    t = up.shape[0]
    nt = t // TF

    def body(b_ref, ac_ref, da_ref, dup_ref, dac_ref):
        acv = ac_ref[...].astype(F32)
        da = da_ref[...].astype(F32)
        s = _sigmoid(acv)
        dup_ref[...] = (da * acv * s).astype(BF16)
        dac_ref[...] = (da * b_ref[...].astype(F32) * s * (1.0 + acv * (1.0 - s))).astype(BF16)

    spec = pl.BlockSpec((TF, FCB), lambda i, j: (i, j))
    bspec = pl.BlockSpec((TF, FCB), lambda i, j: (i, 2 + j))
    return pl.pallas_call(
        body, name="ffn_act_bwd", grid=(nt, 2),
        in_specs=[bspec, spec, spec],
        out_specs=[bspec, spec],
        out_shape=[jax.ShapeDtypeStruct((t, 2 * DFF), BF16), jax.ShapeDtypeStruct((t, DFF), BF16)],
        compiler_params=_cparams(("parallel", "parallel")),
    )(up, ac, dact)


def ffn_conv_bwd(dup, up, dac, wc):
    t = up.shape[0]
    nt = t // TF
    prev_g, next_g = _ffn_halo(lambda j: j, t)

    strip = 16

    def body(dup_in_ref, a_ref, g_ref, gp_ref, gn_ref, w_ref, o_ref, dw_ref, lv_ref, cv_ref, rv_ref, part_ref):
        del dup_in_ref
        i = pl.program_id(1)

        @pl.when(i == 0)
        def _():
            dw_ref[...] = jnp.zeros_like(dw_ref)

        ext = _ext(g_ref, gp_ref, gn_ref, i, nt)
        n = ext.shape[0]
        colid = lax.broadcasted_iota(jnp.int32, (n, 1), 0) % GW
        lv_ref[...] = pltpu.roll(jnp.where(colid != GW - 1, ext, 0.0), 1, 0)
        cv_ref[...] = ext
        rv_ref[...] = pltpu.roll(jnp.where(colid != 0, ext, 0.0), n - 1, 0)
        part_ref[...] = jnp.zeros_like(part_ref)
        views = (lv_ref, cv_ref, rv_ref)

        def one_strip(r, carry):
            r0 = pl.multiple_of(r * strip, strip)
            a = a_ref[pl.ds(r0, strip), :].astype(F32)
            acc = jnp.zeros((strip, FCB), F32)
            for ky in range(3):
                for kx in range(3):
                    kf = 3 * (2 - ky) + (2 - kx)
                    tap = views[kx][pl.ds(r0 + GW * ky, strip), :]
                    acc = acc + w_ref[kf:kf + 1, :] * tap
                    p = a * tap
                    part_ref[8 * kf:8 * kf + 8, :] += p[0:8] + p[8:16]
            o_ref[pl.ds(r0, strip), :] = acc.astype(BF16)
            return carry

        lax.fori_loop(0, TF // strip, one_strip, 0)
        for k in range(9):
            dw_ref[k:k + 1, :] += jnp.sum(part_ref[8 * k:8 * k + 8, :], axis=0, keepdims=True)

    sw = lambda s: pl.BlockSpec(s.block_shape, lambda j, i, f=s.index_map: f(i, j))
    spec = pl.BlockSpec((TF, FCB), lambda j, i: (i, j))
    return pl.pallas_call(
        body, name="ffn_conv_bwd", grid=(2, nt),
        in_specs=[pl.BlockSpec(memory_space=pl.ANY), spec, spec, sw(prev_g), sw(next_g),
                  pl.BlockSpec((16, FCB), lambda j, i: (0, j))],
        out_specs=[spec, pl.BlockSpec((16, FCB), lambda j, i: (0, j))],
        out_shape=[jax.ShapeDtypeStruct(dup.shape, BF16), jax.ShapeDtypeStruct((16, DFF), F32)],
        scratch_shapes=[pltpu.VMEM((TF + 2 * GW, FCB), F32)] * 3 + [pltpu.VMEM((72, FCB), F32)],
        input_output_aliases={0: 0},
        compiler_params=_cparams(("arbitrary", "arbitrary")),
    )(dup, up, dac, dac, dac, wc)


def head_fwd_bwd(h1, f, target, gfin, tab):
    t = h1.shape[0]

    def body(h1_ref, f_ref, t_ref, g_ref, tab_ref, dh2_ref, df_ref, acc_ref):
        i = pl.program_id(0)

        @pl.when(i == 0)
        def _():
            acc_ref[...] = jnp.zeros_like(acc_ref)

        gate = tab_ref[0:1, :]
        fv = f_ref[...].astype(F32)
        h2 = h1_ref[...] + gate * fv
        r = lax.rsqrt(jnp.mean(h2 * h2, axis=-1, keepdims=True) + EPS)
        xh = h2 * r
        gv = g_ref[...]
        err = xh * gv - t_ref[...]
        acc_ref[0:1, :] += jnp.sum(0.5 * jnp.mean(err * err, axis=-1, keepdims=True), axis=0, keepdims=True)
        dy = err * (1.0 / D)
        acc_ref[1:2, :] += jnp.sum(dy * xh, axis=0, keepdims=True)
        dxh = dy * gv
        dh2 = r * (dxh - xh * jnp.mean(dxh * xh, axis=-1, keepdims=True))
        dh2_ref[...] = dh2
        acc_ref[2:3, :] += jnp.sum(dh2 * fv, axis=0, keepdims=True)
        df_ref[...] = (dh2 * gate).astype(BF16)

    return pl.pallas_call(
        body, name="head_fwd_bwd", grid=(t // TR2,),
        in_specs=[_rb(TR2, D), _rb(TR2, D), _rb(TR2, D), _bc(1, D), _bc(8, D)],
        out_specs=[_rb(TR2, D), _rb(TR2, D), _bc(8, D)],
        out_shape=[jax.ShapeDtypeStruct((t, D), F32), jax.ShapeDtypeStruct((t, D), BF16),
                   jax.ShapeDtypeStruct((8, D), F32)],
        compiler_params=_cparams(("arbitrary",)),
    )(h1, f, target, gfin, tab)


def norm2_bwd(h1, dhn2, dh2, out, g, tab):
    t = h1.shape[0]

    def body(h1_ref, dhn_ref, dh2_ref, out_ref, g_ref, tab_ref, dh1_ref, dout_ref, acc_ref):
        i = pl.program_id(0)

        @pl.when(i == 0)
        def _():
            acc_ref[...] = jnp.zeros_like(acc_ref)

        h1v = h1_ref[...]
        r = lax.rsqrt(jnp.mean(h1v * h1v, axis=-1, keepdims=True) + EPS)
        xh = h1v * r
        gv = g_ref[...]
        dhn = dhn_ref[...].astype(F32)
        acc_ref[0:1, :] += jnp.sum(dhn, axis=0, keepdims=True)
        acc_ref[1:2, :] += jnp.sum(dhn * xh * gv, axis=0, keepdims=True)
        dn = dhn * (1.0 + tab_ref[2:3, :])
        acc_ref[2:3, :] += jnp.sum(dn * xh, axis=0, keepdims=True)
        dxh = dn * gv
        dh1 = dh2_ref[...] + r * (dxh - xh * jnp.mean(dxh * xh, axis=-1, keepdims=True))
        dh1_ref[...] = dh1
        acc_ref[3:4, :] += jnp.sum(dh1 * out_ref[...].astype(F32), axis=0, keepdims=True)
        dout_ref[...] = (dh1 * tab_ref[0:1, :]).astype(BF16)

    return pl.pallas_call(
        body, name="norm2_bwd", grid=(t // TR2,),
        in_specs=[_rb(TR2, D), _rb(TR2, D), _rb(TR2, D), _rb(TR2, D), _bc(1, D), _bc(8, D)],
        out_specs=[_rb(TR2, D), _rb(TR2, D), _bc(8, D)],
        out_shape=[jax.ShapeDtypeStruct((t, D), F32), jax.ShapeDtypeStruct((t, D), BF16),
                   jax.ShapeDtypeStruct((8, D), F32)],
        compiler_params=_cparams(("arbitrary",)),
    )(h1, dhn2, dh2, out, g, tab)


def norm1_bwd(ctx, x, da, db, dh1, g, tab, n_ctx_tiles):
    t = x.shape[0]
    tp = t + ctx.shape[0]
    xrow = lambda i: jnp.maximum(i - n_ctx_tiles, 0)

    def body(c_ref, x_ref, da_ref, db_ref, dh1_ref, g_ref, tab_ref, dx_ref, acc_ref):
        i = pl.program_id(0)

        @pl.when(i == 0)
        def _():
            acc_ref[...] = jnp.zeros_like(acc_ref)

        is_ctx = i < n_ctx_tiles
        x = jnp.where(is_ctx, c_ref[...], x_ref[...])
        r = lax.rsqrt(jnp.mean(x * x, axis=-1, keepdims=True) + EPS)
        xh = x * r
        gv = g_ref[...]
        dhn = da_ref[...].astype(F32) + db_ref[...].astype(F32)
        s_shift = jnp.sum(dhn, axis=0, keepdims=True)
        s_scale = jnp.sum(dhn * xh * gv, axis=0, keepdims=True)

        @pl.when(is_ctx)
        def _():
            acc_ref[0:1, :] += s_shift
            acc_ref[1:2, :] += s_scale

        @pl.when(jnp.logical_not(is_ctx))
        def _():
            acc_ref[2:3, :] += s_shift
            acc_ref[3:4, :] += s_scale

        acc_ref[5:6, :] += s_shift
        acc_ref[6:7, :] += s_scale
        sc = jnp.where(is_ctx, tab_ref[1:2, :], tab_ref[3:4, :])
        dn = dhn * (1.0 + sc)
        acc_ref[4:5, :] += jnp.sum(dn * xh, axis=0, keepdims=True)
        dxh = dn * gv
        dx_ref[...] = dh1_ref[...] + r * (dxh - xh * jnp.mean(dxh * xh, axis=-1, keepdims=True))

    return pl.pallas_call(
        body, name="norm1_bwd", grid=(tp // TR,),
        in_specs=_ctx_x_specs(n_ctx_tiles) + [_rb(TR, D), _rb(TR, D), pl.BlockSpec((TR, D), lambda i: (xrow(i), 0)),
                                              _bc(1, D), _bc(8, D)],
        out_specs=[pl.BlockSpec((TR, D), lambda i: (xrow(i), 0)), _bc(8, D)],
        out_shape=[jax.ShapeDtypeStruct((t, D), F32), jax.ShapeDtypeStruct((8, D), F32)],
        compiler_params=_cparams(("arbitrary",)),
    )(ctx, x, da, db, dh1, g, tab)


def adamw(w, g, m, v, name):
    lead = w.ndim - 2
    rows, cols = w.shape[-2:]
    tm = max(t for t in range(8, rows + 1, 8) if rows % t == 0 and (t * cols <= 512 * 1024 or t == 8))
    c1 = 1.0 / (1.0 - ADAM_B1 ** ADAM_STEP)
    c2 = 1.0 / (1.0 - ADAM_B2 ** ADAM_STEP)

    def body(w_ref, g_ref, m_ref, v_ref, d_ref, mo_ref, vo_ref):
        gv = g_ref[...]
        mn = ADAM_B1 * m_ref[...] + (1.0 - ADAM_B1) * gv
        vn = ADAM_B2 * v_ref[...] + (1.0 - ADAM_B2) * (gv * gv)
        mo_ref[...] = mn
        vo_ref[...] = vn
        d_ref[...] = -ADAM_LR * ((mn * c1) / (jnp.sqrt(vn * c2) + ADAM_EPS) + ADAM_WD * w_ref[...])

    spec = pl.BlockSpec((1,) * lead + (tm, cols), lambda i: (0,) * lead + (i, 0))
    sds = jax.ShapeDtypeStruct(w.shape, F32)
    return pl.pallas_call(
        body, name=name, grid=(rows // tm,),
        in_specs=[spec] * 4, out_specs=[spec] * 3, out_shape=[sds] * 3,
        compiler_params=_cparams(("parallel",)),
    )(w, g, m, v)


def add_n(arrs, out_dtype, name):
    shp = arrs[0].shape
    cols = shp[-1]
    flat = [a.reshape(-1, cols) for a in arrs]
    rows = flat[0].shape[0]
    tm = max(t for t in range(16, rows + 1, 16) if rows % t == 0 and t * cols <= 512 * 1024)

    def body(*refs):
        acc = refs[0][...].astype(F32)
        for r in refs[1:-1]:
            acc = acc + r[...].astype(F32)
        refs[-1][...] = acc.astype(refs[-1].dtype)

    spec = pl.BlockSpec((tm, cols), lambda i: (i, 0))
    out = pl.pallas_call(
        body, name=name, grid=(rows // tm,),
        in_specs=[spec] * len(flat), out_specs=spec, out_shape=jax.ShapeDtypeStruct((rows, cols), out_dtype),
        compiler_params=_cparams(("parallel",)),
    )(*flat)
    return out.reshape(shp)


def sum8(stack, name):
    _, rows, cols = stack.shape
    tm = rows if rows <= 2048 else _pick(rows, (512, 256, 128, 64, 8))

    def body(s_ref, o_ref):
        acc = s_ref[0]
        for k in range(1, 8):
            acc = acc + s_ref[k]
        o_ref[...] = acc

    return pl.pallas_call(
        body, name=name, grid=(rows // tm,),
        in_specs=[pl.BlockSpec((8, tm, cols), lambda i: (0, i, 0))],
        out_specs=pl.BlockSpec((tm, cols), lambda i: (i, 0)),
        out_shape=jax.ShapeDtypeStruct((rows, cols), F32),
        compiler_params=_cparams(("parallel",)),
    )(stack)


def _coords():
    return lax.axis_index("x"), lax.axis_index("y"), lax.axis_index("c")


def _other_chips(x, y):
    return [(1 - x, y), (x, 1 - y), (1 - x, 1 - y)]


_ANY = pl.BlockSpec(memory_space=pl.ANY)


def gather_chips(slabs):
    ns = len(slabs)

    def body(*refs):
        x_refs, out_refs = refs[:ns], refs[ns:2 * ns]
        send_sems, recv_sems = refs[2 * ns:]
        x, y, c = _coords()
        me = 2 * x + y
        sibling = (x, y, 1 - c)
        chips = _other_chips(x, y)

        def half(s, chip, hc):
            rh = slabs[s].shape[0] // 2
            return out_refs[s].at[chip, pl.ds(hc * rh, rh), :]

        def own_half(s):
            rh = slabs[s].shape[0] // 2
            return x_refs[s].at[pl.ds(c * rh, rh), :]

        sends = []
        for s in range(ns):
            for j, (px, py) in enumerate(chips):
                cp = pltpu.make_async_remote_copy(
                    src_ref=own_half(s), dst_ref=half(s, me, c), send_sem=send_sems.at[6 * s + j],
                    recv_sem=recv_sems.at[6 * s + j], device_id=(px, py, c), device_id_type=MESH)
                cp.start()
                sends.append(cp)
        for s in range(ns):
            for j, (px, py) in enumerate(chips):
                src = 2 * px + py
                landed = pltpu.make_async_remote_copy(
                    src_ref=half(s, src, c), dst_ref=half(s, src, c), send_sem=send_sems.at[6 * s + j],
                    recv_sem=recv_sems.at[6 * s + j], device_id=(px, py, c), device_id_type=MESH)
                landed.wait_recv()
                fw = pltpu.make_async_remote_copy(
                    src_ref=half(s, src, c), dst_ref=half(s, src, c), send_sem=send_sems.at[6 * s + 3 + j],
                    recv_sem=recv_sems.at[6 * s + 3 + j], device_id=sibling, device_id_type=MESH)
                fw.start()
                sends.append(fw)
        for s in range(ns):
            for j, (px, py) in enumerate(chips):
                src = 2 * px + py
                got = pltpu.make_async_remote_copy(
                    src_ref=half(s, src, 1 - c), dst_ref=half(s, src, 1 - c), send_sem=send_sems.at[6 * s + 3 + j],
                    recv_sem=recv_sems.at[6 * s + 3 + j], device_id=sibling, device_id_type=MESH)
                got.wait_recv()
        for cp in sends:
            cp.wait_send()

    return pl.pallas_call(
        body, name="gather_chips",
        in_specs=[_ANY] * ns, out_specs=[_ANY] * ns,
        out_shape=[jax.ShapeDtypeStruct((4,) + s.shape, s.dtype) for s in slabs],
        scratch_shapes=[pltpu.SemaphoreType.DMA((6 * ns,)), pltpu.SemaphoreType.DMA((6 * ns,))],
    )(*slabs)


def swap_halves(gss):
    ns = len(gss)

    def body(*refs):
        g_refs, out_refs = refs[:ns], refs[ns:2 * ns]
        send_sems, recv_sems = refs[2 * ns:]
        x, y, c = _coords()
        cps = []
        for s in range(ns):
            rh = gss[s].shape[1] // 2
            cp = pltpu.make_async_remote_copy(
                src_ref=g_refs[s].at[:, pl.ds((1 - c) * rh, rh), :], dst_ref=out_refs[s],
                send_sem=send_sems.at[s], recv_sem=recv_sems.at[s], device_id=(x, y, 1 - c), device_id_type=MESH)
            cp.start()
            cps.append(cp)
        for cp in cps:
            cp.wait()

    return pl.pallas_call(
        body, name="swap_halves",
        in_specs=[_ANY] * ns, out_specs=[_ANY] * ns,
        out_shape=[jax.ShapeDtypeStruct((4, g.shape[1] // 2, g.shape[2]), g.dtype) for g in gss],
        scratch_shapes=[pltpu.SemaphoreType.DMA((ns,)), pltpu.SemaphoreType.DMA((ns,))],
    )(*gss)


def join_halves(reds):
    ns = len(reds)

    def body(*refs):
        r_refs, out_refs = refs[:ns], refs[ns:2 * ns]
        send_sems, recv_sems = refs[2 * ns:]
        x, y, c = _coords()
        cps = []
        for s in range(ns):
            cp = pltpu.make_async_remote_copy(
                src_ref=r_refs[s], dst_ref=out_refs[s], send_sem=send_sems.at[s], recv_sem=recv_sems.at[s],
                device_id=(x, y, 1 - c), device_id_type=MESH)
            cp.start()
            cps.append(cp)
        for cp in cps:
            cp.wait()

    return pl.pallas_call(
        body, name="join_halves",
        in_specs=[_ANY] * ns, out_specs=[_ANY] * ns,
        out_shape=[jax.ShapeDtypeStruct(r.shape, r.dtype) for r in reds],
        scratch_shapes=[pltpu.SemaphoreType.DMA((ns,)), pltpu.SemaphoreType.DMA((ns,))],
    )(*reds)


def gather_all(vec, name):
    r, wd = vec.shape

    def body(v_ref, out_ref, send_sems, recv_sems):
        x, y, c = _coords()
        me = 4 * x + 2 * y + c
        cps = []
        for k in range(1, 8):
            mx, my, mc = (k >> 2) & 1, (k >> 1) & 1, k & 1
            peer = (x ^ mx, y ^ my, c ^ mc)
            cp = pltpu.make_async_remote_copy(
                src_ref=v_ref, dst_ref=out_ref.at[me],
                send_sem=send_sems.at[k - 1], recv_sem=recv_sems.at[k - 1], device_id=peer, device_id_type=MESH)
            cp.start()
            cps.append(cp)
        for k in range(1, 8):
            mx, my, mc = (k >> 2) & 1, (k >> 1) & 1, k & 1
            peer = (x ^ mx, y ^ my, c ^ mc)
            src = 4 * peer[0] + 2 * peer[1] + peer[2]
            got = pltpu.make_async_remote_copy(
                src_ref=v_ref, dst_ref=out_ref.at[src],
                send_sem=send_sems.at[k - 1], recv_sem=recv_sems.at[k - 1], device_id=peer, device_id_type=MESH)
            got.wait_recv()
        for cp in cps:
            cp.wait_send()

    return pl.pallas_call(
        body, name=name,
        in_specs=[_ANY], out_specs=_ANY,
        out_shape=jax.ShapeDtypeStruct((8, r, wd), vec.dtype),
        scratch_shapes=[pltpu.SemaphoreType.DMA((7,)), pltpu.SemaphoreType.DMA((7,))],
    )(vec)


def _pad_rows(a, rows):
    return jnp.pad(a, ((0, rows - a.shape[0]), (0, 0)))


def _pad_cols(a, cols):
    return jnp.pad(a, ((0, 0), (0, cols - a.shape[1])))


def local_step(x, c, ctx, c_ctx, target, wt, sm, late_weights=None, grad_hook=None):
    t, tc = x.shape[0], ctx.shape[0]
    tp = t + tc
    nct = tc // TR
    ncc = tc // LC
    nc = tp // LC

    w_in = wt["w_in"]
    segs = {"q": (0, D), "k": (D, 2 * D), "v": (2 * D, 3 * D), "g": (3 * D, 3 * D + NGATE)}
    base = 3 * D + NGATE
    for n_i, nm in enumerate(("o", "u", "vg", "gm", "gg")):
        segs[nm] = (base + n_i * D, base + (n_i + 1) * D)
    order = ("o", "u", "vg", "gm", "gg", "v", "q", "k")
    w_main = jnp.concatenate([w_in[:, segs[nm][0]:segs[nm][1]] for nm in order], axis=1)
    w_g = _pad_cols(w_in[:, segs["g"][0]:segs["g"][1]], LANE)

    cc = _pad_rows(jnp.concatenate([c.reshape(1, D), c_ctx.reshape(1, D)], axis=0), 16)
    modv = mod_fwd(cc, wt["w_mod"][0], wt["w_mod"][1], sm["b_mod"].reshape(1, NMOD * D))
    mx = modv[0].reshape(NMOD, D)
    mc = modv[1].reshape(NMOD, D)
    tab1 = _pad_rows(jnp.stack([mc[0], mc[1], mx[0], mx[1]]), 8)
    tab2 = _pad_rows(jnp.stack([mx[2], mx[3], mx[4]]), 8)
    tab3 = _pad_rows(mx[5:6], 8)

    g1 = sm["norm1_g"].reshape(1, D)
    g2 = sm["norm2_g"].reshape(1, D)
    gfin = sm["final_g"].reshape(1, D)
    gh = sm["head_norm_g"].reshape(1, D)
    ln_g = sm["sgu_ln_g"].reshape(1, D)
    ln_b = sm["sgu_ln_b"].reshape(1, D)
    ws = sm["w_s"].reshape(SG * SCH, SCH).astype(BF16)
    bs_t = _pad_cols(sm["b_s"].reshape(SG, SCH).T, LANE)
    conv_w = _pad_rows(sm["conv_qk"].reshape(3, 2 * D), 8)
    b_gate = _pad_cols(sm["b_gate"].reshape(1, NGATE), LANE)
    wc = _pad_rows(sm["w_ffn_conv"].reshape(9, DFF), 16)

    hn1 = norm1_fwd(ctx, x, g1, tab1, nct)
    if late_weights is None:
        zmain = mm_nn(hn1, w_main, BF16, "mm_zmain")
    else:
        zmain, landed = mm_nn(hn1, w_main, BF16, "mm_zmain", rider=late_weights[0])
        wt = {**wt, **late_weights[1](landed)}
    zg = mm_nn(hn1, w_g, F32, "mm_zg")
    qa, ka, gcol, zgb = qkconv_fwd(zmain, zg, conv_w, b_gate, nct)
    grow = gcol[:, :16].reshape(nc, LC, 16).transpose(0, 2, 1)
    hf, hb, csf, csb, nsf, nsb, msf, msb = mlstm_fwd(qa, ka, zmain, gcol, grow, ncc)
    yms = mixer_fwd(hf, hb, zmain, gh, ln_g, ln_b, ws, bs_t, nct)
    w_br = jnp.stack([wt["w_branch_mlstm"], wt["w_branch_sgu"]])
    pp = mm_nn(yms, w_br, BF16, "mm_branch")
    y = merge_fwd(zmain, pp, nct)
    out = mm_nn(y, wt["w_out"], BF16, "mm_out")
    h1, hn2 = norm2_fwd(x, out, g2, tab2)
    up = mm_nn(hn2, wt["w_up"], BF16, "mm_up")
    act, ac = ffn_act_fwd(up, wc)
    f = mm_nn(act, wt["w_down"], BF16, "mm_down")
    dh2, df, acc_h = head_fwd_bwd(h1, f, target, gfin, tab3)
    loss = acc_h[0, 0]

    g_w_down = mm_tn(act, df, "mmt_down", BF16)
    dact = mm_nn(df, wt["w_down"], BF16, "mm_ddown", b_transposed=True)
    dup, dac = ffn_act_bwd(up, ac, dact)
    dup, g_wc = ffn_conv_bwd(dup, up, dac, wc)
    g_w_up = mm_tn(hn2, dup, "mmt_up", BF16)
    dhn2 = mm_nn(dup, wt["w_up"], BF16, "mm_dup", b_transposed=True)
    dh1, dout, acc_2 = norm2_bwd(h1, dhn2, dh2, out, g2, tab2)
    g_w_out = mm_tn(y, dout, "mmt_out", BF16)
    dy = mm_nn(dout, wt["w_out"], BF16, "mm_dout", b_transposed=True)
    dpp, dz = merge_bwd(zmain, pp, dy, tp, nct)
    g_w_br = mm_tn(yms, dpp, "mmt_branch", BF16)
    dyms = mm_nn(dpp, w_br, BF16, "mm_dbranch", b_transposed=True)
    dz, dhm, g_gh, g_lng, g_lnb, g_ws, g_bs = mixer_bwd(dz, hf, hb, zmain, dyms, gh, ln_g, ln_b, ws, bs_t, nct)
    (dqf, dkf, dvf, colf, rowf, dqb, dkb, dvb, colb, rowb) = mlstm_bwd(
        qa, ka, zmain, gcol, grow, (csf, csb, nsf, nsb, msf, msb), dhm, ncc)

    csum_f = rowf[:, :4, :].transpose(0, 2, 1).reshape(tp, 4)
    csum_b = rowb[:, :4, :].transpose(0, 2, 1).reshape(tp, 4)
    cs = _pad_cols(jnp.concatenate([csum_f, csum_f, csum_b, csum_b], axis=1), LANE)
    dzg, g_bgate = gates_bwd(colf, colb, cs, zgb)

    dz, dc, g_convw = qkconv_bwd_a(dz, zmain, dqf, dqb, dkf, dkb, dvf, dvb, conv_w, nct)
    dz = qkconv_bwd_b(dz, dc, conv_w, nct)

    g_w_main = mm_tn(hn1, dz, "mmt_main", BF16)
    g_w_g = mm_tn(hn1, dzg, "mmt_g", BF16)
    blk = lambda cb: g_w_main[:, cb * D:(cb + 1) * D]
    g_w_in = jnp.concatenate([blk(CB_Q), blk(CB_K), blk(CB_V), g_w_g[:, :NGATE], blk(CB_O), blk(CB_U), blk(CB_VG),
                              blk(CB_GM), blk(CB_GG)], axis=1)
    big = {"w_in": g_w_in, "w_branch_mlstm": g_w_br[0], "w_branch_sgu": g_w_br[1], "w_out": g_w_out,
           "w_up": g_w_up, "w_down": g_w_down}
    if grad_hook is None:
        da, received = mm_nn(dz, w_main, BF16, "mm_dmain", b_transposed=True), None
    else:
        da, received = mm_nn(dz, w_main, BF16, "mm_dmain", rider=grad_hook(big), b_transposed=True)
    db = mm_nn(dzg, w_g, BF16, "mm_dg", b_transposed=True)
    grad_x, acc_1 = norm1_bwd(ctx, x, da, db, dh1, g1, tab1, nct)

    d_modx = jnp.concatenate([acc_1[2], acc_1[3], acc_2[3], acc_2[0], acc_2[1], acc_h[2]])
    d_modc = jnp.concatenate([acc_1[0], acc_1[1], jnp.zeros((4 * D,), F32)])
    d_modb = jnp.concatenate([acc_1[5], acc_1[6], acc_2[3], acc_2[0], acc_2[1], acc_h[2]])

    small = {"b_mod": d_modb, "norm1_g": acc_1[4], "b_gate": g_bgate[0, :NGATE], "conv_qk": g_convw[:3].reshape(-1),
             "head_norm_g": g_gh[0], "sgu_ln_g": g_lng[0], "sgu_ln_b": g_lnb[0], "w_s": g_ws.reshape(-1),
             "b_s": g_bs[:, :SG].T.reshape(-1), "norm2_g": acc_2[2], "w_ffn_conv": g_wc[:9].reshape(-1),
             "final_g": acc_h[1]}
    return loss, grad_x, big, small, d_modx, d_modc, received


def mod_bwd_w(a_all, dm_all, name):
    n = dm_all.shape[1]
    tn = _pick(n, (512, 128))

    def body(a_ref, d_ref, o_ref):
        o_ref[...] = _dot_tn(_silu(a_ref[...]), d_ref[...])

    return pl.pallas_call(
        body, name=name, grid=(n // tn,),
        in_specs=[_bc(16, D), pl.BlockSpec((16, tn), lambda j: (0, j))],
        out_specs=pl.BlockSpec((D, tn), lambda j: (0, j)),
        out_shape=jax.ShapeDtypeStruct((D, n), F32),
        compiler_params=_cparams(("parallel",)),
    )(a_all, dm_all)


def mod_bwd_cctx(dmc, slabs, w_cols, c_ctx):
    per = w_cols // MODB
    nb = 2 * D // MODB

    def body(d_ref, w_ref, c_ref, o_ref):
        j = pl.program_id(0)

        @pl.when(j == 0)
        def _():
            o_ref[...] = jnp.zeros_like(o_ref)

        o_ref[...] += _dot_nt(d_ref[...], w_ref[0])

        @pl.when(j == nb - 1)
        def _():
            o_ref[...] = o_ref[...] * _dsilu(c_ref[...])

    return pl.pallas_call(
        body, name="mod_bwd_cctx", grid=(nb,),
        in_specs=[pl.BlockSpec((16, MODB), lambda j: (0, j)),
                  pl.BlockSpec((1, D, MODB), lambda j: (j // per, 0, j % per)), _bc(1, D)],
        out_specs=_bc(16, D),
        out_shape=jax.ShapeDtypeStruct((16, D), F32),
        compiler_params=_cparams(("arbitrary",)),
    )(dmc, slabs, c_ctx)


BIG = ("w_mod", "w_in", "w_branch_mlstm", "w_branch_sgu", "w_out", "w_up", "w_down")
BIG_AXIS = {"w_mod": 1, "w_in": 1, "w_branch_mlstm": 0, "w_branch_sgu": 0, "w_out": 0, "w_up": 1, "w_down": 0}
SMALL = ("c_ctx", "b_mod", "norm1_g", "b_gate", "conv_qk", "head_norm_g", "sgu_ln_g", "sgu_ln_b", "w_s", "b_s",
         "norm2_g", "w_ffn_conv", "final_g")
SMALL_SHARDED = {"conv_qk": (3, 2 * D), "w_ffn_conv": (9, DFF)}
PACK_ALIGN = 32 * D


def _pack(arrs, dtype, align=PACK_ALIGN, width=D):
    flat = jnp.concatenate([a.reshape(-1).astype(dtype) for a in arrs])
    n = flat.shape[0]
    padded = -(-n // align) * align
    return jnp.pad(flat, (0, padded - n)).reshape(padded // width, width)


def _unpack(slab, shapes):
    flat = slab.reshape(-1)
    outs, off = [], 0
    for shp in shapes:
        n = math.prod(shp)
        outs.append(flat[off:off + n].reshape(shp))
        off += n
    return outs


def _round_up(n, m):
    return -(-n // m) * m


def kernel(x, c, ctx, c_ctx, w_mod, b_mod, norm1_g, w_in, b_gate, conv_qk, head_norm_g, sgu_ln_g, sgu_ln_b, w_s, b_s, w_branch_mlstm, w_branch_sgu, w_out, norm2_g, w_up, w_ffn_conv, w_down, final_g, loss_target, m_c_ctx, m_w_mod, m_b_mod, m_norm1_g, m_w_in, m_b_gate, m_conv_qk, m_head_norm_g, m_sgu_ln_g, m_sgu_ln_b, m_w_s, m_b_s, m_w_branch_mlstm, m_w_branch_sgu, m_w_out, m_norm2_g, m_w_up, m_w_ffn_conv, m_w_down, m_final_g, v_c_ctx, v_w_mod, v_b_mod, v_norm1_g, v_w_in, v_b_gate, v_conv_qk, v_head_norm_g, v_sgu_ln_g, v_sgu_ln_b, v_w_s, v_b_s, v_w_branch_mlstm, v_w_branch_sgu, v_w_out, v_norm2_g, v_w_up, v_w_ffn_conv, v_w_down, v_final_g):
    params = dict(c_ctx=c_ctx, w_mod=w_mod, b_mod=b_mod, norm1_g=norm1_g, w_in=w_in, b_gate=b_gate, conv_qk=conv_qk,
                  head_norm_g=head_norm_g, sgu_ln_g=sgu_ln_g, sgu_ln_b=sgu_ln_b, w_s=w_s, b_s=b_s,
                  w_branch_mlstm=w_branch_mlstm, w_branch_sgu=w_branch_sgu, w_out=w_out, norm2_g=norm2_g, w_up=w_up,
                  w_ffn_conv=w_ffn_conv, w_down=w_down, final_g=final_g)
    mom_m = dict(c_ctx=m_c_ctx, w_mod=m_w_mod, b_mod=m_b_mod, norm1_g=m_norm1_g, w_in=m_w_in, b_gate=m_b_gate,
                 conv_qk=m_conv_qk, head_norm_g=m_head_norm_g, sgu_ln_g=m_sgu_ln_g, sgu_ln_b=m_sgu_ln_b, w_s=m_w_s,
                 b_s=m_b_s, w_branch_mlstm=m_w_branch_mlstm, w_branch_sgu=m_w_branch_sgu, w_out=m_w_out,
                 norm2_g=m_norm2_g, w_up=m_w_up, w_ffn_conv=m_w_ffn_conv, w_down=m_w_down, final_g=m_final_g)
    mom_v = dict(c_ctx=v_c_ctx, w_mod=v_w_mod, b_mod=v_b_mod, norm1_g=v_norm1_g, w_in=v_w_in, b_gate=v_b_gate,
                 conv_qk=v_conv_qk, head_norm_g=v_head_norm_g, sgu_ln_g=v_sgu_ln_g, sgu_ln_b=v_sgu_ln_b, w_s=v_w_s,
                 b_s=v_b_s, w_branch_mlstm=v_w_branch_mlstm, w_branch_sgu=v_w_branch_sgu, w_out=v_w_out,
                 norm2_g=v_norm2_g, w_up=v_w_up, w_ffn_conv=v_w_ffn_conv, w_down=v_w_down, final_g=v_final_g)
    chip = 2 * lax.axis_index("x") + lax.axis_index("y")

    shard2d = {n: params[n].reshape(params[n].shape[-2:]) for n in BIG}
    conv_sh = conv_qk.reshape(3, -1)
    fconv_sh = w_ffn_conv.reshape(9, -1)

    dev = 2 * chip + lax.axis_index("c")

    first_names, row_names = ("w_mod", "w_in"), ("w_branch_mlstm", "w_branch_sgu", "w_out", "w_down")
    first_w = [shard2d[n].shape[1] for n in first_names]
    row_h = [shard2d[n].shape[0] for n in row_names]
    first_slab = _pad_cols(jnp.concatenate([shard2d[n].astype(BF16) for n in first_names], axis=1),
                           _round_up(sum(first_w), LANE))
    up_slab = shard2d["w_up"].astype(BF16)
    row_slab = jnp.concatenate([shard2d[n].astype(BF16) for n in row_names], axis=0)

    def own_in(slab, gathered):
        return jnp.where((jnp.arange(4) == chip)[:, None, None], slab[None], gathered)

    first_all = own_in(first_slab, gather_chips([first_slab])[0])
    wt = {"w_mod": (first_all, first_w[0]),
          "w_in": jnp.concatenate([first_all[j, :, first_w[0]:first_w[0] + first_w[1]] for j in range(4)], axis=1)}

    late_slabs = [up_slab, row_slab]

    def late_copies(in_refs, out_refs, send_sems, recv_sems):
        xx, yy, cc = _coords()
        me = 2 * xx + yy
        cps = []
        for s, slab in enumerate(late_slabs):
            rh = slab.shape[0] // 2
            for j, (px, py) in enumerate(_other_chips(xx, yy)):
                for o in range(2):
                    k = 6 * s + 2 * j + o
                    cps.append(pltpu.make_async_remote_copy(
                        src_ref=in_refs[s].at[pl.ds(cc * rh, rh), :], dst_ref=out_refs[s].at[me, pl.ds(cc * rh, rh), :],
                        send_sem=send_sems.at[k], recv_sem=recv_sems.at[k],
                        device_id=(px, py, cc if o == 0 else 1 - cc), device_id_type=MESH))
        return cps

    def late_finish(landed):
        up_all, row_all = own_in(up_slab, landed[0]), own_in(row_slab, landed[1])
        got = {"w_up": jnp.concatenate([up_all[j] for j in range(4)], axis=1)}
        o = 0
        for n, ht in zip(row_names, row_h):
            got[n] = jnp.concatenate([row_all[j, o:o + ht, :] for j in range(4)], axis=0)
            o += ht
        return got

    late_rider = Rider(late_slabs, [jax.ShapeDtypeStruct((4,) + s_.shape, s_.dtype) for s_ in late_slabs],
                       6 * len(late_slabs), late_copies)

    cvec = _pack([conv_sh, fconv_sh], F32, align=8 * LANE, width=LANE)
    call = gather_all(cvec, "gather_conv")
    cparts = [_unpack(jnp.where(dev == 2 * j, cvec, call[2 * j]), [conv_sh.shape, fconv_sh.shape]) for j in range(4)]
    conv_full = jnp.concatenate([p[0] for p in cparts], axis=1)
    fconv_full = jnp.concatenate([p[1] for p in cparts], axis=1)

    sm = dict(b_mod=b_mod, norm1_g=norm1_g, b_gate=b_gate, conv_qk=conv_full, head_norm_g=head_norm_g,
              sgu_ln_g=sgu_ln_g, sgu_ln_b=sgu_ln_b, w_s=w_s, b_s=b_s, norm2_g=norm2_g, w_ffn_conv=fconv_full,
              final_g=final_g)

    gcol_names = ("w_up", "w_in")
    gcol_w = [shard2d[n].shape[1] for n in gcol_names]
    gcol_pad = _round_up(sum(gcol_w), LANE)
    cidx = lax.axis_index("c")
    kept = {}

    def grad_hook(gbig):
        def chip_cols(j):
            return _pad_cols(jnp.concatenate([gbig[n][:, j * wd:(j + 1) * wd] for n, wd in zip(gcol_names, gcol_w)],
                                             axis=1), gcol_pad)

        def chip_rows(j):
            return jnp.concatenate([gbig[n][j * ht:(j + 1) * ht] for n, ht in zip(row_names, row_h)], axis=0)

        gss = [jnp.stack([chip_cols(j) for j in range(4)]), jnp.stack([chip_rows(j) for j in range(4)])]
        from_sib = swap_halves(gss)
        pair_bf, own_terms = [], []
        for s, (gs, fs) in enumerate(zip(gss, from_sib)):
            rh = gs.shape[1] // 2
            my_half = lax.dynamic_slice_in_dim(gs, cidx * rh, rh, axis=1)
            pair_bf.append(add_n([my_half, fs], BF16, "pair_sum_%d" % s))
            own_terms.append([lax.dynamic_index_in_dim(my_half, chip, axis=0, keepdims=False),
                              lax.dynamic_index_in_dim(fs, chip, axis=0, keepdims=False)])
        kept["own_terms"] = own_terms

        def scatter_copies(in_refs, out_refs, send_sems, recv_sems):
            xx, yy, cc = _coords()
            cps = []
            for s in range(len(pair_bf)):
                for j, (px, py) in enumerate(_other_chips(xx, yy)):
                    cps.append(pltpu.make_async_remote_copy(
                        src_ref=in_refs[s].at[2 * px + py], dst_ref=out_refs[s].at[j],
                        send_sem=send_sems.at[3 * s + j], recv_sem=recv_sems.at[3 * s + j],
                        device_id=(px, py, cc), device_id_type=MESH))
            return cps

        return Rider(pair_bf, [jax.ShapeDtypeStruct((3,) + p.shape[1:], p.dtype) for p in pair_bf],
                     3 * len(pair_bf), scatter_copies)

    loss_l, grad_x, _, gsmall, d_modx, d_modc, recv = local_step(
        x[0], c, ctx[0], c_ctx, loss_target[0], wt, sm, late_weights=(late_rider, late_finish), grad_hook=grad_hook)

    reds = [add_n(kept["own_terms"][s] + [recv[s][0], recv[s][1], recv[s][2]], F32, "chip_sum_%d" % s)
            for s in range(2)]
    others = join_halves(reds)
    full_red = [jnp.where(cidx == 0, jnp.concatenate([m, o], axis=0), jnp.concatenate([o, m], axis=0))
                for m, o in zip(reds, others)]
    g_shard = {}
    off = 0
    for n, wd in zip(gcol_names, gcol_w):
        g_shard[n] = full_red[0][:, off:off + wd]
        off += wd
    off = 0
    for n, ht in zip(row_names, row_h):
        g_shard[n] = full_red[1][off:off + ht]
        off += ht

    small_order = ("b_mod", "norm1_g", "b_gate", "conv_qk", "head_norm_g", "sgu_ln_g", "sgu_ln_b", "w_s", "b_s", "norm2_g",
                   "w_ffn_conv", "final_g")
    vec_parts = [gsmall[n] for n in small_order] + [d_modx, d_modc, c.reshape(-1), loss_l.reshape(1)]
    vec_shapes = [a.shape for a in vec_parts]
    vec = _pack(vec_parts, F32, align=8 * LANE, width=LANE)
    allv = gather_all(vec, "gather_small")
    allv = jnp.where((jnp.arange(8) == dev)[:, None, None], vec[None], allv)
    summed = sum8(allv, "small_sum")
    s_parts = _unpack(summed, vec_shapes)
    g_small = dict(zip(small_order, s_parts[:len(small_order)]))
    dmc_sum = s_parts[len(small_order) + 1]
    loss = s_parts[-1][0]
    flat_all = allv.reshape(8, -1)
    starts = [0]
    for shp_ in vec_shapes:
        starts.append(starts[-1] + math.prod(shp_))
    i_dmx, i_c = len(small_order), len(small_order) + 2
    dmx_all = flat_all[:, starts[i_dmx]:starts[i_dmx + 1]]
    c_all = flat_all[:, starts[i_c]:starts[i_c + 1]]

    a_all = _pad_rows(jnp.concatenate([c_all, c_ctx.reshape(1, D)], axis=0), 16)
    dm_all = _pad_rows(jnp.concatenate([dmx_all, dmc_sum.reshape(1, NMOD * D)], axis=0), 16)
    ncol = NMOD * D // 4
    dm_shard = lax.dynamic_slice_in_dim(dm_all, chip * ncol, ncol, axis=1)
    g_shard["w_mod"] = mod_bwd_w(a_all, dm_shard, "mod_bwd_w")
    g_cctx = mod_bwd_cctx(_pad_rows(dmc_sum[:2 * D].reshape(1, 2 * D), 16), first_all, first_w[0],
                          c_ctx.reshape(1, D))[0]
    g_small["c_ctx"] = g_cctx

    results = {}
    for n in BIG:
        shp = params[n].shape
        g_ = g_shard[n].reshape(shp)
        d_, m_, v_ = adamw(params[n], g_, mom_m[n], mom_v[n], "adamw_" + n)
        results[n] = (g_, d_, m_, v_)

    conv_g = lax.dynamic_slice_in_dim(g_small["conv_qk"].reshape(3, 2 * D), chip * (2 * D // 4), 2 * D // 4, axis=1)
    fconv_g = lax.dynamic_slice_in_dim(g_small["w_ffn_conv"].reshape(9, DFF), chip * (DFF // 4), DFF // 4, axis=1)
    g_small["conv_qk"] = conv_g
    g_small["w_ffn_conv"] = fconv_g
    w_list = [params[n].reshape(-1) for n in SMALL]
    g_list = [g_small[n].reshape(-1) for n in SMALL]
    m_list = [mom_m[n].reshape(-1) for n in SMALL]
    v_list = [mom_v[n].reshape(-1) for n in SMALL]
    sm_shapes = [params[n].shape for n in SMALL]
    pk = lambda lst: _pack(lst, F32, align=8 * LANE, width=LANE)
    gp = pk(g_list)
    d_s, m_s, v_s = adamw(pk(w_list), gp, pk(m_list), pk(v_list), "adamw_small")
    for n, gg, dd, mm, vv in zip(SMALL, _unpack(gp, sm_shapes), _unpack(d_s, sm_shapes), _unpack(m_s, sm_shapes),
                                 _unpack(v_s, sm_shapes)):
        results[n] = (gg, dd, mm, vv)

    order = ("c_ctx", "w_mod", "b_mod", "norm1_g", "w_in", "b_gate", "conv_qk", "head_norm_g", "sgu_ln_g", "sgu_ln_b",
             "w_s", "b_s", "w_branch_mlstm", "w_branch_sgu", "w_out", "norm2_g", "w_up", "w_ffn_conv", "w_down",
             "final_g")
    outs = [loss, grad_x[None]]
    for k in range(4):
        outs += [results[n][k] for n in order]
    return tuple(outs)
```
